```python
import jax, jax.numpy as jnp
from jax import lax
import numpy as np

D_MODEL = 1024
BATCH = 8
SEQ = 8192
DEPTH = 1

D_MIX = 2 * D_MODEL
D_POOL = D_MIX // 2
D_SGU = D_MIX - D_POOL
POOL_WINDOWS = (2, 4, 8, 16)
N_POOL_GROUPS = len(POOL_WINDOWS)
POOL_GROUP = D_POOL // N_POOL_GROUPS
N_SGU_HEADS = 4
SGU_HEAD = D_SGU // N_SGU_HEADS
CHUNK = 128
D_PLE = 256
D_IN = D_POOL + 2 * D_SGU + D_MIX
DEEPNORM_ALPHA = (2 * DEPTH) ** 0.25
DEEPNORM_BETA = (8 * DEPTH) ** -0.25
LN_EPS = 1e-5

kernel_name = "hybrid_pool_sgu_deepnorm_block"


def layer_norm(x, g, b):
    xf = x.astype(jnp.float32)
    mu = jnp.mean(xf, axis=-1, keepdims=True)
    var = jnp.mean(jnp.square(xf - mu), axis=-1, keepdims=True)
    return ((xf - mu) * lax.rsqrt(var + LN_EPS)).astype(x.dtype) * g + b


def pool_mixer(a, pool_w, pool_scale):
    bsz, s, _ = a.shape
    a4 = a.reshape(bsz, s, N_POOL_GROUPS, POOL_GROUP)
    cs = jnp.cumsum(a4.astype(jnp.float32), axis=1)
    t = jnp.arange(s)
    outs = []
    for g, w in enumerate(POOL_WINDOWS):
        c = cs[:, :, g]
        lower = jnp.pad(c, ((0, 0), (w, 0), (0, 0)))[:, :s]
        cnt = jnp.minimum(t + 1, w).astype(jnp.float32)[None, :, None]
        outs.append((c - lower) / cnt)
    pooled = jnp.stack(outs, axis=2).astype(a.dtype) - a4
    mixed = jnp.einsum('bsgc,gcd->bsgd', pooled, pool_w)
    return mixed.reshape(bsz, s, D_POOL) * pool_scale


def spatial_gating(u, v, ln_g, ln_b, w_s, b_s):
    bsz, s, _ = u.shape
    u = jax.nn.gelu(u)
    v = jax.nn.gelu(v)
    vh = v.reshape(bsz, s, N_SGU_HEADS, SGU_HEAD)
    vh = layer_norm(vh, ln_g.reshape(N_SGU_HEADS, SGU_HEAD), ln_b.reshape(N_SGU_HEADS, SGU_HEAD))
    vc = vh.reshape(bsz, s // CHUNK, CHUNK, N_SGU_HEADS, SGU_HEAD)
    mask = jnp.tril(jnp.ones((CHUNK, CHUNK), dtype=bool))
    w = jnp.where(mask[None], w_s, jnp.zeros_like(w_s))
    sv = jnp.einsum('hij,bnjhc->bnihc', w, vc) + b_s.T[None, None, :, :, None]
    return u * sv.reshape(bsz, s, D_SGU)


def _fwd_setup_inputs(seed: int = 0) -> dict:
    key = jax.random.key(seed)
    ks = jax.random.split(key, 16)
    nrm = jax.random.normal
    f32 = jnp.float32
    x = nrm(ks[0], (BATCH, SEQ, D_MODEL), f32)
    p = nrm(ks[1], (DEPTH, BATCH, SEQ, D_PLE), f32)
    w_in = nrm(ks[2], (DEPTH, D_MODEL, D_IN), f32) * D_MODEL ** -0.5
    pool_w = nrm(ks[3], (DEPTH, N_POOL_GROUPS, POOL_GROUP, POOL_GROUP), f32) * POOL_GROUP ** -0.5
    pool_scale = 1.0 + 0.1 * nrm(ks[4], (DEPTH, D_POOL), f32)
    sgu_ln_g = 1.0 + 0.02 * nrm(ks[5], (DEPTH, D_SGU), f32)
    sgu_ln_b = 0.02 * nrm(ks[6], (DEPTH, D_SGU), f32)
    sgu_w = nrm(ks[7], (DEPTH, N_SGU_HEADS, CHUNK, CHUNK), f32) * (0.5 * CHUNK ** -0.5)
    sgu_b = 1.0 + 0.01 * nrm(ks[8], (DEPTH, N_SGU_HEADS, CHUNK), f32)
    w_out = nrm(ks[9], (DEPTH, D_MIX, D_MODEL), f32) * (D_MIX ** -0.5 * DEEPNORM_BETA)
    ln_g = 1.0 + 0.02 * nrm(ks[10], (DEPTH, D_MODEL), f32)
    ln_b = 0.02 * nrm(ks[11], (DEPTH, D_MODEL), f32)
    ple_w = nrm(ks[12], (DEPTH, D_PLE, D_MODEL), f32) * D_PLE ** -0.5
    ple_gate_w = nrm(ks[13], (DEPTH, D_MODEL, D_MODEL), f32) * D_MODEL ** -0.5
    ple_gate_b = 0.02 * nrm(ks[14], (DEPTH, D_MODEL), f32)
    return {"x": x, "p": p, "w_in": w_in, "pool_w": pool_w, "pool_scale": pool_scale,
            "sgu_ln_g": sgu_ln_g, "sgu_ln_b": sgu_ln_b, "sgu_w": sgu_w, "sgu_b": sgu_b,
            "w_out": w_out, "ln_g": ln_g, "ln_b": ln_b, "ple_w": ple_w,
            "ple_gate_w": ple_gate_w, "ple_gate_b": ple_gate_b}


def _fwd_reference(x, p, w_in, pool_w, pool_scale, sgu_ln_g, sgu_ln_b, sgu_w, sgu_b,
              w_out, ln_g, ln_b, ple_w, ple_gate_w, ple_gate_b):
    for i in range(DEPTH):
        h = jnp.einsum('bsd,de->bse', x, w_in[i])
        a, u, v, z = jnp.split(h, [D_POOL, D_POOL + D_SGU, D_POOL + 2 * D_SGU], axis=-1)
        y_pool = pool_mixer(a, pool_w[i], pool_scale[i])
        y_sgu = spatial_gating(u, v, sgu_ln_g[i], sgu_ln_b[i], sgu_w[i], sgu_b[i])
        y = jnp.concatenate([y_pool, y_sgu], axis=-1) * jax.nn.silu(z)
        mix = jnp.einsum('bse,ed->bsd', y, w_out[i])
        x = layer_norm(DEEPNORM_ALPHA * x + mix, ln_g[i], ln_b[i])
        gate = jax.nn.sigmoid(jnp.einsum('bsd,de->bse', x, ple_gate_w[i]) + ple_gate_b[i])
        x = x + gate * jnp.einsum('bsk,kd->bsd', p[i], ple_w[i])
    return x


import jax as _jax
import jax.numpy as _jnp

TWIN_FORMAT = 'train_step'
FWD_PARAMS = ['x', 'p', 'w_in', 'pool_w', 'pool_scale', 'sgu_ln_g', 'sgu_ln_b', 'sgu_w', 'sgu_b', 'w_out', 'ln_g', 'ln_b', 'ple_w', 'ple_gate_w', 'ple_gate_b']
TWIN_WEIGHTS = ['w_in', 'pool_w', 'pool_scale', 'sgu_ln_g', 'sgu_ln_b', 'sgu_w', 'sgu_b', 'w_out', 'ln_g', 'ln_b', 'ple_w', 'ple_gate_w', 'ple_gate_b']
TWIN_DIFF_INPUT = 'x'
TWIN_INPUTS = ['x', 'p', 'w_in', 'pool_w', 'pool_scale', 'sgu_ln_g', 'sgu_ln_b', 'sgu_w', 'sgu_b', 'w_out', 'ln_g', 'ln_b', 'ple_w', 'ple_gate_w', 'ple_gate_b', 'loss_target', 'm_w_in', 'm_pool_w', 'm_pool_scale', 'm_sgu_ln_g', 'm_sgu_ln_b', 'm_sgu_w', 'm_sgu_b', 'm_w_out', 'm_ln_g', 'm_ln_b', 'm_ple_w', 'm_ple_gate_w', 'm_ple_gate_b', 'v_w_in', 'v_pool_w', 'v_pool_scale', 'v_sgu_ln_g', 'v_sgu_ln_b', 'v_sgu_w', 'v_sgu_b', 'v_w_out', 'v_ln_g', 'v_ln_b', 'v_ple_w', 'v_ple_gate_w', 'v_ple_gate_b']
TWIN_OUTPUTS = ['loss', 'grad_x', 'grad_w_in', 'grad_pool_w', 'grad_pool_scale', 'grad_sgu_ln_g', 'grad_sgu_ln_b', 'grad_sgu_w', 'grad_sgu_b', 'grad_w_out', 'grad_ln_g', 'grad_ln_b', 'grad_ple_w', 'grad_ple_gate_w', 'grad_ple_gate_b', 'delta_w_in', 'delta_pool_w', 'delta_pool_scale', 'delta_sgu_ln_g', 'delta_sgu_ln_b', 'delta_sgu_w', 'delta_sgu_b', 'delta_w_out', 'delta_ln_g', 'delta_ln_b', 'delta_ple_w', 'delta_ple_gate_w', 'delta_ple_gate_b', 'new_m_w_in', 'new_m_pool_w', 'new_m_pool_scale', 'new_m_sgu_ln_g', 'new_m_sgu_ln_b', 'new_m_sgu_w', 'new_m_sgu_b', 'new_m_w_out', 'new_m_ln_g', 'new_m_ln_b', 'new_m_ple_w', 'new_m_ple_gate_w', 'new_m_ple_gate_b', 'new_v_w_in', 'new_v_pool_w', 'new_v_pool_scale', 'new_v_sgu_ln_g', 'new_v_sgu_ln_b', 'new_v_sgu_w', 'new_v_sgu_b', 'new_v_w_out', 'new_v_ln_g', 'new_v_ln_b', 'new_v_ple_w', 'new_v_ple_gate_w', 'new_v_ple_gate_b']
TWIN_LEAF_KINDS = {'loss': 'loss', 'grad_x': 'grad_x', 'grad_w_in': 'grad_w', 'grad_pool_w': 'grad_w', 'grad_pool_scale': 'grad_w', 'grad_sgu_ln_g': 'grad_w', 'grad_sgu_ln_b': 'grad_w', 'grad_sgu_w': 'grad_w', 'grad_sgu_b': 'grad_w', 'grad_w_out': 'grad_w', 'grad_ln_g': 'grad_w', 'grad_ln_b': 'grad_w', 'grad_ple_w': 'grad_w', 'grad_ple_gate_w': 'grad_w', 'grad_ple_gate_b': 'grad_w', 'delta_w_in': 'delta_w', 'delta_pool_w': 'delta_w', 'delta_pool_scale': 'delta_w', 'delta_sgu_ln_g': 'delta_w', 'delta_sgu_ln_b': 'delta_w', 'delta_sgu_w': 'delta_w', 'delta_sgu_b': 'delta_w', 'delta_w_out': 'delta_w', 'delta_ln_g': 'delta_w', 'delta_ln_b': 'delta_w', 'delta_ple_w': 'delta_w', 'delta_ple_gate_w': 'delta_w', 'delta_ple_gate_b': 'delta_w', 'new_m_w_in': 'new_m', 'new_m_pool_w': 'new_m', 'new_m_pool_scale': 'new_m', 'new_m_sgu_ln_g': 'new_m', 'new_m_sgu_ln_b': 'new_m', 'new_m_sgu_w': 'new_m', 'new_m_sgu_b': 'new_m', 'new_m_w_out': 'new_m', 'new_m_ln_g': 'new_m', 'new_m_ln_b': 'new_m', 'new_m_ple_w': 'new_m', 'new_m_ple_gate_w': 'new_m', 'new_m_ple_gate_b': 'new_m', 'new_v_w_in': 'new_v', 'new_v_pool_w': 'new_v', 'new_v_pool_scale': 'new_v', 'new_v_sgu_ln_g': 'new_v', 'new_v_sgu_ln_b': 'new_v', 'new_v_sgu_w': 'new_v', 'new_v_sgu_b': 'new_v', 'new_v_w_out': 'new_v', 'new_v_ln_g': 'new_v', 'new_v_ln_b': 'new_v', 'new_v_ple_w': 'new_v', 'new_v_ple_gate_w': 'new_v', 'new_v_ple_gate_b': 'new_v'}


def _forward(args):
    return _fwd_reference(*[args[k] for k in FWD_PARAMS])


def _output_shape():
    def fwd():
        inp = _fwd_setup_inputs(0)
        return _fwd_reference(*[inp[k] for k in FWD_PARAMS])
    out = _jax.eval_shape(fwd)
    return out.shape, out.dtype

N_MICROBATCH = 1
ADAM_LR = 0.001
ADAM_B1 = 0.9
ADAM_B2 = 0.999
ADAM_EPS = 1e-08
ADAM_WD = 0.01
ADAM_STEP = 10
PER_EXAMPLE_BATCH_AXIS = {'x': 0, 'p': 1, 'loss_target': 0}
SHARED_INPUTS = []
_WEIGHT_DTYPES = {'w_in': _jnp.float32, 'pool_w': _jnp.float32, 'pool_scale': _jnp.float32, 'sgu_ln_g': _jnp.float32, 'sgu_ln_b': _jnp.float32, 'sgu_w': _jnp.float32, 'sgu_b': _jnp.float32, 'w_out': _jnp.float32, 'ln_g': _jnp.float32, 'ln_b': _jnp.float32, 'ple_w': _jnp.float32, 'ple_gate_w': _jnp.float32, 'ple_gate_b': _jnp.float32}
MOMENT_SCALE = {'w_in': 4.630645e-02, 'pool_w': 5.567504e-02, 'pool_scale': 5.338074e-02, 'sgu_ln_g': 1.426857e-02, 'sgu_ln_b': 1.569647e-02, 'sgu_w': 4.037377e-02, 'sgu_b': 6.023686e-02, 'w_out': 1.776989e-01, 'ln_g': 6.602869e+01, 'ln_b': 6.522987e+00, 'ple_w': 9.184980e-01, 'ple_gate_w': 1.508354e-01, 'ple_gate_b': 6.620203e+00}


def _to_microbatches(a, axis):
    t = _jnp.moveaxis(a, axis, 0)
    t = t.reshape((N_MICROBATCH, t.shape[0] // N_MICROBATCH) + t.shape[1:])
    return _jnp.moveaxis(t, 1, axis + 1)


def setup_inputs(seed: int = 0) -> dict:
    inp = _fwd_setup_inputs(seed)
    key = _jax.random.fold_in(_jax.random.key(seed), 7919)
    shape, _ = _output_shape()
    out = dict(inp)
    out["loss_target"] = _jax.random.normal(_jax.random.fold_in(key, 0), shape, _jnp.float32)
    for i, name in enumerate(TWIN_WEIGHTS):
        w = inp[name].astype(_jnp.float32)
        if MOMENT_SCALE is None:
            s = _jnp.sqrt(_jnp.mean(_jnp.square(w)) + 1e-30)
        else:
            s = MOMENT_SCALE[name]
        km, kv = _jax.random.split(_jax.random.fold_in(key, i + 1))
        out[name] = w
        out["m_" + name] = s * _jax.random.normal(km, w.shape, _jnp.float32)
        out["v_" + name] = (s * s) * _jax.random.uniform(kv, w.shape, _jnp.float32, 0.5, 1.5)
    if N_MICROBATCH > 1:
        for name, axis in PER_EXAMPLE_BATCH_AXIS.items():
            out[name] = _to_microbatches(out[name], axis)
    return {'x': out['x'], 'p': out['p'], 'w_in': out['w_in'], 'pool_w': out['pool_w'], 'pool_scale': out['pool_scale'], 'sgu_ln_g': out['sgu_ln_g'], 'sgu_ln_b': out['sgu_ln_b'], 'sgu_w': out['sgu_w'], 'sgu_b': out['sgu_b'], 'w_out': out['w_out'], 'ln_g': out['ln_g'], 'ln_b': out['ln_b'], 'ple_w': out['ple_w'], 'ple_gate_w': out['ple_gate_w'], 'ple_gate_b': out['ple_gate_b'], 'loss_target': out['loss_target'], 'm_w_in': out['m_w_in'], 'm_pool_w': out['m_pool_w'], 'm_pool_scale': out['m_pool_scale'], 'm_sgu_ln_g': out['m_sgu_ln_g'], 'm_sgu_ln_b': out['m_sgu_ln_b'], 'm_sgu_w': out['m_sgu_w'], 'm_sgu_b': out['m_sgu_b'], 'm_w_out': out['m_w_out'], 'm_ln_g': out['m_ln_g'], 'm_ln_b': out['m_ln_b'], 'm_ple_w': out['m_ple_w'], 'm_ple_gate_w': out['m_ple_gate_w'], 'm_ple_gate_b': out['m_ple_gate_b'], 'v_w_in': out['v_w_in'], 'v_pool_w': out['v_pool_w'], 'v_pool_scale': out['v_pool_scale'], 'v_sgu_ln_g': out['v_sgu_ln_g'], 'v_sgu_ln_b': out['v_sgu_ln_b'], 'v_sgu_w': out['v_sgu_w'], 'v_sgu_b': out['v_sgu_b'], 'v_w_out': out['v_w_out'], 'v_ln_g': out['v_ln_g'], 'v_ln_b': out['v_ln_b'], 'v_ple_w': out['v_ple_w'], 'v_ple_gate_w': out['v_ple_gate_w'], 'v_ple_gate_b': out['v_ple_gate_b']}


def _loss(weights, diff, rest, loss_target):
    with _jax.named_scope("forward"):
        args = {**rest, TWIN_DIFF_INPUT: diff, **{k: w.astype(_WEIGHT_DTYPES[k]) for k, w in weights.items()}}
        y = _forward(args)
    with _jax.named_scope("loss_head"):
        err = _jnp.square(y.astype(_jnp.float32) - loss_target)
        return 0.5 * _jnp.sum(_jnp.mean(err, axis=-1)) if err.ndim else 0.5 * err


def _adamw(w, g, m, v):
    m = ADAM_B1 * m + (1.0 - ADAM_B1) * g
    v = ADAM_B2 * v + (1.0 - ADAM_B2) * _jnp.square(g)
    m_hat = m / (1.0 - ADAM_B1 ** ADAM_STEP)
    v_hat = v / (1.0 - ADAM_B2 ** ADAM_STEP)
    delta = -ADAM_LR * (m_hat / (_jnp.sqrt(v_hat) + ADAM_EPS) + ADAM_WD * w)
    return delta, m, v


def reference(x, p, w_in, pool_w, pool_scale, sgu_ln_g, sgu_ln_b, sgu_w, sgu_b, w_out, ln_g, ln_b, ple_w, ple_gate_w, ple_gate_b, loss_target, m_w_in, m_pool_w, m_pool_scale, m_sgu_ln_g, m_sgu_ln_b, m_sgu_w, m_sgu_b, m_w_out, m_ln_g, m_ln_b, m_ple_w, m_ple_gate_w, m_ple_gate_b, v_w_in, v_pool_w, v_pool_scale, v_sgu_ln_g, v_sgu_ln_b, v_sgu_w, v_sgu_b, v_w_out, v_ln_g, v_ln_b, v_ple_w, v_ple_gate_w, v_ple_gate_b):
    given = dict(x=x, p=p, w_in=w_in, pool_w=pool_w, pool_scale=pool_scale, sgu_ln_g=sgu_ln_g, sgu_ln_b=sgu_ln_b, sgu_w=sgu_w, sgu_b=sgu_b, w_out=w_out, ln_g=ln_g, ln_b=ln_b, ple_w=ple_w, ple_gate_w=ple_gate_w, ple_gate_b=ple_gate_b, loss_target=loss_target, m_w_in=m_w_in, m_pool_w=m_pool_w, m_pool_scale=m_pool_scale, m_sgu_ln_g=m_sgu_ln_g, m_sgu_ln_b=m_sgu_ln_b, m_sgu_w=m_sgu_w, m_sgu_b=m_sgu_b, m_w_out=m_w_out, m_ln_g=m_ln_g, m_ln_b=m_ln_b, m_ple_w=m_ple_w, m_ple_gate_w=m_ple_gate_w, m_ple_gate_b=m_ple_gate_b, v_w_in=v_w_in, v_pool_w=v_pool_w, v_pool_scale=v_pool_scale, v_sgu_ln_g=v_sgu_ln_g, v_sgu_ln_b=v_sgu_ln_b, v_sgu_w=v_sgu_w, v_sgu_b=v_sgu_b, v_w_out=v_w_out, v_ln_g=v_ln_g, v_ln_b=v_ln_b, v_ple_w=v_ple_w, v_ple_gate_w=v_ple_gate_w, v_ple_gate_b=v_ple_gate_b)
    weights = {n: given[n] for n in TWIN_WEIGHTS}
    shared = {n: given[n] for n in SHARED_INPUTS}
    per_example = {n: given[n] for n in ['x', 'p']}
    grad_fn = _jax.value_and_grad(_loss, argnums=(0, 1))

    def one_microbatch(ex, loss_target):
        ex = dict(ex)
        diff = ex.pop(TWIN_DIFF_INPUT)
        return grad_fn(weights, diff, {**shared, **ex}, loss_target)

    if N_MICROBATCH == 1:
        loss, (grad_w, grad_x) = one_microbatch(per_example, given["loss_target"])
    else:
        def body(carry, xs):
            loss_sum, grad_sum = carry
            l_k, (gw_k, gx_k) = one_microbatch(xs[0], xs[1])
            with _jax.named_scope("update"):
                return (loss_sum + l_k, _jax.tree.map(_jnp.add, grad_sum, gw_k)), gx_k

        init = (_jnp.zeros((), _jnp.float32), _jax.tree.map(_jnp.zeros_like, weights))
        (loss, grad_w), grad_x = _jax.lax.scan(body, init, (per_example, given["loss_target"]))
    with _jax.named_scope("update"):
        delta_w, new_m, new_v = {}, {}, {}
        for n in TWIN_WEIGHTS:
            delta_w[n], new_m[n], new_v[n] = _adamw(weights[n], grad_w[n], given["m_" + n], given["v_" + n])
    return (loss, grad_x, *[grad_w[n] for n in TWIN_WEIGHTS], *[delta_w[n] for n in TWIN_WEIGHTS],
            *[new_m[n] for n in TWIN_WEIGHTS], *[new_v[n] for n in TWIN_WEIGHTS])
```

```python
import functools
import math

import jax
import jax.numpy as jnp
from jax import lax
from jax.experimental import pallas as pl
from jax.experimental.pallas import tpu as pltpu

F32, BF16 = jnp.float32, jnp.bfloat16
MESH = pl.DeviceIdType.MESH
ANY = pl.BlockSpec(memory_space=pl.ANY)

POOL_WINDOWS = (2, 4, 8, 16)
HALO = 16
CHUNK = 128
LN_EPS = 1e-5
DEEPNORM_ALPHA = 2.0 ** 0.25
ADAM_LR, ADAM_B1, ADAM_B2, ADAM_EPS, ADAM_WD, ADAM_STEP = 1e-3, 0.9, 0.999, 1e-8, 0.01, 10
VMEM_LIMIT = 56 * 1024 * 1024
GELU_K = math.sqrt(2.0 / math.pi)
GELU_C = 0.044715


def _mm(a, b):
    return jnp.dot(a, b, preferred_element_type=F32)


def _mm_nt(a, b):
    return lax.dot_general(a, b, (((1,), (1,)), ((), ())), preferred_element_type=F32)


def _mm_tn(a, b):
    return lax.dot_general(a, b, (((0,), (0,)), ((), ())), preferred_element_type=F32)


def _gelu(x):
    t = jnp.tanh(GELU_K * (x + GELU_C * x * x * x))
    return 0.5 * x * (1.0 + t)


def _gelu_and_grad(x):
    x2 = x * x
    t = jnp.tanh(GELU_K * (x + GELU_C * x2 * x))
    g = 0.5 * x * (1.0 + t)
    dg = 0.5 * (1.0 + t) + 0.5 * x * (1.0 - t * t) * (GELU_K * (1.0 + 3.0 * GELU_C * x2))
    return g, dg


def _norm_rows(x):
    mu = jnp.mean(x, axis=-1, keepdims=True)
    xc = x - mu
    var = jnp.mean(xc * xc, axis=-1, keepdims=True)
    rstd = lax.rsqrt(var + LN_EPS)
    return xc * rstd, rstd


def _norm_rows_bwd(dxhat, xhat, rstd):
    m1 = jnp.mean(dxhat, axis=-1, keepdims=True)
    m2 = jnp.mean(dxhat * xhat, axis=-1, keepdims=True)
    return rstd * (dxhat - m1 - xhat * m2)


def _inv_count(row0, rows, w):
    t = row0 + lax.broadcasted_iota(jnp.int32, (rows, 1), 0)
    return 1.0 / jnp.minimum(t + 1, w).astype(F32)


def _causal_window_sum(ext, w):
    s, sh = ext, 1
    while sh < w:
        s = s + pltpu.roll(s, sh, axis=0)
        sh *= 2
    return s[HALO:, :]


def _anticausal_window_sum(ext, w):
    n, s, sh = ext.shape[0], ext, 1
    while sh < w:
        s = s + pltpu.roll(s, n - sh, axis=0)
        sh *= 2
    return s[: n - HALO, :]


def _place():
    return lax.axis_index("x"), lax.axis_index("y"), lax.axis_index("c")


def _gather_weights(shards):
    n = len(shards)
    half = [s.shape[0] // 2 for s in shards]

    def body(*refs):
        srcs, dsts = refs[:n], refs[n:2 * n]
        send_sems, recv_sems, local_sems = refs[2 * n:]
        x, y, c = _place()
        q = 2 * x + y
        sibling = (x, y, 1 - c)
        chips = [(1 - x, y), (x, 1 - y), (1 - x, 1 - y)]

        def rows(k, qq, cc):
            return dsts[k].at[qq, pl.ds(cc * half[k], half[k])]

        def copy(k, sem, qq, cc, to, src=None):
            return pltpu.make_async_remote_copy(
                src_ref=rows(k, qq, cc) if src is None else src, dst_ref=rows(k, qq, cc),
                send_sem=send_sems.at[6 * k + sem], recv_sem=recv_sems.at[6 * k + sem],
                device_id=to, device_id_type=MESH)

        mine = [pltpu.make_async_copy(srcs[k], dsts[k].at[q], local_sems.at[k]) for k in range(n)]
        for cp in mine:
            cp.start()
        started = []
        for k in range(n):
            own = srcs[k].at[pl.ds(c * half[k], half[k])]
            for j, chip in enumerate(chips):
                started.append(copy(k, j, q, c, (*chip, c), src=own))
                started[-1].start()
        for k in range(n):
            for j, chip in enumerate(chips):
                qq = 2 * chip[0] + chip[1]
                copy(k, j, qq, c, (x, y, c)).wait_recv()
                started.append(copy(k, 3 + j, qq, c, sibling))
                started[-1].start()
        for k in range(n):
            for j, chip in enumerate(chips):
                copy(k, 3 + j, 2 * chip[0] + chip[1], 1 - c, (x, y, c)).wait_recv()
        for cp in started:
            cp.wait_send()
        for cp in mine:
            cp.wait()

    return pl.pallas_call(
        body, name="gather_weights",
        out_shape=[jax.ShapeDtypeStruct((4,) + s.shape, s.dtype) for s in shards],
        in_specs=[ANY] * n, out_specs=[ANY] * n,
        scratch_shapes=[pltpu.SemaphoreType.DMA((6 * n,)), pltpu.SemaphoreType.DMA((6 * n,)),
                        pltpu.SemaphoreType.DMA((n,))],
    )(*shards)


def _swap_halves_with_sibling(grads, small):
    n = len(grads)

    def body(*refs):
        srcs, small_ref, dsts, small_dst = refs[:n], refs[n], refs[n + 1:2 * n + 1], refs[2 * n + 1]
        send_sems, recv_sems = refs[2 * n + 2:]
        x, y, c = _place()
        copies = []
        for k in range(n):
            copies.append(pltpu.make_async_remote_copy(
                src_ref=srcs[k].at[:, 1 - c], dst_ref=dsts[k], send_sem=send_sems.at[k], recv_sem=recv_sems.at[k],
                device_id=(x, y, 1 - c), device_id_type=MESH))
        copies.append(pltpu.make_async_remote_copy(
            src_ref=small_ref, dst_ref=small_dst, send_sem=send_sems.at[n], recv_sem=recv_sems.at[n],
            device_id=(x, y, 1 - c), device_id_type=MESH))
        for cp in copies:
            cp.start()
        for cp in copies:
            cp.wait()

    return pl.pallas_call(
        body, name="swap_halves",
        out_shape=[jax.ShapeDtypeStruct((4,) + g.shape[2:], g.dtype) for g in grads]
        + [jax.ShapeDtypeStruct(small.shape, small.dtype)],
        in_specs=[ANY] * (n + 1), out_specs=[ANY] * (n + 1),
        scratch_shapes=[pltpu.SemaphoreType.DMA((n + 1,)), pltpu.SemaphoreType.DMA((n + 1,))],
    )(*grads, small)


def _scatter_to_chips(parts, small):
    n = len(parts)

    def body(*refs):
        srcs, small_ref, dsts, small_dst = refs[:n], refs[n], refs[n + 1:2 * n + 1], refs[2 * n + 1]
        send_sems, recv_sems, local_sems = refs[2 * n + 2:]
        x, y, c = _place()
        q = 2 * x + y
        chips = [(1 - x, y), (x, 1 - y), (1 - x, 1 - y)]
        mine = [pltpu.make_async_copy(srcs[k].at[q], dsts[k].at[3], local_sems.at[k]) for k in range(n)]
        mine.append(pltpu.make_async_copy(small_ref, small_dst.at[3], local_sems.at[n]))
        for cp in mine:
            cp.start()
        copies = []
        for j, chip in enumerate(chips):
            for k in range(n):
                copies.append(pltpu.make_async_remote_copy(
                    src_ref=srcs[k].at[2 * chip[0] + chip[1]], dst_ref=dsts[k].at[j],
                    send_sem=send_sems.at[3 * k + j], recv_sem=recv_sems.at[3 * k + j],
                    device_id=(*chip, c), device_id_type=MESH))
            copies.append(pltpu.make_async_remote_copy(
                src_ref=small_ref, dst_ref=small_dst.at[j],
                send_sem=send_sems.at[3 * n + j], recv_sem=recv_sems.at[3 * n + j],
                device_id=(*chip, c), device_id_type=MESH))
        for cp in copies:
            cp.start()
        for cp in copies:
            cp.wait()
        for cp in mine:
            cp.wait()

    return pl.pallas_call(
        body, name="scatter_to_chips",
        out_shape=[jax.ShapeDtypeStruct(p.shape, p.dtype) for p in parts]
        + [jax.ShapeDtypeStruct((4,) + small.shape, small.dtype)],
        in_specs=[ANY] * (n + 1), out_specs=[ANY] * (n + 1),
        scratch_shapes=[pltpu.SemaphoreType.DMA((3 * n + 3,)), pltpu.SemaphoreType.DMA((3 * n + 3,)),
                        pltpu.SemaphoreType.DMA((n + 1,))],
    )(*parts, small)


def _join_halves_with_sibling(halves):
    n = len(halves)

    def body(*refs):
        srcs, dsts = refs[:n], refs[n:2 * n]
        send_sems, recv_sems, local_sems = refs[2 * n:]
        x, y, c = _place()
        mine = [pltpu.make_async_copy(srcs[k], dsts[k].at[c], local_sems.at[k]) for k in range(n)]
        copies = [pltpu.make_async_remote_copy(
            src_ref=srcs[k], dst_ref=dsts[k].at[c], send_sem=send_sems.at[k], recv_sem=recv_sems.at[k],
            device_id=(x, y, 1 - c), device_id_type=MESH) for k in range(n)]
        for cp in mine + copies:
            cp.start()
        for cp in copies:
            cp.wait()
        for cp in mine:
            cp.wait()

    return pl.pallas_call(
        body, name="join_halves",
        out_shape=[jax.ShapeDtypeStruct((2,) + h.shape, h.dtype) for h in halves],
        in_specs=[ANY] * n, out_specs=[ANY] * n,
        scratch_shapes=[pltpu.SemaphoreType.DMA((n,)), pltpu.SemaphoreType.DMA((n,)), pltpu.SemaphoreType.DMA((n,))],
    )(*halves)


def _row_block(rows, cols, n_arrays):
    cap = max(8, (VMEM_LIMIT // 4) // (8 * n_arrays * cols))
    rb = rows
    while rb > cap and rb % 2 == 0:
        rb //= 2
    return rb


def _add_own_half(grad, landed, c):
    _, _, hr, cols = grad.shape
    rb = _row_block(hr, cols, 3)

    def body(c_ref, g_ref, l_ref, o_ref):
        o_ref[...] = g_ref[...] + l_ref[...]

    return pl.pallas_call(
        body, name="add_own_half",
        out_shape=jax.ShapeDtypeStruct(landed.shape, F32),
        grid_spec=pltpu.PrefetchScalarGridSpec(
            num_scalar_prefetch=1, grid=(4, hr // rb),
            in_specs=[pl.BlockSpec((None, None, rb, cols), lambda qq, r, c_ref: (qq, c_ref[0], r, 0)),
                      pl.BlockSpec((None, rb, cols), lambda qq, r, c_ref: (qq, r, 0))],
            out_specs=pl.BlockSpec((None, rb, cols), lambda qq, r, c_ref: (qq, r, 0))),
    )(jnp.reshape(c, (1,)).astype(jnp.int32), grad, landed)


def _add_pairs(a, b, name):
    rows, cols = a.shape
    rb = _row_block(rows, cols, 3)

    def body(a_ref, b_ref, o_ref):
        o_ref[...] = a_ref[...] + b_ref[...]

    spec = pl.BlockSpec((rb, cols), lambda r: (r, 0))
    return pl.pallas_call(body, name=name, out_shape=jax.ShapeDtypeStruct(a.shape, F32), grid=(rows // rb,),
                          in_specs=[spec, spec], out_specs=spec)(a, b)


def _sum_four(slots, name):
    _, rows, cols = slots.shape
    rb = _row_block(rows, cols, 5)

    def body(s_ref, o_ref):
        o_ref[...] = (s_ref[3] + s_ref[0]) + (s_ref[1] + s_ref[2])

    return pl.pallas_call(
        body, name=name, out_shape=jax.ShapeDtypeStruct((rows, cols), F32), grid=(rows // rb,),
        in_specs=[pl.BlockSpec((4, rb, cols), lambda r: (0, r, 0))], out_specs=pl.BlockSpec((rb, cols), lambda r: (r, 0)),
    )(slots)


def _adamw(w, g, m, v, name):
    rows, cols = w.shape
    rb = _row_block(rows, cols, 7)
    c1 = 1.0 / (1.0 - ADAM_B1 ** ADAM_STEP)
    c2 = 1.0 / (1.0 - ADAM_B2 ** ADAM_STEP)

    def body(w_ref, g_ref, m_ref, v_ref, d_ref, nm_ref, nv_ref):
        gg = g_ref[...]
        nm = ADAM_B1 * m_ref[...] + (1.0 - ADAM_B1) * gg
        nv = ADAM_B2 * v_ref[...] + (1.0 - ADAM_B2) * (gg * gg)
        d_ref[...] = -ADAM_LR * ((nm * c1) / (jnp.sqrt(nv * c2) + ADAM_EPS) + ADAM_WD * w_ref[...])
        nm_ref[...] = nm
        nv_ref[...] = nv

    spec = pl.BlockSpec((rb, cols), lambda r: (r, 0))
    out = jax.ShapeDtypeStruct(w.shape, F32)
    return pl.pallas_call(body, name=name, out_shape=[out, out, out], grid=(rows // rb,),
                          in_specs=[spec] * 4, out_specs=[spec] * 3)(w, g, m, v)


def _front_forward(x, w_in, pool_w, pool_scale, sgu_g, sgu_b, sgu_wm, sgu_bias_t, tm):
    T, D = x.shape
    nq, _, cq = w_in.shape
    G, PG = pool_w.shape[0], pool_w.shape[1]
    nt = T // tm

    def body(x_ref, win_any, pw_any, ps_ref, lg_ref, lb_ref, sw_ref, sb_ref, h_ref, y_ref, win_v, pw_v, carry, sems):
        i = pl.program_id(0)

        @pl.when(i == 0)
        def _():
            c1 = pltpu.make_async_copy(win_any, win_v, sems.at[0])
            c2 = pltpu.make_async_copy(pw_any, pw_v, sems.at[1])
            c1.start()
            c2.start()
            carry[...] = jnp.zeros_like(carry)
            c1.wait()
            c2.wait()

        xb = x_ref[...].astype(BF16)
        for qq in range(nq):
            h_ref[:, qq * cq:(qq + 1) * cq] = _mm(xb, win_v[qq])

        a = h_ref[:, 0:D]
        ext = jnp.concatenate([carry[...], a], axis=0)
        carry[...] = a[tm - HALO:, :]
        for g, w in enumerate(POOL_WINDOWS):
            sl = slice(g * PG, (g + 1) * PG)
            pooled = _causal_window_sum(ext[:, sl], w) * _inv_count(i * tm, tm, w) - a[:, sl]
            mixed = _mm(pooled.astype(BF16), pw_v[g])
            z = h_ref[:, 3 * D + g * PG:3 * D + (g + 1) * PG]
            y_ref[:, sl] = (mixed * ps_ref[:, sl] * (z * jax.nn.sigmoid(z))).astype(BF16)

        for hd in range(D // PG):
            sl = slice(hd * PG, (hd + 1) * PG)
            vhat, _ = _norm_rows(_gelu(h_ref[:, 2 * D + hd * PG:2 * D + (hd + 1) * PG]))
            vn = (vhat * lg_ref[:, sl] + lb_ref[:, sl]).astype(BF16)
            for n in range(tm // CHUNK):
                rs = slice(n * CHUNK, (n + 1) * CHUNK)
                sv = _mm(sw_ref[hd], vn[rs, :]) + sb_ref[:, hd:hd + 1]
                u = h_ref[rs, D + hd * PG:D + (hd + 1) * PG]
                z = h_ref[rs, 4 * D + hd * PG:4 * D + (hd + 1) * PG]
                y_ref[rs, D + hd * PG:D + (hd + 1) * PG] = (_gelu(u) * sv * (z * jax.nn.sigmoid(z))).astype(BF16)

    vec = pl.BlockSpec((1, D), lambda i: (0, 0))
    return pl.pallas_call(
        body, name="front_forward",
        out_shape=[jax.ShapeDtypeStruct((T, 5 * D), F32), jax.ShapeDtypeStruct((T, 2 * D), BF16)],
        grid=(nt,),
        in_specs=[pl.BlockSpec((tm, D), lambda i: (i, 0)), ANY, ANY, vec, vec, vec,
                  pl.BlockSpec(sgu_wm.shape, lambda i: (0, 0, 0)), pl.BlockSpec(sgu_bias_t.shape, lambda i: (0, 0))],
        out_specs=[pl.BlockSpec((tm, 5 * D), lambda i: (i, 0)), pl.BlockSpec((tm, 2 * D), lambda i: (i, 0))],
        scratch_shapes=[pltpu.VMEM(w_in.shape, BF16), pltpu.VMEM(pool_w.shape, BF16), pltpu.VMEM((HALO, D), F32),
                        pltpu.SemaphoreType.DMA((2,))],
        compiler_params=pltpu.CompilerParams(dimension_semantics=("arbitrary",), vmem_limit_bytes=VMEM_LIMIT),
    )(x, w_in, pool_w, pool_scale, sgu_g, sgu_b, sgu_wm, sgu_bias_t)


def _tail(y, x, p, target, w_out, w_gate, w_ple, ln_g, ln_b, gate_b, tm):
    T, D = x.shape
    K = p.shape[1]
    nq, _, cq = w_ple.shape
    nt = T // tm

    def body(y_ref, x_ref, p_ref, t_ref, wout_any, wg_any, wp_any, lng_ref, lnb_ref, bg_ref,
             dxp_ref, dy_ref, dwout_any, dwg_any, dwp_any, dlng_ref, dlnb_ref, dbg_ref, ssq_ref,
             wout_v, wg_v, wp_v, dwout_acc, dwg_acc, dwp_acc, sems):
        i = pl.program_id(0)

        @pl.when(i == 0)
        def _():
            loads = [pltpu.make_async_copy(s, d, sems.at[k])
                     for k, (s, d) in enumerate(((wout_any, wout_v), (wg_any, wg_v), (wp_any, wp_v)))]
            for cp in loads:
                cp.start()
            for ref in (dwout_acc, dwg_acc, dwp_acc, dlng_ref, dlnb_ref, dbg_ref, ssq_ref):
                ref[...] = jnp.zeros_like(ref)
            for cp in loads:
                cp.wait()

        yb = y_ref[...]
        pb = p_ref[...].astype(BF16)
        xhat, rstd = _norm_rows(DEEPNORM_ALPHA * x_ref[...] + _mm(yb, wout_v[...]))
        x1 = xhat * lng_ref[...] + lnb_ref[...]
        x1b = x1.astype(BF16)
        gate = jax.nn.sigmoid(_mm(x1b, wg_v[...]) + bg_ref[...])
        e = jnp.concatenate([_mm(pb, wp_v[qq]) for qq in range(nq)], axis=1)
        diff = x1 + gate * e - t_ref[...]
        ssq_ref[...] += jnp.sum(diff * diff, axis=0, keepdims=True)

        dout = diff * (1.0 / D)
        d_e = (dout * gate).astype(BF16)
        dgl = dout * e * gate * (1.0 - gate)
        dglb = dgl.astype(BF16)
        for qq in range(nq):
            dwp_acc[qq] += _mm_tn(pb, d_e[:, qq * cq:(qq + 1) * cq])
        dwg_acc[...] += _mm_tn(x1b, dglb)
        dbg_ref[...] += jnp.sum(dgl, axis=0, keepdims=True)
        d_x1 = dout + _mm_nt(dglb, wg_v[...])
        dlng_ref[...] += jnp.sum(d_x1 * xhat, axis=0, keepdims=True)
        dlnb_ref[...] += jnp.sum(d_x1, axis=0, keepdims=True)
        d_r = _norm_rows_bwd(d_x1 * lng_ref[...], xhat, rstd)
        drb = d_r.astype(BF16)
        dxp_ref[...] = DEEPNORM_ALPHA * d_r
        dwout_acc[...] += _mm_tn(yb, drb)
        dy_ref[...] = _mm_nt(drb, wout_v[...])

        @pl.when(i == nt - 1)
        def _():
            stores = [pltpu.make_async_copy(s, d, sems.at[k])
                      for k, (s, d) in enumerate(((dwout_acc, dwout_any), (dwg_acc, dwg_any), (dwp_acc, dwp_any)))]
            for cp in stores:
                cp.start()
            for cp in stores:
                cp.wait()

    vec = pl.BlockSpec((1, D), lambda i: (0, 0))
    vec_shape = jax.ShapeDtypeStruct((1, D), F32)

    def tile(cols):
        return pl.BlockSpec((tm, cols), lambda i: (i, 0))

    return pl.pallas_call(
        body, name="tail",
        out_shape=[jax.ShapeDtypeStruct((T, D), F32), jax.ShapeDtypeStruct((T, 2 * D), F32),
                   jax.ShapeDtypeStruct(w_out.shape, F32), jax.ShapeDtypeStruct(w_gate.shape, F32),
                   jax.ShapeDtypeStruct(w_ple.shape, F32), vec_shape, vec_shape, vec_shape, vec_shape],
        grid=(nt,),
        in_specs=[tile(2 * D), tile(D), tile(K), tile(D), ANY, ANY, ANY, vec, vec, vec],
        out_specs=[tile(D), tile(2 * D), ANY, ANY, ANY, vec, vec, vec, vec],
        scratch_shapes=[pltpu.VMEM(w_out.shape, BF16), pltpu.VMEM(w_gate.shape, BF16), pltpu.VMEM(w_ple.shape, BF16),
                        pltpu.VMEM(w_out.shape, F32), pltpu.VMEM(w_gate.shape, F32), pltpu.VMEM(w_ple.shape, F32),
                        pltpu.SemaphoreType.DMA((3,))],
        compiler_params=pltpu.CompilerParams(dimension_semantics=("arbitrary",), vmem_limit_bytes=VMEM_LIMIT),
    )(y, x, p, target, w_out, w_gate, w_ple, ln_g, ln_b, gate_b)


def _front_backward(h, d_y, pool_w, pool_scale, sgu_g, sgu_b, sgu_wm, sgu_bias_t, tm):
    T = h.shape[0]
    D = h.shape[1] // 5
    G, PG = pool_w.shape[0], pool_w.shape[1]
    nt = T // tm
    hpt = tm // HALO

    def body(h_ref, halo_ref, dy_ref, pw_ref, ps_ref, lg_ref, lb_ref, sw_ref, sb_ref,
             dh_ref, dpw_ref, dps_ref, dlg_ref, dlb_ref, dsw_ref, dsb_ref, carry):
        i = pl.program_id(0)
        ti = nt - 1 - i

        @pl.when(i == 0)
        def _():
            carry[...] = jnp.zeros_like(carry)
            for ref in (dpw_ref, dps_ref, dlg_ref, dlb_ref, dsw_ref, dsb_ref):
                ref[...] = jnp.zeros_like(ref)

        a = h_ref[:, 0:D]
        before = jnp.where(ti > 0, halo_ref[...], 0.0)
        ext = jnp.concatenate([before, a], axis=0)
        for g, w in enumerate(POOL_WINDOWS):
            sl = slice(g * PG, (g + 1) * PG)
            inv = _inv_count(ti * tm, tm, w)
            pooled = (_causal_window_sum(ext[:, sl], w) * inv - a[:, sl]).astype(BF16)
            mixed = _mm(pooled, pw_ref[g])
            z = h_ref[:, 3 * D + g * PG:3 * D + (g + 1) * PG]
            sig = jax.nn.sigmoid(z)
            dy = dy_ref[:, sl]
            d_ypool = dy * (z * sig)
            dh_ref[:, 3 * D + g * PG:3 * D + (g + 1) * PG] = (
                dy * (mixed * ps_ref[:, sl]) * (sig * (1.0 + z * (1.0 - sig)))).astype(BF16)
            dps_ref[:, sl] += jnp.sum(d_ypool * mixed, axis=0, keepdims=True)
            d_mixed = (d_ypool * ps_ref[:, sl]).astype(BF16)
            dpw_ref[g] += _mm_tn(pooled, d_mixed)
            d_pooled = _mm_nt(d_mixed, pw_ref[g])
            scaled = d_pooled * inv
            after = jnp.concatenate([scaled, carry[:, sl]], axis=0)
            carry[:, sl] = scaled[:HALO, :]
            dh_ref[:, sl] = (_anticausal_window_sum(after, w) - d_pooled).astype(BF16)

        for hd in range(D // PG):
            sl = slice(hd * PG, (hd + 1) * PG)
            vg, dvg = _gelu_and_grad(h_ref[:, 2 * D + hd * PG:2 * D + (hd + 1) * PG])
            vhat, rstd = _norm_rows(vg)
            vn = (vhat * lg_ref[:, sl] + lb_ref[:, sl]).astype(BF16)
            d_vn_chunks = []
            for n in range(tm // CHUNK):
                rs = slice(n * CHUNK, (n + 1) * CHUNK)
                sv = _mm(sw_ref[hd], vn[rs, :]) + sb_ref[:, hd:hd + 1]
                ug, dug = _gelu_and_grad(h_ref[rs, D + hd * PG:D + (hd + 1) * PG])
                z = h_ref[rs, 4 * D + hd * PG:4 * D + (hd + 1) * PG]
                sig = jax.nn.sigmoid(z)
                dy = dy_ref[rs, D + hd * PG:D + (hd + 1) * PG]
                d_ysgu = dy * (z * sig)
                dh_ref[rs, 4 * D + hd * PG:4 * D + (hd + 1) * PG] = (
                    dy * (ug * sv) * (sig * (1.0 + z * (1.0 - sig)))).astype(BF16)
                dh_ref[rs, D + hd * PG:D + (hd + 1) * PG] = (d_ysgu * sv * dug).astype(BF16)
                d_sv = d_ysgu * ug
                dsb_ref[:, hd:hd + 1] += jnp.sum(d_sv, axis=1, keepdims=True)
                d_svb = d_sv.astype(BF16)
                dsw_ref[hd] += _mm_nt(d_svb, vn[rs, :])
                d_vn_chunks.append(_mm_tn(sw_ref[hd], d_svb))
            d_vn = jnp.concatenate(d_vn_chunks, axis=0)
            dlg_ref[:, sl] += jnp.sum(d_vn * vhat, axis=0, keepdims=True)
            dlb_ref[:, sl] += jnp.sum(d_vn, axis=0, keepdims=True)
            d_vg = _norm_rows_bwd(d_vn * lg_ref[:, sl], vhat, rstd)
            dh_ref[:, 2 * D + hd * PG:2 * D + (hd + 1) * PG] = (d_vg * dvg).astype(BF16)

    vec = pl.BlockSpec((1, D), lambda i: (0, 0))
    vec_shape = jax.ShapeDtypeStruct((1, D), F32)

    def whole(shape):
        return pl.BlockSpec(shape, lambda i: (0,) * len(shape))

    return pl.pallas_call(
        body, name="front_backward",
        out_shape=[jax.ShapeDtypeStruct((T, 5 * D), BF16), jax.ShapeDtypeStruct(pool_w.shape, F32), vec_shape, vec_shape,
                   vec_shape, jax.ShapeDtypeStruct(sgu_wm.shape, F32), jax.ShapeDtypeStruct(sgu_bias_t.shape, F32)],
        grid=(nt,),
        in_specs=[pl.BlockSpec((tm, 5 * D), lambda i: (nt - 1 - i, 0)),
                  pl.BlockSpec((HALO, D), lambda i: (jnp.maximum((nt - 1 - i) * hpt - 1, 0), 0)),
                  pl.BlockSpec((tm, 2 * D), lambda i: (nt - 1 - i, 0)),
                  whole(pool_w.shape), vec, vec, vec, whole(sgu_wm.shape), whole(sgu_bias_t.shape)],
        out_specs=[pl.BlockSpec((tm, 5 * D), lambda i: (nt - 1 - i, 0)), whole(pool_w.shape), vec, vec, vec,
                   whole(sgu_wm.shape), whole(sgu_bias_t.shape)],
        scratch_shapes=[pltpu.VMEM((HALO, D), F32)],
        compiler_params=pltpu.CompilerParams(dimension_semantics=("arbitrary",), vmem_limit_bytes=VMEM_LIMIT),
    )(h, h, d_y, pool_w, pool_scale, sgu_g, sgu_b, sgu_wm, sgu_bias_t)


def _input_backward(d_h, x, dx_part, w_in, tm):
    T, D = x.shape
    nq, _, cq = w_in.shape
    nt = T // tm

    def body(dh_ref, x_ref, dxp_ref, win_any, dx_ref, dwin_any, win_v, dwin_acc, sems):
        i = pl.program_id(0)

        @pl.when(i == 0)
        def _():
            cp = pltpu.make_async_copy(win_any, win_v, sems.at[0])
            cp.start()
            dwin_acc[...] = jnp.zeros_like(dwin_acc)
            cp.wait()

        xb = x_ref[...].astype(BF16)
        dx = dxp_ref[...]
        for qq in range(nq):
            dhq = dh_ref[:, qq * cq:(qq + 1) * cq]
            dx = dx + _mm_nt(dhq, win_v[qq])
            dwin_acc[qq] += _mm_tn(xb, dhq)
        dx_ref[...] = dx

        @pl.when(i == nt - 1)
        def _():
            cp = pltpu.make_async_copy(dwin_acc, dwin_any, sems.at[0])
            cp.start()
            cp.wait()

    return pl.pallas_call(
        body, name="input_backward",
        out_shape=[jax.ShapeDtypeStruct((T, D), F32), jax.ShapeDtypeStruct(w_in.shape, F32)],
        grid=(nt,),
        in_specs=[pl.BlockSpec((tm, 5 * D), lambda i: (i, 0)), pl.BlockSpec((tm, D), lambda i: (i, 0)),
                  pl.BlockSpec((tm, D), lambda i: (i, 0)), ANY],
        out_specs=[pl.BlockSpec((tm, D), lambda i: (i, 0)), ANY],
        scratch_shapes=[pltpu.VMEM(w_in.shape, BF16), pltpu.VMEM(w_in.shape, F32), pltpu.SemaphoreType.DMA((1,))],
        compiler_params=pltpu.CompilerParams(dimension_semantics=("arbitrary",), vmem_limit_bytes=VMEM_LIMIT),
    )(d_h, x, dx_part, w_in)


def _token_tile(T):
    tm = 256
    while T % tm:
        tm //= 2
    return tm


def kernel(x, p, w_in, pool_w, pool_scale, sgu_ln_g, sgu_ln_b, sgu_w, sgu_b, w_out, ln_g, ln_b, ple_w, ple_gate_w, ple_gate_b, loss_target, m_w_in, m_pool_w, m_pool_scale, m_sgu_ln_g, m_sgu_ln_b, m_sgu_w, m_sgu_b, m_w_out, m_ln_g, m_ln_b, m_ple_w, m_ple_gate_w, m_ple_gate_b, v_w_in, v_pool_w, v_pool_scale, v_sgu_ln_g, v_sgu_ln_b, v_sgu_w, v_sgu_b, v_w_out, v_ln_g, v_ln_b, v_ple_w, v_ple_gate_w, v_ple_gate_b):
    c = lax.axis_index("c")
    T, D = x.shape[1], x.shape[2]
    tm = _token_tile(T)
    x2, p2, tgt = x[0], p[0, 0], loss_target[0]
    G, PGQ, PG = pool_w.shape[1], pool_w.shape[2], pool_w.shape[3]

    shards = [w_in[0], w_out[0], ple_gate_w[0], ple_w[0], pool_w[0].reshape(G * PGQ, PG)]
    w_in_f, w_out_f, w_gate_f, w_ple_f, pool_f = _gather_weights([s.astype(BF16) for s in shards])
    w_out_f = w_out_f.reshape(-1, D)
    w_gate_f = w_gate_f.reshape(-1, D)
    pool_f = pool_f.reshape(4, G, PGQ, PG).transpose(1, 0, 2, 3).reshape(G, 4 * PGQ, PG)
    tril = jnp.tril(jnp.ones((CHUNK, CHUNK), dtype=bool))
    sgu_wm = jnp.where(tril[None], sgu_w[0], 0.0).astype(BF16)
    sgu_bias_t = sgu_b[0].T

    h, y = _front_forward(x2, w_in_f, pool_f, pool_scale, sgu_ln_g, sgu_ln_b, sgu_wm, sgu_bias_t, tm)
    (dx_part, d_y, d_w_out, d_w_gate, d_w_ple, d_ln_g, d_ln_b, d_gate_b, ssq) = _tail(
        y, x2, p2, tgt, w_out_f, w_gate_f, w_ple_f, ln_g, ln_b, ple_gate_b, tm)
    d_h, d_pool_w, d_pool_scale, d_sgu_g, d_sgu_b, d_sgu_w, d_sgu_bias_t = _front_backward(
        h, d_y, pool_f, pool_scale, sgu_ln_g, sgu_ln_b, sgu_wm, sgu_bias_t, tm)
    d_x, d_w_in = _input_backward(d_h, x2, dx_part, w_in_f, tm)
    loss = lax.psum((0.5 / D) * jnp.sum(ssq), ("x", "y", "c"))

    grads = [d_w_in, d_w_out.reshape(4, -1, D), d_w_gate.reshape(4, -1, D), d_w_ple,
             d_pool_w.reshape(G, 4, PGQ, PG).transpose(1, 0, 2, 3).reshape(4, G * PGQ, PG)]
    grads = [g.reshape(4, 2, g.shape[1] // 2, g.shape[2]) for g in grads]
    d_sgu_w = jnp.where(tril[None], d_sgu_w, 0.0)
    small_names = ["pool_scale", "sgu_ln_g", "sgu_ln_b", "ln_g", "ln_b", "ple_gate_b", "sgu_b", "sgu_w"]
    small_grads = [d_pool_scale, d_sgu_g, d_sgu_b, d_ln_g, d_ln_b, d_gate_b, d_sgu_bias_t.T, d_sgu_w]

    def pack(arrays):
        rows = [a.reshape(-1) for a in arrays[:6]] + [jnp.pad(arrays[6].reshape(-1), (0, 2 * D - arrays[6].size))]
        return jnp.concatenate([r.reshape(-1, D) for r in rows] + [arrays[7].reshape(-1, D)], axis=0)

    def unpack(packed, like):
        out = [packed[k].reshape(like[k].shape) for k in range(6)]
        out.append(packed[6, :like[6].size].reshape(like[6].shape))
        out.append(packed[8:].reshape(like[7].shape))
        return out

    small = pack(small_grads)
    *landed, small_landed = _swap_halves_with_sibling(grads, small)
    parts = [_add_own_half(g, l, c) for g, l in zip(grads, landed)]
    small_chip = _add_pairs(small, small_landed, "add_small")
    *slots, small_slots = _scatter_to_chips(parts, small_chip)
    halves = [_sum_four(s, "sum_four_%d" % k) for k, s in enumerate(slots)]
    small_total = _sum_four(small_slots, "sum_four_small")
    quarters = [q.reshape(-1, q.shape[2]) for q in _join_halves_with_sibling(halves)]

    big_names = ["w_in", "w_out", "ple_gate_w", "ple_w", "pool_w"]
    given = dict(w_in=(w_in, m_w_in, v_w_in), w_out=(w_out, m_w_out, v_w_out),
                 ple_gate_w=(ple_gate_w, m_ple_gate_w, v_ple_gate_w), ple_w=(ple_w, m_ple_w, v_ple_w),
                 pool_w=(pool_w, m_pool_w, v_pool_w), pool_scale=(pool_scale, m_pool_scale, v_pool_scale),
                 sgu_ln_g=(sgu_ln_g, m_sgu_ln_g, v_sgu_ln_g), sgu_ln_b=(sgu_ln_b, m_sgu_ln_b, v_sgu_ln_b),
                 sgu_w=(sgu_w, m_sgu_w, v_sgu_w), sgu_b=(sgu_b, m_sgu_b, v_sgu_b), ln_g=(ln_g, m_ln_g, v_ln_g),
                 ln_b=(ln_b, m_ln_b, v_ln_b), ple_gate_b=(ple_gate_b, m_ple_gate_b, v_ple_gate_b))
    grad, delta, new_m, new_v = {}, {}, {}, {}
    for name, g in zip(big_names, quarters):
        w, m, v = given[name]
        d, nm, nv = _adamw(w.reshape(g.shape), g, m.reshape(g.shape), v.reshape(g.shape), "adamw_" + name)
        grad[name], delta[name], new_m[name], new_v[name] = (t.reshape(w.shape) for t in (g, d, nm, nv))
    small_w, small_m, small_v = (pack([given[n][k] for n in small_names]) for k in range(3))
    small_out = _adamw(small_w, small_total, small_m, small_v, "adamw_small")
    like = [given[n][0] for n in small_names]
    for k, name in enumerate(small_names):
        grad[name], delta[name], new_m[name], new_v[name] = (unpack(t, like)[k] for t in (small_total, *small_out))

    order = ["w_in", "pool_w", "pool_scale", "sgu_ln_g", "sgu_ln_b", "sgu_w", "sgu_b", "w_out", "ln_g", "ln_b",
             "ple_w", "ple_gate_w", "ple_gate_b"]
    return (loss, d_x[None], *[grad[n] for n in order], *[delta[n] for n in order],
            *[new_m[n] for n in order], *[new_v[n] for n in order])
```

```python
import functools
import math

import jax
import jax.numpy as jnp
from jax import lax
from jax.experimental import pallas as pl
from jax.experimental.pallas import tpu as pltpu

F32, BF16 = jnp.float32, jnp.bfloat16
MESH = pl.DeviceIdType.MESH
ANY = pl.BlockSpec(memory_space=pl.ANY)

POOL_WINDOWS = (2, 4, 8, 16)
HALO = 16
CHUNK = 128
LN_EPS = 1e-5
DEEPNORM_ALPHA = 2.0 ** 0.25
ADAM_LR, ADAM_B1, ADAM_B2, ADAM_EPS, ADAM_WD, ADAM_STEP = 1e-3, 0.9, 0.999, 1e-8, 0.01, 10
VMEM_LIMIT = 56 * 1024 * 1024
GELU_K = math.sqrt(2.0 / math.pi)
GELU_C = 0.044715


def _mm(a, b):
    return jnp.dot(a, b, preferred_element_type=F32)


def _mm_nt(a, b):
    return lax.dot_general(a, b, (((1,), (1,)), ((), ())), preferred_element_type=F32)


def _mm_tn(a, b):
    return lax.dot_general(a, b, (((0,), (0,)), ((), ())), preferred_element_type=F32)


def _gelu(x):
    t = jnp.tanh(GELU_K * (x + GELU_C * x * x * x))
    return 0.5 * x * (1.0 + t)


def _gelu_and_grad(x):
    x2 = x * x
    t = jnp.tanh(GELU_K * (x + GELU_C * x2 * x))
    g = 0.5 * x * (1.0 + t)
    dg = 0.5 * (1.0 + t) + 0.5 * x * (1.0 - t * t) * (GELU_K * (1.0 + 3.0 * GELU_C * x2))
    return g, dg


def _norm_rows(x):
    mu = jnp.mean(x, axis=-1, keepdims=True)
    xc = x - mu
    var = jnp.mean(xc * xc, axis=-1, keepdims=True)
    rstd = lax.rsqrt(var + LN_EPS)
    return xc * rstd, rstd


def _norm_rows_bwd(dxhat, xhat, rstd):
    m1 = jnp.mean(dxhat, axis=-1, keepdims=True)
    m2 = jnp.mean(dxhat * xhat, axis=-1, keepdims=True)
    return rstd * (dxhat - m1 - xhat * m2)


def _inv_count(row0, rows, w):
    t = row0 + lax.broadcasted_iota(jnp.int32, (rows, 1), 0)
    return 1.0 / jnp.minimum(t + 1, w).astype(F32)


def _causal_window_sum(ext, w):
    s, sh = ext, 1
    while sh < w:
        s = s + pltpu.roll(s, sh, axis=0)
        sh *= 2
    return s[HALO:, :]


def _anticausal_window_sum(ext, w):
    n, s, sh = ext.shape[0], ext, 1
    while sh < w:
        s = s + pltpu.roll(s, n - sh, axis=0)
        sh *= 2
    return s[: n - HALO, :]


def _place():
    return lax.axis_index("x"), lax.axis_index("y"), lax.axis_index("c")


def _gather_weights(shards):
    n = len(shards)
    half = [s.shape[0] // 2 for s in shards]

    def body(*refs):
        wide, dsts, srcs = refs[:n], refs[n:2 * n], refs[2 * n:3 * n]
        send_sems, recv_sems, local_sems = refs[3 * n:]
        for k in range(n):
            for r0 in range(0, 2 * half[k], CHUNK):
                srcs[k][r0:r0 + CHUNK, :] = wide[k][r0:r0 + CHUNK, :].astype(BF16)
        x, y, c = _place()
        q = 2 * x + y
        sibling = (x, y, 1 - c)
        chips = [(1 - x, y), (x, 1 - y), (1 - x, 1 - y)]

        def rows(k, qq, cc):
            return dsts[k].at[qq, pl.ds(cc * half[k], half[k])]

        def copy(k, sem, qq, cc, to, src=None):
            return pltpu.make_async_remote_copy(
                src_ref=rows(k, qq, cc) if src is None else src, dst_ref=rows(k, qq, cc),
                send_sem=send_sems.at[6 * k + sem], recv_sem=recv_sems.at[6 * k + sem],
                device_id=to, device_id_type=MESH)

        mine = [pltpu.make_async_copy(srcs[k], dsts[k].at[q], local_sems.at[k]) for k in range(n)]
        for cp in mine:
            cp.start()
        started = []
        for k in range(n):
            own = srcs[k].at[pl.ds(c * half[k], half[k])]
            for j, chip in enumerate(chips):
                started.append(copy(k, j, q, c, (*chip, c), src=own))
                started[-1].start()
        for k in range(n):
            for j, chip in enumerate(chips):
                qq = 2 * chip[0] + chip[1]
                copy(k, j, qq, c, (x, y, c)).wait_recv()
                started.append(copy(k, 3 + j, qq, c, sibling))
                started[-1].start()
        for k in range(n):
            for j, chip in enumerate(chips):
                copy(k, 3 + j, 2 * chip[0] + chip[1], 1 - c, (x, y, c)).wait_recv()
        for cp in started:
            cp.wait_send()
        for cp in mine:
            cp.wait()

    return pl.pallas_call(
        body, name="gather_weights",
        out_shape=[jax.ShapeDtypeStruct((4,) + s.shape, BF16) for s in shards],
        in_specs=[pl.BlockSpec(memory_space=pltpu.VMEM)] * n, out_specs=[ANY] * n,
        scratch_shapes=[pltpu.VMEM(s.shape, BF16) for s in shards]
        + [pltpu.SemaphoreType.DMA((6 * n,)), pltpu.SemaphoreType.DMA((6 * n,)), pltpu.SemaphoreType.DMA((n,))],
        compiler_params=pltpu.CompilerParams(vmem_limit_bytes=VMEM_LIMIT),
    )(*shards)


def _swap_halves_with_sibling(grads, small):
    n = len(grads)

    def body(*refs):
        srcs, small_ref, dsts, small_dst = refs[:n], refs[n], refs[n + 1:2 * n + 1], refs[2 * n + 1]
        send_sems, recv_sems = refs[2 * n + 2:]
        x, y, c = _place()
        copies = []
        for k in range(n):
            copies.append(pltpu.make_async_remote_copy(
                src_ref=srcs[k].at[:, 1 - c], dst_ref=dsts[k], send_sem=send_sems.at[k], recv_sem=recv_sems.at[k],
                device_id=(x, y, 1 - c), device_id_type=MESH))
        copies.append(pltpu.make_async_remote_copy(
            src_ref=small_ref, dst_ref=small_dst, send_sem=send_sems.at[n], recv_sem=recv_sems.at[n],
            device_id=(x, y, 1 - c), device_id_type=MESH))
        for cp in copies:
            cp.start()
        for cp in copies:
            cp.wait()

    return pl.pallas_call(
        body, name="swap_halves",
        out_shape=[jax.ShapeDtypeStruct((4,) + g.shape[2:], g.dtype) for g in grads]
        + [jax.ShapeDtypeStruct(small.shape, small.dtype)],
        in_specs=[ANY] * (n + 1), out_specs=[ANY] * (n + 1),
        scratch_shapes=[pltpu.SemaphoreType.DMA((n + 1,)), pltpu.SemaphoreType.DMA((n + 1,))],
    )(*grads, small)


def _scatter_to_chips(parts, small):
    n = len(parts)

    def body(*refs):
        srcs, small_ref, dsts, small_dst = refs[:n], refs[n], refs[n + 1:2 * n + 1], refs[2 * n + 1]
        send_sems, recv_sems = refs[2 * n + 2:]
        x, y, c = _place()
        chips = [(1 - x, y), (x, 1 - y), (1 - x, 1 - y)]
        copies = []
        for j, chip in enumerate(chips):
            for k in range(n):
                copies.append(pltpu.make_async_remote_copy(
                    src_ref=srcs[k].at[2 * chip[0] + chip[1]], dst_ref=dsts[k].at[j],
                    send_sem=send_sems.at[3 * k + j], recv_sem=recv_sems.at[3 * k + j],
                    device_id=(*chip, c), device_id_type=MESH))
            copies.append(pltpu.make_async_remote_copy(
                src_ref=small_ref, dst_ref=small_dst.at[j],
                send_sem=send_sems.at[3 * n + j], recv_sem=recv_sems.at[3 * n + j],
                device_id=(*chip, c), device_id_type=MESH))
        for cp in copies:
            cp.start()
        for cp in copies:
            cp.wait()

    return pl.pallas_call(
        body, name="scatter_to_chips",
        out_shape=[jax.ShapeDtypeStruct((3,) + p.shape[1:], p.dtype) for p in parts]
        + [jax.ShapeDtypeStruct((3,) + small.shape, small.dtype)],
        in_specs=[ANY] * (n + 1), out_specs=[ANY] * (n + 1),
        scratch_shapes=[pltpu.SemaphoreType.DMA((3 * n + 3,)), pltpu.SemaphoreType.DMA((3 * n + 3,))],
    )(*parts, small)


def _join_halves_with_sibling(halves):
    n = len(halves)

    def body(*refs):
        srcs, dsts = refs[:n], refs[n:2 * n]
        send_sems, recv_sems = refs[2 * n:]
        x, y, c = _place()
        copies = [pltpu.make_async_remote_copy(
            src_ref=srcs[k], dst_ref=dsts[k], send_sem=send_sems.at[k], recv_sem=recv_sems.at[k],
            device_id=(x, y, 1 - c), device_id_type=MESH) for k in range(n)]
        for cp in copies:
            cp.start()
        for cp in copies:
            cp.wait()

    return pl.pallas_call(
        body, name="join_halves",
        out_shape=[jax.ShapeDtypeStruct(h.shape, h.dtype) for h in halves],
        in_specs=[ANY] * n, out_specs=[ANY] * n,
        scratch_shapes=[pltpu.SemaphoreType.DMA((n,)), pltpu.SemaphoreType.DMA((n,))],
    )(*halves)


def _row_block(rows, cols, n_arrays):
    cap = max(8, (VMEM_LIMIT // 4) // (8 * n_arrays * cols))
    rb = rows
    while rb > cap and rb % 2 == 0:
        rb //= 2
    return rb


def _add_own_half(grad, landed, c):
    _, _, hr, cols = grad.shape
    rb = _row_block(hr, cols, 3)

    def body(c_ref, g_ref, l_ref, o_ref):
        o_ref[...] = (g_ref[...] + l_ref[...]).astype(BF16)

    return pl.pallas_call(
        body, name="add_own_half",
        out_shape=jax.ShapeDtypeStruct(landed.shape, BF16),
        grid_spec=pltpu.PrefetchScalarGridSpec(
            num_scalar_prefetch=1, grid=(4, hr // rb),
            in_specs=[pl.BlockSpec((None, None, rb, cols), lambda qq, r, c_ref: (qq, c_ref[0], r, 0)),
                      pl.BlockSpec((None, rb, cols), lambda qq, r, c_ref: (qq, r, 0))],
            out_specs=pl.BlockSpec((None, rb, cols), lambda qq, r, c_ref: (qq, r, 0))),
    )(jnp.reshape(c, (1,)).astype(jnp.int32), grad, landed)


def _add_pairs(a, b, name):
    rows, cols = a.shape
    rb = _row_block(rows, cols, 3)

    def body(a_ref, b_ref, o_ref):
        o_ref[...] = a_ref[...] + b_ref[...]

    spec = pl.BlockSpec((rb, cols), lambda r: (r, 0))
    return pl.pallas_call(body, name=name, out_shape=jax.ShapeDtypeStruct(a.shape, F32), grid=(rows // rb,),
                          in_specs=[spec, spec], out_specs=spec)(a, b)


def _sum_four(own, slots, q, name):
    _, rows, cols = slots.shape
    rb = _row_block(rows, cols, 5)

    def body(q_ref, own_ref, s_ref, o_ref):
        o_ref[...] = ((own_ref[...].astype(F32) + s_ref[0].astype(F32))
                      + (s_ref[1].astype(F32) + s_ref[2].astype(F32)))

    if own.ndim == 3:
        own_spec = pl.BlockSpec((None, rb, cols), lambda r, q_ref: (q_ref[0], r, 0))
    else:
        own_spec = pl.BlockSpec((rb, cols), lambda r, q_ref: (r, 0))
    return pl.pallas_call(
        body, name=name, out_shape=jax.ShapeDtypeStruct((rows, cols), F32),
        grid_spec=pltpu.PrefetchScalarGridSpec(
            num_scalar_prefetch=1, grid=(rows // rb,),
            in_specs=[own_spec, pl.BlockSpec((3, rb, cols), lambda r, q_ref: (0, r, 0))],
            out_specs=pl.BlockSpec((rb, cols), lambda r, q_ref: (r, 0))),
    )(jnp.reshape(q, (1,)).astype(jnp.int32), own, slots)


def _adamw_math(w, g, m, v):
    nm = ADAM_B1 * m + (1.0 - ADAM_B1) * g
    nv = ADAM_B2 * v + (1.0 - ADAM_B2) * (g * g)
    m_hat = nm / (1.0 - ADAM_B1 ** ADAM_STEP)
    v_hat = nv / (1.0 - ADAM_B2 ** ADAM_STEP)
    return -ADAM_LR * (m_hat / (jnp.sqrt(v_hat) + ADAM_EPS) + ADAM_WD * w), nm, nv


def _adamw(w, g, m, v, name):
    rows, cols = w.shape
    rb = _row_block(rows, cols, 7)

    def body(w_ref, g_ref, m_ref, v_ref, d_ref, nm_ref, nv_ref):
        d_ref[...], nm_ref[...], nv_ref[...] = _adamw_math(w_ref[...], g_ref[...], m_ref[...], v_ref[...])

    spec = pl.BlockSpec((rb, cols), lambda r: (r, 0))
    out = jax.ShapeDtypeStruct(w.shape, F32)
    return pl.pallas_call(body, name=name, out_shape=[out, out, out], grid=(rows // rb,),
                          in_specs=[spec] * 4, out_specs=[spec] * 3)(w, g, m, v)


def _adamw_joined(w, g_mine, g_sibling, m, v, c, name):
    rows, cols = w.shape
    rb = _row_block(rows // 2, cols, 9)
    nb = rows // 2 // rb

    def body(c_ref, w_ref, gm_ref, gs_ref, m_ref, v_ref, g_ref, d_ref, nm_ref, nv_ref):
        g = jnp.where(c_ref[0] == pl.program_id(0), gm_ref[...], gs_ref[...])
        g_ref[...] = g
        d_ref[...], nm_ref[...], nv_ref[...] = _adamw_math(w_ref[...], g, m_ref[...], v_ref[...])

    full = pl.BlockSpec((rb, cols), lambda hf, r, c_ref: (hf * nb + r, 0))
    part = pl.BlockSpec((rb, cols), lambda hf, r, c_ref: (r, 0))
    out = jax.ShapeDtypeStruct(w.shape, F32)
    return pl.pallas_call(
        body, name=name, out_shape=[out] * 4,
        grid_spec=pltpu.PrefetchScalarGridSpec(
            num_scalar_prefetch=1, grid=(2, nb),
            in_specs=[full, part, part, full, full], out_specs=[full] * 4),
    )(jnp.reshape(c, (1,)).astype(jnp.int32), w, g_mine, g_sibling, m, v)


def _front_forward(x, w_in, pool_w, pool_scale, sgu_g, sgu_b, sgu_wm, sgu_bias_t, tm):
    T, D = x.shape
    nq, _, cq = w_in.shape
    G, PG = pool_w.shape[0], pool_w.shape[1]
    nt = T // tm

    def body(x_ref, win_any, pw_any, ps_ref, lg_ref, lb_ref, sw_ref, sb_ref, h_ref, y_ref, win_v, pw_v, carry, sems):
        i = pl.program_id(0)

        @pl.when(i == 0)
        def _():
            c1 = pltpu.make_async_copy(win_any, win_v, sems.at[0])
            c2 = pltpu.make_async_copy(pw_any, pw_v, sems.at[1])
            c1.start()
            c2.start()
            carry[...] = jnp.zeros_like(carry)
            c1.wait()
            c2.wait()

        xb = x_ref[...].astype(BF16)
        for qq in range(nq):
            h_ref[:, qq * cq:(qq + 1) * cq] = _mm(xb, win_v[qq])

        a = h_ref[:, 0:D]
        ext = jnp.concatenate([carry[...], a], axis=0)
        carry[...] = a[tm - HALO:, :]
        for g, w in enumerate(POOL_WINDOWS):
            sl = slice(g * PG, (g + 1) * PG)
            pooled = _causal_window_sum(ext[:, sl], w) * _inv_count(i * tm, tm, w) - a[:, sl]
            mixed = _mm(pooled.astype(BF16), pw_v[g])
            z = h_ref[:, 3 * D + g * PG:3 * D + (g + 1) * PG]
            y_ref[:, sl] = (mixed * ps_ref[:, sl] * (z * jax.nn.sigmoid(z))).astype(BF16)

        for hd in range(D // PG):
            sl = slice(hd * PG, (hd + 1) * PG)
            vhat, _ = _norm_rows(_gelu(h_ref[:, 2 * D + hd * PG:2 * D + (hd + 1) * PG]))
            vn = (vhat * lg_ref[:, sl] + lb_ref[:, sl]).astype(BF16)
            for n in range(tm // CHUNK):
                rs = slice(n * CHUNK, (n + 1) * CHUNK)
                sv = _mm(sw_ref[hd], vn[rs, :]) + sb_ref[:, hd:hd + 1]
                u = h_ref[rs, D + hd * PG:D + (hd + 1) * PG]
                z = h_ref[rs, 4 * D + hd * PG:4 * D + (hd + 1) * PG]
                y_ref[rs, D + hd * PG:D + (hd + 1) * PG] = (_gelu(u) * sv * (z * jax.nn.sigmoid(z))).astype(BF16)

    vec = pl.BlockSpec((1, D), lambda i: (0, 0))
    return pl.pallas_call(
        body, name="front_forward",
        out_shape=[jax.ShapeDtypeStruct((T, 5 * D), F32), jax.ShapeDtypeStruct((T, 2 * D), BF16)],
        grid=(nt,),
        in_specs=[pl.BlockSpec((tm, D), lambda i: (i, 0)), ANY, ANY, vec, vec, vec,
                  pl.BlockSpec(sgu_wm.shape, lambda i: (0, 0, 0)), pl.BlockSpec(sgu_bias_t.shape, lambda i: (0, 0))],
        out_specs=[pl.BlockSpec((tm, 5 * D), lambda i: (i, 0)), pl.BlockSpec((tm, 2 * D), lambda i: (i, 0))],
        scratch_shapes=[pltpu.VMEM(w_in.shape, BF16), pltpu.VMEM(pool_w.shape, BF16), pltpu.VMEM((HALO, D), F32),
                        pltpu.SemaphoreType.DMA((2,))],
        compiler_params=pltpu.CompilerParams(dimension_semantics=("arbitrary",), vmem_limit_bytes=VMEM_LIMIT),
    )(x, w_in, pool_w, pool_scale, sgu_g, sgu_b, sgu_wm, sgu_bias_t)


def _tail(y, x, p, target, w_out, w_gate, w_ple, ln_g, ln_b, gate_b, tm):
    T, D = x.shape
    K = p.shape[1]
    nq, _, cq = w_ple.shape
    nt = T // tm

    def body(y_ref, x_ref, p_ref, t_ref, wout_any, wg_any, wp_any, lng_ref, lnb_ref, bg_ref,
             dxp_ref, dy_ref, dwout_any, dwg_any, dwp_any, dlng_ref, dlnb_ref, dbg_ref, ssq_ref,
             wout_v, wg_v, wp_v, dwout_acc, dwg_acc, dwp_acc, sems):
        i = pl.program_id(0)

        @pl.when(i == 0)
        def _():
            loads = [pltpu.make_async_copy(s, d, sems.at[k])
                     for k, (s, d) in enumerate(((wout_any, wout_v), (wg_any, wg_v), (wp_any, wp_v)))]
            for cp in loads:
                cp.start()
            for ref in (dwout_acc, dwg_acc, dwp_acc, dlng_ref, dlnb_ref, dbg_ref, ssq_ref):
                ref[...] = jnp.zeros_like(ref)
            for cp in loads:
                cp.wait()

        yb = y_ref[...]
        pb = p_ref[...].astype(BF16)
        xhat, rstd = _norm_rows(DEEPNORM_ALPHA * x_ref[...] + _mm(yb, wout_v[...]))
        x1 = xhat * lng_ref[...] + lnb_ref[...]
        x1b = x1.astype(BF16)
        gate = jax.nn.sigmoid(_mm(x1b, wg_v[...]) + bg_ref[...])
        e = jnp.concatenate([_mm(pb, wp_v[qq]) for qq in range(nq)], axis=1)
        diff = x1 + gate * e - t_ref[...]
        ssq_ref[...] += jnp.sum(diff * diff, axis=0, keepdims=True)

        dout = diff * (1.0 / D)
        d_e = (dout * gate).astype(BF16)
        dgl = dout * e * gate * (1.0 - gate)
        dglb = dgl.astype(BF16)
        for qq in range(nq):
            dwp_acc[qq] += _mm_tn(pb, d_e[:, qq * cq:(qq + 1) * cq])
        dwg_acc[...] += _mm_tn(x1b, dglb)
        dbg_ref[...] += jnp.sum(dgl, axis=0, keepdims=True)
        d_x1 = dout + _mm_nt(dglb, wg_v[...])
        dlng_ref[...] += jnp.sum(d_x1 * xhat, axis=0, keepdims=True)
        dlnb_ref[...] += jnp.sum(d_x1, axis=0, keepdims=True)
        d_r = _norm_rows_bwd(d_x1 * lng_ref[...], xhat, rstd)
        drb = d_r.astype(BF16)
        dxp_ref[...] = DEEPNORM_ALPHA * d_r
        dwout_acc[...] += _mm_tn(yb, drb)
        dy_ref[...] = _mm_nt(drb, wout_v[...])

        @pl.when(i == nt - 1)
        def _():
            stores = [pltpu.make_async_copy(s, d, sems.at[k])
                      for k, (s, d) in enumerate(((dwout_acc, dwout_any), (dwg_acc, dwg_any), (dwp_acc, dwp_any)))]
            for cp in stores:
                cp.start()
            for cp in stores:
                cp.wait()

    vec = pl.BlockSpec((1, D), lambda i: (0, 0))
    vec_shape = jax.ShapeDtypeStruct((1, D), F32)

    def tile(cols):
        return pl.BlockSpec((tm, cols), lambda i: (i, 0))

    return pl.pallas_call(
        body, name="tail",
        out_shape=[jax.ShapeDtypeStruct((T, D), F32), jax.ShapeDtypeStruct((T, 2 * D), F32),
                   jax.ShapeDtypeStruct(w_out.shape, F32), jax.ShapeDtypeStruct(w_gate.shape, F32),
                   jax.ShapeDtypeStruct(w_ple.shape, F32), vec_shape, vec_shape, vec_shape, vec_shape],
        grid=(nt,),
        in_specs=[tile(2 * D), tile(D), tile(K), tile(D), ANY, ANY, ANY, vec, vec, vec],
        out_specs=[tile(D), tile(2 * D), ANY, ANY, ANY, vec, vec, vec, vec],
        scratch_shapes=[pltpu.VMEM(w_out.shape, BF16), pltpu.VMEM(w_gate.shape, BF16), pltpu.VMEM(w_ple.shape, BF16),
                        pltpu.VMEM(w_out.shape, F32), pltpu.VMEM(w_gate.shape, F32), pltpu.VMEM(w_ple.shape, F32),
                        pltpu.SemaphoreType.DMA((3,))],
        compiler_params=pltpu.CompilerParams(dimension_semantics=("arbitrary",), vmem_limit_bytes=VMEM_LIMIT),
    )(y, x, p, target, w_out, w_gate, w_ple, ln_g, ln_b, gate_b)


def _front_backward(h, d_y, pool_w, pool_scale, sgu_g, sgu_b, sgu_wm, sgu_bias_t, tm):
    T = h.shape[0]
    D = h.shape[1] // 5
    G, PG = pool_w.shape[0], pool_w.shape[1]
    nt = T // tm
    hpt = tm // HALO

    def body(h_ref, halo_ref, dy_ref, pw_ref, ps_ref, lg_ref, lb_ref, sw_ref, sb_ref,
             dh_ref, dpw_ref, dps_ref, dlg_ref, dlb_ref, dsw_ref, dsb_ref, carry):
        i = pl.program_id(0)
        ti = nt - 1 - i

        @pl.when(i == 0)
        def _():
            carry[...] = jnp.zeros_like(carry)
            for ref in (dpw_ref, dps_ref, dlg_ref, dlb_ref, dsw_ref, dsb_ref):
                ref[...] = jnp.zeros_like(ref)

        a = h_ref[:, 0:D]
        before = jnp.where(ti > 0, halo_ref[...], 0.0)
        ext = jnp.concatenate([before, a], axis=0)
        for g, w in enumerate(POOL_WINDOWS):
            sl = slice(g * PG, (g + 1) * PG)
            inv = _inv_count(ti * tm, tm, w)
            pooled = (_causal_window_sum(ext[:, sl], w) * inv - a[:, sl]).astype(BF16)
            mixed = _mm(pooled, pw_ref[g])
            z = h_ref[:, 3 * D + g * PG:3 * D + (g + 1) * PG]
            sig = jax.nn.sigmoid(z)
            dy = dy_ref[:, sl]
            d_ypool = dy * (z * sig)
            dh_ref[:, 3 * D + g * PG:3 * D + (g + 1) * PG] = (
                dy * (mixed * ps_ref[:, sl]) * (sig * (1.0 + z * (1.0 - sig)))).astype(BF16)
            dps_ref[:, sl] += jnp.sum(d_ypool * mixed, axis=0, keepdims=True)
            d_mixed = (d_ypool * ps_ref[:, sl]).astype(BF16)
            dpw_ref[g] += _mm_tn(pooled, d_mixed)
            d_pooled = _mm_nt(d_mixed, pw_ref[g])
            scaled = d_pooled * inv
            after = jnp.concatenate([scaled, carry[:, sl]], axis=0)
            carry[:, sl] = scaled[:HALO, :]
            dh_ref[:, sl] = (_anticausal_window_sum(after, w) - d_pooled).astype(BF16)

        for hd in range(D // PG):
            sl = slice(hd * PG, (hd + 1) * PG)
            vg, dvg = _gelu_and_grad(h_ref[:, 2 * D + hd * PG:2 * D + (hd + 1) * PG])
            vhat, rstd = _norm_rows(vg)
            vn = (vhat * lg_ref[:, sl] + lb_ref[:, sl]).astype(BF16)
            d_vn_chunks = []
            for n in range(tm // CHUNK):
                rs = slice(n * CHUNK, (n + 1) * CHUNK)
                sv = _mm(sw_ref[hd], vn[rs, :]) + sb_ref[:, hd:hd + 1]
                ug, dug = _gelu_and_grad(h_ref[rs, D + hd * PG:D + (hd + 1) * PG])
                z = h_ref[rs, 4 * D + hd * PG:4 * D + (hd + 1) * PG]
                sig = jax.nn.sigmoid(z)
                dy = dy_ref[rs, D + hd * PG:D + (hd + 1) * PG]
                d_ysgu = dy * (z * sig)
                dh_ref[rs, 4 * D + hd * PG:4 * D + (hd + 1) * PG] = (
                    dy * (ug * sv) * (sig * (1.0 + z * (1.0 - sig)))).astype(BF16)
                dh_ref[rs, D + hd * PG:D + (hd + 1) * PG] = (d_ysgu * sv * dug).astype(BF16)
                d_sv = d_ysgu * ug
                dsb_ref[:, hd:hd + 1] += jnp.sum(d_sv, axis=1, keepdims=True)
                d_svb = d_sv.astype(BF16)
                dsw_ref[hd] += _mm_nt(d_svb, vn[rs, :])
                d_vn_chunks.append(_mm_tn(sw_ref[hd], d_svb))
            d_vn = jnp.concatenate(d_vn_chunks, axis=0)
            dlg_ref[:, sl] += jnp.sum(d_vn * vhat, axis=0, keepdims=True)
            dlb_ref[:, sl] += jnp.sum(d_vn, axis=0, keepdims=True)
            d_vg = _norm_rows_bwd(d_vn * lg_ref[:, sl], vhat, rstd)
            dh_ref[:, 2 * D + hd * PG:2 * D + (hd + 1) * PG] = (d_vg * dvg).astype(BF16)

    vec = pl.BlockSpec((1, D), lambda i: (0, 0))
    vec_shape = jax.ShapeDtypeStruct((1, D), F32)

    def whole(shape):
        return pl.BlockSpec(shape, lambda i: (0,) * len(shape))

    return pl.pallas_call(
        body, name="front_backward",
        out_shape=[jax.ShapeDtypeStruct((T, 5 * D), BF16), jax.ShapeDtypeStruct(pool_w.shape, F32), vec_shape, vec_shape,
                   vec_shape, jax.ShapeDtypeStruct(sgu_wm.shape, F32), jax.ShapeDtypeStruct(sgu_bias_t.shape, F32)],
        grid=(nt,),
        in_specs=[pl.BlockSpec((tm, 5 * D), lambda i: (nt - 1 - i, 0)),
                  pl.BlockSpec((HALO, D), lambda i: (jnp.maximum((nt - 1 - i) * hpt - 1, 0), 0)),
                  pl.BlockSpec((tm, 2 * D), lambda i: (nt - 1 - i, 0)),
                  whole(pool_w.shape), vec, vec, vec, whole(sgu_wm.shape), whole(sgu_bias_t.shape)],
        out_specs=[pl.BlockSpec((tm, 5 * D), lambda i: (nt - 1 - i, 0)), whole(pool_w.shape), vec, vec, vec,
                   whole(sgu_wm.shape), whole(sgu_bias_t.shape)],
        scratch_shapes=[pltpu.VMEM((HALO, D), F32)],
        compiler_params=pltpu.CompilerParams(dimension_semantics=("arbitrary",), vmem_limit_bytes=VMEM_LIMIT),
    )(h, h, d_y, pool_w, pool_scale, sgu_g, sgu_b, sgu_wm, sgu_bias_t)


def _input_backward(d_h, x, dx_part, w_in, tm):
    T, D = x.shape
    nq, _, cq = w_in.shape
    nt = T // tm

    def body(dh_ref, x_ref, dxp_ref, win_any, dx_ref, dwin_any, win_v, dwin_acc, sems):
        i = pl.program_id(0)

        @pl.when(i == 0)
        def _():
            cp = pltpu.make_async_copy(win_any, win_v, sems.at[0])
            cp.start()
            dwin_acc[...] = jnp.zeros_like(dwin_acc)
            cp.wait()

        xb = x_ref[...].astype(BF16)
        dx = dxp_ref[...]
        for qq in range(nq):
            dhq = dh_ref[:, qq * cq:(qq + 1) * cq]
            dx = dx + _mm_nt(dhq, win_v[qq])
            dwin_acc[qq] += _mm_tn(xb, dhq)
        dx_ref[...] = dx

        @pl.when(i == nt - 1)
        def _():
            cp = pltpu.make_async_copy(dwin_acc, dwin_any, sems.at[0])
            cp.start()
            cp.wait()

    return pl.pallas_call(
        body, name="input_backward",
        out_shape=[jax.ShapeDtypeStruct((T, D), F32), jax.ShapeDtypeStruct(w_in.shape, F32)],
        grid=(nt,),
        in_specs=[pl.BlockSpec((tm, 5 * D), lambda i: (i, 0)), pl.BlockSpec((tm, D), lambda i: (i, 0)),
                  pl.BlockSpec((tm, D), lambda i: (i, 0)), ANY],
        out_specs=[pl.BlockSpec((tm, D), lambda i: (i, 0)), ANY],
        scratch_shapes=[pltpu.VMEM(w_in.shape, BF16), pltpu.VMEM(w_in.shape, F32), pltpu.SemaphoreType.DMA((1,))],
        compiler_params=pltpu.CompilerParams(dimension_semantics=("arbitrary",), vmem_limit_bytes=VMEM_LIMIT),
    )(d_h, x, dx_part, w_in)


def _token_tile(T):
    tm = 256
    while T % tm:
        tm //= 2
    return tm


def kernel(x, p, w_in, pool_w, pool_scale, sgu_ln_g, sgu_ln_b, sgu_w, sgu_b, w_out, ln_g, ln_b, ple_w, ple_gate_w, ple_gate_b, loss_target, m_w_in, m_pool_w, m_pool_scale, m_sgu_ln_g, m_sgu_ln_b, m_sgu_w, m_sgu_b, m_w_out, m_ln_g, m_ln_b, m_ple_w, m_ple_gate_w, m_ple_gate_b, v_w_in, v_pool_w, v_pool_scale, v_sgu_ln_g, v_sgu_ln_b, v_sgu_w, v_sgu_b, v_w_out, v_ln_g, v_ln_b, v_ple_w, v_ple_gate_w, v_ple_gate_b):
    c = lax.axis_index("c")
    T, D = x.shape[1], x.shape[2]
    tm = _token_tile(T)
    x2, p2, tgt = x[0], p[0, 0], loss_target[0]
    G, PGQ, PG = pool_w.shape[1], pool_w.shape[2], pool_w.shape[3]

    shards = [w_in[0], w_out[0], ple_gate_w[0], ple_w[0], pool_w[0].reshape(G * PGQ, PG)]
    w_in_f, w_out_f, w_gate_f, w_ple_f, pool_f = _gather_weights(shards)
    w_out_f = w_out_f.reshape(-1, D)
    w_gate_f = w_gate_f.reshape(-1, D)
    pool_f = pool_f.reshape(4, G, PGQ, PG).transpose(1, 0, 2, 3).reshape(G, 4 * PGQ, PG)
    tril = jnp.tril(jnp.ones((CHUNK, CHUNK), dtype=bool))
    sgu_wm = jnp.where(tril[None], sgu_w[0], 0.0).astype(BF16)
    sgu_bias_t = sgu_b[0].T

    h, y = _front_forward(x2, w_in_f, pool_f, pool_scale, sgu_ln_g, sgu_ln_b, sgu_wm, sgu_bias_t, tm)
    (dx_part, d_y, d_w_out, d_w_gate, d_w_ple, d_ln_g, d_ln_b, d_gate_b, ssq) = _tail(
        y, x2, p2, tgt, w_out_f, w_gate_f, w_ple_f, ln_g, ln_b, ple_gate_b, tm)
    d_h, d_pool_w, d_pool_scale, d_sgu_g, d_sgu_b, d_sgu_w, d_sgu_bias_t = _front_backward(
        h, d_y, pool_f, pool_scale, sgu_ln_g, sgu_ln_b, sgu_wm, sgu_bias_t, tm)
    d_x, d_w_in = _input_backward(d_h, x2, dx_part, w_in_f, tm)
    loss = lax.psum((0.5 / D) * jnp.sum(ssq), ("x", "y", "c"))

    grads = [d_w_in, d_w_out.reshape(4, -1, D), d_w_gate.reshape(4, -1, D), d_w_ple,
             d_pool_w.reshape(G, 4, PGQ, PG).transpose(1, 0, 2, 3).reshape(4, G * PGQ, PG)]
    grads = [g.reshape(4, 2, g.shape[1] // 2, g.shape[2]) for g in grads]
    d_sgu_w = jnp.where(tril[None], d_sgu_w, 0.0)
    small_names = ["pool_scale", "sgu_ln_g", "sgu_ln_b", "ln_g", "ln_b", "ple_gate_b", "sgu_b", "sgu_w"]
    small_grads = [d_pool_scale, d_sgu_g, d_sgu_b, d_ln_g, d_ln_b, d_gate_b, d_sgu_bias_t.T, d_sgu_w]

    def pack(arrays):
        rows = [a.reshape(-1) for a in arrays[:6]] + [jnp.pad(arrays[6].reshape(-1), (0, 2 * D - arrays[6].size))]
        return jnp.concatenate([r.reshape(-1, D) for r in rows] + [arrays[7].reshape(-1, D)], axis=0)

    def unpack(packed, like):
        out = [packed[k].reshape(like[k].shape) for k in range(6)]
        out.append(packed[6, :like[6].size].reshape(like[6].shape))
        out.append(packed[8:].reshape(like[7].shape))
        return out

    small = pack(small_grads)
    *landed, small_landed = _swap_halves_with_sibling(grads, small)
    parts = [_add_own_half(g, l, c) for g, l in zip(grads, landed)]
    small_chip = _add_pairs(small, small_landed, "add_small")
    *slots, small_slots = _scatter_to_chips(parts, small_chip)
    q = 2 * lax.axis_index("x") + lax.axis_index("y")
    halves = [_sum_four(pt, s, q, "sum_four_%d" % k) for k, (pt, s) in enumerate(zip(parts, slots))]
    small_total = _sum_four(small_chip, small_slots, q, "sum_four_small")
    sibling_halves = _join_halves_with_sibling(halves)

    big_names = ["w_in", "w_out", "ple_gate_w", "ple_w", "pool_w"]
    given = dict(w_in=(w_in, m_w_in, v_w_in), w_out=(w_out, m_w_out, v_w_out),
                 ple_gate_w=(ple_gate_w, m_ple_gate_w, v_ple_gate_w), ple_w=(ple_w, m_ple_w, v_ple_w),
                 pool_w=(pool_w, m_pool_w, v_pool_w), pool_scale=(pool_scale, m_pool_scale, v_pool_scale),
                 sgu_ln_g=(sgu_ln_g, m_sgu_ln_g, v_sgu_ln_g), sgu_ln_b=(sgu_ln_b, m_sgu_ln_b, v_sgu_ln_b),
                 sgu_w=(sgu_w, m_sgu_w, v_sgu_w), sgu_b=(sgu_b, m_sgu_b, v_sgu_b), ln_g=(ln_g, m_ln_g, v_ln_g),
                 ln_b=(ln_b, m_ln_b, v_ln_b), ple_gate_b=(ple_gate_b, m_ple_gate_b, v_ple_gate_b))
    grad, delta, new_m, new_v = {}, {}, {}, {}
    for name, g_mine, g_sibling in zip(big_names, halves, sibling_halves):
        w, m, v = given[name]
        flat = (2 * g_mine.shape[0], g_mine.shape[1])
        outs = _adamw_joined(w.reshape(flat), g_mine, g_sibling, m.reshape(flat), v.reshape(flat), c, "adamw_" + name)
        grad[name], delta[name], new_m[name], new_v[name] = (t.reshape(w.shape) for t in outs)
    small_w, small_m, small_v = (pack([given[n][k] for n in small_names]) for k in range(3))
    small_out = _adamw(small_w, small_total, small_m, small_v, "adamw_small")
    like = [given[n][0] for n in small_names]
    for k, name in enumerate(small_names):
        grad[name], delta[name], new_m[name], new_v[name] = (unpack(t, like)[k] for t in (small_total, *small_out))

    order = ["w_in", "pool_w", "pool_scale", "sgu_ln_g", "sgu_ln_b", "sgu_w", "sgu_b", "w_out", "ln_g", "ln_b",
             "ple_w", "ple_gate_w", "ple_gate_b"]
    return (loss, d_x[None], *[grad[n] for n in order], *[delta[n] for n in order],
            *[new_m[n] for n in order], *[new_v[n] for n in order])
```

```python
import functools
import math

import jax
import jax.numpy as jnp
from jax import lax
from jax.experimental import pallas as pl
from jax.experimental.pallas import tpu as pltpu

F32, BF16 = jnp.float32, jnp.bfloat16
MESH = pl.DeviceIdType.MESH
ANY = pl.BlockSpec(memory_space=pl.ANY)

POOL_WINDOWS = (2, 4, 8, 16)
HALO = 16
CHUNK = 128
LN_EPS = 1e-5
DEEPNORM_ALPHA = 2.0 ** 0.25
ADAM_LR, ADAM_B1, ADAM_B2, ADAM_EPS, ADAM_WD, ADAM_STEP = 1e-3, 0.9, 0.999, 1e-8, 0.01, 10
VMEM_LIMIT = 56 * 1024 * 1024
GELU_K = math.sqrt(2.0 / math.pi)
GELU_C = 0.044715


def _mm(a, b):
    return jnp.dot(a, b, preferred_element_type=F32)


def _mm_nt(a, b):
    return lax.dot_general(a, b, (((1,), (1,)), ((), ())), preferred_element_type=F32)


def _mm_tn(a, b):
    return lax.dot_general(a, b, (((0,), (0,)), ((), ())), preferred_element_type=F32)


def _gelu(x):
    t = jnp.tanh(GELU_K * (x + GELU_C * x * x * x))
    return 0.5 * x * (1.0 + t)


def _gelu_and_grad(x):
    x2 = x * x
    t = jnp.tanh(GELU_K * (x + GELU_C * x2 * x))
    g = 0.5 * x * (1.0 + t)
    dg = 0.5 * (1.0 + t) + 0.5 * x * (1.0 - t * t) * (GELU_K * (1.0 + 3.0 * GELU_C * x2))
    return g, dg


def _norm_rows(x):
    mu = jnp.mean(x, axis=-1, keepdims=True)
    xc = x - mu
    var = jnp.mean(xc * xc, axis=-1, keepdims=True)
    rstd = lax.rsqrt(var + LN_EPS)
    return xc * rstd, rstd


def _norm_rows_bwd(dxhat, xhat, rstd):
    m1 = jnp.mean(dxhat, axis=-1, keepdims=True)
    m2 = jnp.mean(dxhat * xhat, axis=-1, keepdims=True)
    return rstd * (dxhat - m1 - xhat * m2)


def _inv_count(row0, rows, w):
    t = row0 + lax.broadcasted_iota(jnp.int32, (rows, 1), 0)
    return 1.0 / jnp.minimum(t + 1, w).astype(F32)


def _causal_window_sum(ext, w):
    s, sh = ext, 1
    while sh < w:
        s = s + pltpu.roll(s, sh, axis=0)
        sh *= 2
    return s[HALO:, :]


def _anticausal_window_sum(ext, w):
    n, s, sh = ext.shape[0], ext, 1
    while sh < w:
        s = s + pltpu.roll(s, n - sh, axis=0)
        sh *= 2
    return s[: n - HALO, :]


def _place():
    return lax.axis_index("x"), lax.axis_index("y"), lax.axis_index("c")


def _gather_weights(shards):
    n = len(shards)
    half = [s.shape[0] // 2 for s in shards]

    def body(*refs):
        wide, dsts, srcs = refs[:n], refs[n:2 * n], refs[2 * n:3 * n]
        send_sems, recv_sems, local_sems = refs[3 * n:]
        for k in range(n):
            for r0 in range(0, 2 * half[k], CHUNK):
                srcs[k][r0:r0 + CHUNK, :] = wide[k][r0:r0 + CHUNK, :].astype(BF16)
        x, y, c = _place()
        q = 2 * x + y
        sibling = (x, y, 1 - c)
        chips = [(1 - x, y), (x, 1 - y), (1 - x, 1 - y)]

        def rows(k, qq, cc):
            return dsts[k].at[qq, pl.ds(cc * half[k], half[k])]

        def copy(k, sem, qq, cc, to, src=None):
            return pltpu.make_async_remote_copy(
                src_ref=rows(k, qq, cc) if src is None else src, dst_ref=rows(k, qq, cc),
                send_sem=send_sems.at[6 * k + sem], recv_sem=recv_sems.at[6 * k + sem],
                device_id=to, device_id_type=MESH)

        mine = [pltpu.make_async_copy(srcs[k], dsts[k].at[q], local_sems.at[k]) for k in range(n)]
        for cp in mine:
            cp.start()
        started = []
        for k in range(n):
            own = srcs[k].at[pl.ds(c * half[k], half[k])]
            for j, chip in enumerate(chips):
                started.append(copy(k, j, q, c, (*chip, c), src=own))
                started[-1].start()
        for k in range(n):
            for j, chip in enumerate(chips):
                qq = 2 * chip[0] + chip[1]
                copy(k, j, qq, c, (x, y, c)).wait_recv()
                started.append(copy(k, 3 + j, qq, c, sibling))
                started[-1].start()
        for k in range(n):
            for j, chip in enumerate(chips):
                copy(k, 3 + j, 2 * chip[0] + chip[1], 1 - c, (x, y, c)).wait_recv()
        for cp in started:
            cp.wait_send()
        for cp in mine:
            cp.wait()

    return pl.pallas_call(
        body, name="gather_weights",
        out_shape=[jax.ShapeDtypeStruct((4,) + s.shape, BF16) for s in shards],
        in_specs=[pl.BlockSpec(memory_space=pltpu.VMEM)] * n, out_specs=[ANY] * n,
        scratch_shapes=[pltpu.VMEM(s.shape, BF16) for s in shards]
        + [pltpu.SemaphoreType.DMA((6 * n,)), pltpu.SemaphoreType.DMA((6 * n,)), pltpu.SemaphoreType.DMA((n,))],
        compiler_params=pltpu.CompilerParams(vmem_limit_bytes=VMEM_LIMIT),
    )(*shards)


def _swap_copies(srcs, dsts, send_sems, recv_sems):
    x, y, c = _place()
    return [pltpu.make_async_remote_copy(
        src_ref=src.at[:, 1 - c] if len(src.shape) == 4 else src, dst_ref=dst,
        send_sem=send_sems.at[k], recv_sem=recv_sems.at[k], device_id=(x, y, 1 - c), device_id_type=MESH)
        for k, (src, dst) in enumerate(zip(srcs, dsts))]


def _swap_shapes(grads):
    return [jax.ShapeDtypeStruct((4,) + g.shape[2:] if g.ndim == 4 else g.shape, g.dtype) for g in grads]


def _scatter_copies(srcs, dsts, send_sems, recv_sems):
    x, y, c = _place()
    copies = []
    for j, chip in enumerate([(1 - x, y), (x, 1 - y), (1 - x, 1 - y)]):
        for k, (src, dst) in enumerate(zip(srcs, dsts)):
            copies.append(pltpu.make_async_remote_copy(
                src_ref=src.at[2 * chip[0] + chip[1]] if len(src.shape) == 3 else src, dst_ref=dst.at[j],
                send_sem=send_sems.at[3 * k + j], recv_sem=recv_sems.at[3 * k + j],
                device_id=(*chip, c), device_id_type=MESH))
    return copies


def _scatter_shapes(parts):
    return [jax.ShapeDtypeStruct((3,) + (p.shape[1:] if p.ndim == 3 else p.shape), p.dtype) for p in parts]


def _join_halves_with_sibling(halves):
    n = len(halves)

    def body(*refs):
        srcs, dsts = refs[:n], refs[n:2 * n]
        send_sems, recv_sems = refs[2 * n:]
        x, y, c = _place()
        copies = [pltpu.make_async_remote_copy(
            src_ref=srcs[k], dst_ref=dsts[k], send_sem=send_sems.at[k], recv_sem=recv_sems.at[k],
            device_id=(x, y, 1 - c), device_id_type=MESH) for k in range(n)]
        for cp in copies:
            cp.start()
        for cp in copies:
            cp.wait()

    return pl.pallas_call(
        body, name="join_halves",
        out_shape=[jax.ShapeDtypeStruct(h.shape, h.dtype) for h in halves],
        in_specs=[ANY] * n, out_specs=[ANY] * n,
        scratch_shapes=[pltpu.SemaphoreType.DMA((n,)), pltpu.SemaphoreType.DMA((n,))],
    )(*halves)


def _row_block(rows, cols, n_arrays):
    cap = max(8, (VMEM_LIMIT // 4) // (8 * n_arrays * cols))
    rb = rows
    while rb > cap and rb % 2 == 0:
        rb //= 2
    return rb


def _add_own_half(grad, landed, c):
    _, _, hr, cols = grad.shape
    rb = _row_block(hr, cols, 3)

    def body(c_ref, g_ref, l_ref, o_ref):
        o_ref[...] = (g_ref[...] + l_ref[...]).astype(BF16)

    return pl.pallas_call(
        body, name="add_own_half",
        out_shape=jax.ShapeDtypeStruct(landed.shape, BF16),
        grid_spec=pltpu.PrefetchScalarGridSpec(
            num_scalar_prefetch=1, grid=(4, hr // rb),
            in_specs=[pl.BlockSpec((None, None, rb, cols), lambda qq, r, c_ref: (qq, c_ref[0], r, 0)),
                      pl.BlockSpec((None, rb, cols), lambda qq, r, c_ref: (qq, r, 0))],
            out_specs=pl.BlockSpec((None, rb, cols), lambda qq, r, c_ref: (qq, r, 0))),
    )(jnp.reshape(c, (1,)).astype(jnp.int32), grad, landed)


def _add_pairs(a, b, name):
    rows, cols = a.shape
    rb = _row_block(rows, cols, 3)

    def body(a_ref, b_ref, o_ref):
        o_ref[...] = a_ref[...] + b_ref[...]

    spec = pl.BlockSpec((rb, cols), lambda r: (r, 0))
    return pl.pallas_call(body, name=name, out_shape=jax.ShapeDtypeStruct(a.shape, F32), grid=(rows // rb,),
                          in_specs=[spec, spec], out_specs=spec)(a, b)


def _sum_four(own, slots, q, name):
    _, rows, cols = slots.shape
    rb = _row_block(rows, cols, 5)

    def body(q_ref, own_ref, s_ref, o_ref):
        o_ref[...] = ((own_ref[...].astype(F32) + s_ref[0].astype(F32))
                      + (s_ref[1].astype(F32) + s_ref[2].astype(F32)))

    if own.ndim == 3:
        own_spec = pl.BlockSpec((None, rb, cols), lambda r, q_ref: (q_ref[0], r, 0))
    else:
        own_spec = pl.BlockSpec((rb, cols), lambda r, q_ref: (r, 0))
    return pl.pallas_call(
        body, name=name, out_shape=jax.ShapeDtypeStruct((rows, cols), F32),
        grid_spec=pltpu.PrefetchScalarGridSpec(
            num_scalar_prefetch=1, grid=(rows // rb,),
            in_specs=[own_spec, pl.BlockSpec((3, rb, cols), lambda r, q_ref: (0, r, 0))],
            out_specs=pl.BlockSpec((rb, cols), lambda r, q_ref: (r, 0))),
    )(jnp.reshape(q, (1,)).astype(jnp.int32), own, slots)


def _adamw_math(w, g, m, v):
    nm = ADAM_B1 * m + (1.0 - ADAM_B1) * g
    nv = ADAM_B2 * v + (1.0 - ADAM_B2) * (g * g)
    m_hat = nm / (1.0 - ADAM_B1 ** ADAM_STEP)
    v_hat = nv / (1.0 - ADAM_B2 ** ADAM_STEP)
    return -ADAM_LR * (m_hat / (jnp.sqrt(v_hat) + ADAM_EPS) + ADAM_WD * w), nm, nv


def _adamw(w, g, m, v, name):
    rows, cols = w.shape
    rb = _row_block(rows, cols, 7)

    def body(w_ref, g_ref, m_ref, v_ref, d_ref, nm_ref, nv_ref):
        d_ref[...], nm_ref[...], nv_ref[...] = _adamw_math(w_ref[...], g_ref[...], m_ref[...], v_ref[...])

    spec = pl.BlockSpec((rb, cols), lambda r: (r, 0))
    out = jax.ShapeDtypeStruct(w.shape, F32)
    return pl.pallas_call(body, name=name, out_shape=[out, out, out], grid=(rows // rb,),
                          in_specs=[spec] * 4, out_specs=[spec] * 3)(w, g, m, v)


def _adamw_joined(w, g_mine, g_sibling, m, v, c, name):
    rows, cols = w.shape
    rb = _row_block(rows // 2, cols, 9)
    nb = rows // 2 // rb

    def body(c_ref, w_ref, gm_ref, gs_ref, m_ref, v_ref, g_ref, d_ref, nm_ref, nv_ref):
        g = jnp.where(c_ref[0] == pl.program_id(0), gm_ref[...], gs_ref[...])
        g_ref[...] = g
        d_ref[...], nm_ref[...], nv_ref[...] = _adamw_math(w_ref[...], g, m_ref[...], v_ref[...])

    full = pl.BlockSpec((rb, cols), lambda hf, r, c_ref: (hf * nb + r, 0))
    part = pl.BlockSpec((rb, cols), lambda hf, r, c_ref: (r, 0))
    out = jax.ShapeDtypeStruct(w.shape, F32)
    return pl.pallas_call(
        body, name=name, out_shape=[out] * 4,
        grid_spec=pltpu.PrefetchScalarGridSpec(
            num_scalar_prefetch=1, grid=(2, nb),
            in_specs=[full, part, part, full, full], out_specs=[full] * 4),
    )(jnp.reshape(c, (1,)).astype(jnp.int32), w, g_mine, g_sibling, m, v)


def _front_forward(x, w_in, pool_w, pool_scale, sgu_g, sgu_b, sgu_wm, sgu_bias_t, tm):
    T, D = x.shape
    nq, _, cq = w_in.shape
    G, PG = pool_w.shape[0], pool_w.shape[1]
    nt = T // tm

    def body(x_ref, win_any, pw_any, ps_ref, lg_ref, lb_ref, sw_ref, sb_ref, h_ref, y_ref, win_v, pw_v, carry, sems):
        i = pl.program_id(0)

        @pl.when(i == 0)
        def _():
            c1 = pltpu.make_async_copy(win_any, win_v, sems.at[0])
            c2 = pltpu.make_async_copy(pw_any, pw_v, sems.at[1])
            c1.start()
            c2.start()
            carry[...] = jnp.zeros_like(carry)
            c1.wait()
            c2.wait()

        xb = x_ref[...].astype(BF16)
        for qq in range(nq):
            h_ref[:, qq * cq:(qq + 1) * cq] = _mm(xb, win_v[qq])

        a = h_ref[:, 0:D]
        ext = jnp.concatenate([carry[...], a], axis=0)
        carry[...] = a[tm - HALO:, :]
        for g, w in enumerate(POOL_WINDOWS):
            sl = slice(g * PG, (g + 1) * PG)
            pooled = _causal_window_sum(ext[:, sl], w) * _inv_count(i * tm, tm, w) - a[:, sl]
            mixed = _mm(pooled.astype(BF16), pw_v[g])
            z = h_ref[:, 3 * D + g * PG:3 * D + (g + 1) * PG]
            y_ref[:, sl] = (mixed * ps_ref[:, sl] * (z * jax.nn.sigmoid(z))).astype(BF16)

        for hd in range(D // PG):
            sl = slice(hd * PG, (hd + 1) * PG)
            vhat, _ = _norm_rows(_gelu(h_ref[:, 2 * D + hd * PG:2 * D + (hd + 1) * PG]))
            vn = (vhat * lg_ref[:, sl] + lb_ref[:, sl]).astype(BF16)
            for n in range(tm // CHUNK):
                rs = slice(n * CHUNK, (n + 1) * CHUNK)
                sv = _mm(sw_ref[hd], vn[rs, :]) + sb_ref[:, hd:hd + 1]
                u = h_ref[rs, D + hd * PG:D + (hd + 1) * PG]
                z = h_ref[rs, 4 * D + hd * PG:4 * D + (hd + 1) * PG]
                y_ref[rs, D + hd * PG:D + (hd + 1) * PG] = (_gelu(u) * sv * (z * jax.nn.sigmoid(z))).astype(BF16)

    vec = pl.BlockSpec((1, D), lambda i: (0, 0))
    return pl.pallas_call(
        body, name="front_forward",
        out_shape=[jax.ShapeDtypeStruct((T, 5 * D), F32), jax.ShapeDtypeStruct((T, 2 * D), BF16)],
        grid=(nt,),
        in_specs=[pl.BlockSpec((tm, D), lambda i: (i, 0)), ANY, ANY, vec, vec, vec,
                  pl.BlockSpec(sgu_wm.shape, lambda i: (0, 0, 0)), pl.BlockSpec(sgu_bias_t.shape, lambda i: (0, 0))],
        out_specs=[pl.BlockSpec((tm, 5 * D), lambda i: (i, 0)), pl.BlockSpec((tm, 2 * D), lambda i: (i, 0))],
        scratch_shapes=[pltpu.VMEM(w_in.shape, BF16), pltpu.VMEM(pool_w.shape, BF16), pltpu.VMEM((HALO, D), F32),
                        pltpu.SemaphoreType.DMA((2,))],
        compiler_params=pltpu.CompilerParams(dimension_semantics=("arbitrary",), vmem_limit_bytes=VMEM_LIMIT),
    )(x, w_in, pool_w, pool_scale, sgu_g, sgu_b, sgu_wm, sgu_bias_t)


def _tail(y, x, p, target, w_out, w_gate, w_ple, ln_g, ln_b, gate_b, tm):
    T, D = x.shape
    K = p.shape[1]
    nq, _, cq = w_ple.shape
    nt = T // tm

    def body(y_ref, x_ref, p_ref, t_ref, wout_any, wg_any, wp_any, lng_ref, lnb_ref, bg_ref,
             dxp_ref, dy_ref, dwout_any, dwg_any, dwp_any, dlng_ref, dlnb_ref, dbg_ref, ssq_ref,
             wout_v, wg_v, wp_v, dwout_acc, dwg_acc, dwp_acc, sems):
        i = pl.program_id(0)

        @pl.when(i == 0)
        def _():
            loads = [pltpu.make_async_copy(s, d, sems.at[k])
                     for k, (s, d) in enumerate(((wout_any, wout_v), (wg_any, wg_v), (wp_any, wp_v)))]
            for cp in loads:
                cp.start()
            for ref in (dwout_acc, dwg_acc, dwp_acc, dlng_ref, dlnb_ref, dbg_ref, ssq_ref):
                ref[...] = jnp.zeros_like(ref)
            for cp in loads:
                cp.wait()

        yb = y_ref[...]
        pb = p_ref[...].astype(BF16)
        xhat, rstd = _norm_rows(DEEPNORM_ALPHA * x_ref[...] + _mm(yb, wout_v[...]))
        x1 = xhat * lng_ref[...] + lnb_ref[...]
        x1b = x1.astype(BF16)
        gate = jax.nn.sigmoid(_mm(x1b, wg_v[...]) + bg_ref[...])
        e = jnp.concatenate([_mm(pb, wp_v[qq]) for qq in range(nq)], axis=1)
        diff = x1 + gate * e - t_ref[...]
        ssq_ref[...] += jnp.sum(diff * diff, axis=0, keepdims=True)

        dout = diff * (1.0 / D)
        d_e = (dout * gate).astype(BF16)
        dgl = dout * e * gate * (1.0 - gate)
        dglb = dgl.astype(BF16)
        for qq in range(nq):
            dwp_acc[qq] += _mm_tn(pb, d_e[:, qq * cq:(qq + 1) * cq])
        dwg_acc[...] += _mm_tn(x1b, dglb)
        dbg_ref[...] += jnp.sum(dgl, axis=0, keepdims=True)
        d_x1 = dout + _mm_nt(dglb, wg_v[...])
        dlng_ref[...] += jnp.sum(d_x1 * xhat, axis=0, keepdims=True)
        dlnb_ref[...] += jnp.sum(d_x1, axis=0, keepdims=True)
        d_r = _norm_rows_bwd(d_x1 * lng_ref[...], xhat, rstd)
        drb = d_r.astype(BF16)
        dxp_ref[...] = DEEPNORM_ALPHA * d_r
        dwout_acc[...] += _mm_tn(yb, drb)
        dy_ref[...] = _mm_nt(drb, wout_v[...])

        @pl.when(i == nt - 1)
        def _():
            stores = [pltpu.make_async_copy(s, d, sems.at[k])
                      for k, (s, d) in enumerate(((dwout_acc, dwout_any), (dwg_acc, dwg_any), (dwp_acc, dwp_any)))]
            for cp in stores:
                cp.start()
            for cp in stores:
                cp.wait()

    vec = pl.BlockSpec((1, D), lambda i: (0, 0))
    vec_shape = jax.ShapeDtypeStruct((1, D), F32)

    def tile(cols):
        return pl.BlockSpec((tm, cols), lambda i: (i, 0))

    return pl.pallas_call(
        body, name="tail",
        out_shape=[jax.ShapeDtypeStruct((T, D), F32), jax.ShapeDtypeStruct((T, 2 * D), F32),
                   jax.ShapeDtypeStruct(w_out.shape, F32), jax.ShapeDtypeStruct(w_gate.shape, F32),
                   jax.ShapeDtypeStruct(w_ple.shape, F32), vec_shape, vec_shape, vec_shape, vec_shape],
        grid=(nt,),
        in_specs=[tile(2 * D), tile(D), tile(K), tile(D), ANY, ANY, ANY, vec, vec, vec],
        out_specs=[tile(D), tile(2 * D), ANY, ANY, ANY, vec, vec, vec, vec],
        scratch_shapes=[pltpu.VMEM(w_out.shape, BF16), pltpu.VMEM(w_gate.shape, BF16), pltpu.VMEM(w_ple.shape, BF16),
                        pltpu.VMEM(w_out.shape, F32), pltpu.VMEM(w_gate.shape, F32), pltpu.VMEM(w_ple.shape, F32),
                        pltpu.SemaphoreType.DMA((3,))],
        compiler_params=pltpu.CompilerParams(dimension_semantics=("arbitrary",), vmem_limit_bytes=VMEM_LIMIT),
    )(y, x, p, target, w_out, w_gate, w_ple, ln_g, ln_b, gate_b)


def _front_backward(h, d_y, pool_w, pool_scale, sgu_g, sgu_b, sgu_wm, sgu_bias_t, tm):
    T = h.shape[0]
    D = h.shape[1] // 5
    G, PG = pool_w.shape[0], pool_w.shape[1]
    nt = T // tm
    hpt = tm // HALO

    def body(h_ref, halo_ref, dy_ref, pw_ref, ps_ref, lg_ref, lb_ref, sw_ref, sb_ref,
             dh_ref, dpw_ref, dps_ref, dlg_ref, dlb_ref, dsw_ref, dsb_ref, carry):
        i = pl.program_id(0)
        ti = nt - 1 - i

        @pl.when(i == 0)
        def _():
            carry[...] = jnp.zeros_like(carry)
            for ref in (dpw_ref, dps_ref, dlg_ref, dlb_ref, dsw_ref, dsb_ref):
                ref[...] = jnp.zeros_like(ref)

        a = h_ref[:, 0:D]
        before = jnp.where(ti > 0, halo_ref[...], 0.0)
        ext = jnp.concatenate([before, a], axis=0)
        for g, w in enumerate(POOL_WINDOWS):
            sl = slice(g * PG, (g + 1) * PG)
            inv = _inv_count(ti * tm, tm, w)
            pooled = (_causal_window_sum(ext[:, sl], w) * inv - a[:, sl]).astype(BF16)
            mixed = _mm(pooled, pw_ref[g])
            z = h_ref[:, 3 * D + g * PG:3 * D + (g + 1) * PG]
            sig = jax.nn.sigmoid(z)
            dy = dy_ref[:, sl]
            d_ypool = dy * (z * sig)
            dh_ref[:, 3 * D + g * PG:3 * D + (g + 1) * PG] = (
                dy * (mixed * ps_ref[:, sl]) * (sig * (1.0 + z * (1.0 - sig)))).astype(BF16)
            dps_ref[:, sl] += jnp.sum(d_ypool * mixed, axis=0, keepdims=True)
            d_mixed = (d_ypool * ps_ref[:, sl]).astype(BF16)
            dpw_ref[g] += _mm_tn(pooled, d_mixed)
            d_pooled = _mm_nt(d_mixed, pw_ref[g])
            scaled = d_pooled * inv
            after = jnp.concatenate([scaled, carry[:, sl]], axis=0)
            carry[:, sl] = scaled[:HALO, :]
            dh_ref[:, sl] = (_anticausal_window_sum(after, w) - d_pooled).astype(BF16)

        for hd in range(D // PG):
            sl = slice(hd * PG, (hd + 1) * PG)
            vg, dvg = _gelu_and_grad(h_ref[:, 2 * D + hd * PG:2 * D + (hd + 1) * PG])
            vhat, rstd = _norm_rows(vg)
            vn = (vhat * lg_ref[:, sl] + lb_ref[:, sl]).astype(BF16)
            d_vn_chunks = []
            for n in range(tm // CHUNK):
                rs = slice(n * CHUNK, (n + 1) * CHUNK)
                sv = _mm(sw_ref[hd], vn[rs, :]) + sb_ref[:, hd:hd + 1]
                ug, dug = _gelu_and_grad(h_ref[rs, D + hd * PG:D + (hd + 1) * PG])
                z = h_ref[rs, 4 * D + hd * PG:4 * D + (hd + 1) * PG]
                sig = jax.nn.sigmoid(z)
                dy = dy_ref[rs, D + hd * PG:D + (hd + 1) * PG]
                d_ysgu = dy * (z * sig)
                dh_ref[rs, 4 * D + hd * PG:4 * D + (hd + 1) * PG] = (
                    dy * (ug * sv) * (sig * (1.0 + z * (1.0 - sig)))).astype(BF16)
                dh_ref[rs, D + hd * PG:D + (hd + 1) * PG] = (d_ysgu * sv * dug).astype(BF16)
                d_sv = d_ysgu * ug
                dsb_ref[:, hd:hd + 1] += jnp.sum(d_sv, axis=1, keepdims=True)
                d_svb = d_sv.astype(BF16)
                dsw_ref[hd] += _mm_nt(d_svb, vn[rs, :])
                d_vn_chunks.append(_mm_tn(sw_ref[hd], d_svb))
            d_vn = jnp.concatenate(d_vn_chunks, axis=0)
            dlg_ref[:, sl] += jnp.sum(d_vn * vhat, axis=0, keepdims=True)
            dlb_ref[:, sl] += jnp.sum(d_vn, axis=0, keepdims=True)
            d_vg = _norm_rows_bwd(d_vn * lg_ref[:, sl], vhat, rstd)
            dh_ref[:, 2 * D + hd * PG:2 * D + (hd + 1) * PG] = (d_vg * dvg).astype(BF16)

    vec = pl.BlockSpec((1, D), lambda i: (0, 0))
    vec_shape = jax.ShapeDtypeStruct((1, D), F32)

    def whole(shape):
        return pl.BlockSpec(shape, lambda i: (0,) * len(shape))

    return pl.pallas_call(
        body, name="front_backward",
        out_shape=[jax.ShapeDtypeStruct((T, 5 * D), BF16), jax.ShapeDtypeStruct(pool_w.shape, F32), vec_shape, vec_shape,
                   vec_shape, jax.ShapeDtypeStruct(sgu_wm.shape, F32), jax.ShapeDtypeStruct(sgu_bias_t.shape, F32)],
        grid=(nt,),
        in_specs=[pl.BlockSpec((tm, 5 * D), lambda i: (nt - 1 - i, 0)),
                  pl.BlockSpec((HALO, D), lambda i: (jnp.maximum((nt - 1 - i) * hpt - 1, 0), 0)),
                  pl.BlockSpec((tm, 2 * D), lambda i: (nt - 1 - i, 0)),
                  whole(pool_w.shape), vec, vec, vec, whole(sgu_wm.shape), whole(sgu_bias_t.shape)],
        out_specs=[pl.BlockSpec((tm, 5 * D), lambda i: (nt - 1 - i, 0)), whole(pool_w.shape), vec, vec, vec,
                   whole(sgu_wm.shape), whole(sgu_bias_t.shape)],
        scratch_shapes=[pltpu.VMEM((HALO, D), F32)],
        compiler_params=pltpu.CompilerParams(dimension_semantics=("arbitrary",), vmem_limit_bytes=VMEM_LIMIT),
    )(h, h, d_y, pool_w, pool_scale, sgu_g, sgu_b, sgu_wm, sgu_bias_t)


def _weight_backward(d_h, x, q, swap_srcs, tm):
    T, D = x.shape
    cq = d_h.shape[1] // 4
    hr = D // 2
    nt = T // tm
    ns = len(swap_srcs)

    def body(q_ref, dh_ref, x_ref, *refs):
        srcs, out_any, dsts = refs[:ns], refs[ns], refs[ns + 1:2 * ns + 1]
        (acc, land_a, send_b, land_b, mine_f, theirs_f,
         a_send, a_recv, b_send, b_recv, j_sems, o_sems, s_send, s_recv) = refs[2 * ns + 1:]
        s, t = pl.program_id(0), pl.program_id(1)
        x_, y_, c = _place()
        sibling = (x_, y_, 1 - c)
        own_rows = pl.ds(pl.multiple_of(c * hr, hr), hr)
        other_rows = pl.ds(pl.multiple_of((1 - c) * hr, hr), hr)

        @pl.when((s == 0) & (t == 0))
        def _():
            for cp in _swap_copies(srcs, dsts, s_send, s_recv):
                cp.start()

        part = _mm_tn(x_ref[...].astype(BF16), dh_ref[...])

        @pl.when(t == 0)
        def _():
            acc[...] = part

        @pl.when(t > 0)
        def _():
            acc[...] += part

        def pair_sum(slot):
            swap = pltpu.make_async_remote_copy(
                src_ref=acc.at[other_rows], dst_ref=land_a.at[slot], send_sem=a_send.at[slot],
                recv_sem=a_recv.at[slot], device_id=sibling, device_id_type=MESH)
            swap.start()
            swap.wait()
            return acc[own_rows, :] + land_a[slot]

        def to_owner(slot):
            flip_x, flip_y = (slot + 1) >> 1, (slot + 1) & 1
            owner = (1 - x_ if flip_x else x_, 1 - y_ if flip_y else y_, c)
            return pltpu.make_async_remote_copy(
                src_ref=send_b.at[slot], dst_ref=land_b.at[slot], send_sem=b_send.at[slot],
                recv_sem=b_recv.at[slot], device_id=owner, device_id_type=MESH)

        for slot in range(3):
            @pl.when((s == slot) & (t == nt - 1))
            def _(slot=slot):
                send_b[slot] = pair_sum(slot).astype(BF16)
                to_owner(slot).start()

        @pl.when((s == 3) & (t == nt - 1))
        def _():
            own = pair_sum(3)
            for slot in range(3):
                to_owner(slot).wait_recv()
            mine_f[...] = (own + land_b[0].astype(F32)) + (land_b[1].astype(F32) + land_b[2].astype(F32))
            join = pltpu.make_async_remote_copy(
                src_ref=mine_f, dst_ref=theirs_f, send_sem=j_sems.at[0], recv_sem=j_sems.at[1],
                device_id=sibling, device_id_type=MESH)
            join.start()
            out_mine = pltpu.make_async_copy(mine_f, out_any.at[own_rows], o_sems.at[0])
            out_mine.start()
            join.wait()
            out_theirs = pltpu.make_async_copy(theirs_f, out_any.at[other_rows], o_sems.at[1])
            out_theirs.start()
            for slot in range(3):
                to_owner(slot).wait_send()
            for cp in _swap_copies(srcs, dsts, s_send, s_recv):
                cp.wait()
            out_mine.wait()
            out_theirs.wait()

    def quarter(s, t, q_ref):
        return (t, jnp.where(s == 3, q_ref[0], q_ref[0] ^ (s + 1)))

    dma = pltpu.SemaphoreType.DMA
    return pl.pallas_call(
        body, name="weight_backward",
        out_shape=[jax.ShapeDtypeStruct((D, cq), F32)] + _swap_shapes(swap_srcs),
        grid_spec=pltpu.PrefetchScalarGridSpec(
            num_scalar_prefetch=1, grid=(4, nt),
            in_specs=[pl.BlockSpec((tm, cq), quarter), pl.BlockSpec((tm, D), lambda s, t, q_ref: (t, 0))] + [ANY] * ns,
            out_specs=[ANY] * (ns + 1),
            scratch_shapes=[pltpu.VMEM((D, cq), F32), pltpu.VMEM((4, hr, cq), F32), pltpu.VMEM((3, hr, cq), BF16),
                            pltpu.VMEM((3, hr, cq), BF16), pltpu.VMEM((hr, cq), F32), pltpu.VMEM((hr, cq), F32),
                            dma((4,)), dma((4,)), dma((3,)), dma((3,)), dma((2,)), dma((2,)), dma((ns,)), dma((ns,))]),
        compiler_params=pltpu.CompilerParams(dimension_semantics=("arbitrary", "arbitrary"),
                                             vmem_limit_bytes=VMEM_LIMIT),
    )(jnp.reshape(q, (1,)).astype(jnp.int32), d_h, x, *swap_srcs)


def _input_backward(d_h, dx_part, w_in, scatter_srcs, tm):
    T, D = dx_part.shape
    nq, _, cq = w_in.shape
    nt = T // tm
    ns = len(scatter_srcs)

    def body(dh_ref, dxp_ref, win_any, *refs):
        srcs, dx_ref, dsts = refs[:ns], refs[ns], refs[ns + 1:2 * ns + 1]
        win_v, sems, s_send, s_recv = refs[2 * ns + 1:]
        i = pl.program_id(0)

        @pl.when(i == 0)
        def _():
            for cp in _scatter_copies(srcs, dsts, s_send, s_recv):
                cp.start()
            cp = pltpu.make_async_copy(win_any, win_v, sems.at[0])
            cp.start()
            cp.wait()

        dx = dxp_ref[...]
        for qq in range(nq):
            dx = dx + _mm_nt(dh_ref[:, qq * cq:(qq + 1) * cq], win_v[qq])
        dx_ref[...] = dx

        @pl.when(i == nt - 1)
        def _():
            for cp in _scatter_copies(srcs, dsts, s_send, s_recv):
                cp.wait()

    return pl.pallas_call(
        body, name="input_backward",
        out_shape=[jax.ShapeDtypeStruct((T, D), F32)] + _scatter_shapes(scatter_srcs),
        grid=(nt,),
        in_specs=[pl.BlockSpec((tm, 5 * D), lambda i: (i, 0)), pl.BlockSpec((tm, D), lambda i: (i, 0)), ANY] + [ANY] * ns,
        out_specs=[pl.BlockSpec((tm, D), lambda i: (i, 0))] + [ANY] * ns,
        scratch_shapes=[pltpu.VMEM(w_in.shape, BF16), pltpu.SemaphoreType.DMA((1,)),
                        pltpu.SemaphoreType.DMA((3 * ns,)), pltpu.SemaphoreType.DMA((3 * ns,))],
        compiler_params=pltpu.CompilerParams(dimension_semantics=("arbitrary",), vmem_limit_bytes=VMEM_LIMIT),
    )(d_h, dx_part, w_in, *scatter_srcs)


def _token_tile(T):
    tm = 256
    while T % tm:
        tm //= 2
    return tm


def kernel(x, p, w_in, pool_w, pool_scale, sgu_ln_g, sgu_ln_b, sgu_w, sgu_b, w_out, ln_g, ln_b, ple_w, ple_gate_w, ple_gate_b, loss_target, m_w_in, m_pool_w, m_pool_scale, m_sgu_ln_g, m_sgu_ln_b, m_sgu_w, m_sgu_b, m_w_out, m_ln_g, m_ln_b, m_ple_w, m_ple_gate_w, m_ple_gate_b, v_w_in, v_pool_w, v_pool_scale, v_sgu_ln_g, v_sgu_ln_b, v_sgu_w, v_sgu_b, v_w_out, v_ln_g, v_ln_b, v_ple_w, v_ple_gate_w, v_ple_gate_b):
    c = lax.axis_index("c")
    T, D = x.shape[1], x.shape[2]
    tm = _token_tile(T)
    x2, p2, tgt = x[0], p[0, 0], loss_target[0]
    G, PGQ, PG = pool_w.shape[1], pool_w.shape[2], pool_w.shape[3]

    shards = [w_in[0], w_out[0], ple_gate_w[0], ple_w[0], pool_w[0].reshape(G * PGQ, PG)]
    w_in_f, w_out_f, w_gate_f, w_ple_f, pool_f = _gather_weights(shards)
    w_out_f = w_out_f.reshape(-1, D)
    w_gate_f = w_gate_f.reshape(-1, D)
    pool_f = pool_f.reshape(4, G, PGQ, PG).transpose(1, 0, 2, 3).reshape(G, 4 * PGQ, PG)
    tril = jnp.tril(jnp.ones((CHUNK, CHUNK), dtype=bool))
    sgu_wm = jnp.where(tril[None], sgu_w[0], 0.0).astype(BF16)
    sgu_bias_t = sgu_b[0].T

    h, y = _front_forward(x2, w_in_f, pool_f, pool_scale, sgu_ln_g, sgu_ln_b, sgu_wm, sgu_bias_t, tm)
    (dx_part, d_y, d_w_out, d_w_gate, d_w_ple, d_ln_g, d_ln_b, d_gate_b, ssq) = _tail(
        y, x2, p2, tgt, w_out_f, w_gate_f, w_ple_f, ln_g, ln_b, ple_gate_b, tm)
    d_h, d_pool_w, d_pool_scale, d_sgu_g, d_sgu_b, d_sgu_w, d_sgu_bias_t = _front_backward(
        h, d_y, pool_f, pool_scale, sgu_ln_g, sgu_ln_b, sgu_wm, sgu_bias_t, tm)
    loss = lax.psum((0.5 / D) * jnp.sum(ssq), ("x", "y", "c"))

    grads = [d_w_out.reshape(4, -1, D), d_w_gate.reshape(4, -1, D), d_w_ple,
             d_pool_w.reshape(G, 4, PGQ, PG).transpose(1, 0, 2, 3).reshape(4, G * PGQ, PG)]
    grads = [g.reshape(4, 2, g.shape[1] // 2, g.shape[2]) for g in grads]
    d_sgu_w = jnp.where(tril[None], d_sgu_w, 0.0)
    small_names = ["pool_scale", "sgu_ln_g", "sgu_ln_b", "ln_g", "ln_b", "ple_gate_b", "sgu_b", "sgu_w"]
    small_grads = [d_pool_scale, d_sgu_g, d_sgu_b, d_ln_g, d_ln_b, d_gate_b, d_sgu_bias_t.T, d_sgu_w]

    def pack(arrays):
        rows = [a.reshape(-1) for a in arrays[:6]] + [jnp.pad(arrays[6].reshape(-1), (0, 2 * D - arrays[6].size))]
        return jnp.concatenate([r.reshape(-1, D) for r in rows] + [arrays[7].reshape(-1, D)], axis=0)

    def unpack(packed, like):
        out = [packed[k].reshape(like[k].shape) for k in range(6)]
        out.append(packed[6, :like[6].size].reshape(like[6].shape))
        out.append(packed[8:].reshape(like[7].shape))
        return out

    small = pack(small_grads)
    q = 2 * lax.axis_index("x") + lax.axis_index("y")
    tmw = 2 * tm if T % (2 * tm) == 0 else tm
    d_w_in, *landed, small_landed = _weight_backward(d_h, x2, q, grads + [small], tmw)
    parts = [_add_own_half(g, l, c) for g, l in zip(grads, landed)]
    small_chip = _add_pairs(small, small_landed, "add_small")
    d_x, *slots, small_slots = _input_backward(d_h, dx_part, w_in_f, parts + [small_chip], tmw)
    halves = [_sum_four(pt, s, q, "sum_four_%d" % k) for k, (pt, s) in enumerate(zip(parts, slots))]
    small_total = _sum_four(small_chip, small_slots, q, "sum_four_small")
    sibling_halves = _join_halves_with_sibling(halves)

    big_names = ["w_out", "ple_gate_w", "ple_w", "pool_w"]
    given = dict(w_in=(w_in, m_w_in, v_w_in), w_out=(w_out, m_w_out, v_w_out),
                 ple_gate_w=(ple_gate_w, m_ple_gate_w, v_ple_gate_w), ple_w=(ple_w, m_ple_w, v_ple_w),
                 pool_w=(pool_w, m_pool_w, v_pool_w), pool_scale=(pool_scale, m_pool_scale, v_pool_scale),
                 sgu_ln_g=(sgu_ln_g, m_sgu_ln_g, v_sgu_ln_g), sgu_ln_b=(sgu_ln_b, m_sgu_ln_b, v_sgu_ln_b),
                 sgu_w=(sgu_w, m_sgu_w, v_sgu_w), sgu_b=(sgu_b, m_sgu_b, v_sgu_b), ln_g=(ln_g, m_ln_g, v_ln_g),
                 ln_b=(ln_b, m_ln_b, v_ln_b), ple_gate_b=(ple_gate_b, m_ple_gate_b, v_ple_gate_b))
    grad, delta, new_m, new_v = {}, {}, {}, {}
    for name, g_mine, g_sibling in zip(big_names, halves, sibling_halves):
        w, m, v = given[name]
        flat = (2 * g_mine.shape[0], g_mine.shape[1])
        outs = _adamw_joined(w.reshape(flat), g_mine, g_sibling, m.reshape(flat), v.reshape(flat), c, "adamw_" + name)
        grad[name], delta[name], new_m[name], new_v[name] = (t.reshape(w.shape) for t in outs)
    grad["w_in"] = d_w_in[None]
    delta["w_in"], new_m["w_in"], new_v["w_in"] = (
        t[None] for t in _adamw(w_in[0], d_w_in, m_w_in[0], v_w_in[0], "adamw_w_in"))
    small_w, small_m, small_v = (pack([given[n][k] for n in small_names]) for k in range(3))
    small_out = _adamw(small_w, small_total, small_m, small_v, "adamw_small")
    like = [given[n][0] for n in small_names]
    for k, name in enumerate(small_names):
        grad[name], delta[name], new_m[name], new_v[name] = (unpack(t, like)[k] for t in (small_total, *small_out))

    order = ["w_in", "pool_w", "pool_scale", "sgu_ln_g", "sgu_ln_b", "sgu_w", "sgu_b", "w_out", "ln_g", "ln_b",
             "ple_w", "ple_gate_w", "ple_gate_b"]
    return (loss, d_x[None], *[grad[n] for n in order], *[delta[n] for n in order],
            *[new_m[n] for n in order], *[new_v[n] for n in order])
```

```python
import functools
import math

import jax
import jax.numpy as jnp
from jax import lax
from jax.experimental import pallas as pl
from jax.experimental.pallas import tpu as pltpu

F32, BF16 = jnp.float32, jnp.bfloat16
MESH = pl.DeviceIdType.MESH
ANY = pl.BlockSpec(memory_space=pl.ANY)

POOL_WINDOWS = (2, 4, 8, 16)
HALO = 16
CHUNK = 128
MXU_COLS = 256
LN_EPS = 1e-5
DEEPNORM_ALPHA = 2.0 ** 0.25
ADAM_LR, ADAM_B1, ADAM_B2, ADAM_EPS, ADAM_WD, ADAM_STEP = 1e-3, 0.9, 0.999, 1e-8, 0.01, 10
VMEM_LIMIT = 56 * 1024 * 1024
GELU_K = math.sqrt(2.0 / math.pi)
GELU_C = 0.044715


def _mm(a, b):
    return jnp.dot(a, b, preferred_element_type=F32)


def _mm_nt(a, b):
    return lax.dot_general(a, b, (((1,), (1,)), ((), ())), preferred_element_type=F32)


def _mm_tn(a, b):
    return lax.dot_general(a, b, (((0,), (0,)), ((), ())), preferred_element_type=F32)


def _gelu(x):
    t = jnp.tanh(GELU_K * (x + GELU_C * x * x * x))
    return 0.5 * x * (1.0 + t)


def _gelu_and_grad(x):
    x2 = x * x
    t = jnp.tanh(GELU_K * (x + GELU_C * x2 * x))
    g = 0.5 * x * (1.0 + t)
    dg = 0.5 * (1.0 + t) + 0.5 * x * (1.0 - t * t) * (GELU_K * (1.0 + 3.0 * GELU_C * x2))
    return g, dg


def _norm_rows(x):
    mu = jnp.mean(x, axis=-1, keepdims=True)
    xc = x - mu
    var = jnp.mean(xc * xc, axis=-1, keepdims=True)
    rstd = lax.rsqrt(var + LN_EPS)
    return xc * rstd, rstd


def _norm_rows_bwd(dxhat, xhat, rstd):
    m1 = jnp.mean(dxhat, axis=-1, keepdims=True)
    m2 = jnp.mean(dxhat * xhat, axis=-1, keepdims=True)
    return rstd * (dxhat - m1 - xhat * m2)


def _inv_count(row0, rows, w):
    t = row0 + lax.broadcasted_iota(jnp.int32, (rows, 1), 0)
    return 1.0 / jnp.minimum(t + 1, w).astype(F32)


def _causal_window_sum(ext, w):
    s, sh = ext, 1
    while sh < w:
        s = s + pltpu.roll(s, sh, axis=0)
        sh *= 2
    return s[HALO:, :]


def _anticausal_window_sum(ext, w):
    n, s, sh = ext.shape[0], ext, 1
    while sh < w:
        s = s + pltpu.roll(s, n - sh, axis=0)
        sh *= 2
    return s[: n - HALO, :]


def _place():
    return lax.axis_index("x"), lax.axis_index("y"), lax.axis_index("c")


def _gather_weights(shards):
    n = len(shards)
    half = [s.shape[0] // 2 for s in shards]

    def body(*refs):
        wide, dsts, srcs = refs[:n], refs[n:2 * n], refs[2 * n:3 * n]
        send_sems, recv_sems, local_sems = refs[3 * n:]
        for k in range(n):
            for r0 in range(0, 2 * half[k], CHUNK):
                srcs[k][r0:r0 + CHUNK, :] = wide[k][r0:r0 + CHUNK, :].astype(BF16)
        x, y, c = _place()
        q = 2 * x + y
        sibling = (x, y, 1 - c)
        chips = [(1 - x, y), (x, 1 - y), (1 - x, 1 - y)]

        def rows(k, qq, cc):
            return dsts[k].at[qq, pl.ds(cc * half[k], half[k])]

        def copy(k, sem, qq, cc, to, src=None):
            return pltpu.make_async_remote_copy(
                src_ref=rows(k, qq, cc) if src is None else src, dst_ref=rows(k, qq, cc),
                send_sem=send_sems.at[6 * k + sem], recv_sem=recv_sems.at[6 * k + sem],
                device_id=to, device_id_type=MESH)

        mine = [pltpu.make_async_copy(srcs[k], dsts[k].at[q], local_sems.at[k]) for k in range(n)]
        for cp in mine:
            cp.start()
        started = []
        for k in range(n):
            own = srcs[k].at[pl.ds(c * half[k], half[k])]
            for j, chip in enumerate(chips):
                started.append(copy(k, j, q, c, (*chip, c), src=own))
                started[-1].start()
        for k in range(n):
            for j, chip in enumerate(chips):
                qq = 2 * chip[0] + chip[1]
                copy(k, j, qq, c, (x, y, c)).wait_recv()
                started.append(copy(k, 3 + j, qq, c, sibling))
                started[-1].start()
        for k in range(n):
            for j, chip in enumerate(chips):
                copy(k, 3 + j, 2 * chip[0] + chip[1], 1 - c, (x, y, c)).wait_recv()
        for cp in started:
            cp.wait_send()
        for cp in mine:
            cp.wait()

    return pl.pallas_call(
        body, name="gather_weights",
        out_shape=[jax.ShapeDtypeStruct((4,) + s.shape, BF16) for s in shards],
        in_specs=[pl.BlockSpec(memory_space=pltpu.VMEM)] * n, out_specs=[ANY] * n,
        scratch_shapes=[pltpu.VMEM(s.shape, BF16) for s in shards]
        + [pltpu.SemaphoreType.DMA((6 * n,)), pltpu.SemaphoreType.DMA((6 * n,)), pltpu.SemaphoreType.DMA((n,))],
        compiler_params=pltpu.CompilerParams(vmem_limit_bytes=VMEM_LIMIT),
    )(*shards)


def _swap_copies(srcs, dsts, send_sems, recv_sems):
    x, y, c = _place()
    return [pltpu.make_async_remote_copy(
        src_ref=src.at[:, 1 - c] if len(src.shape) == 4 else src, dst_ref=dst,
        send_sem=send_sems.at[k], recv_sem=recv_sems.at[k], device_id=(x, y, 1 - c), device_id_type=MESH)
        for k, (src, dst) in enumerate(zip(srcs, dsts))]


def _swap_shapes(grads):
    return [jax.ShapeDtypeStruct((4,) + g.shape[2:] if g.ndim == 4 else g.shape, g.dtype) for g in grads]


def _scatter_copies(srcs, dsts, send_sems, recv_sems):
    x, y, c = _place()
    copies = []
    for j, chip in enumerate([(1 - x, y), (x, 1 - y), (1 - x, 1 - y)]):
        for k, (src, dst) in enumerate(zip(srcs, dsts)):
            copies.append(pltpu.make_async_remote_copy(
                src_ref=src.at[2 * chip[0] + chip[1]] if len(src.shape) == 3 else src, dst_ref=dst.at[j],
                send_sem=send_sems.at[3 * k + j], recv_sem=recv_sems.at[3 * k + j],
                device_id=(*chip, c), device_id_type=MESH))
    return copies


def _scatter_shapes(parts):
    return [jax.ShapeDtypeStruct((3,) + (p.shape[1:] if p.ndim == 3 else p.shape), p.dtype) for p in parts]


def _join_halves_with_sibling(halves):
    n = len(halves)

    def body(*refs):
        srcs, dsts = refs[:n], refs[n:2 * n]
        send_sems, recv_sems = refs[2 * n:]
        x, y, c = _place()
        copies = [pltpu.make_async_remote_copy(
            src_ref=srcs[k], dst_ref=dsts[k], send_sem=send_sems.at[k], recv_sem=recv_sems.at[k],
            device_id=(x, y, 1 - c), device_id_type=MESH) for k in range(n)]
        for cp in copies:
            cp.start()
        for cp in copies:
            cp.wait()

    return pl.pallas_call(
        body, name="join_halves",
        out_shape=[jax.ShapeDtypeStruct(h.shape, h.dtype) for h in halves],
        in_specs=[ANY] * n, out_specs=[ANY] * n,
        scratch_shapes=[pltpu.SemaphoreType.DMA((n,)), pltpu.SemaphoreType.DMA((n,))],
    )(*halves)


def _row_block(rows, cols, n_arrays):
    cap = max(8, (VMEM_LIMIT // 4) // (8 * n_arrays * cols))
    rb = rows
    while rb > cap and rb % 2 == 0:
        rb //= 2
    return rb


def _add_own_half(grad, landed, c):
    _, _, hr, cols = grad.shape
    rb = _row_block(hr, cols, 3)

    def body(c_ref, g_ref, l_ref, o_ref):
        o_ref[...] = (g_ref[...] + l_ref[...]).astype(BF16)

    return pl.pallas_call(
        body, name="add_own_half",
        out_shape=jax.ShapeDtypeStruct(landed.shape, BF16),
        grid_spec=pltpu.PrefetchScalarGridSpec(
            num_scalar_prefetch=1, grid=(4, hr // rb),
            in_specs=[pl.BlockSpec((None, None, rb, cols), lambda qq, r, c_ref: (qq, c_ref[0], r, 0)),
                      pl.BlockSpec((None, rb, cols), lambda qq, r, c_ref: (qq, r, 0))],
            out_specs=pl.BlockSpec((None, rb, cols), lambda qq, r, c_ref: (qq, r, 0))),
    )(jnp.reshape(c, (1,)).astype(jnp.int32), grad, landed)


def _add_pairs(a, b, name):
    rows, cols = a.shape
    rb = _row_block(rows, cols, 3)

    def body(a_ref, b_ref, o_ref):
        o_ref[...] = a_ref[...] + b_ref[...]

    spec = pl.BlockSpec((rb, cols), lambda r: (r, 0))
    return pl.pallas_call(body, name=name, out_shape=jax.ShapeDtypeStruct(a.shape, F32), grid=(rows // rb,),
                          in_specs=[spec, spec], out_specs=spec)(a, b)


def _sum_four(own, slots, q, name):
    _, rows, cols = slots.shape
    rb = _row_block(rows, cols, 5)

    def body(q_ref, own_ref, s_ref, o_ref):
        o_ref[...] = ((own_ref[...].astype(F32) + s_ref[0].astype(F32))
                      + (s_ref[1].astype(F32) + s_ref[2].astype(F32)))

    if own.ndim == 3:
        own_spec = pl.BlockSpec((None, rb, cols), lambda r, q_ref: (q_ref[0], r, 0))
    else:
        own_spec = pl.BlockSpec((rb, cols), lambda r, q_ref: (r, 0))
    return pl.pallas_call(
        body, name=name, out_shape=jax.ShapeDtypeStruct((rows, cols), F32),
        grid_spec=pltpu.PrefetchScalarGridSpec(
            num_scalar_prefetch=1, grid=(rows // rb,),
            in_specs=[own_spec, pl.BlockSpec((3, rb, cols), lambda r, q_ref: (0, r, 0))],
            out_specs=pl.BlockSpec((rb, cols), lambda r, q_ref: (r, 0))),
    )(jnp.reshape(q, (1,)).astype(jnp.int32), own, slots)


def _adamw_math(w, g, m, v):
    nm = ADAM_B1 * m + (1.0 - ADAM_B1) * g
    nv = ADAM_B2 * v + (1.0 - ADAM_B2) * (g * g)
    m_hat = nm / (1.0 - ADAM_B1 ** ADAM_STEP)
    v_hat = nv / (1.0 - ADAM_B2 ** ADAM_STEP)
    return -ADAM_LR * (m_hat / (jnp.sqrt(v_hat) + ADAM_EPS) + ADAM_WD * w), nm, nv


def _adamw(w, g, m, v, name):
    rows, cols = w.shape
    rb = _row_block(rows, cols, 8)

    def body(w_ref, g_ref, m_ref, v_ref, go_ref, d_ref, nm_ref, nv_ref):
        go_ref[...] = g_ref[...]
        d_ref[...], nm_ref[...], nv_ref[...] = _adamw_math(w_ref[...], g_ref[...], m_ref[...], v_ref[...])

    spec = pl.BlockSpec((rb, cols), lambda r: (r, 0))
    out = jax.ShapeDtypeStruct(w.shape, F32)
    return pl.pallas_call(body, name=name, out_shape=[out] * 4, grid=(rows // rb,),
                          in_specs=[spec] * 4, out_specs=[spec] * 4)(w, g, m, v)


def _adamw_joined(w, g_mine, g_sibling, m, v, c, name):
    rows, cols = w.shape
    rb = _row_block(rows // 2, cols, 9)
    nb = rows // 2 // rb

    def body(c_ref, w_ref, gm_ref, gs_ref, m_ref, v_ref, g_ref, d_ref, nm_ref, nv_ref):
        g = jnp.where(c_ref[0] == pl.program_id(0), gm_ref[...], gs_ref[...])
        g_ref[...] = g
        d_ref[...], nm_ref[...], nv_ref[...] = _adamw_math(w_ref[...], g, m_ref[...], v_ref[...])

    full = pl.BlockSpec((rb, cols), lambda hf, r, c_ref: (hf * nb + r, 0))
    part = pl.BlockSpec((rb, cols), lambda hf, r, c_ref: (r, 0))
    out = jax.ShapeDtypeStruct(w.shape, F32)
    return pl.pallas_call(
        body, name=name, out_shape=[out] * 4,
        grid_spec=pltpu.PrefetchScalarGridSpec(
            num_scalar_prefetch=1, grid=(2, nb),
            in_specs=[full, part, part, full, full], out_specs=[full] * 4),
    )(jnp.reshape(c, (1,)).astype(jnp.int32), w, g_mine, g_sibling, m, v)


def _front_forward(x, w_in, pool_w, pool_scale, sgu_g, sgu_b, sgu_wm, sgu_bias_t, tm):
    T, D = x.shape
    nq, _, cq = w_in.shape
    G, PG = pool_w.shape[0], pool_w.shape[1]
    nt = T // tm

    def body(x_ref, win_any, pw_any, ps_ref, lg_ref, lb_ref, sw_ref, sb_ref, h_ref, y_ref, xt_ref,
             win_v, pw_v, carry, sems):
        i = pl.program_id(0)

        @pl.when(i == 0)
        def _():
            c1 = pltpu.make_async_copy(win_any, win_v, sems.at[0])
            c2 = pltpu.make_async_copy(pw_any, pw_v, sems.at[1])
            c1.start()
            c2.start()
            carry[...] = jnp.zeros_like(carry)
            c1.wait()
            c2.wait()

        xb = x_ref[...].astype(BF16)
        xt_ref[...] = x_ref[...].T.astype(BF16)
        for qq in range(nq):
            h_ref[:, qq * cq:(qq + 1) * cq] = _mm(xb, win_v[qq])

        a = h_ref[:, 0:D]
        ext = jnp.concatenate([carry[...], a], axis=0)
        carry[...] = a[tm - HALO:, :]
        for g, w in enumerate(POOL_WINDOWS):
            sl = slice(g * PG, (g + 1) * PG)
            pooled = _causal_window_sum(ext[:, sl], w) * _inv_count(i * tm, tm, w) - a[:, sl]
            mixed = _mm(pooled.astype(BF16), pw_v[g])
            z = h_ref[:, 3 * D + g * PG:3 * D + (g + 1) * PG]
            y_ref[:, sl] = (mixed * ps_ref[:, sl] * (z * jax.nn.sigmoid(z))).astype(BF16)

        for hd in range(D // PG):
            sl = slice(hd * PG, (hd + 1) * PG)
            vhat, _ = _norm_rows(_gelu(h_ref[:, 2 * D + hd * PG:2 * D + (hd + 1) * PG]))
            vn = (vhat * lg_ref[:, sl] + lb_ref[:, sl]).astype(BF16)
            for n in range(tm // CHUNK):
                rs = slice(n * CHUNK, (n + 1) * CHUNK)
                sv = _mm(sw_ref[hd], vn[rs, :]) + sb_ref[:, hd:hd + 1]
                u = h_ref[rs, D + hd * PG:D + (hd + 1) * PG]
                z = h_ref[rs, 4 * D + hd * PG:4 * D + (hd + 1) * PG]
                y_ref[rs, D + hd * PG:D + (hd + 1) * PG] = (_gelu(u) * sv * (z * jax.nn.sigmoid(z))).astype(BF16)

    vec = pl.BlockSpec((1, D), lambda i: (0, 0))
    return pl.pallas_call(
        body, name="front_forward",
        out_shape=[jax.ShapeDtypeStruct((T, 5 * D), F32), jax.ShapeDtypeStruct((T, 2 * D), BF16),
                   jax.ShapeDtypeStruct((D, T), BF16)],
        grid=(nt,),
        in_specs=[pl.BlockSpec((tm, D), lambda i: (i, 0)), ANY, ANY, vec, vec, vec,
                  pl.BlockSpec(sgu_wm.shape, lambda i: (0, 0, 0)), pl.BlockSpec(sgu_bias_t.shape, lambda i: (0, 0))],
        out_specs=[pl.BlockSpec((tm, 5 * D), lambda i: (i, 0)), pl.BlockSpec((tm, 2 * D), lambda i: (i, 0)),
                   pl.BlockSpec((D, tm), lambda i: (0, i))],
        scratch_shapes=[pltpu.VMEM(w_in.shape, BF16), pltpu.VMEM(pool_w.shape, BF16), pltpu.VMEM((HALO, D), F32),
                        pltpu.SemaphoreType.DMA((2,))],
        compiler_params=pltpu.CompilerParams(dimension_semantics=("arbitrary",), vmem_limit_bytes=VMEM_LIMIT),
    )(x, w_in, pool_w, pool_scale, sgu_g, sgu_b, sgu_wm, sgu_bias_t)


def _tail(y, x, p, target, w_out, w_gate, w_ple, ln_g, ln_b, gate_b, tm):
    T, D = x.shape
    K = p.shape[1]
    nq, _, cq = w_ple.shape
    nt = T // tm

    def body(y_ref, x_ref, p_ref, t_ref, wout_any, wg_any, wp_any, lng_ref, lnb_ref, bg_ref,
             dxp_ref, dy_ref, dwout_any, dwg_any, dwp_any, dlng_ref, dlnb_ref, dbg_ref, ssq_ref,
             wout_v, wg_v, wp_v, dwout_acc, dwg_acc, dwp_acc, sems):
        i = pl.program_id(0)

        @pl.when(i == 0)
        def _():
            loads = [pltpu.make_async_copy(s, d, sems.at[k])
                     for k, (s, d) in enumerate(((wout_any, wout_v), (wg_any, wg_v), (wp_any, wp_v)))]
            for cp in loads:
                cp.start()
            for ref in (dwout_acc, dwg_acc, dwp_acc, dlng_ref, dlnb_ref, dbg_ref, ssq_ref):
                ref[...] = jnp.zeros_like(ref)
            for cp in loads:
                cp.wait()

        yb = y_ref[...]
        pb = p_ref[...].astype(BF16)
        xhat, rstd = _norm_rows(DEEPNORM_ALPHA * x_ref[...] + _mm(yb, wout_v[...]))
        x1 = xhat * lng_ref[...] + lnb_ref[...]
        x1b = x1.astype(BF16)
        gate = jax.nn.sigmoid(_mm(x1b, wg_v[...]) + bg_ref[...])
        e = jnp.concatenate([_mm(pb, wp_v[qq]) for qq in range(nq)], axis=1)
        diff = x1 + gate * e - t_ref[...]
        ssq_ref[...] += jnp.sum(diff * diff, axis=0, keepdims=True)

        dout = diff * (1.0 / D)
        d_e = (dout * gate).astype(BF16)
        dgl = dout * e * gate * (1.0 - gate)
        dglb = dgl.astype(BF16)
        for qq in range(nq):
            dwp_acc[qq] += _mm_tn(pb, d_e[:, qq * cq:(qq + 1) * cq])
        dwg_acc[...] += _mm_tn(x1b, dglb)
        dbg_ref[...] += jnp.sum(dgl, axis=0, keepdims=True)
        d_x1 = dout + _mm_nt(dglb, wg_v[...])
        dlng_ref[...] += jnp.sum(d_x1 * xhat, axis=0, keepdims=True)
        dlnb_ref[...] += jnp.sum(d_x1, axis=0, keepdims=True)
        d_r = _norm_rows_bwd(d_x1 * lng_ref[...], xhat, rstd)
        drb = d_r.astype(BF16)
        dxp_ref[...] = DEEPNORM_ALPHA * d_r
        dwout_acc[...] += _mm_tn(yb, drb)
        dy_ref[...] = _mm_nt(drb, wout_v[...])

        @pl.when(i == nt - 1)
        def _():
            stores = [pltpu.make_async_copy(s, d, sems.at[k])
                      for k, (s, d) in enumerate(((dwout_acc, dwout_any), (dwg_acc, dwg_any), (dwp_acc, dwp_any)))]
            for cp in stores:
                cp.start()
            for cp in stores:
                cp.wait()

    vec = pl.BlockSpec((1, D), lambda i: (0, 0))
    vec_shape = jax.ShapeDtypeStruct((1, D), F32)

    def tile(cols):
        return pl.BlockSpec((tm, cols), lambda i: (i, 0))

    return pl.pallas_call(
        body, name="tail",
        out_shape=[jax.ShapeDtypeStruct((T, D), F32), jax.ShapeDtypeStruct((T, 2 * D), F32),
                   jax.ShapeDtypeStruct(w_out.shape, F32), jax.ShapeDtypeStruct(w_gate.shape, F32),
                   jax.ShapeDtypeStruct(w_ple.shape, F32), vec_shape, vec_shape, vec_shape, vec_shape],
        grid=(nt,),
        in_specs=[tile(2 * D), tile(D), tile(K), tile(D), ANY, ANY, ANY, vec, vec, vec],
        out_specs=[tile(D), tile(2 * D), ANY, ANY, ANY, vec, vec, vec, vec],
        scratch_shapes=[pltpu.VMEM(w_out.shape, BF16), pltpu.VMEM(w_gate.shape, BF16), pltpu.VMEM(w_ple.shape, BF16),
                        pltpu.VMEM(w_out.shape, F32), pltpu.VMEM(w_gate.shape, F32), pltpu.VMEM(w_ple.shape, F32),
                        pltpu.SemaphoreType.DMA((3,))],
        compiler_params=pltpu.CompilerParams(dimension_semantics=("arbitrary",), vmem_limit_bytes=VMEM_LIMIT),
    )(y, x, p, target, w_out, w_gate, w_ple, ln_g, ln_b, gate_b)


def _front_backward(h, d_y, pool_w, pool_scale, sgu_g, sgu_b, sgu_wm, sgu_bias_t, tm):
    T = h.shape[0]
    D = h.shape[1] // 5
    G, PG = pool_w.shape[0], pool_w.shape[1]
    nt = T // tm
    hpt = tm // HALO

    def body(h_ref, halo_ref, dy_ref, pw_ref, ps_ref, lg_ref, lb_ref, sw_ref, sb_ref,
             dh_ref, dpw_ref, dps_ref, dlg_ref, dlb_ref, dsw_ref, dsb_ref, carry):
        i = pl.program_id(0)
        ti = nt - 1 - i

        @pl.when(i == 0)
        def _():
            carry[...] = jnp.zeros_like(carry)
            for ref in (dpw_ref, dps_ref, dlg_ref, dlb_ref, dsw_ref, dsb_ref):
                ref[...] = jnp.zeros_like(ref)

        a = h_ref[:, 0:D]
        before = jnp.where(ti > 0, halo_ref[...], 0.0)
        ext = jnp.concatenate([before, a], axis=0)
        for g, w in enumerate(POOL_WINDOWS):
            sl = slice(g * PG, (g + 1) * PG)
            inv = _inv_count(ti * tm, tm, w)
            pooled = (_causal_window_sum(ext[:, sl], w) * inv - a[:, sl]).astype(BF16)
            mixed = _mm(pooled, pw_ref[g])
            z = h_ref[:, 3 * D + g * PG:3 * D + (g + 1) * PG]
            sig = jax.nn.sigmoid(z)
            dy = dy_ref[:, sl]
            d_ypool = dy * (z * sig)
            dh_ref[:, 3 * D + g * PG:3 * D + (g + 1) * PG] = (
                dy * (mixed * ps_ref[:, sl]) * (sig * (1.0 + z * (1.0 - sig)))).astype(BF16)
            dps_ref[:, sl] += jnp.sum(d_ypool * mixed, axis=0, keepdims=True)
            d_mixed = (d_ypool * ps_ref[:, sl]).astype(BF16)
            dpw_ref[g] += _mm_tn(pooled, d_mixed)
            d_pooled = _mm_nt(d_mixed, pw_ref[g])
            scaled = d_pooled * inv
            after = jnp.concatenate([scaled, carry[:, sl]], axis=0)
            carry[:, sl] = scaled[:HALO, :]
            dh_ref[:, sl] = (_anticausal_window_sum(after, w) - d_pooled).astype(BF16)

        for hd in range(D // PG):
            sl = slice(hd * PG, (hd + 1) * PG)
            vg, dvg = _gelu_and_grad(h_ref[:, 2 * D + hd * PG:2 * D + (hd + 1) * PG])
            vhat, rstd = _norm_rows(vg)
            vn = (vhat * lg_ref[:, sl] + lb_ref[:, sl]).astype(BF16)
            d_vn_chunks = []
            for n in range(tm // CHUNK):
                rs = slice(n * CHUNK, (n + 1) * CHUNK)
                sv = _mm(sw_ref[hd], vn[rs, :]) + sb_ref[:, hd:hd + 1]
                ug, dug = _gelu_and_grad(h_ref[rs, D + hd * PG:D + (hd + 1) * PG])
                z = h_ref[rs, 4 * D + hd * PG:4 * D + (hd + 1) * PG]
                sig = jax.nn.sigmoid(z)
                dy = dy_ref[rs, D + hd * PG:D + (hd + 1) * PG]
                d_ysgu = dy * (z * sig)
                dh_ref[rs, 4 * D + hd * PG:4 * D + (hd + 1) * PG] = (
                    dy * (ug * sv) * (sig * (1.0 + z * (1.0 - sig)))).astype(BF16)
                dh_ref[rs, D + hd * PG:D + (hd + 1) * PG] = (d_ysgu * sv * dug).astype(BF16)
                d_sv = d_ysgu * ug
                dsb_ref[:, hd:hd + 1] += jnp.sum(d_sv, axis=1, keepdims=True)
                d_svb = d_sv.astype(BF16)
                dsw_ref[hd] += _mm_nt(d_svb, vn[rs, :])
                d_vn_chunks.append(_mm_tn(sw_ref[hd], d_svb))
            d_vn = jnp.concatenate(d_vn_chunks, axis=0)
            dlg_ref[:, sl] += jnp.sum(d_vn * vhat, axis=0, keepdims=True)
            dlb_ref[:, sl] += jnp.sum(d_vn, axis=0, keepdims=True)
            d_vg = _norm_rows_bwd(d_vn * lg_ref[:, sl], vhat, rstd)
            dh_ref[:, 2 * D + hd * PG:2 * D + (hd + 1) * PG] = (d_vg * dvg).astype(BF16)

    vec = pl.BlockSpec((1, D), lambda i: (0, 0))
    vec_shape = jax.ShapeDtypeStruct((1, D), F32)

    def whole(shape):
        return pl.BlockSpec(shape, lambda i: (0,) * len(shape))

    return pl.pallas_call(
        body, name="front_backward",
        out_shape=[jax.ShapeDtypeStruct((T, 5 * D), BF16), jax.ShapeDtypeStruct(pool_w.shape, F32), vec_shape, vec_shape,
                   vec_shape, jax.ShapeDtypeStruct(sgu_wm.shape, F32), jax.ShapeDtypeStruct(sgu_bias_t.shape, F32)],
        grid=(nt,),
        in_specs=[pl.BlockSpec((tm, 5 * D), lambda i: (nt - 1 - i, 0)),
                  pl.BlockSpec((HALO, D), lambda i: (jnp.maximum((nt - 1 - i) * hpt - 1, 0), 0)),
                  pl.BlockSpec((tm, 2 * D), lambda i: (nt - 1 - i, 0)),
                  whole(pool_w.shape), vec, vec, vec, whole(sgu_wm.shape), whole(sgu_bias_t.shape)],
        out_specs=[pl.BlockSpec((tm, 5 * D), lambda i: (nt - 1 - i, 0)), whole(pool_w.shape), vec, vec, vec,
                   whole(sgu_wm.shape), whole(sgu_bias_t.shape)],
        scratch_shapes=[pltpu.VMEM((HALO, D), F32)],
        compiler_params=pltpu.CompilerParams(dimension_semantics=("arbitrary",), vmem_limit_bytes=VMEM_LIMIT),
    )(h, h, d_y, pool_w, pool_scale, sgu_g, sgu_b, sgu_wm, sgu_bias_t)


def _weight_backward(d_h, xt, q, swap_srcs, tm):
    D, T = xt.shape
    cq = d_h.shape[1] // 4
    hr = D // 2
    nt = T // tm
    ns = len(swap_srcs)

    def body(q_ref, dh_ref, xt_ref, *refs):
        srcs, out_any, dsts = refs[:ns], refs[ns], refs[ns + 1:2 * ns + 1]
        (acc, land_a, send_b, land_b, mine_f, theirs_f,
         a_send, a_recv, b_send, b_recv, j_sems, o_sems, s_send, s_recv) = refs[2 * ns + 1:]
        s, t = pl.program_id(0), pl.program_id(1)
        x_, y_, c = _place()
        sibling = (x_, y_, 1 - c)
        own_rows = pl.ds(pl.multiple_of(c * hr, hr), hr)
        other_rows = pl.ds(pl.multiple_of((1 - c) * hr, hr), hr)

        @pl.when((s == 0) & (t == 0))
        def _():
            for cp in _swap_copies(srcs, dsts, s_send, s_recv):
                cp.start()

        @pl.when(t == 0)
        def _():
            acc[...] = jnp.zeros_like(acc)

        for c0 in range(0, cq, MXU_COLS):
            acc[:, c0:c0 + MXU_COLS] += _mm(xt_ref[...], dh_ref[:, c0:c0 + MXU_COLS])

        def pair_sum(slot):
            swap = pltpu.make_async_remote_copy(
                src_ref=acc.at[other_rows], dst_ref=land_a.at[slot], send_sem=a_send.at[slot],
                recv_sem=a_recv.at[slot], device_id=sibling, device_id_type=MESH)
            swap.start()
            swap.wait()
            return acc[own_rows, :] + land_a[slot]

        def to_owner(slot):
            flip_x, flip_y = (slot + 1) >> 1, (slot + 1) & 1
            owner = (1 - x_ if flip_x else x_, 1 - y_ if flip_y else y_, c)
            return pltpu.make_async_remote_copy(
                src_ref=send_b.at[slot], dst_ref=land_b.at[slot], send_sem=b_send.at[slot],
                recv_sem=b_recv.at[slot], device_id=owner, device_id_type=MESH)

        for slot in range(3):
            @pl.when((s == slot) & (t == nt - 1))
            def _(slot=slot):
                send_b[slot] = pair_sum(slot).astype(BF16)
                to_owner(slot).start()

        @pl.when((s == 3) & (t == nt - 1))
        def _():
            own = pair_sum(3)
            for slot in range(3):
                to_owner(slot).wait_recv()
            mine_f[...] = (own + land_b[0].astype(F32)) + (land_b[1].astype(F32) + land_b[2].astype(F32))
            join = pltpu.make_async_remote_copy(
                src_ref=mine_f, dst_ref=theirs_f, send_sem=j_sems.at[0], recv_sem=j_sems.at[1],
                device_id=sibling, device_id_type=MESH)
            join.start()
            out_mine = pltpu.make_async_copy(mine_f, out_any.at[own_rows], o_sems.at[0])
            out_mine.start()
            join.wait()
            out_theirs = pltpu.make_async_copy(theirs_f, out_any.at[other_rows], o_sems.at[1])
            out_theirs.start()
            for slot in range(3):
                to_owner(slot).wait_send()
            for cp in _swap_copies(srcs, dsts, s_send, s_recv):
                cp.wait()
            out_mine.wait()
            out_theirs.wait()

    def quarter(s, t, q_ref):
        return (t, jnp.where(s == 3, q_ref[0], q_ref[0] ^ (s + 1)))

    dma = pltpu.SemaphoreType.DMA
    return pl.pallas_call(
        body, name="weight_backward",
        out_shape=[jax.ShapeDtypeStruct((D, cq), F32)] + _swap_shapes(swap_srcs),
        grid_spec=pltpu.PrefetchScalarGridSpec(
            num_scalar_prefetch=1, grid=(4, nt),
            in_specs=[pl.BlockSpec((tm, cq), quarter), pl.BlockSpec((D, tm), lambda s, t, q_ref: (0, t))] + [ANY] * ns,
            out_specs=[ANY] * (ns + 1),
            scratch_shapes=[pltpu.VMEM((D, cq), F32), pltpu.VMEM((4, hr, cq), F32), pltpu.VMEM((3, hr, cq), BF16),
                            pltpu.VMEM((3, hr, cq), BF16), pltpu.VMEM((hr, cq), F32), pltpu.VMEM((hr, cq), F32),
                            dma((4,)), dma((4,)), dma((3,)), dma((3,)), dma((2,)), dma((2,)), dma((ns,)), dma((ns,))]),
        compiler_params=pltpu.CompilerParams(dimension_semantics=("arbitrary", "arbitrary"),
                                             vmem_limit_bytes=VMEM_LIMIT),
    )(jnp.reshape(q, (1,)).astype(jnp.int32), d_h, xt, *swap_srcs)


def _input_backward(d_h, dx_part, w_in, scatter_srcs, tm):
    T, D = dx_part.shape
    nq, _, cq = w_in.shape
    nt = T // tm
    ns = len(scatter_srcs)

    def body(dh_ref, dxp_ref, win_any, *refs):
        srcs, dx_ref, dsts = refs[:ns], refs[ns], refs[ns + 1:2 * ns + 1]
        win_v, sems, s_send, s_recv = refs[2 * ns + 1:]
        i = pl.program_id(0)

        @pl.when(i == 0)
        def _():
            for cp in _scatter_copies(srcs, dsts, s_send, s_recv):
                cp.start()
            cp = pltpu.make_async_copy(win_any, win_v, sems.at[0])
            cp.start()
            cp.wait()

        dx = dxp_ref[...]
        for qq in range(nq):
            dx = dx + _mm_nt(dh_ref[:, qq * cq:(qq + 1) * cq], win_v[qq])
        dx_ref[...] = dx

        @pl.when(i == nt - 1)
        def _():
            for cp in _scatter_copies(srcs, dsts, s_send, s_recv):
                cp.wait()

    return pl.pallas_call(
        body, name="input_backward",
        out_shape=[jax.ShapeDtypeStruct((T, D), F32)] + _scatter_shapes(scatter_srcs),
        grid=(nt,),
        in_specs=[pl.BlockSpec((tm, 5 * D), lambda i: (i, 0)), pl.BlockSpec((tm, D), lambda i: (i, 0)), ANY] + [ANY] * ns,
        out_specs=[pl.BlockSpec((tm, D), lambda i: (i, 0))] + [ANY] * ns,
        scratch_shapes=[pltpu.VMEM(w_in.shape, BF16), pltpu.SemaphoreType.DMA((1,)),
                        pltpu.SemaphoreType.DMA((3 * ns,)), pltpu.SemaphoreType.DMA((3 * ns,))],
        compiler_params=pltpu.CompilerParams(dimension_semantics=("arbitrary",), vmem_limit_bytes=VMEM_LIMIT),
    )(d_h, dx_part, w_in, *scatter_srcs)


def _token_tile(T):
    tm = 256
    while T % tm:
        tm //= 2
    return tm


def kernel(x, p, w_in, pool_w, pool_scale, sgu_ln_g, sgu_ln_b, sgu_w, sgu_b, w_out, ln_g, ln_b, ple_w, ple_gate_w, ple_gate_b, loss_target, m_w_in, m_pool_w, m_pool_scale, m_sgu_ln_g, m_sgu_ln_b, m_sgu_w, m_sgu_b, m_w_out, m_ln_g, m_ln_b, m_ple_w, m_ple_gate_w, m_ple_gate_b, v_w_in, v_pool_w, v_pool_scale, v_sgu_ln_g, v_sgu_ln_b, v_sgu_w, v_sgu_b, v_w_out, v_ln_g, v_ln_b, v_ple_w, v_ple_gate_w, v_ple_gate_b):
    c = lax.axis_index("c")
    T, D = x.shape[1], x.shape[2]
    tm = _token_tile(T)
    x2, p2, tgt = x[0], p[0, 0], loss_target[0]
    G, PGQ, PG = pool_w.shape[1], pool_w.shape[2], pool_w.shape[3]

    shards = [w_in[0], w_out[0], ple_gate_w[0], ple_w[0], pool_w[0].reshape(G * PGQ, PG)]
    w_in_f, w_out_f, w_gate_f, w_ple_f, pool_f = _gather_weights(shards)
    w_out_f = w_out_f.reshape(-1, D)
    w_gate_f = w_gate_f.reshape(-1, D)
    pool_f = pool_f.reshape(4, G, PGQ, PG).transpose(1, 0, 2, 3).reshape(G, 4 * PGQ, PG)
    tril = jnp.tril(jnp.ones((CHUNK, CHUNK), dtype=bool))
    sgu_wm = jnp.where(tril[None], sgu_w[0], 0.0).astype(BF16)
    sgu_bias_t = sgu_b[0].T

    h, y, xt = _front_forward(x2, w_in_f, pool_f, pool_scale, sgu_ln_g, sgu_ln_b, sgu_wm, sgu_bias_t, tm)
    (dx_part, d_y, d_w_out, d_w_gate, d_w_ple, d_ln_g, d_ln_b, d_gate_b, ssq) = _tail(
        y, x2, p2, tgt, w_out_f, w_gate_f, w_ple_f, ln_g, ln_b, ple_gate_b, tm)
    d_h, d_pool_w, d_pool_scale, d_sgu_g, d_sgu_b, d_sgu_w, d_sgu_bias_t = _front_backward(
        h, d_y, pool_f, pool_scale, sgu_ln_g, sgu_ln_b, sgu_wm, sgu_bias_t, tm)
    loss = lax.psum((0.5 / D) * jnp.sum(ssq), ("x", "y", "c"))

    grads = [d_w_out.reshape(4, -1, D), d_w_gate.reshape(4, -1, D), d_w_ple,
             d_pool_w.reshape(G, 4, PGQ, PG).transpose(1, 0, 2, 3).reshape(4, G * PGQ, PG)]
    grads = [g.reshape(4, 2, g.shape[1] // 2, g.shape[2]) for g in grads]
    d_sgu_w = jnp.where(tril[None], d_sgu_w, 0.0)
    small_names = ["pool_scale", "sgu_ln_g", "sgu_ln_b", "ln_g", "ln_b", "ple_gate_b", "sgu_b", "sgu_w"]
    small_grads = [d_pool_scale, d_sgu_g, d_sgu_b, d_ln_g, d_ln_b, d_gate_b, d_sgu_bias_t.T, d_sgu_w]

    def pack(arrays):
        rows = [a.reshape(-1) for a in arrays[:6]] + [jnp.pad(arrays[6].reshape(-1), (0, 2 * D - arrays[6].size))]
        return jnp.concatenate([r.reshape(-1, D) for r in rows] + [arrays[7].reshape(-1, D)], axis=0)

    def unpack(packed, like):
        out = [packed[k].reshape(like[k].shape) for k in range(6)]
        out.append(packed[6, :like[6].size].reshape(like[6].shape))
        out.append(packed[8:].reshape(like[7].shape))
        return out

    small = pack(small_grads)
    q = 2 * lax.axis_index("x") + lax.axis_index("y")
    tmw = 2 * tm if T % (2 * tm) == 0 else tm
    d_w_in, *landed, small_landed = _weight_backward(d_h, xt, q, grads + [small], math.gcd(T, 8 * tm))
    parts = [_add_own_half(g, l, c) for g, l in zip(grads, landed)]
    small_chip = _add_pairs(small, small_landed, "add_small")
    d_x, *slots, small_slots = _input_backward(d_h, dx_part, w_in_f, parts + [small_chip], tmw)
    halves = [_sum_four(pt, s, q, "sum_four_%d" % k) for k, (pt, s) in enumerate(zip(parts, slots))]
    small_total = _sum_four(small_chip, small_slots, q, "sum_four_small")
    sibling_halves = _join_halves_with_sibling(halves)

    big_names = ["w_out", "ple_gate_w", "ple_w", "pool_w"]
    given = dict(w_in=(w_in, m_w_in, v_w_in), w_out=(w_out, m_w_out, v_w_out),
                 ple_gate_w=(ple_gate_w, m_ple_gate_w, v_ple_gate_w), ple_w=(ple_w, m_ple_w, v_ple_w),
                 pool_w=(pool_w, m_pool_w, v_pool_w), pool_scale=(pool_scale, m_pool_scale, v_pool_scale),
                 sgu_ln_g=(sgu_ln_g, m_sgu_ln_g, v_sgu_ln_g), sgu_ln_b=(sgu_ln_b, m_sgu_ln_b, v_sgu_ln_b),
                 sgu_w=(sgu_w, m_sgu_w, v_sgu_w), sgu_b=(sgu_b, m_sgu_b, v_sgu_b), ln_g=(ln_g, m_ln_g, v_ln_g),
                 ln_b=(ln_b, m_ln_b, v_ln_b), ple_gate_b=(ple_gate_b, m_ple_gate_b, v_ple_gate_b))
    grad, delta, new_m, new_v = {}, {}, {}, {}
    for name, g_mine, g_sibling in zip(big_names, halves, sibling_halves):
        w, m, v = given[name]
        flat = (2 * g_mine.shape[0], g_mine.shape[1])
        outs = _adamw_joined(w.reshape(flat), g_mine, g_sibling, m.reshape(flat), v.reshape(flat), c, "adamw_" + name)
        grad[name], delta[name], new_m[name], new_v[name] = (t.reshape(w.shape) for t in outs)
    grad["w_in"], delta["w_in"], new_m["w_in"], new_v["w_in"] = (
        t[None] for t in _adamw(w_in[0], d_w_in, m_w_in[0], v_w_in[0], "adamw_w_in"))
    small_w, small_m, small_v = (pack([given[n][k] for n in small_names]) for k in range(3))
    small_out = _adamw(small_w, small_total, small_m, small_v, "adamw_small")
    like = [given[n][0] for n in small_names]
    for k, name in enumerate(small_names):
        grad[name], delta[name], new_m[name], new_v[name] = (unpack(t, like)[k] for t in small_out)

    order = ["w_in", "pool_w", "pool_scale", "sgu_ln_g", "sgu_ln_b", "sgu_w", "sgu_b", "w_out", "ln_g", "ln_b",
             "ple_w", "ple_gate_w", "ple_gate_b"]
    return (loss, d_x[None], *[grad[n] for n in order], *[delta[n] for n in order],
            *[new_m[n] for n in order], *[new_v[n] for n in order])
```

```python
import functools
import math

import jax
import jax.numpy as jnp
from jax import lax
from jax.experimental import pallas as pl
from jax.experimental.pallas import tpu as pltpu

F32, BF16 = jnp.float32, jnp.bfloat16
MESH = pl.DeviceIdType.MESH
ANY = pl.BlockSpec(memory_space=pl.ANY)

POOL_WINDOWS = (2, 4, 8, 16)
HALO = 16
CHUNK = 128
MXU_COLS = 256
LN_EPS = 1e-5
DEEPNORM_ALPHA = 2.0 ** 0.25
ADAM_LR, ADAM_B1, ADAM_B2, ADAM_EPS, ADAM_WD, ADAM_STEP = 1e-3, 0.9, 0.999, 1e-8, 0.01, 10
VMEM_LIMIT = 56 * 1024 * 1024
GELU_K = math.sqrt(2.0 / math.pi)
GELU_C = 0.044715


def _mm(a, b):
    return jnp.dot(a, b, preferred_element_type=F32)


def _mm_nt(a, b):
    return lax.dot_general(a, b, (((1,), (1,)), ((), ())), preferred_element_type=F32)


def _mm_tn(a, b):
    return lax.dot_general(a, b, (((0,), (0,)), ((), ())), preferred_element_type=F32)


def _gelu(x):
    t = jnp.tanh(GELU_K * (x + GELU_C * x * x * x))
    return 0.5 * x * (1.0 + t)


def _gelu_and_grad(x):
    x2 = x * x
    t = jnp.tanh(GELU_K * (x + GELU_C * x2 * x))
    g = 0.5 * x * (1.0 + t)
    dg = 0.5 * (1.0 + t) + 0.5 * x * (1.0 - t * t) * (GELU_K * (1.0 + 3.0 * GELU_C * x2))
    return g, dg


def _norm_rows(x):
    mu = jnp.mean(x, axis=-1, keepdims=True)
    xc = x - mu
    var = jnp.mean(xc * xc, axis=-1, keepdims=True)
    rstd = lax.rsqrt(var + LN_EPS)
    return xc * rstd, rstd


def _norm_rows_bwd(dxhat, xhat, rstd):
    m1 = jnp.mean(dxhat, axis=-1, keepdims=True)
    m2 = jnp.mean(dxhat * xhat, axis=-1, keepdims=True)
    return rstd * (dxhat - m1 - xhat * m2)


def _inv_count(row0, rows, w):
    t = row0 + lax.broadcasted_iota(jnp.int32, (rows, 1), 0)
    return 1.0 / jnp.minimum(t + 1, w).astype(F32)


def _causal_window_sum(ext, w):
    s, sh = ext, 1
    while sh < w:
        s = s + pltpu.roll(s, sh, axis=0)
        sh *= 2
    return s[HALO:, :]


def _anticausal_window_sum(ext, w):
    n, s, sh = ext.shape[0], ext, 1
    while sh < w:
        s = s + pltpu.roll(s, n - sh, axis=0)
        sh *= 2
    return s[: n - HALO, :]


def _place():
    return lax.axis_index("x"), lax.axis_index("y"), lax.axis_index("c")


def _gather_weights(shards):
    n = len(shards)
    half = [s.shape[0] // 2 for s in shards]

    def body(*refs):
        wide, dsts, srcs = refs[:n], refs[n:2 * n], refs[2 * n:3 * n]
        send_sems, recv_sems, local_sems = refs[3 * n:]
        for k in range(n):
            for r0 in range(0, 2 * half[k], CHUNK):
                srcs[k][r0:r0 + CHUNK, :] = wide[k][r0:r0 + CHUNK, :].astype(BF16)
        x, y, c = _place()
        q = 2 * x + y
        sibling = (x, y, 1 - c)
        chips = [(1 - x, y), (x, 1 - y), (1 - x, 1 - y)]

        def rows(k, qq, cc):
            return dsts[k].at[qq, pl.ds(cc * half[k], half[k])]

        def copy(k, sem, qq, cc, to, src=None):
            return pltpu.make_async_remote_copy(
                src_ref=rows(k, qq, cc) if src is None else src, dst_ref=rows(k, qq, cc),
                send_sem=send_sems.at[6 * k + sem], recv_sem=recv_sems.at[6 * k + sem],
                device_id=to, device_id_type=MESH)

        mine = [pltpu.make_async_copy(srcs[k], dsts[k].at[q], local_sems.at[k]) for k in range(n)]
        for cp in mine:
            cp.start()
        started = []
        for k in range(n):
            own = srcs[k].at[pl.ds(c * half[k], half[k])]
            for j, chip in enumerate(chips):
                started.append(copy(k, j, q, c, (*chip, c), src=own))
                started[-1].start()
        for k in range(n):
            for j, chip in enumerate(chips):
                qq = 2 * chip[0] + chip[1]
                copy(k, j, qq, c, (x, y, c)).wait_recv()
                started.append(copy(k, 3 + j, qq, c, sibling))
                started[-1].start()
        for k in range(n):
            for j, chip in enumerate(chips):
                copy(k, 3 + j, 2 * chip[0] + chip[1], 1 - c, (x, y, c)).wait_recv()
        for cp in started:
            cp.wait_send()
        for cp in mine:
            cp.wait()

    return pl.pallas_call(
        body, name="gather_weights",
        out_shape=[jax.ShapeDtypeStruct((4,) + s.shape, BF16) for s in shards],
        in_specs=[pl.BlockSpec(memory_space=pltpu.VMEM)] * n, out_specs=[ANY] * n,
        scratch_shapes=[pltpu.VMEM(s.shape, BF16) for s in shards]
        + [pltpu.SemaphoreType.DMA((6 * n,)), pltpu.SemaphoreType.DMA((6 * n,)), pltpu.SemaphoreType.DMA((n,))],
        compiler_params=pltpu.CompilerParams(vmem_limit_bytes=VMEM_LIMIT),
    )(*shards)


def _swap_copies(srcs, dsts, send_sems, recv_sems):
    x, y, c = _place()
    return [pltpu.make_async_remote_copy(
        src_ref=src.at[:, 1 - c] if len(src.shape) == 4 else src, dst_ref=dst,
        send_sem=send_sems.at[k], recv_sem=recv_sems.at[k], device_id=(x, y, 1 - c), device_id_type=MESH)
        for k, (src, dst) in enumerate(zip(srcs, dsts))]


def _swap_shapes(grads):
    return [jax.ShapeDtypeStruct((4,) + g.shape[2:] if g.ndim == 4 else g.shape, g.dtype) for g in grads]


def _scatter_copies(srcs, dsts, send_sems, recv_sems):
    x, y, c = _place()
    copies = []
    for j, chip in enumerate([(1 - x, y), (x, 1 - y), (1 - x, 1 - y)]):
        for k, (src, dst) in enumerate(zip(srcs, dsts)):
            copies.append(pltpu.make_async_remote_copy(
                src_ref=src.at[2 * chip[0] + chip[1]] if len(src.shape) == 3 else src, dst_ref=dst.at[j],
                send_sem=send_sems.at[3 * k + j], recv_sem=recv_sems.at[3 * k + j],
                device_id=(*chip, c), device_id_type=MESH))
    return copies


def _scatter_shapes(parts):
    return [jax.ShapeDtypeStruct((3,) + (p.shape[1:] if p.ndim == 3 else p.shape), p.dtype) for p in parts]


def _join_halves_with_sibling(halves):
    n = len(halves)

    def body(*refs):
        srcs, dsts = refs[:n], refs[n:2 * n]
        send_sems, recv_sems = refs[2 * n:]
        x, y, c = _place()
        copies = [pltpu.make_async_remote_copy(
            src_ref=srcs[k], dst_ref=dsts[k], send_sem=send_sems.at[k], recv_sem=recv_sems.at[k],
            device_id=(x, y, 1 - c), device_id_type=MESH) for k in range(n)]
        for cp in copies:
            cp.start()
        for cp in copies:
            cp.wait()

    return pl.pallas_call(
        body, name="join_halves",
        out_shape=[jax.ShapeDtypeStruct(h.shape, h.dtype) for h in halves],
        in_specs=[ANY] * n, out_specs=[ANY] * n,
        scratch_shapes=[pltpu.SemaphoreType.DMA((n,)), pltpu.SemaphoreType.DMA((n,))],
    )(*halves)


def _row_block(rows, cols, n_arrays):
    cap = max(8, (VMEM_LIMIT // 4) // (8 * n_arrays * cols))
    rb = rows
    while rb > cap and rb % 2 == 0:
        rb //= 2
    return rb


def _add_own_half(grad, landed, c):
    _, _, hr, cols = grad.shape
    rb = _row_block(hr, cols, 3)

    def body(c_ref, g_ref, l_ref, o_ref):
        o_ref[...] = (g_ref[...] + l_ref[...]).astype(BF16)

    return pl.pallas_call(
        body, name="add_own_half",
        out_shape=jax.ShapeDtypeStruct(landed.shape, BF16),
        grid_spec=pltpu.PrefetchScalarGridSpec(
            num_scalar_prefetch=1, grid=(4, hr // rb),
            in_specs=[pl.BlockSpec((None, None, rb, cols), lambda qq, r, c_ref: (qq, c_ref[0], r, 0)),
                      pl.BlockSpec((None, rb, cols), lambda qq, r, c_ref: (qq, r, 0))],
            out_specs=pl.BlockSpec((None, rb, cols), lambda qq, r, c_ref: (qq, r, 0))),
    )(jnp.reshape(c, (1,)).astype(jnp.int32), grad, landed)


def _add_pairs(a, b, name):
    rows, cols = a.shape
    rb = _row_block(rows, cols, 3)

    def body(a_ref, b_ref, o_ref):
        o_ref[...] = a_ref[...] + b_ref[...]

    spec = pl.BlockSpec((rb, cols), lambda r: (r, 0))
    return pl.pallas_call(body, name=name, out_shape=jax.ShapeDtypeStruct(a.shape, F32), grid=(rows // rb,),
                          in_specs=[spec, spec], out_specs=spec)(a, b)


def _sum_four(own, slots, q, name):
    _, rows, cols = slots.shape
    rb = _row_block(rows, cols, 5)

    def body(q_ref, own_ref, s_ref, o_ref):
        o_ref[...] = ((own_ref[...].astype(F32) + s_ref[0].astype(F32))
                      + (s_ref[1].astype(F32) + s_ref[2].astype(F32)))

    if own.ndim == 3:
        own_spec = pl.BlockSpec((None, rb, cols), lambda r, q_ref: (q_ref[0], r, 0))
    else:
        own_spec = pl.BlockSpec((rb, cols), lambda r, q_ref: (r, 0))
    return pl.pallas_call(
        body, name=name, out_shape=jax.ShapeDtypeStruct((rows, cols), F32),
        grid_spec=pltpu.PrefetchScalarGridSpec(
            num_scalar_prefetch=1, grid=(rows // rb,),
            in_specs=[own_spec, pl.BlockSpec((3, rb, cols), lambda r, q_ref: (0, r, 0))],
            out_specs=pl.BlockSpec((rb, cols), lambda r, q_ref: (r, 0))),
    )(jnp.reshape(q, (1,)).astype(jnp.int32), own, slots)


def _adamw_math(w, g, m, v):
    nm = ADAM_B1 * m + (1.0 - ADAM_B1) * g
    nv = ADAM_B2 * v + (1.0 - ADAM_B2) * (g * g)
    m_hat = nm / (1.0 - ADAM_B1 ** ADAM_STEP)
    v_hat = nv / (1.0 - ADAM_B2 ** ADAM_STEP)
    return -ADAM_LR * (m_hat / (jnp.sqrt(v_hat) + ADAM_EPS) + ADAM_WD * w), nm, nv


def _adamw(w, g, m, v, name):
    rows, cols = w.shape
    rb = _row_block(rows, cols, 8)

    def body(w_ref, g_ref, m_ref, v_ref, go_ref, d_ref, nm_ref, nv_ref):
        go_ref[...] = g_ref[...]
        d_ref[...], nm_ref[...], nv_ref[...] = _adamw_math(w_ref[...], g_ref[...], m_ref[...], v_ref[...])

    spec = pl.BlockSpec((rb, cols), lambda r: (r, 0))
    out = jax.ShapeDtypeStruct(w.shape, F32)
    return pl.pallas_call(body, name=name, out_shape=[out] * 4, grid=(rows // rb,),
                          in_specs=[spec] * 4, out_specs=[spec] * 4)(w, g, m, v)


def _adamw_joined(w, g_mine, g_sibling, m, v, c, name):
    rows, cols = w.shape
    rb = _row_block(rows // 2, cols, 9)
    nb = rows // 2 // rb

    def body(c_ref, w_ref, gm_ref, gs_ref, m_ref, v_ref, g_ref, d_ref, nm_ref, nv_ref):
        g = jnp.where(c_ref[0] == pl.program_id(0), gm_ref[...], gs_ref[...])
        g_ref[...] = g
        d_ref[...], nm_ref[...], nv_ref[...] = _adamw_math(w_ref[...], g, m_ref[...], v_ref[...])

    full = pl.BlockSpec((rb, cols), lambda hf, r, c_ref: (hf * nb + r, 0))
    part = pl.BlockSpec((rb, cols), lambda hf, r, c_ref: (r, 0))
    out = jax.ShapeDtypeStruct(w.shape, F32)
    return pl.pallas_call(
        body, name=name, out_shape=[out] * 4,
        grid_spec=pltpu.PrefetchScalarGridSpec(
            num_scalar_prefetch=1, grid=(2, nb),
            in_specs=[full, part, part, full, full], out_specs=[full] * 4),
    )(jnp.reshape(c, (1,)).astype(jnp.int32), w, g_mine, g_sibling, m, v)


def _front_forward(x, w_in, pool_w, pool_scale, sgu_g, sgu_b, sgu_wm, sgu_bias_t, tm):
    T, D = x.shape
    nq, _, cq = w_in.shape
    G, PG = pool_w.shape[0], pool_w.shape[1]
    nt = T // tm
    bpd = D // PG

    def body(x_ref, win_any, pw_any, ps_ref, lg_ref, lb_ref, sw_ref, sb_ref, h_ref, y_ref, xt_ref,
             win_v, pw_v, carry, sems):
        i = pl.program_id(0)

        @pl.when(i == 0)
        def _():
            c1 = pltpu.make_async_copy(win_any, win_v, sems.at[0])
            c2 = pltpu.make_async_copy(pw_any, pw_v, sems.at[1])
            c1.start()
            c2.start()
            carry[...] = jnp.zeros_like(carry)
            c1.wait()
            c2.wait()

        xb = x_ref[...].astype(BF16)
        xt_ref[...] = x_ref[...].T.astype(BF16)

        def h_block(j):
            qq, off = divmod(j * PG, cq)
            blk = _mm(xb, win_v[qq, :, off:off + PG])
            h_ref[:, j * PG:(j + 1) * PG] = blk.astype(BF16)
            return blk

        for g, w in enumerate(POOL_WINDOWS):
            sl = slice(g * PG, (g + 1) * PG)
            a, z = h_block(g), h_block(3 * bpd + g)
            ext = jnp.concatenate([carry[:, sl], a], axis=0)
            carry[:, sl] = a[tm - HALO:, :]
            pooled = _causal_window_sum(ext, w) * _inv_count(i * tm, tm, w) - a
            mixed = _mm(pooled.astype(BF16), pw_v[g])
            y_ref[:, sl] = (mixed * ps_ref[:, sl] * (z * jax.nn.sigmoid(z))).astype(BF16)

        for hd in range(bpd):
            sl = slice(hd * PG, (hd + 1) * PG)
            u, v, z = h_block(bpd + hd), h_block(2 * bpd + hd), h_block(4 * bpd + hd)
            vhat, _ = _norm_rows(_gelu(v))
            vn = (vhat * lg_ref[:, sl] + lb_ref[:, sl]).astype(BF16)
            gated = _gelu(u) * (z * jax.nn.sigmoid(z))
            for n in range(tm // CHUNK):
                rs = slice(n * CHUNK, (n + 1) * CHUNK)
                sv = _mm(sw_ref[hd], vn[rs, :]) + sb_ref[:, hd:hd + 1]
                y_ref[rs, D + hd * PG:D + (hd + 1) * PG] = (gated[rs, :] * sv).astype(BF16)

    vec = pl.BlockSpec((1, D), lambda i: (0, 0))
    return pl.pallas_call(
        body, name="front_forward",
        out_shape=[jax.ShapeDtypeStruct((T, 5 * D), BF16), jax.ShapeDtypeStruct((T, 2 * D), BF16),
                   jax.ShapeDtypeStruct((D, T), BF16)],
        grid=(nt,),
        in_specs=[pl.BlockSpec((tm, D), lambda i: (i, 0)), ANY, ANY, vec, vec, vec,
                  pl.BlockSpec(sgu_wm.shape, lambda i: (0, 0, 0)), pl.BlockSpec(sgu_bias_t.shape, lambda i: (0, 0))],
        out_specs=[pl.BlockSpec((tm, 5 * D), lambda i: (i, 0)), pl.BlockSpec((tm, 2 * D), lambda i: (i, 0)),
                   pl.BlockSpec((D, tm), lambda i: (0, i))],
        scratch_shapes=[pltpu.VMEM(w_in.shape, BF16), pltpu.VMEM(pool_w.shape, BF16), pltpu.VMEM((HALO, D), F32),
                        pltpu.SemaphoreType.DMA((2,))],
        compiler_params=pltpu.CompilerParams(dimension_semantics=("arbitrary",), vmem_limit_bytes=VMEM_LIMIT),
    )(x, w_in, pool_w, pool_scale, sgu_g, sgu_b, sgu_wm, sgu_bias_t)


def _tail(y, x, p, target, w_out, w_gate, w_ple, ln_g, ln_b, gate_b, tm):
    T, D = x.shape
    K = p.shape[1]
    nq, _, cq = w_ple.shape
    nt = T // tm

    def body(y_ref, x_ref, p_ref, t_ref, wout_any, wg_any, wp_any, lng_ref, lnb_ref, bg_ref,
             dxp_ref, dy_ref, dwout_any, dwg_any, dwp_any, dlng_ref, dlnb_ref, dbg_ref, ssq_ref,
             wout_v, wg_v, wp_v, dwout_acc, dwg_acc, dwp_acc, sems):
        i = pl.program_id(0)

        @pl.when(i == 0)
        def _():
            loads = [pltpu.make_async_copy(s, d, sems.at[k])
                     for k, (s, d) in enumerate(((wout_any, wout_v), (wg_any, wg_v), (wp_any, wp_v)))]
            for cp in loads:
                cp.start()
            for ref in (dwout_acc, dwg_acc, dwp_acc, dlng_ref, dlnb_ref, dbg_ref, ssq_ref):
                ref[...] = jnp.zeros_like(ref)
            for cp in loads:
                cp.wait()

        yb = y_ref[...]
        pb = p_ref[...].astype(BF16)
        xhat, rstd = _norm_rows(DEEPNORM_ALPHA * x_ref[...] + _mm(yb, wout_v[...]))
        x1 = xhat * lng_ref[...] + lnb_ref[...]
        x1b = x1.astype(BF16)
        gate = jax.nn.sigmoid(_mm(x1b, wg_v[...]) + bg_ref[...])
        e = jnp.concatenate([_mm(pb, wp_v[qq]) for qq in range(nq)], axis=1)
        diff = x1 + gate * e - t_ref[...]
        ssq_ref[...] += jnp.sum(diff * diff, axis=0, keepdims=True)

        dout = diff * (1.0 / D)
        d_e = (dout * gate).astype(BF16)
        dgl = dout * e * gate * (1.0 - gate)
        dglb = dgl.astype(BF16)
        for qq in range(nq):
            dwp_acc[qq] += _mm_tn(pb, d_e[:, qq * cq:(qq + 1) * cq])
        for c0 in range(0, D, MXU_COLS):
            dwg_acc[:, c0:c0 + MXU_COLS] += _mm_tn(x1b, dglb[:, c0:c0 + MXU_COLS])
        dbg_ref[...] += jnp.sum(dgl, axis=0, keepdims=True)
        d_x1 = dout + _mm_nt(dglb, wg_v[...])
        dlng_ref[...] += jnp.sum(d_x1 * xhat, axis=0, keepdims=True)
        dlnb_ref[...] += jnp.sum(d_x1, axis=0, keepdims=True)
        d_r = _norm_rows_bwd(d_x1 * lng_ref[...], xhat, rstd)
        drb = d_r.astype(BF16)
        dxp_ref[...] = DEEPNORM_ALPHA * d_r
        for c0 in range(0, D, MXU_COLS):
            dwout_acc[:, c0:c0 + MXU_COLS] += _mm_tn(yb, drb[:, c0:c0 + MXU_COLS])
        for c0 in range(0, 2 * D, 2 * MXU_COLS):
            dy_ref[:, c0:c0 + 2 * MXU_COLS] = _mm_nt(drb, wout_v[c0:c0 + 2 * MXU_COLS, :]).astype(BF16)

        @pl.when(i == nt - 1)
        def _():
            stores = [pltpu.make_async_copy(s, d, sems.at[k])
                      for k, (s, d) in enumerate(((dwout_acc, dwout_any), (dwg_acc, dwg_any), (dwp_acc, dwp_any)))]
            for cp in stores:
                cp.start()
            for cp in stores:
                cp.wait()

    vec = pl.BlockSpec((1, D), lambda i: (0, 0))
    vec_shape = jax.ShapeDtypeStruct((1, D), F32)

    def tile(cols):
        return pl.BlockSpec((tm, cols), lambda i: (i, 0))

    return pl.pallas_call(
        body, name="tail",
        out_shape=[jax.ShapeDtypeStruct((T, D), F32), jax.ShapeDtypeStruct((T, 2 * D), BF16),
                   jax.ShapeDtypeStruct(w_out.shape, F32), jax.ShapeDtypeStruct(w_gate.shape, F32),
                   jax.ShapeDtypeStruct(w_ple.shape, F32), vec_shape, vec_shape, vec_shape, vec_shape],
        grid=(nt,),
        in_specs=[tile(2 * D), tile(D), tile(K), tile(D), ANY, ANY, ANY, vec, vec, vec],
        out_specs=[tile(D), tile(2 * D), ANY, ANY, ANY, vec, vec, vec, vec],
        scratch_shapes=[pltpu.VMEM(w_out.shape, BF16), pltpu.VMEM(w_gate.shape, BF16), pltpu.VMEM(w_ple.shape, BF16),
                        pltpu.VMEM(w_out.shape, F32), pltpu.VMEM(w_gate.shape, F32), pltpu.VMEM(w_ple.shape, F32),
                        pltpu.SemaphoreType.DMA((3,))],
        compiler_params=pltpu.CompilerParams(dimension_semantics=("arbitrary",), vmem_limit_bytes=VMEM_LIMIT),
    )(y, x, p, target, w_out, w_gate, w_ple, ln_g, ln_b, gate_b)


def _front_backward(h, d_y, pool_w, pool_scale, sgu_g, sgu_b, sgu_wm, sgu_bias_t, tm):
    T = h.shape[0]
    D = h.shape[1] // 5
    G, PG = pool_w.shape[0], pool_w.shape[1]
    nt = T // tm
    hpt = tm // HALO

    def body(h_ref, halo_ref, dy_ref, pw_ref, ps_ref, lg_ref, lb_ref, sw_ref, sb_ref,
             dh_ref, dpw_ref, dps_ref, dlg_ref, dlb_ref, dsw_ref, dsb_ref, carry):
        i = pl.program_id(0)
        ti = nt - 1 - i

        @pl.when(i == 0)
        def _():
            carry[...] = jnp.zeros_like(carry)
            for ref in (dpw_ref, dps_ref, dlg_ref, dlb_ref, dsw_ref, dsb_ref):
                ref[...] = jnp.zeros_like(ref)

        a = h_ref[:, 0:D].astype(F32)
        before = jnp.where(ti > 0, halo_ref[...].astype(F32), 0.0)
        ext = jnp.concatenate([before, a], axis=0)
        for g, w in enumerate(POOL_WINDOWS):
            sl = slice(g * PG, (g + 1) * PG)
            inv = _inv_count(ti * tm, tm, w)
            pooled = (_causal_window_sum(ext[:, sl], w) * inv - a[:, sl]).astype(BF16)
            mixed = _mm(pooled, pw_ref[g])
            z = h_ref[:, 3 * D + g * PG:3 * D + (g + 1) * PG].astype(F32)
            sig = jax.nn.sigmoid(z)
            dy = dy_ref[:, sl].astype(F32)
            d_ypool = dy * (z * sig)
            dh_ref[:, 3 * D + g * PG:3 * D + (g + 1) * PG] = (
                dy * (mixed * ps_ref[:, sl]) * (sig * (1.0 + z * (1.0 - sig)))).astype(BF16)
            dps_ref[:, sl] += jnp.sum(d_ypool * mixed, axis=0, keepdims=True)
            d_mixed = (d_ypool * ps_ref[:, sl]).astype(BF16)
            dpw_ref[g] += _mm_tn(pooled, d_mixed)
            d_pooled = _mm_nt(d_mixed, pw_ref[g])
            scaled = d_pooled * inv
            after = jnp.concatenate([scaled, carry[:, sl]], axis=0)
            carry[:, sl] = scaled[:HALO, :]
            dh_ref[:, sl] = (_anticausal_window_sum(after, w) - d_pooled).astype(BF16)

        for hd in range(D // PG):
            sl = slice(hd * PG, (hd + 1) * PG)
            vg, dvg = _gelu_and_grad(h_ref[:, 2 * D + hd * PG:2 * D + (hd + 1) * PG].astype(F32))
            vhat, rstd = _norm_rows(vg)
            vn = (vhat * lg_ref[:, sl] + lb_ref[:, sl]).astype(BF16)
            d_vn_chunks = []
            for n in range(tm // CHUNK):
                rs = slice(n * CHUNK, (n + 1) * CHUNK)
                sv = _mm(sw_ref[hd], vn[rs, :]) + sb_ref[:, hd:hd + 1]
                ug, dug = _gelu_and_grad(h_ref[rs, D + hd * PG:D + (hd + 1) * PG].astype(F32))
                z = h_ref[rs, 4 * D + hd * PG:4 * D + (hd + 1) * PG].astype(F32)
                sig = jax.nn.sigmoid(z)
                dy = dy_ref[rs, D + hd * PG:D + (hd + 1) * PG].astype(F32)
                d_ysgu = dy * (z * sig)
                dh_ref[rs, 4 * D + hd * PG:4 * D + (hd + 1) * PG] = (
                    dy * (ug * sv) * (sig * (1.0 + z * (1.0 - sig)))).astype(BF16)
                dh_ref[rs, D + hd * PG:D + (hd + 1) * PG] = (d_ysgu * sv * dug).astype(BF16)
                d_sv = d_ysgu * ug
                dsb_ref[:, hd:hd + 1] += jnp.sum(d_sv, axis=1, keepdims=True)
                d_svb = d_sv.astype(BF16)
                dsw_ref[hd] += _mm_nt(d_svb, vn[rs, :])
                d_vn_chunks.append(_mm_tn(sw_ref[hd], d_svb))
            d_vn = jnp.concatenate(d_vn_chunks, axis=0)
            dlg_ref[:, sl] += jnp.sum(d_vn * vhat, axis=0, keepdims=True)
            dlb_ref[:, sl] += jnp.sum(d_vn, axis=0, keepdims=True)
            d_vg = _norm_rows_bwd(d_vn * lg_ref[:, sl], vhat, rstd)
            dh_ref[:, 2 * D + hd * PG:2 * D + (hd + 1) * PG] = (d_vg * dvg).astype(BF16)

    vec = pl.BlockSpec((1, D), lambda i: (0, 0))
    vec_shape = jax.ShapeDtypeStruct((1, D), F32)

    def whole(shape):
        return pl.BlockSpec(shape, lambda i: (0,) * len(shape))

    return pl.pallas_call(
        body, name="front_backward",
        out_shape=[jax.ShapeDtypeStruct((T, 5 * D), BF16), jax.ShapeDtypeStruct(pool_w.shape, F32), vec_shape, vec_shape,
                   vec_shape, jax.ShapeDtypeStruct(sgu_wm.shape, F32), jax.ShapeDtypeStruct(sgu_bias_t.shape, F32)],
        grid=(nt,),
        in_specs=[pl.BlockSpec((tm, 5 * D), lambda i: (nt - 1 - i, 0)),
                  pl.BlockSpec((HALO, D), lambda i: (jnp.maximum((nt - 1 - i) * hpt - 1, 0), 0)),
                  pl.BlockSpec((tm, 2 * D), lambda i: (nt - 1 - i, 0)),
                  whole(pool_w.shape), vec, vec, vec, whole(sgu_wm.shape), whole(sgu_bias_t.shape)],
        out_specs=[pl.BlockSpec((tm, 5 * D), lambda i: (nt - 1 - i, 0)), whole(pool_w.shape), vec, vec, vec,
                   whole(sgu_wm.shape), whole(sgu_bias_t.shape)],
        scratch_shapes=[pltpu.VMEM((HALO, D), F32)],
        compiler_params=pltpu.CompilerParams(dimension_semantics=("arbitrary",), vmem_limit_bytes=VMEM_LIMIT),
    )(h, h, d_y, pool_w, pool_scale, sgu_g, sgu_b, sgu_wm, sgu_bias_t)


def _weight_backward(d_h, xt, q, swap_srcs, tm):
    D, T = xt.shape
    cq = d_h.shape[1] // 4
    hr = D // 2
    nt = T // tm
    ns = len(swap_srcs)

    def body(q_ref, dh_ref, xt_ref, *refs):
        srcs, out_any, dsts = refs[:ns], refs[ns], refs[ns + 1:2 * ns + 1]
        (acc, land_a, send_b, land_b, mine_f, theirs_f,
         a_send, a_recv, b_send, b_recv, j_sems, o_sems, s_send, s_recv) = refs[2 * ns + 1:]
        s, t = pl.program_id(0), pl.program_id(1)
        x_, y_, c = _place()
        sibling = (x_, y_, 1 - c)
        own_rows = pl.ds(pl.multiple_of(c * hr, hr), hr)
        other_rows = pl.ds(pl.multiple_of((1 - c) * hr, hr), hr)

        @pl.when((s == 0) & (t == 0))
        def _():
            for cp in _swap_copies(srcs, dsts, s_send, s_recv):
                cp.start()

        @pl.when(t == 0)
        def _():
            acc[...] = jnp.zeros_like(acc)

        for c0 in range(0, cq, MXU_COLS):
            acc[:, c0:c0 + MXU_COLS] += _mm(xt_ref[...], dh_ref[:, c0:c0 + MXU_COLS])

        def pair_sum(slot):
            swap = pltpu.make_async_remote_copy(
                src_ref=acc.at[other_rows], dst_ref=land_a.at[slot], send_sem=a_send.at[slot],
                recv_sem=a_recv.at[slot], device_id=sibling, device_id_type=MESH)
            swap.start()
            swap.wait()
            return acc[own_rows, :] + land_a[slot]

        def to_owner(slot):
            flip_x, flip_y = (slot + 1) >> 1, (slot + 1) & 1
            owner = (1 - x_ if flip_x else x_, 1 - y_ if flip_y else y_, c)
            return pltpu.make_async_remote_copy(
                src_ref=send_b.at[slot], dst_ref=land_b.at[slot], send_sem=b_send.at[slot],
                recv_sem=b_recv.at[slot], device_id=owner, device_id_type=MESH)

        for slot in range(3):
            @pl.when((s == slot) & (t == nt - 1))
            def _(slot=slot):
                send_b[slot] = pair_sum(slot).astype(BF16)
                to_owner(slot).start()

        @pl.when((s == 3) & (t == nt - 1))
        def _():
            own = pair_sum(3)
            for slot in range(3):
                to_owner(slot).wait_recv()
            mine_f[...] = (own + land_b[0].astype(F32)) + (land_b[1].astype(F32) + land_b[2].astype(F32))
            join = pltpu.make_async_remote_copy(
                src_ref=mine_f, dst_ref=theirs_f, send_sem=j_sems.at[0], recv_sem=j_sems.at[1],
                device_id=sibling, device_id_type=MESH)
            join.start()
            out_mine = pltpu.make_async_copy(mine_f, out_any.at[own_rows], o_sems.at[0])
            out_mine.start()
            join.wait()
            out_theirs = pltpu.make_async_copy(theirs_f, out_any.at[other_rows], o_sems.at[1])
            out_theirs.start()
            for slot in range(3):
                to_owner(slot).wait_send()
            for cp in _swap_copies(srcs, dsts, s_send, s_recv):
                cp.wait()
            out_mine.wait()
            out_theirs.wait()

    def quarter(s, t, q_ref):
        return (t, jnp.where(s == 3, q_ref[0], q_ref[0] ^ (s + 1)))

    dma = pltpu.SemaphoreType.DMA
    return pl.pallas_call(
        body, name="weight_backward",
        out_shape=[jax.ShapeDtypeStruct((D, cq), F32)] + _swap_shapes(swap_srcs),
        grid_spec=pltpu.PrefetchScalarGridSpec(
            num_scalar_prefetch=1, grid=(4, nt),
            in_specs=[pl.BlockSpec((tm, cq), quarter), pl.BlockSpec((D, tm), lambda s, t, q_ref: (0, t))] + [ANY] * ns,
            out_specs=[ANY] * (ns + 1),
            scratch_shapes=[pltpu.VMEM((D, cq), F32), pltpu.VMEM((4, hr, cq), F32), pltpu.VMEM((3, hr, cq), BF16),
                            pltpu.VMEM((3, hr, cq), BF16), pltpu.VMEM((hr, cq), F32), pltpu.VMEM((hr, cq), F32),
                            dma((4,)), dma((4,)), dma((3,)), dma((3,)), dma((2,)), dma((2,)), dma((ns,)), dma((ns,))]),
        compiler_params=pltpu.CompilerParams(dimension_semantics=("arbitrary", "arbitrary"),
                                             vmem_limit_bytes=VMEM_LIMIT),
    )(jnp.reshape(q, (1,)).astype(jnp.int32), d_h, xt, *swap_srcs)


def _input_backward(d_h, dx_part, w_in, scatter_srcs, tm):
    T, D = dx_part.shape
    nq, _, cq = w_in.shape
    nt = T // tm
    ns = len(scatter_srcs)

    def body(dh_ref, dxp_ref, win_any, *refs):
        srcs, dx_ref, dsts = refs[:ns], refs[ns], refs[ns + 1:2 * ns + 1]
        win_v, sems, s_send, s_recv = refs[2 * ns + 1:]
        i = pl.program_id(0)

        @pl.when(i == 0)
        def _():
            for cp in _scatter_copies(srcs, dsts, s_send, s_recv):
                cp.start()
            cp = pltpu.make_async_copy(win_any, win_v, sems.at[0])
            cp.start()
            cp.wait()

        dx = dxp_ref[...]
        for qq in range(nq):
            dx = dx + _mm_nt(dh_ref[:, qq * cq:(qq + 1) * cq], win_v[qq])
        dx_ref[...] = dx

        @pl.when(i == nt - 1)
        def _():
            for cp in _scatter_copies(srcs, dsts, s_send, s_recv):
                cp.wait()

    return pl.pallas_call(
        body, name="input_backward",
        out_shape=[jax.ShapeDtypeStruct((T, D), F32)] + _scatter_shapes(scatter_srcs),
        grid=(nt,),
        in_specs=[pl.BlockSpec((tm, 5 * D), lambda i: (i, 0)), pl.BlockSpec((tm, D), lambda i: (i, 0)), ANY] + [ANY] * ns,
        out_specs=[pl.BlockSpec((tm, D), lambda i: (i, 0))] + [ANY] * ns,
        scratch_shapes=[pltpu.VMEM(w_in.shape, BF16), pltpu.SemaphoreType.DMA((1,)),
                        pltpu.SemaphoreType.DMA((3 * ns,)), pltpu.SemaphoreType.DMA((3 * ns,))],
        compiler_params=pltpu.CompilerParams(dimension_semantics=("arbitrary",), vmem_limit_bytes=VMEM_LIMIT),
    )(d_h, dx_part, w_in, *scatter_srcs)


def _token_tile(T, want):
    return math.gcd(T, want)


def kernel(x, p, w_in, pool_w, pool_scale, sgu_ln_g, sgu_ln_b, sgu_w, sgu_b, w_out, ln_g, ln_b, ple_w, ple_gate_w, ple_gate_b, loss_target, m_w_in, m_pool_w, m_pool_scale, m_sgu_ln_g, m_sgu_ln_b, m_sgu_w, m_sgu_b, m_w_out, m_ln_g, m_ln_b, m_ple_w, m_ple_gate_w, m_ple_gate_b, v_w_in, v_pool_w, v_pool_scale, v_sgu_ln_g, v_sgu_ln_b, v_sgu_w, v_sgu_b, v_w_out, v_ln_g, v_ln_b, v_ple_w, v_ple_gate_w, v_ple_gate_b):
    c = lax.axis_index("c")
    T, D = x.shape[1], x.shape[2]
    tm, tm_vpu, tm_acc = _token_tile(T, 512), _token_tile(T, 256), _token_tile(T, 2048)
    x2, p2, tgt = x[0], p[0, 0], loss_target[0]
    G, PGQ, PG = pool_w.shape[1], pool_w.shape[2], pool_w.shape[3]

    shards = [w_in[0], w_out[0], ple_gate_w[0], ple_w[0], pool_w[0].reshape(G * PGQ, PG)]
    w_in_f, w_out_f, w_gate_f, w_ple_f, pool_f = _gather_weights(shards)
    w_out_f = w_out_f.reshape(-1, D)
    w_gate_f = w_gate_f.reshape(-1, D)
    pool_f = pool_f.reshape(4, G, PGQ, PG).transpose(1, 0, 2, 3).reshape(G, 4 * PGQ, PG)
    tril = jnp.tril(jnp.ones((CHUNK, CHUNK), dtype=bool))
    sgu_wm = jnp.where(tril[None], sgu_w[0], 0.0).astype(BF16)
    sgu_bias_t = sgu_b[0].T

    h, y, xt = _front_forward(x2, w_in_f, pool_f, pool_scale, sgu_ln_g, sgu_ln_b, sgu_wm, sgu_bias_t, tm)
    (dx_part, d_y, d_w_out, d_w_gate, d_w_ple, d_ln_g, d_ln_b, d_gate_b, ssq) = _tail(
        y, x2, p2, tgt, w_out_f, w_gate_f, w_ple_f, ln_g, ln_b, ple_gate_b, tm)
    d_h, d_pool_w, d_pool_scale, d_sgu_g, d_sgu_b, d_sgu_w, d_sgu_bias_t = _front_backward(
        h, d_y, pool_f, pool_scale, sgu_ln_g, sgu_ln_b, sgu_wm, sgu_bias_t, tm_vpu)
    loss = lax.psum((0.5 / D) * jnp.sum(ssq), ("x", "y", "c"))

    grads = [d_w_out.reshape(4, -1, D), d_w_gate.reshape(4, -1, D), d_w_ple,
             d_pool_w.reshape(G, 4, PGQ, PG).transpose(1, 0, 2, 3).reshape(4, G * PGQ, PG)]
    grads = [g.reshape(4, 2, g.shape[1] // 2, g.shape[2]) for g in grads]
    d_sgu_w = jnp.where(tril[None], d_sgu_w, 0.0)
    small_names = ["pool_scale", "sgu_ln_g", "sgu_ln_b", "ln_g", "ln_b", "ple_gate_b", "sgu_b", "sgu_w"]
    small_grads = [d_pool_scale, d_sgu_g, d_sgu_b, d_ln_g, d_ln_b, d_gate_b, d_sgu_bias_t.T, d_sgu_w]

    def pack(arrays):
        rows = [a.reshape(-1) for a in arrays[:6]] + [jnp.pad(arrays[6].reshape(-1), (0, 2 * D - arrays[6].size))]
        return jnp.concatenate([r.reshape(-1, D) for r in rows] + [arrays[7].reshape(-1, D)], axis=0)

    def unpack(packed, like):
        out = [packed[k].reshape(like[k].shape) for k in range(6)]
        out.append(packed[6, :like[6].size].reshape(like[6].shape))
        out.append(packed[8:].reshape(like[7].shape))
        return out

    small = pack(small_grads)
    q = 2 * lax.axis_index("x") + lax.axis_index("y")
    d_w_in, *landed, small_landed = _weight_backward(d_h, xt, q, grads + [small], tm_acc)
    parts = [_add_own_half(g, l, c) for g, l in zip(grads, landed)]
    small_chip = _add_pairs(small, small_landed, "add_small")
    d_x, *slots, small_slots = _input_backward(d_h, dx_part, w_in_f, parts + [small_chip], tm)
    halves = [_sum_four(pt, s, q, "sum_four_%d" % k) for k, (pt, s) in enumerate(zip(parts, slots))]
    small_total = _sum_four(small_chip, small_slots, q, "sum_four_small")
    sibling_halves = _join_halves_with_sibling(halves)

    big_names = ["w_out", "ple_gate_w", "ple_w", "pool_w"]
    given = dict(w_in=(w_in, m_w_in, v_w_in), w_out=(w_out, m_w_out, v_w_out),
                 ple_gate_w=(ple_gate_w, m_ple_gate_w, v_ple_gate_w), ple_w=(ple_w, m_ple_w, v_ple_w),
                 pool_w=(pool_w, m_pool_w, v_pool_w), pool_scale=(pool_scale, m_pool_scale, v_pool_scale),
                 sgu_ln_g=(sgu_ln_g, m_sgu_ln_g, v_sgu_ln_g), sgu_ln_b=(sgu_ln_b, m_sgu_ln_b, v_sgu_ln_b),
                 sgu_w=(sgu_w, m_sgu_w, v_sgu_w), sgu_b=(sgu_b, m_sgu_b, v_sgu_b), ln_g=(ln_g, m_ln_g, v_ln_g),
                 ln_b=(ln_b, m_ln_b, v_ln_b), ple_gate_b=(ple_gate_b, m_ple_gate_b, v_ple_gate_b))
    grad, delta, new_m, new_v = {}, {}, {}, {}
    for name, g_mine, g_sibling in zip(big_names, halves, sibling_halves):
        w, m, v = given[name]
        flat = (2 * g_mine.shape[0], g_mine.shape[1])
        outs = _adamw_joined(w.reshape(flat), g_mine, g_sibling, m.reshape(flat), v.reshape(flat), c, "adamw_" + name)
        grad[name], delta[name], new_m[name], new_v[name] = (t.reshape(w.shape) for t in outs)
    grad["w_in"], delta["w_in"], new_m["w_in"], new_v["w_in"] = (
        t[None] for t in _adamw(w_in[0], d_w_in, m_w_in[0], v_w_in[0], "adamw_w_in"))
    small_w, small_m, small_v = (pack([given[n][k] for n in small_names]) for k in range(3))
    small_out = _adamw(small_w, small_total, small_m, small_v, "adamw_small")
    like = [given[n][0] for n in small_names]
    for k, name in enumerate(small_names):
        grad[name], delta[name], new_m[name], new_v[name] = (unpack(t, like)[k] for t in small_out)

    order = ["w_in", "pool_w", "pool_scale", "sgu_ln_g", "sgu_ln_b", "sgu_w", "sgu_b", "w_out", "ln_g", "ln_b",
             "ple_w", "ple_gate_w", "ple_gate_b"]
    return (loss, d_x[None], *[grad[n] for n in order], *[delta[n] for n in order],
            *[new_m[n] for n in order], *[new_v[n] for n in order])
```

```python
import functools
import math

import jax
import jax.numpy as jnp
from jax import lax
from jax.experimental import pallas as pl
from jax.experimental.pallas import tpu as pltpu

F32, BF16 = jnp.float32, jnp.bfloat16
MESH = pl.DeviceIdType.MESH
ANY = pl.BlockSpec(memory_space=pl.ANY)

POOL_WINDOWS = (2, 4, 8, 16)
HALO = 16
CHUNK = 128
MXU_COLS = 256
LN_EPS = 1e-5
DEEPNORM_ALPHA = 2.0 ** 0.25
ADAM_LR, ADAM_B1, ADAM_B2, ADAM_EPS, ADAM_WD, ADAM_STEP = 1e-3, 0.9, 0.999, 1e-8, 0.01, 10
VMEM_LIMIT = 56 * 1024 * 1024
GELU_K = math.sqrt(2.0 / math.pi)
GELU_C = 0.044715


def _mm(a, b):
    return jnp.dot(a, b, preferred_element_type=F32)


def _mm_nt(a, b):
    return lax.dot_general(a, b, (((1,), (1,)), ((), ())), preferred_element_type=F32)


def _mm_tn(a, b):
    return lax.dot_general(a, b, (((0,), (0,)), ((), ())), preferred_element_type=F32)


def _gelu(x):
    t = jnp.tanh(GELU_K * (x + GELU_C * x * x * x))
    return 0.5 * x * (1.0 + t)


def _gelu_and_grad(x):
    x2 = x * x
    t = jnp.tanh(GELU_K * (x + GELU_C * x2 * x))
    g = 0.5 * x * (1.0 + t)
    dg = 0.5 * (1.0 + t) + 0.5 * x * (1.0 - t * t) * (GELU_K * (1.0 + 3.0 * GELU_C * x2))
    return g, dg


def _norm_rows(x):
    mu = jnp.mean(x, axis=-1, keepdims=True)
    xc = x - mu
    var = jnp.mean(xc * xc, axis=-1, keepdims=True)
    rstd = lax.rsqrt(var + LN_EPS)
    return xc * rstd, rstd


def _norm_rows_bwd(dxhat, xhat, rstd):
    m1 = jnp.mean(dxhat, axis=-1, keepdims=True)
    m2 = jnp.mean(dxhat * xhat, axis=-1, keepdims=True)
    return rstd * (dxhat - m1 - xhat * m2)


def _inv_count(row0, rows, w):
    t = row0 + lax.broadcasted_iota(jnp.int32, (rows, 1), 0)
    return 1.0 / jnp.minimum(t + 1, w).astype(F32)


def _causal_window_sum(ext, w):
    s, sh = ext, 1
    while sh < w:
        s = s + pltpu.roll(s, sh, axis=0)
        sh *= 2
    return s[HALO:, :]


def _anticausal_window_sum(ext, w):
    n, s, sh = ext.shape[0], ext, 1
    while sh < w:
        s = s + pltpu.roll(s, n - sh, axis=0)
        sh *= 2
    return s[: n - HALO, :]


def _place():
    return lax.axis_index("x"), lax.axis_index("y"), lax.axis_index("c")


def _gather_weights(shards):
    n = len(shards)
    half = [s.shape[0] // 2 for s in shards]

    def body(*refs):
        wide, dsts, srcs = refs[:n], refs[n:2 * n], refs[2 * n:3 * n]
        send_sems, recv_sems, local_sems = refs[3 * n:]
        for k in range(n):
            for r0 in range(0, 2 * half[k], CHUNK):
                srcs[k][r0:r0 + CHUNK, :] = wide[k][r0:r0 + CHUNK, :].astype(BF16)
        x, y, c = _place()
        q = 2 * x + y
        sibling = (x, y, 1 - c)
        chips = [(1 - x, y), (x, 1 - y), (1 - x, 1 - y)]

        def rows(k, qq, cc):
            return dsts[k].at[qq, pl.ds(cc * half[k], half[k])]

        def copy(k, sem, qq, cc, to, src=None):
            return pltpu.make_async_remote_copy(
                src_ref=rows(k, qq, cc) if src is None else src, dst_ref=rows(k, qq, cc),
                send_sem=send_sems.at[6 * k + sem], recv_sem=recv_sems.at[6 * k + sem],
                device_id=to, device_id_type=MESH)

        mine = [pltpu.make_async_copy(srcs[k], dsts[k].at[q], local_sems.at[k]) for k in range(n)]
        for cp in mine:
            cp.start()
        started = []
        for k in range(n):
            own = srcs[k].at[pl.ds(c * half[k], half[k])]
            for j, chip in enumerate(chips):
                started.append(copy(k, j, q, c, (*chip, c), src=own))
                started[-1].start()
        for k in range(n):
            for j, chip in enumerate(chips):
                qq = 2 * chip[0] + chip[1]
                copy(k, j, qq, c, (x, y, c)).wait_recv()
                started.append(copy(k, 3 + j, qq, c, sibling))
                started[-1].start()
        for k in range(n):
            for j, chip in enumerate(chips):
                copy(k, 3 + j, 2 * chip[0] + chip[1], 1 - c, (x, y, c)).wait_recv()
        for cp in started:
            cp.wait_send()
        for cp in mine:
            cp.wait()

    return pl.pallas_call(
        body, name="gather_weights",
        out_shape=[jax.ShapeDtypeStruct((4,) + s.shape, BF16) for s in shards],
        in_specs=[pl.BlockSpec(memory_space=pltpu.VMEM)] * n, out_specs=[ANY] * n,
        scratch_shapes=[pltpu.VMEM(s.shape, BF16) for s in shards]
        + [pltpu.SemaphoreType.DMA((6 * n,)), pltpu.SemaphoreType.DMA((6 * n,)), pltpu.SemaphoreType.DMA((n,))],
        compiler_params=pltpu.CompilerParams(vmem_limit_bytes=VMEM_LIMIT),
    )(*shards)


def _swap_copies(srcs, dsts, send_sems, recv_sems):
    x, y, c = _place()
    return [pltpu.make_async_remote_copy(
        src_ref=src.at[:, 1 - c] if len(src.shape) == 4 else src, dst_ref=dst,
        send_sem=send_sems.at[k], recv_sem=recv_sems.at[k], device_id=(x, y, 1 - c), device_id_type=MESH)
        for k, (src, dst) in enumerate(zip(srcs, dsts))]


def _swap_shapes(grads):
    return [jax.ShapeDtypeStruct((4,) + g.shape[2:] if g.ndim == 4 else g.shape, g.dtype) for g in grads]


def _scatter_copies(srcs, dsts, send_sems, recv_sems):
    x, y, c = _place()
    copies = []
    for j, chip in enumerate([(1 - x, y), (x, 1 - y), (1 - x, 1 - y)]):
        for k, (src, dst) in enumerate(zip(srcs, dsts)):
            copies.append(pltpu.make_async_remote_copy(
                src_ref=src.at[2 * chip[0] + chip[1]] if len(src.shape) == 3 else src, dst_ref=dst.at[j],
                send_sem=send_sems.at[3 * k + j], recv_sem=recv_sems.at[3 * k + j],
                device_id=(*chip, c), device_id_type=MESH))
    return copies


def _scatter_shapes(parts):
    return [jax.ShapeDtypeStruct((3,) + (p.shape[1:] if p.ndim == 3 else p.shape), p.dtype) for p in parts]


def _join_halves_with_sibling(halves):
    n = len(halves)

    def body(*refs):
        srcs, dsts = refs[:n], refs[n:2 * n]
        send_sems, recv_sems = refs[2 * n:]
        x, y, c = _place()
        copies = [pltpu.make_async_remote_copy(
            src_ref=srcs[k], dst_ref=dsts[k], send_sem=send_sems.at[k], recv_sem=recv_sems.at[k],
            device_id=(x, y, 1 - c), device_id_type=MESH) for k in range(n)]
        for cp in copies:
            cp.start()
        for cp in copies:
            cp.wait()

    return pl.pallas_call(
        body, name="join_halves",
        out_shape=[jax.ShapeDtypeStruct(h.shape, h.dtype) for h in halves],
        in_specs=[ANY] * n, out_specs=[ANY] * n,
        scratch_shapes=[pltpu.SemaphoreType.DMA((n,)), pltpu.SemaphoreType.DMA((n,))],
    )(*halves)


def _row_block(rows, cols, n_arrays):
    cap = max(8, (VMEM_LIMIT // 4) // (8 * n_arrays * cols))
    rb = rows
    while rb > cap and rb % 2 == 0:
        rb //= 2
    return rb


def _add_own_half(grad, landed, c):
    _, _, hr, cols = grad.shape
    rb = _row_block(hr, cols, 3)

    def body(c_ref, g_ref, l_ref, o_ref):
        o_ref[...] = (g_ref[...] + l_ref[...]).astype(BF16)

    return pl.pallas_call(
        body, name="add_own_half",
        out_shape=jax.ShapeDtypeStruct(landed.shape, BF16),
        grid_spec=pltpu.PrefetchScalarGridSpec(
            num_scalar_prefetch=1, grid=(4, hr // rb),
            in_specs=[pl.BlockSpec((None, None, rb, cols), lambda qq, r, c_ref: (qq, c_ref[0], r, 0)),
                      pl.BlockSpec((None, rb, cols), lambda qq, r, c_ref: (qq, r, 0))],
            out_specs=pl.BlockSpec((None, rb, cols), lambda qq, r, c_ref: (qq, r, 0))),
    )(jnp.reshape(c, (1,)).astype(jnp.int32), grad, landed)


def _add_pairs(a, b, name):
    rows, cols = a.shape
    rb = _row_block(rows, cols, 3)

    def body(a_ref, b_ref, o_ref):
        o_ref[...] = a_ref[...] + b_ref[...]

    spec = pl.BlockSpec((rb, cols), lambda r: (r, 0))
    return pl.pallas_call(body, name=name, out_shape=jax.ShapeDtypeStruct(a.shape, F32), grid=(rows // rb,),
                          in_specs=[spec, spec], out_specs=spec)(a, b)


def _sum_four(own, slots, q, name):
    _, rows, cols = slots.shape
    rb = _row_block(rows, cols, 5)

    def body(q_ref, own_ref, s_ref, o_ref):
        o_ref[...] = ((own_ref[...].astype(F32) + s_ref[0].astype(F32))
                      + (s_ref[1].astype(F32) + s_ref[2].astype(F32)))

    if own.ndim == 3:
        own_spec = pl.BlockSpec((None, rb, cols), lambda r, q_ref: (q_ref[0], r, 0))
    else:
        own_spec = pl.BlockSpec((rb, cols), lambda r, q_ref: (r, 0))
    return pl.pallas_call(
        body, name=name, out_shape=jax.ShapeDtypeStruct((rows, cols), F32),
        grid_spec=pltpu.PrefetchScalarGridSpec(
            num_scalar_prefetch=1, grid=(rows // rb,),
            in_specs=[own_spec, pl.BlockSpec((3, rb, cols), lambda r, q_ref: (0, r, 0))],
            out_specs=pl.BlockSpec((rb, cols), lambda r, q_ref: (r, 0))),
    )(jnp.reshape(q, (1,)).astype(jnp.int32), own, slots)


def _adamw_math(w, g, m, v):
    nm = ADAM_B1 * m + (1.0 - ADAM_B1) * g
    nv = ADAM_B2 * v + (1.0 - ADAM_B2) * (g * g)
    m_hat = nm / (1.0 - ADAM_B1 ** ADAM_STEP)
    v_hat = nv / (1.0 - ADAM_B2 ** ADAM_STEP)
    return -ADAM_LR * (m_hat / (jnp.sqrt(v_hat) + ADAM_EPS) + ADAM_WD * w), nm, nv


def _adamw(w, g, m, v, name):
    rows, cols = w.shape
    rb = _row_block(rows, cols, 8)

    def body(w_ref, g_ref, m_ref, v_ref, go_ref, d_ref, nm_ref, nv_ref):
        go_ref[...] = g_ref[...]
        d_ref[...], nm_ref[...], nv_ref[...] = _adamw_math(w_ref[...], g_ref[...], m_ref[...], v_ref[...])

    spec = pl.BlockSpec((rb, cols), lambda r: (r, 0))
    out = jax.ShapeDtypeStruct(w.shape, F32)
    return pl.pallas_call(body, name=name, out_shape=[out] * 4, grid=(rows // rb,),
                          in_specs=[spec] * 4, out_specs=[spec] * 4)(w, g, m, v)


def _adamw_joined(w, g_mine, g_sibling, m, v, c, name):
    rows, cols = w.shape
    rb = _row_block(rows // 2, cols, 9)
    nb = rows // 2 // rb

    def body(c_ref, w_ref, gm_ref, gs_ref, m_ref, v_ref, g_ref, d_ref, nm_ref, nv_ref):
        g = jnp.where(c_ref[0] == pl.program_id(0), gm_ref[...], gs_ref[...])
        g_ref[...] = g
        d_ref[...], nm_ref[...], nv_ref[...] = _adamw_math(w_ref[...], g, m_ref[...], v_ref[...])

    full = pl.BlockSpec((rb, cols), lambda hf, r, c_ref: (hf * nb + r, 0))
    part = pl.BlockSpec((rb, cols), lambda hf, r, c_ref: (r, 0))
    out = jax.ShapeDtypeStruct(w.shape, F32)
    return pl.pallas_call(
        body, name=name, out_shape=[out] * 4,
        grid_spec=pltpu.PrefetchScalarGridSpec(
            num_scalar_prefetch=1, grid=(2, nb),
            in_specs=[full, part, part, full, full], out_specs=[full] * 4),
    )(jnp.reshape(c, (1,)).astype(jnp.int32), w, g_mine, g_sibling, m, v)


def _front_forward(x, w_in, pool_w, pool_scale, sgu_g, sgu_b, sgu_wm, sgu_bias_t, tm):
    T, D = x.shape
    nq, _, cq = w_in.shape
    G, PG = pool_w.shape[0], pool_w.shape[1]
    nt = T // tm
    bpd = D // PG

    def body(x_ref, win_any, pw_any, ps_ref, lg_ref, lb_ref, sw_ref, sb_ref, h_ref, y_ref, xt_ref,
             win_v, pw_v, carry, sems):
        i = pl.program_id(0)

        @pl.when(i == 0)
        def _():
            c1 = pltpu.make_async_copy(win_any, win_v, sems.at[0])
            c2 = pltpu.make_async_copy(pw_any, pw_v, sems.at[1])
            c1.start()
            c2.start()
            carry[...] = jnp.zeros_like(carry)
            c1.wait()
            c2.wait()

        xb = x_ref[...].astype(BF16)
        xt_ref[...] = x_ref[...].T.astype(BF16)

        def h_block(j):
            qq, off = divmod(j * PG, cq)
            blk = _mm(xb, win_v[qq, :, off:off + PG])
            h_ref[:, j * PG:(j + 1) * PG] = blk.astype(BF16)
            return blk

        for g, w in enumerate(POOL_WINDOWS):
            sl = slice(g * PG, (g + 1) * PG)
            a, z = h_block(g), h_block(3 * bpd + g)
            ext = jnp.concatenate([carry[:, sl], a], axis=0)
            carry[:, sl] = a[tm - HALO:, :]
            pooled = _causal_window_sum(ext, w) * _inv_count(i * tm, tm, w) - a
            mixed = _mm(pooled.astype(BF16), pw_v[g])
            y_ref[:, sl] = (mixed * ps_ref[:, sl] * (z * jax.nn.sigmoid(z))).astype(BF16)

        for hd in range(bpd):
            sl = slice(hd * PG, (hd + 1) * PG)
            u, v, z = h_block(bpd + hd), h_block(2 * bpd + hd), h_block(4 * bpd + hd)
            vhat, _ = _norm_rows(_gelu(v))
            vn = (vhat * lg_ref[:, sl] + lb_ref[:, sl]).astype(BF16)
            gated = _gelu(u) * (z * jax.nn.sigmoid(z))
            for n in range(tm // CHUNK):
                rs = slice(n * CHUNK, (n + 1) * CHUNK)
                sv = _mm(sw_ref[hd], vn[rs, :]) + sb_ref[:, hd:hd + 1]
                y_ref[rs, D + hd * PG:D + (hd + 1) * PG] = (gated[rs, :] * sv).astype(BF16)

    vec = pl.BlockSpec((1, D), lambda i: (0, 0))
    return pl.pallas_call(
        body, name="front_forward",
        out_shape=[jax.ShapeDtypeStruct((T, 5 * D), BF16), jax.ShapeDtypeStruct((T, 2 * D), BF16),
                   jax.ShapeDtypeStruct((D, T), BF16)],
        grid=(nt,),
        in_specs=[pl.BlockSpec((tm, D), lambda i: (i, 0)), ANY, ANY, vec, vec, vec,
                  pl.BlockSpec(sgu_wm.shape, lambda i: (0, 0, 0)), pl.BlockSpec(sgu_bias_t.shape, lambda i: (0, 0))],
        out_specs=[pl.BlockSpec((tm, 5 * D), lambda i: (i, 0)), pl.BlockSpec((tm, 2 * D), lambda i: (i, 0)),
                   pl.BlockSpec((D, tm), lambda i: (0, i))],
        scratch_shapes=[pltpu.VMEM(w_in.shape, BF16), pltpu.VMEM(pool_w.shape, BF16), pltpu.VMEM((HALO, D), F32),
                        pltpu.SemaphoreType.DMA((2,))],
        compiler_params=pltpu.CompilerParams(dimension_semantics=("arbitrary",), vmem_limit_bytes=VMEM_LIMIT),
    )(x, w_in, pool_w, pool_scale, sgu_g, sgu_b, sgu_wm, sgu_bias_t)


def _tail(y, x, p, target, w_out, w_gate, w_ple, ln_g, ln_b, gate_b, tm):
    T, D = x.shape
    K = p.shape[1]
    nq, _, cq = w_ple.shape
    nt = T // tm

    def body(y_ref, x_ref, p_ref, t_ref, wout_any, wg_any, wp_any, lng_ref, lnb_ref, bg_ref,
             dxp_ref, dy_ref, dwout_any, dwg_any, dwp_any, dlng_ref, dlnb_ref, dbg_ref, ssq_ref,
             wout_v, wg_v, wp_v, dwout_acc, dwg_acc, dwp_acc, sems):
        i = pl.program_id(0)

        @pl.when(i == 0)
        def _():
            loads = [pltpu.make_async_copy(s, d, sems.at[k])
                     for k, (s, d) in enumerate(((wout_any, wout_v), (wg_any, wg_v), (wp_any, wp_v)))]
            for cp in loads:
                cp.start()
            for ref in (dwout_acc, dwg_acc, dwp_acc, dlng_ref, dlnb_ref, dbg_ref, ssq_ref):
                ref[...] = jnp.zeros_like(ref)
            for cp in loads:
                cp.wait()

        yb = y_ref[...]
        pb = p_ref[...].astype(BF16)
        xhat, rstd = _norm_rows(DEEPNORM_ALPHA * x_ref[...] + _mm(yb, wout_v[...]))
        x1 = xhat * lng_ref[...] + lnb_ref[...]
        x1b = x1.astype(BF16)
        gate = jax.nn.sigmoid(_mm(x1b, wg_v[...]) + bg_ref[...])
        e = jnp.concatenate([_mm(pb, wp_v[qq]) for qq in range(nq)], axis=1)
        diff = x1 + gate * e - t_ref[...]
        ssq_ref[...] += jnp.sum(diff * diff, axis=0, keepdims=True)

        dout = diff * (1.0 / D)
        d_e = (dout * gate).astype(BF16)
        dgl = dout * e * gate * (1.0 - gate)
        dglb = dgl.astype(BF16)
        for qq in range(nq):
            dwp_acc[qq] += _mm_tn(pb, d_e[:, qq * cq:(qq + 1) * cq])
        for c0 in range(0, D, MXU_COLS):
            dwg_acc[:, c0:c0 + MXU_COLS] += _mm_tn(x1b, dglb[:, c0:c0 + MXU_COLS])
        dbg_ref[...] += jnp.sum(dgl, axis=0, keepdims=True)
        d_x1 = dout + _mm_nt(dglb, wg_v[...])
        dlng_ref[...] += jnp.sum(d_x1 * xhat, axis=0, keepdims=True)
        dlnb_ref[...] += jnp.sum(d_x1, axis=0, keepdims=True)
        d_r = _norm_rows_bwd(d_x1 * lng_ref[...], xhat, rstd)
        drb = d_r.astype(BF16)
        dxp_ref[...] = DEEPNORM_ALPHA * d_r
        for c0 in range(0, D, MXU_COLS):
            dwout_acc[:, c0:c0 + MXU_COLS] += _mm_tn(yb, drb[:, c0:c0 + MXU_COLS])
        for c0 in range(0, 2 * D, 2 * MXU_COLS):
            dy_ref[:, c0:c0 + 2 * MXU_COLS] = _mm_nt(drb, wout_v[c0:c0 + 2 * MXU_COLS, :]).astype(BF16)

        @pl.when(i == nt - 1)
        def _():
            stores = [pltpu.make_async_copy(s, d, sems.at[k])
                      for k, (s, d) in enumerate(((dwout_acc, dwout_any), (dwg_acc, dwg_any), (dwp_acc, dwp_any)))]
            for cp in stores:
                cp.start()
            for cp in stores:
                cp.wait()

    vec = pl.BlockSpec((1, D), lambda i: (0, 0))
    vec_shape = jax.ShapeDtypeStruct((1, D), F32)

    def tile(cols):
        return pl.BlockSpec((tm, cols), lambda i: (i, 0))

    return pl.pallas_call(
        body, name="tail",
        out_shape=[jax.ShapeDtypeStruct((T, D), F32), jax.ShapeDtypeStruct((T, 2 * D), BF16),
                   jax.ShapeDtypeStruct(w_out.shape, F32), jax.ShapeDtypeStruct(w_gate.shape, F32),
                   jax.ShapeDtypeStruct(w_ple.shape, F32), vec_shape, vec_shape, vec_shape, vec_shape],
        grid=(nt,),
        in_specs=[tile(2 * D), tile(D), tile(K), tile(D), ANY, ANY, ANY, vec, vec, vec],
        out_specs=[tile(D), tile(2 * D), ANY, ANY, ANY, vec, vec, vec, vec],
        scratch_shapes=[pltpu.VMEM(w_out.shape, BF16), pltpu.VMEM(w_gate.shape, BF16), pltpu.VMEM(w_ple.shape, BF16),
                        pltpu.VMEM(w_out.shape, F32), pltpu.VMEM(w_gate.shape, F32), pltpu.VMEM(w_ple.shape, F32),
                        pltpu.SemaphoreType.DMA((3,))],
        compiler_params=pltpu.CompilerParams(dimension_semantics=("arbitrary",), vmem_limit_bytes=VMEM_LIMIT),
    )(y, x, p, target, w_out, w_gate, w_ple, ln_g, ln_b, gate_b)


def _front_backward(h, d_y, dx_part, w_in, pool_w, pool_scale, sgu_g, sgu_b, sgu_wm, sgu_bias_t, tm):
    T = h.shape[0]
    D = h.shape[1] // 5
    nq, _, cq = w_in.shape
    G, PG = pool_w.shape[0], pool_w.shape[1]
    nt = T // tm
    hpt = tm // HALO

    def tile_of(i):
        return nt - 1 - jnp.minimum(i, nt - 1)

    def body(h_ref, halo_ref, dy_ref, dxp_ref, win_any, pw_ref, ps_ref, lg_ref, lb_ref, sw_ref, sb_ref,
             dh_ref, dx_ref, dpw_ref, dps_ref, dlg_ref, dlb_ref, dsw_ref, dsb_ref, win_v, dh_keep, carry, sems):
        i = pl.program_id(0)
        ti = tile_of(i)
        live = (i < nt).astype(F32)

        @pl.when(i == 0)
        def _():
            cp = pltpu.make_async_copy(win_any, win_v, sems.at[0])
            cp.start()
            carry[...] = jnp.zeros_like(carry)
            dh_keep[...] = jnp.zeros_like(dh_keep)
            for ref in (dpw_ref, dps_ref, dlg_ref, dlb_ref, dsw_ref, dsb_ref):
                ref[...] = jnp.zeros_like(ref)
            cp.wait()

        for r0 in range(0, D, MXU_COLS):
            dx = dxp_ref[:, r0:r0 + MXU_COLS]
            for qq in range(nq):
                dx = dx + _mm_nt(dh_keep[(i + 1) % 2, :, qq * cq:(qq + 1) * cq], win_v[qq, r0:r0 + MXU_COLS, :])
            dx_ref[:, r0:r0 + MXU_COLS] = dx

        a = h_ref[:, 0:D].astype(F32)
        before = jnp.where(ti > 0, halo_ref[...].astype(F32), 0.0)
        ext = jnp.concatenate([before, a], axis=0)
        for g, w in enumerate(POOL_WINDOWS):
            sl = slice(g * PG, (g + 1) * PG)
            inv = _inv_count(ti * tm, tm, w)
            pooled = (_causal_window_sum(ext[:, sl], w) * inv - a[:, sl]).astype(BF16)
            mixed = _mm(pooled, pw_ref[g])
            z = h_ref[:, 3 * D + g * PG:3 * D + (g + 1) * PG].astype(F32)
            sig = jax.nn.sigmoid(z)
            dy = dy_ref[:, sl].astype(F32)
            d_ypool = dy * (z * sig)
            dh_ref[:, 3 * D + g * PG:3 * D + (g + 1) * PG] = (
                dy * (mixed * ps_ref[:, sl]) * (sig * (1.0 + z * (1.0 - sig)))).astype(BF16)
            dps_ref[:, sl] += live * jnp.sum(d_ypool * mixed, axis=0, keepdims=True)
            d_mixed = (d_ypool * ps_ref[:, sl]).astype(BF16)
            dpw_ref[g] += live * _mm_tn(pooled, d_mixed)
            d_pooled = _mm_nt(d_mixed, pw_ref[g])
            scaled = d_pooled * inv
            after = jnp.concatenate([scaled, carry[:, sl]], axis=0)
            carry[:, sl] = jnp.where(i < nt - 1, scaled[:HALO, :], carry[:, sl])
            dh_ref[:, sl] = (_anticausal_window_sum(after, w) - d_pooled).astype(BF16)

        for hd in range(D // PG):
            sl = slice(hd * PG, (hd + 1) * PG)
            vg, dvg = _gelu_and_grad(h_ref[:, 2 * D + hd * PG:2 * D + (hd + 1) * PG].astype(F32))
            vhat, rstd = _norm_rows(vg)
            vn = (vhat * lg_ref[:, sl] + lb_ref[:, sl]).astype(BF16)
            d_vn_chunks = []
            for n in range(tm // CHUNK):
                rs = slice(n * CHUNK, (n + 1) * CHUNK)
                sv = _mm(sw_ref[hd], vn[rs, :]) + sb_ref[:, hd:hd + 1]
                ug, dug = _gelu_and_grad(h_ref[rs, D + hd * PG:D + (hd + 1) * PG].astype(F32))
                z = h_ref[rs, 4 * D + hd * PG:4 * D + (hd + 1) * PG].astype(F32)
                sig = jax.nn.sigmoid(z)
                dy = dy_ref[rs, D + hd * PG:D + (hd + 1) * PG].astype(F32)
                d_ysgu = dy * (z * sig)
                dh_ref[rs, 4 * D + hd * PG:4 * D + (hd + 1) * PG] = (
                    dy * (ug * sv) * (sig * (1.0 + z * (1.0 - sig)))).astype(BF16)
                dh_ref[rs, D + hd * PG:D + (hd + 1) * PG] = (d_ysgu * sv * dug).astype(BF16)
                d_sv = d_ysgu * ug
                dsb_ref[:, hd:hd + 1] += live * jnp.sum(d_sv, axis=1, keepdims=True)
                d_svb = d_sv.astype(BF16)
                dsw_ref[hd] += live * _mm_nt(d_svb, vn[rs, :])
                d_vn_chunks.append(_mm_tn(sw_ref[hd], d_svb))
            d_vn = jnp.concatenate(d_vn_chunks, axis=0)
            dlg_ref[:, sl] += live * jnp.sum(d_vn * vhat, axis=0, keepdims=True)
            dlb_ref[:, sl] += live * jnp.sum(d_vn, axis=0, keepdims=True)
            d_vg = _norm_rows_bwd(d_vn * lg_ref[:, sl], vhat, rstd)
            dh_ref[:, 2 * D + hd * PG:2 * D + (hd + 1) * PG] = (d_vg * dvg).astype(BF16)

        dh_keep[i % 2] = dh_ref[...]

    vec = pl.BlockSpec((1, D), lambda i: (0, 0))
    vec_shape = jax.ShapeDtypeStruct((1, D), F32)

    def whole(shape):
        return pl.BlockSpec(shape, lambda i: (0,) * len(shape))

    return pl.pallas_call(
        body, name="front_backward",
        out_shape=[jax.ShapeDtypeStruct((T, 5 * D), BF16), jax.ShapeDtypeStruct((T, D), F32),
                   jax.ShapeDtypeStruct(pool_w.shape, F32), vec_shape, vec_shape,
                   vec_shape, jax.ShapeDtypeStruct(sgu_wm.shape, F32), jax.ShapeDtypeStruct(sgu_bias_t.shape, F32)],
        grid=(nt + 1,),
        in_specs=[pl.BlockSpec((tm, 5 * D), lambda i: (tile_of(i), 0)),
                  pl.BlockSpec((HALO, D), lambda i: (jnp.maximum(tile_of(i) * hpt - 1, 0), 0)),
                  pl.BlockSpec((tm, 2 * D), lambda i: (tile_of(i), 0)),
                  pl.BlockSpec((tm, D), lambda i: (jnp.minimum(nt - i, nt - 1), 0)), ANY,
                  whole(pool_w.shape), vec, vec, vec, whole(sgu_wm.shape), whole(sgu_bias_t.shape)],
        out_specs=[pl.BlockSpec((tm, 5 * D), lambda i: (tile_of(i), 0)),
                   pl.BlockSpec((tm, D), lambda i: (jnp.minimum(nt - i, nt - 1), 0)),
                   whole(pool_w.shape), vec, vec, vec, whole(sgu_wm.shape), whole(sgu_bias_t.shape)],
        scratch_shapes=[pltpu.VMEM(w_in.shape, BF16), pltpu.VMEM((2, tm, 5 * D), BF16), pltpu.VMEM((HALO, D), F32),
                        pltpu.SemaphoreType.DMA((1,))],
        compiler_params=pltpu.CompilerParams(dimension_semantics=("arbitrary",), vmem_limit_bytes=VMEM_LIMIT),
    )(h, h, d_y, dx_part, w_in, pool_w, pool_scale, sgu_g, sgu_b, sgu_wm, sgu_bias_t)


def _weight_backward(d_h, xt, q, scatter_srcs, tm):
    D, T = xt.shape
    cq = d_h.shape[1] // 4
    hr = D // 2
    nt = T // tm
    ns = len(scatter_srcs)

    def body(q_ref, dh_ref, xt_ref, *refs):
        srcs, out_any, dsts = refs[:ns], refs[ns], refs[ns + 1:2 * ns + 1]
        (acc, land_a, send_b, land_b, mine_f, theirs_f,
         a_send, a_recv, b_send, b_recv, j_sems, o_sems, s_send, s_recv) = refs[2 * ns + 1:]
        s, t = pl.program_id(0), pl.program_id(1)
        x_, y_, c = _place()
        sibling = (x_, y_, 1 - c)
        own_rows = pl.ds(pl.multiple_of(c * hr, hr), hr)
        other_rows = pl.ds(pl.multiple_of((1 - c) * hr, hr), hr)

        @pl.when((s == 0) & (t == 0))
        def _():
            for cp in _scatter_copies(srcs, dsts, s_send, s_recv):
                cp.start()

        @pl.when(t == 0)
        def _():
            acc[...] = jnp.zeros_like(acc)

        for c0 in range(0, cq, MXU_COLS):
            acc[:, c0:c0 + MXU_COLS] += _mm(xt_ref[...], dh_ref[:, c0:c0 + MXU_COLS])

        def pair_sum(slot):
            swap = pltpu.make_async_remote_copy(
                src_ref=acc.at[other_rows], dst_ref=land_a.at[slot], send_sem=a_send.at[slot],
                recv_sem=a_recv.at[slot], device_id=sibling, device_id_type=MESH)
            swap.start()
            swap.wait()
            return acc[own_rows, :] + land_a[slot]

        def to_owner(slot):
            flip_x, flip_y = (slot + 1) >> 1, (slot + 1) & 1
            owner = (1 - x_ if flip_x else x_, 1 - y_ if flip_y else y_, c)
            return pltpu.make_async_remote_copy(
                src_ref=send_b.at[slot], dst_ref=land_b.at[slot], send_sem=b_send.at[slot],
                recv_sem=b_recv.at[slot], device_id=owner, device_id_type=MESH)

        for slot in range(3):
            @pl.when((s == slot) & (t == nt - 1))
            def _(slot=slot):
                send_b[slot] = pair_sum(slot).astype(BF16)
                to_owner(slot).start()

        @pl.when((s == 3) & (t == nt - 1))
        def _():
            own = pair_sum(3)
            for slot in range(3):
                to_owner(slot).wait_recv()
            mine_f[...] = (own + land_b[0].astype(F32)) + (land_b[1].astype(F32) + land_b[2].astype(F32))
            join = pltpu.make_async_remote_copy(
                src_ref=mine_f, dst_ref=theirs_f, send_sem=j_sems.at[0], recv_sem=j_sems.at[1],
                device_id=sibling, device_id_type=MESH)
            join.start()
            out_mine = pltpu.make_async_copy(mine_f, out_any.at[own_rows], o_sems.at[0])
            out_mine.start()
            join.wait()
            out_theirs = pltpu.make_async_copy(theirs_f, out_any.at[other_rows], o_sems.at[1])
            out_theirs.start()
            for slot in range(3):
                to_owner(slot).wait_send()
            for cp in _scatter_copies(srcs, dsts, s_send, s_recv):
                cp.wait()
            out_mine.wait()
            out_theirs.wait()

    def quarter(s, t, q_ref):
        return (t, jnp.where(s == 3, q_ref[0], q_ref[0] ^ (s + 1)))

    dma = pltpu.SemaphoreType.DMA
    return pl.pallas_call(
        body, name="weight_backward",
        out_shape=[jax.ShapeDtypeStruct((D, cq), F32)] + _scatter_shapes(scatter_srcs),
        grid_spec=pltpu.PrefetchScalarGridSpec(
            num_scalar_prefetch=1, grid=(4, nt),
            in_specs=[pl.BlockSpec((tm, cq), quarter), pl.BlockSpec((D, tm), lambda s, t, q_ref: (0, t))] + [ANY] * ns,
            out_specs=[ANY] * (ns + 1),
            scratch_shapes=[pltpu.VMEM((D, cq), F32), pltpu.VMEM((4, hr, cq), F32), pltpu.VMEM((3, hr, cq), BF16),
                            pltpu.VMEM((3, hr, cq), BF16), pltpu.VMEM((hr, cq), F32), pltpu.VMEM((hr, cq), F32),
                            dma((4,)), dma((4,)), dma((3,)), dma((3,)), dma((2,)), dma((2,)), dma((3 * ns,)), dma((3 * ns,))]),
        compiler_params=pltpu.CompilerParams(dimension_semantics=("arbitrary", "arbitrary"),
                                             vmem_limit_bytes=VMEM_LIMIT),
    )(jnp.reshape(q, (1,)).astype(jnp.int32), d_h, xt, *scatter_srcs)


def _swap_with_sibling(grads):
    ns = len(grads)

    def body(*refs):
        copies = _swap_copies(refs[:ns], refs[ns:2 * ns], refs[2 * ns], refs[2 * ns + 1])
        for cp in copies:
            cp.start()
        for cp in copies:
            cp.wait()

    return pl.pallas_call(
        body, name="swap_with_sibling", out_shape=_swap_shapes(grads), in_specs=[ANY] * ns, out_specs=[ANY] * ns,
        scratch_shapes=[pltpu.SemaphoreType.DMA((ns,)), pltpu.SemaphoreType.DMA((ns,))],
    )(*grads)


def _token_tile(T, want):
    return math.gcd(T, want)


def kernel(x, p, w_in, pool_w, pool_scale, sgu_ln_g, sgu_ln_b, sgu_w, sgu_b, w_out, ln_g, ln_b, ple_w, ple_gate_w, ple_gate_b, loss_target, m_w_in, m_pool_w, m_pool_scale, m_sgu_ln_g, m_sgu_ln_b, m_sgu_w, m_sgu_b, m_w_out, m_ln_g, m_ln_b, m_ple_w, m_ple_gate_w, m_ple_gate_b, v_w_in, v_pool_w, v_pool_scale, v_sgu_ln_g, v_sgu_ln_b, v_sgu_w, v_sgu_b, v_w_out, v_ln_g, v_ln_b, v_ple_w, v_ple_gate_w, v_ple_gate_b):
    c = lax.axis_index("c")
    T, D = x.shape[1], x.shape[2]
    tm, tm_vpu, tm_acc = _token_tile(T, 512), _token_tile(T, 256), _token_tile(T, 2048)
    x2, p2, tgt = x[0], p[0, 0], loss_target[0]
    G, PGQ, PG = pool_w.shape[1], pool_w.shape[2], pool_w.shape[3]

    shards = [w_in[0], w_out[0], ple_gate_w[0], ple_w[0], pool_w[0].reshape(G * PGQ, PG)]
    w_in_f, w_out_f, w_gate_f, w_ple_f, pool_f = _gather_weights(shards)
    w_out_f = w_out_f.reshape(-1, D)
    w_gate_f = w_gate_f.reshape(-1, D)
    pool_f = pool_f.reshape(4, G, PGQ, PG).transpose(1, 0, 2, 3).reshape(G, 4 * PGQ, PG)
    tril = jnp.tril(jnp.ones((CHUNK, CHUNK), dtype=bool))
    sgu_wm = jnp.where(tril[None], sgu_w[0], 0.0).astype(BF16)
    sgu_bias_t = sgu_b[0].T

    h, y, xt = _front_forward(x2, w_in_f, pool_f, pool_scale, sgu_ln_g, sgu_ln_b, sgu_wm, sgu_bias_t, tm)
    (dx_part, d_y, d_w_out, d_w_gate, d_w_ple, d_ln_g, d_ln_b, d_gate_b, ssq) = _tail(
        y, x2, p2, tgt, w_out_f, w_gate_f, w_ple_f, ln_g, ln_b, ple_gate_b, tm)
    d_h, d_x, d_pool_w, d_pool_scale, d_sgu_g, d_sgu_b, d_sgu_w, d_sgu_bias_t = _front_backward(
        h, d_y, dx_part, w_in_f, pool_f, pool_scale, sgu_ln_g, sgu_ln_b, sgu_wm, sgu_bias_t, tm_vpu)
    loss = lax.psum((0.5 / D) * jnp.sum(ssq), ("x", "y", "c"))

    grads = [d_w_out.reshape(4, -1, D), d_w_gate.reshape(4, -1, D), d_w_ple,
             d_pool_w.reshape(G, 4, PGQ, PG).transpose(1, 0, 2, 3).reshape(4, G * PGQ, PG)]
    grads = [g.reshape(4, 2, g.shape[1] // 2, g.shape[2]) for g in grads]
    d_sgu_w = jnp.where(tril[None], d_sgu_w, 0.0)
    small_names = ["pool_scale", "sgu_ln_g", "sgu_ln_b", "ln_g", "ln_b", "ple_gate_b", "sgu_b", "sgu_w"]
    small_grads = [d_pool_scale, d_sgu_g, d_sgu_b, d_ln_g, d_ln_b, d_gate_b, d_sgu_bias_t.T, d_sgu_w]

    def pack(arrays):
        rows = [a.reshape(-1) for a in arrays[:6]] + [jnp.pad(arrays[6].reshape(-1), (0, 2 * D - arrays[6].size))]
        return jnp.concatenate([r.reshape(-1, D) for r in rows] + [arrays[7].reshape(-1, D)], axis=0)

    def unpack(packed, like):
        out = [packed[k].reshape(like[k].shape) for k in range(6)]
        out.append(packed[6, :like[6].size].reshape(like[6].shape))
        out.append(packed[8:].reshape(like[7].shape))
        return out

    small = pack(small_grads)
    q = 2 * lax.axis_index("x") + lax.axis_index("y")
    *landed, small_landed = _swap_with_sibling(grads + [small])
    parts = [_add_own_half(g, l, c) for g, l in zip(grads, landed)]
    small_chip = _add_pairs(small, small_landed, "add_small")
    d_w_in, *slots, small_slots = _weight_backward(d_h, xt, q, parts + [small_chip], tm_acc)
    halves = [_sum_four(pt, s, q, "sum_four_%d" % k) for k, (pt, s) in enumerate(zip(parts, slots))]
    small_total = _sum_four(small_chip, small_slots, q, "sum_four_small")
    sibling_halves = _join_halves_with_sibling(halves)

    big_names = ["w_out", "ple_gate_w", "ple_w", "pool_w"]
    given = dict(w_in=(w_in, m_w_in, v_w_in), w_out=(w_out, m_w_out, v_w_out),
                 ple_gate_w=(ple_gate_w, m_ple_gate_w, v_ple_gate_w), ple_w=(ple_w, m_ple_w, v_ple_w),
                 pool_w=(pool_w, m_pool_w, v_pool_w), pool_scale=(pool_scale, m_pool_scale, v_pool_scale),
                 sgu_ln_g=(sgu_ln_g, m_sgu_ln_g, v_sgu_ln_g), sgu_ln_b=(sgu_ln_b, m_sgu_ln_b, v_sgu_ln_b),
                 sgu_w=(sgu_w, m_sgu_w, v_sgu_w), sgu_b=(sgu_b, m_sgu_b, v_sgu_b), ln_g=(ln_g, m_ln_g, v_ln_g),
                 ln_b=(ln_b, m_ln_b, v_ln_b), ple_gate_b=(ple_gate_b, m_ple_gate_b, v_ple_gate_b))
    grad, delta, new_m, new_v = {}, {}, {}, {}
    for name, g_mine, g_sibling in zip(big_names, halves, sibling_halves):
        w, m, v = given[name]
        flat = (2 * g_mine.shape[0], g_mine.shape[1])
        outs = _adamw_joined(w.reshape(flat), g_mine, g_sibling, m.reshape(flat), v.reshape(flat), c, "adamw_" + name)
        grad[name], delta[name], new_m[name], new_v[name] = (t.reshape(w.shape) for t in outs)
    grad["w_in"], delta["w_in"], new_m["w_in"], new_v["w_in"] = (
        t[None] for t in _adamw(w_in[0], d_w_in, m_w_in[0], v_w_in[0], "adamw_w_in"))
    small_w, small_m, small_v = (pack([given[n][k] for n in small_names]) for k in range(3))
    small_out = _adamw(small_w, small_total, small_m, small_v, "adamw_small")
    like = [given[n][0] for n in small_names]
    for k, name in enumerate(small_names):
        grad[name], delta[name], new_m[name], new_v[name] = (unpack(t, like)[k] for t in small_out)

    order = ["w_in", "pool_w", "pool_scale", "sgu_ln_g", "sgu_ln_b", "sgu_w", "sgu_b", "w_out", "ln_g", "ln_b",
             "ple_w", "ple_gate_w", "ple_gate_b"]
    return (loss, d_x[None], *[grad[n] for n in order], *[delta[n] for n in order],
            *[new_m[n] for n in order], *[new_v[n] for n in order])
```

```python
import functools
import math

import jax
import jax.numpy as jnp
from jax import lax
from jax.experimental import pallas as pl
from jax.experimental.pallas import tpu as pltpu

F32, BF16 = jnp.float32, jnp.bfloat16
MESH = pl.DeviceIdType.MESH
ANY = pl.BlockSpec(memory_space=pl.ANY)

POOL_WINDOWS = (2, 4, 8, 16)
HALO = 16
CHUNK = 128
MXU_COLS = 256
LN_EPS = 1e-5
DEEPNORM_ALPHA = 2.0 ** 0.25
ADAM_LR, ADAM_B1, ADAM_B2, ADAM_EPS, ADAM_WD, ADAM_STEP = 1e-3, 0.9, 0.999, 1e-8, 0.01, 10
VMEM_LIMIT = 56 * 1024 * 1024
GELU_K = math.sqrt(2.0 / math.pi)
GELU_C = 0.044715


def _mm(a, b):
    return jnp.dot(a, b, preferred_element_type=F32)


def _mm_nt(a, b):
    return lax.dot_general(a, b, (((1,), (1,)), ((), ())), preferred_element_type=F32)


def _mm_tn(a, b):
    return lax.dot_general(a, b, (((0,), (0,)), ((), ())), preferred_element_type=F32)


def _gelu(x):
    t = jnp.tanh(GELU_K * (x + GELU_C * x * x * x))
    return 0.5 * x * (1.0 + t)


def _gelu_and_grad(x):
    x2 = x * x
    t = jnp.tanh(GELU_K * (x + GELU_C * x2 * x))
    g = 0.5 * x * (1.0 + t)
    dg = 0.5 * (1.0 + t) + 0.5 * x * (1.0 - t * t) * (GELU_K * (1.0 + 3.0 * GELU_C * x2))
    return g, dg


def _norm_rows(x):
    mu = jnp.mean(x, axis=-1, keepdims=True)
    xc = x - mu
    var = jnp.mean(xc * xc, axis=-1, keepdims=True)
    rstd = lax.rsqrt(var + LN_EPS)
    return xc * rstd, rstd


def _norm_rows_bwd(dxhat, xhat, rstd):
    m1 = jnp.mean(dxhat, axis=-1, keepdims=True)
    m2 = jnp.mean(dxhat * xhat, axis=-1, keepdims=True)
    return rstd * (dxhat - m1 - xhat * m2)


def _inv_count(row0, rows, w):
    t = row0 + lax.broadcasted_iota(jnp.int32, (rows, 1), 0)
    return 1.0 / jnp.minimum(t + 1, w).astype(F32)


def _causal_window_sum(ext, w):
    s, sh = ext, 1
    while sh < w:
        s = s + pltpu.roll(s, sh, axis=0)
        sh *= 2
    return s[HALO:, :]


def _anticausal_window_sum(ext, w):
    n, s, sh = ext.shape[0], ext, 1
    while sh < w:
        s = s + pltpu.roll(s, n - sh, axis=0)
        sh *= 2
    return s[: n - HALO, :]


def _place():
    return lax.axis_index("x"), lax.axis_index("y"), lax.axis_index("c")


def _gather_weights(shards):
    n = len(shards)
    half = [s.shape[0] // 2 for s in shards]

    def body(*refs):
        wide, dsts, srcs = refs[:n], refs[n:2 * n], refs[2 * n:3 * n]
        send_sems, recv_sems, local_sems = refs[3 * n:]
        for k in range(n):
            for r0 in range(0, 2 * half[k], CHUNK):
                srcs[k][r0:r0 + CHUNK, :] = wide[k][r0:r0 + CHUNK, :].astype(BF16)
        x, y, c = _place()
        q = 2 * x + y
        sibling = (x, y, 1 - c)
        chips = [(1 - x, y), (x, 1 - y), (1 - x, 1 - y)]

        def rows(k, qq, cc):
            return dsts[k].at[qq, pl.ds(cc * half[k], half[k])]

        def copy(k, sem, qq, cc, to, src=None):
            return pltpu.make_async_remote_copy(
                src_ref=rows(k, qq, cc) if src is None else src, dst_ref=rows(k, qq, cc),
                send_sem=send_sems.at[6 * k + sem], recv_sem=recv_sems.at[6 * k + sem],
                device_id=to, device_id_type=MESH)

        mine = [pltpu.make_async_copy(srcs[k], dsts[k].at[q], local_sems.at[k]) for k in range(n)]
        for cp in mine:
            cp.start()
        started = []
        for k in range(n):
            own = srcs[k].at[pl.ds(c * half[k], half[k])]
            for j, chip in enumerate(chips):
                started.append(copy(k, j, q, c, (*chip, c), src=own))
                started[-1].start()
        for k in range(n):
            for j, chip in enumerate(chips):
                qq = 2 * chip[0] + chip[1]
                copy(k, j, qq, c, (x, y, c)).wait_recv()
                started.append(copy(k, 3 + j, qq, c, sibling))
                started[-1].start()
        for k in range(n):
            for j, chip in enumerate(chips):
                copy(k, 3 + j, 2 * chip[0] + chip[1], 1 - c, (x, y, c)).wait_recv()
        for cp in started:
            cp.wait_send()
        for cp in mine:
            cp.wait()

    return pl.pallas_call(
        body, name="gather_weights",
        out_shape=[jax.ShapeDtypeStruct((4,) + s.shape, BF16) for s in shards],
        in_specs=[pl.BlockSpec(memory_space=pltpu.VMEM)] * n, out_specs=[ANY] * n,
        scratch_shapes=[pltpu.VMEM(s.shape, BF16) for s in shards]
        + [pltpu.SemaphoreType.DMA((6 * n,)), pltpu.SemaphoreType.DMA((6 * n,)), pltpu.SemaphoreType.DMA((n,))],
        compiler_params=pltpu.CompilerParams(vmem_limit_bytes=VMEM_LIMIT),
    )(*shards)


def _direct_gather_copies(srcs, dsts, send_sems, recv_sems):
    x, y, c = _place()
    q = 2 * x + y
    sends, recvs = [], []
    for k, (src, dst) in enumerate(zip(srcs, dsts)):
        half = src.shape[0] // 2
        for j, chip in enumerate([(1 - x, y), (x, 1 - y), (1 - x, 1 - y)]):
            for core in range(2):
                sends.append(pltpu.make_async_remote_copy(
                    src_ref=src.at[pl.ds(c * half, half)], dst_ref=dst.at[q, pl.ds(c * half, half)],
                    send_sem=send_sems.at[6 * k + 2 * j + core], recv_sem=recv_sems.at[6 * k + 2 * j + c],
                    device_id=(*chip, core), device_id_type=MESH))
                landed = dst.at[2 * chip[0] + chip[1], pl.ds(core * half, half)]
                recvs.append(pltpu.make_async_remote_copy(
                    src_ref=landed, dst_ref=landed, send_sem=send_sems.at[6 * k + 2 * j + core],
                    recv_sem=recv_sems.at[6 * k + 2 * j + core], device_id=(x, y, c), device_id_type=MESH))
    return sends, recvs


def _swap_copies(srcs, dsts, send_sems, recv_sems):
    x, y, c = _place()
    return [pltpu.make_async_remote_copy(
        src_ref=src.at[:, 1 - c] if len(src.shape) == 4 else src, dst_ref=dst,
        send_sem=send_sems.at[k], recv_sem=recv_sems.at[k], device_id=(x, y, 1 - c), device_id_type=MESH)
        for k, (src, dst) in enumerate(zip(srcs, dsts))]


def _swap_shapes(grads):
    return [jax.ShapeDtypeStruct((4,) + g.shape[2:] if g.ndim == 4 else g.shape, g.dtype) for g in grads]


def _scatter_copies(srcs, dsts, send_sems, recv_sems):
    x, y, c = _place()
    copies = []
    for j, chip in enumerate([(1 - x, y), (x, 1 - y), (1 - x, 1 - y)]):
        for k, (src, dst) in enumerate(zip(srcs, dsts)):
            copies.append(pltpu.make_async_remote_copy(
                src_ref=src.at[2 * chip[0] + chip[1]] if len(src.shape) == 3 else src, dst_ref=dst.at[j],
                send_sem=send_sems.at[3 * k + j], recv_sem=recv_sems.at[3 * k + j],
                device_id=(*chip, c), device_id_type=MESH))
    return copies


def _scatter_shapes(parts):
    return [jax.ShapeDtypeStruct((3,) + (p.shape[1:] if p.ndim == 3 else p.shape), p.dtype) for p in parts]


def _join_halves_with_sibling(halves):
    n = len(halves)

    def body(*refs):
        srcs, dsts = refs[:n], refs[n:2 * n]
        send_sems, recv_sems = refs[2 * n:]
        x, y, c = _place()
        copies = [pltpu.make_async_remote_copy(
            src_ref=srcs[k], dst_ref=dsts[k], send_sem=send_sems.at[k], recv_sem=recv_sems.at[k],
            device_id=(x, y, 1 - c), device_id_type=MESH) for k in range(n)]
        for cp in copies:
            cp.start()
        for cp in copies:
            cp.wait()

    return pl.pallas_call(
        body, name="join_halves",
        out_shape=[jax.ShapeDtypeStruct(h.shape, h.dtype) for h in halves],
        in_specs=[ANY] * n, out_specs=[ANY] * n,
        scratch_shapes=[pltpu.SemaphoreType.DMA((n,)), pltpu.SemaphoreType.DMA((n,))],
    )(*halves)


def _row_block(rows, cols, n_arrays):
    cap = max(8, (VMEM_LIMIT // 4) // (8 * n_arrays * cols))
    rb = rows
    while rb > cap and rb % 2 == 0:
        rb //= 2
    return rb


def _add_own_half(grad, landed, c):
    _, _, hr, cols = grad.shape
    rb = _row_block(hr, cols, 3)

    def body(c_ref, g_ref, l_ref, o_ref):
        o_ref[...] = (g_ref[...] + l_ref[...]).astype(BF16)

    return pl.pallas_call(
        body, name="add_own_half",
        out_shape=jax.ShapeDtypeStruct(landed.shape, BF16),
        grid_spec=pltpu.PrefetchScalarGridSpec(
            num_scalar_prefetch=1, grid=(4, hr // rb),
            in_specs=[pl.BlockSpec((None, None, rb, cols), lambda qq, r, c_ref: (qq, c_ref[0], r, 0)),
                      pl.BlockSpec((None, rb, cols), lambda qq, r, c_ref: (qq, r, 0))],
            out_specs=pl.BlockSpec((None, rb, cols), lambda qq, r, c_ref: (qq, r, 0))),
    )(jnp.reshape(c, (1,)).astype(jnp.int32), grad, landed)


def _add_pairs(a, b, name):
    rows, cols = a.shape
    rb = _row_block(rows, cols, 3)

    def body(a_ref, b_ref, o_ref):
        o_ref[...] = a_ref[...] + b_ref[...]

    spec = pl.BlockSpec((rb, cols), lambda r: (r, 0))
    return pl.pallas_call(body, name=name, out_shape=jax.ShapeDtypeStruct(a.shape, F32), grid=(rows // rb,),
                          in_specs=[spec, spec], out_specs=spec)(a, b)


def _sum_four(own, slots, q, name):
    _, rows, cols = slots.shape
    rb = _row_block(rows, cols, 5)

    def body(q_ref, own_ref, s_ref, o_ref):
        o_ref[...] = ((own_ref[...].astype(F32) + s_ref[0].astype(F32))
                      + (s_ref[1].astype(F32) + s_ref[2].astype(F32)))

    if own.ndim == 3:
        own_spec = pl.BlockSpec((None, rb, cols), lambda r, q_ref: (q_ref[0], r, 0))
    else:
        own_spec = pl.BlockSpec((rb, cols), lambda r, q_ref: (r, 0))
    return pl.pallas_call(
        body, name=name, out_shape=jax.ShapeDtypeStruct((rows, cols), F32),
        grid_spec=pltpu.PrefetchScalarGridSpec(
            num_scalar_prefetch=1, grid=(rows // rb,),
            in_specs=[own_spec, pl.BlockSpec((3, rb, cols), lambda r, q_ref: (0, r, 0))],
            out_specs=pl.BlockSpec((rb, cols), lambda r, q_ref: (r, 0))),
    )(jnp.reshape(q, (1,)).astype(jnp.int32), own, slots)


def _adamw_math(w, g, m, v):
    nm = ADAM_B1 * m + (1.0 - ADAM_B1) * g
    nv = ADAM_B2 * v + (1.0 - ADAM_B2) * (g * g)
    m_hat = nm / (1.0 - ADAM_B1 ** ADAM_STEP)
    v_hat = nv / (1.0 - ADAM_B2 ** ADAM_STEP)
    return -ADAM_LR * (m_hat / (jnp.sqrt(v_hat) + ADAM_EPS) + ADAM_WD * w), nm, nv


def _adamw(w, g, m, v, name):
    rows, cols = w.shape
    rb = _row_block(rows, cols, 8)

    def body(w_ref, g_ref, m_ref, v_ref, go_ref, d_ref, nm_ref, nv_ref):
        go_ref[...] = g_ref[...]
        d_ref[...], nm_ref[...], nv_ref[...] = _adamw_math(w_ref[...], g_ref[...], m_ref[...], v_ref[...])

    spec = pl.BlockSpec((rb, cols), lambda r: (r, 0))
    out = jax.ShapeDtypeStruct(w.shape, F32)
    return pl.pallas_call(body, name=name, out_shape=[out] * 4, grid=(rows // rb,),
                          in_specs=[spec] * 4, out_specs=[spec] * 4)(w, g, m, v)


def _adamw_joined(w, g_mine, g_sibling, m, v, c, name):
    rows, cols = w.shape
    rb = _row_block(rows // 2, cols, 9)
    nb = rows // 2 // rb

    def body(c_ref, w_ref, gm_ref, gs_ref, m_ref, v_ref, g_ref, d_ref, nm_ref, nv_ref):
        g = jnp.where(c_ref[0] == pl.program_id(0), gm_ref[...], gs_ref[...])
        g_ref[...] = g
        d_ref[...], nm_ref[...], nv_ref[...] = _adamw_math(w_ref[...], g, m_ref[...], v_ref[...])

    full = pl.BlockSpec((rb, cols), lambda hf, r, c_ref: (hf * nb + r, 0))
    part = pl.BlockSpec((rb, cols), lambda hf, r, c_ref: (r, 0))
    out = jax.ShapeDtypeStruct(w.shape, F32)
    return pl.pallas_call(
        body, name=name, out_shape=[out] * 4,
        grid_spec=pltpu.PrefetchScalarGridSpec(
            num_scalar_prefetch=1, grid=(2, nb),
            in_specs=[full, part, part, full, full], out_specs=[full] * 4),
    )(jnp.reshape(c, (1,)).astype(jnp.int32), w, g_mine, g_sibling, m, v)


def _front_forward(x, w_in, pool_w, pool_scale, sgu_g, sgu_b, sgu_wm, sgu_bias_t, later_shards, tm):
    T, D = x.shape
    nq, _, cq = w_in.shape
    G, PG = pool_w.shape[0], pool_w.shape[1]
    nt = T // tm
    bpd = D // PG
    nl = len(later_shards)

    def body(x_ref, win_any, pw_any, ps_ref, lg_ref, lb_ref, sw_ref, sb_ref, *refs):
        shards_any, (h_ref, y_ref, xt_ref), gathered = refs[:nl], refs[nl:nl + 3], refs[nl + 3:2 * nl + 3]
        wide, narrow = refs[2 * nl + 3:3 * nl + 3], refs[3 * nl + 3:4 * nl + 3]
        win_v, pw_v, carry, sems, load_sems, own_sems, send_sems, recv_sems = refs[4 * nl + 3:]
        i = pl.program_id(0)
        own_quarter = 2 * lax.axis_index("x") + lax.axis_index("y")

        def own_copies():
            return [pltpu.make_async_copy(narrow[k], gathered[k].at[own_quarter], own_sems.at[k]) for k in range(nl)]

        @pl.when(i == 0)
        def _():
            c1 = pltpu.make_async_copy(win_any, win_v, sems.at[0])
            c2 = pltpu.make_async_copy(pw_any, pw_v, sems.at[1])
            loads = [pltpu.make_async_copy(shards_any[k], wide[k], load_sems.at[k]) for k in range(nl)]
            for cp in [c1, c2] + loads:
                cp.start()
            carry[...] = jnp.zeros_like(carry)
            for k in range(nl):
                loads[k].wait()
                for r0 in range(0, wide[k].shape[0], CHUNK):
                    narrow[k][r0:r0 + CHUNK, :] = wide[k][r0:r0 + CHUNK, :].astype(BF16)
            for cp in own_copies() + _direct_gather_copies(narrow, gathered, send_sems, recv_sems)[0]:
                cp.start()
            c1.wait()
            c2.wait()

        xb = x_ref[...].astype(BF16)
        xt_ref[...] = x_ref[...].T.astype(BF16)

        def h_block(j):
            qq, off = divmod(j * PG, cq)
            blk = _mm(xb, win_v[qq, :, off:off + PG])
            h_ref[:, j * PG:(j + 1) * PG] = blk.astype(BF16)
            return blk

        for g, w in enumerate(POOL_WINDOWS):
            sl = slice(g * PG, (g + 1) * PG)
            a, z = h_block(g), h_block(3 * bpd + g)
            ext = jnp.concatenate([carry[:, sl], a], axis=0)
            carry[:, sl] = a[tm - HALO:, :]
            pooled = _causal_window_sum(ext, w) * _inv_count(i * tm, tm, w) - a
            mixed = _mm(pooled.astype(BF16), pw_v[g])
            y_ref[:, sl] = (mixed * ps_ref[:, sl] * (z * jax.nn.sigmoid(z))).astype(BF16)

        for hd in range(bpd):
            sl = slice(hd * PG, (hd + 1) * PG)
            u, v, z = h_block(bpd + hd), h_block(2 * bpd + hd), h_block(4 * bpd + hd)
            vhat, _ = _norm_rows(_gelu(v))
            vn = (vhat * lg_ref[:, sl] + lb_ref[:, sl]).astype(BF16)
            gated = _gelu(u) * (z * jax.nn.sigmoid(z))
            for n in range(tm // CHUNK):
                rs = slice(n * CHUNK, (n + 1) * CHUNK)
                sv = _mm(sw_ref[hd], vn[rs, :]) + sb_ref[:, hd:hd + 1]
                y_ref[rs, D + hd * PG:D + (hd + 1) * PG] = (gated[rs, :] * sv).astype(BF16)

        @pl.when(i == nt - 1)
        def _():
            sends, recvs = _direct_gather_copies(narrow, gathered, send_sems, recv_sems)
            for cp in sends:
                cp.wait_send()
            for cp in recvs:
                cp.wait_recv()
            for cp in own_copies():
                cp.wait()

    vec = pl.BlockSpec((1, D), lambda i: (0, 0))
    return pl.pallas_call(
        body, name="front_forward",
        out_shape=[jax.ShapeDtypeStruct((T, 5 * D), BF16), jax.ShapeDtypeStruct((T, 2 * D), BF16),
                   jax.ShapeDtypeStruct((D, T), BF16)]
        + [jax.ShapeDtypeStruct((4,) + s.shape, BF16) for s in later_shards],
        grid=(nt,),
        in_specs=[pl.BlockSpec((tm, D), lambda i: (i, 0)), ANY, ANY, vec, vec, vec,
                  pl.BlockSpec(sgu_wm.shape, lambda i: (0, 0, 0)), pl.BlockSpec(sgu_bias_t.shape, lambda i: (0, 0))]
        + [ANY] * nl,
        out_specs=[pl.BlockSpec((tm, 5 * D), lambda i: (i, 0)), pl.BlockSpec((tm, 2 * D), lambda i: (i, 0)),
                   pl.BlockSpec((D, tm), lambda i: (0, i))] + [ANY] * nl,
        scratch_shapes=[pltpu.VMEM(s.shape, F32) for s in later_shards]
        + [pltpu.VMEM(s.shape, BF16) for s in later_shards]
        + [pltpu.VMEM(w_in.shape, BF16), pltpu.VMEM(pool_w.shape, BF16), pltpu.VMEM((HALO, D), F32),
           pltpu.SemaphoreType.DMA((2,)), pltpu.SemaphoreType.DMA((nl,)), pltpu.SemaphoreType.DMA((nl,)),
           pltpu.SemaphoreType.DMA((6 * nl,)), pltpu.SemaphoreType.DMA((6 * nl,))],
        compiler_params=pltpu.CompilerParams(dimension_semantics=("arbitrary",), vmem_limit_bytes=VMEM_LIMIT),
    )(x, w_in, pool_w, pool_scale, sgu_g, sgu_b, sgu_wm, sgu_bias_t, *later_shards)


def _tail(y, x, p, target, w_out, w_gate, w_ple, ln_g, ln_b, gate_b, tm):
    T, D = x.shape
    K = p.shape[1]
    nq, _, cq = w_ple.shape
    nt = T // tm

    def body(y_ref, x_ref, p_ref, t_ref, wout_any, wg_any, wp_any, lng_ref, lnb_ref, bg_ref,
             dxp_ref, dy_ref, dwout_any, dwg_any, dwp_any, dlng_ref, dlnb_ref, dbg_ref, ssq_ref,
             wout_v, wg_v, wp_v, dwout_acc, dwg_acc, dwp_acc, sems):
        i = pl.program_id(0)

        @pl.when(i == 0)
        def _():
            loads = [pltpu.make_async_copy(s, d, sems.at[k])
                     for k, (s, d) in enumerate(((wout_any, wout_v), (wg_any, wg_v), (wp_any, wp_v)))]
            for cp in loads:
                cp.start()
            for ref in (dwout_acc, dwg_acc, dwp_acc, dlng_ref, dlnb_ref, dbg_ref, ssq_ref):
                ref[...] = jnp.zeros_like(ref)
            for cp in loads:
                cp.wait()

        yb = y_ref[...]
        pb = p_ref[...].astype(BF16)
        xhat, rstd = _norm_rows(DEEPNORM_ALPHA * x_ref[...] + _mm(yb, wout_v[...]))
        x1 = xhat * lng_ref[...] + lnb_ref[...]
        x1b = x1.astype(BF16)
        gate = jax.nn.sigmoid(_mm(x1b, wg_v[...]) + bg_ref[...])
        e = jnp.concatenate([_mm(pb, wp_v[qq]) for qq in range(nq)], axis=1)
        diff = x1 + gate * e - t_ref[...]
        ssq_ref[...] += jnp.sum(diff * diff, axis=0, keepdims=True)

        dout = diff * (1.0 / D)
        d_e = (dout * gate).astype(BF16)
        dgl = dout * e * gate * (1.0 - gate)
        dglb = dgl.astype(BF16)
        for qq in range(nq):
            dwp_acc[qq] += _mm_tn(pb, d_e[:, qq * cq:(qq + 1) * cq])
        for c0 in range(0, D, MXU_COLS):
            dwg_acc[:, c0:c0 + MXU_COLS] += _mm_tn(x1b, dglb[:, c0:c0 + MXU_COLS])
        dbg_ref[...] += jnp.sum(dgl, axis=0, keepdims=True)
        d_x1 = dout + _mm_nt(dglb, wg_v[...])
        dlng_ref[...] += jnp.sum(d_x1 * xhat, axis=0, keepdims=True)
        dlnb_ref[...] += jnp.sum(d_x1, axis=0, keepdims=True)
        d_r = _norm_rows_bwd(d_x1 * lng_ref[...], xhat, rstd)
        drb = d_r.astype(BF16)
        dxp_ref[...] = DEEPNORM_ALPHA * d_r
        for c0 in range(0, D, MXU_COLS):
            dwout_acc[:, c0:c0 + MXU_COLS] += _mm_tn(yb, drb[:, c0:c0 + MXU_COLS])
        for c0 in range(0, 2 * D, 2 * MXU_COLS):
            dy_ref[:, c0:c0 + 2 * MXU_COLS] = _mm_nt(drb, wout_v[c0:c0 + 2 * MXU_COLS, :]).astype(BF16)

        @pl.when(i == nt - 1)
        def _():
            stores = [pltpu.make_async_copy(s, d, sems.at[k])
                      for k, (s, d) in enumerate(((dwout_acc, dwout_any), (dwg_acc, dwg_any), (dwp_acc, dwp_any)))]
            for cp in stores:
                cp.start()
            for cp in stores:
                cp.wait()

    vec = pl.BlockSpec((1, D), lambda i: (0, 0))
    vec_shape = jax.ShapeDtypeStruct((1, D), F32)

    def tile(cols):
        return pl.BlockSpec((tm, cols), lambda i: (i, 0))

    return pl.pallas_call(
        body, name="tail",
        out_shape=[jax.ShapeDtypeStruct((T, D), F32), jax.ShapeDtypeStruct((T, 2 * D), BF16),
                   jax.ShapeDtypeStruct(w_out.shape, F32), jax.ShapeDtypeStruct(w_gate.shape, F32),
                   jax.ShapeDtypeStruct(w_ple.shape, F32), vec_shape, vec_shape, vec_shape, vec_shape],
        grid=(nt,),
        in_specs=[tile(2 * D), tile(D), tile(K), tile(D), ANY, ANY, ANY, vec, vec, vec],
        out_specs=[tile(D), tile(2 * D), ANY, ANY, ANY, vec, vec, vec, vec],
        scratch_shapes=[pltpu.VMEM(w_out.shape, BF16), pltpu.VMEM(w_gate.shape, BF16), pltpu.VMEM(w_ple.shape, BF16),
                        pltpu.VMEM(w_out.shape, F32), pltpu.VMEM(w_gate.shape, F32), pltpu.VMEM(w_ple.shape, F32),
                        pltpu.SemaphoreType.DMA((3,))],
        compiler_params=pltpu.CompilerParams(dimension_semantics=("arbitrary",), vmem_limit_bytes=VMEM_LIMIT),
    )(y, x, p, target, w_out, w_gate, w_ple, ln_g, ln_b, gate_b)


def _front_backward(h, d_y, dx_part, w_in, pool_w, pool_scale, sgu_g, sgu_b, sgu_wm, sgu_bias_t, tm):
    T = h.shape[0]
    D = h.shape[1] // 5
    nq, _, cq = w_in.shape
    G, PG = pool_w.shape[0], pool_w.shape[1]
    nt = T // tm
    hpt = tm // HALO

    def tile_of(i):
        return nt - 1 - jnp.minimum(i, nt - 1)

    def body(h_ref, halo_ref, dy_ref, dxp_ref, win_any, pw_ref, ps_ref, lg_ref, lb_ref, sw_ref, sb_ref,
             dh_ref, dx_ref, dpw_ref, dps_ref, dlg_ref, dlb_ref, dsw_ref, dsb_ref, win_v, dh_keep, carry, sems):
        i = pl.program_id(0)
        ti = tile_of(i)
        live = (i < nt).astype(F32)

        @pl.when(i == 0)
        def _():
            cp = pltpu.make_async_copy(win_any, win_v, sems.at[0])
            cp.start()
            carry[...] = jnp.zeros_like(carry)
            dh_keep[...] = jnp.zeros_like(dh_keep)
            for ref in (dpw_ref, dps_ref, dlg_ref, dlb_ref, dsw_ref, dsb_ref):
                ref[...] = jnp.zeros_like(ref)
            cp.wait()

        for r0 in range(0, D, MXU_COLS):
            dx = dxp_ref[:, r0:r0 + MXU_COLS]
            for qq in range(nq):
                dx = dx + _mm_nt(dh_keep[(i + 1) % 2, :, qq * cq:(qq + 1) * cq], win_v[qq, r0:r0 + MXU_COLS, :])
            dx_ref[:, r0:r0 + MXU_COLS] = dx

        a = h_ref[:, 0:D].astype(F32)
        before = jnp.where(ti > 0, halo_ref[...].astype(F32), 0.0)
        ext = jnp.concatenate([before, a], axis=0)
        for g, w in enumerate(POOL_WINDOWS):
            sl = slice(g * PG, (g + 1) * PG)
            inv = _inv_count(ti * tm, tm, w)
            pooled = (_causal_window_sum(ext[:, sl], w) * inv - a[:, sl]).astype(BF16)
            mixed = _mm(pooled, pw_ref[g])
            z = h_ref[:, 3 * D + g * PG:3 * D + (g + 1) * PG].astype(F32)
            sig = jax.nn.sigmoid(z)
            dy = dy_ref[:, sl].astype(F32)
            d_ypool = dy * (z * sig)
            dh_ref[:, 3 * D + g * PG:3 * D + (g + 1) * PG] = (
                dy * (mixed * ps_ref[:, sl]) * (sig * (1.0 + z * (1.0 - sig)))).astype(BF16)
            dps_ref[:, sl] += live * jnp.sum(d_ypool * mixed, axis=0, keepdims=True)
            d_mixed = (d_ypool * ps_ref[:, sl]).astype(BF16)
            dpw_ref[g] += live * _mm_tn(pooled, d_mixed)
            d_pooled = _mm_nt(d_mixed, pw_ref[g])
            scaled = d_pooled * inv
            after = jnp.concatenate([scaled, carry[:, sl]], axis=0)
            carry[:, sl] = jnp.where(i < nt - 1, scaled[:HALO, :], carry[:, sl])
            dh_ref[:, sl] = (_anticausal_window_sum(after, w) - d_pooled).astype(BF16)

        for hd in range(D // PG):
            sl = slice(hd * PG, (hd + 1) * PG)
            vg, dvg = _gelu_and_grad(h_ref[:, 2 * D + hd * PG:2 * D + (hd + 1) * PG].astype(F32))
            vhat, rstd = _norm_rows(vg)
            vn = (vhat * lg_ref[:, sl] + lb_ref[:, sl]).astype(BF16)
            d_vn_chunks = []
            for n in range(tm // CHUNK):
                rs = slice(n * CHUNK, (n + 1) * CHUNK)
                sv = _mm(sw_ref[hd], vn[rs, :]) + sb_ref[:, hd:hd + 1]
                ug, dug = _gelu_and_grad(h_ref[rs, D + hd * PG:D + (hd + 1) * PG].astype(F32))
                z = h_ref[rs, 4 * D + hd * PG:4 * D + (hd + 1) * PG].astype(F32)
                sig = jax.nn.sigmoid(z)
                dy = dy_ref[rs, D + hd * PG:D + (hd + 1) * PG].astype(F32)
                d_ysgu = dy * (z * sig)
                dh_ref[rs, 4 * D + hd * PG:4 * D + (hd + 1) * PG] = (
                    dy * (ug * sv) * (sig * (1.0 + z * (1.0 - sig)))).astype(BF16)
                dh_ref[rs, D + hd * PG:D + (hd + 1) * PG] = (d_ysgu * sv * dug).astype(BF16)
                d_sv = d_ysgu * ug
                dsb_ref[:, hd:hd + 1] += live * jnp.sum(d_sv, axis=1, keepdims=True)
                d_svb = d_sv.astype(BF16)
                dsw_ref[hd] += live * _mm_nt(d_svb, vn[rs, :])
                d_vn_chunks.append(_mm_tn(sw_ref[hd], d_svb))
            d_vn = jnp.concatenate(d_vn_chunks, axis=0)
            dlg_ref[:, sl] += live * jnp.sum(d_vn * vhat, axis=0, keepdims=True)
            dlb_ref[:, sl] += live * jnp.sum(d_vn, axis=0, keepdims=True)
            d_vg = _norm_rows_bwd(d_vn * lg_ref[:, sl], vhat, rstd)
            dh_ref[:, 2 * D + hd * PG:2 * D + (hd + 1) * PG] = (d_vg * dvg).astype(BF16)

        dh_keep[i % 2] = dh_ref[...]

    vec = pl.BlockSpec((1, D), lambda i: (0, 0))
    vec_shape = jax.ShapeDtypeStruct((1, D), F32)

    def whole(shape):
        return pl.BlockSpec(shape, lambda i: (0,) * len(shape))

    return pl.pallas_call(
        body, name="front_backward",
        out_shape=[jax.ShapeDtypeStruct((T, 5 * D), BF16), jax.ShapeDtypeStruct((T, D), F32),
                   jax.ShapeDtypeStruct(pool_w.shape, F32), vec_shape, vec_shape,
                   vec_shape, jax.ShapeDtypeStruct(sgu_wm.shape, F32), jax.ShapeDtypeStruct(sgu_bias_t.shape, F32)],
        grid=(nt + 1,),
        in_specs=[pl.BlockSpec((tm, 5 * D), lambda i: (tile_of(i), 0)),
                  pl.BlockSpec((HALO, D), lambda i: (jnp.maximum(tile_of(i) * hpt - 1, 0), 0)),
                  pl.BlockSpec((tm, 2 * D), lambda i: (tile_of(i), 0)),
                  pl.BlockSpec((tm, D), lambda i: (jnp.minimum(nt - i, nt - 1), 0)), ANY,
                  whole(pool_w.shape), vec, vec, vec, whole(sgu_wm.shape), whole(sgu_bias_t.shape)],
        out_specs=[pl.BlockSpec((tm, 5 * D), lambda i: (tile_of(i), 0)),
                   pl.BlockSpec((tm, D), lambda i: (jnp.minimum(nt - i, nt - 1), 0)),
                   whole(pool_w.shape), vec, vec, vec, whole(sgu_wm.shape), whole(sgu_bias_t.shape)],
        scratch_shapes=[pltpu.VMEM(w_in.shape, BF16), pltpu.VMEM((2, tm, 5 * D), BF16), pltpu.VMEM((HALO, D), F32),
                        pltpu.SemaphoreType.DMA((1,))],
        compiler_params=pltpu.CompilerParams(dimension_semantics=("arbitrary",), vmem_limit_bytes=VMEM_LIMIT),
    )(h, h, d_y, dx_part, w_in, pool_w, pool_scale, sgu_g, sgu_b, sgu_wm, sgu_bias_t)


def _weight_backward(d_h, xt, q, scatter_srcs, tm):
    D, T = xt.shape
    cq = d_h.shape[1] // 4
    hr = D // 2
    nt = T // tm
    ns = len(scatter_srcs)

    def body(q_ref, dh_ref, xt_ref, *refs):
        srcs, out_any, dsts = refs[:ns], refs[ns], refs[ns + 1:2 * ns + 1]
        (acc, land_a, send_b, land_b, mine_f, theirs_f,
         a_send, a_recv, b_send, b_recv, j_sems, o_sems, s_send, s_recv) = refs[2 * ns + 1:]
        s, t = pl.program_id(0), pl.program_id(1)
        x_, y_, c = _place()
        sibling = (x_, y_, 1 - c)
        own_rows = pl.ds(pl.multiple_of(c * hr, hr), hr)
        other_rows = pl.ds(pl.multiple_of((1 - c) * hr, hr), hr)

        @pl.when((s == 0) & (t == 0))
        def _():
            for cp in _scatter_copies(srcs, dsts, s_send, s_recv):
                cp.start()

        @pl.when(t == 0)
        def _():
            acc[...] = jnp.zeros_like(acc)

        for c0 in range(0, cq, MXU_COLS):
            acc[:, c0:c0 + MXU_COLS] += _mm(xt_ref[...], dh_ref[:, c0:c0 + MXU_COLS])

        def pair_sum(slot):
            swap = pltpu.make_async_remote_copy(
                src_ref=acc.at[other_rows], dst_ref=land_a.at[slot], send_sem=a_send.at[slot],
                recv_sem=a_recv.at[slot], device_id=sibling, device_id_type=MESH)
            swap.start()
            swap.wait()
            return acc[own_rows, :] + land_a[slot]

        def to_owner(slot):
            flip_x, flip_y = (slot + 1) >> 1, (slot + 1) & 1
            owner = (1 - x_ if flip_x else x_, 1 - y_ if flip_y else y_, c)
            return pltpu.make_async_remote_copy(
                src_ref=send_b.at[slot], dst_ref=land_b.at[slot], send_sem=b_send.at[slot],
                recv_sem=b_recv.at[slot], device_id=owner, device_id_type=MESH)

        for slot in range(3):
            @pl.when((s == slot) & (t == nt - 1))
            def _(slot=slot):
                send_b[slot] = pair_sum(slot).astype(BF16)
                to_owner(slot).start()

        @pl.when((s == 3) & (t == nt - 1))
        def _():
            own = pair_sum(3)
            for slot in range(3):
                to_owner(slot).wait_recv()
            mine_f[...] = (own + land_b[0].astype(F32)) + (land_b[1].astype(F32) + land_b[2].astype(F32))
            join = pltpu.make_async_remote_copy(
                src_ref=mine_f, dst_ref=theirs_f, send_sem=j_sems.at[0], recv_sem=j_sems.at[1],
                device_id=sibling, device_id_type=MESH)
            join.start()
            out_mine = pltpu.make_async_copy(mine_f, out_any.at[own_rows], o_sems.at[0])
            out_mine.start()
            join.wait()
            out_theirs = pltpu.make_async_copy(theirs_f, out_any.at[other_rows], o_sems.at[1])
            out_theirs.start()
            for slot in range(3):
                to_owner(slot).wait_send()
            for cp in _scatter_copies(srcs, dsts, s_send, s_recv):
                cp.wait()
            out_mine.wait()
            out_theirs.wait()

    def quarter(s, t, q_ref):
        return (t, jnp.where(s == 3, q_ref[0], q_ref[0] ^ (s + 1)))

    dma = pltpu.SemaphoreType.DMA
    return pl.pallas_call(
        body, name="weight_backward",
        out_shape=[jax.ShapeDtypeStruct((D, cq), F32)] + _scatter_shapes(scatter_srcs),
        grid_spec=pltpu.PrefetchScalarGridSpec(
            num_scalar_prefetch=1, grid=(4, nt),
            in_specs=[pl.BlockSpec((tm, cq), quarter), pl.BlockSpec((D, tm), lambda s, t, q_ref: (0, t))] + [ANY] * ns,
            out_specs=[ANY] * (ns + 1),
            scratch_shapes=[pltpu.VMEM((D, cq), F32), pltpu.VMEM((4, hr, cq), F32), pltpu.VMEM((3, hr, cq), BF16),
                            pltpu.VMEM((3, hr, cq), BF16), pltpu.VMEM((hr, cq), F32), pltpu.VMEM((hr, cq), F32),
                            dma((4,)), dma((4,)), dma((3,)), dma((3,)), dma((2,)), dma((2,)), dma((3 * ns,)), dma((3 * ns,))]),
        compiler_params=pltpu.CompilerParams(dimension_semantics=("arbitrary", "arbitrary"),
                                             vmem_limit_bytes=VMEM_LIMIT),
    )(jnp.reshape(q, (1,)).astype(jnp.int32), d_h, xt, *scatter_srcs)


def _swap_with_sibling(grads):
    ns = len(grads)

    def body(*refs):
        copies = _swap_copies(refs[:ns], refs[ns:2 * ns], refs[2 * ns], refs[2 * ns + 1])
        for cp in copies:
            cp.start()
        for cp in copies:
            cp.wait()

    return pl.pallas_call(
        body, name="swap_with_sibling", out_shape=_swap_shapes(grads), in_specs=[ANY] * ns, out_specs=[ANY] * ns,
        scratch_shapes=[pltpu.SemaphoreType.DMA((ns,)), pltpu.SemaphoreType.DMA((ns,))],
    )(*grads)


def _token_tile(T, want):
    return math.gcd(T, want)


def kernel(x, p, w_in, pool_w, pool_scale, sgu_ln_g, sgu_ln_b, sgu_w, sgu_b, w_out, ln_g, ln_b, ple_w, ple_gate_w, ple_gate_b, loss_target, m_w_in, m_pool_w, m_pool_scale, m_sgu_ln_g, m_sgu_ln_b, m_sgu_w, m_sgu_b, m_w_out, m_ln_g, m_ln_b, m_ple_w, m_ple_gate_w, m_ple_gate_b, v_w_in, v_pool_w, v_pool_scale, v_sgu_ln_g, v_sgu_ln_b, v_sgu_w, v_sgu_b, v_w_out, v_ln_g, v_ln_b, v_ple_w, v_ple_gate_w, v_ple_gate_b):
    c = lax.axis_index("c")
    T, D = x.shape[1], x.shape[2]
    tm, tm_vpu, tm_acc = _token_tile(T, 512), _token_tile(T, 256), _token_tile(T, 2048)
    x2, p2, tgt = x[0], p[0, 0], loss_target[0]
    G, PGQ, PG = pool_w.shape[1], pool_w.shape[2], pool_w.shape[3]

    w_in_f, pool_f = _gather_weights([w_in[0], pool_w[0].reshape(G * PGQ, PG)])
    pool_f = pool_f.reshape(4, G, PGQ, PG).transpose(1, 0, 2, 3).reshape(G, 4 * PGQ, PG)
    tril = jnp.tril(jnp.ones((CHUNK, CHUNK), dtype=bool))
    sgu_wm = jnp.where(tril[None], sgu_w[0], 0.0).astype(BF16)
    sgu_bias_t = sgu_b[0].T

    h, y, xt, w_out_f, w_gate_f, w_ple_f = _front_forward(
        x2, w_in_f, pool_f, pool_scale, sgu_ln_g, sgu_ln_b, sgu_wm, sgu_bias_t, [w_out[0], ple_gate_w[0], ple_w[0]], tm)
    w_out_f = w_out_f.reshape(-1, D)
    w_gate_f = w_gate_f.reshape(-1, D)
    (dx_part, d_y, d_w_out, d_w_gate, d_w_ple, d_ln_g, d_ln_b, d_gate_b, ssq) = _tail(
        y, x2, p2, tgt, w_out_f, w_gate_f, w_ple_f, ln_g, ln_b, ple_gate_b, tm)
    d_h, d_x, d_pool_w, d_pool_scale, d_sgu_g, d_sgu_b, d_sgu_w, d_sgu_bias_t = _front_backward(
        h, d_y, dx_part, w_in_f, pool_f, pool_scale, sgu_ln_g, sgu_ln_b, sgu_wm, sgu_bias_t, tm_vpu)
    loss = lax.psum((0.5 / D) * jnp.sum(ssq), ("x", "y", "c"))

    grads = [d_w_out.reshape(4, -1, D), d_w_gate.reshape(4, -1, D), d_w_ple,
             d_pool_w.reshape(G, 4, PGQ, PG).transpose(1, 0, 2, 3).reshape(4, G * PGQ, PG)]
    grads = [g.reshape(4, 2, g.shape[1] // 2, g.shape[2]) for g in grads]
    d_sgu_w = jnp.where(tril[None], d_sgu_w, 0.0)
    small_names = ["pool_scale", "sgu_ln_g", "sgu_ln_b", "ln_g", "ln_b", "ple_gate_b", "sgu_b", "sgu_w"]
    small_grads = [d_pool_scale, d_sgu_g, d_sgu_b, d_ln_g, d_ln_b, d_gate_b, d_sgu_bias_t.T, d_sgu_w]

    def pack(arrays):
        rows = [a.reshape(-1) for a in arrays[:6]] + [jnp.pad(arrays[6].reshape(-1), (0, 2 * D - arrays[6].size))]
        return jnp.concatenate([r.reshape(-1, D) for r in rows] + [arrays[7].reshape(-1, D)], axis=0)

    def unpack(packed, like):
        out = [packed[k].reshape(like[k].shape) for k in range(6)]
        out.append(packed[6, :like[6].size].reshape(like[6].shape))
        out.append(packed[8:].reshape(like[7].shape))
        return out

    small = pack(small_grads)
    q = 2 * lax.axis_index("x") + lax.axis_index("y")
    *landed, small_landed = _swap_with_sibling(grads + [small])
    parts = [_add_own_half(g, l, c) for g, l in zip(grads, landed)]
    small_chip = _add_pairs(small, small_landed, "add_small")
    d_w_in, *slots, small_slots = _weight_backward(d_h, xt, q, parts + [small_chip], tm_acc)
    halves = [_sum_four(pt, s, q, "sum_four_%d" % k) for k, (pt, s) in enumerate(zip(parts, slots))]
    small_total = _sum_four(small_chip, small_slots, q, "sum_four_small")
    sibling_halves = _join_halves_with_sibling(halves)

    big_names = ["w_out", "ple_gate_w", "ple_w", "pool_w"]
    given = dict(w_in=(w_in, m_w_in, v_w_in), w_out=(w_out, m_w_out, v_w_out),
                 ple_gate_w=(ple_gate_w, m_ple_gate_w, v_ple_gate_w), ple_w=(ple_w, m_ple_w, v_ple_w),
                 pool_w=(pool_w, m_pool_w, v_pool_w), pool_scale=(pool_scale, m_pool_scale, v_pool_scale),
                 sgu_ln_g=(sgu_ln_g, m_sgu_ln_g, v_sgu_ln_g), sgu_ln_b=(sgu_ln_b, m_sgu_ln_b, v_sgu_ln_b),
                 sgu_w=(sgu_w, m_sgu_w, v_sgu_w), sgu_b=(sgu_b, m_sgu_b, v_sgu_b), ln_g=(ln_g, m_ln_g, v_ln_g),
                 ln_b=(ln_b, m_ln_b, v_ln_b), ple_gate_b=(ple_gate_b, m_ple_gate_b, v_ple_gate_b))
    grad, delta, new_m, new_v = {}, {}, {}, {}
    for name, g_mine, g_sibling in zip(big_names, halves, sibling_halves):
        w, m, v = given[name]
        flat = (2 * g_mine.shape[0], g_mine.shape[1])
        outs = _adamw_joined(w.reshape(flat), g_mine, g_sibling, m.reshape(flat), v.reshape(flat), c, "adamw_" + name)
        grad[name], delta[name], new_m[name], new_v[name] = (t.reshape(w.shape) for t in outs)
    grad["w_in"], delta["w_in"], new_m["w_in"], new_v["w_in"] = (
        t[None] for t in _adamw(w_in[0], d_w_in, m_w_in[0], v_w_in[0], "adamw_w_in"))
    small_w, small_m, small_v = (pack([given[n][k] for n in small_names]) for k in range(3))
    small_out = _adamw(small_w, small_total, small_m, small_v, "adamw_small")
    like = [given[n][0] for n in small_names]
    for k, name in enumerate(small_names):
        grad[name], delta[name], new_m[name], new_v[name] = (unpack(t, like)[k] for t in small_out)

    order = ["w_in", "pool_w", "pool_scale", "sgu_ln_g", "sgu_ln_b", "sgu_w", "sgu_b", "w_out", "ln_g", "ln_b",
             "ple_w", "ple_gate_w", "ple_gate_b"]
    return (loss, d_x[None], *[grad[n] for n in order], *[delta[n] for n in order],
            *[new_m[n] for n in order], *[new_v[n] for n in order])
```

```python
import functools
import math

import jax
import jax.numpy as jnp
from jax import lax
from jax.experimental import pallas as pl
from jax.experimental.pallas import tpu as pltpu

F32, BF16 = jnp.float32, jnp.bfloat16
MESH = pl.DeviceIdType.MESH
ANY = pl.BlockSpec(memory_space=pl.ANY)

POOL_WINDOWS = (2, 4, 8, 16)
HALO = 16
CHUNK = 128
MXU_COLS = 256
LN_EPS = 1e-5
DEEPNORM_ALPHA = 2.0 ** 0.25
ADAM_LR, ADAM_B1, ADAM_B2, ADAM_EPS, ADAM_WD, ADAM_STEP = 1e-3, 0.9, 0.999, 1e-8, 0.01, 10
VMEM_LIMIT = 56 * 1024 * 1024
GELU_K = math.sqrt(2.0 / math.pi)
GELU_C = 0.044715
SAVED = {"pooled": 0, "gelu_u": 1, "dgelu_u": 2, "vhat": 3, "rstd_dgelu_v": 4, "silu": 5, "dsilu": 7}
SAVED_WIDTH = 9


def _mm(a, b):
    return jnp.dot(a, b, preferred_element_type=F32)


def _mm_nt(a, b):
    return lax.dot_general(a, b, (((1,), (1,)), ((), ())), preferred_element_type=F32)


def _mm_tn(a, b):
    return lax.dot_general(a, b, (((0,), (0,)), ((), ())), preferred_element_type=F32)


def _gelu_and_grad(x):
    x2 = x * x
    t = jnp.tanh(x * (GELU_K + (GELU_K * GELU_C) * x2))
    hx = 0.5 * x
    g = hx + hx * t
    dg = (0.5 + 0.5 * t) + (hx - hx * t * t) * (GELU_K + (3.0 * GELU_K * GELU_C) * x2)
    return g, dg


def _silu_and_grad(z):
    sig = jax.nn.sigmoid(z)
    zs = z * sig
    return zs, sig + zs * (1.0 - sig)


def _norm_rows(x):
    mu = jnp.mean(x, axis=-1, keepdims=True)
    xc = x - mu
    var = jnp.mean(xc * xc, axis=-1, keepdims=True)
    rstd = lax.rsqrt(var + LN_EPS)
    return xc * rstd, rstd


def _norm_rows_bwd(dxhat, xhat, rstd):
    m1 = jnp.mean(dxhat, axis=-1, keepdims=True)
    m2 = jnp.mean(dxhat * xhat, axis=-1, keepdims=True)
    return rstd * (dxhat - m1 - xhat * m2)


def _inv_count(row0, rows, w):
    t = row0 + lax.broadcasted_iota(jnp.int32, (rows, 1), 0)
    return 1.0 / jnp.minimum(t + 1, w).astype(F32)


def _causal_window_sum(ext, w):
    s, sh = ext, 1
    while sh < w:
        s = s + pltpu.roll(s, sh, axis=0)
        sh *= 2
    return s[HALO:, :]


def _anticausal_window_sum(ext, w):
    n, s, sh = ext.shape[0], ext, 1
    while sh < w:
        s = s + pltpu.roll(s, n - sh, axis=0)
        sh *= 2
    return s[: n - HALO, :]


def _place():
    return lax.axis_index("x"), lax.axis_index("y"), lax.axis_index("c")


def _gather_weights(shards):
    n = len(shards)
    half = [s.shape[0] // 2 for s in shards]

    def body(*refs):
        wide, dsts, srcs = refs[:n], refs[n:2 * n], refs[2 * n:3 * n]
        send_sems, recv_sems, local_sems = refs[3 * n:]
        for k in range(n):
            for r0 in range(0, 2 * half[k], CHUNK):
                srcs[k][r0:r0 + CHUNK, :] = wide[k][r0:r0 + CHUNK, :].astype(BF16)
        x, y, c = _place()
        q = 2 * x + y
        sibling = (x, y, 1 - c)
        chips = [(1 - x, y), (x, 1 - y), (1 - x, 1 - y)]

        def rows(k, qq, cc):
            return dsts[k].at[qq, pl.ds(cc * half[k], half[k])]

        def copy(k, sem, qq, cc, to, src=None):
            return pltpu.make_async_remote_copy(
                src_ref=rows(k, qq, cc) if src is None else src, dst_ref=rows(k, qq, cc),
                send_sem=send_sems.at[6 * k + sem], recv_sem=recv_sems.at[6 * k + sem],
                device_id=to, device_id_type=MESH)

        mine = [pltpu.make_async_copy(srcs[k], dsts[k].at[q], local_sems.at[k]) for k in range(n)]
        for cp in mine:
            cp.start()
        started = []
        for k in range(n):
            own = srcs[k].at[pl.ds(c * half[k], half[k])]
            for j, chip in enumerate(chips):
                started.append(copy(k, j, q, c, (*chip, c), src=own))
                started[-1].start()
        for k in range(n):
            for j, chip in enumerate(chips):
                qq = 2 * chip[0] + chip[1]
                copy(k, j, qq, c, (x, y, c)).wait_recv()
                started.append(copy(k, 3 + j, qq, c, sibling))
                started[-1].start()
        for k in range(n):
            for j, chip in enumerate(chips):
                copy(k, 3 + j, 2 * chip[0] + chip[1], 1 - c, (x, y, c)).wait_recv()
        for cp in started:
            cp.wait_send()
        for cp in mine:
            cp.wait()

    return pl.pallas_call(
        body, name="gather_weights",
        out_shape=[jax.ShapeDtypeStruct((4,) + s.shape, BF16) for s in shards],
        in_specs=[pl.BlockSpec(memory_space=pltpu.VMEM)] * n, out_specs=[ANY] * n,
        scratch_shapes=[pltpu.VMEM(s.shape, BF16) for s in shards]
        + [pltpu.SemaphoreType.DMA((6 * n,)), pltpu.SemaphoreType.DMA((6 * n,)), pltpu.SemaphoreType.DMA((n,))],
        compiler_params=pltpu.CompilerParams(vmem_limit_bytes=VMEM_LIMIT),
    )(*shards)


def _direct_gather_copies(srcs, dsts, send_sems, recv_sems):
    x, y, c = _place()
    q = 2 * x + y
    sends, recvs = [], []
    for k, (src, dst) in enumerate(zip(srcs, dsts)):
        half = src.shape[0] // 2
        for j, chip in enumerate([(1 - x, y), (x, 1 - y), (1 - x, 1 - y)]):
            for core in range(2):
                sends.append(pltpu.make_async_remote_copy(
                    src_ref=src.at[pl.ds(c * half, half)], dst_ref=dst.at[q, pl.ds(c * half, half)],
                    send_sem=send_sems.at[6 * k + 2 * j + core], recv_sem=recv_sems.at[6 * k + 2 * j + c],
                    device_id=(*chip, core), device_id_type=MESH))
                landed = dst.at[2 * chip[0] + chip[1], pl.ds(core * half, half)]
                recvs.append(pltpu.make_async_remote_copy(
                    src_ref=landed, dst_ref=landed, send_sem=send_sems.at[6 * k + 2 * j + core],
                    recv_sem=recv_sems.at[6 * k + 2 * j + core], device_id=(x, y, c), device_id_type=MESH))
    return sends, recvs


def _swap_copies(srcs, dsts, send_sems, recv_sems):
    x, y, c = _place()
    return [pltpu.make_async_remote_copy(
        src_ref=src.at[:, 1 - c] if len(src.shape) == 4 else src, dst_ref=dst,
        send_sem=send_sems.at[k], recv_sem=recv_sems.at[k], device_id=(x, y, 1 - c), device_id_type=MESH)
        for k, (src, dst) in enumerate(zip(srcs, dsts))]


def _swap_shapes(grads):
    return [jax.ShapeDtypeStruct((4,) + g.shape[2:] if g.ndim == 4 else g.shape, g.dtype) for g in grads]


def _scatter_copies(srcs, dsts, send_sems, recv_sems):
    x, y, c = _place()
    copies = []
    for j, chip in enumerate([(1 - x, y), (x, 1 - y), (1 - x, 1 - y)]):
        for k, (src, dst) in enumerate(zip(srcs, dsts)):
            copies.append(pltpu.make_async_remote_copy(
                src_ref=src.at[2 * chip[0] + chip[1]] if len(src.shape) == 3 else src, dst_ref=dst.at[j],
                send_sem=send_sems.at[3 * k + j], recv_sem=recv_sems.at[3 * k + j],
                device_id=(*chip, c), device_id_type=MESH))
    return copies


def _scatter_shapes(parts):
    return [jax.ShapeDtypeStruct((3,) + (p.shape[1:] if p.ndim == 3 else p.shape), p.dtype) for p in parts]


def _join_halves_with_sibling(halves):
    n = len(halves)

    def body(*refs):
        srcs, dsts = refs[:n], refs[n:2 * n]
        send_sems, recv_sems = refs[2 * n:]
        x, y, c = _place()
        copies = [pltpu.make_async_remote_copy(
            src_ref=srcs[k], dst_ref=dsts[k], send_sem=send_sems.at[k], recv_sem=recv_sems.at[k],
            device_id=(x, y, 1 - c), device_id_type=MESH) for k in range(n)]
        for cp in copies:
            cp.start()
        for cp in copies:
            cp.wait()

    return pl.pallas_call(
        body, name="join_halves",
        out_shape=[jax.ShapeDtypeStruct(h.shape, h.dtype) for h in halves],
        in_specs=[ANY] * n, out_specs=[ANY] * n,
        scratch_shapes=[pltpu.SemaphoreType.DMA((n,)), pltpu.SemaphoreType.DMA((n,))],
    )(*halves)


def _row_block(rows, cols, n_arrays):
    cap = max(8, (VMEM_LIMIT // 4) // (8 * n_arrays * cols))
    rb = rows
    while rb > cap and rb % 2 == 0:
        rb //= 2
    return rb


def _add_own_half(grad, landed, c):
    _, _, hr, cols = grad.shape
    rb = _row_block(hr, cols, 3)

    def body(c_ref, g_ref, l_ref, o_ref):
        o_ref[...] = (g_ref[...] + l_ref[...]).astype(BF16)

    return pl.pallas_call(
        body, name="add_own_half",
        out_shape=jax.ShapeDtypeStruct(landed.shape, BF16),
        grid_spec=pltpu.PrefetchScalarGridSpec(
            num_scalar_prefetch=1, grid=(4, hr // rb),
            in_specs=[pl.BlockSpec((None, None, rb, cols), lambda qq, r, c_ref: (qq, c_ref[0], r, 0)),
                      pl.BlockSpec((None, rb, cols), lambda qq, r, c_ref: (qq, r, 0))],
            out_specs=pl.BlockSpec((None, rb, cols), lambda qq, r, c_ref: (qq, r, 0))),
    )(jnp.reshape(c, (1,)).astype(jnp.int32), grad, landed)


def _add_pairs(a, b, name):
    rows, cols = a.shape
    rb = _row_block(rows, cols, 3)

    def body(a_ref, b_ref, o_ref):
        o_ref[...] = a_ref[...] + b_ref[...]

    spec = pl.BlockSpec((rb, cols), lambda r: (r, 0))
    return pl.pallas_call(body, name=name, out_shape=jax.ShapeDtypeStruct(a.shape, F32), grid=(rows // rb,),
                          in_specs=[spec, spec], out_specs=spec)(a, b)


def _sum_four(own, slots, q, name):
    _, rows, cols = slots.shape
    rb = _row_block(rows, cols, 5)

    def body(q_ref, own_ref, s_ref, o_ref):
        o_ref[...] = ((own_ref[...].astype(F32) + s_ref[0].astype(F32))
                      + (s_ref[1].astype(F32) + s_ref[2].astype(F32)))

    if own.ndim == 3:
        own_spec = pl.BlockSpec((None, rb, cols), lambda r, q_ref: (q_ref[0], r, 0))
    else:
        own_spec = pl.BlockSpec((rb, cols), lambda r, q_ref: (r, 0))
    return pl.pallas_call(
        body, name=name, out_shape=jax.ShapeDtypeStruct((rows, cols), F32),
        grid_spec=pltpu.PrefetchScalarGridSpec(
            num_scalar_prefetch=1, grid=(rows // rb,),
            in_specs=[own_spec, pl.BlockSpec((3, rb, cols), lambda r, q_ref: (0, r, 0))],
            out_specs=pl.BlockSpec((rb, cols), lambda r, q_ref: (r, 0))),
    )(jnp.reshape(q, (1,)).astype(jnp.int32), own, slots)


def _adamw_math(w, g, m, v):
    nm = ADAM_B1 * m + (1.0 - ADAM_B1) * g
    nv = ADAM_B2 * v + (1.0 - ADAM_B2) * (g * g)
    m_hat = nm / (1.0 - ADAM_B1 ** ADAM_STEP)
    v_hat = nv / (1.0 - ADAM_B2 ** ADAM_STEP)
    return -ADAM_LR * (m_hat / (jnp.sqrt(v_hat) + ADAM_EPS) + ADAM_WD * w), nm, nv


def _adamw(w, g, m, v, name):
    rows, cols = w.shape
    rb = _row_block(rows, cols, 8)

    def body(w_ref, g_ref, m_ref, v_ref, go_ref, d_ref, nm_ref, nv_ref):
        go_ref[...] = g_ref[...]
        d_ref[...], nm_ref[...], nv_ref[...] = _adamw_math(w_ref[...], g_ref[...], m_ref[...], v_ref[...])

    spec = pl.BlockSpec((rb, cols), lambda r: (r, 0))
    out = jax.ShapeDtypeStruct(w.shape, F32)
    return pl.pallas_call(body, name=name, out_shape=[out] * 4, grid=(rows // rb,),
                          in_specs=[spec] * 4, out_specs=[spec] * 4)(w, g, m, v)


def _adamw_joined(w, g_mine, g_sibling, m, v, c, name):
    rows, cols = w.shape
    rb = _row_block(rows // 2, cols, 9)
    nb = rows // 2 // rb

    def body(c_ref, w_ref, gm_ref, gs_ref, m_ref, v_ref, g_ref, d_ref, nm_ref, nv_ref):
        g = jnp.where(c_ref[0] == pl.program_id(0), gm_ref[...], gs_ref[...])
        g_ref[...] = g
        d_ref[...], nm_ref[...], nv_ref[...] = _adamw_math(w_ref[...], g, m_ref[...], v_ref[...])

    full = pl.BlockSpec((rb, cols), lambda hf, r, c_ref: (hf * nb + r, 0))
    part = pl.BlockSpec((rb, cols), lambda hf, r, c_ref: (r, 0))
    out = jax.ShapeDtypeStruct(w.shape, F32)
    return pl.pallas_call(
        body, name=name, out_shape=[out] * 4,
        grid_spec=pltpu.PrefetchScalarGridSpec(
            num_scalar_prefetch=1, grid=(2, nb),
            in_specs=[full, part, part, full, full], out_specs=[full] * 4),
    )(jnp.reshape(c, (1,)).astype(jnp.int32), w, g_mine, g_sibling, m, v)


def _front_forward(x, w_in, pool_w, pool_scale, sgu_g, sgu_b, sgu_wm, sgu_bias_t, later_shards, tm):
    T, D = x.shape
    nq, _, cq = w_in.shape
    G, PG = pool_w.shape[0], pool_w.shape[1]
    nt = T // tm
    bpd = D // PG
    nl = len(later_shards)

    def body(x_ref, win_any, pw_any, ps_ref, lg_ref, lb_ref, sw_ref, sb_ref, *refs):
        shards_any, (keep_ref, y_ref, xt_ref), gathered = refs[:nl], refs[nl:nl + 3], refs[nl + 3:2 * nl + 3]
        wide, narrow = refs[2 * nl + 3:3 * nl + 3], refs[3 * nl + 3:4 * nl + 3]
        win_v, pw_v, carry, sems, load_sems, own_sems, send_sems, recv_sems = refs[4 * nl + 3:]
        i = pl.program_id(0)
        own_quarter = 2 * lax.axis_index("x") + lax.axis_index("y")

        def own_copies():
            return [pltpu.make_async_copy(narrow[k], gathered[k].at[own_quarter], own_sems.at[k]) for k in range(nl)]

        @pl.when(i == 0)
        def _():
            c1 = pltpu.make_async_copy(win_any, win_v, sems.at[0])
            c2 = pltpu.make_async_copy(pw_any, pw_v, sems.at[1])
            loads = [pltpu.make_async_copy(shards_any[k], wide[k], load_sems.at[k]) for k in range(nl)]
            for cp in [c1, c2] + loads:
                cp.start()
            carry[...] = jnp.zeros_like(carry)
            for k in range(nl):
                loads[k].wait()
                for r0 in range(0, wide[k].shape[0], CHUNK):
                    narrow[k][r0:r0 + CHUNK, :] = wide[k][r0:r0 + CHUNK, :].astype(BF16)
            for cp in own_copies() + _direct_gather_copies(narrow, gathered, send_sems, recv_sems)[0]:
                cp.start()
            c1.wait()
            c2.wait()

        xb = x_ref[...].astype(BF16)
        xt_ref[...] = x_ref[...].T.astype(BF16)

        def h_block(j):
            qq, off = divmod(j * PG, cq)
            return _mm(xb, win_v[qq, :, off:off + PG])

        def keep(part, col, value):
            keep_ref[:, SAVED[part] * D + col:SAVED[part] * D + col + PG] = value.astype(BF16)

        def ahead(stage):
            if stage < G:
                return h_block(stage), h_block(3 * bpd + stage)
            if stage < G + bpd:
                hd = stage - G
                return h_block(bpd + hd), h_block(2 * bpd + hd), h_block(4 * bpd + hd)
            return None

        blocks = ahead(0)

        for g, w in enumerate(POOL_WINDOWS):
            sl = slice(g * PG, (g + 1) * PG)
            a, z = blocks
            blocks = ahead(g + 1)
            ext = jnp.concatenate([carry[:, sl], a], axis=0)
            carry[:, sl] = a[tm - HALO:, :]
            pooled = (_causal_window_sum(ext, w) * _inv_count(i * tm, tm, w) - a).astype(BF16)
            mixed = _mm(pooled, pw_v[g])
            zs, dzs = _silu_and_grad(z)
            keep("pooled", g * PG, pooled)
            keep("silu", g * PG, zs)
            keep("dsilu", g * PG, dzs)
            y_ref[:, sl] = (mixed * ps_ref[:, sl] * zs).astype(BF16)

        for hd in range(bpd):
            sl = slice(hd * PG, (hd + 1) * PG)
            u, v, z = blocks
            blocks = ahead(G + hd + 1)
            ug, dug = _gelu_and_grad(u)
            vg, dvg = _gelu_and_grad(v)
            vhat, rstd = _norm_rows(vg)
            zs, dzs = _silu_and_grad(z)
            keep("gelu_u", hd * PG, ug)
            keep("dgelu_u", hd * PG, dug)
            keep("vhat", hd * PG, vhat)
            keep("rstd_dgelu_v", hd * PG, rstd * dvg)
            keep("silu", D + hd * PG, zs)
            keep("dsilu", D + hd * PG, dzs)
            vn = (vhat * lg_ref[:, sl] + lb_ref[:, sl]).astype(BF16)
            gated = ug * zs
            for n in range(tm // CHUNK):
                rs = slice(n * CHUNK, (n + 1) * CHUNK)
                sv = _mm(sw_ref[hd], vn[rs, :]) + sb_ref[:, hd:hd + 1]
                y_ref[rs, D + hd * PG:D + (hd + 1) * PG] = (gated[rs, :] * sv).astype(BF16)

        @pl.when(i == nt - 1)
        def _():
            sends, recvs = _direct_gather_copies(narrow, gathered, send_sems, recv_sems)
            for cp in sends:
                cp.wait_send()
            for cp in recvs:
                cp.wait_recv()
            for cp in own_copies():
                cp.wait()

    vec = pl.BlockSpec((1, D), lambda i: (0, 0))
    return pl.pallas_call(
        body, name="front_forward",
        out_shape=[jax.ShapeDtypeStruct((T, SAVED_WIDTH * D), BF16), jax.ShapeDtypeStruct((T, 2 * D), BF16),
                   jax.ShapeDtypeStruct((D, T), BF16)]
        + [jax.ShapeDtypeStruct((4,) + s.shape, BF16) for s in later_shards],
        grid=(nt,),
        in_specs=[pl.BlockSpec((tm, D), lambda i: (i, 0)), ANY, ANY, vec, vec, vec,
                  pl.BlockSpec(sgu_wm.shape, lambda i: (0, 0, 0)), pl.BlockSpec(sgu_bias_t.shape, lambda i: (0, 0))]
        + [ANY] * nl,
        out_specs=[pl.BlockSpec((tm, SAVED_WIDTH * D), lambda i: (i, 0)), pl.BlockSpec((tm, 2 * D), lambda i: (i, 0)),
                   pl.BlockSpec((D, tm), lambda i: (0, i))] + [ANY] * nl,
        scratch_shapes=[pltpu.VMEM(s.shape, F32) for s in later_shards]
        + [pltpu.VMEM(s.shape, BF16) for s in later_shards]
        + [pltpu.VMEM(w_in.shape, BF16), pltpu.VMEM(pool_w.shape, BF16), pltpu.VMEM((HALO, D), F32),
           pltpu.SemaphoreType.DMA((2,)), pltpu.SemaphoreType.DMA((nl,)), pltpu.SemaphoreType.DMA((nl,)),
           pltpu.SemaphoreType.DMA((6 * nl,)), pltpu.SemaphoreType.DMA((6 * nl,))],
        compiler_params=pltpu.CompilerParams(dimension_semantics=("arbitrary",), vmem_limit_bytes=VMEM_LIMIT),
    )(x, w_in, pool_w, pool_scale, sgu_g, sgu_b, sgu_wm, sgu_bias_t, *later_shards)


def _tail(y, x, p, target, w_out, w_gate, w_ple, ln_g, ln_b, gate_b, tm):
    T, D = x.shape
    K = p.shape[1]
    nq, _, cq = w_ple.shape
    nt = T // tm

    def body(y_ref, x_ref, p_ref, t_ref, wout_any, wg_any, wp_any, lng_ref, lnb_ref, bg_ref,
             dxp_ref, dy_ref, dwout_any, dwg_any, dwp_any, dlng_ref, dlnb_ref, dbg_ref, ssq_ref,
             wout_v, wg_v, wp_v, dwout_acc, dwg_acc, dwp_acc, sems):
        i = pl.program_id(0)

        @pl.when(i == 0)
        def _():
            loads = [pltpu.make_async_copy(s, d, sems.at[k])
                     for k, (s, d) in enumerate(((wout_any, wout_v), (wg_any, wg_v), (wp_any, wp_v)))]
            for cp in loads:
                cp.start()
            for ref in (dwout_acc, dwg_acc, dwp_acc, dlng_ref, dlnb_ref, dbg_ref, ssq_ref):
                ref[...] = jnp.zeros_like(ref)
            for cp in loads:
                cp.wait()

        yb = y_ref[...]
        pb = p_ref[...].astype(BF16)
        xhat, rstd = _norm_rows(DEEPNORM_ALPHA * x_ref[...] + _mm(yb, wout_v[...]))
        x1 = xhat * lng_ref[...] + lnb_ref[...]
        x1b = x1.astype(BF16)
        gate = jax.nn.sigmoid(_mm(x1b, wg_v[...]) + bg_ref[...])
        e = jnp.concatenate([_mm(pb, wp_v[qq]) for qq in range(nq)], axis=1)
        diff = x1 + gate * e - t_ref[...]
        ssq_ref[...] += jnp.sum(diff * diff, axis=0, keepdims=True)

        dout = diff * (1.0 / D)
        d_e = (dout * gate).astype(BF16)
        dgl = dout * e * gate * (1.0 - gate)
        dglb = dgl.astype(BF16)
        for qq in range(nq):
            dwp_acc[qq] += _mm_tn(pb, d_e[:, qq * cq:(qq + 1) * cq])
        for c0 in range(0, D, MXU_COLS):
            dwg_acc[:, c0:c0 + MXU_COLS] += _mm_tn(x1b, dglb[:, c0:c0 + MXU_COLS])
        dbg_ref[...] += jnp.sum(dgl, axis=0, keepdims=True)
        d_x1 = dout + _mm_nt(dglb, wg_v[...])
        dlng_ref[...] += jnp.sum(d_x1 * xhat, axis=0, keepdims=True)
        dlnb_ref[...] += jnp.sum(d_x1, axis=0, keepdims=True)
        d_r = _norm_rows_bwd(d_x1 * lng_ref[...], xhat, rstd)
        drb = d_r.astype(BF16)
        dxp_ref[...] = DEEPNORM_ALPHA * d_r
        for c0 in range(0, D, MXU_COLS):
            dwout_acc[:, c0:c0 + MXU_COLS] += _mm_tn(yb, drb[:, c0:c0 + MXU_COLS])
        for c0 in range(0, 2 * D, 2 * MXU_COLS):
            dy_ref[:, c0:c0 + 2 * MXU_COLS] = _mm_nt(drb, wout_v[c0:c0 + 2 * MXU_COLS, :]).astype(BF16)

        @pl.when(i == nt - 1)
        def _():
            stores = [pltpu.make_async_copy(s, d, sems.at[k])
                      for k, (s, d) in enumerate(((dwout_acc, dwout_any), (dwg_acc, dwg_any), (dwp_acc, dwp_any)))]
            for cp in stores:
                cp.start()
            for cp in stores:
                cp.wait()

    vec = pl.BlockSpec((1, D), lambda i: (0, 0))
    vec_shape = jax.ShapeDtypeStruct((1, D), F32)

    def tile(cols):
        return pl.BlockSpec((tm, cols), lambda i: (i, 0))

    return pl.pallas_call(
        body, name="tail",
        out_shape=[jax.ShapeDtypeStruct((T, D), F32), jax.ShapeDtypeStruct((T, 2 * D), BF16),
                   jax.ShapeDtypeStruct(w_out.shape, F32), jax.ShapeDtypeStruct(w_gate.shape, F32),
                   jax.ShapeDtypeStruct(w_ple.shape, F32), vec_shape, vec_shape, vec_shape, vec_shape],
        grid=(nt,),
        in_specs=[tile(2 * D), tile(D), tile(K), tile(D), ANY, ANY, ANY, vec, vec, vec],
        out_specs=[tile(D), tile(2 * D), ANY, ANY, ANY, vec, vec, vec, vec],
        scratch_shapes=[pltpu.VMEM(w_out.shape, BF16), pltpu.VMEM(w_gate.shape, BF16), pltpu.VMEM(w_ple.shape, BF16),
                        pltpu.VMEM(w_out.shape, F32), pltpu.VMEM(w_gate.shape, F32), pltpu.VMEM(w_ple.shape, F32),
                        pltpu.SemaphoreType.DMA((3,))],
        compiler_params=pltpu.CompilerParams(dimension_semantics=("arbitrary",), vmem_limit_bytes=VMEM_LIMIT),
    )(y, x, p, target, w_out, w_gate, w_ple, ln_g, ln_b, gate_b)


def _front_backward(kept, d_y, dx_part, w_in, pool_w, pool_scale, sgu_g, sgu_b, sgu_wm, sgu_bias_t, tm):
    T = kept.shape[0]
    D = kept.shape[1] // SAVED_WIDTH
    nq, _, cq = w_in.shape
    G, PG = pool_w.shape[0], pool_w.shape[1]
    nt = T // tm

    def tile_of(i):
        return nt - 1 - jnp.minimum(i, nt - 1)

    def body(kept_ref, dy_ref, dxp_ref, win_any, pw_ref, ps_ref, lg_ref, lb_ref, sw_ref, sb_ref,
             dh_ref, dx_ref, dpw_ref, dps_ref, dlg_ref, dlb_ref, dsw_ref, dsb_ref, win_v, dh_keep, carry, sems):
        i = pl.program_id(0)
        ti = tile_of(i)
        live = (i < nt).astype(F32)

        def saved(part, col, rows=slice(None)):
            return kept_ref[rows, SAVED[part] * D + col:SAVED[part] * D + col + PG]

        @pl.when(i == 0)
        def _():
            cp = pltpu.make_async_copy(win_any, win_v, sems.at[0])
            cp.start()
            carry[...] = jnp.zeros_like(carry)
            dh_keep[...] = jnp.zeros_like(dh_keep)
            for ref in (dpw_ref, dps_ref, dlg_ref, dlb_ref, dsw_ref, dsb_ref):
                ref[...] = jnp.zeros_like(ref)
            cp.wait()

        def dx_columns(r0):
            dx = dxp_ref[:, r0:r0 + MXU_COLS]
            for qq in range(nq):
                dx = dx + _mm_nt(dh_keep[(i + 1) % 2, :, qq * cq:(qq + 1) * cq], win_v[qq, r0:r0 + MXU_COLS, :])
            dx_ref[:, r0:r0 + MXU_COLS] = dx

        dx_chunks = list(range(0, D, MXU_COLS))
        stages = G + D // PG

        for g, w in enumerate(POOL_WINDOWS):
            for r0 in dx_chunks[g * len(dx_chunks) // stages:(g + 1) * len(dx_chunks) // stages]:
                dx_columns(r0)
            sl = slice(g * PG, (g + 1) * PG)
            pooled = saved("pooled", g * PG)
            mixed = _mm(pooled, pw_ref[g])
            dy = dy_ref[:, sl].astype(F32)
            d_ypool = dy * saved("silu", g * PG).astype(F32)
            dh_ref[:, 3 * D + g * PG:3 * D + (g + 1) * PG] = (
                dy * (mixed * ps_ref[:, sl]) * saved("dsilu", g * PG).astype(F32)).astype(BF16)
            dps_ref[:, sl] += live * jnp.sum(d_ypool * mixed, axis=0, keepdims=True)
            d_mixed = (d_ypool * ps_ref[:, sl]).astype(BF16)
            dpw_ref[g] += live * _mm_tn(pooled, d_mixed)
            d_pooled = _mm_nt(d_mixed, pw_ref[g])
            scaled = d_pooled * _inv_count(ti * tm, tm, w)
            after = jnp.concatenate([scaled, carry[:, sl]], axis=0)
            carry[:, sl] = jnp.where(i < nt - 1, scaled[:HALO, :], carry[:, sl])
            dh_ref[:, sl] = (_anticausal_window_sum(after, w) - d_pooled).astype(BF16)

        for hd in range(D // PG):
            for r0 in dx_chunks[(G + hd) * len(dx_chunks) // stages:(G + hd + 1) * len(dx_chunks) // stages]:
                dx_columns(r0)
            sl = slice(hd * PG, (hd + 1) * PG)
            vhat = saved("vhat", hd * PG).astype(F32)
            vn = (vhat * lg_ref[:, sl] + lb_ref[:, sl]).astype(BF16)
            d_vn_chunks = []
            for n in range(tm // CHUNK):
                rs = slice(n * CHUNK, (n + 1) * CHUNK)
                sv = _mm(sw_ref[hd], vn[rs, :]) + sb_ref[:, hd:hd + 1]
                ug = saved("gelu_u", hd * PG, rs).astype(F32)
                dy = dy_ref[rs, D + hd * PG:D + (hd + 1) * PG].astype(F32)
                d_ysgu = dy * saved("silu", D + hd * PG, rs).astype(F32)
                dh_ref[rs, 4 * D + hd * PG:4 * D + (hd + 1) * PG] = (
                    dy * (ug * sv) * saved("dsilu", D + hd * PG, rs).astype(F32)).astype(BF16)
                dh_ref[rs, D + hd * PG:D + (hd + 1) * PG] = (
                    d_ysgu * sv * saved("dgelu_u", hd * PG, rs).astype(F32)).astype(BF16)
                d_sv = d_ysgu * ug
                dsb_ref[:, hd:hd + 1] += live * jnp.sum(d_sv, axis=1, keepdims=True)
                d_svb = d_sv.astype(BF16)
                dsw_ref[hd] += live * _mm_nt(d_svb, vn[rs, :])
                d_vn_chunks.append(_mm_tn(sw_ref[hd], d_svb))
            d_vn = jnp.concatenate(d_vn_chunks, axis=0)
            dlg_ref[:, sl] += live * jnp.sum(d_vn * vhat, axis=0, keepdims=True)
            dlb_ref[:, sl] += live * jnp.sum(d_vn, axis=0, keepdims=True)
            d_vg = _norm_rows_bwd(d_vn * lg_ref[:, sl], vhat, saved("rstd_dgelu_v", hd * PG).astype(F32))
            dh_ref[:, 2 * D + hd * PG:2 * D + (hd + 1) * PG] = d_vg.astype(BF16)

        dh_keep[i % 2] = dh_ref[...]

    vec = pl.BlockSpec((1, D), lambda i: (0, 0))
    vec_shape = jax.ShapeDtypeStruct((1, D), F32)

    def whole(shape):
        return pl.BlockSpec(shape, lambda i: (0,) * len(shape))

    return pl.pallas_call(
        body, name="front_backward",
        out_shape=[jax.ShapeDtypeStruct((T, 5 * D), BF16), jax.ShapeDtypeStruct((T, D), F32),
                   jax.ShapeDtypeStruct(pool_w.shape, F32), vec_shape, vec_shape,
                   vec_shape, jax.ShapeDtypeStruct(sgu_wm.shape, F32), jax.ShapeDtypeStruct(sgu_bias_t.shape, F32)],
        grid=(nt + 1,),
        in_specs=[pl.BlockSpec((tm, SAVED_WIDTH * D), lambda i: (tile_of(i), 0)),
                  pl.BlockSpec((tm, 2 * D), lambda i: (tile_of(i), 0)),
                  pl.BlockSpec((tm, D), lambda i: (jnp.minimum(nt - i, nt - 1), 0)), ANY,
                  whole(pool_w.shape), vec, vec, vec, whole(sgu_wm.shape), whole(sgu_bias_t.shape)],
        out_specs=[pl.BlockSpec((tm, 5 * D), lambda i: (tile_of(i), 0)),
                   pl.BlockSpec((tm, D), lambda i: (jnp.minimum(nt - i, nt - 1), 0)),
                   whole(pool_w.shape), vec, vec, vec, whole(sgu_wm.shape), whole(sgu_bias_t.shape)],
        scratch_shapes=[pltpu.VMEM(w_in.shape, BF16), pltpu.VMEM((2, tm, 5 * D), BF16), pltpu.VMEM((HALO, D), F32),
                        pltpu.SemaphoreType.DMA((1,))],
        compiler_params=pltpu.CompilerParams(dimension_semantics=("arbitrary",), vmem_limit_bytes=VMEM_LIMIT),
    )(kept, d_y, dx_part, w_in, pool_w, pool_scale, sgu_g, sgu_b, sgu_wm, sgu_bias_t)


def _weight_backward(d_h, xt, q, scatter_srcs, tm):
    D, T = xt.shape
    cq = d_h.shape[1] // 4
    hr = D // 2
    nt = T // tm
    ns = len(scatter_srcs)

    def body(q_ref, dh_ref, xt_ref, *refs):
        srcs, out_any, dsts = refs[:ns], refs[ns], refs[ns + 1:2 * ns + 1]
        (acc, land_a, send_b, land_b, mine_f, theirs_f,
         a_send, a_recv, b_send, b_recv, j_sems, o_sems, s_send, s_recv) = refs[2 * ns + 1:]
        s, t = pl.program_id(0), pl.program_id(1)
        x_, y_, c = _place()
        sibling = (x_, y_, 1 - c)
        own_rows = pl.ds(pl.multiple_of(c * hr, hr), hr)
        other_rows = pl.ds(pl.multiple_of((1 - c) * hr, hr), hr)

        @pl.when((s == 0) & (t == 0))
        def _():
            for cp in _scatter_copies(srcs, dsts, s_send, s_recv):
                cp.start()

        @pl.when(t == 0)
        def _():
            acc[...] = jnp.zeros_like(acc)

        for c0 in range(0, cq, MXU_COLS):
            acc[:, c0:c0 + MXU_COLS] += _mm(xt_ref[...], dh_ref[:, c0:c0 + MXU_COLS])

        def pair_sum(slot):
            swap = pltpu.make_async_remote_copy(
                src_ref=acc.at[other_rows], dst_ref=land_a.at[slot], send_sem=a_send.at[slot],
                recv_sem=a_recv.at[slot], device_id=sibling, device_id_type=MESH)
            swap.start()
            swap.wait()
            return acc[own_rows, :] + land_a[slot]

        def to_owner(slot):
            flip_x, flip_y = (slot + 1) >> 1, (slot + 1) & 1
            owner = (1 - x_ if flip_x else x_, 1 - y_ if flip_y else y_, c)
            return pltpu.make_async_remote_copy(
                src_ref=send_b.at[slot], dst_ref=land_b.at[slot], send_sem=b_send.at[slot],
                recv_sem=b_recv.at[slot], device_id=owner, device_id_type=MESH)

        for slot in range(3):
            @pl.when((s == slot) & (t == nt - 1))
            def _(slot=slot):
                send_b[slot] = pair_sum(slot).astype(BF16)
                to_owner(slot).start()

        @pl.when((s == 3) & (t == nt - 1))
        def _():
            own = pair_sum(3)
            for slot in range(3):
                to_owner(slot).wait_recv()
            mine_f[...] = (own + land_b[0].astype(F32)) + (land_b[1].astype(F32) + land_b[2].astype(F32))
            join = pltpu.make_async_remote_copy(
                src_ref=mine_f, dst_ref=theirs_f, send_sem=j_sems.at[0], recv_sem=j_sems.at[1],
                device_id=sibling, device_id_type=MESH)
            join.start()
            out_mine = pltpu.make_async_copy(mine_f, out_any.at[own_rows], o_sems.at[0])
            out_mine.start()
            join.wait()
            out_theirs = pltpu.make_async_copy(theirs_f, out_any.at[other_rows], o_sems.at[1])
            out_theirs.start()
            for slot in range(3):
                to_owner(slot).wait_send()
            for cp in _scatter_copies(srcs, dsts, s_send, s_recv):
                cp.wait()
            out_mine.wait()
            out_theirs.wait()

    def quarter(s, t, q_ref):
        return (t, jnp.where(s == 3, q_ref[0], q_ref[0] ^ (s + 1)))

    dma = pltpu.SemaphoreType.DMA
    return pl.pallas_call(
        body, name="weight_backward",
        out_shape=[jax.ShapeDtypeStruct((D, cq), F32)] + _scatter_shapes(scatter_srcs),
        grid_spec=pltpu.PrefetchScalarGridSpec(
            num_scalar_prefetch=1, grid=(4, nt),
            in_specs=[pl.BlockSpec((tm, cq), quarter), pl.BlockSpec((D, tm), lambda s, t, q_ref: (0, t))] + [ANY] * ns,
            out_specs=[ANY] * (ns + 1),
            scratch_shapes=[pltpu.VMEM((D, cq), F32), pltpu.VMEM((4, hr, cq), F32), pltpu.VMEM((3, hr, cq), BF16),
                            pltpu.VMEM((3, hr, cq), BF16), pltpu.VMEM((hr, cq), F32), pltpu.VMEM((hr, cq), F32),
                            dma((4,)), dma((4,)), dma((3,)), dma((3,)), dma((2,)), dma((2,)), dma((3 * ns,)), dma((3 * ns,))]),
        compiler_params=pltpu.CompilerParams(dimension_semantics=("arbitrary", "arbitrary"),
                                             vmem_limit_bytes=VMEM_LIMIT),
    )(jnp.reshape(q, (1,)).astype(jnp.int32), d_h, xt, *scatter_srcs)


def _swap_with_sibling(grads):
    ns = len(grads)

    def body(*refs):
        copies = _swap_copies(refs[:ns], refs[ns:2 * ns], refs[2 * ns], refs[2 * ns + 1])
        for cp in copies:
            cp.start()
        for cp in copies:
            cp.wait()

    return pl.pallas_call(
        body, name="swap_with_sibling", out_shape=_swap_shapes(grads), in_specs=[ANY] * ns, out_specs=[ANY] * ns,
        scratch_shapes=[pltpu.SemaphoreType.DMA((ns,)), pltpu.SemaphoreType.DMA((ns,))],
    )(*grads)


def _token_tile(T, want):
    return math.gcd(T, want)


def kernel(x, p, w_in, pool_w, pool_scale, sgu_ln_g, sgu_ln_b, sgu_w, sgu_b, w_out, ln_g, ln_b, ple_w, ple_gate_w, ple_gate_b, loss_target, m_w_in, m_pool_w, m_pool_scale, m_sgu_ln_g, m_sgu_ln_b, m_sgu_w, m_sgu_b, m_w_out, m_ln_g, m_ln_b, m_ple_w, m_ple_gate_w, m_ple_gate_b, v_w_in, v_pool_w, v_pool_scale, v_sgu_ln_g, v_sgu_ln_b, v_sgu_w, v_sgu_b, v_w_out, v_ln_g, v_ln_b, v_ple_w, v_ple_gate_w, v_ple_gate_b):
    c = lax.axis_index("c")
    T, D = x.shape[1], x.shape[2]
    tm, tm_vpu, tm_acc = _token_tile(T, 512), _token_tile(T, 256), _token_tile(T, 2048)
    x2, p2, tgt = x[0], p[0, 0], loss_target[0]
    G, PGQ, PG = pool_w.shape[1], pool_w.shape[2], pool_w.shape[3]

    w_in_f, pool_f = _gather_weights([w_in[0], pool_w[0].reshape(G * PGQ, PG)])
    pool_f = pool_f.reshape(4, G, PGQ, PG).transpose(1, 0, 2, 3).reshape(G, 4 * PGQ, PG)
    tril = jnp.tril(jnp.ones((CHUNK, CHUNK), dtype=bool))
    sgu_wm = jnp.where(tril[None], sgu_w[0], 0.0).astype(BF16)
    sgu_bias_t = sgu_b[0].T

    kept, y, xt, w_out_f, w_gate_f, w_ple_f = _front_forward(
        x2, w_in_f, pool_f, pool_scale, sgu_ln_g, sgu_ln_b, sgu_wm, sgu_bias_t, [w_out[0], ple_gate_w[0], ple_w[0]],
        tm_vpu)
    w_out_f = w_out_f.reshape(-1, D)
    w_gate_f = w_gate_f.reshape(-1, D)
    (dx_part, d_y, d_w_out, d_w_gate, d_w_ple, d_ln_g, d_ln_b, d_gate_b, ssq) = _tail(
        y, x2, p2, tgt, w_out_f, w_gate_f, w_ple_f, ln_g, ln_b, ple_gate_b, tm)
    d_h, d_x, d_pool_w, d_pool_scale, d_sgu_g, d_sgu_b, d_sgu_w, d_sgu_bias_t = _front_backward(
        kept, d_y, dx_part, w_in_f, pool_f, pool_scale, sgu_ln_g, sgu_ln_b, sgu_wm, sgu_bias_t, tm_vpu)
    loss = lax.psum((0.5 / D) * jnp.sum(ssq), ("x", "y", "c"))

    grads = [d_w_out.reshape(4, -1, D), d_w_gate.reshape(4, -1, D), d_w_ple,
             d_pool_w.reshape(G, 4, PGQ, PG).transpose(1, 0, 2, 3).reshape(4, G * PGQ, PG)]
    grads = [g.reshape(4, 2, g.shape[1] // 2, g.shape[2]) for g in grads]
    d_sgu_w = jnp.where(tril[None], d_sgu_w, 0.0)
    small_names = ["pool_scale", "sgu_ln_g", "sgu_ln_b", "ln_g", "ln_b", "ple_gate_b", "sgu_b", "sgu_w"]
    small_grads = [d_pool_scale, d_sgu_g, d_sgu_b, d_ln_g, d_ln_b, d_gate_b, d_sgu_bias_t.T, d_sgu_w]

    def pack(arrays):
        rows = [a.reshape(-1) for a in arrays[:6]] + [jnp.pad(arrays[6].reshape(-1), (0, 2 * D - arrays[6].size))]
        return jnp.concatenate([r.reshape(-1, D) for r in rows] + [arrays[7].reshape(-1, D)], axis=0)

    def unpack(packed, like):
        out = [packed[k].reshape(like[k].shape) for k in range(6)]
        out.append(packed[6, :like[6].size].reshape(like[6].shape))
        out.append(packed[8:].reshape(like[7].shape))
        return out

    small = pack(small_grads)
    q = 2 * lax.axis_index("x") + lax.axis_index("y")
    *landed, small_landed = _swap_with_sibling(grads + [small])
    parts = [_add_own_half(g, l, c) for g, l in zip(grads, landed)]
    small_chip = _add_pairs(small, small_landed, "add_small")
    d_w_in, *slots, small_slots = _weight_backward(d_h, xt, q, parts + [small_chip], tm_acc)
    halves = [_sum_four(pt, s, q, "sum_four_%d" % k) for k, (pt, s) in enumerate(zip(parts, slots))]
    small_total = _sum_four(small_chip, small_slots, q, "sum_four_small")
    sibling_halves = _join_halves_with_sibling(halves)

    big_names = ["w_out", "ple_gate_w", "ple_w", "pool_w"]
    given = dict(w_in=(w_in, m_w_in, v_w_in), w_out=(w_out, m_w_out, v_w_out),
                 ple_gate_w=(ple_gate_w, m_ple_gate_w, v_ple_gate_w), ple_w=(ple_w, m_ple_w, v_ple_w),
                 pool_w=(pool_w, m_pool_w, v_pool_w), pool_scale=(pool_scale, m_pool_scale, v_pool_scale),
                 sgu_ln_g=(sgu_ln_g, m_sgu_ln_g, v_sgu_ln_g), sgu_ln_b=(sgu_ln_b, m_sgu_ln_b, v_sgu_ln_b),
                 sgu_w=(sgu_w, m_sgu_w, v_sgu_w), sgu_b=(sgu_b, m_sgu_b, v_sgu_b), ln_g=(ln_g, m_ln_g, v_ln_g),
                 ln_b=(ln_b, m_ln_b, v_ln_b), ple_gate_b=(ple_gate_b, m_ple_gate_b, v_ple_gate_b))
    grad, delta, new_m, new_v = {}, {}, {}, {}
    for name, g_mine, g_sibling in zip(big_names, halves, sibling_halves):
        w, m, v = given[name]
        flat = (2 * g_mine.shape[0], g_mine.shape[1])
        outs = _adamw_joined(w.reshape(flat), g_mine, g_sibling, m.reshape(flat), v.reshape(flat), c, "adamw_" + name)
        grad[name], delta[name], new_m[name], new_v[name] = (t.reshape(w.shape) for t in outs)
    grad["w_in"], delta["w_in"], new_m["w_in"], new_v["w_in"] = (
        t[None] for t in _adamw(w_in[0], d_w_in, m_w_in[0], v_w_in[0], "adamw_w_in"))
    small_w, small_m, small_v = (pack([given[n][k] for n in small_names]) for k in range(3))
    small_out = _adamw(small_w, small_total, small_m, small_v, "adamw_small")
    like = [given[n][0] for n in small_names]
    for k, name in enumerate(small_names):
        grad[name], delta[name], new_m[name], new_v[name] = (unpack(t, like)[k] for t in small_out)

    order = ["w_in", "pool_w", "pool_scale", "sgu_ln_g", "sgu_ln_b", "sgu_w", "sgu_b", "w_out", "ln_g", "ln_b",
             "ple_w", "ple_gate_w", "ple_gate_b"]
    return (loss, d_x[None], *[grad[n] for n in order], *[delta[n] for n in order],
            *[new_m[n] for n in order], *[new_v[n] for n in order])
```

```python
import functools
import math

import jax
import jax.numpy as jnp
from jax import lax
from jax.experimental import pallas as pl
from jax.experimental.pallas import tpu as pltpu

F32, BF16 = jnp.float32, jnp.bfloat16
MESH = pl.DeviceIdType.MESH
ANY = pl.BlockSpec(memory_space=pl.ANY)

POOL_WINDOWS = (2, 4, 8, 16)
HALO = 16
CHUNK = 128
MXU_COLS = 256
LN_EPS = 1e-5
DEEPNORM_ALPHA = 2.0 ** 0.25
ADAM_LR, ADAM_B1, ADAM_B2, ADAM_EPS, ADAM_WD, ADAM_STEP = 1e-3, 0.9, 0.999, 1e-8, 0.01, 10
VMEM_LIMIT = 56 * 1024 * 1024
GELU_K = math.sqrt(2.0 / math.pi)
GELU_C = 0.044715
SAVED = {"pooled": 0, "gelu_u": 1, "dgelu_u": 2, "vhat": 3, "rstd_dgelu_v": 4, "silu": 5, "dsilu": 7}
SAVED_WIDTH = 9


def _mm(a, b):
    return jnp.dot(a, b, preferred_element_type=F32)


def _mm_nt(a, b):
    return lax.dot_general(a, b, (((1,), (1,)), ((), ())), preferred_element_type=F32)


def _mm_tn(a, b):
    return lax.dot_general(a, b, (((0,), (0,)), ((), ())), preferred_element_type=F32)


def _gelu_and_grad(x):
    x2 = x * x
    t = jnp.tanh(x * (GELU_K + (GELU_K * GELU_C) * x2))
    hx = 0.5 * x
    g = hx + hx * t
    dg = (0.5 + 0.5 * t) + (hx - hx * t * t) * (GELU_K + (3.0 * GELU_K * GELU_C) * x2)
    return g, dg


def _silu_and_grad(z):
    sig = jax.nn.sigmoid(z)
    zs = z * sig
    return zs, sig + zs * (1.0 - sig)


def _norm_rows(x):
    mu = jnp.mean(x, axis=-1, keepdims=True)
    xc = x - mu
    var = jnp.mean(xc * xc, axis=-1, keepdims=True)
    rstd = lax.rsqrt(var + LN_EPS)
    return xc * rstd, rstd


def _norm_rows_bwd(dxhat, xhat, rstd):
    m1 = jnp.mean(dxhat, axis=-1, keepdims=True)
    m2 = jnp.mean(dxhat * xhat, axis=-1, keepdims=True)
    return rstd * (dxhat - m1 - xhat * m2)


def _inv_count(row0, rows, w):
    t = row0 + lax.broadcasted_iota(jnp.int32, (rows, 1), 0)
    return 1.0 / jnp.minimum(t + 1, w).astype(F32)


def _causal_window_sum(ext, w):
    s, sh = ext, 1
    while sh < w:
        s = s + pltpu.roll(s, sh, axis=0)
        sh *= 2
    return s[HALO:, :]


def _anticausal_window_sum(ext, w):
    n, s, sh = ext.shape[0], ext, 1
    while sh < w:
        s = s + pltpu.roll(s, n - sh, axis=0)
        sh *= 2
    return s[: n - HALO, :]


def _place():
    return lax.axis_index("x"), lax.axis_index("y"), lax.axis_index("c")


def _gather_weights(shards):
    n = len(shards)
    half = [s.shape[0] // 2 for s in shards]

    def body(*refs):
        wide, dsts, srcs = refs[:n], refs[n:2 * n], refs[2 * n:3 * n]
        send_sems, recv_sems, local_sems = refs[3 * n:]
        for k in range(n):
            for r0 in range(0, 2 * half[k], CHUNK):
                srcs[k][r0:r0 + CHUNK, :] = wide[k][r0:r0 + CHUNK, :].astype(BF16)
        x, y, c = _place()
        q = 2 * x + y
        sibling = (x, y, 1 - c)
        chips = [(1 - x, y), (x, 1 - y), (1 - x, 1 - y)]

        def rows(k, qq, cc):
            return dsts[k].at[qq, pl.ds(cc * half[k], half[k])]

        def copy(k, sem, qq, cc, to, src=None):
            return pltpu.make_async_remote_copy(
                src_ref=rows(k, qq, cc) if src is None else src, dst_ref=rows(k, qq, cc),
                send_sem=send_sems.at[6 * k + sem], recv_sem=recv_sems.at[6 * k + sem],
                device_id=to, device_id_type=MESH)

        mine = [pltpu.make_async_copy(srcs[k], dsts[k].at[q], local_sems.at[k]) for k in range(n)]
        for cp in mine:
            cp.start()
        started = []
        for k in range(n):
            own = srcs[k].at[pl.ds(c * half[k], half[k])]
            for j, chip in enumerate(chips):
                started.append(copy(k, j, q, c, (*chip, c), src=own))
                started[-1].start()
        for k in range(n):
            for j, chip in enumerate(chips):
                qq = 2 * chip[0] + chip[1]
                copy(k, j, qq, c, (x, y, c)).wait_recv()
                started.append(copy(k, 3 + j, qq, c, sibling))
                started[-1].start()
        for k in range(n):
            for j, chip in enumerate(chips):
                copy(k, 3 + j, 2 * chip[0] + chip[1], 1 - c, (x, y, c)).wait_recv()
        for cp in started:
            cp.wait_send()
        for cp in mine:
            cp.wait()

    return pl.pallas_call(
        body, name="gather_weights",
        out_shape=[jax.ShapeDtypeStruct((4,) + s.shape, BF16) for s in shards],
        in_specs=[pl.BlockSpec(memory_space=pltpu.VMEM)] * n, out_specs=[ANY] * n,
        scratch_shapes=[pltpu.VMEM(s.shape, BF16) for s in shards]
        + [pltpu.SemaphoreType.DMA((6 * n,)), pltpu.SemaphoreType.DMA((6 * n,)), pltpu.SemaphoreType.DMA((n,))],
        compiler_params=pltpu.CompilerParams(vmem_limit_bytes=VMEM_LIMIT),
    )(*shards)


def _direct_gather_copies(srcs, dsts, send_sems, recv_sems):
    x, y, c = _place()
    q = 2 * x + y
    sends, recvs = [], []
    for k, (src, dst) in enumerate(zip(srcs, dsts)):
        half = src.shape[0] // 2
        for j, chip in enumerate([(1 - x, y), (x, 1 - y), (1 - x, 1 - y)]):
            for core in range(2):
                sends.append(pltpu.make_async_remote_copy(
                    src_ref=src.at[pl.ds(c * half, half)], dst_ref=dst.at[q, pl.ds(c * half, half)],
                    send_sem=send_sems.at[6 * k + 2 * j + core], recv_sem=recv_sems.at[6 * k + 2 * j + c],
                    device_id=(*chip, core), device_id_type=MESH))
                landed = dst.at[2 * chip[0] + chip[1], pl.ds(core * half, half)]
                recvs.append(pltpu.make_async_remote_copy(
                    src_ref=landed, dst_ref=landed, send_sem=send_sems.at[6 * k + 2 * j + core],
                    recv_sem=recv_sems.at[6 * k + 2 * j + core], device_id=(x, y, c), device_id_type=MESH))
    return sends, recvs


def _swap_copies(srcs, dsts, send_sems, recv_sems):
    x, y, c = _place()
    return [pltpu.make_async_remote_copy(
        src_ref=src.at[:, 1 - c] if len(src.shape) == 4 else src, dst_ref=dst,
        send_sem=send_sems.at[k], recv_sem=recv_sems.at[k], device_id=(x, y, 1 - c), device_id_type=MESH)
        for k, (src, dst) in enumerate(zip(srcs, dsts))]


def _swap_shapes(grads):
    return [jax.ShapeDtypeStruct((4,) + g.shape[2:] if g.ndim == 4 else g.shape, g.dtype) for g in grads]


def _scatter_copies(srcs, dsts, send_sems, recv_sems):
    x, y, c = _place()
    copies = []
    for j, chip in enumerate([(1 - x, y), (x, 1 - y), (1 - x, 1 - y)]):
        for k, (src, dst) in enumerate(zip(srcs, dsts)):
            copies.append(pltpu.make_async_remote_copy(
                src_ref=src.at[2 * chip[0] + chip[1]] if len(src.shape) == 3 else src, dst_ref=dst.at[j],
                send_sem=send_sems.at[3 * k + j], recv_sem=recv_sems.at[3 * k + j],
                device_id=(*chip, c), device_id_type=MESH))
    return copies


def _scatter_shapes(parts):
    return [jax.ShapeDtypeStruct((3,) + (p.shape[1:] if p.ndim == 3 else p.shape), p.dtype) for p in parts]


def _join_halves_with_sibling(halves):
    n = len(halves)

    def body(*refs):
        srcs, dsts = refs[:n], refs[n:2 * n]
        send_sems, recv_sems = refs[2 * n:]
        x, y, c = _place()
        copies = [pltpu.make_async_remote_copy(
            src_ref=srcs[k], dst_ref=dsts[k], send_sem=send_sems.at[k], recv_sem=recv_sems.at[k],
            device_id=(x, y, 1 - c), device_id_type=MESH) for k in range(n)]
        for cp in copies:
            cp.start()
        for cp in copies:
            cp.wait()

    return pl.pallas_call(
        body, name="join_halves",
        out_shape=[jax.ShapeDtypeStruct(h.shape, h.dtype) for h in halves],
        in_specs=[ANY] * n, out_specs=[ANY] * n,
        scratch_shapes=[pltpu.SemaphoreType.DMA((n,)), pltpu.SemaphoreType.DMA((n,))],
    )(*halves)


def _row_block(rows, cols, n_arrays):
    cap = max(8, (VMEM_LIMIT // 4) // (8 * n_arrays * cols))
    rb = rows
    while rb > cap and rb % 2 == 0:
        rb //= 2
    return rb


def _scalar(value):
    return jnp.reshape(value, (1,)).astype(jnp.int32)


def _whole(a):
    return pl.BlockSpec(a.shape, lambda i, s_ref: (0,) * a.ndim)


def _add_own_halves(grads, landed, small, small_landed, c):
    n = len(grads)

    def body(c_ref, *refs):
        g, l, (s, sl), o, so = refs[:n], refs[n:2 * n], refs[2 * n:2 * n + 2], refs[2 * n + 2:3 * n + 2], refs[3 * n + 2]
        for k in range(n):
            o[k][...] = (g[k][...] + l[k][...]).astype(BF16)
        so[...] = s[...] + sl[...]

    def half(a):
        return pl.BlockSpec((4, a.shape[1] // 2, a.shape[2]), lambda i, c_ref: (0, i, 0))

    return pl.pallas_call(
        body, name="add_own_halves",
        out_shape=[jax.ShapeDtypeStruct(a.shape, BF16) for a in landed] + [jax.ShapeDtypeStruct(small.shape, F32)],
        grid_spec=pltpu.PrefetchScalarGridSpec(
            num_scalar_prefetch=1, grid=(2,),
            in_specs=[pl.BlockSpec((4, None, a.shape[2] // 2, a.shape[3]), lambda i, c_ref: (0, c_ref[0], i, 0))
                      for a in grads] + [half(a) for a in landed] + [_whole(small), _whole(small_landed)],
            out_specs=[half(a) for a in landed] + [_whole(small)]),
        compiler_params=pltpu.CompilerParams(vmem_limit_bytes=VMEM_LIMIT),
    )(_scalar(c), *grads, *landed, small, small_landed)


def _sum_fours(parts, slots, small_chip, small_slots, q):
    n = len(parts)

    def four(own, s):
        return (own[...].astype(F32) + s[0].astype(F32)) + (s[1].astype(F32) + s[2].astype(F32))

    def body(q_ref, *refs):
        p, s, (sc, ss), o, so = refs[:n], refs[n:2 * n], refs[2 * n:2 * n + 2], refs[2 * n + 2:3 * n + 2], refs[3 * n + 2]
        for k in range(n):
            o[k][...] = four(p[k], s[k])
        so[...] = four(sc, ss)

    return pl.pallas_call(
        body, name="sum_fours",
        out_shape=[jax.ShapeDtypeStruct(a.shape[1:], F32) for a in parts] + [jax.ShapeDtypeStruct(small_chip.shape, F32)],
        grid_spec=pltpu.PrefetchScalarGridSpec(
            num_scalar_prefetch=1, grid=(2,),
            in_specs=[pl.BlockSpec((None, a.shape[1] // 2, a.shape[2]), lambda i, q_ref: (q_ref[0], i, 0)) for a in parts]
            + [pl.BlockSpec((3, a.shape[1] // 2, a.shape[2]), lambda i, q_ref: (0, i, 0)) for a in slots]
            + [_whole(small_chip), _whole(small_slots)],
            out_specs=[pl.BlockSpec((a.shape[1] // 2, a.shape[2]), lambda i, q_ref: (i, 0)) for a in parts]
            + [_whole(small_chip)]),
        compiler_params=pltpu.CompilerParams(vmem_limit_bytes=VMEM_LIMIT),
    )(_scalar(q), *parts, *slots, small_chip, small_slots)


def _adamw_math(w, g, m, v):
    nm = ADAM_B1 * m + (1.0 - ADAM_B1) * g
    nv = ADAM_B2 * v + (1.0 - ADAM_B2) * (g * g)
    m_hat = nm / (1.0 - ADAM_B1 ** ADAM_STEP)
    v_hat = nv / (1.0 - ADAM_B2 ** ADAM_STEP)
    return -ADAM_LR * (m_hat / (jnp.sqrt(v_hat) + ADAM_EPS) + ADAM_WD * w), nm, nv


def _adamw(w, g, m, v, name):
    rows, cols = w.shape
    rb = _row_block(rows, cols, 8)

    def body(w_ref, g_ref, m_ref, v_ref, go_ref, d_ref, nm_ref, nv_ref):
        go_ref[...] = g_ref[...]
        d_ref[...], nm_ref[...], nv_ref[...] = _adamw_math(w_ref[...], g_ref[...], m_ref[...], v_ref[...])

    spec = pl.BlockSpec((rb, cols), lambda r: (r, 0))
    out = jax.ShapeDtypeStruct(w.shape, F32)
    return pl.pallas_call(body, name=name, out_shape=[out] * 4, grid=(rows // rb,),
                          in_specs=[spec] * 4, out_specs=[spec] * 4)(w, g, m, v)


def _adamw_joined(ws, g_mine, g_sibling, ms, vs, small, c):
    n = len(ws)

    def body(c_ref, *refs):
        ins, outs = refs[:5 * n + 4], refs[5 * n + 4:]
        mine = c_ref[0] == pl.program_id(0)
        for k in range(n):
            w, gm, gs, m, v = ins[5 * k:5 * k + 5]
            g = jnp.where(mine, gm[...], gs[...])
            outs[4 * k][...] = g
            outs[4 * k + 1][...], outs[4 * k + 2][...], outs[4 * k + 3][...] = _adamw_math(w[...], g, m[...], v[...])
        w, g, m, v = ins[5 * n:]
        outs[4 * n][...] = g[...]
        outs[4 * n + 1][...], outs[4 * n + 2][...], outs[4 * n + 3][...] = _adamw_math(w[...], g[...], m[...], v[...])

    def half(a):
        return pl.BlockSpec((a.shape[0] // 2, a.shape[1]), lambda hf, c_ref: (hf, 0))

    in_specs, operands = [], []
    for k in range(n):
        in_specs += [half(ws[k]), _whole(g_mine[k]), _whole(g_sibling[k]), half(ms[k]), half(vs[k])]
        operands += [ws[k], g_mine[k], g_sibling[k], ms[k], vs[k]]
    outs = pl.pallas_call(
        body, name="adamw_joined",
        out_shape=[jax.ShapeDtypeStruct(w.shape, F32) for w in ws for _ in range(4)]
        + [jax.ShapeDtypeStruct(small[0].shape, F32)] * 4,
        grid_spec=pltpu.PrefetchScalarGridSpec(
            num_scalar_prefetch=1, grid=(2,),
            in_specs=in_specs + [_whole(a) for a in small],
            out_specs=[half(w) for w in ws for _ in range(4)] + [_whole(small[0])] * 4),
        compiler_params=pltpu.CompilerParams(vmem_limit_bytes=VMEM_LIMIT),
    )(_scalar(c), *operands, *small)
    return [outs[4 * k:4 * k + 4] for k in range(n + 1)]


def _front_forward(x, w_in, pool_w, pool_scale, sgu_g, sgu_b, sgu_wm, sgu_bias_t, later_shards, tm):
    T, D = x.shape
    nq, _, cq = w_in.shape
    G, PG = pool_w.shape[0], pool_w.shape[1]
    nt = T // tm
    bpd = D // PG
    nl = len(later_shards)

    def body(x_ref, win_any, pw_any, ps_ref, lg_ref, lb_ref, sw_ref, sb_ref, *refs):
        shards_any, (keep_ref, y_ref, xt_ref), gathered = refs[:nl], refs[nl:nl + 3], refs[nl + 3:2 * nl + 3]
        wide, narrow = refs[2 * nl + 3:3 * nl + 3], refs[3 * nl + 3:4 * nl + 3]
        win_v, pw_v, carry, sems, load_sems, own_sems, send_sems, recv_sems = refs[4 * nl + 3:]
        i = pl.program_id(0)
        own_quarter = 2 * lax.axis_index("x") + lax.axis_index("y")

        def own_copies():
            return [pltpu.make_async_copy(narrow[k], gathered[k].at[own_quarter], own_sems.at[k]) for k in range(nl)]

        @pl.when(i == 0)
        def _():
            c1 = pltpu.make_async_copy(win_any, win_v, sems.at[0])
            c2 = pltpu.make_async_copy(pw_any, pw_v, sems.at[1])
            loads = [pltpu.make_async_copy(shards_any[k], wide[k], load_sems.at[k]) for k in range(nl)]
            for cp in [c1, c2] + loads:
                cp.start()
            carry[...] = jnp.zeros_like(carry)
            for k in range(nl):
                loads[k].wait()
                for r0 in range(0, wide[k].shape[0], CHUNK):
                    narrow[k][r0:r0 + CHUNK, :] = wide[k][r0:r0 + CHUNK, :].astype(BF16)
            for cp in own_copies() + _direct_gather_copies(narrow, gathered, send_sems, recv_sems)[0]:
                cp.start()
            c1.wait()
            c2.wait()

        xb = x_ref[...].astype(BF16)
        xt_ref[...] = x_ref[...].T.astype(BF16)

        def h_block(j):
            qq, off = divmod(j * PG, cq)
            return _mm(xb, win_v[qq, :, off:off + PG])

        def keep(part, col, value):
            keep_ref[:, SAVED[part] * D + col:SAVED[part] * D + col + PG] = value.astype(BF16)

        def ahead(stage):
            if stage < G:
                return h_block(stage), h_block(3 * bpd + stage)
            if stage < G + bpd:
                hd = stage - G
                return h_block(bpd + hd), h_block(2 * bpd + hd), h_block(4 * bpd + hd)
            return None

        blocks = ahead(0)

        for g, w in enumerate(POOL_WINDOWS):
            sl = slice(g * PG, (g + 1) * PG)
            a, z = blocks
            blocks = ahead(g + 1)
            ext = jnp.concatenate([carry[:, sl], a], axis=0)
            carry[:, sl] = a[tm - HALO:, :]
            pooled = (_causal_window_sum(ext, w) * _inv_count(i * tm, tm, w) - a).astype(BF16)
            mixed = _mm(pooled, pw_v[g])
            zs, dzs = _silu_and_grad(z)
            keep("pooled", g * PG, pooled)
            keep("silu", g * PG, zs)
            keep("dsilu", g * PG, dzs)
            y_ref[:, sl] = (mixed * ps_ref[:, sl] * zs).astype(BF16)

        for hd in range(bpd):
            sl = slice(hd * PG, (hd + 1) * PG)
            u, v, z = blocks
            blocks = ahead(G + hd + 1)
            ug, dug = _gelu_and_grad(u)
            vg, dvg = _gelu_and_grad(v)
            vhat, rstd = _norm_rows(vg)
            zs, dzs = _silu_and_grad(z)
            keep("gelu_u", hd * PG, ug)
            keep("dgelu_u", hd * PG, dug)
            keep("vhat", hd * PG, vhat)
            keep("rstd_dgelu_v", hd * PG, rstd * dvg)
            keep("silu", D + hd * PG, zs)
            keep("dsilu", D + hd * PG, dzs)
            vn = (vhat * lg_ref[:, sl] + lb_ref[:, sl]).astype(BF16)
            gated = ug * zs
            for n in range(tm // CHUNK):
                rs = slice(n * CHUNK, (n + 1) * CHUNK)
                sv = _mm(sw_ref[hd], vn[rs, :]) + sb_ref[:, hd:hd + 1]
                y_ref[rs, D + hd * PG:D + (hd + 1) * PG] = (gated[rs, :] * sv).astype(BF16)

        @pl.when(i == nt - 1)
        def _():
            sends, recvs = _direct_gather_copies(narrow, gathered, send_sems, recv_sems)
            for cp in sends:
                cp.wait_send()
            for cp in recvs:
                cp.wait_recv()
            for cp in own_copies():
                cp.wait()

    vec = pl.BlockSpec((1, D), lambda i: (0, 0))
    return pl.pallas_call(
        body, name="front_forward",
        out_shape=[jax.ShapeDtypeStruct((T, SAVED_WIDTH * D), BF16), jax.ShapeDtypeStruct((T, 2 * D), BF16),
                   jax.ShapeDtypeStruct((D, T), BF16)]
        + [jax.ShapeDtypeStruct((4,) + s.shape, BF16) for s in later_shards],
        grid=(nt,),
        in_specs=[pl.BlockSpec((tm, D), lambda i: (i, 0)), ANY, ANY, vec, vec, vec,
                  pl.BlockSpec(sgu_wm.shape, lambda i: (0, 0, 0)), pl.BlockSpec(sgu_bias_t.shape, lambda i: (0, 0))]
        + [ANY] * nl,
        out_specs=[pl.BlockSpec((tm, SAVED_WIDTH * D), lambda i: (i, 0)), pl.BlockSpec((tm, 2 * D), lambda i: (i, 0)),
                   pl.BlockSpec((D, tm), lambda i: (0, i))] + [ANY] * nl,
        scratch_shapes=[pltpu.VMEM(s.shape, F32) for s in later_shards]
        + [pltpu.VMEM(s.shape, BF16) for s in later_shards]
        + [pltpu.VMEM(w_in.shape, BF16), pltpu.VMEM(pool_w.shape, BF16), pltpu.VMEM((HALO, D), F32),
           pltpu.SemaphoreType.DMA((2,)), pltpu.SemaphoreType.DMA((nl,)), pltpu.SemaphoreType.DMA((nl,)),
           pltpu.SemaphoreType.DMA((6 * nl,)), pltpu.SemaphoreType.DMA((6 * nl,))],
        compiler_params=pltpu.CompilerParams(dimension_semantics=("arbitrary",), vmem_limit_bytes=VMEM_LIMIT),
    )(x, w_in, pool_w, pool_scale, sgu_g, sgu_b, sgu_wm, sgu_bias_t, *later_shards)


def _tail(y, x, p, target, w_out, w_gate, w_ple, ln_g, ln_b, gate_b, tm):
    T, D = x.shape
    K = p.shape[1]
    nq, _, cq = w_ple.shape
    nt = T // tm

    def body(y_ref, x_ref, p_ref, t_ref, wout_any, wg_any, wp_any, lng_ref, lnb_ref, bg_ref,
             dxp_ref, dy_ref, dwout_any, dwg_any, dwp_any, dlng_ref, dlnb_ref, dbg_ref, ssq_ref,
             wout_v, wg_v, wp_v, dwout_acc, dwg_acc, dwp_acc, sems):
        i = pl.program_id(0)

        @pl.when(i == 0)
        def _():
            loads = [pltpu.make_async_copy(s, d, sems.at[k])
                     for k, (s, d) in enumerate(((wout_any, wout_v), (wg_any, wg_v), (wp_any, wp_v)))]
            for cp in loads:
                cp.start()
            for ref in (dwout_acc, dwg_acc, dwp_acc, dlng_ref, dlnb_ref, dbg_ref, ssq_ref):
                ref[...] = jnp.zeros_like(ref)
            for cp in loads:
                cp.wait()

        yb = y_ref[...]
        pb = p_ref[...].astype(BF16)
        xhat, rstd = _norm_rows(DEEPNORM_ALPHA * x_ref[...] + _mm(yb, wout_v[...]))
        x1 = xhat * lng_ref[...] + lnb_ref[...]
        x1b = x1.astype(BF16)
        gate = jax.nn.sigmoid(_mm(x1b, wg_v[...]) + bg_ref[...])
        e = jnp.concatenate([_mm(pb, wp_v[qq]) for qq in range(nq)], axis=1)
        diff = x1 + gate * e - t_ref[...]
        ssq_ref[...] += jnp.sum(diff * diff, axis=0, keepdims=True)

        dout = diff * (1.0 / D)
        d_e = (dout * gate).astype(BF16)
        dgl = dout * e * gate * (1.0 - gate)
        dglb = dgl.astype(BF16)
        for qq in range(nq):
            dwp_acc[qq] += _mm_tn(pb, d_e[:, qq * cq:(qq + 1) * cq])
        for c0 in range(0, D, MXU_COLS):
            dwg_acc[:, c0:c0 + MXU_COLS] += _mm_tn(x1b, dglb[:, c0:c0 + MXU_COLS])
        dbg_ref[...] += jnp.sum(dgl, axis=0, keepdims=True)
        d_x1 = dout + _mm_nt(dglb, wg_v[...])
        dlng_ref[...] += jnp.sum(d_x1 * xhat, axis=0, keepdims=True)
        dlnb_ref[...] += jnp.sum(d_x1, axis=0, keepdims=True)
        d_r = _norm_rows_bwd(d_x1 * lng_ref[...], xhat, rstd)
        drb = d_r.astype(BF16)
        dxp_ref[...] = DEEPNORM_ALPHA * d_r
        for c0 in range(0, D, MXU_COLS):
            dwout_acc[:, c0:c0 + MXU_COLS] += _mm_tn(yb, drb[:, c0:c0 + MXU_COLS])
        for c0 in range(0, 2 * D, 2 * MXU_COLS):
            dy_ref[:, c0:c0 + 2 * MXU_COLS] = _mm_nt(drb, wout_v[c0:c0 + 2 * MXU_COLS, :]).astype(BF16)

        @pl.when(i == nt - 1)
        def _():
            stores = [pltpu.make_async_copy(s, d, sems.at[k])
                      for k, (s, d) in enumerate(((dwout_acc, dwout_any), (dwg_acc, dwg_any), (dwp_acc, dwp_any)))]
            for cp in stores:
                cp.start()
            for cp in stores:
                cp.wait()

    vec = pl.BlockSpec((1, D), lambda i: (0, 0))
    vec_shape = jax.ShapeDtypeStruct((1, D), F32)

    def tile(cols):
        return pl.BlockSpec((tm, cols), lambda i: (i, 0))

    return pl.pallas_call(
        body, name="tail",
        out_shape=[jax.ShapeDtypeStruct((T, D), F32), jax.ShapeDtypeStruct((T, 2 * D), BF16),
                   jax.ShapeDtypeStruct(w_out.shape, F32), jax.ShapeDtypeStruct(w_gate.shape, F32),
                   jax.ShapeDtypeStruct(w_ple.shape, F32), vec_shape, vec_shape, vec_shape, vec_shape],
        grid=(nt,),
        in_specs=[tile(2 * D), tile(D), tile(K), tile(D), ANY, ANY, ANY, vec, vec, vec],
        out_specs=[tile(D), tile(2 * D), ANY, ANY, ANY, vec, vec, vec, vec],
        scratch_shapes=[pltpu.VMEM(w_out.shape, BF16), pltpu.VMEM(w_gate.shape, BF16), pltpu.VMEM(w_ple.shape, BF16),
                        pltpu.VMEM(w_out.shape, F32), pltpu.VMEM(w_gate.shape, F32), pltpu.VMEM(w_ple.shape, F32),
                        pltpu.SemaphoreType.DMA((3,))],
        compiler_params=pltpu.CompilerParams(dimension_semantics=("arbitrary",), vmem_limit_bytes=VMEM_LIMIT),
    )(y, x, p, target, w_out, w_gate, w_ple, ln_g, ln_b, gate_b)


def _front_backward(kept, d_y, dx_part, w_in, pool_w, pool_scale, sgu_g, sgu_b, sgu_wm, sgu_bias_t, tm):
    T = kept.shape[0]
    D = kept.shape[1] // SAVED_WIDTH
    nq, _, cq = w_in.shape
    G, PG = pool_w.shape[0], pool_w.shape[1]
    nt = T // tm

    def tile_of(i):
        return nt - 1 - jnp.minimum(i, nt - 1)

    def body(kept_ref, dy_ref, dxp_ref, win_any, pw_ref, ps_ref, lg_ref, lb_ref, sw_ref, sb_ref,
             dh_ref, dx_ref, dpw_ref, dps_ref, dlg_ref, dlb_ref, dsw_ref, dsb_ref, win_v, dh_keep, carry, sems):
        i = pl.program_id(0)
        ti = tile_of(i)
        live = (i < nt).astype(F32)

        def saved(part, col, rows=slice(None)):
            return kept_ref[rows, SAVED[part] * D + col:SAVED[part] * D + col + PG]

        @pl.when(i == 0)
        def _():
            cp = pltpu.make_async_copy(win_any, win_v, sems.at[0])
            cp.start()
            carry[...] = jnp.zeros_like(carry)
            dh_keep[...] = jnp.zeros_like(dh_keep)
            for ref in (dpw_ref, dps_ref, dlg_ref, dlb_ref, dsw_ref, dsb_ref):
                ref[...] = jnp.zeros_like(ref)
            cp.wait()

        def dx_columns(r0):
            dx = dxp_ref[:, r0:r0 + MXU_COLS]
            for qq in range(nq):
                dx = dx + _mm_nt(dh_keep[(i + 1) % 2, :, qq * cq:(qq + 1) * cq], win_v[qq, r0:r0 + MXU_COLS, :])
            dx_ref[:, r0:r0 + MXU_COLS] = dx

        dx_chunks = list(range(0, D, MXU_COLS))
        stages = G + D // PG

        for g, w in enumerate(POOL_WINDOWS):
            for r0 in dx_chunks[g * len(dx_chunks) // stages:(g + 1) * len(dx_chunks) // stages]:
                dx_columns(r0)
            sl = slice(g * PG, (g + 1) * PG)
            pooled = saved("pooled", g * PG)
            mixed = _mm(pooled, pw_ref[g])
            dy = dy_ref[:, sl].astype(F32)
            d_ypool = dy * saved("silu", g * PG).astype(F32)
            dh_ref[:, 3 * D + g * PG:3 * D + (g + 1) * PG] = (
                dy * (mixed * ps_ref[:, sl]) * saved("dsilu", g * PG).astype(F32)).astype(BF16)
            dps_ref[:, sl] += live * jnp.sum(d_ypool * mixed, axis=0, keepdims=True)
            d_mixed = (d_ypool * ps_ref[:, sl]).astype(BF16)
            dpw_ref[g] += live * _mm_tn(pooled, d_mixed)
            d_pooled = _mm_nt(d_mixed, pw_ref[g])
            scaled = d_pooled * _inv_count(ti * tm, tm, w)
            after = jnp.concatenate([scaled, carry[:, sl]], axis=0)
            carry[:, sl] = jnp.where(i < nt - 1, scaled[:HALO, :], carry[:, sl])
            dh_ref[:, sl] = (_anticausal_window_sum(after, w) - d_pooled).astype(BF16)

        for hd in range(D // PG):
            for r0 in dx_chunks[(G + hd) * len(dx_chunks) // stages:(G + hd + 1) * len(dx_chunks) // stages]:
                dx_columns(r0)
            sl = slice(hd * PG, (hd + 1) * PG)
            vhat = saved("vhat", hd * PG).astype(F32)
            vn = (vhat * lg_ref[:, sl] + lb_ref[:, sl]).astype(BF16)
            d_vn_chunks = []
            for n in range(tm // CHUNK):
                rs = slice(n * CHUNK, (n + 1) * CHUNK)
                sv = _mm(sw_ref[hd], vn[rs, :]) + sb_ref[:, hd:hd + 1]
                ug = saved("gelu_u", hd * PG, rs).astype(F32)
                dy = dy_ref[rs, D + hd * PG:D + (hd + 1) * PG].astype(F32)
                d_ysgu = dy * saved("silu", D + hd * PG, rs).astype(F32)
                dh_ref[rs, 4 * D + hd * PG:4 * D + (hd + 1) * PG] = (
                    dy * (ug * sv) * saved("dsilu", D + hd * PG, rs).astype(F32)).astype(BF16)
                dh_ref[rs, D + hd * PG:D + (hd + 1) * PG] = (
                    d_ysgu * sv * saved("dgelu_u", hd * PG, rs).astype(F32)).astype(BF16)
                d_sv = d_ysgu * ug
                dsb_ref[:, hd:hd + 1] += live * jnp.sum(d_sv, axis=1, keepdims=True)
                d_svb = d_sv.astype(BF16)
                dsw_ref[hd] += live * _mm_nt(d_svb, vn[rs, :])
                d_vn_chunks.append(_mm_tn(sw_ref[hd], d_svb))
            d_vn = jnp.concatenate(d_vn_chunks, axis=0)
            dlg_ref[:, sl] += live * jnp.sum(d_vn * vhat, axis=0, keepdims=True)
            dlb_ref[:, sl] += live * jnp.sum(d_vn, axis=0, keepdims=True)
            d_vg = _norm_rows_bwd(d_vn * lg_ref[:, sl], vhat, saved("rstd_dgelu_v", hd * PG).astype(F32))
            dh_ref[:, 2 * D + hd * PG:2 * D + (hd + 1) * PG] = d_vg.astype(BF16)

        dh_keep[i % 2] = dh_ref[...]

    vec = pl.BlockSpec((1, D), lambda i: (0, 0))
    vec_shape = jax.ShapeDtypeStruct((1, D), F32)

    def whole(shape):
        return pl.BlockSpec(shape, lambda i: (0,) * len(shape))

    return pl.pallas_call(
        body, name="front_backward",
        out_shape=[jax.ShapeDtypeStruct((T, 5 * D), BF16), jax.ShapeDtypeStruct((T, D), F32),
                   jax.ShapeDtypeStruct(pool_w.shape, F32), vec_shape, vec_shape,
                   vec_shape, jax.ShapeDtypeStruct(sgu_wm.shape, F32), jax.ShapeDtypeStruct(sgu_bias_t.shape, F32)],
        grid=(nt + 1,),
        in_specs=[pl.BlockSpec((tm, SAVED_WIDTH * D), lambda i: (tile_of(i), 0)),
                  pl.BlockSpec((tm, 2 * D), lambda i: (tile_of(i), 0)),
                  pl.BlockSpec((tm, D), lambda i: (jnp.minimum(nt - i, nt - 1), 0)), ANY,
                  whole(pool_w.shape), vec, vec, vec, whole(sgu_wm.shape), whole(sgu_bias_t.shape)],
        out_specs=[pl.BlockSpec((tm, 5 * D), lambda i: (tile_of(i), 0)),
                   pl.BlockSpec((tm, D), lambda i: (jnp.minimum(nt - i, nt - 1), 0)),
                   whole(pool_w.shape), vec, vec, vec, whole(sgu_wm.shape), whole(sgu_bias_t.shape)],
        scratch_shapes=[pltpu.VMEM(w_in.shape, BF16), pltpu.VMEM((2, tm, 5 * D), BF16), pltpu.VMEM((HALO, D), F32),
                        pltpu.SemaphoreType.DMA((1,))],
        compiler_params=pltpu.CompilerParams(dimension_semantics=("arbitrary",), vmem_limit_bytes=VMEM_LIMIT),
    )(kept, d_y, dx_part, w_in, pool_w, pool_scale, sgu_g, sgu_b, sgu_wm, sgu_bias_t)


def _weight_backward(d_h, xt, q, scatter_srcs, tm):
    D, T = xt.shape
    cq = d_h.shape[1] // 4
    hr = D // 2
    nt = T // tm
    ns = len(scatter_srcs)

    def body(q_ref, dh_ref, xt_ref, *refs):
        srcs, out_any, dsts = refs[:ns], refs[ns], refs[ns + 1:2 * ns + 1]
        (acc, land_a, send_b, land_b, mine_f, theirs_f,
         a_send, a_recv, b_send, b_recv, j_sems, o_sems, s_send, s_recv) = refs[2 * ns + 1:]
        s, t = pl.program_id(0), pl.program_id(1)
        x_, y_, c = _place()
        sibling = (x_, y_, 1 - c)
        own_rows = pl.ds(pl.multiple_of(c * hr, hr), hr)
        other_rows = pl.ds(pl.multiple_of((1 - c) * hr, hr), hr)

        @pl.when((s == 0) & (t == 0))
        def _():
            for cp in _scatter_copies(srcs, dsts, s_send, s_recv):
                cp.start()

        @pl.when(t == 0)
        def _():
            acc[...] = jnp.zeros_like(acc)

        for c0 in range(0, cq, MXU_COLS):
            acc[:, c0:c0 + MXU_COLS] += _mm(xt_ref[...], dh_ref[:, c0:c0 + MXU_COLS])

        def pair_sum(slot):
            swap = pltpu.make_async_remote_copy(
                src_ref=acc.at[other_rows], dst_ref=land_a.at[slot], send_sem=a_send.at[slot],
                recv_sem=a_recv.at[slot], device_id=sibling, device_id_type=MESH)
            swap.start()
            swap.wait()
            return acc[own_rows, :] + land_a[slot]

        def to_owner(slot):
            flip_x, flip_y = (slot + 1) >> 1, (slot + 1) & 1
            owner = (1 - x_ if flip_x else x_, 1 - y_ if flip_y else y_, c)
            return pltpu.make_async_remote_copy(
                src_ref=send_b.at[slot], dst_ref=land_b.at[slot], send_sem=b_send.at[slot],
                recv_sem=b_recv.at[slot], device_id=owner, device_id_type=MESH)

        for slot in range(3):
            @pl.when((s == slot) & (t == nt - 1))
            def _(slot=slot):
                send_b[slot] = pair_sum(slot).astype(BF16)
                to_owner(slot).start()

        @pl.when((s == 3) & (t == nt - 1))
        def _():
            own = pair_sum(3)
            for slot in range(3):
                to_owner(slot).wait_recv()
            mine_f[...] = (own + land_b[0].astype(F32)) + (land_b[1].astype(F32) + land_b[2].astype(F32))
            join = pltpu.make_async_remote_copy(
                src_ref=mine_f, dst_ref=theirs_f, send_sem=j_sems.at[0], recv_sem=j_sems.at[1],
                device_id=sibling, device_id_type=MESH)
            join.start()
            out_mine = pltpu.make_async_copy(mine_f, out_any.at[own_rows], o_sems.at[0])
            out_mine.start()
            join.wait()
            out_theirs = pltpu.make_async_copy(theirs_f, out_any.at[other_rows], o_sems.at[1])
            out_theirs.start()
            for slot in range(3):
                to_owner(slot).wait_send()
            for cp in _scatter_copies(srcs, dsts, s_send, s_recv):
                cp.wait()
            out_mine.wait()
            out_theirs.wait()

    def quarter(s, t, q_ref):
        return (t, jnp.where(s == 3, q_ref[0], q_ref[0] ^ (s + 1)))

    dma = pltpu.SemaphoreType.DMA
    return pl.pallas_call(
        body, name="weight_backward",
        out_shape=[jax.ShapeDtypeStruct((D, cq), F32)] + _scatter_shapes(scatter_srcs),
        grid_spec=pltpu.PrefetchScalarGridSpec(
            num_scalar_prefetch=1, grid=(4, nt),
            in_specs=[pl.BlockSpec((tm, cq), quarter), pl.BlockSpec((D, tm), lambda s, t, q_ref: (0, t))] + [ANY] * ns,
            out_specs=[ANY] * (ns + 1),
            scratch_shapes=[pltpu.VMEM((D, cq), F32), pltpu.VMEM((4, hr, cq), F32), pltpu.VMEM((3, hr, cq), BF16),
                            pltpu.VMEM((3, hr, cq), BF16), pltpu.VMEM((hr, cq), F32), pltpu.VMEM((hr, cq), F32),
                            dma((4,)), dma((4,)), dma((3,)), dma((3,)), dma((2,)), dma((2,)), dma((3 * ns,)), dma((3 * ns,))]),
        compiler_params=pltpu.CompilerParams(dimension_semantics=("arbitrary", "arbitrary"),
                                             vmem_limit_bytes=VMEM_LIMIT),
    )(jnp.reshape(q, (1,)).astype(jnp.int32), d_h, xt, *scatter_srcs)


def _swap_with_sibling(grads):
    ns = len(grads)

    def body(*refs):
        copies = _swap_copies(refs[:ns], refs[ns:2 * ns], refs[2 * ns], refs[2 * ns + 1])
        for cp in copies:
            cp.start()
        for cp in copies:
            cp.wait()

    return pl.pallas_call(
        body, name="swap_with_sibling", out_shape=_swap_shapes(grads), in_specs=[ANY] * ns, out_specs=[ANY] * ns,
        scratch_shapes=[pltpu.SemaphoreType.DMA((ns,)), pltpu.SemaphoreType.DMA((ns,))],
    )(*grads)


def _token_tile(T, want):
    return math.gcd(T, want)


def kernel(x, p, w_in, pool_w, pool_scale, sgu_ln_g, sgu_ln_b, sgu_w, sgu_b, w_out, ln_g, ln_b, ple_w, ple_gate_w, ple_gate_b, loss_target, m_w_in, m_pool_w, m_pool_scale, m_sgu_ln_g, m_sgu_ln_b, m_sgu_w, m_sgu_b, m_w_out, m_ln_g, m_ln_b, m_ple_w, m_ple_gate_w, m_ple_gate_b, v_w_in, v_pool_w, v_pool_scale, v_sgu_ln_g, v_sgu_ln_b, v_sgu_w, v_sgu_b, v_w_out, v_ln_g, v_ln_b, v_ple_w, v_ple_gate_w, v_ple_gate_b):
    c = lax.axis_index("c")
    T, D = x.shape[1], x.shape[2]
    tm, tm_vpu, tm_acc = _token_tile(T, 512), _token_tile(T, 256), _token_tile(T, 2048)
    x2, p2, tgt = x[0], p[0, 0], loss_target[0]
    G, PGQ, PG = pool_w.shape[1], pool_w.shape[2], pool_w.shape[3]

    w_in_f, pool_f = _gather_weights([w_in[0], pool_w[0].reshape(G * PGQ, PG)])
    pool_f = pool_f.reshape(4, G, PGQ, PG).transpose(1, 0, 2, 3).reshape(G, 4 * PGQ, PG)
    tril = jnp.tril(jnp.ones((CHUNK, CHUNK), dtype=bool))
    sgu_wm = jnp.where(tril[None], sgu_w[0], 0.0).astype(BF16)
    sgu_bias_t = sgu_b[0].T

    kept, y, xt, w_out_f, w_gate_f, w_ple_f = _front_forward(
        x2, w_in_f, pool_f, pool_scale, sgu_ln_g, sgu_ln_b, sgu_wm, sgu_bias_t, [w_out[0], ple_gate_w[0], ple_w[0]],
        tm_vpu)
    w_out_f = w_out_f.reshape(-1, D)
    w_gate_f = w_gate_f.reshape(-1, D)
    (dx_part, d_y, d_w_out, d_w_gate, d_w_ple, d_ln_g, d_ln_b, d_gate_b, ssq) = _tail(
        y, x2, p2, tgt, w_out_f, w_gate_f, w_ple_f, ln_g, ln_b, ple_gate_b, tm)
    d_h, d_x, d_pool_w, d_pool_scale, d_sgu_g, d_sgu_b, d_sgu_w, d_sgu_bias_t = _front_backward(
        kept, d_y, dx_part, w_in_f, pool_f, pool_scale, sgu_ln_g, sgu_ln_b, sgu_wm, sgu_bias_t, tm_vpu)
    grads = [d_w_out.reshape(4, -1, D), d_w_gate.reshape(4, -1, D), d_w_ple,
             d_pool_w.reshape(G, 4, PGQ, PG).transpose(1, 0, 2, 3).reshape(4, G * PGQ, PG)]
    grads = [g.reshape(4, 2, g.shape[1] // 2, g.shape[2]) for g in grads]
    d_sgu_w = jnp.where(tril[None], d_sgu_w, 0.0)
    small_names = ["pool_scale", "sgu_ln_g", "sgu_ln_b", "ln_g", "ln_b", "ple_gate_b", "sgu_b", "sgu_w"]
    small_grads = [d_pool_scale, d_sgu_g, d_sgu_b, d_ln_g, d_ln_b, d_gate_b, d_sgu_bias_t.T, d_sgu_w]

    def pack(arrays, extra):
        rows = [a.reshape(-1) for a in arrays[:6]]
        rows.append(jnp.pad(arrays[6].reshape(-1), (0, D - arrays[6].size)))
        rows.append(jnp.pad(jnp.reshape(extra, (1,)), (0, D - 1)))
        return jnp.concatenate([r.reshape(-1, D) for r in rows] + [arrays[7].reshape(-1, D)], axis=0)

    def unpack(packed, like):
        out = [packed[k].reshape(like[k].shape) for k in range(6)]
        out.append(packed[6, :like[6].size].reshape(like[6].shape))
        out.append(packed[8:].reshape(like[7].shape))
        return out

    small = pack(small_grads, (0.5 / D) * jnp.sum(ssq))
    q = 2 * lax.axis_index("x") + lax.axis_index("y")
    *landed, small_landed = _swap_with_sibling(grads + [small])
    *parts, small_chip = _add_own_halves(grads, landed, small, small_landed, c)
    d_w_in, *slots, small_slots = _weight_backward(d_h, xt, q, parts + [small_chip], tm_acc)
    *halves, small_total = _sum_fours(parts, slots, small_chip, small_slots, q)
    sibling_halves = _join_halves_with_sibling(halves)
    loss = small_total[7, 0]

    big_names = ["w_out", "ple_gate_w", "ple_w", "pool_w"]
    given = dict(w_in=(w_in, m_w_in, v_w_in), w_out=(w_out, m_w_out, v_w_out),
                 ple_gate_w=(ple_gate_w, m_ple_gate_w, v_ple_gate_w), ple_w=(ple_w, m_ple_w, v_ple_w),
                 pool_w=(pool_w, m_pool_w, v_pool_w), pool_scale=(pool_scale, m_pool_scale, v_pool_scale),
                 sgu_ln_g=(sgu_ln_g, m_sgu_ln_g, v_sgu_ln_g), sgu_ln_b=(sgu_ln_b, m_sgu_ln_b, v_sgu_ln_b),
                 sgu_w=(sgu_w, m_sgu_w, v_sgu_w), sgu_b=(sgu_b, m_sgu_b, v_sgu_b), ln_g=(ln_g, m_ln_g, v_ln_g),
                 ln_b=(ln_b, m_ln_b, v_ln_b), ple_gate_b=(ple_gate_b, m_ple_gate_b, v_ple_gate_b))
    grad, delta, new_m, new_v = {}, {}, {}, {}
    grad["w_in"], delta["w_in"], new_m["w_in"], new_v["w_in"] = (
        t[None] for t in _adamw(w_in[0], d_w_in, m_w_in[0], v_w_in[0], "adamw_w_in"))
    flat = [(2 * g.shape[0], g.shape[1]) for g in halves]
    small_w, small_m, small_v = (pack([given[n][k] for n in small_names], 0.0) for k in range(3))
    *big_out, small_out = _adamw_joined(
        [given[n][0].reshape(f) for n, f in zip(big_names, flat)], halves, sibling_halves,
        [given[n][1].reshape(f) for n, f in zip(big_names, flat)],
        [given[n][2].reshape(f) for n, f in zip(big_names, flat)], (small_w, small_total, small_m, small_v), c)
    for name, outs in zip(big_names, big_out):
        grad[name], delta[name], new_m[name], new_v[name] = (t.reshape(given[name][0].shape) for t in outs)
    like = [given[n][0] for n in small_names]
    for k, name in enumerate(small_names):
        grad[name], delta[name], new_m[name], new_v[name] = (unpack(t, like)[k] for t in small_out)

    order = ["w_in", "pool_w", "pool_scale", "sgu_ln_g", "sgu_ln_b", "sgu_w", "sgu_b", "w_out", "ln_g", "ln_b",
             "ple_w", "ple_gate_w", "ple_gate_b"]
    return (loss, d_x[None], *[grad[n] for n in order], *[delta[n] for n in order],
            *[new_m[n] for n in order], *[new_v[n] for n in order])
```

```python
import functools
import math

import jax
import jax.numpy as jnp
from jax import lax
from jax.experimental import pallas as pl
from jax.experimental.pallas import tpu as pltpu

F32, BF16 = jnp.float32, jnp.bfloat16
MESH = pl.DeviceIdType.MESH
ANY = pl.BlockSpec(memory_space=pl.ANY)

POOL_WINDOWS = (2, 4, 8, 16)
HALO = 16
CHUNK = 128
MXU_COLS = 256
LN_EPS = 1e-5
DEEPNORM_ALPHA = 2.0 ** 0.25
ADAM_LR, ADAM_B1, ADAM_B2, ADAM_EPS, ADAM_WD, ADAM_STEP = 1e-3, 0.9, 0.999, 1e-8, 0.01, 10
VMEM_LIMIT = 56 * 1024 * 1024
GELU_K = math.sqrt(2.0 / math.pi)
GELU_C = 0.044715
SAVED = {"pooled": 0, "gelu_u": 1, "dgelu_u": 2, "vhat": 3, "rstd_dgelu_v": 4, "silu": 5, "dsilu": 7}
SAVED_WIDTH = 9


def _mm(a, b):
    return jnp.dot(a, b, preferred_element_type=F32)


def _mm_nt(a, b):
    return lax.dot_general(a, b, (((1,), (1,)), ((), ())), preferred_element_type=F32)


def _mm_tn(a, b):
    return lax.dot_general(a, b, (((0,), (0,)), ((), ())), preferred_element_type=F32)


def _gelu_and_grad(x):
    x2 = x * x
    t = jnp.tanh(x * (GELU_K + (GELU_K * GELU_C) * x2))
    hx = 0.5 * x
    g = hx + hx * t
    dg = (0.5 + 0.5 * t) + (hx - hx * t * t) * (GELU_K + (3.0 * GELU_K * GELU_C) * x2)
    return g, dg


def _silu_and_grad(z):
    sig = jax.nn.sigmoid(z)
    zs = z * sig
    return zs, sig + zs * (1.0 - sig)


def _norm_rows(x):
    mu = jnp.mean(x, axis=-1, keepdims=True)
    xc = x - mu
    var = jnp.mean(xc * xc, axis=-1, keepdims=True)
    rstd = lax.rsqrt(var + LN_EPS)
    return xc * rstd, rstd


def _norm_rows_bwd(dxhat, xhat, rstd):
    m1 = jnp.mean(dxhat, axis=-1, keepdims=True)
    m2 = jnp.mean(dxhat * xhat, axis=-1, keepdims=True)
    return rstd * (dxhat - m1 - xhat * m2)


def _inv_count(row0, rows, w):
    t = row0 + lax.broadcasted_iota(jnp.int32, (rows, 1), 0)
    return 1.0 / jnp.minimum(t + 1, w).astype(F32)


def _causal_window_sum(ext, w):
    s, sh = ext, 1
    while sh < w:
        s = s + pltpu.roll(s, sh, axis=0)
        sh *= 2
    return s[HALO:, :]


def _anticausal_window_sum(ext, w):
    n, s, sh = ext.shape[0], ext, 1
    while sh < w:
        s = s + pltpu.roll(s, n - sh, axis=0)
        sh *= 2
    return s[: n - HALO, :]


def _place():
    return lax.axis_index("x"), lax.axis_index("y"), lax.axis_index("c")


def _gather_weights(shards):
    n = len(shards)
    piece = [s.shape[0] // 4 for s in shards]

    def body(*refs):
        wide, dsts, srcs = refs[:n], refs[n:2 * n], refs[2 * n:3 * n]
        send_sems, recv_sems, local_sems = refs[3 * n:]
        for k in range(n):
            for r0 in range(0, 4 * piece[k], CHUNK):
                srcs[k][r0:r0 + CHUNK, :] = wide[k][r0:r0 + CHUNK, :].astype(BF16)
        x, y, c = _place()
        me, sibling = (x, y, c), (x, y, 1 - c)
        across_x, across_y = (1 - x, y, c), (x, 1 - y, c)
        q, qx, qy, qf = 2 * x + y, 2 * (1 - x) + y, 2 * x + (1 - y), 2 * (1 - x) + (1 - y)

        def rows(ref, cc, p, k):
            return ref.at[pl.ds((2 * cc + p) * piece[k], piece[k])]

        def copy(k, sem, qq, cc, p, to, own=False):
            landing = rows(dsts[k].at[qq], cc, p, k)
            return pltpu.make_async_remote_copy(
                src_ref=rows(srcs[k], cc, p, k) if own else landing, dst_ref=landing,
                send_sem=send_sems.at[12 * k + sem], recv_sem=recv_sems.at[12 * k + sem],
                device_id=to, device_id_type=MESH)

        started = []

        def go(cp):
            cp.start()
            started.append(cp)

        mine = [pltpu.make_async_copy(srcs[k], dsts[k].at[q], local_sems.at[k]) for k in range(n)]
        for cp in mine:
            cp.start()
        for k in range(n):
            for p in range(2):
                go(copy(k, p, q, c, p, across_x, own=True))
                go(copy(k, 2 + p, q, c, p, across_y, own=True))
        for k in range(n):
            copy(k, 0, qx, c, 0, me).wait_recv()
            go(copy(k, 4, qx, c, 0, across_y))
            go(copy(k, 6, qx, c, 0, sibling))
            copy(k, 3, qy, c, 1, me).wait_recv()
            go(copy(k, 5, qy, c, 1, across_x))
            go(copy(k, 9, qy, c, 1, sibling))
        for k in range(n):
            copy(k, 1, qx, c, 1, me).wait_recv()
            go(copy(k, 7, qx, c, 1, sibling))
            copy(k, 2, qy, c, 0, me).wait_recv()
            go(copy(k, 8, qy, c, 0, sibling))
        for k in range(n):
            copy(k, 4, qf, c, 0, me).wait_recv()
            go(copy(k, 10, qf, c, 0, sibling))
            copy(k, 5, qf, c, 1, me).wait_recv()
            go(copy(k, 11, qf, c, 1, sibling))
        for k in range(n):
            for sem, qq, p in ((6, qx, 0), (7, qx, 1), (8, qy, 0), (9, qy, 1), (10, qf, 0), (11, qf, 1)):
                copy(k, sem, qq, 1 - c, p, me).wait_recv()
        for cp in started:
            cp.wait_send()
        for cp in mine:
            cp.wait()

    return pl.pallas_call(
        body, name="gather_weights",
        out_shape=[jax.ShapeDtypeStruct((4,) + s.shape, BF16) for s in shards],
        in_specs=[pl.BlockSpec(memory_space=pltpu.VMEM)] * n, out_specs=[ANY] * n,
        scratch_shapes=[pltpu.VMEM(s.shape, BF16) for s in shards]
        + [pltpu.SemaphoreType.DMA((12 * n,)), pltpu.SemaphoreType.DMA((12 * n,)), pltpu.SemaphoreType.DMA((n,))],
        compiler_params=pltpu.CompilerParams(vmem_limit_bytes=VMEM_LIMIT),
    )(*shards)


def _direct_gather_copies(srcs, dsts, send_sems, recv_sems):
    x, y, c = _place()
    q = 2 * x + y
    sends, recvs = [], []
    for k, (src, dst) in enumerate(zip(srcs, dsts)):
        half = src.shape[0] // 2
        for j, chip in enumerate([(1 - x, y), (x, 1 - y), (1 - x, 1 - y)]):
            for core in range(2):
                sends.append(pltpu.make_async_remote_copy(
                    src_ref=src.at[pl.ds(c * half, half)], dst_ref=dst.at[q, pl.ds(c * half, half)],
                    send_sem=send_sems.at[6 * k + 2 * j + core], recv_sem=recv_sems.at[6 * k + 2 * j + c],
                    device_id=(*chip, core), device_id_type=MESH))
                landed = dst.at[2 * chip[0] + chip[1], pl.ds(core * half, half)]
                recvs.append(pltpu.make_async_remote_copy(
                    src_ref=landed, dst_ref=landed, send_sem=send_sems.at[6 * k + 2 * j + core],
                    recv_sem=recv_sems.at[6 * k + 2 * j + core], device_id=(x, y, c), device_id_type=MESH))
    return sends, recvs


def _swap_copies(srcs, dsts, send_sems, recv_sems):
    x, y, c = _place()
    return [pltpu.make_async_remote_copy(
        src_ref=src.at[:, 1 - c] if len(src.shape) == 4 else src, dst_ref=dst,
        send_sem=send_sems.at[k], recv_sem=recv_sems.at[k], device_id=(x, y, 1 - c), device_id_type=MESH)
        for k, (src, dst) in enumerate(zip(srcs, dsts))]


def _swap_shapes(grads):
    return [jax.ShapeDtypeStruct((4,) + g.shape[2:] if g.ndim == 4 else g.shape, g.dtype) for g in grads]


def _scatter_copies(srcs, dsts, send_sems, recv_sems):
    x, y, c = _place()
    copies = []
    for j, chip in enumerate([(1 - x, y), (x, 1 - y), (1 - x, 1 - y)]):
        for k, (src, dst) in enumerate(zip(srcs, dsts)):
            copies.append(pltpu.make_async_remote_copy(
                src_ref=src.at[2 * chip[0] + chip[1]] if len(src.shape) == 3 else src, dst_ref=dst.at[j],
                send_sem=send_sems.at[3 * k + j], recv_sem=recv_sems.at[3 * k + j],
                device_id=(*chip, c), device_id_type=MESH))
    return copies


def _scatter_shapes(parts):
    return [jax.ShapeDtypeStruct((3,) + (p.shape[1:] if p.ndim == 3 else p.shape), p.dtype) for p in parts]


def _join_halves_with_sibling(halves):
    n = len(halves)

    def body(*refs):
        srcs, dsts = refs[:n], refs[n:2 * n]
        send_sems, recv_sems = refs[2 * n:]
        x, y, c = _place()
        copies = [pltpu.make_async_remote_copy(
            src_ref=srcs[k], dst_ref=dsts[k], send_sem=send_sems.at[k], recv_sem=recv_sems.at[k],
            device_id=(x, y, 1 - c), device_id_type=MESH) for k in range(n)]
        for cp in copies:
            cp.start()
        for cp in copies:
            cp.wait()

    return pl.pallas_call(
        body, name="join_halves",
        out_shape=[jax.ShapeDtypeStruct(h.shape, h.dtype) for h in halves],
        in_specs=[ANY] * n, out_specs=[ANY] * n,
        scratch_shapes=[pltpu.SemaphoreType.DMA((n,)), pltpu.SemaphoreType.DMA((n,))],
    )(*halves)


def _row_block(rows, cols, n_arrays):
    cap = max(8, (VMEM_LIMIT // 4) // (8 * n_arrays * cols))
    rb = rows
    while rb > cap and rb % 2 == 0:
        rb //= 2
    return rb


def _scalar(value):
    return jnp.reshape(value, (1,)).astype(jnp.int32)


def _whole(a):
    return pl.BlockSpec(a.shape, lambda i, s_ref: (0,) * a.ndim)


def _add_own_halves(grads, landed, small, small_landed, c):
    n = len(grads)

    def body(c_ref, *refs):
        g, l, (s, sl), o, so = refs[:n], refs[n:2 * n], refs[2 * n:2 * n + 2], refs[2 * n + 2:3 * n + 2], refs[3 * n + 2]
        for k in range(n):
            o[k][...] = (g[k][...] + l[k][...]).astype(BF16)
        so[...] = s[...] + sl[...]

    def half(a):
        return pl.BlockSpec((4, a.shape[1] // 2, a.shape[2]), lambda i, c_ref: (0, i, 0))

    return pl.pallas_call(
        body, name="add_own_halves",
        out_shape=[jax.ShapeDtypeStruct(a.shape, BF16) for a in landed] + [jax.ShapeDtypeStruct(small.shape, F32)],
        grid_spec=pltpu.PrefetchScalarGridSpec(
            num_scalar_prefetch=1, grid=(2,),
            in_specs=[pl.BlockSpec((4, None, a.shape[2] // 2, a.shape[3]), lambda i, c_ref: (0, c_ref[0], i, 0))
                      for a in grads] + [half(a) for a in landed] + [_whole(small), _whole(small_landed)],
            out_specs=[half(a) for a in landed] + [_whole(small)]),
        compiler_params=pltpu.CompilerParams(vmem_limit_bytes=VMEM_LIMIT),
    )(_scalar(c), *grads, *landed, small, small_landed)


def _sum_fours(parts, slots, small_chip, small_slots, q):
    n = len(parts)

    def four(own, s):
        return (own[...].astype(F32) + s[0].astype(F32)) + (s[1].astype(F32) + s[2].astype(F32))

    def body(q_ref, *refs):
        p, s, (sc, ss), o, so = refs[:n], refs[n:2 * n], refs[2 * n:2 * n + 2], refs[2 * n + 2:3 * n + 2], refs[3 * n + 2]
        for k in range(n):
            o[k][...] = four(p[k], s[k])
        so[...] = four(sc, ss)

    return pl.pallas_call(
        body, name="sum_fours",
        out_shape=[jax.ShapeDtypeStruct(a.shape[1:], F32) for a in parts] + [jax.ShapeDtypeStruct(small_chip.shape, F32)],
        grid_spec=pltpu.PrefetchScalarGridSpec(
            num_scalar_prefetch=1, grid=(2,),
            in_specs=[pl.BlockSpec((None, a.shape[1] // 2, a.shape[2]), lambda i, q_ref: (q_ref[0], i, 0)) for a in parts]
            + [pl.BlockSpec((3, a.shape[1] // 2, a.shape[2]), lambda i, q_ref: (0, i, 0)) for a in slots]
            + [_whole(small_chip), _whole(small_slots)],
            out_specs=[pl.BlockSpec((a.shape[1] // 2, a.shape[2]), lambda i, q_ref: (i, 0)) for a in parts]
            + [_whole(small_chip)]),
        compiler_params=pltpu.CompilerParams(vmem_limit_bytes=VMEM_LIMIT),
    )(_scalar(q), *parts, *slots, small_chip, small_slots)


def _adamw_math(w, g, m, v):
    nm = ADAM_B1 * m + (1.0 - ADAM_B1) * g
    nv = ADAM_B2 * v + (1.0 - ADAM_B2) * (g * g)
    m_hat = nm / (1.0 - ADAM_B1 ** ADAM_STEP)
    v_hat = nv / (1.0 - ADAM_B2 ** ADAM_STEP)
    return -ADAM_LR * (m_hat / (jnp.sqrt(v_hat) + ADAM_EPS) + ADAM_WD * w), nm, nv


def _adamw(w, g, m, v, name):
    rows, cols = w.shape
    rb = _row_block(rows, cols, 8)

    def body(w_ref, g_ref, m_ref, v_ref, go_ref, d_ref, nm_ref, nv_ref):
        go_ref[...] = g_ref[...]
        d_ref[...], nm_ref[...], nv_ref[...] = _adamw_math(w_ref[...], g_ref[...], m_ref[...], v_ref[...])

    spec = pl.BlockSpec((rb, cols), lambda r: (r, 0))
    out = jax.ShapeDtypeStruct(w.shape, F32)
    return pl.pallas_call(body, name=name, out_shape=[out] * 4, grid=(rows // rb,),
                          in_specs=[spec] * 4, out_specs=[spec] * 4)(w, g, m, v)


def _adamw_joined(ws, g_mine, g_sibling, ms, vs, small, c):
    n = len(ws)

    def body(c_ref, *refs):
        ins, outs = refs[:5 * n + 4], refs[5 * n + 4:]
        mine = c_ref[0] == pl.program_id(0)
        for k in range(n):
            w, gm, gs, m, v = ins[5 * k:5 * k + 5]
            g = jnp.where(mine, gm[...], gs[...])
            outs[4 * k][...] = g
            outs[4 * k + 1][...], outs[4 * k + 2][...], outs[4 * k + 3][...] = _adamw_math(w[...], g, m[...], v[...])
        w, g, m, v = ins[5 * n:]
        outs[4 * n][...] = g[...]
        outs[4 * n + 1][...], outs[4 * n + 2][...], outs[4 * n + 3][...] = _adamw_math(w[...], g[...], m[...], v[...])

    def half(a):
        return pl.BlockSpec((a.shape[0] // 2, a.shape[1]), lambda hf, c_ref: (hf, 0))

    in_specs, operands = [], []
    for k in range(n):
        in_specs += [half(ws[k]), _whole(g_mine[k]), _whole(g_sibling[k]), half(ms[k]), half(vs[k])]
        operands += [ws[k], g_mine[k], g_sibling[k], ms[k], vs[k]]
    outs = pl.pallas_call(
        body, name="adamw_joined",
        out_shape=[jax.ShapeDtypeStruct(w.shape, F32) for w in ws for _ in range(4)]
        + [jax.ShapeDtypeStruct(small[0].shape, F32)] * 4,
        grid_spec=pltpu.PrefetchScalarGridSpec(
            num_scalar_prefetch=1, grid=(2,),
            in_specs=in_specs + [_whole(a) for a in small],
            out_specs=[half(w) for w in ws for _ in range(4)] + [_whole(small[0])] * 4),
        compiler_params=pltpu.CompilerParams(vmem_limit_bytes=VMEM_LIMIT),
    )(_scalar(c), *operands, *small)
    return [outs[4 * k:4 * k + 4] for k in range(n + 1)]


def _front_forward(x, w_in, pool_w, pool_scale, sgu_g, sgu_b, sgu_wm, sgu_bias_t, later_shards, tm):
    T, D = x.shape
    nq, _, cq = w_in.shape
    G, PG = pool_w.shape[0], pool_w.shape[1]
    nt = T // tm
    bpd = D // PG
    nl = len(later_shards)

    def body(x_ref, win_any, pw_any, ps_ref, lg_ref, lb_ref, sw_ref, sb_ref, *refs):
        shards_any, (keep_ref, y_ref, xt_ref), gathered = refs[:nl], refs[nl:nl + 3], refs[nl + 3:2 * nl + 3]
        wide, narrow = refs[2 * nl + 3:3 * nl + 3], refs[3 * nl + 3:4 * nl + 3]
        win_v, pw_v, carry, sems, load_sems, own_sems, send_sems, recv_sems = refs[4 * nl + 3:]
        i = pl.program_id(0)
        own_quarter = 2 * lax.axis_index("x") + lax.axis_index("y")

        def own_copies():
            return [pltpu.make_async_copy(narrow[k], gathered[k].at[own_quarter], own_sems.at[k]) for k in range(nl)]

        @pl.when(i == 0)
        def _():
            c1 = pltpu.make_async_copy(win_any, win_v, sems.at[0])
            c2 = pltpu.make_async_copy(pw_any, pw_v, sems.at[1])
            loads = [pltpu.make_async_copy(shards_any[k], wide[k], load_sems.at[k]) for k in range(nl)]
            for cp in [c1, c2] + loads:
                cp.start()
            carry[...] = jnp.zeros_like(carry)
            for k in range(nl):
                loads[k].wait()
                for r0 in range(0, wide[k].shape[0], CHUNK):
                    narrow[k][r0:r0 + CHUNK, :] = wide[k][r0:r0 + CHUNK, :].astype(BF16)
            for cp in own_copies() + _direct_gather_copies(narrow, gathered, send_sems, recv_sems)[0]:
                cp.start()
            c1.wait()
            c2.wait()

        xb = x_ref[...].astype(BF16)
        xt_ref[...] = x_ref[...].T.astype(BF16)

        def h_block(j):
            qq, off = divmod(j * PG, cq)
            return _mm(xb, win_v[qq, :, off:off + PG])

        def keep(part, col, value):
            keep_ref[:, SAVED[part] * D + col:SAVED[part] * D + col + PG] = value.astype(BF16)

        def ahead(stage):
            if stage < G:
                return h_block(stage), h_block(3 * bpd + stage)
            if stage < G + bpd:
                hd = stage - G
                return h_block(bpd + hd), h_block(2 * bpd + hd), h_block(4 * bpd + hd)
            return None

        blocks = ahead(0)

        for g, w in enumerate(POOL_WINDOWS):
            sl = slice(g * PG, (g + 1) * PG)
            a, z = blocks
            blocks = ahead(g + 1)
            ext = jnp.concatenate([carry[:, sl], a], axis=0)
            carry[:, sl] = a[tm - HALO:, :]
            pooled = (_causal_window_sum(ext, w) * _inv_count(i * tm, tm, w) - a).astype(BF16)
            mixed = _mm(pooled, pw_v[g])
            zs, dzs = _silu_and_grad(z)
            keep("pooled", g * PG, pooled)
            keep("silu", g * PG, zs)
            keep("dsilu", g * PG, dzs)
            y_ref[:, sl] = (mixed * ps_ref[:, sl] * zs).astype(BF16)

        for hd in range(bpd):
            sl = slice(hd * PG, (hd + 1) * PG)
            u, v, z = blocks
            blocks = ahead(G + hd + 1)
            ug, dug = _gelu_and_grad(u)
            vg, dvg = _gelu_and_grad(v)
            vhat, rstd = _norm_rows(vg)
            zs, dzs = _silu_and_grad(z)
            keep("gelu_u", hd * PG, ug)
            keep("dgelu_u", hd * PG, dug)
            keep("vhat", hd * PG, vhat)
            keep("rstd_dgelu_v", hd * PG, rstd * dvg)
            keep("silu", D + hd * PG, zs)
            keep("dsilu", D + hd * PG, dzs)
            vn = (vhat * lg_ref[:, sl] + lb_ref[:, sl]).astype(BF16)
            gated = ug * zs
            for n in range(tm // CHUNK):
                rs = slice(n * CHUNK, (n + 1) * CHUNK)
                sv = _mm(sw_ref[hd], vn[rs, :]) + sb_ref[:, hd:hd + 1]
                y_ref[rs, D + hd * PG:D + (hd + 1) * PG] = (gated[rs, :] * sv).astype(BF16)

        @pl.when(i == nt - 1)
        def _():
            sends, recvs = _direct_gather_copies(narrow, gathered, send_sems, recv_sems)
            for cp in sends:
                cp.wait_send()
            for cp in recvs:
                cp.wait_recv()
            for cp in own_copies():
                cp.wait()

    vec = pl.BlockSpec((1, D), lambda i: (0, 0))
    return pl.pallas_call(
        body, name="front_forward",
        out_shape=[jax.ShapeDtypeStruct((T, SAVED_WIDTH * D), BF16), jax.ShapeDtypeStruct((T, 2 * D), BF16),
                   jax.ShapeDtypeStruct((D, T), BF16)]
        + [jax.ShapeDtypeStruct((4,) + s.shape, BF16) for s in later_shards],
        grid=(nt,),
        in_specs=[pl.BlockSpec((tm, D), lambda i: (i, 0)), ANY, ANY, vec, vec, vec,
                  pl.BlockSpec(sgu_wm.shape, lambda i: (0, 0, 0)), pl.BlockSpec(sgu_bias_t.shape, lambda i: (0, 0))]
        + [ANY] * nl,
        out_specs=[pl.BlockSpec((tm, SAVED_WIDTH * D), lambda i: (i, 0)), pl.BlockSpec((tm, 2 * D), lambda i: (i, 0)),
                   pl.BlockSpec((D, tm), lambda i: (0, i))] + [ANY] * nl,
        scratch_shapes=[pltpu.VMEM(s.shape, F32) for s in later_shards]
        + [pltpu.VMEM(s.shape, BF16) for s in later_shards]
        + [pltpu.VMEM(w_in.shape, BF16), pltpu.VMEM(pool_w.shape, BF16), pltpu.VMEM((HALO, D), F32),
           pltpu.SemaphoreType.DMA((2,)), pltpu.SemaphoreType.DMA((nl,)), pltpu.SemaphoreType.DMA((nl,)),
           pltpu.SemaphoreType.DMA((6 * nl,)), pltpu.SemaphoreType.DMA((6 * nl,))],
        compiler_params=pltpu.CompilerParams(dimension_semantics=("arbitrary",), vmem_limit_bytes=VMEM_LIMIT),
    )(x, w_in, pool_w, pool_scale, sgu_g, sgu_b, sgu_wm, sgu_bias_t, *later_shards)


def _tail(y, x, p, target, w_out, w_gate, w_ple, ln_g, ln_b, gate_b, tm):
    T, D = x.shape
    K = p.shape[1]
    nq, _, cq = w_ple.shape
    nt = T // tm

    def body(y_ref, x_ref, p_ref, t_ref, wout_any, wg_any, wp_any, lng_ref, lnb_ref, bg_ref,
             dxp_ref, dy_ref, dwout_any, dwg_any, dwp_any, dlng_ref, dlnb_ref, dbg_ref, ssq_ref,
             wout_v, wg_v, wp_v, dwout_acc, dwg_acc, dwp_acc, sems):
        i = pl.program_id(0)

        @pl.when(i == 0)
        def _():
            loads = [pltpu.make_async_copy(s, d, sems.at[k])
                     for k, (s, d) in enumerate(((wout_any, wout_v), (wg_any, wg_v), (wp_any, wp_v)))]
            for cp in loads:
                cp.start()
            for ref in (dwout_acc, dwg_acc, dwp_acc, dlng_ref, dlnb_ref, dbg_ref, ssq_ref):
                ref[...] = jnp.zeros_like(ref)
            for cp in loads:
                cp.wait()

        yb = y_ref[...]
        pb = p_ref[...].astype(BF16)
        xhat, rstd = _norm_rows(DEEPNORM_ALPHA * x_ref[...] + _mm(yb, wout_v[...]))
        x1 = xhat * lng_ref[...] + lnb_ref[...]
        x1b = x1.astype(BF16)
        gate = jax.nn.sigmoid(_mm(x1b, wg_v[...]) + bg_ref[...])
        e = jnp.concatenate([_mm(pb, wp_v[qq]) for qq in range(nq)], axis=1)
        diff = x1 + gate * e - t_ref[...]
        ssq_ref[...] += jnp.sum(diff * diff, axis=0, keepdims=True)

        dout = diff * (1.0 / D)
        d_e = (dout * gate).astype(BF16)
        dgl = dout * e * gate * (1.0 - gate)
        dglb = dgl.astype(BF16)
        for qq in range(nq):
            dwp_acc[qq] += _mm_tn(pb, d_e[:, qq * cq:(qq + 1) * cq])
        for c0 in range(0, D, MXU_COLS):
            dwg_acc[:, c0:c0 + MXU_COLS] += _mm_tn(x1b, dglb[:, c0:c0 + MXU_COLS])
        dbg_ref[...] += jnp.sum(dgl, axis=0, keepdims=True)
        d_x1 = dout + _mm_nt(dglb, wg_v[...])
        dlng_ref[...] += jnp.sum(d_x1 * xhat, axis=0, keepdims=True)
        dlnb_ref[...] += jnp.sum(d_x1, axis=0, keepdims=True)
        d_r = _norm_rows_bwd(d_x1 * lng_ref[...], xhat, rstd)
        drb = d_r.astype(BF16)
        dxp_ref[...] = DEEPNORM_ALPHA * d_r
        for c0 in range(0, D, MXU_COLS):
            dwout_acc[:, c0:c0 + MXU_COLS] += _mm_tn(yb, drb[:, c0:c0 + MXU_COLS])
        for c0 in range(0, 2 * D, 2 * MXU_COLS):
            dy_ref[:, c0:c0 + 2 * MXU_COLS] = _mm_nt(drb, wout_v[c0:c0 + 2 * MXU_COLS, :]).astype(BF16)

        @pl.when(i == nt - 1)
        def _():
            stores = [pltpu.make_async_copy(s, d, sems.at[k])
                      for k, (s, d) in enumerate(((dwout_acc, dwout_any), (dwg_acc, dwg_any), (dwp_acc, dwp_any)))]
            for cp in stores:
                cp.start()
            for cp in stores:
                cp.wait()

    vec = pl.BlockSpec((1, D), lambda i: (0, 0))
    vec_shape = jax.ShapeDtypeStruct((1, D), F32)

    def tile(cols):
        return pl.BlockSpec((tm, cols), lambda i: (i, 0))

    return pl.pallas_call(
        body, name="tail",
        out_shape=[jax.ShapeDtypeStruct((T, D), F32), jax.ShapeDtypeStruct((T, 2 * D), BF16),
                   jax.ShapeDtypeStruct(w_out.shape, F32), jax.ShapeDtypeStruct(w_gate.shape, F32),
                   jax.ShapeDtypeStruct(w_ple.shape, F32), vec_shape, vec_shape, vec_shape, vec_shape],
        grid=(nt,),
        in_specs=[tile(2 * D), tile(D), tile(K), tile(D), ANY, ANY, ANY, vec, vec, vec],
        out_specs=[tile(D), tile(2 * D), ANY, ANY, ANY, vec, vec, vec, vec],
        scratch_shapes=[pltpu.VMEM(w_out.shape, BF16), pltpu.VMEM(w_gate.shape, BF16), pltpu.VMEM(w_ple.shape, BF16),
                        pltpu.VMEM(w_out.shape, F32), pltpu.VMEM(w_gate.shape, F32), pltpu.VMEM(w_ple.shape, F32),
                        pltpu.SemaphoreType.DMA((3,))],
        compiler_params=pltpu.CompilerParams(dimension_semantics=("arbitrary",), vmem_limit_bytes=VMEM_LIMIT),
    )(y, x, p, target, w_out, w_gate, w_ple, ln_g, ln_b, gate_b)


def _front_backward(kept, d_y, dx_part, w_in, pool_w, pool_scale, sgu_g, sgu_b, sgu_wm, sgu_bias_t, tm):
    T = kept.shape[0]
    D = kept.shape[1] // SAVED_WIDTH
    nq, _, cq = w_in.shape
    G, PG = pool_w.shape[0], pool_w.shape[1]
    nt = T // tm

    def tile_of(i):
        return nt - 1 - jnp.minimum(i, nt - 1)

    def body(kept_ref, dy_ref, dxp_ref, win_any, pw_ref, ps_ref, lg_ref, lb_ref, sw_ref, sb_ref,
             dh_ref, dx_ref, dpw_ref, dps_ref, dlg_ref, dlb_ref, dsw_ref, dsb_ref, win_v, dh_keep, carry, sems):
        i = pl.program_id(0)
        ti = tile_of(i)
        live = (i < nt).astype(F32)

        def saved(part, col, rows=slice(None)):
            return kept_ref[rows, SAVED[part] * D + col:SAVED[part] * D + col + PG]

        @pl.when(i == 0)
        def _():
            cp = pltpu.make_async_copy(win_any, win_v, sems.at[0])
            cp.start()
            carry[...] = jnp.zeros_like(carry)
            dh_keep[...] = jnp.zeros_like(dh_keep)
            for ref in (dpw_ref, dps_ref, dlg_ref, dlb_ref, dsw_ref, dsb_ref):
                ref[...] = jnp.zeros_like(ref)
            cp.wait()

        def dx_columns(r0):
            dx = dxp_ref[:, r0:r0 + MXU_COLS]
            for qq in range(nq):
                dx = dx + _mm_nt(dh_keep[(i + 1) % 2, :, qq * cq:(qq + 1) * cq], win_v[qq, r0:r0 + MXU_COLS, :])
            dx_ref[:, r0:r0 + MXU_COLS] = dx

        dx_chunks = list(range(0, D, MXU_COLS))
        stages = G + D // PG

        for g, w in enumerate(POOL_WINDOWS):
            for r0 in dx_chunks[g * len(dx_chunks) // stages:(g + 1) * len(dx_chunks) // stages]:
                dx_columns(r0)
            sl = slice(g * PG, (g + 1) * PG)
            pooled = saved("pooled", g * PG)
            mixed = _mm(pooled, pw_ref[g])
            dy = dy_ref[:, sl].astype(F32)
            d_ypool = dy * saved("silu", g * PG).astype(F32)
            dh_ref[:, 3 * D + g * PG:3 * D + (g + 1) * PG] = (
                dy * (mixed * ps_ref[:, sl]) * saved("dsilu", g * PG).astype(F32)).astype(BF16)
            dps_ref[:, sl] += live * jnp.sum(d_ypool * mixed, axis=0, keepdims=True)
            d_mixed = (d_ypool * ps_ref[:, sl]).astype(BF16)
            dpw_ref[g] += live * _mm_tn(pooled, d_mixed)
            d_pooled = _mm_nt(d_mixed, pw_ref[g])
            scaled = d_pooled * _inv_count(ti * tm, tm, w)
            after = jnp.concatenate([scaled, carry[:, sl]], axis=0)
            carry[:, sl] = jnp.where(i < nt - 1, scaled[:HALO, :], carry[:, sl])
            dh_ref[:, sl] = (_anticausal_window_sum(after, w) - d_pooled).astype(BF16)

        for hd in range(D // PG):
            for r0 in dx_chunks[(G + hd) * len(dx_chunks) // stages:(G + hd + 1) * len(dx_chunks) // stages]:
                dx_columns(r0)
            sl = slice(hd * PG, (hd + 1) * PG)
            vhat = saved("vhat", hd * PG).astype(F32)
            vn = (vhat * lg_ref[:, sl] + lb_ref[:, sl]).astype(BF16)
            d_vn_chunks = []
            for n in range(tm // CHUNK):
                rs = slice(n * CHUNK, (n + 1) * CHUNK)
                sv = _mm(sw_ref[hd], vn[rs, :]) + sb_ref[:, hd:hd + 1]
                ug = saved("gelu_u", hd * PG, rs).astype(F32)
                dy = dy_ref[rs, D + hd * PG:D + (hd + 1) * PG].astype(F32)
                d_ysgu = dy * saved("silu", D + hd * PG, rs).astype(F32)
                dh_ref[rs, 4 * D + hd * PG:4 * D + (hd + 1) * PG] = (
                    dy * (ug * sv) * saved("dsilu", D + hd * PG, rs).astype(F32)).astype(BF16)
                dh_ref[rs, D + hd * PG:D + (hd + 1) * PG] = (
                    d_ysgu * sv * saved("dgelu_u", hd * PG, rs).astype(F32)).astype(BF16)
                d_sv = d_ysgu * ug
                dsb_ref[:, hd:hd + 1] += live * jnp.sum(d_sv, axis=1, keepdims=True)
                d_svb = d_sv.astype(BF16)
                dsw_ref[hd] += live * _mm_nt(d_svb, vn[rs, :])
                d_vn_chunks.append(_mm_tn(sw_ref[hd], d_svb))
            d_vn = jnp.concatenate(d_vn_chunks, axis=0)
            dlg_ref[:, sl] += live * jnp.sum(d_vn * vhat, axis=0, keepdims=True)
            dlb_ref[:, sl] += live * jnp.sum(d_vn, axis=0, keepdims=True)
            d_vg = _norm_rows_bwd(d_vn * lg_ref[:, sl], vhat, saved("rstd_dgelu_v", hd * PG).astype(F32))
            dh_ref[:, 2 * D + hd * PG:2 * D + (hd + 1) * PG] = d_vg.astype(BF16)

        dh_keep[i % 2] = dh_ref[...]

    vec = pl.BlockSpec((1, D), lambda i: (0, 0))
    vec_shape = jax.ShapeDtypeStruct((1, D), F32)

    def whole(shape):
        return pl.BlockSpec(shape, lambda i: (0,) * len(shape))

    return pl.pallas_call(
        body, name="front_backward",
        out_shape=[jax.ShapeDtypeStruct((T, 5 * D), BF16), jax.ShapeDtypeStruct((T, D), F32),
                   jax.ShapeDtypeStruct(pool_w.shape, F32), vec_shape, vec_shape,
                   vec_shape, jax.ShapeDtypeStruct(sgu_wm.shape, F32), jax.ShapeDtypeStruct(sgu_bias_t.shape, F32)],
        grid=(nt + 1,),
        in_specs=[pl.BlockSpec((tm, SAVED_WIDTH * D), lambda i: (tile_of(i), 0)),
                  pl.BlockSpec((tm, 2 * D), lambda i: (tile_of(i), 0)),
                  pl.BlockSpec((tm, D), lambda i: (jnp.minimum(nt - i, nt - 1), 0)), ANY,
                  whole(pool_w.shape), vec, vec, vec, whole(sgu_wm.shape), whole(sgu_bias_t.shape)],
        out_specs=[pl.BlockSpec((tm, 5 * D), lambda i: (tile_of(i), 0)),
                   pl.BlockSpec((tm, D), lambda i: (jnp.minimum(nt - i, nt - 1), 0)),
                   whole(pool_w.shape), vec, vec, vec, whole(sgu_wm.shape), whole(sgu_bias_t.shape)],
        scratch_shapes=[pltpu.VMEM(w_in.shape, BF16), pltpu.VMEM((2, tm, 5 * D), BF16), pltpu.VMEM((HALO, D), F32),
                        pltpu.SemaphoreType.DMA((1,))],
        compiler_params=pltpu.CompilerParams(dimension_semantics=("arbitrary",), vmem_limit_bytes=VMEM_LIMIT),
    )(kept, d_y, dx_part, w_in, pool_w, pool_scale, sgu_g, sgu_b, sgu_wm, sgu_bias_t)


def _weight_backward(d_h, xt, q, scatter_srcs, tm):
    D, T = xt.shape
    cq = d_h.shape[1] // 4
    hr = D // 2
    nt = T // tm
    ns = len(scatter_srcs)

    def body(q_ref, dh_ref, xt_ref, *refs):
        srcs, out_any, dsts = refs[:ns], refs[ns], refs[ns + 1:2 * ns + 1]
        (acc, land_a, send_b, land_b, mine_f, theirs_f,
         a_send, a_recv, b_send, b_recv, j_sems, o_sems, s_send, s_recv) = refs[2 * ns + 1:]
        s, t = pl.program_id(0), pl.program_id(1)
        x_, y_, c = _place()
        sibling = (x_, y_, 1 - c)
        own_rows = pl.ds(pl.multiple_of(c * hr, hr), hr)
        other_rows = pl.ds(pl.multiple_of((1 - c) * hr, hr), hr)

        @pl.when((s == 0) & (t == 0))
        def _():
            for cp in _scatter_copies(srcs, dsts, s_send, s_recv):
                cp.start()

        @pl.when(t == 0)
        def _():
            acc[...] = jnp.zeros_like(acc)

        for c0 in range(0, cq, MXU_COLS):
            acc[:, c0:c0 + MXU_COLS] += _mm(xt_ref[...], dh_ref[:, c0:c0 + MXU_COLS])

        def pair_sum(slot):
            swap = pltpu.make_async_remote_copy(
                src_ref=acc.at[other_rows], dst_ref=land_a.at[slot], send_sem=a_send.at[slot],
                recv_sem=a_recv.at[slot], device_id=sibling, device_id_type=MESH)
            swap.start()
            swap.wait()
            return acc[own_rows, :] + land_a[slot]

        def to_owner(slot):
            flip_x, flip_y = (slot + 1) >> 1, (slot + 1) & 1
            owner = (1 - x_ if flip_x else x_, 1 - y_ if flip_y else y_, c)
            return pltpu.make_async_remote_copy(
                src_ref=send_b.at[slot], dst_ref=land_b.at[slot], send_sem=b_send.at[slot],
                recv_sem=b_recv.at[slot], device_id=owner, device_id_type=MESH)

        for slot in range(3):
            @pl.when((s == slot) & (t == nt - 1))
            def _(slot=slot):
                send_b[slot] = pair_sum(slot).astype(BF16)
                to_owner(slot).start()

        @pl.when((s == 3) & (t == nt - 1))
        def _():
            own = pair_sum(3)
            for slot in range(3):
                to_owner(slot).wait_recv()
            mine_f[...] = (own + land_b[0].astype(F32)) + (land_b[1].astype(F32) + land_b[2].astype(F32))
            join = pltpu.make_async_remote_copy(
                src_ref=mine_f, dst_ref=theirs_f, send_sem=j_sems.at[0], recv_sem=j_sems.at[1],
                device_id=sibling, device_id_type=MESH)
            join.start()
            out_mine = pltpu.make_async_copy(mine_f, out_any.at[own_rows], o_sems.at[0])
            out_mine.start()
            join.wait()
            out_theirs = pltpu.make_async_copy(theirs_f, out_any.at[other_rows], o_sems.at[1])
            out_theirs.start()
            for slot in range(3):
                to_owner(slot).wait_send()
            for cp in _scatter_copies(srcs, dsts, s_send, s_recv):
                cp.wait()
            out_mine.wait()
            out_theirs.wait()

    def quarter(s, t, q_ref):
        return (t, jnp.where(s == 3, q_ref[0], q_ref[0] ^ (s + 1)))

    dma = pltpu.SemaphoreType.DMA
    return pl.pallas_call(
        body, name="weight_backward",
        out_shape=[jax.ShapeDtypeStruct((D, cq), F32)] + _scatter_shapes(scatter_srcs),
        grid_spec=pltpu.PrefetchScalarGridSpec(
            num_scalar_prefetch=1, grid=(4, nt),
            in_specs=[pl.BlockSpec((tm, cq), quarter), pl.BlockSpec((D, tm), lambda s, t, q_ref: (0, t))] + [ANY] * ns,
            out_specs=[ANY] * (ns + 1),
            scratch_shapes=[pltpu.VMEM((D, cq), F32), pltpu.VMEM((4, hr, cq), F32), pltpu.VMEM((3, hr, cq), BF16),
                            pltpu.VMEM((3, hr, cq), BF16), pltpu.VMEM((hr, cq), F32), pltpu.VMEM((hr, cq), F32),
                            dma((4,)), dma((4,)), dma((3,)), dma((3,)), dma((2,)), dma((2,)), dma((3 * ns,)), dma((3 * ns,))]),
        compiler_params=pltpu.CompilerParams(dimension_semantics=("arbitrary", "arbitrary"),
                                             vmem_limit_bytes=VMEM_LIMIT),
    )(jnp.reshape(q, (1,)).astype(jnp.int32), d_h, xt, *scatter_srcs)


def _swap_with_sibling(grads):
    ns = len(grads)

    def body(*refs):
        copies = _swap_copies(refs[:ns], refs[ns:2 * ns], refs[2 * ns], refs[2 * ns + 1])
        for cp in copies:
            cp.start()
        for cp in copies:
            cp.wait()

    return pl.pallas_call(
        body, name="swap_with_sibling", out_shape=_swap_shapes(grads), in_specs=[ANY] * ns, out_specs=[ANY] * ns,
        scratch_shapes=[pltpu.SemaphoreType.DMA((ns,)), pltpu.SemaphoreType.DMA((ns,))],
    )(*grads)


def _token_tile(T, want):
    return math.gcd(T, want)


def kernel(x, p, w_in, pool_w, pool_scale, sgu_ln_g, sgu_ln_b, sgu_w, sgu_b, w_out, ln_g, ln_b, ple_w, ple_gate_w, ple_gate_b, loss_target, m_w_in, m_pool_w, m_pool_scale, m_sgu_ln_g, m_sgu_ln_b, m_sgu_w, m_sgu_b, m_w_out, m_ln_g, m_ln_b, m_ple_w, m_ple_gate_w, m_ple_gate_b, v_w_in, v_pool_w, v_pool_scale, v_sgu_ln_g, v_sgu_ln_b, v_sgu_w, v_sgu_b, v_w_out, v_ln_g, v_ln_b, v_ple_w, v_ple_gate_w, v_ple_gate_b):
    c = lax.axis_index("c")
    T, D = x.shape[1], x.shape[2]
    tm, tm_vpu, tm_acc = _token_tile(T, 512), _token_tile(T, 256), _token_tile(T, 2048)
    x2, p2, tgt = x[0], p[0, 0], loss_target[0]
    G, PGQ, PG = pool_w.shape[1], pool_w.shape[2], pool_w.shape[3]

    w_in_f, pool_f = _gather_weights([w_in[0], pool_w[0].reshape(G * PGQ, PG)])
    pool_f = pool_f.reshape(4, G, PGQ, PG).transpose(1, 0, 2, 3).reshape(G, 4 * PGQ, PG)
    tril = jnp.tril(jnp.ones((CHUNK, CHUNK), dtype=bool))
    sgu_wm = jnp.where(tril[None], sgu_w[0], 0.0).astype(BF16)
    sgu_bias_t = sgu_b[0].T

    kept, y, xt, w_out_f, w_gate_f, w_ple_f = _front_forward(
        x2, w_in_f, pool_f, pool_scale, sgu_ln_g, sgu_ln_b, sgu_wm, sgu_bias_t, [w_out[0], ple_gate_w[0], ple_w[0]],
        tm_vpu)
    w_out_f = w_out_f.reshape(-1, D)
    w_gate_f = w_gate_f.reshape(-1, D)
    (dx_part, d_y, d_w_out, d_w_gate, d_w_ple, d_ln_g, d_ln_b, d_gate_b, ssq) = _tail(
        y, x2, p2, tgt, w_out_f, w_gate_f, w_ple_f, ln_g, ln_b, ple_gate_b, tm)
    d_h, d_x, d_pool_w, d_pool_scale, d_sgu_g, d_sgu_b, d_sgu_w, d_sgu_bias_t = _front_backward(
        kept, d_y, dx_part, w_in_f, pool_f, pool_scale, sgu_ln_g, sgu_ln_b, sgu_wm, sgu_bias_t, tm_vpu)
    grads = [d_w_out.reshape(4, -1, D), d_w_gate.reshape(4, -1, D), d_w_ple,
             d_pool_w.reshape(G, 4, PGQ, PG).transpose(1, 0, 2, 3).reshape(4, G * PGQ, PG)]
    grads = [g.reshape(4, 2, g.shape[1] // 2, g.shape[2]) for g in grads]
    d_sgu_w = jnp.where(tril[None], d_sgu_w, 0.0)
    small_names = ["pool_scale", "sgu_ln_g", "sgu_ln_b", "ln_g", "ln_b", "ple_gate_b", "sgu_b", "sgu_w"]
    small_grads = [d_pool_scale, d_sgu_g, d_sgu_b, d_ln_g, d_ln_b, d_gate_b, d_sgu_bias_t.T, d_sgu_w]

    def pack(arrays, extra):
        rows = [a.reshape(-1) for a in arrays[:6]]
        rows.append(jnp.pad(arrays[6].reshape(-1), (0, D - arrays[6].size)))
        rows.append(jnp.pad(jnp.reshape(extra, (1,)), (0, D - 1)))
        return jnp.concatenate([r.reshape(-1, D) for r in rows] + [arrays[7].reshape(-1, D)], axis=0)

    def unpack(packed, like):
        out = [packed[k].reshape(like[k].shape) for k in range(6)]
        out.append(packed[6, :like[6].size].reshape(like[6].shape))
        out.append(packed[8:].reshape(like[7].shape))
        return out

    small = pack(small_grads, (0.5 / D) * jnp.sum(ssq))
    q = 2 * lax.axis_index("x") + lax.axis_index("y")
    *landed, small_landed = _swap_with_sibling(grads + [small])
    *parts, small_chip = _add_own_halves(grads, landed, small, small_landed, c)
    d_w_in, *slots, small_slots = _weight_backward(d_h, xt, q, parts + [small_chip], tm_acc)
    *halves, small_total = _sum_fours(parts, slots, small_chip, small_slots, q)
    sibling_halves = _join_halves_with_sibling(halves)
    loss = small_total[7, 0]

    big_names = ["w_out", "ple_gate_w", "ple_w", "pool_w"]
    given = dict(w_in=(w_in, m_w_in, v_w_in), w_out=(w_out, m_w_out, v_w_out),
                 ple_gate_w=(ple_gate_w, m_ple_gate_w, v_ple_gate_w), ple_w=(ple_w, m_ple_w, v_ple_w),
                 pool_w=(pool_w, m_pool_w, v_pool_w), pool_scale=(pool_scale, m_pool_scale, v_pool_scale),
                 sgu_ln_g=(sgu_ln_g, m_sgu_ln_g, v_sgu_ln_g), sgu_ln_b=(sgu_ln_b, m_sgu_ln_b, v_sgu_ln_b),
                 sgu_w=(sgu_w, m_sgu_w, v_sgu_w), sgu_b=(sgu_b, m_sgu_b, v_sgu_b), ln_g=(ln_g, m_ln_g, v_ln_g),
                 ln_b=(ln_b, m_ln_b, v_ln_b), ple_gate_b=(ple_gate_b, m_ple_gate_b, v_ple_gate_b))
    grad, delta, new_m, new_v = {}, {}, {}, {}
    grad["w_in"], delta["w_in"], new_m["w_in"], new_v["w_in"] = (
        t[None] for t in _adamw(w_in[0], d_w_in, m_w_in[0], v_w_in[0], "adamw_w_in"))
    flat = [(2 * g.shape[0], g.shape[1]) for g in halves]
    small_w, small_m, small_v = (pack([given[n][k] for n in small_names], 0.0) for k in range(3))
    *big_out, small_out = _adamw_joined(
        [given[n][0].reshape(f) for n, f in zip(big_names, flat)], halves, sibling_halves,
        [given[n][1].reshape(f) for n, f in zip(big_names, flat)],
        [given[n][2].reshape(f) for n, f in zip(big_names, flat)], (small_w, small_total, small_m, small_v), c)
    for name, outs in zip(big_names, big_out):
        grad[name], delta[name], new_m[name], new_v[name] = (t.reshape(given[name][0].shape) for t in outs)
    like = [given[n][0] for n in small_names]
    for k, name in enumerate(small_names):
        grad[name], delta[name], new_m[name], new_v[name] = (unpack(t, like)[k] for t in small_out)

    order = ["w_in", "pool_w", "pool_scale", "sgu_ln_g", "sgu_ln_b", "sgu_w", "sgu_b", "w_out", "ln_g", "ln_b",
             "ple_w", "ple_gate_w", "ple_gate_b"]
    return (loss, d_x[None], *[grad[n] for n in order], *[delta[n] for n in order],
            *[new_m[n] for n in order], *[new_v[n] for n in order])
```

```python
import functools
import math

import jax
import jax.numpy as jnp
from jax import lax
from jax.experimental import pallas as pl
from jax.experimental.pallas import tpu as pltpu

F32, BF16 = jnp.float32, jnp.bfloat16
MESH = pl.DeviceIdType.MESH
ANY = pl.BlockSpec(memory_space=pl.ANY)

POOL_WINDOWS = (2, 4, 8, 16)
HALO = 16
CHUNK = 128
MXU_COLS = 256
LN_EPS = 1e-5
DEEPNORM_ALPHA = 2.0 ** 0.25
ADAM_LR, ADAM_B1, ADAM_B2, ADAM_EPS, ADAM_WD, ADAM_STEP = 1e-3, 0.9, 0.999, 1e-8, 0.01, 10
VMEM_LIMIT = 56 * 1024 * 1024
GELU_K = math.sqrt(2.0 / math.pi)
GELU_C = 0.044715
SAVED = {"pooled": 0, "gelu_u": 1, "dgelu_u": 2, "vhat": 3, "rstd_dgelu_v": 4, "silu": 5, "dsilu": 7}
SAVED_WIDTH = 9


def _mm(a, b):
    return jnp.dot(a, b, preferred_element_type=F32)


def _mm_nt(a, b):
    return lax.dot_general(a, b, (((1,), (1,)), ((), ())), preferred_element_type=F32)


def _mm_tn(a, b):
    return lax.dot_general(a, b, (((0,), (0,)), ((), ())), preferred_element_type=F32)


def _gelu_and_grad(x):
    x2 = x * x
    t = jnp.tanh(x * (GELU_K + (GELU_K * GELU_C) * x2))
    hx = 0.5 * x
    g = hx + hx * t
    dg = (0.5 + 0.5 * t) + (hx - hx * t * t) * (GELU_K + (3.0 * GELU_K * GELU_C) * x2)
    return g, dg


def _silu_and_grad(z):
    sig = jax.nn.sigmoid(z)
    zs = z * sig
    return zs, sig + zs * (1.0 - sig)


def _norm_rows(x):
    mu = jnp.mean(x, axis=-1, keepdims=True)
    xc = x - mu
    var = jnp.mean(xc * xc, axis=-1, keepdims=True)
    rstd = lax.rsqrt(var + LN_EPS)
    return xc * rstd, rstd


def _norm_rows_bwd(dxhat, xhat, rstd):
    m1 = jnp.mean(dxhat, axis=-1, keepdims=True)
    m2 = jnp.mean(dxhat * xhat, axis=-1, keepdims=True)
    return rstd * (dxhat - m1 - xhat * m2)


def _inv_count(row0, rows, w):
    t = row0 + lax.broadcasted_iota(jnp.int32, (rows, 1), 0)
    return 1.0 / jnp.minimum(t + 1, w).astype(F32)


def _causal_window_sum(ext, w):
    s, sh = ext, 1
    while sh < w:
        s = s + pltpu.roll(s, sh, axis=0)
        sh *= 2
    return s[HALO:, :]


def _anticausal_window_sum(ext, w):
    n, s, sh = ext.shape[0], ext, 1
    while sh < w:
        s = s + pltpu.roll(s, n - sh, axis=0)
        sh *= 2
    return s[: n - HALO, :]


def _place():
    return lax.axis_index("x"), lax.axis_index("y"), lax.axis_index("c")


def _gather_weights(shards):
    n = len(shards)
    piece = [s.shape[0] // 4 for s in shards]

    def body(*refs):
        wide, dsts, srcs = refs[:n], refs[n:2 * n], refs[2 * n:3 * n]
        send_sems, recv_sems, local_sems = refs[3 * n:]
        for k in range(n):
            for r0 in range(0, 4 * piece[k], CHUNK):
                srcs[k][r0:r0 + CHUNK, :] = wide[k][r0:r0 + CHUNK, :].astype(BF16)
        x, y, c = _place()
        me, sibling = (x, y, c), (x, y, 1 - c)
        across_x, across_y = (1 - x, y, c), (x, 1 - y, c)
        q, qx, qy, qf = 2 * x + y, 2 * (1 - x) + y, 2 * x + (1 - y), 2 * (1 - x) + (1 - y)

        def rows(ref, cc, p, k):
            return ref.at[pl.ds((2 * cc + p) * piece[k], piece[k])]

        def copy(k, sem, qq, cc, p, to, own=False):
            landing = rows(dsts[k].at[qq], cc, p, k)
            return pltpu.make_async_remote_copy(
                src_ref=rows(srcs[k], cc, p, k) if own else landing, dst_ref=landing,
                send_sem=send_sems.at[12 * k + sem], recv_sem=recv_sems.at[12 * k + sem],
                device_id=to, device_id_type=MESH)

        started = []

        def go(cp):
            cp.start()
            started.append(cp)

        mine = [pltpu.make_async_copy(srcs[k], dsts[k].at[q], local_sems.at[k]) for k in range(n)]
        for cp in mine:
            cp.start()
        for k in range(n):
            for p in range(2):
                go(copy(k, p, q, c, p, across_x, own=True))
                go(copy(k, 2 + p, q, c, p, across_y, own=True))
        for k in range(n):
            copy(k, 0, qx, c, 0, me).wait_recv()
            go(copy(k, 4, qx, c, 0, across_y))
            go(copy(k, 6, qx, c, 0, sibling))
            copy(k, 3, qy, c, 1, me).wait_recv()
            go(copy(k, 5, qy, c, 1, across_x))
            go(copy(k, 9, qy, c, 1, sibling))
        for k in range(n):
            copy(k, 1, qx, c, 1, me).wait_recv()
            go(copy(k, 7, qx, c, 1, sibling))
            copy(k, 2, qy, c, 0, me).wait_recv()
            go(copy(k, 8, qy, c, 0, sibling))
        for k in range(n):
            copy(k, 4, qf, c, 0, me).wait_recv()
            go(copy(k, 10, qf, c, 0, sibling))
            copy(k, 5, qf, c, 1, me).wait_recv()
            go(copy(k, 11, qf, c, 1, sibling))
        for k in range(n):
            for sem, qq, p in ((6, qx, 0), (7, qx, 1), (8, qy, 0), (9, qy, 1), (10, qf, 0), (11, qf, 1)):
                copy(k, sem, qq, 1 - c, p, me).wait_recv()
        for cp in started:
            cp.wait_send()
        for cp in mine:
            cp.wait()

    return pl.pallas_call(
        body, name="gather_weights",
        out_shape=[jax.ShapeDtypeStruct((4,) + s.shape, BF16) for s in shards],
        in_specs=[pl.BlockSpec(memory_space=pltpu.VMEM)] * n, out_specs=[ANY] * n,
        scratch_shapes=[pltpu.VMEM(s.shape, BF16) for s in shards]
        + [pltpu.SemaphoreType.DMA((12 * n,)), pltpu.SemaphoreType.DMA((12 * n,)), pltpu.SemaphoreType.DMA((n,))],
        compiler_params=pltpu.CompilerParams(vmem_limit_bytes=VMEM_LIMIT),
    )(*shards)


def _direct_gather_copies(srcs, dsts, send_sems, recv_sems):
    x, y, c = _place()
    q = 2 * x + y
    sends, recvs = [], []
    for k, (src, dst) in enumerate(zip(srcs, dsts)):
        half = src.shape[0] // 2
        for j, chip in enumerate([(1 - x, y), (x, 1 - y), (1 - x, 1 - y)]):
            for core in range(2):
                sends.append(pltpu.make_async_remote_copy(
                    src_ref=src.at[pl.ds(c * half, half)], dst_ref=dst.at[q, pl.ds(c * half, half)],
                    send_sem=send_sems.at[6 * k + 2 * j + core], recv_sem=recv_sems.at[6 * k + 2 * j + c],
                    device_id=(*chip, core), device_id_type=MESH))
                landed = dst.at[2 * chip[0] + chip[1], pl.ds(core * half, half)]
                recvs.append(pltpu.make_async_remote_copy(
                    src_ref=landed, dst_ref=landed, send_sem=send_sems.at[6 * k + 2 * j + core],
                    recv_sem=recv_sems.at[6 * k + 2 * j + core], device_id=(x, y, c), device_id_type=MESH))
    return sends, recvs


def _swap_copies(srcs, dsts, send_sems, recv_sems):
    x, y, c = _place()
    return [pltpu.make_async_remote_copy(
        src_ref=src.at[:, 1 - c] if len(src.shape) == 4 else src, dst_ref=dst,
        send_sem=send_sems.at[k], recv_sem=recv_sems.at[k], device_id=(x, y, 1 - c), device_id_type=MESH)
        for k, (src, dst) in enumerate(zip(srcs, dsts))]


def _swap_shapes(grads):
    return [jax.ShapeDtypeStruct((4,) + g.shape[2:] if g.ndim == 4 else g.shape, g.dtype) for g in grads]


def _scatter_copies(srcs, dsts, send_sems, recv_sems):
    x, y, c = _place()
    copies = []
    for j, chip in enumerate([(1 - x, y), (x, 1 - y), (1 - x, 1 - y)]):
        for k, (src, dst) in enumerate(zip(srcs, dsts)):
            copies.append(pltpu.make_async_remote_copy(
                src_ref=src.at[2 * chip[0] + chip[1]] if len(src.shape) == 3 else src, dst_ref=dst.at[j],
                send_sem=send_sems.at[3 * k + j], recv_sem=recv_sems.at[3 * k + j],
                device_id=(*chip, c), device_id_type=MESH))
    return copies


def _scatter_shapes(parts):
    return [jax.ShapeDtypeStruct((3,) + (p.shape[1:] if p.ndim == 3 else p.shape), p.dtype) for p in parts]


def _join_halves_with_sibling(halves):
    n = len(halves)

    def body(*refs):
        srcs, dsts = refs[:n], refs[n:2 * n]
        send_sems, recv_sems = refs[2 * n:]
        x, y, c = _place()
        copies = [pltpu.make_async_remote_copy(
            src_ref=srcs[k], dst_ref=dsts[k], send_sem=send_sems.at[k], recv_sem=recv_sems.at[k],
            device_id=(x, y, 1 - c), device_id_type=MESH) for k in range(n)]
        for cp in copies:
            cp.start()
        for cp in copies:
            cp.wait()

    return pl.pallas_call(
        body, name="join_halves",
        out_shape=[jax.ShapeDtypeStruct(h.shape, h.dtype) for h in halves],
        in_specs=[ANY] * n, out_specs=[ANY] * n,
        scratch_shapes=[pltpu.SemaphoreType.DMA((n,)), pltpu.SemaphoreType.DMA((n,))],
    )(*halves)


def _row_block(rows, cols, n_arrays):
    cap = max(8, (VMEM_LIMIT // 4) // (8 * n_arrays * cols))
    rb = rows
    while rb > cap and rb % 2 == 0:
        rb //= 2
    return rb


def _scalar(value):
    return jnp.reshape(value, (1,)).astype(jnp.int32)


def _whole(a):
    return pl.BlockSpec(a.shape, lambda i, s_ref: (0,) * a.ndim)


def _add_own_halves(grads, landed, small, small_landed, c):
    n = len(grads)

    def body(c_ref, *refs):
        g, l, (s, sl), o, so = refs[:n], refs[n:2 * n], refs[2 * n:2 * n + 2], refs[2 * n + 2:3 * n + 2], refs[3 * n + 2]
        for k in range(n):
            o[k][...] = (g[k][...] + l[k][...]).astype(BF16)
        so[...] = s[...] + sl[...]

    def half(a):
        return pl.BlockSpec((4, a.shape[1] // 2, a.shape[2]), lambda i, c_ref: (0, i, 0))

    return pl.pallas_call(
        body, name="add_own_halves",
        out_shape=[jax.ShapeDtypeStruct(a.shape, BF16) for a in landed] + [jax.ShapeDtypeStruct(small.shape, F32)],
        grid_spec=pltpu.PrefetchScalarGridSpec(
            num_scalar_prefetch=1, grid=(2,),
            in_specs=[pl.BlockSpec((4, None, a.shape[2] // 2, a.shape[3]), lambda i, c_ref: (0, c_ref[0], i, 0))
                      for a in grads] + [half(a) for a in landed] + [_whole(small), _whole(small_landed)],
            out_specs=[half(a) for a in landed] + [_whole(small)]),
        compiler_params=pltpu.CompilerParams(vmem_limit_bytes=VMEM_LIMIT),
    )(_scalar(c), *grads, *landed, small, small_landed)


def _sum_fours(parts, slots, small_chip, small_slots, q):
    n = len(parts)

    def four(own, s):
        return (own[...].astype(F32) + s[0].astype(F32)) + (s[1].astype(F32) + s[2].astype(F32))

    def body(q_ref, *refs):
        p, s, (sc, ss), o, so = refs[:n], refs[n:2 * n], refs[2 * n:2 * n + 2], refs[2 * n + 2:3 * n + 2], refs[3 * n + 2]
        for k in range(n):
            o[k][...] = four(p[k], s[k])
        so[...] = four(sc, ss)

    return pl.pallas_call(
        body, name="sum_fours",
        out_shape=[jax.ShapeDtypeStruct(a.shape[1:], F32) for a in parts] + [jax.ShapeDtypeStruct(small_chip.shape, F32)],
        grid_spec=pltpu.PrefetchScalarGridSpec(
            num_scalar_prefetch=1, grid=(2,),
            in_specs=[pl.BlockSpec((None, a.shape[1] // 2, a.shape[2]), lambda i, q_ref: (q_ref[0], i, 0)) for a in parts]
            + [pl.BlockSpec((3, a.shape[1] // 2, a.shape[2]), lambda i, q_ref: (0, i, 0)) for a in slots]
            + [_whole(small_chip), _whole(small_slots)],
            out_specs=[pl.BlockSpec((a.shape[1] // 2, a.shape[2]), lambda i, q_ref: (i, 0)) for a in parts]
            + [_whole(small_chip)]),
        compiler_params=pltpu.CompilerParams(vmem_limit_bytes=VMEM_LIMIT),
    )(_scalar(q), *parts, *slots, small_chip, small_slots)


def _adamw_math(w, g, m, v):
    nm = ADAM_B1 * m + (1.0 - ADAM_B1) * g
    nv = ADAM_B2 * v + (1.0 - ADAM_B2) * (g * g)
    m_hat = nm / (1.0 - ADAM_B1 ** ADAM_STEP)
    v_hat = nv / (1.0 - ADAM_B2 ** ADAM_STEP)
    return -ADAM_LR * (m_hat / (jnp.sqrt(v_hat) + ADAM_EPS) + ADAM_WD * w), nm, nv


def _adamw(w, g, m, v, name):
    rows, cols = w.shape
    rb = _row_block(rows, cols, 8)

    def body(w_ref, g_ref, m_ref, v_ref, go_ref, d_ref, nm_ref, nv_ref):
        go_ref[...] = g_ref[...]
        d_ref[...], nm_ref[...], nv_ref[...] = _adamw_math(w_ref[...], g_ref[...], m_ref[...], v_ref[...])

    spec = pl.BlockSpec((rb, cols), lambda r: (r, 0))
    out = jax.ShapeDtypeStruct(w.shape, F32)
    return pl.pallas_call(body, name=name, out_shape=[out] * 4, grid=(rows // rb,),
                          in_specs=[spec] * 4, out_specs=[spec] * 4)(w, g, m, v)


def _adamw_joined(ws, g_mine, g_sibling, ms, vs, small, c):
    n = len(ws)

    def body(c_ref, *refs):
        ins, outs = refs[:5 * n + 4], refs[5 * n + 4:]
        mine = c_ref[0] == pl.program_id(0)
        for k in range(n):
            w, gm, gs, m, v = ins[5 * k:5 * k + 5]
            g = jnp.where(mine, gm[...], gs[...])
            outs[4 * k][...] = g
            outs[4 * k + 1][...], outs[4 * k + 2][...], outs[4 * k + 3][...] = _adamw_math(w[...], g, m[...], v[...])
        w, g, m, v = ins[5 * n:]
        outs[4 * n][...] = g[...]
        outs[4 * n + 1][...], outs[4 * n + 2][...], outs[4 * n + 3][...] = _adamw_math(w[...], g[...], m[...], v[...])

    def half(a):
        return pl.BlockSpec((a.shape[0] // 2, a.shape[1]), lambda hf, c_ref: (hf, 0))

    in_specs, operands = [], []
    for k in range(n):
        in_specs += [half(ws[k]), _whole(g_mine[k]), _whole(g_sibling[k]), half(ms[k]), half(vs[k])]
        operands += [ws[k], g_mine[k], g_sibling[k], ms[k], vs[k]]
    outs = pl.pallas_call(
        body, name="adamw_joined",
        out_shape=[jax.ShapeDtypeStruct(w.shape, F32) for w in ws for _ in range(4)]
        + [jax.ShapeDtypeStruct(small[0].shape, F32)] * 4,
        grid_spec=pltpu.PrefetchScalarGridSpec(
            num_scalar_prefetch=1, grid=(2,),
            in_specs=in_specs + [_whole(a) for a in small],
            out_specs=[half(w) for w in ws for _ in range(4)] + [_whole(small[0])] * 4),
        compiler_params=pltpu.CompilerParams(vmem_limit_bytes=VMEM_LIMIT),
    )(_scalar(c), *operands, *small)
    return [outs[4 * k:4 * k + 4] for k in range(n + 1)]


def _front_forward(x, w_in, pool_w, pool_scale, sgu_g, sgu_b, sgu_wm, sgu_bias_t, later_shards, tm):
    T, D = x.shape
    nq, _, cq = w_in.shape
    G, PG = pool_w.shape[0], pool_w.shape[1]
    nt = T // tm
    bpd = D // PG
    nl = len(later_shards)

    def body(x_ref, win_any, pw_any, ps_ref, lg_ref, lb_ref, sw_ref, sb_ref, *refs):
        shards_any, (keep_ref, y_ref, xt_ref), gathered = refs[:nl], refs[nl:nl + 3], refs[nl + 3:2 * nl + 3]
        wide, narrow = refs[2 * nl + 3:3 * nl + 3], refs[3 * nl + 3:4 * nl + 3]
        win_v, pw_v, carry, sems, load_sems, own_sems, send_sems, recv_sems = refs[4 * nl + 3:]
        i = pl.program_id(0)
        own_quarter = 2 * lax.axis_index("x") + lax.axis_index("y")

        def own_copies():
            return [pltpu.make_async_copy(narrow[k], gathered[k].at[own_quarter], own_sems.at[k]) for k in range(nl)]

        @pl.when(i == 0)
        def _():
            c1 = pltpu.make_async_copy(win_any, win_v, sems.at[0])
            c2 = pltpu.make_async_copy(pw_any, pw_v, sems.at[1])
            loads = [pltpu.make_async_copy(shards_any[k], wide[k], load_sems.at[k]) for k in range(nl)]
            for cp in [c1, c2] + loads:
                cp.start()
            carry[...] = jnp.zeros_like(carry)
            for k in range(nl):
                loads[k].wait()
                for r0 in range(0, wide[k].shape[0], CHUNK):
                    narrow[k][r0:r0 + CHUNK, :] = wide[k][r0:r0 + CHUNK, :].astype(BF16)
            for cp in own_copies() + _direct_gather_copies(narrow, gathered, send_sems, recv_sems)[0]:
                cp.start()
            c1.wait()
            c2.wait()

        xb = x_ref[...].astype(BF16)
        xt_ref[...] = x_ref[...].T.astype(BF16)

        def h_block(j):
            qq, off = divmod(j * PG, cq)
            return _mm(xb, win_v[qq, :, off:off + PG])

        def keep(part, col, value):
            keep_ref[:, SAVED[part] * D + col:SAVED[part] * D + col + PG] = value.astype(BF16)

        def ahead(stage):
            if stage < G:
                return h_block(stage), h_block(3 * bpd + stage)
            if stage < G + bpd:
                hd = stage - G
                return h_block(bpd + hd), h_block(2 * bpd + hd), h_block(4 * bpd + hd)
            return None

        blocks = ahead(0)

        for g, w in enumerate(POOL_WINDOWS):
            sl = slice(g * PG, (g + 1) * PG)
            a, z = blocks
            blocks = ahead(g + 1)
            ext = jnp.concatenate([carry[:, sl], a], axis=0)
            carry[:, sl] = a[tm - HALO:, :]
            pooled = (_causal_window_sum(ext, w) * _inv_count(i * tm, tm, w) - a).astype(BF16)
            mixed = _mm(pooled, pw_v[g])
            zs, dzs = _silu_and_grad(z)
            keep("pooled", g * PG, pooled)
            keep("silu", g * PG, zs)
            keep("dsilu", g * PG, dzs)
            y_ref[:, sl] = (mixed * ps_ref[:, sl] * zs).astype(BF16)

        for hd in range(bpd):
            sl = slice(hd * PG, (hd + 1) * PG)
            u, v, z = blocks
            blocks = ahead(G + hd + 1)
            ug, dug = _gelu_and_grad(u)
            vg, dvg = _gelu_and_grad(v)
            vhat, rstd = _norm_rows(vg)
            zs, dzs = _silu_and_grad(z)
            keep("gelu_u", hd * PG, ug)
            keep("dgelu_u", hd * PG, dug)
            keep("vhat", hd * PG, vhat)
            keep("rstd_dgelu_v", hd * PG, rstd * dvg)
            keep("silu", D + hd * PG, zs)
            keep("dsilu", D + hd * PG, dzs)
            vn = (vhat * lg_ref[:, sl] + lb_ref[:, sl]).astype(BF16)
            gated = ug * zs
            for n in range(tm // CHUNK):
                rs = slice(n * CHUNK, (n + 1) * CHUNK)
                sv = _mm(sw_ref[hd], vn[rs, :]) + sb_ref[:, hd:hd + 1]
                y_ref[rs, D + hd * PG:D + (hd + 1) * PG] = (gated[rs, :] * sv).astype(BF16)

        @pl.when(i == nt - 1)
        def _():
            sends, recvs = _direct_gather_copies(narrow, gathered, send_sems, recv_sems)
            for cp in sends:
                cp.wait_send()
            for cp in recvs:
                cp.wait_recv()
            for cp in own_copies():
                cp.wait()

    vec = pl.BlockSpec((1, D), lambda i: (0, 0))
    return pl.pallas_call(
        body, name="front_forward",
        out_shape=[jax.ShapeDtypeStruct((T, SAVED_WIDTH * D), BF16), jax.ShapeDtypeStruct((T, 2 * D), BF16),
                   jax.ShapeDtypeStruct((D, T), BF16)]
        + [jax.ShapeDtypeStruct((4,) + s.shape, BF16) for s in later_shards],
        grid=(nt,),
        in_specs=[pl.BlockSpec((tm, D), lambda i: (i, 0)), ANY, ANY, vec, vec, vec,
                  pl.BlockSpec(sgu_wm.shape, lambda i: (0, 0, 0)), pl.BlockSpec(sgu_bias_t.shape, lambda i: (0, 0))]
        + [ANY] * nl,
        out_specs=[pl.BlockSpec((tm, SAVED_WIDTH * D), lambda i: (i, 0)), pl.BlockSpec((tm, 2 * D), lambda i: (i, 0)),
                   pl.BlockSpec((D, tm), lambda i: (0, i))] + [ANY] * nl,
        scratch_shapes=[pltpu.VMEM(s.shape, F32) for s in later_shards]
        + [pltpu.VMEM(s.shape, BF16) for s in later_shards]
        + [pltpu.VMEM(w_in.shape, BF16), pltpu.VMEM(pool_w.shape, BF16), pltpu.VMEM((HALO, D), F32),
           pltpu.SemaphoreType.DMA((2,)), pltpu.SemaphoreType.DMA((nl,)), pltpu.SemaphoreType.DMA((nl,)),
           pltpu.SemaphoreType.DMA((6 * nl,)), pltpu.SemaphoreType.DMA((6 * nl,))],
        compiler_params=pltpu.CompilerParams(dimension_semantics=("arbitrary",), vmem_limit_bytes=VMEM_LIMIT),
    )(x, w_in, pool_w, pool_scale, sgu_g, sgu_b, sgu_wm, sgu_bias_t, *later_shards)


def _tail(y, x, p, target, w_out, w_gate, w_ple, ln_g, ln_b, gate_b, tm):
    T, D = x.shape
    K = p.shape[1]
    nq, _, cq = w_ple.shape
    nt = T // tm

    def body(y_ref, x_ref, p_ref, t_ref, wout_any, wg_any, wp_any, lng_ref, lnb_ref, bg_ref,
             dxp_ref, dy_ref, dwout_any, dwg_any, dwp_any, dlng_ref, dlnb_ref, dbg_ref, ssq_ref,
             wout_v, wg_v, wp_v, dwout_acc, dwg_acc, dwp_acc, sems):
        i = pl.program_id(0)

        @pl.when(i == 0)
        def _():
            loads = [pltpu.make_async_copy(s, d, sems.at[k])
                     for k, (s, d) in enumerate(((wout_any, wout_v), (wg_any, wg_v), (wp_any, wp_v)))]
            for cp in loads:
                cp.start()
            for ref in (dwout_acc, dwg_acc, dwp_acc, dlng_ref, dlnb_ref, dbg_ref, ssq_ref):
                ref[...] = jnp.zeros_like(ref)
            for cp in loads:
                cp.wait()

        halves = [slice(k * tm // 2, (k + 1) * tm // 2) for k in range(2)]

        def total(parts):
            return sum(jnp.sum(part, axis=0, keepdims=True) for part in parts)

        yb = [y_ref[r, :] for r in halves]
        pb = [p_ref[r, :].astype(BF16) for r in halves]
        mix = [_mm(v, wout_v[...]) for v in yb]
        normed = [_norm_rows(DEEPNORM_ALPHA * x_ref[r, :] + m) for r, m in zip(halves, mix)]
        xhat, rstd = [n[0] for n in normed], [n[1] for n in normed]
        x1 = [v * lng_ref[...] + lnb_ref[...] for v in xhat]
        x1b = [v.astype(BF16) for v in x1]
        gate = [jax.nn.sigmoid(_mm(v, wg_v[...]) + bg_ref[...]) for v in x1b]
        e = [jnp.concatenate([_mm(v, wp_v[qq]) for qq in range(nq)], axis=1) for v in pb]
        diff = [a + g * ee - t_ref[r, :] for a, g, ee, r in zip(x1, gate, e, halves)]
        ssq_ref[...] += total([d * d for d in diff])

        dout = [d * (1.0 / D) for d in diff]
        d_e = [(do * g).astype(BF16) for do, g in zip(dout, gate)]
        dgl = [do * ee * g * (1.0 - g) for do, ee, g in zip(dout, e, gate)]
        dglb = [v.astype(BF16) for v in dgl]
        dbg_ref[...] += total(dgl)
        pb_t, d_e_t, x1b_t, dglb_t = (jnp.concatenate(v, axis=0) for v in (pb, d_e, x1b, dglb))
        for qq in range(nq):
            dwp_acc[qq] += _mm_tn(pb_t, d_e_t[:, qq * cq:(qq + 1) * cq])
        for c0 in range(0, D, MXU_COLS):
            dwg_acc[:, c0:c0 + MXU_COLS] += _mm_tn(x1b_t, dglb_t[:, c0:c0 + MXU_COLS])
        d_x1 = [do + _mm_nt(dg, wg_v[...]) for do, dg in zip(dout, dglb)]
        dlng_ref[...] += total([d * xh for d, xh in zip(d_x1, xhat)])
        dlnb_ref[...] += total(d_x1)
        d_r = [_norm_rows_bwd(d * lng_ref[...], xh, rs) for d, xh, rs in zip(d_x1, xhat, rstd)]
        drb = [v.astype(BF16) for v in d_r]
        for r, v in zip(halves, d_r):
            dxp_ref[r, :] = DEEPNORM_ALPHA * v
        for c0 in range(0, 2 * D, 2 * MXU_COLS):
            for r, v in zip(halves, drb):
                dy_ref[r, c0:c0 + 2 * MXU_COLS] = _mm_nt(v, wout_v[c0:c0 + 2 * MXU_COLS, :]).astype(BF16)

        drb_t = jnp.concatenate(drb, axis=0)
        for c0 in range(0, D, MXU_COLS):
            dwout_acc[:, c0:c0 + MXU_COLS] += _mm_tn(y_ref[...], drb_t[:, c0:c0 + MXU_COLS])

        @pl.when(i == nt - 1)
        def _():
            stores = [pltpu.make_async_copy(s, d, sems.at[k])
                      for k, (s, d) in enumerate(((dwout_acc, dwout_any), (dwg_acc, dwg_any), (dwp_acc, dwp_any)))]
            for cp in stores:
                cp.start()
            for cp in stores:
                cp.wait()

    vec = pl.BlockSpec((1, D), lambda i: (0, 0))
    vec_shape = jax.ShapeDtypeStruct((1, D), F32)

    def tile(cols):
        return pl.BlockSpec((tm, cols), lambda i: (i, 0))

    return pl.pallas_call(
        body, name="tail",
        out_shape=[jax.ShapeDtypeStruct((T, D), F32), jax.ShapeDtypeStruct((T, 2 * D), BF16),
                   jax.ShapeDtypeStruct(w_out.shape, F32), jax.ShapeDtypeStruct(w_gate.shape, F32),
                   jax.ShapeDtypeStruct(w_ple.shape, F32), vec_shape, vec_shape, vec_shape, vec_shape],
        grid=(nt,),
        in_specs=[tile(2 * D), tile(D), tile(K), tile(D), ANY, ANY, ANY, vec, vec, vec],
        out_specs=[tile(D), tile(2 * D), ANY, ANY, ANY, vec, vec, vec, vec],
        scratch_shapes=[pltpu.VMEM(w_out.shape, BF16), pltpu.VMEM(w_gate.shape, BF16), pltpu.VMEM(w_ple.shape, BF16),
                        pltpu.VMEM(w_out.shape, F32), pltpu.VMEM(w_gate.shape, F32), pltpu.VMEM(w_ple.shape, F32),
                        pltpu.SemaphoreType.DMA((3,))],
        compiler_params=pltpu.CompilerParams(dimension_semantics=("arbitrary",), vmem_limit_bytes=VMEM_LIMIT),
    )(y, x, p, target, w_out, w_gate, w_ple, ln_g, ln_b, gate_b)


def _front_backward(kept, d_y, dx_part, w_in, pool_w, pool_scale, sgu_g, sgu_b, sgu_wm, sgu_bias_t, tm):
    T = kept.shape[0]
    D = kept.shape[1] // SAVED_WIDTH
    nq, _, cq = w_in.shape
    G, PG = pool_w.shape[0], pool_w.shape[1]
    nt = T // tm

    def tile_of(i):
        return nt - 1 - jnp.minimum(i, nt - 1)

    def body(kept_ref, dy_ref, dxp_ref, win_any, pw_ref, ps_ref, lg_ref, lb_ref, sw_ref, sb_ref,
             dh_ref, dx_ref, dpw_ref, dps_ref, dlg_ref, dlb_ref, dsw_ref, dsb_ref, win_v, dh_keep, carry, sems):
        i = pl.program_id(0)
        ti = tile_of(i)
        live = (i < nt).astype(F32)

        def saved(part, col, rows=slice(None)):
            return kept_ref[rows, SAVED[part] * D + col:SAVED[part] * D + col + PG]

        @pl.when(i == 0)
        def _():
            cp = pltpu.make_async_copy(win_any, win_v, sems.at[0])
            cp.start()
            carry[...] = jnp.zeros_like(carry)
            dh_keep[...] = jnp.zeros_like(dh_keep)
            for ref in (dpw_ref, dps_ref, dlg_ref, dlb_ref, dsw_ref, dsb_ref):
                ref[...] = jnp.zeros_like(ref)
            cp.wait()

        def dx_columns(r0):
            dx = dxp_ref[:, r0:r0 + MXU_COLS]
            for qq in range(nq):
                dx = dx + _mm_nt(dh_keep[(i + 1) % 2, :, qq * cq:(qq + 1) * cq], win_v[qq, r0:r0 + MXU_COLS, :])
            dx_ref[:, r0:r0 + MXU_COLS] = dx

        dx_chunks = list(range(0, D, MXU_COLS))
        stages = G + D // PG

        for g, w in enumerate(POOL_WINDOWS):
            for r0 in dx_chunks[g * len(dx_chunks) // stages:(g + 1) * len(dx_chunks) // stages]:
                dx_columns(r0)
            sl = slice(g * PG, (g + 1) * PG)
            pooled = saved("pooled", g * PG)
            mixed = _mm(pooled, pw_ref[g])
            dy = dy_ref[:, sl].astype(F32)
            d_ypool = dy * saved("silu", g * PG).astype(F32)
            dh_ref[:, 3 * D + g * PG:3 * D + (g + 1) * PG] = (
                dy * (mixed * ps_ref[:, sl]) * saved("dsilu", g * PG).astype(F32)).astype(BF16)
            dps_ref[:, sl] += live * jnp.sum(d_ypool * mixed, axis=0, keepdims=True)
            d_mixed = (d_ypool * ps_ref[:, sl]).astype(BF16)
            dpw_ref[g] += live * _mm_tn(pooled, d_mixed)
            d_pooled = _mm_nt(d_mixed, pw_ref[g])
            scaled = d_pooled * _inv_count(ti * tm, tm, w)
            after = jnp.concatenate([scaled, carry[:, sl]], axis=0)
            carry[:, sl] = jnp.where(i < nt - 1, scaled[:HALO, :], carry[:, sl])
            dh_ref[:, sl] = (_anticausal_window_sum(after, w) - d_pooled).astype(BF16)

        for hd in range(D // PG):
            for r0 in dx_chunks[(G + hd) * len(dx_chunks) // stages:(G + hd + 1) * len(dx_chunks) // stages]:
                dx_columns(r0)
            sl = slice(hd * PG, (hd + 1) * PG)
            vhat = saved("vhat", hd * PG).astype(F32)
            vn = (vhat * lg_ref[:, sl] + lb_ref[:, sl]).astype(BF16)
            d_vn_chunks = []
            for n in range(tm // CHUNK):
                rs = slice(n * CHUNK, (n + 1) * CHUNK)
                sv = _mm(sw_ref[hd], vn[rs, :]) + sb_ref[:, hd:hd + 1]
                ug = saved("gelu_u", hd * PG, rs).astype(F32)
                dy = dy_ref[rs, D + hd * PG:D + (hd + 1) * PG].astype(F32)
                d_ysgu = dy * saved("silu", D + hd * PG, rs).astype(F32)
                dh_ref[rs, 4 * D + hd * PG:4 * D + (hd + 1) * PG] = (
                    dy * (ug * sv) * saved("dsilu", D + hd * PG, rs).astype(F32)).astype(BF16)
                dh_ref[rs, D + hd * PG:D + (hd + 1) * PG] = (
                    d_ysgu * sv * saved("dgelu_u", hd * PG, rs).astype(F32)).astype(BF16)
                d_sv = d_ysgu * ug
                dsb_ref[:, hd:hd + 1] += live * jnp.sum(d_sv, axis=1, keepdims=True)
                d_svb = d_sv.astype(BF16)
                dsw_ref[hd] += live * _mm_nt(d_svb, vn[rs, :])
                d_vn_chunks.append(_mm_tn(sw_ref[hd], d_svb))
            d_vn = jnp.concatenate(d_vn_chunks, axis=0)
            dlg_ref[:, sl] += live * jnp.sum(d_vn * vhat, axis=0, keepdims=True)
            dlb_ref[:, sl] += live * jnp.sum(d_vn, axis=0, keepdims=True)
            d_vg = _norm_rows_bwd(d_vn * lg_ref[:, sl], vhat, saved("rstd_dgelu_v", hd * PG).astype(F32))
            dh_ref[:, 2 * D + hd * PG:2 * D + (hd + 1) * PG] = d_vg.astype(BF16)

        dh_keep[i % 2] = dh_ref[...]

    vec = pl.BlockSpec((1, D), lambda i: (0, 0))
    vec_shape = jax.ShapeDtypeStruct((1, D), F32)

    def whole(shape):
        return pl.BlockSpec(shape, lambda i: (0,) * len(shape))

    return pl.pallas_call(
        body, name="front_backward",
        out_shape=[jax.ShapeDtypeStruct((T, 5 * D), BF16), jax.ShapeDtypeStruct((T, D), F32),
                   jax.ShapeDtypeStruct(pool_w.shape, F32), vec_shape, vec_shape,
                   vec_shape, jax.ShapeDtypeStruct(sgu_wm.shape, F32), jax.ShapeDtypeStruct(sgu_bias_t.shape, F32)],
        grid=(nt + 1,),
        in_specs=[pl.BlockSpec((tm, SAVED_WIDTH * D), lambda i: (tile_of(i), 0)),
                  pl.BlockSpec((tm, 2 * D), lambda i: (tile_of(i), 0)),
                  pl.BlockSpec((tm, D), lambda i: (jnp.minimum(nt - i, nt - 1), 0)), ANY,
                  whole(pool_w.shape), vec, vec, vec, whole(sgu_wm.shape), whole(sgu_bias_t.shape)],
        out_specs=[pl.BlockSpec((tm, 5 * D), lambda i: (tile_of(i), 0)),
                   pl.BlockSpec((tm, D), lambda i: (jnp.minimum(nt - i, nt - 1), 0)),
                   whole(pool_w.shape), vec, vec, vec, whole(sgu_wm.shape), whole(sgu_bias_t.shape)],
        scratch_shapes=[pltpu.VMEM(w_in.shape, BF16), pltpu.VMEM((2, tm, 5 * D), BF16), pltpu.VMEM((HALO, D), F32),
                        pltpu.SemaphoreType.DMA((1,))],
        compiler_params=pltpu.CompilerParams(dimension_semantics=("arbitrary",), vmem_limit_bytes=VMEM_LIMIT),
    )(kept, d_y, dx_part, w_in, pool_w, pool_scale, sgu_g, sgu_b, sgu_wm, sgu_bias_t)


def _weight_backward(d_h, xt, q, scatter_srcs, tm):
    D, T = xt.shape
    cq = d_h.shape[1] // 4
    hr = D // 2
    nt = T // tm
    ns = len(scatter_srcs)

    def body(q_ref, dh_ref, xt_ref, *refs):
        srcs, out_any, dsts = refs[:ns], refs[ns], refs[ns + 1:2 * ns + 1]
        (acc, land_a, send_b, land_b, mine_f, theirs_f,
         a_send, a_recv, b_send, b_recv, j_sems, o_sems, s_send, s_recv) = refs[2 * ns + 1:]
        s, t = pl.program_id(0), pl.program_id(1)
        x_, y_, c = _place()
        sibling = (x_, y_, 1 - c)
        own_rows = pl.ds(pl.multiple_of(c * hr, hr), hr)
        other_rows = pl.ds(pl.multiple_of((1 - c) * hr, hr), hr)

        @pl.when((s == 0) & (t == 0))
        def _():
            for cp in _scatter_copies(srcs, dsts, s_send, s_recv):
                cp.start()

        @pl.when(t == 0)
        def _():
            acc[s % 2] = jnp.zeros((D, cq), F32)

        for c0 in range(0, cq, MXU_COLS):
            acc[s % 2, :, c0:c0 + MXU_COLS] += _mm(xt_ref[...], dh_ref[:, c0:c0 + MXU_COLS])

        def swap(phase):
            return pltpu.make_async_remote_copy(
                src_ref=acc.at[phase % 2, other_rows], dst_ref=land_a.at[phase % 2], send_sem=a_send.at[phase],
                recv_sem=a_recv.at[phase], device_id=sibling, device_id_type=MESH)

        def pair_sum(phase):
            swap(phase).wait()
            return acc[phase % 2, own_rows, :] + land_a[phase % 2]

        def to_owner(slot):
            flip_x, flip_y = (slot + 1) >> 1, (slot + 1) & 1
            owner = (1 - x_ if flip_x else x_, 1 - y_ if flip_y else y_, c)
            return pltpu.make_async_remote_copy(
                src_ref=send_b.at[slot], dst_ref=land_b.at[slot], send_sem=b_send.at[slot],
                recv_sem=b_recv.at[slot], device_id=owner, device_id_type=MESH)

        for slot in range(3):
            @pl.when((s == slot) & (t == nt - 1))
            def _(slot=slot):
                swap(slot).start()

            @pl.when((s == slot + 1) & (t == 0))
            def _(slot=slot):
                send_b[slot] = pair_sum(slot).astype(BF16)
                to_owner(slot).start()

        @pl.when((s == 3) & (t == nt - 1))
        def _():
            swap(3).start()
            own = pair_sum(3)
            for slot in range(3):
                to_owner(slot).wait_recv()
            mine_f[...] = (own + land_b[0].astype(F32)) + (land_b[1].astype(F32) + land_b[2].astype(F32))
            join = pltpu.make_async_remote_copy(
                src_ref=mine_f, dst_ref=theirs_f, send_sem=j_sems.at[0], recv_sem=j_sems.at[1],
                device_id=sibling, device_id_type=MESH)
            join.start()
            out_mine = pltpu.make_async_copy(mine_f, out_any.at[own_rows], o_sems.at[0])
            out_mine.start()
            join.wait()
            out_theirs = pltpu.make_async_copy(theirs_f, out_any.at[other_rows], o_sems.at[1])
            out_theirs.start()
            for slot in range(3):
                to_owner(slot).wait_send()
            for cp in _scatter_copies(srcs, dsts, s_send, s_recv):
                cp.wait()
            out_mine.wait()
            out_theirs.wait()

    def quarter(s, t, q_ref):
        return (t, jnp.where(s == 3, q_ref[0], q_ref[0] ^ (s + 1)))

    dma = pltpu.SemaphoreType.DMA
    return pl.pallas_call(
        body, name="weight_backward",
        out_shape=[jax.ShapeDtypeStruct((D, cq), F32)] + _scatter_shapes(scatter_srcs),
        grid_spec=pltpu.PrefetchScalarGridSpec(
            num_scalar_prefetch=1, grid=(4, nt),
            in_specs=[pl.BlockSpec((tm, cq), quarter), pl.BlockSpec((D, tm), lambda s, t, q_ref: (0, t))] + [ANY] * ns,
            out_specs=[ANY] * (ns + 1),
            scratch_shapes=[pltpu.VMEM((2, D, cq), F32), pltpu.VMEM((2, hr, cq), F32), pltpu.VMEM((3, hr, cq), BF16),
                            pltpu.VMEM((3, hr, cq), BF16), pltpu.VMEM((hr, cq), F32), pltpu.VMEM((hr, cq), F32),
                            dma((4,)), dma((4,)), dma((3,)), dma((3,)), dma((2,)), dma((2,)), dma((3 * ns,)), dma((3 * ns,))]),
        compiler_params=pltpu.CompilerParams(dimension_semantics=("arbitrary", "arbitrary"),
                                             vmem_limit_bytes=VMEM_LIMIT),
    )(jnp.reshape(q, (1,)).astype(jnp.int32), d_h, xt, *scatter_srcs)


def _swap_with_sibling(grads):
    ns = len(grads)

    def body(*refs):
        copies = _swap_copies(refs[:ns], refs[ns:2 * ns], refs[2 * ns], refs[2 * ns + 1])
        for cp in copies:
            cp.start()
        for cp in copies:
            cp.wait()

    return pl.pallas_call(
        body, name="swap_with_sibling", out_shape=_swap_shapes(grads), in_specs=[ANY] * ns, out_specs=[ANY] * ns,
        scratch_shapes=[pltpu.SemaphoreType.DMA((ns,)), pltpu.SemaphoreType.DMA((ns,))],
    )(*grads)


def _token_tile(T, want):
    return math.gcd(T, want)


def kernel(x, p, w_in, pool_w, pool_scale, sgu_ln_g, sgu_ln_b, sgu_w, sgu_b, w_out, ln_g, ln_b, ple_w, ple_gate_w, ple_gate_b, loss_target, m_w_in, m_pool_w, m_pool_scale, m_sgu_ln_g, m_sgu_ln_b, m_sgu_w, m_sgu_b, m_w_out, m_ln_g, m_ln_b, m_ple_w, m_ple_gate_w, m_ple_gate_b, v_w_in, v_pool_w, v_pool_scale, v_sgu_ln_g, v_sgu_ln_b, v_sgu_w, v_sgu_b, v_w_out, v_ln_g, v_ln_b, v_ple_w, v_ple_gate_w, v_ple_gate_b):
    c = lax.axis_index("c")
    T, D = x.shape[1], x.shape[2]
    tm, tm_vpu, tm_acc = _token_tile(T, 512), _token_tile(T, 256), _token_tile(T, 2048)
    x2, p2, tgt = x[0], p[0, 0], loss_target[0]
    G, PGQ, PG = pool_w.shape[1], pool_w.shape[2], pool_w.shape[3]

    w_in_f, pool_f = _gather_weights([w_in[0], pool_w[0].reshape(G * PGQ, PG)])
    pool_f = pool_f.reshape(4, G, PGQ, PG).transpose(1, 0, 2, 3).reshape(G, 4 * PGQ, PG)
    tril = jnp.tril(jnp.ones((CHUNK, CHUNK), dtype=bool))
    sgu_wm = jnp.where(tril[None], sgu_w[0], 0.0).astype(BF16)
    sgu_bias_t = sgu_b[0].T

    kept, y, xt, w_out_f, w_gate_f, w_ple_f = _front_forward(
        x2, w_in_f, pool_f, pool_scale, sgu_ln_g, sgu_ln_b, sgu_wm, sgu_bias_t, [w_out[0], ple_gate_w[0], ple_w[0]],
        tm_vpu)
    w_out_f = w_out_f.reshape(-1, D)
    w_gate_f = w_gate_f.reshape(-1, D)
    (dx_part, d_y, d_w_out, d_w_gate, d_w_ple, d_ln_g, d_ln_b, d_gate_b, ssq) = _tail(
        y, x2, p2, tgt, w_out_f, w_gate_f, w_ple_f, ln_g, ln_b, ple_gate_b, tm)
    d_h, d_x, d_pool_w, d_pool_scale, d_sgu_g, d_sgu_b, d_sgu_w, d_sgu_bias_t = _front_backward(
        kept, d_y, dx_part, w_in_f, pool_f, pool_scale, sgu_ln_g, sgu_ln_b, sgu_wm, sgu_bias_t, tm_vpu)
    grads = [d_w_out.reshape(4, -1, D), d_w_gate.reshape(4, -1, D), d_w_ple,
             d_pool_w.reshape(G, 4, PGQ, PG).transpose(1, 0, 2, 3).reshape(4, G * PGQ, PG)]
    grads = [g.reshape(4, 2, g.shape[1] // 2, g.shape[2]) for g in grads]
    d_sgu_w = jnp.where(tril[None], d_sgu_w, 0.0)
    small_names = ["pool_scale", "sgu_ln_g", "sgu_ln_b", "ln_g", "ln_b", "ple_gate_b", "sgu_b", "sgu_w"]
    small_grads = [d_pool_scale, d_sgu_g, d_sgu_b, d_ln_g, d_ln_b, d_gate_b, d_sgu_bias_t.T, d_sgu_w]

    def pack(arrays, extra):
        rows = [a.reshape(-1) for a in arrays[:6]]
        rows.append(jnp.pad(arrays[6].reshape(-1), (0, D - arrays[6].size)))
        rows.append(jnp.pad(jnp.reshape(extra, (1,)), (0, D - 1)))
        return jnp.concatenate([r.reshape(-1, D) for r in rows] + [arrays[7].reshape(-1, D)], axis=0)

    def unpack(packed, like):
        out = [packed[k].reshape(like[k].shape) for k in range(6)]
        out.append(packed[6, :like[6].size].reshape(like[6].shape))
        out.append(packed[8:].reshape(like[7].shape))
        return out

    small = pack(small_grads, (0.5 / D) * jnp.sum(ssq))
    q = 2 * lax.axis_index("x") + lax.axis_index("y")
    *landed, small_landed = _swap_with_sibling(grads + [small])
    *parts, small_chip = _add_own_halves(grads, landed, small, small_landed, c)
    d_w_in, *slots, small_slots = _weight_backward(d_h, xt, q, parts + [small_chip], tm_acc)
    *halves, small_total = _sum_fours(parts, slots, small_chip, small_slots, q)
    sibling_halves = _join_halves_with_sibling(halves)
    loss = small_total[7, 0]

    big_names = ["w_out", "ple_gate_w", "ple_w", "pool_w"]
    given = dict(w_in=(w_in, m_w_in, v_w_in), w_out=(w_out, m_w_out, v_w_out),
                 ple_gate_w=(ple_gate_w, m_ple_gate_w, v_ple_gate_w), ple_w=(ple_w, m_ple_w, v_ple_w),
                 pool_w=(pool_w, m_pool_w, v_pool_w), pool_scale=(pool_scale, m_pool_scale, v_pool_scale),
                 sgu_ln_g=(sgu_ln_g, m_sgu_ln_g, v_sgu_ln_g), sgu_ln_b=(sgu_ln_b, m_sgu_ln_b, v_sgu_ln_b),
                 sgu_w=(sgu_w, m_sgu_w, v_sgu_w), sgu_b=(sgu_b, m_sgu_b, v_sgu_b), ln_g=(ln_g, m_ln_g, v_ln_g),
                 ln_b=(ln_b, m_ln_b, v_ln_b), ple_gate_b=(ple_gate_b, m_ple_gate_b, v_ple_gate_b))
    grad, delta, new_m, new_v = {}, {}, {}, {}
    grad["w_in"], delta["w_in"], new_m["w_in"], new_v["w_in"] = (
        t[None] for t in _adamw(w_in[0], d_w_in, m_w_in[0], v_w_in[0], "adamw_w_in"))
    flat = [(2 * g.shape[0], g.shape[1]) for g in halves]
    small_w, small_m, small_v = (pack([given[n][k] for n in small_names], 0.0) for k in range(3))
    *big_out, small_out = _adamw_joined(
        [given[n][0].reshape(f) for n, f in zip(big_names, flat)], halves, sibling_halves,
        [given[n][1].reshape(f) for n, f in zip(big_names, flat)],
        [given[n][2].reshape(f) for n, f in zip(big_names, flat)], (small_w, small_total, small_m, small_v), c)
    for name, outs in zip(big_names, big_out):
        grad[name], delta[name], new_m[name], new_v[name] = (t.reshape(given[name][0].shape) for t in outs)
    like = [given[n][0] for n in small_names]
    for k, name in enumerate(small_names):
        grad[name], delta[name], new_m[name], new_v[name] = (unpack(t, like)[k] for t in small_out)

    order = ["w_in", "pool_w", "pool_scale", "sgu_ln_g", "sgu_ln_b", "sgu_w", "sgu_b", "w_out", "ln_g", "ln_b",
             "ple_w", "ple_gate_w", "ple_gate_b"]
    return (loss, d_x[None], *[grad[n] for n in order], *[delta[n] for n in order],
            *[new_m[n] for n in order], *[new_v[n] for n in order])
```

```python
import functools
import math

import jax
import jax.numpy as jnp
from jax import lax
from jax.experimental import pallas as pl
from jax.experimental.pallas import tpu as pltpu

F32, BF16 = jnp.float32, jnp.bfloat16
MESH = pl.DeviceIdType.MESH
ANY = pl.BlockSpec(memory_space=pl.ANY)

POOL_WINDOWS = (2, 4, 8, 16)
HALO = 16
CHUNK = 128
MXU_COLS = 256
LN_EPS = 1e-5
DEEPNORM_ALPHA = 2.0 ** 0.25
ADAM_LR, ADAM_B1, ADAM_B2, ADAM_EPS, ADAM_WD, ADAM_STEP = 1e-3, 0.9, 0.999, 1e-8, 0.01, 10
VMEM_LIMIT = 56 * 1024 * 1024
GELU_K = math.sqrt(2.0 / math.pi)
GELU_C = 0.044715
SAVED = {"pooled": 0, "gelu_u": 1, "dgelu_u": 2, "vhat": 3, "rstd_dgelu_v": 4, "silu": 5, "dsilu": 7}
SAVED_WIDTH = 9


def _mm(a, b):
    return jnp.dot(a, b, preferred_element_type=F32)


def _mm_nt(a, b):
    return lax.dot_general(a, b, (((1,), (1,)), ((), ())), preferred_element_type=F32)


def _mm_tn(a, b):
    return lax.dot_general(a, b, (((0,), (0,)), ((), ())), preferred_element_type=F32)


def _gelu_and_grad(x):
    x2 = x * x
    t = jnp.tanh(x * (GELU_K + (GELU_K * GELU_C) * x2))
    hx = 0.5 * x
    g = hx + hx * t
    dg = (0.5 + 0.5 * t) + (hx - hx * t * t) * (GELU_K + (3.0 * GELU_K * GELU_C) * x2)
    return g, dg


def _silu_and_grad(z):
    sig = jax.nn.sigmoid(z)
    zs = z * sig
    return zs, sig + zs * (1.0 - sig)


def _norm_rows(x):
    mu = jnp.mean(x, axis=-1, keepdims=True)
    xc = x - mu
    var = jnp.mean(xc * xc, axis=-1, keepdims=True)
    rstd = lax.rsqrt(var + LN_EPS)
    return xc * rstd, rstd


def _norm_rows_bwd(dxhat, xhat, rstd):
    m1 = jnp.mean(dxhat, axis=-1, keepdims=True)
    m2 = jnp.mean(dxhat * xhat, axis=-1, keepdims=True)
    return rstd * (dxhat - m1 - xhat * m2)


def _inv_count(row0, rows, w):
    t = row0 + lax.broadcasted_iota(jnp.int32, (rows, 1), 0)
    return 1.0 / jnp.minimum(t + 1, w).astype(F32)


def _causal_window_sum(ext, w):
    s, sh = ext, 1
    while sh < w:
        s = s + pltpu.roll(s, sh, axis=0)
        sh *= 2
    return s[HALO:, :]


def _anticausal_window_sum(ext, w):
    n, s, sh = ext.shape[0], ext, 1
    while sh < w:
        s = s + pltpu.roll(s, n - sh, axis=0)
        sh *= 2
    return s[: n - HALO, :]


def _place():
    return lax.axis_index("x"), lax.axis_index("y"), lax.axis_index("c")


def _gather_weights(shards):
    n = len(shards)
    piece = [s.shape[0] // 4 for s in shards]

    def body(*refs):
        wide, dsts, srcs = refs[:n], refs[n:2 * n], refs[2 * n:3 * n]
        send_sems, recv_sems, local_sems = refs[3 * n:]
        for k in range(n):
            for r0 in range(0, 4 * piece[k], CHUNK):
                srcs[k][r0:r0 + CHUNK, :] = wide[k][r0:r0 + CHUNK, :].astype(BF16)
        x, y, c = _place()
        me, sibling = (x, y, c), (x, y, 1 - c)
        across_x, across_y = (1 - x, y, c), (x, 1 - y, c)
        q, qx, qy, qf = 2 * x + y, 2 * (1 - x) + y, 2 * x + (1 - y), 2 * (1 - x) + (1 - y)

        def rows(ref, cc, p, k):
            return ref.at[pl.ds((2 * cc + p) * piece[k], piece[k])]

        def copy(k, sem, qq, cc, p, to, own=False):
            landing = rows(dsts[k].at[qq], cc, p, k)
            return pltpu.make_async_remote_copy(
                src_ref=rows(srcs[k], cc, p, k) if own else landing, dst_ref=landing,
                send_sem=send_sems.at[12 * k + sem], recv_sem=recv_sems.at[12 * k + sem],
                device_id=to, device_id_type=MESH)

        started = []

        def go(cp):
            cp.start()
            started.append(cp)

        mine = [pltpu.make_async_copy(srcs[k], dsts[k].at[q], local_sems.at[k]) for k in range(n)]
        for cp in mine:
            cp.start()
        for k in range(n):
            for p in range(2):
                go(copy(k, p, q, c, p, across_x, own=True))
                go(copy(k, 2 + p, q, c, p, across_y, own=True))
        for k in range(n):
            copy(k, 0, qx, c, 0, me).wait_recv()
            go(copy(k, 4, qx, c, 0, across_y))
            go(copy(k, 6, qx, c, 0, sibling))
            copy(k, 3, qy, c, 1, me).wait_recv()
            go(copy(k, 5, qy, c, 1, across_x))
            go(copy(k, 9, qy, c, 1, sibling))
        for k in range(n):
            copy(k, 1, qx, c, 1, me).wait_recv()
            go(copy(k, 7, qx, c, 1, sibling))
            copy(k, 2, qy, c, 0, me).wait_recv()
            go(copy(k, 8, qy, c, 0, sibling))
        for k in range(n):
            copy(k, 4, qf, c, 0, me).wait_recv()
            go(copy(k, 10, qf, c, 0, sibling))
            copy(k, 5, qf, c, 1, me).wait_recv()
            go(copy(k, 11, qf, c, 1, sibling))
        for k in range(n):
            for sem, qq, p in ((6, qx, 0), (7, qx, 1), (8, qy, 0), (9, qy, 1), (10, qf, 0), (11, qf, 1)):
                copy(k, sem, qq, 1 - c, p, me).wait_recv()
        for cp in started:
            cp.wait_send()
        for cp in mine:
            cp.wait()

    return pl.pallas_call(
        body, name="gather_weights",
        out_shape=[jax.ShapeDtypeStruct((4,) + s.shape, BF16) for s in shards],
        in_specs=[pl.BlockSpec(memory_space=pltpu.VMEM)] * n, out_specs=[ANY] * n,
        scratch_shapes=[pltpu.VMEM(s.shape, BF16) for s in shards]
        + [pltpu.SemaphoreType.DMA((12 * n,)), pltpu.SemaphoreType.DMA((12 * n,)), pltpu.SemaphoreType.DMA((n,))],
        compiler_params=pltpu.CompilerParams(vmem_limit_bytes=VMEM_LIMIT),
    )(*shards)


def _direct_gather_copies(srcs, dsts, send_sems, recv_sems):
    x, y, c = _place()
    q = 2 * x + y
    sends, recvs = [], []
    for k, (src, dst) in enumerate(zip(srcs, dsts)):
        half = src.shape[0] // 2
        for j, chip in enumerate([(1 - x, y), (x, 1 - y), (1 - x, 1 - y)]):
            for core in range(2):
                sends.append(pltpu.make_async_remote_copy(
                    src_ref=src.at[pl.ds(c * half, half)], dst_ref=dst.at[q, pl.ds(c * half, half)],
                    send_sem=send_sems.at[6 * k + 2 * j + core], recv_sem=recv_sems.at[6 * k + 2 * j + c],
                    device_id=(*chip, core), device_id_type=MESH))
                landed = dst.at[2 * chip[0] + chip[1], pl.ds(core * half, half)]
                recvs.append(pltpu.make_async_remote_copy(
                    src_ref=landed, dst_ref=landed, send_sem=send_sems.at[6 * k + 2 * j + core],
                    recv_sem=recv_sems.at[6 * k + 2 * j + core], device_id=(x, y, c), device_id_type=MESH))
    return sends, recvs


def _swap_copies(srcs, dsts, send_sems, recv_sems):
    x, y, c = _place()
    return [pltpu.make_async_remote_copy(
        src_ref=src.at[:, 1 - c] if len(src.shape) == 4 else src, dst_ref=dst,
        send_sem=send_sems.at[k], recv_sem=recv_sems.at[k], device_id=(x, y, 1 - c), device_id_type=MESH)
        for k, (src, dst) in enumerate(zip(srcs, dsts))]


def _swap_shapes(grads):
    return [jax.ShapeDtypeStruct((4,) + g.shape[2:] if g.ndim == 4 else g.shape, g.dtype) for g in grads]


def _scatter_copies(srcs, dsts, send_sems, recv_sems):
    x, y, c = _place()
    copies = []
    for j, chip in enumerate([(1 - x, y), (x, 1 - y), (1 - x, 1 - y)]):
        for k, (src, dst) in enumerate(zip(srcs, dsts)):
            copies.append(pltpu.make_async_remote_copy(
                src_ref=src.at[2 * chip[0] + chip[1]] if len(src.shape) == 3 else src, dst_ref=dst.at[j],
                send_sem=send_sems.at[3 * k + j], recv_sem=recv_sems.at[3 * k + j],
                device_id=(*chip, c), device_id_type=MESH))
    return copies


def _scatter_shapes(parts):
    return [jax.ShapeDtypeStruct((3,) + (p.shape[1:] if p.ndim == 3 else p.shape), p.dtype) for p in parts]


def _join_halves_with_sibling(halves):
    n = len(halves)

    def body(*refs):
        srcs, dsts = refs[:n], refs[n:2 * n]
        send_sems, recv_sems = refs[2 * n:]
        x, y, c = _place()
        copies = [pltpu.make_async_remote_copy(
            src_ref=srcs[k], dst_ref=dsts[k], send_sem=send_sems.at[k], recv_sem=recv_sems.at[k],
            device_id=(x, y, 1 - c), device_id_type=MESH) for k in range(n)]
        for cp in copies:
            cp.start()
        for cp in copies:
            cp.wait()

    return pl.pallas_call(
        body, name="join_halves",
        out_shape=[jax.ShapeDtypeStruct(h.shape, h.dtype) for h in halves],
        in_specs=[ANY] * n, out_specs=[ANY] * n,
        scratch_shapes=[pltpu.SemaphoreType.DMA((n,)), pltpu.SemaphoreType.DMA((n,))],
    )(*halves)


def _row_block(rows, cols, n_arrays):
    cap = max(8, (VMEM_LIMIT // 4) // (8 * n_arrays * cols))
    rb = rows
    while rb > cap and rb % 2 == 0:
        rb //= 2
    return rb


def _scalar(value):
    return jnp.reshape(value, (1,)).astype(jnp.int32)


def _whole(a):
    return pl.BlockSpec(a.shape, lambda i, s_ref: (0,) * a.ndim)


def _add_own_halves(grads, landed, wholes, wholes_landed, c):
    n, nw = len(grads), len(wholes)

    def body(c_ref, *refs):
        g, l, w, wl = refs[:n], refs[n:2 * n], refs[2 * n:2 * n + nw], refs[2 * n + nw:2 * n + 2 * nw]
        o, wo = refs[2 * n + 2 * nw:3 * n + 2 * nw], refs[3 * n + 2 * nw:]
        for k in range(n):
            o[k][...] = (g[k][...] + l[k][...]).astype(BF16)
        for k in range(nw):
            wo[k][...] = w[k][...] + wl[k][...]

    def half(a):
        return pl.BlockSpec((4, a.shape[1] // 2, a.shape[2]), lambda i, c_ref: (0, i, 0))

    return pl.pallas_call(
        body, name="add_own_halves",
        out_shape=[jax.ShapeDtypeStruct(a.shape, BF16) for a in landed] + [jax.ShapeDtypeStruct(a.shape, F32) for a in wholes],
        grid_spec=pltpu.PrefetchScalarGridSpec(
            num_scalar_prefetch=1, grid=(2,),
            in_specs=[pl.BlockSpec((4, None, a.shape[2] // 2, a.shape[3]), lambda i, c_ref: (0, c_ref[0], i, 0))
                      for a in grads] + [half(a) for a in landed] + [_whole(a) for a in wholes + wholes_landed],
            out_specs=[half(a) for a in landed] + [_whole(a) for a in wholes]),
        compiler_params=pltpu.CompilerParams(vmem_limit_bytes=VMEM_LIMIT),
    )(_scalar(c), *grads, *landed, *wholes, *wholes_landed)


def _sum_fours(parts, slots, wholes, wholes_slots, q):
    n, nw = len(parts), len(wholes)

    def four(own, s):
        return (own[...].astype(F32) + s[0].astype(F32)) + (s[1].astype(F32) + s[2].astype(F32))

    def body(q_ref, *refs):
        p, s, w, ws = refs[:n], refs[n:2 * n], refs[2 * n:2 * n + nw], refs[2 * n + nw:2 * n + 2 * nw]
        o, wo = refs[2 * n + 2 * nw:3 * n + 2 * nw], refs[3 * n + 2 * nw:]
        for k in range(n):
            o[k][...] = four(p[k], s[k])
        for k in range(nw):
            wo[k][...] = four(w[k], ws[k])

    return pl.pallas_call(
        body, name="sum_fours",
        out_shape=[jax.ShapeDtypeStruct(a.shape[1:], F32) for a in parts] + [jax.ShapeDtypeStruct(a.shape, F32) for a in wholes],
        grid_spec=pltpu.PrefetchScalarGridSpec(
            num_scalar_prefetch=1, grid=(2,),
            in_specs=[pl.BlockSpec((None, a.shape[1] // 2, a.shape[2]), lambda i, q_ref: (q_ref[0], i, 0)) for a in parts]
            + [pl.BlockSpec((3, a.shape[1] // 2, a.shape[2]), lambda i, q_ref: (0, i, 0)) for a in slots]
            + [_whole(a) for a in wholes + wholes_slots],
            out_specs=[pl.BlockSpec((a.shape[1] // 2, a.shape[2]), lambda i, q_ref: (i, 0)) for a in parts]
            + [_whole(a) for a in wholes]),
        compiler_params=pltpu.CompilerParams(vmem_limit_bytes=VMEM_LIMIT),
    )(_scalar(q), *parts, *slots, *wholes, *wholes_slots)


def _adamw_math(w, g, m, v):
    nm = ADAM_B1 * m + (1.0 - ADAM_B1) * g
    nv = ADAM_B2 * v + (1.0 - ADAM_B2) * (g * g)
    m_hat = nm / (1.0 - ADAM_B1 ** ADAM_STEP)
    v_hat = nv / (1.0 - ADAM_B2 ** ADAM_STEP)
    return -ADAM_LR * (m_hat / (jnp.sqrt(v_hat) + ADAM_EPS) + ADAM_WD * w), nm, nv


def _adamw(w, g, m, v, name):
    rows, cols = w.shape
    rb = _row_block(rows, cols, 8)

    def body(w_ref, g_ref, m_ref, v_ref, go_ref, d_ref, nm_ref, nv_ref):
        go_ref[...] = g_ref[...]
        d_ref[...], nm_ref[...], nv_ref[...] = _adamw_math(w_ref[...], g_ref[...], m_ref[...], v_ref[...])

    spec = pl.BlockSpec((rb, cols), lambda r: (r, 0))
    out = jax.ShapeDtypeStruct(w.shape, F32)
    return pl.pallas_call(body, name=name, out_shape=[out] * 4, grid=(rows // rb,),
                          in_specs=[spec] * 4, out_specs=[spec] * 4)(w, g, m, v)


def _adamw_joined(ws, g_mine, g_sibling, ms, vs, small, small_grads, c):
    n, ns, ng = len(ws), len(small), len(small_grads)

    def body(c_ref, *refs):
        big, tot, sm = refs[:5 * n], refs[5 * n:5 * n + ng], refs[5 * n + ng:5 * n + ng + 3 * ns]
        outs = refs[5 * n + ng + 3 * ns:]
        mine = c_ref[0] == pl.program_id(0)
        for k in range(n):
            w, gm, gs, m, v = big[5 * k:5 * k + 5]
            g = jnp.where(mine, gm[...], gs[...])
            outs[4 * k][...] = g
            outs[4 * k + 1][...], outs[4 * k + 2][...], outs[4 * k + 3][...] = _adamw_math(w[...], g, m[...], v[...])
        for k in range(ns):
            w, m, v = sm[3 * k:3 * k + 3]
            which, rows, cols = small[k][3]
            g = tot[which][rows, cols]
            o = outs[4 * (n + k):4 * (n + k) + 4]
            o[0][...] = g
            o[1][...], o[2][...], o[3][...] = _adamw_math(w[...], g, m[...], v[...])

    def half(a):
        return pl.BlockSpec((a.shape[0] // 2, a.shape[1]), lambda hf, c_ref: (hf, 0))

    in_specs, operands = [], []
    for k in range(n):
        in_specs += [half(ws[k]), _whole(g_mine[k]), _whole(g_sibling[k]), half(ms[k]), half(vs[k])]
        operands += [ws[k], g_mine[k], g_sibling[k], ms[k], vs[k]]
    operands += list(small_grads) + [a for item in small for a in item[:3]]
    in_specs += [_whole(a) for a in operands[5 * n:]]
    outs = pl.pallas_call(
        body, name="adamw_joined",
        out_shape=[jax.ShapeDtypeStruct(w.shape, F32) for w in ws for _ in range(4)]
        + [jax.ShapeDtypeStruct(item[0].shape, F32) for item in small for _ in range(4)],
        grid_spec=pltpu.PrefetchScalarGridSpec(
            num_scalar_prefetch=1, grid=(2,),
            in_specs=in_specs,
            out_specs=[half(w) for w in ws for _ in range(4)] + [_whole(item[0]) for item in small for _ in range(4)]),
        compiler_params=pltpu.CompilerParams(vmem_limit_bytes=VMEM_LIMIT),
    )(_scalar(c), *operands)
    return [outs[4 * k:4 * k + 4] for k in range(n + ns)]


def _front_forward(x, w_in, pool_w, pool_scale, sgu_g, sgu_b, sgu_wm, sgu_bias_t, later_shards, tm):
    T, D = x.shape
    nq, _, cq = w_in.shape
    G, PG = pool_w.shape[0], pool_w.shape[1]
    nt = T // tm
    bpd = D // PG
    nl = len(later_shards)

    def body(x_ref, win_any, pw_any, ps_ref, lg_ref, lb_ref, sw_ref, sb_ref, *refs):
        shards_any, (keep_ref, y_ref, xt_ref), gathered = refs[:nl], refs[nl:nl + 3], refs[nl + 3:2 * nl + 3]
        wide, narrow = refs[2 * nl + 3:3 * nl + 3], refs[3 * nl + 3:4 * nl + 3]
        win_v, pw_v, carry, sems, load_sems, own_sems, send_sems, recv_sems = refs[4 * nl + 3:]
        i = pl.program_id(0)
        own_quarter = 2 * lax.axis_index("x") + lax.axis_index("y")

        def own_copies():
            return [pltpu.make_async_copy(narrow[k], gathered[k].at[own_quarter], own_sems.at[k]) for k in range(nl)]

        @pl.when(i == 0)
        def _():
            c1 = pltpu.make_async_copy(win_any, win_v, sems.at[0])
            c2 = pltpu.make_async_copy(pw_any, pw_v, sems.at[1])
            loads = [pltpu.make_async_copy(shards_any[k], wide[k], load_sems.at[k]) for k in range(nl)]
            for cp in [c1, c2] + loads:
                cp.start()
            carry[...] = jnp.zeros_like(carry)
            for k in range(nl):
                loads[k].wait()
                for r0 in range(0, wide[k].shape[0], CHUNK):
                    narrow[k][r0:r0 + CHUNK, :] = wide[k][r0:r0 + CHUNK, :].astype(BF16)
            for cp in own_copies() + _direct_gather_copies(narrow, gathered, send_sems, recv_sems)[0]:
                cp.start()
            c1.wait()
            c2.wait()

        xb = x_ref[...].astype(BF16)
        xt_ref[...] = x_ref[...].T.astype(BF16)

        def h_block(j):
            qq, off = divmod(j * PG, cq)
            return _mm(xb, win_v[qq, :, off:off + PG])

        def keep(part, col, value):
            keep_ref[:, SAVED[part] * D + col:SAVED[part] * D + col + PG] = value.astype(BF16)

        def ahead(stage):
            if stage < G:
                return h_block(stage), h_block(3 * bpd + stage)
            if stage < G + bpd:
                hd = stage - G
                return h_block(bpd + hd), h_block(2 * bpd + hd), h_block(4 * bpd + hd)
            return None

        blocks = ahead(0)

        for g, w in enumerate(POOL_WINDOWS):
            sl = slice(g * PG, (g + 1) * PG)
            a, z = blocks
            blocks = ahead(g + 1)
            ext = jnp.concatenate([carry[:, sl], a], axis=0)
            carry[:, sl] = a[tm - HALO:, :]
            pooled = (_causal_window_sum(ext, w) * _inv_count(i * tm, tm, w) - a).astype(BF16)
            mixed = _mm(pooled, pw_v[g])
            zs, dzs = _silu_and_grad(z)
            keep("pooled", g * PG, pooled)
            keep("silu", g * PG, zs)
            keep("dsilu", g * PG, dzs)
            y_ref[:, sl] = (mixed * ps_ref[:, sl] * zs).astype(BF16)

        for hd in range(bpd):
            sl = slice(hd * PG, (hd + 1) * PG)
            u, v, z = blocks
            blocks = ahead(G + hd + 1)
            ug, dug = _gelu_and_grad(u)
            vg, dvg = _gelu_and_grad(v)
            vhat, rstd = _norm_rows(vg)
            zs, dzs = _silu_and_grad(z)
            keep("gelu_u", hd * PG, ug)
            keep("dgelu_u", hd * PG, dug)
            keep("vhat", hd * PG, vhat)
            keep("rstd_dgelu_v", hd * PG, rstd * dvg)
            keep("silu", D + hd * PG, zs)
            keep("dsilu", D + hd * PG, dzs)
            vn = (vhat * lg_ref[:, sl] + lb_ref[:, sl]).astype(BF16)
            gated = ug * zs
            for n in range(tm // CHUNK):
                rs = slice(n * CHUNK, (n + 1) * CHUNK)
                sv = _mm(sw_ref[hd], vn[rs, :]) + sb_ref[:, hd:hd + 1]
                y_ref[rs, D + hd * PG:D + (hd + 1) * PG] = (gated[rs, :] * sv).astype(BF16)

        @pl.when(i == nt - 1)
        def _():
            sends, recvs = _direct_gather_copies(narrow, gathered, send_sems, recv_sems)
            for cp in sends:
                cp.wait_send()
            for cp in recvs:
                cp.wait_recv()
            for cp in own_copies():
                cp.wait()

    vec = pl.BlockSpec((1, D), lambda i: (0, 0))
    return pl.pallas_call(
        body, name="front_forward",
        out_shape=[jax.ShapeDtypeStruct((T, SAVED_WIDTH * D), BF16), jax.ShapeDtypeStruct((T, 2 * D), BF16),
                   jax.ShapeDtypeStruct((D, T), BF16)]
        + [jax.ShapeDtypeStruct((4,) + s.shape, BF16) for s in later_shards],
        grid=(nt,),
        in_specs=[pl.BlockSpec((tm, D), lambda i: (i, 0)), ANY, ANY, vec, vec, vec,
                  pl.BlockSpec(sgu_wm.shape, lambda i: (0, 0, 0)), pl.BlockSpec(sgu_bias_t.shape, lambda i: (0, 0))]
        + [ANY] * nl,
        out_specs=[pl.BlockSpec((tm, SAVED_WIDTH * D), lambda i: (i, 0)), pl.BlockSpec((tm, 2 * D), lambda i: (i, 0)),
                   pl.BlockSpec((D, tm), lambda i: (0, i))] + [ANY] * nl,
        scratch_shapes=[pltpu.VMEM(s.shape, F32) for s in later_shards]
        + [pltpu.VMEM(s.shape, BF16) for s in later_shards]
        + [pltpu.VMEM(w_in.shape, BF16), pltpu.VMEM(pool_w.shape, BF16), pltpu.VMEM((HALO, D), F32),
           pltpu.SemaphoreType.DMA((2,)), pltpu.SemaphoreType.DMA((nl,)), pltpu.SemaphoreType.DMA((nl,)),
           pltpu.SemaphoreType.DMA((6 * nl,)), pltpu.SemaphoreType.DMA((6 * nl,))],
        compiler_params=pltpu.CompilerParams(dimension_semantics=("arbitrary",), vmem_limit_bytes=VMEM_LIMIT),
    )(x, w_in, pool_w, pool_scale, sgu_g, sgu_b, sgu_wm, sgu_bias_t, *later_shards)


def _tail(y, x, p, target, w_out, w_gate, w_ple, ln_g, ln_b, gate_b, tm):
    T, D = x.shape
    K = p.shape[1]
    nq, _, cq = w_ple.shape
    nt = T // tm

    def body(y_ref, x_ref, p_ref, t_ref, wout_any, wg_any, wp_any, lng_ref, lnb_ref, bg_ref,
             dxp_ref, dy_ref, dwout_any, dwg_any, dwp_any, dlng_ref, dlnb_ref, dbg_ref, ssq_ref,
             wout_v, wg_v, wp_v, dwout_acc, dwg_acc, dwp_acc, sems):
        i = pl.program_id(0)

        @pl.when(i == 0)
        def _():
            loads = [pltpu.make_async_copy(s, d, sems.at[k])
                     for k, (s, d) in enumerate(((wout_any, wout_v), (wg_any, wg_v), (wp_any, wp_v)))]
            for cp in loads:
                cp.start()
            for ref in (dwout_acc, dwg_acc, dwp_acc, dlng_ref, dlnb_ref, dbg_ref, ssq_ref):
                ref[...] = jnp.zeros_like(ref)
            for cp in loads:
                cp.wait()

        halves = [slice(k * tm // 2, (k + 1) * tm // 2) for k in range(2)]

        def total(parts):
            return sum(jnp.sum(part, axis=0, keepdims=True) for part in parts)

        yb = [y_ref[r, :] for r in halves]
        pb = [p_ref[r, :].astype(BF16) for r in halves]
        mix = [_mm(v, wout_v[...]) for v in yb]
        normed = [_norm_rows(DEEPNORM_ALPHA * x_ref[r, :] + m) for r, m in zip(halves, mix)]
        xhat, rstd = [n[0] for n in normed], [n[1] for n in normed]
        x1 = [v * lng_ref[...] + lnb_ref[...] for v in xhat]
        x1b = [v.astype(BF16) for v in x1]
        gate = [jax.nn.sigmoid(_mm(v, wg_v[...]) + bg_ref[...]) for v in x1b]
        e = [jnp.concatenate([_mm(v, wp_v[qq]) for qq in range(nq)], axis=1) for v in pb]
        diff = [a + g * ee - t_ref[r, :] for a, g, ee, r in zip(x1, gate, e, halves)]
        ssq_ref[...] += total([d * d for d in diff])

        dout = [d * (1.0 / D) for d in diff]
        d_e = [(do * g).astype(BF16) for do, g in zip(dout, gate)]
        dgl = [do * ee * g * (1.0 - g) for do, ee, g in zip(dout, e, gate)]
        dglb = [v.astype(BF16) for v in dgl]
        dbg_ref[...] += total(dgl)
        pb_t, d_e_t, x1b_t, dglb_t = (jnp.concatenate(v, axis=0) for v in (pb, d_e, x1b, dglb))
        for qq in range(nq):
            dwp_acc[qq] += _mm_tn(pb_t, d_e_t[:, qq * cq:(qq + 1) * cq])
        for c0 in range(0, D, MXU_COLS):
            dwg_acc[:, c0:c0 + MXU_COLS] += _mm_tn(x1b_t, dglb_t[:, c0:c0 + MXU_COLS])
        d_x1 = [do + _mm_nt(dg, wg_v[...]) for do, dg in zip(dout, dglb)]
        dlng_ref[...] += total([d * xh for d, xh in zip(d_x1, xhat)])
        dlnb_ref[...] += total(d_x1)
        d_r = [_norm_rows_bwd(d * lng_ref[...], xh, rs) for d, xh, rs in zip(d_x1, xhat, rstd)]
        drb = [v.astype(BF16) for v in d_r]
        for r, v in zip(halves, d_r):
            dxp_ref[r, :] = DEEPNORM_ALPHA * v
        for c0 in range(0, 2 * D, 2 * MXU_COLS):
            for r, v in zip(halves, drb):
                dy_ref[r, c0:c0 + 2 * MXU_COLS] = _mm_nt(v, wout_v[c0:c0 + 2 * MXU_COLS, :]).astype(BF16)

        drb_t = jnp.concatenate(drb, axis=0)
        for c0 in range(0, D, MXU_COLS):
            dwout_acc[:, c0:c0 + MXU_COLS] += _mm_tn(y_ref[...], drb_t[:, c0:c0 + MXU_COLS])

        @pl.when(i == nt - 1)
        def _():
            stores = [pltpu.make_async_copy(s, d, sems.at[k])
                      for k, (s, d) in enumerate(((dwout_acc, dwout_any), (dwg_acc, dwg_any), (dwp_acc, dwp_any)))]
            for cp in stores:
                cp.start()
            for cp in stores:
                cp.wait()

    vec = pl.BlockSpec((1, D), lambda i: (0, 0))
    vec_shape = jax.ShapeDtypeStruct((1, D), F32)

    def tile(cols):
        return pl.BlockSpec((tm, cols), lambda i: (i, 0))

    return pl.pallas_call(
        body, name="tail",
        out_shape=[jax.ShapeDtypeStruct((T, D), F32), jax.ShapeDtypeStruct((T, 2 * D), BF16),
                   jax.ShapeDtypeStruct(w_out.shape, F32), jax.ShapeDtypeStruct(w_gate.shape, F32),
                   jax.ShapeDtypeStruct(w_ple.shape, F32), vec_shape, vec_shape, vec_shape, vec_shape],
        grid=(nt,),
        in_specs=[tile(2 * D), tile(D), tile(K), tile(D), ANY, ANY, ANY, vec, vec, vec],
        out_specs=[tile(D), tile(2 * D), ANY, ANY, ANY, vec, vec, vec, vec],
        scratch_shapes=[pltpu.VMEM(w_out.shape, BF16), pltpu.VMEM(w_gate.shape, BF16), pltpu.VMEM(w_ple.shape, BF16),
                        pltpu.VMEM(w_out.shape, F32), pltpu.VMEM(w_gate.shape, F32), pltpu.VMEM(w_ple.shape, F32),
                        pltpu.SemaphoreType.DMA((3,))],
        compiler_params=pltpu.CompilerParams(dimension_semantics=("arbitrary",), vmem_limit_bytes=VMEM_LIMIT),
    )(y, x, p, target, w_out, w_gate, w_ple, ln_g, ln_b, gate_b)


def _front_backward(kept, d_y, dx_part, w_in, pool_w, pool_scale, sgu_g, sgu_b, sgu_wm, sgu_bias_t, tm):
    T = kept.shape[0]
    D = kept.shape[1] // SAVED_WIDTH
    nq, _, cq = w_in.shape
    G, PG = pool_w.shape[0], pool_w.shape[1]
    nt = T // tm

    def tile_of(i):
        return nt - 1 - jnp.minimum(i, nt - 1)

    def body(kept_ref, dy_ref, dxp_ref, win_any, pw_ref, ps_ref, lg_ref, lb_ref, sw_ref, sb_ref,
             dh_ref, dx_ref, dpw_ref, dps_ref, dlg_ref, dlb_ref, dsw_ref, dsb_ref, win_v, dh_keep, carry, sems):
        i = pl.program_id(0)
        ti = tile_of(i)
        live = (i < nt).astype(F32)

        def saved(part, col, rows=slice(None)):
            return kept_ref[rows, SAVED[part] * D + col:SAVED[part] * D + col + PG]

        @pl.when(i == 0)
        def _():
            cp = pltpu.make_async_copy(win_any, win_v, sems.at[0])
            cp.start()
            carry[...] = jnp.zeros_like(carry)
            dh_keep[...] = jnp.zeros_like(dh_keep)
            for ref in (dpw_ref, dps_ref, dlg_ref, dlb_ref, dsw_ref, dsb_ref):
                ref[...] = jnp.zeros_like(ref)
            cp.wait()

        def dx_columns(r0):
            dx = dxp_ref[:, r0:r0 + MXU_COLS]
            for qq in range(nq):
                dx = dx + _mm_nt(dh_keep[(i + 1) % 2, :, qq * cq:(qq + 1) * cq], win_v[qq, r0:r0 + MXU_COLS, :])
            dx_ref[:, r0:r0 + MXU_COLS] = dx

        dx_chunks = list(range(0, D, MXU_COLS))
        stages = G + D // PG

        for g, w in enumerate(POOL_WINDOWS):
            for r0 in dx_chunks[g * len(dx_chunks) // stages:(g + 1) * len(dx_chunks) // stages]:
                dx_columns(r0)
            sl = slice(g * PG, (g + 1) * PG)
            pooled = saved("pooled", g * PG)
            mixed = _mm(pooled, pw_ref[g])
            dy = dy_ref[:, sl].astype(F32)
            d_ypool = dy * saved("silu", g * PG).astype(F32)
            dh_ref[:, 3 * D + g * PG:3 * D + (g + 1) * PG] = (
                dy * (mixed * ps_ref[:, sl]) * saved("dsilu", g * PG).astype(F32)).astype(BF16)
            dps_ref[:, sl] += live * jnp.sum(d_ypool * mixed, axis=0, keepdims=True)
            d_mixed = (d_ypool * ps_ref[:, sl]).astype(BF16)
            dpw_ref[g] += live * _mm_tn(pooled, d_mixed)
            d_pooled = _mm_nt(d_mixed, pw_ref[g])
            scaled = d_pooled * _inv_count(ti * tm, tm, w)
            after = jnp.concatenate([scaled, carry[:, sl]], axis=0)
            carry[:, sl] = jnp.where(i < nt - 1, scaled[:HALO, :], carry[:, sl])
            dh_ref[:, sl] = (_anticausal_window_sum(after, w) - d_pooled).astype(BF16)

        for hd in range(D // PG):
            for r0 in dx_chunks[(G + hd) * len(dx_chunks) // stages:(G + hd + 1) * len(dx_chunks) // stages]:
                dx_columns(r0)
            sl = slice(hd * PG, (hd + 1) * PG)
            vhat = saved("vhat", hd * PG).astype(F32)
            vn = (vhat * lg_ref[:, sl] + lb_ref[:, sl]).astype(BF16)
            d_vn_chunks = []
            for n in range(tm // CHUNK):
                rs = slice(n * CHUNK, (n + 1) * CHUNK)
                sv = _mm(sw_ref[hd], vn[rs, :]) + sb_ref[:, hd:hd + 1]
                ug = saved("gelu_u", hd * PG, rs).astype(F32)
                dy = dy_ref[rs, D + hd * PG:D + (hd + 1) * PG].astype(F32)
                d_ysgu = dy * saved("silu", D + hd * PG, rs).astype(F32)
                dh_ref[rs, 4 * D + hd * PG:4 * D + (hd + 1) * PG] = (
                    dy * (ug * sv) * saved("dsilu", D + hd * PG, rs).astype(F32)).astype(BF16)
                dh_ref[rs, D + hd * PG:D + (hd + 1) * PG] = (
                    d_ysgu * sv * saved("dgelu_u", hd * PG, rs).astype(F32)).astype(BF16)
                d_sv = d_ysgu * ug
                dsb_ref[:, hd:hd + 1] += live * jnp.sum(d_sv, axis=1, keepdims=True)
                d_svb = d_sv.astype(BF16)
                dsw_ref[hd] += live * _mm_nt(d_svb, vn[rs, :])
                d_vn_chunks.append(_mm_tn(sw_ref[hd], d_svb))
            d_vn = jnp.concatenate(d_vn_chunks, axis=0)
            dlg_ref[:, sl] += live * jnp.sum(d_vn * vhat, axis=0, keepdims=True)
            dlb_ref[:, sl] += live * jnp.sum(d_vn, axis=0, keepdims=True)
            d_vg = _norm_rows_bwd(d_vn * lg_ref[:, sl], vhat, saved("rstd_dgelu_v", hd * PG).astype(F32))
            dh_ref[:, 2 * D + hd * PG:2 * D + (hd + 1) * PG] = d_vg.astype(BF16)

        dh_keep[i % 2] = dh_ref[...]

    vec = pl.BlockSpec((1, D), lambda i: (0, 0))
    vec_shape = jax.ShapeDtypeStruct((1, D), F32)

    def whole(shape):
        return pl.BlockSpec(shape, lambda i: (0,) * len(shape))

    return pl.pallas_call(
        body, name="front_backward",
        out_shape=[jax.ShapeDtypeStruct((T, 5 * D), BF16), jax.ShapeDtypeStruct((T, D), F32),
                   jax.ShapeDtypeStruct(pool_w.shape, F32), vec_shape, vec_shape,
                   vec_shape, jax.ShapeDtypeStruct(sgu_wm.shape, F32), jax.ShapeDtypeStruct(sgu_bias_t.shape, F32)],
        grid=(nt + 1,),
        in_specs=[pl.BlockSpec((tm, SAVED_WIDTH * D), lambda i: (tile_of(i), 0)),
                  pl.BlockSpec((tm, 2 * D), lambda i: (tile_of(i), 0)),
                  pl.BlockSpec((tm, D), lambda i: (jnp.minimum(nt - i, nt - 1), 0)), ANY,
                  whole(pool_w.shape), vec, vec, vec, whole(sgu_wm.shape), whole(sgu_bias_t.shape)],
        out_specs=[pl.BlockSpec((tm, 5 * D), lambda i: (tile_of(i), 0)),
                   pl.BlockSpec((tm, D), lambda i: (jnp.minimum(nt - i, nt - 1), 0)),
                   whole(pool_w.shape), vec, vec, vec, whole(sgu_wm.shape), whole(sgu_bias_t.shape)],
        scratch_shapes=[pltpu.VMEM(w_in.shape, BF16), pltpu.VMEM((2, tm, 5 * D), BF16), pltpu.VMEM((HALO, D), F32),
                        pltpu.SemaphoreType.DMA((1,))],
        compiler_params=pltpu.CompilerParams(dimension_semantics=("arbitrary",), vmem_limit_bytes=VMEM_LIMIT),
    )(kept, d_y, dx_part, w_in, pool_w, pool_scale, sgu_g, sgu_b, sgu_wm, sgu_bias_t)


def _weight_backward(d_h, xt, q, scatter_srcs, tm):
    D, T = xt.shape
    cq = d_h.shape[1] // 4
    hr = D // 2
    nt = T // tm
    ns = len(scatter_srcs)

    def body(q_ref, dh_ref, xt_ref, *refs):
        srcs, out_any, dsts = refs[:ns], refs[ns], refs[ns + 1:2 * ns + 1]
        (acc, land_a, send_b, land_b, mine_f, theirs_f,
         a_send, a_recv, b_send, b_recv, j_sems, o_sems, s_send, s_recv) = refs[2 * ns + 1:]
        s, t = pl.program_id(0), pl.program_id(1)
        x_, y_, c = _place()
        sibling = (x_, y_, 1 - c)
        own_rows = pl.ds(pl.multiple_of(c * hr, hr), hr)
        other_rows = pl.ds(pl.multiple_of((1 - c) * hr, hr), hr)

        @pl.when((s == 0) & (t == 0))
        def _():
            for cp in _scatter_copies(srcs, dsts, s_send, s_recv):
                cp.start()

        @pl.when(t == 0)
        def _():
            acc[s % 2] = jnp.zeros((D, cq), F32)

        for c0 in range(0, cq, MXU_COLS):
            acc[s % 2, :, c0:c0 + MXU_COLS] += _mm(xt_ref[...], dh_ref[:, c0:c0 + MXU_COLS])

        def swap(phase):
            return pltpu.make_async_remote_copy(
                src_ref=acc.at[phase % 2, other_rows], dst_ref=land_a.at[phase % 2], send_sem=a_send.at[phase],
                recv_sem=a_recv.at[phase], device_id=sibling, device_id_type=MESH)

        def pair_sum(phase):
            swap(phase).wait()
            return acc[phase % 2, own_rows, :] + land_a[phase % 2]

        def to_owner(slot):
            flip_x, flip_y = (slot + 1) >> 1, (slot + 1) & 1
            owner = (1 - x_ if flip_x else x_, 1 - y_ if flip_y else y_, c)
            return pltpu.make_async_remote_copy(
                src_ref=send_b.at[slot], dst_ref=land_b.at[slot], send_sem=b_send.at[slot],
                recv_sem=b_recv.at[slot], device_id=owner, device_id_type=MESH)

        for slot in range(3):
            @pl.when((s == slot) & (t == nt - 1))
            def _(slot=slot):
                swap(slot).start()

            @pl.when((s == slot + 1) & (t == 0))
            def _(slot=slot):
                send_b[slot] = pair_sum(slot).astype(BF16)
                to_owner(slot).start()

        @pl.when((s == 3) & (t == nt - 1))
        def _():
            swap(3).start()
            own = pair_sum(3)
            for slot in range(3):
                to_owner(slot).wait_recv()
            mine_f[...] = (own + land_b[0].astype(F32)) + (land_b[1].astype(F32) + land_b[2].astype(F32))
            join = pltpu.make_async_remote_copy(
                src_ref=mine_f, dst_ref=theirs_f, send_sem=j_sems.at[0], recv_sem=j_sems.at[1],
                device_id=sibling, device_id_type=MESH)
            join.start()
            out_mine = pltpu.make_async_copy(mine_f, out_any.at[own_rows], o_sems.at[0])
            out_mine.start()
            join.wait()
            out_theirs = pltpu.make_async_copy(theirs_f, out_any.at[other_rows], o_sems.at[1])
            out_theirs.start()
            for slot in range(3):
                to_owner(slot).wait_send()
            for cp in _scatter_copies(srcs, dsts, s_send, s_recv):
                cp.wait()
            out_mine.wait()
            out_theirs.wait()

    def quarter(s, t, q_ref):
        return (t, jnp.where(s == 3, q_ref[0], q_ref[0] ^ (s + 1)))

    dma = pltpu.SemaphoreType.DMA
    return pl.pallas_call(
        body, name="weight_backward",
        out_shape=[jax.ShapeDtypeStruct((D, cq), F32)] + _scatter_shapes(scatter_srcs),
        grid_spec=pltpu.PrefetchScalarGridSpec(
            num_scalar_prefetch=1, grid=(4, nt),
            in_specs=[pl.BlockSpec((tm, cq), quarter), pl.BlockSpec((D, tm), lambda s, t, q_ref: (0, t))] + [ANY] * ns,
            out_specs=[ANY] * (ns + 1),
            scratch_shapes=[pltpu.VMEM((2, D, cq), F32), pltpu.VMEM((2, hr, cq), F32), pltpu.VMEM((3, hr, cq), BF16),
                            pltpu.VMEM((3, hr, cq), BF16), pltpu.VMEM((hr, cq), F32), pltpu.VMEM((hr, cq), F32),
                            dma((4,)), dma((4,)), dma((3,)), dma((3,)), dma((2,)), dma((2,)), dma((3 * ns,)), dma((3 * ns,))]),
        compiler_params=pltpu.CompilerParams(dimension_semantics=("arbitrary", "arbitrary"),
                                             vmem_limit_bytes=VMEM_LIMIT),
    )(jnp.reshape(q, (1,)).astype(jnp.int32), d_h, xt, *scatter_srcs)


def _swap_with_sibling(grads):
    ns = len(grads)

    def body(*refs):
        copies = _swap_copies(refs[:ns], refs[ns:2 * ns], refs[2 * ns], refs[2 * ns + 1])
        for cp in copies:
            cp.start()
        for cp in copies:
            cp.wait()

    return pl.pallas_call(
        body, name="swap_with_sibling", out_shape=_swap_shapes(grads), in_specs=[ANY] * ns, out_specs=[ANY] * ns,
        scratch_shapes=[pltpu.SemaphoreType.DMA((ns,)), pltpu.SemaphoreType.DMA((ns,))],
    )(*grads)


def _token_tile(T, want):
    return math.gcd(T, want)


def kernel(x, p, w_in, pool_w, pool_scale, sgu_ln_g, sgu_ln_b, sgu_w, sgu_b, w_out, ln_g, ln_b, ple_w, ple_gate_w, ple_gate_b, loss_target, m_w_in, m_pool_w, m_pool_scale, m_sgu_ln_g, m_sgu_ln_b, m_sgu_w, m_sgu_b, m_w_out, m_ln_g, m_ln_b, m_ple_w, m_ple_gate_w, m_ple_gate_b, v_w_in, v_pool_w, v_pool_scale, v_sgu_ln_g, v_sgu_ln_b, v_sgu_w, v_sgu_b, v_w_out, v_ln_g, v_ln_b, v_ple_w, v_ple_gate_w, v_ple_gate_b):
    c = lax.axis_index("c")
    T, D = x.shape[1], x.shape[2]
    tm, tm_vpu, tm_acc = _token_tile(T, 512), _token_tile(T, 256), _token_tile(T, 2048)
    x2, p2, tgt = x[0], p[0, 0], loss_target[0]
    G, PGQ, PG = pool_w.shape[1], pool_w.shape[2], pool_w.shape[3]

    w_in_f, pool_f = _gather_weights([w_in[0], pool_w[0].reshape(G * PGQ, PG)])
    pool_f = pool_f.reshape(4, G, PGQ, PG).transpose(1, 0, 2, 3).reshape(G, 4 * PGQ, PG)
    tril = jnp.tril(jnp.ones((CHUNK, CHUNK), dtype=bool))
    sgu_wm = jnp.where(tril[None], sgu_w[0], 0.0).astype(BF16)
    sgu_bias_t = sgu_b[0].T

    kept, y, xt, w_out_f, w_gate_f, w_ple_f = _front_forward(
        x2, w_in_f, pool_f, pool_scale, sgu_ln_g, sgu_ln_b, sgu_wm, sgu_bias_t, [w_out[0], ple_gate_w[0], ple_w[0]],
        tm_vpu)
    w_out_f = w_out_f.reshape(-1, D)
    w_gate_f = w_gate_f.reshape(-1, D)
    (dx_part, d_y, d_w_out, d_w_gate, d_w_ple, d_ln_g, d_ln_b, d_gate_b, ssq) = _tail(
        y, x2, p2, tgt, w_out_f, w_gate_f, w_ple_f, ln_g, ln_b, ple_gate_b, tm)
    d_h, d_x, d_pool_w, d_pool_scale, d_sgu_g, d_sgu_b, d_sgu_w, d_sgu_bias_t = _front_backward(
        kept, d_y, dx_part, w_in_f, pool_f, pool_scale, sgu_ln_g, sgu_ln_b, sgu_wm, sgu_bias_t, tm_vpu)
    grads = [d_w_out.reshape(4, -1, D), d_w_gate.reshape(4, -1, D), d_w_ple,
             d_pool_w.reshape(G, 4, PGQ, PG).transpose(1, 0, 2, 3).reshape(4, G * PGQ, PG)]
    grads = [g.reshape(4, 2, g.shape[1] // 2, g.shape[2]) for g in grads]
    vectors = ["pool_scale", "sgu_ln_g", "sgu_ln_b", "ln_g", "ln_b", "ple_gate_b"]
    rows = [d_pool_scale, d_sgu_g, d_sgu_b, d_ln_g, d_ln_b, d_gate_b,
            jnp.pad(jnp.reshape((0.5 / D) * jnp.sum(ssq), (1, 1)), ((0, 1), (0, D - 1))),
            jnp.pad(d_sgu_bias_t.T, ((0, 4), (0, D - CHUNK)))]
    small = jnp.concatenate(rows, axis=0)
    d_sgu_w = jnp.where(tril[None], d_sgu_w, 0.0).reshape(-1, CHUNK)
    q = 2 * lax.axis_index("x") + lax.axis_index("y")
    *landed, small_landed, sw_landed = _swap_with_sibling(grads + [small, d_sgu_w])
    *parts, small_chip, sw_chip = _add_own_halves(grads, landed, [small, d_sgu_w], [small_landed, sw_landed], c)
    d_w_in, *slots, small_slots, sw_slots = _weight_backward(d_h, xt, q, parts + [small_chip, sw_chip], tm_acc)
    *halves, small_total, sw_total = _sum_fours(parts, slots, [small_chip, sw_chip], [small_slots, sw_slots], q)
    sibling_halves = _join_halves_with_sibling(halves)
    loss = small_total[6, 0]

    big_names = ["w_out", "ple_gate_w", "ple_w", "pool_w"]
    given = dict(w_in=(w_in, m_w_in, v_w_in), w_out=(w_out, m_w_out, v_w_out),
                 ple_gate_w=(ple_gate_w, m_ple_gate_w, v_ple_gate_w), ple_w=(ple_w, m_ple_w, v_ple_w),
                 pool_w=(pool_w, m_pool_w, v_pool_w), pool_scale=(pool_scale, m_pool_scale, v_pool_scale),
                 sgu_ln_g=(sgu_ln_g, m_sgu_ln_g, v_sgu_ln_g), sgu_ln_b=(sgu_ln_b, m_sgu_ln_b, v_sgu_ln_b),
                 sgu_w=(sgu_w, m_sgu_w, v_sgu_w), sgu_b=(sgu_b, m_sgu_b, v_sgu_b), ln_g=(ln_g, m_ln_g, v_ln_g),
                 ln_b=(ln_b, m_ln_b, v_ln_b), ple_gate_b=(ple_gate_b, m_ple_gate_b, v_ple_gate_b))
    grad, delta, new_m, new_v = {}, {}, {}, {}
    grad["w_in"], delta["w_in"], new_m["w_in"], new_v["w_in"] = (
        t[None] for t in _adamw(w_in[0], d_w_in, m_w_in[0], v_w_in[0], "adamw_w_in"))
    flat = [(2 * g.shape[0], g.shape[1]) for g in halves]
    small_items = [(*given[n], (0, slice(k, k + 1), slice(None))) for k, n in enumerate(vectors)]
    small_items.append((*(t[0] for t in given["sgu_b"]), (0, slice(8, 8 + sgu_b.shape[1]), slice(0, CHUNK))))
    small_items.append((*(t.reshape(-1, CHUNK) for t in given["sgu_w"]), (1, slice(None), slice(None))))
    outs = _adamw_joined(
        [given[n][0].reshape(f) for n, f in zip(big_names, flat)], halves, sibling_halves,
        [given[n][1].reshape(f) for n, f in zip(big_names, flat)],
        [given[n][2].reshape(f) for n, f in zip(big_names, flat)], small_items, [small_total, sw_total], c)
    for name, four in zip(big_names + vectors + ["sgu_b", "sgu_w"], outs):
        grad[name], delta[name], new_m[name], new_v[name] = (t.reshape(given[name][0].shape) for t in four)

    order = ["w_in", "pool_w", "pool_scale", "sgu_ln_g", "sgu_ln_b", "sgu_w", "sgu_b", "w_out", "ln_g", "ln_b",
             "ple_w", "ple_gate_w", "ple_gate_b"]
    return (loss, d_x[None], *[grad[n] for n in order], *[delta[n] for n in order],
            *[new_m[n] for n in order], *[new_v[n] for n in order])
```

```python
import functools
import math

import jax
import jax.numpy as jnp
from jax import lax
from jax.experimental import pallas as pl
from jax.experimental.pallas import tpu as pltpu

F32, BF16 = jnp.float32, jnp.bfloat16
MESH = pl.DeviceIdType.MESH
ANY = pl.BlockSpec(memory_space=pl.ANY)

POOL_WINDOWS = (2, 4, 8, 16)
HALO = 16
CHUNK = 128
MXU_COLS = 256
LN_EPS = 1e-5
DEEPNORM_ALPHA = 2.0 ** 0.25
ADAM_LR, ADAM_B1, ADAM_B2, ADAM_EPS, ADAM_WD, ADAM_STEP = 1e-3, 0.9, 0.999, 1e-8, 0.01, 10
VMEM_LIMIT = 56 * 1024 * 1024
GELU_K = math.sqrt(2.0 / math.pi)
GELU_C = 0.044715
SAVED = {"pooled": 0, "gelu_u": 1, "dgelu_u": 2, "vhat": 3, "rstd_dgelu_v": 4, "silu": 5, "dsilu": 7}
SAVED_WIDTH = 9


def _mm(a, b):
    return jnp.dot(a, b, preferred_element_type=F32)


def _mm_nt(a, b):
    return lax.dot_general(a, b, (((1,), (1,)), ((), ())), preferred_element_type=F32)


def _mm_tn(a, b):
    return lax.dot_general(a, b, (((0,), (0,)), ((), ())), preferred_element_type=F32)


def _gelu_and_grad(x):
    x2 = x * x
    t = jnp.tanh(x * (GELU_K + (GELU_K * GELU_C) * x2))
    hx = 0.5 * x
    g = hx + hx * t
    dg = (0.5 + 0.5 * t) + (hx - hx * t * t) * (GELU_K + (3.0 * GELU_K * GELU_C) * x2)
    return g, dg


def _silu_and_grad(z):
    sig = jax.nn.sigmoid(z)
    zs = z * sig
    return zs, sig + zs * (1.0 - sig)


def _norm_rows(x):
    mu = jnp.mean(x, axis=-1, keepdims=True)
    xc = x - mu
    var = jnp.mean(xc * xc, axis=-1, keepdims=True)
    rstd = lax.rsqrt(var + LN_EPS)
    return xc * rstd, rstd


def _norm_rows_bwd(dxhat, xhat, rstd):
    m1 = jnp.mean(dxhat, axis=-1, keepdims=True)
    m2 = jnp.mean(dxhat * xhat, axis=-1, keepdims=True)
    return rstd * (dxhat - m1 - xhat * m2)


def _inv_count(row0, rows, w):
    t = row0 + lax.broadcasted_iota(jnp.int32, (rows, 1), 0)
    return 1.0 / jnp.minimum(t + 1, w).astype(F32)


def _causal_window_sum(ext, w):
    s, sh = ext, 1
    while sh < w:
        s = s + pltpu.roll(s, sh, axis=0)
        sh *= 2
    return s[HALO:, :]


def _anticausal_window_sum(ext, w):
    n, s, sh = ext.shape[0], ext, 1
    while sh < w:
        s = s + pltpu.roll(s, n - sh, axis=0)
        sh *= 2
    return s[: n - HALO, :]


def _place():
    return lax.axis_index("x"), lax.axis_index("y"), lax.axis_index("c")


def _gather_weights(shards):
    n = len(shards)
    piece = [s.shape[0] // 4 for s in shards]

    def body(*refs):
        wide, dsts, srcs = refs[:n], refs[n:2 * n], refs[2 * n:3 * n]
        send_sems, recv_sems, local_sems = refs[3 * n:]
        for k in range(n):
            for r0 in range(0, 4 * piece[k], CHUNK):
                srcs[k][r0:r0 + CHUNK, :] = wide[k][r0:r0 + CHUNK, :].astype(BF16)
        x, y, c = _place()
        me, sibling = (x, y, c), (x, y, 1 - c)
        across_x, across_y = (1 - x, y, c), (x, 1 - y, c)
        q, qx, qy, qf = 2 * x + y, 2 * (1 - x) + y, 2 * x + (1 - y), 2 * (1 - x) + (1 - y)

        def rows(ref, cc, p, k):
            return ref.at[pl.ds((2 * cc + p) * piece[k], piece[k])]

        def copy(k, sem, qq, cc, p, to, own=False):
            landing = rows(dsts[k].at[qq], cc, p, k)
            return pltpu.make_async_remote_copy(
                src_ref=rows(srcs[k], cc, p, k) if own else landing, dst_ref=landing,
                send_sem=send_sems.at[12 * k + sem], recv_sem=recv_sems.at[12 * k + sem],
                device_id=to, device_id_type=MESH)

        started = []

        def go(cp):
            cp.start()
            started.append(cp)

        mine = [pltpu.make_async_copy(srcs[k], dsts[k].at[q], local_sems.at[k]) for k in range(n)]
        for cp in mine:
            cp.start()
        for k in range(n):
            for p in range(2):
                go(copy(k, p, q, c, p, across_x, own=True))
                go(copy(k, 2 + p, q, c, p, across_y, own=True))
        for k in range(n):
            copy(k, 0, qx, c, 0, me).wait_recv()
            go(copy(k, 4, qx, c, 0, across_y))
            go(copy(k, 6, qx, c, 0, sibling))
            copy(k, 3, qy, c, 1, me).wait_recv()
            go(copy(k, 5, qy, c, 1, across_x))
            go(copy(k, 9, qy, c, 1, sibling))
        for k in range(n):
            copy(k, 1, qx, c, 1, me).wait_recv()
            go(copy(k, 7, qx, c, 1, sibling))
            copy(k, 2, qy, c, 0, me).wait_recv()
            go(copy(k, 8, qy, c, 0, sibling))
        for k in range(n):
            copy(k, 4, qf, c, 0, me).wait_recv()
            go(copy(k, 10, qf, c, 0, sibling))
            copy(k, 5, qf, c, 1, me).wait_recv()
            go(copy(k, 11, qf, c, 1, sibling))
        for k in range(n):
            for sem, qq, p in ((6, qx, 0), (7, qx, 1), (8, qy, 0), (9, qy, 1), (10, qf, 0), (11, qf, 1)):
                copy(k, sem, qq, 1 - c, p, me).wait_recv()
        for cp in started:
            cp.wait_send()
        for cp in mine:
            cp.wait()

    return pl.pallas_call(
        body, name="gather_weights",
        out_shape=[jax.ShapeDtypeStruct((4,) + s.shape, BF16) for s in shards],
        in_specs=[pl.BlockSpec(memory_space=pltpu.VMEM)] * n, out_specs=[ANY] * n,
        scratch_shapes=[pltpu.VMEM(s.shape, BF16) for s in shards]
        + [pltpu.SemaphoreType.DMA((12 * n,)), pltpu.SemaphoreType.DMA((12 * n,)), pltpu.SemaphoreType.DMA((n,))],
        compiler_params=pltpu.CompilerParams(vmem_limit_bytes=VMEM_LIMIT),
    )(*shards)


def _direct_gather_copies(srcs, dsts, send_sems, recv_sems):
    x, y, c = _place()
    q = 2 * x + y
    sends, recvs = [], []
    for k, (src, dst) in enumerate(zip(srcs, dsts)):
        half = src.shape[0] // 2
        for j, chip in enumerate([(1 - x, y), (x, 1 - y), (1 - x, 1 - y)]):
            for core in range(2):
                sends.append(pltpu.make_async_remote_copy(
                    src_ref=src.at[pl.ds(c * half, half)], dst_ref=dst.at[q, pl.ds(c * half, half)],
                    send_sem=send_sems.at[6 * k + 2 * j + core], recv_sem=recv_sems.at[6 * k + 2 * j + c],
                    device_id=(*chip, core), device_id_type=MESH))
                landed = dst.at[2 * chip[0] + chip[1], pl.ds(core * half, half)]
                recvs.append(pltpu.make_async_remote_copy(
                    src_ref=landed, dst_ref=landed, send_sem=send_sems.at[6 * k + 2 * j + core],
                    recv_sem=recv_sems.at[6 * k + 2 * j + core], device_id=(x, y, c), device_id_type=MESH))
    return sends, recvs


def _scatter_copies(srcs, dsts, send_sems, recv_sems):
    x, y, c = _place()
    copies = []
    for j, chip in enumerate([(1 - x, y), (x, 1 - y), (1 - x, 1 - y)]):
        for k, (src, dst) in enumerate(zip(srcs, dsts)):
            copies.append(pltpu.make_async_remote_copy(
                src_ref=src.at[2 * chip[0] + chip[1]] if len(src.shape) == 3 else src, dst_ref=dst.at[j],
                send_sem=send_sems.at[3 * k + j], recv_sem=recv_sems.at[3 * k + j],
                device_id=(*chip, c), device_id_type=MESH))
    return copies


def _scatter_shapes(parts):
    return [jax.ShapeDtypeStruct((3,) + (p.shape[1:] if p.ndim == 3 else p.shape), p.dtype) for p in parts]


def _join_halves_with_sibling(halves):
    n = len(halves)

    def body(*refs):
        srcs, dsts = refs[:n], refs[n:2 * n]
        send_sems, recv_sems = refs[2 * n:]
        x, y, c = _place()
        copies = [pltpu.make_async_remote_copy(
            src_ref=srcs[k], dst_ref=dsts[k], send_sem=send_sems.at[k], recv_sem=recv_sems.at[k],
            device_id=(x, y, 1 - c), device_id_type=MESH) for k in range(n)]
        for cp in copies:
            cp.start()
        for cp in copies:
            cp.wait()

    return pl.pallas_call(
        body, name="join_halves",
        out_shape=[jax.ShapeDtypeStruct(h.shape, h.dtype) for h in halves],
        in_specs=[ANY] * n, out_specs=[ANY] * n,
        scratch_shapes=[pltpu.SemaphoreType.DMA((n,)), pltpu.SemaphoreType.DMA((n,))],
    )(*halves)


def _row_block(rows, cols, n_arrays):
    cap = max(8, (VMEM_LIMIT // 4) // (8 * n_arrays * cols))
    rb = rows
    while rb > cap and rb % 2 == 0:
        rb //= 2
    return rb


def _scalar(value):
    return jnp.reshape(value, (1,)).astype(jnp.int32)


def _whole(a):
    return pl.BlockSpec(a.shape, lambda i, s_ref: (0,) * a.ndim)


def _sum_fours(parts, slots, wholes, wholes_slots, q):
    n, nw = len(parts), len(wholes)

    def four(own, s):
        return (own[...].astype(F32) + s[0].astype(F32)) + (s[1].astype(F32) + s[2].astype(F32))

    def body(q_ref, *refs):
        p, s, w, ws = refs[:n], refs[n:2 * n], refs[2 * n:2 * n + nw], refs[2 * n + nw:2 * n + 2 * nw]
        o, wo = refs[2 * n + 2 * nw:3 * n + 2 * nw], refs[3 * n + 2 * nw:]
        for k in range(n):
            o[k][...] = four(p[k], s[k])
        for k in range(nw):
            wo[k][...] = four(w[k], ws[k])

    return pl.pallas_call(
        body, name="sum_fours",
        out_shape=[jax.ShapeDtypeStruct(a.shape[1:], F32) for a in parts] + [jax.ShapeDtypeStruct(a.shape, F32) for a in wholes],
        grid_spec=pltpu.PrefetchScalarGridSpec(
            num_scalar_prefetch=1, grid=(2,),
            in_specs=[pl.BlockSpec((None, a.shape[1] // 2, a.shape[2]), lambda i, q_ref: (q_ref[0], i, 0)) for a in parts]
            + [pl.BlockSpec((3, a.shape[1] // 2, a.shape[2]), lambda i, q_ref: (0, i, 0)) for a in slots]
            + [_whole(a) for a in wholes + wholes_slots],
            out_specs=[pl.BlockSpec((a.shape[1] // 2, a.shape[2]), lambda i, q_ref: (i, 0)) for a in parts]
            + [_whole(a) for a in wholes]),
        compiler_params=pltpu.CompilerParams(vmem_limit_bytes=VMEM_LIMIT),
    )(_scalar(q), *parts, *slots, *wholes, *wholes_slots)


def _adamw_math(w, g, m, v):
    nm = ADAM_B1 * m + (1.0 - ADAM_B1) * g
    nv = ADAM_B2 * v + (1.0 - ADAM_B2) * (g * g)
    m_hat = nm / (1.0 - ADAM_B1 ** ADAM_STEP)
    v_hat = nv / (1.0 - ADAM_B2 ** ADAM_STEP)
    return -ADAM_LR * (m_hat / (jnp.sqrt(v_hat) + ADAM_EPS) + ADAM_WD * w), nm, nv


def _adamw(w, g, m, v, name):
    rows, cols = w.shape
    rb = _row_block(rows, cols, 8)

    def body(w_ref, g_ref, m_ref, v_ref, go_ref, d_ref, nm_ref, nv_ref):
        go_ref[...] = g_ref[...]
        d_ref[...], nm_ref[...], nv_ref[...] = _adamw_math(w_ref[...], g_ref[...], m_ref[...], v_ref[...])

    spec = pl.BlockSpec((rb, cols), lambda r: (r, 0))
    out = jax.ShapeDtypeStruct(w.shape, F32)
    return pl.pallas_call(body, name=name, out_shape=[out] * 4, grid=(rows // rb,),
                          in_specs=[spec] * 4, out_specs=[spec] * 4)(w, g, m, v)


def _adamw_joined(ws, g_mine, g_sibling, ms, vs, small, small_grads, c):
    n, ns, ng = len(ws), len(small), len(small_grads)

    def body(c_ref, *refs):
        big, tot, sm = refs[:5 * n], refs[5 * n:5 * n + ng], refs[5 * n + ng:5 * n + ng + 3 * ns]
        outs = refs[5 * n + ng + 3 * ns:]
        mine = c_ref[0] == pl.program_id(0)
        for k in range(n):
            w, gm, gs, m, v = big[5 * k:5 * k + 5]
            g = jnp.where(mine, gm[...], gs[...])
            outs[4 * k][...] = g
            outs[4 * k + 1][...], outs[4 * k + 2][...], outs[4 * k + 3][...] = _adamw_math(w[...], g, m[...], v[...])
        for k in range(ns):
            w, m, v = sm[3 * k:3 * k + 3]
            which, rows, cols = small[k][3]
            g = tot[which][rows, cols]
            o = outs[4 * (n + k):4 * (n + k) + 4]
            o[0][...] = g
            o[1][...], o[2][...], o[3][...] = _adamw_math(w[...], g, m[...], v[...])

    def half(a):
        return pl.BlockSpec((a.shape[0] // 2, a.shape[1]), lambda hf, c_ref: (hf, 0))

    in_specs, operands = [], []
    for k in range(n):
        in_specs += [half(ws[k]), _whole(g_mine[k]), _whole(g_sibling[k]), half(ms[k]), half(vs[k])]
        operands += [ws[k], g_mine[k], g_sibling[k], ms[k], vs[k]]
    operands += list(small_grads) + [a for item in small for a in item[:3]]
    in_specs += [_whole(a) for a in operands[5 * n:]]
    outs = pl.pallas_call(
        body, name="adamw_joined",
        out_shape=[jax.ShapeDtypeStruct(w.shape, F32) for w in ws for _ in range(4)]
        + [jax.ShapeDtypeStruct(item[0].shape, F32) for item in small for _ in range(4)],
        grid_spec=pltpu.PrefetchScalarGridSpec(
            num_scalar_prefetch=1, grid=(2,),
            in_specs=in_specs,
            out_specs=[half(w) for w in ws for _ in range(4)] + [_whole(item[0]) for item in small for _ in range(4)]),
        compiler_params=pltpu.CompilerParams(vmem_limit_bytes=VMEM_LIMIT),
    )(_scalar(c), *operands)
    return [outs[4 * k:4 * k + 4] for k in range(n + ns)]


def _front_forward(x, w_in, pool_w, pool_scale, sgu_g, sgu_b, sgu_wm, sgu_bias_t, later_shards, tm):
    T, D = x.shape
    nq, _, cq = w_in.shape
    G, PG = pool_w.shape[0], pool_w.shape[1]
    nt = T // tm
    bpd = D // PG
    nl = len(later_shards)

    def body(x_ref, win_any, pw_any, ps_ref, lg_ref, lb_ref, sw_ref, sb_ref, *refs):
        shards_any, (keep_ref, y_ref, xt_ref), gathered = refs[:nl], refs[nl:nl + 3], refs[nl + 3:2 * nl + 3]
        wide, narrow = refs[2 * nl + 3:3 * nl + 3], refs[3 * nl + 3:4 * nl + 3]
        win_v, pw_v, carry, sems, load_sems, own_sems, send_sems, recv_sems = refs[4 * nl + 3:]
        i = pl.program_id(0)
        own_quarter = 2 * lax.axis_index("x") + lax.axis_index("y")

        def own_copies():
            return [pltpu.make_async_copy(narrow[k], gathered[k].at[own_quarter], own_sems.at[k]) for k in range(nl)]

        @pl.when(i == 0)
        def _():
            c1 = pltpu.make_async_copy(win_any, win_v, sems.at[0])
            c2 = pltpu.make_async_copy(pw_any, pw_v, sems.at[1])
            loads = [pltpu.make_async_copy(shards_any[k], wide[k], load_sems.at[k]) for k in range(nl)]
            for cp in [c1, c2] + loads:
                cp.start()
            carry[...] = jnp.zeros_like(carry)
            for k in range(nl):
                loads[k].wait()
                for r0 in range(0, wide[k].shape[0], CHUNK):
                    narrow[k][r0:r0 + CHUNK, :] = wide[k][r0:r0 + CHUNK, :].astype(BF16)
            for cp in own_copies() + _direct_gather_copies(narrow, gathered, send_sems, recv_sems)[0]:
                cp.start()
            c1.wait()
            c2.wait()

        xb = x_ref[...].astype(BF16)
        xt_ref[...] = x_ref[...].T.astype(BF16)

        def h_block(j):
            qq, off = divmod(j * PG, cq)
            return _mm(xb, win_v[qq, :, off:off + PG])

        def keep(part, col, value):
            keep_ref[:, SAVED[part] * D + col:SAVED[part] * D + col + PG] = value.astype(BF16)

        def ahead(stage):
            if stage < G:
                return h_block(stage), h_block(3 * bpd + stage)
            if stage < G + bpd:
                hd = stage - G
                return h_block(bpd + hd), h_block(2 * bpd + hd), h_block(4 * bpd + hd)
            return None

        blocks = ahead(0)

        for g, w in enumerate(POOL_WINDOWS):
            sl = slice(g * PG, (g + 1) * PG)
            a, z = blocks
            blocks = ahead(g + 1)
            ext = jnp.concatenate([carry[:, sl], a], axis=0)
            carry[:, sl] = a[tm - HALO:, :]
            pooled = (_causal_window_sum(ext, w) * _inv_count(i * tm, tm, w) - a).astype(BF16)
            mixed = _mm(pooled, pw_v[g])
            zs, dzs = _silu_and_grad(z)
            keep("pooled", g * PG, pooled)
            keep("silu", g * PG, zs)
            keep("dsilu", g * PG, dzs)
            y_ref[:, sl] = (mixed * ps_ref[:, sl] * zs).astype(BF16)

        for hd in range(bpd):
            sl = slice(hd * PG, (hd + 1) * PG)
            u, v, z = blocks
            blocks = ahead(G + hd + 1)
            ug, dug = _gelu_and_grad(u)
            vg, dvg = _gelu_and_grad(v)
            vhat, rstd = _norm_rows(vg)
            zs, dzs = _silu_and_grad(z)
            keep("gelu_u", hd * PG, ug)
            keep("dgelu_u", hd * PG, dug)
            keep("vhat", hd * PG, vhat)
            keep("rstd_dgelu_v", hd * PG, rstd * dvg)
            keep("silu", D + hd * PG, zs)
            keep("dsilu", D + hd * PG, dzs)
            vn = (vhat * lg_ref[:, sl] + lb_ref[:, sl]).astype(BF16)
            gated = ug * zs
            for n in range(tm // CHUNK):
                rs = slice(n * CHUNK, (n + 1) * CHUNK)
                sv = _mm(sw_ref[hd], vn[rs, :]) + sb_ref[:, hd:hd + 1]
                y_ref[rs, D + hd * PG:D + (hd + 1) * PG] = (gated[rs, :] * sv).astype(BF16)

        @pl.when(i == nt - 1)
        def _():
            sends, recvs = _direct_gather_copies(narrow, gathered, send_sems, recv_sems)
            for cp in sends:
                cp.wait_send()
            for cp in recvs:
                cp.wait_recv()
            for cp in own_copies():
                cp.wait()

    vec = pl.BlockSpec((1, D), lambda i: (0, 0))
    return pl.pallas_call(
        body, name="front_forward",
        out_shape=[jax.ShapeDtypeStruct((T, SAVED_WIDTH * D), BF16), jax.ShapeDtypeStruct((T, 2 * D), BF16),
                   jax.ShapeDtypeStruct((D, T), BF16)]
        + [jax.ShapeDtypeStruct((4,) + s.shape, BF16) for s in later_shards],
        grid=(nt,),
        in_specs=[pl.BlockSpec((tm, D), lambda i: (i, 0)), ANY, ANY, vec, vec, vec,
                  pl.BlockSpec(sgu_wm.shape, lambda i: (0, 0, 0)), pl.BlockSpec(sgu_bias_t.shape, lambda i: (0, 0))]
        + [ANY] * nl,
        out_specs=[pl.BlockSpec((tm, SAVED_WIDTH * D), lambda i: (i, 0)), pl.BlockSpec((tm, 2 * D), lambda i: (i, 0)),
                   pl.BlockSpec((D, tm), lambda i: (0, i))] + [ANY] * nl,
        scratch_shapes=[pltpu.VMEM(s.shape, F32) for s in later_shards]
        + [pltpu.VMEM(s.shape, BF16) for s in later_shards]
        + [pltpu.VMEM(w_in.shape, BF16), pltpu.VMEM(pool_w.shape, BF16), pltpu.VMEM((HALO, D), F32),
           pltpu.SemaphoreType.DMA((2,)), pltpu.SemaphoreType.DMA((nl,)), pltpu.SemaphoreType.DMA((nl,)),
           pltpu.SemaphoreType.DMA((6 * nl,)), pltpu.SemaphoreType.DMA((6 * nl,))],
        compiler_params=pltpu.CompilerParams(dimension_semantics=("arbitrary",), vmem_limit_bytes=VMEM_LIMIT),
    )(x, w_in, pool_w, pool_scale, sgu_g, sgu_b, sgu_wm, sgu_bias_t, *later_shards)


def _tail(y, x, p, target, w_out, w_gate, w_ple, ln_g, ln_b, gate_b, tm):
    T, D = x.shape
    K = p.shape[1]
    nq, _, cq = w_ple.shape
    nt = T // tm

    def body(y_ref, x_ref, p_ref, t_ref, wout_any, wg_any, wp_any, lng_ref, lnb_ref, bg_ref,
             dxp_ref, dy_ref, dwout_any, dwg_any, dwp_any, dlng_ref, dlnb_ref, dbg_ref, ssq_ref,
             wout_v, wg_v, wp_v, dwout_acc, dwg_acc, dwp_acc, sems):
        i = pl.program_id(0)

        @pl.when(i == 0)
        def _():
            loads = [pltpu.make_async_copy(s, d, sems.at[k])
                     for k, (s, d) in enumerate(((wout_any, wout_v), (wg_any, wg_v), (wp_any, wp_v)))]
            for cp in loads:
                cp.start()
            for ref in (dwout_acc, dwg_acc, dwp_acc, dlng_ref, dlnb_ref, dbg_ref, ssq_ref):
                ref[...] = jnp.zeros_like(ref)
            for cp in loads:
                cp.wait()

        halves = [slice(k * tm // 2, (k + 1) * tm // 2) for k in range(2)]

        def total(parts):
            return sum(jnp.sum(part, axis=0, keepdims=True) for part in parts)

        yb = [y_ref[r, :] for r in halves]
        pb = [p_ref[r, :].astype(BF16) for r in halves]
        mix = [_mm(v, wout_v[...]) for v in yb]
        normed = [_norm_rows(DEEPNORM_ALPHA * x_ref[r, :] + m) for r, m in zip(halves, mix)]
        xhat, rstd = [n[0] for n in normed], [n[1] for n in normed]
        x1 = [v * lng_ref[...] + lnb_ref[...] for v in xhat]
        x1b = [v.astype(BF16) for v in x1]
        gate = [jax.nn.sigmoid(_mm(v, wg_v[...]) + bg_ref[...]) for v in x1b]
        e = [jnp.concatenate([_mm(v, wp_v[qq]) for qq in range(nq)], axis=1) for v in pb]
        diff = [a + g * ee - t_ref[r, :] for a, g, ee, r in zip(x1, gate, e, halves)]
        ssq_ref[...] += total([d * d for d in diff])

        dout = [d * (1.0 / D) for d in diff]
        d_e = [(do * g).astype(BF16) for do, g in zip(dout, gate)]
        dgl = [do * ee * g * (1.0 - g) for do, ee, g in zip(dout, e, gate)]
        dglb = [v.astype(BF16) for v in dgl]
        dbg_ref[...] += total(dgl)
        pb_t, d_e_t, x1b_t, dglb_t = (jnp.concatenate(v, axis=0) for v in (pb, d_e, x1b, dglb))
        for qq in range(nq):
            dwp_acc[qq] += _mm_tn(pb_t, d_e_t[:, qq * cq:(qq + 1) * cq])
        for c0 in range(0, D, MXU_COLS):
            dwg_acc[:, c0:c0 + MXU_COLS] += _mm_tn(x1b_t, dglb_t[:, c0:c0 + MXU_COLS])
        d_x1 = [do + _mm_nt(dg, wg_v[...]) for do, dg in zip(dout, dglb)]
        dlng_ref[...] += total([d * xh for d, xh in zip(d_x1, xhat)])
        dlnb_ref[...] += total(d_x1)
        d_r = [_norm_rows_bwd(d * lng_ref[...], xh, rs) for d, xh, rs in zip(d_x1, xhat, rstd)]
        drb = [v.astype(BF16) for v in d_r]
        for r, v in zip(halves, d_r):
            dxp_ref[r, :] = DEEPNORM_ALPHA * v
        for c0 in range(0, 2 * D, 2 * MXU_COLS):
            for r, v in zip(halves, drb):
                dy_ref[r, c0:c0 + 2 * MXU_COLS] = _mm_nt(v, wout_v[c0:c0 + 2 * MXU_COLS, :]).astype(BF16)

        drb_t = jnp.concatenate(drb, axis=0)
        for c0 in range(0, D, MXU_COLS):
            dwout_acc[:, c0:c0 + MXU_COLS] += _mm_tn(y_ref[...], drb_t[:, c0:c0 + MXU_COLS])

        @pl.when(i == nt - 1)
        def _():
            stores = [pltpu.make_async_copy(s, d, sems.at[k])
                      for k, (s, d) in enumerate(((dwout_acc, dwout_any), (dwg_acc, dwg_any), (dwp_acc, dwp_any)))]
            for cp in stores:
                cp.start()
            for cp in stores:
                cp.wait()

    vec = pl.BlockSpec((1, D), lambda i: (0, 0))
    vec_shape = jax.ShapeDtypeStruct((1, D), F32)

    def tile(cols):
        return pl.BlockSpec((tm, cols), lambda i: (i, 0))

    return pl.pallas_call(
        body, name="tail",
        out_shape=[jax.ShapeDtypeStruct((T, D), F32), jax.ShapeDtypeStruct((T, 2 * D), BF16),
                   jax.ShapeDtypeStruct(w_out.shape, F32), jax.ShapeDtypeStruct(w_gate.shape, F32),
                   jax.ShapeDtypeStruct(w_ple.shape, F32), vec_shape, vec_shape, vec_shape, vec_shape],
        grid=(nt,),
        in_specs=[tile(2 * D), tile(D), tile(K), tile(D), ANY, ANY, ANY, vec, vec, vec],
        out_specs=[tile(D), tile(2 * D), ANY, ANY, ANY, vec, vec, vec, vec],
        scratch_shapes=[pltpu.VMEM(w_out.shape, BF16), pltpu.VMEM(w_gate.shape, BF16), pltpu.VMEM(w_ple.shape, BF16),
                        pltpu.VMEM(w_out.shape, F32), pltpu.VMEM(w_gate.shape, F32), pltpu.VMEM(w_ple.shape, F32),
                        pltpu.SemaphoreType.DMA((3,))],
        compiler_params=pltpu.CompilerParams(dimension_semantics=("arbitrary",), vmem_limit_bytes=VMEM_LIMIT),
    )(y, x, p, target, w_out, w_gate, w_ple, ln_g, ln_b, gate_b)


def _front_backward(kept, d_y, dx_part, w_in, pool_w, pool_scale, sgu_g, sgu_b, sgu_wm, sgu_bias_t, tm):
    T = kept.shape[0]
    D = kept.shape[1] // SAVED_WIDTH
    nq, _, cq = w_in.shape
    G, PG = pool_w.shape[0], pool_w.shape[1]
    nt = T // tm

    def tile_of(i):
        return nt - 1 - jnp.minimum(i, nt - 1)

    def body(kept_ref, dy_ref, dxp_ref, win_any, pw_ref, ps_ref, lg_ref, lb_ref, sw_ref, sb_ref,
             dh_ref, dx_ref, dpw_ref, dps_ref, dlg_ref, dlb_ref, dsw_ref, dsb_ref, win_v, dh_keep, carry, sems):
        i = pl.program_id(0)
        ti = tile_of(i)
        live = (i < nt).astype(F32)

        def saved(part, col, rows=slice(None)):
            return kept_ref[rows, SAVED[part] * D + col:SAVED[part] * D + col + PG]

        @pl.when(i == 0)
        def _():
            cp = pltpu.make_async_copy(win_any, win_v, sems.at[0])
            cp.start()
            carry[...] = jnp.zeros_like(carry)
            dh_keep[...] = jnp.zeros_like(dh_keep)
            for ref in (dpw_ref, dps_ref, dlg_ref, dlb_ref, dsw_ref, dsb_ref):
                ref[...] = jnp.zeros_like(ref)
            cp.wait()

        def dx_columns(r0):
            dx = dxp_ref[:, r0:r0 + MXU_COLS]
            for qq in range(nq):
                dx = dx + _mm_nt(dh_keep[(i + 1) % 2, :, qq * cq:(qq + 1) * cq], win_v[qq, r0:r0 + MXU_COLS, :])
            dx_ref[:, r0:r0 + MXU_COLS] = dx

        dx_chunks = list(range(0, D, MXU_COLS))
        stages = G + D // PG

        for g, w in enumerate(POOL_WINDOWS):
            for r0 in dx_chunks[g * len(dx_chunks) // stages:(g + 1) * len(dx_chunks) // stages]:
                dx_columns(r0)
            sl = slice(g * PG, (g + 1) * PG)
            pooled = saved("pooled", g * PG)
            mixed = _mm(pooled, pw_ref[g])
            dy = dy_ref[:, sl].astype(F32)
            d_ypool = dy * saved("silu", g * PG).astype(F32)
            dh_ref[:, 3 * D + g * PG:3 * D + (g + 1) * PG] = (
                dy * (mixed * ps_ref[:, sl]) * saved("dsilu", g * PG).astype(F32)).astype(BF16)
            dps_ref[:, sl] += live * jnp.sum(d_ypool * mixed, axis=0, keepdims=True)
            d_mixed = (d_ypool * ps_ref[:, sl]).astype(BF16)
            dpw_ref[g] += live * _mm_tn(pooled, d_mixed)
            d_pooled = _mm_nt(d_mixed, pw_ref[g])
            scaled = d_pooled * _inv_count(ti * tm, tm, w)
            after = jnp.concatenate([scaled, carry[:, sl]], axis=0)
            carry[:, sl] = jnp.where(i < nt - 1, scaled[:HALO, :], carry[:, sl])
            dh_ref[:, sl] = (_anticausal_window_sum(after, w) - d_pooled).astype(BF16)

        for hd in range(D // PG):
            for r0 in dx_chunks[(G + hd) * len(dx_chunks) // stages:(G + hd + 1) * len(dx_chunks) // stages]:
                dx_columns(r0)
            sl = slice(hd * PG, (hd + 1) * PG)
            vhat = saved("vhat", hd * PG).astype(F32)
            vn = (vhat * lg_ref[:, sl] + lb_ref[:, sl]).astype(BF16)
            d_vn_chunks = []
            for n in range(tm // CHUNK):
                rs = slice(n * CHUNK, (n + 1) * CHUNK)
                sv = _mm(sw_ref[hd], vn[rs, :]) + sb_ref[:, hd:hd + 1]
                ug = saved("gelu_u", hd * PG, rs).astype(F32)
                dy = dy_ref[rs, D + hd * PG:D + (hd + 1) * PG].astype(F32)
                d_ysgu = dy * saved("silu", D + hd * PG, rs).astype(F32)
                dh_ref[rs, 4 * D + hd * PG:4 * D + (hd + 1) * PG] = (
                    dy * (ug * sv) * saved("dsilu", D + hd * PG, rs).astype(F32)).astype(BF16)
                dh_ref[rs, D + hd * PG:D + (hd + 1) * PG] = (
                    d_ysgu * sv * saved("dgelu_u", hd * PG, rs).astype(F32)).astype(BF16)
                d_sv = d_ysgu * ug
                dsb_ref[:, hd:hd + 1] += live * jnp.sum(d_sv, axis=1, keepdims=True)
                d_svb = d_sv.astype(BF16)
                dsw_ref[hd] += live * _mm_nt(d_svb, vn[rs, :])
                d_vn_chunks.append(_mm_tn(sw_ref[hd], d_svb))
            d_vn = jnp.concatenate(d_vn_chunks, axis=0)
            dlg_ref[:, sl] += live * jnp.sum(d_vn * vhat, axis=0, keepdims=True)
            dlb_ref[:, sl] += live * jnp.sum(d_vn, axis=0, keepdims=True)
            d_vg = _norm_rows_bwd(d_vn * lg_ref[:, sl], vhat, saved("rstd_dgelu_v", hd * PG).astype(F32))
            dh_ref[:, 2 * D + hd * PG:2 * D + (hd + 1) * PG] = d_vg.astype(BF16)

        dh_keep[i % 2] = dh_ref[...]

    vec = pl.BlockSpec((1, D), lambda i: (0, 0))
    vec_shape = jax.ShapeDtypeStruct((1, D), F32)

    def whole(shape):
        return pl.BlockSpec(shape, lambda i: (0,) * len(shape))

    return pl.pallas_call(
        body, name="front_backward",
        out_shape=[jax.ShapeDtypeStruct((T, 5 * D), BF16), jax.ShapeDtypeStruct((T, D), F32),
                   jax.ShapeDtypeStruct(pool_w.shape, F32), vec_shape, vec_shape,
                   vec_shape, jax.ShapeDtypeStruct(sgu_wm.shape, F32), jax.ShapeDtypeStruct(sgu_bias_t.shape, F32)],
        grid=(nt + 1,),
        in_specs=[pl.BlockSpec((tm, SAVED_WIDTH * D), lambda i: (tile_of(i), 0)),
                  pl.BlockSpec((tm, 2 * D), lambda i: (tile_of(i), 0)),
                  pl.BlockSpec((tm, D), lambda i: (jnp.minimum(nt - i, nt - 1), 0)), ANY,
                  whole(pool_w.shape), vec, vec, vec, whole(sgu_wm.shape), whole(sgu_bias_t.shape)],
        out_specs=[pl.BlockSpec((tm, 5 * D), lambda i: (tile_of(i), 0)),
                   pl.BlockSpec((tm, D), lambda i: (jnp.minimum(nt - i, nt - 1), 0)),
                   whole(pool_w.shape), vec, vec, vec, whole(sgu_wm.shape), whole(sgu_bias_t.shape)],
        scratch_shapes=[pltpu.VMEM(w_in.shape, BF16), pltpu.VMEM((2, tm, 5 * D), BF16), pltpu.VMEM((HALO, D), F32),
                        pltpu.SemaphoreType.DMA((1,))],
        compiler_params=pltpu.CompilerParams(dimension_semantics=("arbitrary",), vmem_limit_bytes=VMEM_LIMIT),
    )(kept, d_y, dx_part, w_in, pool_w, pool_scale, sgu_g, sgu_b, sgu_wm, sgu_bias_t)


def _weight_backward(d_h, xt, q, scatter_srcs, tm):
    D, T = xt.shape
    cq = d_h.shape[1] // 4
    hr = D // 2
    nt = T // tm
    ns = len(scatter_srcs)

    def body(q_ref, dh_ref, xt_ref, *refs):
        srcs, out_any, dsts = refs[:ns], refs[ns], refs[ns + 1:2 * ns + 1]
        (acc, land_a, send_b, land_b, mine_f, theirs_f,
         a_send, a_recv, b_send, b_recv, j_sems, o_sems, s_send, s_recv) = refs[2 * ns + 1:]
        s, t = pl.program_id(0), pl.program_id(1)
        x_, y_, c = _place()
        sibling = (x_, y_, 1 - c)
        own_rows = pl.ds(pl.multiple_of(c * hr, hr), hr)
        other_rows = pl.ds(pl.multiple_of((1 - c) * hr, hr), hr)

        @pl.when((s == 0) & (t == 0))
        def _():
            for cp in _scatter_copies(srcs, dsts, s_send, s_recv):
                cp.start()

        @pl.when(t == 0)
        def _():
            acc[s % 2] = jnp.zeros((D, cq), F32)

        for c0 in range(0, cq, MXU_COLS):
            acc[s % 2, :, c0:c0 + MXU_COLS] += _mm(xt_ref[...], dh_ref[:, c0:c0 + MXU_COLS])

        def swap(phase):
            return pltpu.make_async_remote_copy(
                src_ref=acc.at[phase % 2, other_rows], dst_ref=land_a.at[phase % 2], send_sem=a_send.at[phase],
                recv_sem=a_recv.at[phase], device_id=sibling, device_id_type=MESH)

        def pair_sum(phase):
            swap(phase).wait()
            return acc[phase % 2, own_rows, :] + land_a[phase % 2]

        def to_owner(slot):
            flip_x, flip_y = (slot + 1) >> 1, (slot + 1) & 1
            owner = (1 - x_ if flip_x else x_, 1 - y_ if flip_y else y_, c)
            return pltpu.make_async_remote_copy(
                src_ref=send_b.at[slot], dst_ref=land_b.at[slot], send_sem=b_send.at[slot],
                recv_sem=b_recv.at[slot], device_id=owner, device_id_type=MESH)

        for slot in range(3):
            @pl.when((s == slot) & (t == nt - 1))
            def _(slot=slot):
                swap(slot).start()

            @pl.when((s == slot + 1) & (t == 0))
            def _(slot=slot):
                send_b[slot] = pair_sum(slot).astype(BF16)
                to_owner(slot).start()

        @pl.when((s == 3) & (t == nt - 1))
        def _():
            swap(3).start()
            own = pair_sum(3)
            for slot in range(3):
                to_owner(slot).wait_recv()
            mine_f[...] = (own + land_b[0].astype(F32)) + (land_b[1].astype(F32) + land_b[2].astype(F32))
            join = pltpu.make_async_remote_copy(
                src_ref=mine_f, dst_ref=theirs_f, send_sem=j_sems.at[0], recv_sem=j_sems.at[1],
                device_id=sibling, device_id_type=MESH)
            join.start()
            out_mine = pltpu.make_async_copy(mine_f, out_any.at[own_rows], o_sems.at[0])
            out_mine.start()
            join.wait()
            out_theirs = pltpu.make_async_copy(theirs_f, out_any.at[other_rows], o_sems.at[1])
            out_theirs.start()
            for slot in range(3):
                to_owner(slot).wait_send()
            for cp in _scatter_copies(srcs, dsts, s_send, s_recv):
                cp.wait()
            out_mine.wait()
            out_theirs.wait()

    def quarter(s, t, q_ref):
        return (t, jnp.where(s == 3, q_ref[0], q_ref[0] ^ (s + 1)))

    dma = pltpu.SemaphoreType.DMA
    return pl.pallas_call(
        body, name="weight_backward",
        out_shape=[jax.ShapeDtypeStruct((D, cq), F32)] + _scatter_shapes(scatter_srcs),
        grid_spec=pltpu.PrefetchScalarGridSpec(
            num_scalar_prefetch=1, grid=(4, nt),
            in_specs=[pl.BlockSpec((tm, cq), quarter), pl.BlockSpec((D, tm), lambda s, t, q_ref: (0, t))] + [ANY] * ns,
            out_specs=[ANY] * (ns + 1),
            scratch_shapes=[pltpu.VMEM((2, D, cq), F32), pltpu.VMEM((2, hr, cq), F32), pltpu.VMEM((3, hr, cq), BF16),
                            pltpu.VMEM((3, hr, cq), BF16), pltpu.VMEM((hr, cq), F32), pltpu.VMEM((hr, cq), F32),
                            dma((4,)), dma((4,)), dma((3,)), dma((3,)), dma((2,)), dma((2,)), dma((3 * ns,)), dma((3 * ns,))]),
        compiler_params=pltpu.CompilerParams(dimension_semantics=("arbitrary", "arbitrary"),
                                             vmem_limit_bytes=VMEM_LIMIT),
    )(jnp.reshape(q, (1,)).astype(jnp.int32), d_h, xt, *scatter_srcs)


def _pair_reduce(grads, wholes):
    n, nw = len(grads), len(wholes)
    half_shapes = [(4,) + g.shape[2:] for g in grads]

    def body(*refs):
        g_any, w_any, o_any, wo_any = refs[:n], refs[n:n + nw], refs[n + nw:2 * n + nw], refs[2 * n + nw:2 * n + 2 * nw]
        rest = refs[2 * n + 2 * nw:]
        mine, theirs, send, land, out = (rest[k * n:(k + 1) * n] for k in range(5))
        w_v, w_land, w_out = (rest[5 * n + k * nw:5 * n + (k + 1) * nw] for k in range(3))
        load_sems, send_sems, recv_sems, store_sems = rest[5 * n + 3 * nw:]
        x, y, c = _place()
        sibling = (x, y, 1 - c)
        loads = [pltpu.make_async_copy(g_any[k].at[:, 1 - c], theirs[k], load_sems.at[k]) for k in range(n)]
        loads += [pltpu.make_async_copy(g_any[k].at[:, c], mine[k], load_sems.at[n + k]) for k in range(n)]
        loads += [pltpu.make_async_copy(w_any[k], w_v[k], load_sems.at[2 * n + k]) for k in range(nw)]
        for cp in loads:
            cp.start()
        sends = []
        for k in range(n):
            loads[k].wait()
            send[k][...] = theirs[k][...].astype(BF16)
            sends.append(pltpu.make_async_remote_copy(
                src_ref=send[k], dst_ref=land[k], send_sem=send_sems.at[k], recv_sem=recv_sems.at[k],
                device_id=sibling, device_id_type=MESH))
            sends[-1].start()
        for k in range(nw):
            loads[2 * n + k].wait()
            sends.append(pltpu.make_async_remote_copy(
                src_ref=w_v[k], dst_ref=w_land[k], send_sem=send_sems.at[n + k], recv_sem=recv_sems.at[n + k],
                device_id=sibling, device_id_type=MESH))
            sends[-1].start()
        stores = []
        for k in range(n):
            loads[n + k].wait()
            sends[k].wait_recv()
            out[k][...] = (mine[k][...] + land[k][...].astype(F32)).astype(BF16)
            stores.append(pltpu.make_async_copy(out[k], o_any[k], store_sems.at[k]))
            stores[-1].start()
        for k in range(nw):
            sends[n + k].wait_recv()
            w_out[k][...] = w_v[k][...] + w_land[k][...]
            stores.append(pltpu.make_async_copy(w_out[k], wo_any[k], store_sems.at[n + k]))
            stores[-1].start()
        for cp in sends:
            cp.wait_send()
        for cp in stores:
            cp.wait()

    dma = pltpu.SemaphoreType.DMA
    return pl.pallas_call(
        body, name="pair_reduce",
        out_shape=[jax.ShapeDtypeStruct(s, BF16) for s in half_shapes] + [jax.ShapeDtypeStruct(w.shape, F32) for w in wholes],
        in_specs=[ANY] * (n + nw), out_specs=[ANY] * (n + nw),
        scratch_shapes=[pltpu.VMEM(s, F32) for s in half_shapes] * 2 + [pltpu.VMEM(s, BF16) for s in half_shapes] * 3
        + [pltpu.VMEM(w.shape, F32) for w in wholes] * 3
        + [dma((2 * n + nw,)), dma((n + nw,)), dma((n + nw,)), dma((n + nw,))],
        compiler_params=pltpu.CompilerParams(vmem_limit_bytes=VMEM_LIMIT),
    )(*grads, *wholes)


def _token_tile(T, want):
    return math.gcd(T, want)


def kernel(x, p, w_in, pool_w, pool_scale, sgu_ln_g, sgu_ln_b, sgu_w, sgu_b, w_out, ln_g, ln_b, ple_w, ple_gate_w, ple_gate_b, loss_target, m_w_in, m_pool_w, m_pool_scale, m_sgu_ln_g, m_sgu_ln_b, m_sgu_w, m_sgu_b, m_w_out, m_ln_g, m_ln_b, m_ple_w, m_ple_gate_w, m_ple_gate_b, v_w_in, v_pool_w, v_pool_scale, v_sgu_ln_g, v_sgu_ln_b, v_sgu_w, v_sgu_b, v_w_out, v_ln_g, v_ln_b, v_ple_w, v_ple_gate_w, v_ple_gate_b):
    c = lax.axis_index("c")
    T, D = x.shape[1], x.shape[2]
    tm, tm_vpu, tm_acc = _token_tile(T, 512), _token_tile(T, 256), _token_tile(T, 2048)
    x2, p2, tgt = x[0], p[0, 0], loss_target[0]
    G, PGQ, PG = pool_w.shape[1], pool_w.shape[2], pool_w.shape[3]

    w_in_f, pool_f = _gather_weights([w_in[0], pool_w[0].reshape(G * PGQ, PG)])
    pool_f = pool_f.reshape(4, G, PGQ, PG).transpose(1, 0, 2, 3).reshape(G, 4 * PGQ, PG)
    tril = jnp.tril(jnp.ones((CHUNK, CHUNK), dtype=bool))
    sgu_wm = jnp.where(tril[None], sgu_w[0], 0.0).astype(BF16)
    sgu_bias_t = sgu_b[0].T

    kept, y, xt, w_out_f, w_gate_f, w_ple_f = _front_forward(
        x2, w_in_f, pool_f, pool_scale, sgu_ln_g, sgu_ln_b, sgu_wm, sgu_bias_t, [w_out[0], ple_gate_w[0], ple_w[0]],
        tm_vpu)
    w_out_f = w_out_f.reshape(-1, D)
    w_gate_f = w_gate_f.reshape(-1, D)
    (dx_part, d_y, d_w_out, d_w_gate, d_w_ple, d_ln_g, d_ln_b, d_gate_b, ssq) = _tail(
        y, x2, p2, tgt, w_out_f, w_gate_f, w_ple_f, ln_g, ln_b, ple_gate_b, tm)
    d_h, d_x, d_pool_w, d_pool_scale, d_sgu_g, d_sgu_b, d_sgu_w, d_sgu_bias_t = _front_backward(
        kept, d_y, dx_part, w_in_f, pool_f, pool_scale, sgu_ln_g, sgu_ln_b, sgu_wm, sgu_bias_t, tm_vpu)
    grads = [d_w_out.reshape(4, -1, D), d_w_gate.reshape(4, -1, D), d_w_ple,
             d_pool_w.reshape(G, 4, PGQ, PG).transpose(1, 0, 2, 3).reshape(4, G * PGQ, PG)]
    grads = [g.reshape(4, 2, g.shape[1] // 2, g.shape[2]) for g in grads]
    vectors = ["pool_scale", "sgu_ln_g", "sgu_ln_b", "ln_g", "ln_b", "ple_gate_b"]
    rows = [d_pool_scale, d_sgu_g, d_sgu_b, d_ln_g, d_ln_b, d_gate_b,
            jnp.pad(jnp.reshape((0.5 / D) * jnp.sum(ssq), (1, 1)), ((0, 1), (0, D - 1))),
            jnp.pad(d_sgu_bias_t.T, ((0, 4), (0, D - CHUNK)))]
    small = jnp.concatenate(rows, axis=0)
    d_sgu_w = jnp.where(tril[None], d_sgu_w, 0.0).reshape(-1, CHUNK)
    q = 2 * lax.axis_index("x") + lax.axis_index("y")
    *parts, small_chip, sw_chip = _pair_reduce(grads, [small, d_sgu_w])
    d_w_in, *slots, small_slots, sw_slots = _weight_backward(d_h, xt, q, parts + [small_chip, sw_chip], tm_acc)
    *halves, small_total, sw_total = _sum_fours(parts, slots, [small_chip, sw_chip], [small_slots, sw_slots], q)
    sibling_halves = _join_halves_with_sibling(halves)
    loss = small_total[6, 0]

    big_names = ["w_out", "ple_gate_w", "ple_w", "pool_w"]
    given = dict(w_in=(w_in, m_w_in, v_w_in), w_out=(w_out, m_w_out, v_w_out),
                 ple_gate_w=(ple_gate_w, m_ple_gate_w, v_ple_gate_w), ple_w=(ple_w, m_ple_w, v_ple_w),
                 pool_w=(pool_w, m_pool_w, v_pool_w), pool_scale=(pool_scale, m_pool_scale, v_pool_scale),
                 sgu_ln_g=(sgu_ln_g, m_sgu_ln_g, v_sgu_ln_g), sgu_ln_b=(sgu_ln_b, m_sgu_ln_b, v_sgu_ln_b),
                 sgu_w=(sgu_w, m_sgu_w, v_sgu_w), sgu_b=(sgu_b, m_sgu_b, v_sgu_b), ln_g=(ln_g, m_ln_g, v_ln_g),
                 ln_b=(ln_b, m_ln_b, v_ln_b), ple_gate_b=(ple_gate_b, m_ple_gate_b, v_ple_gate_b))
    grad, delta, new_m, new_v = {}, {}, {}, {}
    grad["w_in"], delta["w_in"], new_m["w_in"], new_v["w_in"] = (
        t[None] for t in _adamw(w_in[0], d_w_in, m_w_in[0], v_w_in[0], "adamw_w_in"))
    flat = [(2 * g.shape[0], g.shape[1]) for g in halves]
    small_items = [(*given[n], (0, slice(k, k + 1), slice(None))) for k, n in enumerate(vectors)]
    small_items.append((*(t[0] for t in given["sgu_b"]), (0, slice(8, 8 + sgu_b.shape[1]), slice(0, CHUNK))))
    small_items.append((*(t.reshape(-1, CHUNK) for t in given["sgu_w"]), (1, slice(None), slice(None))))
    outs = _adamw_joined(
        [given[n][0].reshape(f) for n, f in zip(big_names, flat)], halves, sibling_halves,
        [given[n][1].reshape(f) for n, f in zip(big_names, flat)],
        [given[n][2].reshape(f) for n, f in zip(big_names, flat)], small_items, [small_total, sw_total], c)
    for name, four in zip(big_names + vectors + ["sgu_b", "sgu_w"], outs):
        grad[name], delta[name], new_m[name], new_v[name] = (t.reshape(given[name][0].shape) for t in four)

    order = ["w_in", "pool_w", "pool_scale", "sgu_ln_g", "sgu_ln_b", "sgu_w", "sgu_b", "w_out", "ln_g", "ln_b",
             "ple_w", "ple_gate_w", "ple_gate_b"]
    return (loss, d_x[None], *[grad[n] for n in order], *[delta[n] for n in order],
            *[new_m[n] for n in order], *[new_v[n] for n in order])
```

```python
import functools
import math

import jax
import jax.numpy as jnp
from jax import lax
from jax.experimental import pallas as pl
from jax.experimental.pallas import tpu as pltpu

F32, BF16 = jnp.float32, jnp.bfloat16
MESH = pl.DeviceIdType.MESH
ANY = pl.BlockSpec(memory_space=pl.ANY)

POOL_WINDOWS = (2, 4, 8, 16)
HALO = 16
CHUNK = 128
MXU_COLS = 256
LN_EPS = 1e-5
DEEPNORM_ALPHA = 2.0 ** 0.25
ADAM_LR, ADAM_B1, ADAM_B2, ADAM_EPS, ADAM_WD, ADAM_STEP = 1e-3, 0.9, 0.999, 1e-8, 0.01, 10
VMEM_LIMIT = 56 * 1024 * 1024
GELU_K = math.sqrt(2.0 / math.pi)
GELU_C = 0.044715
SAVED = {"pooled": 0, "gelu_u": 1, "dgelu_u": 2, "vhat": 3, "rstd_dgelu_v": 4, "silu": 5, "dsilu": 7}
SAVED_WIDTH = 9


def _mm(a, b):
    return jnp.dot(a, b, preferred_element_type=F32)


def _mm_nt(a, b):
    return lax.dot_general(a, b, (((1,), (1,)), ((), ())), preferred_element_type=F32)


def _mm_tn(a, b):
    return lax.dot_general(a, b, (((0,), (0,)), ((), ())), preferred_element_type=F32)


def _gelu_and_grad(x):
    x2 = x * x
    t = jnp.tanh(x * (GELU_K + (GELU_K * GELU_C) * x2))
    hx = 0.5 * x
    g = hx + hx * t
    dg = (0.5 + 0.5 * t) + (hx - hx * t * t) * (GELU_K + (3.0 * GELU_K * GELU_C) * x2)
    return g, dg


def _silu_and_grad(z):
    sig = jax.nn.sigmoid(z)
    zs = z * sig
    return zs, sig + zs * (1.0 - sig)


def _norm_rows(x):
    mu = jnp.mean(x, axis=-1, keepdims=True)
    xc = x - mu
    var = jnp.mean(xc * xc, axis=-1, keepdims=True)
    rstd = lax.rsqrt(var + LN_EPS)
    return xc * rstd, rstd


def _norm_rows_bwd(dxhat, xhat, rstd):
    m1 = jnp.mean(dxhat, axis=-1, keepdims=True)
    m2 = jnp.mean(dxhat * xhat, axis=-1, keepdims=True)
    return rstd * (dxhat - m1 - xhat * m2)


def _inv_count(row0, rows, w):
    t = row0 + lax.broadcasted_iota(jnp.int32, (rows, 1), 0)
    return 1.0 / jnp.minimum(t + 1, w).astype(F32)


def _causal_window_sum(ext, w):
    s, sh = ext, 1
    while sh < w:
        s = s + pltpu.roll(s, sh, axis=0)
        sh *= 2
    return s[HALO:, :]


def _anticausal_window_sum(ext, w):
    n, s, sh = ext.shape[0], ext, 1
    while sh < w:
        s = s + pltpu.roll(s, n - sh, axis=0)
        sh *= 2
    return s[: n - HALO, :]


def _place():
    return lax.axis_index("x"), lax.axis_index("y"), lax.axis_index("c")


def _gather_weights(shards):
    n = len(shards)
    piece = [s.shape[0] // 4 for s in shards]

    def body(*refs):
        wide, dsts, srcs = refs[:n], refs[n:2 * n], refs[2 * n:3 * n]
        send_sems, recv_sems, local_sems = refs[3 * n:]
        for k in range(n):
            for r0 in range(0, 4 * piece[k], CHUNK):
                srcs[k][r0:r0 + CHUNK, :] = wide[k][r0:r0 + CHUNK, :].astype(BF16)
        x, y, c = _place()
        me, sibling = (x, y, c), (x, y, 1 - c)
        across_x, across_y = (1 - x, y, c), (x, 1 - y, c)
        q, qx, qy, qf = 2 * x + y, 2 * (1 - x) + y, 2 * x + (1 - y), 2 * (1 - x) + (1 - y)

        def rows(ref, cc, p, k):
            return ref.at[pl.ds((2 * cc + p) * piece[k], piece[k])]

        def copy(k, sem, qq, cc, p, to, own=False):
            landing = rows(dsts[k].at[qq], cc, p, k)
            return pltpu.make_async_remote_copy(
                src_ref=rows(srcs[k], cc, p, k) if own else landing, dst_ref=landing,
                send_sem=send_sems.at[12 * k + sem], recv_sem=recv_sems.at[12 * k + sem],
                device_id=to, device_id_type=MESH)

        started = []

        def go(cp):
            cp.start()
            started.append(cp)

        mine = [pltpu.make_async_copy(srcs[k], dsts[k].at[q], local_sems.at[k]) for k in range(n)]
        for cp in mine:
            cp.start()
        for k in range(n):
            for p in range(2):
                go(copy(k, p, q, c, p, across_x, own=True))
                go(copy(k, 2 + p, q, c, p, across_y, own=True))
        for k in range(n):
            copy(k, 0, qx, c, 0, me).wait_recv()
            go(copy(k, 4, qx, c, 0, across_y))
            go(copy(k, 6, qx, c, 0, sibling))
            copy(k, 3, qy, c, 1, me).wait_recv()
            go(copy(k, 5, qy, c, 1, across_x))
            go(copy(k, 9, qy, c, 1, sibling))
        for k in range(n):
            copy(k, 1, qx, c, 1, me).wait_recv()
            go(copy(k, 7, qx, c, 1, sibling))
            copy(k, 2, qy, c, 0, me).wait_recv()
            go(copy(k, 8, qy, c, 0, sibling))
        for k in range(n):
            copy(k, 4, qf, c, 0, me).wait_recv()
            go(copy(k, 10, qf, c, 0, sibling))
            copy(k, 5, qf, c, 1, me).wait_recv()
            go(copy(k, 11, qf, c, 1, sibling))
        for k in range(n):
            for sem, qq, p in ((6, qx, 0), (7, qx, 1), (8, qy, 0), (9, qy, 1), (10, qf, 0), (11, qf, 1)):
                copy(k, sem, qq, 1 - c, p, me).wait_recv()
        for cp in started:
            cp.wait_send()
        for cp in mine:
            cp.wait()

    return pl.pallas_call(
        body, name="gather_weights",
        out_shape=[jax.ShapeDtypeStruct((4,) + s.shape, BF16) for s in shards],
        in_specs=[pl.BlockSpec(memory_space=pltpu.VMEM)] * n, out_specs=[ANY] * n,
        scratch_shapes=[pltpu.VMEM(s.shape, BF16) for s in shards]
        + [pltpu.SemaphoreType.DMA((12 * n,)), pltpu.SemaphoreType.DMA((12 * n,)), pltpu.SemaphoreType.DMA((n,))],
        compiler_params=pltpu.CompilerParams(vmem_limit_bytes=VMEM_LIMIT),
    )(*shards)


def _direct_gather_copies(srcs, dsts, send_sems, recv_sems):
    x, y, c = _place()
    q = 2 * x + y
    sends, recvs = [], []
    for k, (src, dst) in enumerate(zip(srcs, dsts)):
        half = src.shape[0] // 2
        for j, chip in enumerate([(1 - x, y), (x, 1 - y), (1 - x, 1 - y)]):
            for core in range(2):
                sends.append(pltpu.make_async_remote_copy(
                    src_ref=src.at[pl.ds(c * half, half)], dst_ref=dst.at[q, pl.ds(c * half, half)],
                    send_sem=send_sems.at[6 * k + 2 * j + core], recv_sem=recv_sems.at[6 * k + 2 * j + c],
                    device_id=(*chip, core), device_id_type=MESH))
                landed = dst.at[2 * chip[0] + chip[1], pl.ds(core * half, half)]
                recvs.append(pltpu.make_async_remote_copy(
                    src_ref=landed, dst_ref=landed, send_sem=send_sems.at[6 * k + 2 * j + core],
                    recv_sem=recv_sems.at[6 * k + 2 * j + core], device_id=(x, y, c), device_id_type=MESH))
    return sends, recvs


def _scatter_copies(srcs, dsts, send_sems, recv_sems):
    x, y, c = _place()
    copies = []
    for j, chip in enumerate([(1 - x, y), (x, 1 - y), (1 - x, 1 - y)]):
        for k, (src, dst) in enumerate(zip(srcs, dsts)):
            copies.append(pltpu.make_async_remote_copy(
                src_ref=src.at[2 * chip[0] + chip[1]] if len(src.shape) == 3 else src, dst_ref=dst.at[j],
                send_sem=send_sems.at[3 * k + j], recv_sem=recv_sems.at[3 * k + j],
                device_id=(*chip, c), device_id_type=MESH))
    return copies


def _scatter_shapes(parts):
    return [jax.ShapeDtypeStruct((3,) + (p.shape[1:] if p.ndim == 3 else p.shape), p.dtype) for p in parts]


def _join_halves_with_sibling(halves):
    n = len(halves)

    def body(*refs):
        srcs, dsts = refs[:n], refs[n:2 * n]
        send_sems, recv_sems = refs[2 * n:]
        x, y, c = _place()
        copies = [pltpu.make_async_remote_copy(
            src_ref=srcs[k], dst_ref=dsts[k], send_sem=send_sems.at[k], recv_sem=recv_sems.at[k],
            device_id=(x, y, 1 - c), device_id_type=MESH) for k in range(n)]
        for cp in copies:
            cp.start()
        for cp in copies:
            cp.wait()

    return pl.pallas_call(
        body, name="join_halves",
        out_shape=[jax.ShapeDtypeStruct(h.shape, h.dtype) for h in halves],
        in_specs=[ANY] * n, out_specs=[ANY] * n,
        scratch_shapes=[pltpu.SemaphoreType.DMA((n,)), pltpu.SemaphoreType.DMA((n,))],
    )(*halves)


def _row_block(rows, cols, n_arrays):
    cap = max(8, (VMEM_LIMIT // 4) // (8 * n_arrays * cols))
    rb = rows
    while rb > cap and rb % 2 == 0:
        rb //= 2
    return rb


def _scalar(value):
    return jnp.reshape(value, (1,)).astype(jnp.int32)


def _whole(a):
    return pl.BlockSpec(a.shape, lambda i, s_ref: (0,) * a.ndim)


def _sum_fours(parts, slots, wholes, wholes_slots, q):
    n, nw = len(parts), len(wholes)

    def four(own, s):
        return (own[...].astype(F32) + s[0].astype(F32)) + (s[1].astype(F32) + s[2].astype(F32))

    def body(q_ref, *refs):
        p, s, w, ws = refs[:n], refs[n:2 * n], refs[2 * n:2 * n + nw], refs[2 * n + nw:2 * n + 2 * nw]
        o, wo = refs[2 * n + 2 * nw:3 * n + 2 * nw], refs[3 * n + 2 * nw:]
        for k in range(n):
            o[k][...] = four(p[k], s[k])
        for k in range(nw):
            wo[k][...] = four(w[k], ws[k])

    return pl.pallas_call(
        body, name="sum_fours",
        out_shape=[jax.ShapeDtypeStruct(a.shape[1:], F32) for a in parts] + [jax.ShapeDtypeStruct(a.shape, F32) for a in wholes],
        grid_spec=pltpu.PrefetchScalarGridSpec(
            num_scalar_prefetch=1, grid=(2,),
            in_specs=[pl.BlockSpec((None, a.shape[1] // 2, a.shape[2]), lambda i, q_ref: (q_ref[0], i, 0)) for a in parts]
            + [pl.BlockSpec((3, a.shape[1] // 2, a.shape[2]), lambda i, q_ref: (0, i, 0)) for a in slots]
            + [_whole(a) for a in wholes + wholes_slots],
            out_specs=[pl.BlockSpec((a.shape[1] // 2, a.shape[2]), lambda i, q_ref: (i, 0)) for a in parts]
            + [_whole(a) for a in wholes]),
        compiler_params=pltpu.CompilerParams(vmem_limit_bytes=VMEM_LIMIT),
    )(_scalar(q), *parts, *slots, *wholes, *wholes_slots)


def _adamw_math(w, g, m, v):
    nm = ADAM_B1 * m + (1.0 - ADAM_B1) * g
    nv = ADAM_B2 * v + (1.0 - ADAM_B2) * (g * g)
    m_hat = nm / (1.0 - ADAM_B1 ** ADAM_STEP)
    v_hat = nv / (1.0 - ADAM_B2 ** ADAM_STEP)
    return -ADAM_LR * (m_hat / (jnp.sqrt(v_hat) + ADAM_EPS) + ADAM_WD * w), nm, nv


def _adamw(w, g, m, v, name):
    rows, cols = w.shape
    rb = _row_block(rows, cols, 8)

    def body(w_ref, g_ref, m_ref, v_ref, go_ref, d_ref, nm_ref, nv_ref):
        go_ref[...] = g_ref[...]
        d_ref[...], nm_ref[...], nv_ref[...] = _adamw_math(w_ref[...], g_ref[...], m_ref[...], v_ref[...])

    spec = pl.BlockSpec((rb, cols), lambda r: (r, 0))
    out = jax.ShapeDtypeStruct(w.shape, F32)
    return pl.pallas_call(body, name=name, out_shape=[out] * 4, grid=(rows // rb,),
                          in_specs=[spec] * 4, out_specs=[spec] * 4)(w, g, m, v)


def _adamw_joined(ws, g_mine, g_sibling, ms, vs, small, small_grads, c):
    n, ns, ng = len(ws), len(small), len(small_grads)

    def body(c_ref, *refs):
        big, tot, sm = refs[:5 * n], refs[5 * n:5 * n + ng], refs[5 * n + ng:5 * n + ng + 3 * ns]
        outs = refs[5 * n + ng + 3 * ns:]
        mine = c_ref[0] == pl.program_id(0)
        for k in range(n):
            w, gm, gs, m, v = big[5 * k:5 * k + 5]
            g = jnp.where(mine, gm[...], gs[...])
            outs[4 * k][...] = g
            outs[4 * k + 1][...], outs[4 * k + 2][...], outs[4 * k + 3][...] = _adamw_math(w[...], g, m[...], v[...])
        for k in range(ns):
            w, m, v = sm[3 * k:3 * k + 3]
            which, rows, cols = small[k][3]
            g = tot[which][rows, cols]
            o = outs[4 * (n + k):4 * (n + k) + 4]
            o[0][...] = g
            o[1][...], o[2][...], o[3][...] = _adamw_math(w[...], g, m[...], v[...])

    def half(a):
        return pl.BlockSpec((a.shape[0] // 2, a.shape[1]), lambda hf, c_ref: (hf, 0))

    in_specs, operands = [], []
    for k in range(n):
        in_specs += [half(ws[k]), _whole(g_mine[k]), _whole(g_sibling[k]), half(ms[k]), half(vs[k])]
        operands += [ws[k], g_mine[k], g_sibling[k], ms[k], vs[k]]
    operands += list(small_grads) + [a for item in small for a in item[:3]]
    in_specs += [_whole(a) for a in operands[5 * n:]]
    outs = pl.pallas_call(
        body, name="adamw_joined",
        out_shape=[jax.ShapeDtypeStruct(w.shape, F32) for w in ws for _ in range(4)]
        + [jax.ShapeDtypeStruct(item[0].shape, F32) for item in small for _ in range(4)],
        grid_spec=pltpu.PrefetchScalarGridSpec(
            num_scalar_prefetch=1, grid=(2,),
            in_specs=in_specs,
            out_specs=[half(w) for w in ws for _ in range(4)] + [_whole(item[0]) for item in small for _ in range(4)]),
        compiler_params=pltpu.CompilerParams(vmem_limit_bytes=VMEM_LIMIT),
    )(_scalar(c), *operands)
    return [outs[4 * k:4 * k + 4] for k in range(n + ns)]


def _front_forward(x, w_in, pool_w, pool_scale, sgu_g, sgu_b, sgu_wm, sgu_bias_t, later_shards, tm):
    T, D = x.shape
    nq, _, cq = w_in.shape
    G, PG = pool_w.shape[0], pool_w.shape[1]
    nt = T // tm
    bpd = D // PG
    nl = len(later_shards)

    def body(x_ref, win_any, pw_any, ps_ref, lg_ref, lb_ref, sw_ref, sb_ref, *refs):
        shards_any, (keep_ref, y_ref, xt_ref), gathered = refs[:nl], refs[nl:nl + 3], refs[nl + 3:2 * nl + 3]
        wide, narrow = refs[2 * nl + 3:3 * nl + 3], refs[3 * nl + 3:4 * nl + 3]
        win_v, pw_v, carry, sems, load_sems, own_sems, send_sems, recv_sems = refs[4 * nl + 3:]
        i = pl.program_id(0)
        own_quarter = 2 * lax.axis_index("x") + lax.axis_index("y")

        def own_copies():
            return [pltpu.make_async_copy(narrow[k], gathered[k].at[own_quarter], own_sems.at[k]) for k in range(nl)]

        @pl.when(i == 0)
        def _():
            c1 = pltpu.make_async_copy(win_any, win_v, sems.at[0])
            c2 = pltpu.make_async_copy(pw_any, pw_v, sems.at[1])
            loads = [pltpu.make_async_copy(shards_any[k], wide[k], load_sems.at[k]) for k in range(nl)]
            for cp in [c1, c2] + loads:
                cp.start()
            carry[...] = jnp.zeros_like(carry)
            for k in range(nl):
                loads[k].wait()
                for r0 in range(0, wide[k].shape[0], CHUNK):
                    narrow[k][r0:r0 + CHUNK, :] = wide[k][r0:r0 + CHUNK, :].astype(BF16)
            for cp in own_copies() + _direct_gather_copies(narrow, gathered, send_sems, recv_sems)[0]:
                cp.start()
            c1.wait()
            c2.wait()

        xb = x_ref[...].astype(BF16)
        xt_ref[...] = x_ref[...].T.astype(BF16)

        def h_block(j):
            qq, off = divmod(j * PG, cq)
            return _mm(xb, win_v[qq, :, off:off + PG])

        def keep(part, col, value):
            keep_ref[:, SAVED[part] * D + col:SAVED[part] * D + col + PG] = value.astype(BF16)

        def ahead(stage):
            if stage < G:
                return h_block(stage), h_block(3 * bpd + stage)
            if stage < G + bpd:
                hd = stage - G
                return h_block(bpd + hd), h_block(2 * bpd + hd), h_block(4 * bpd + hd)
            return None

        blocks = ahead(0)

        for g, w in enumerate(POOL_WINDOWS):
            sl = slice(g * PG, (g + 1) * PG)
            a, z = blocks
            blocks = ahead(g + 1)
            ext = jnp.concatenate([carry[:, sl], a], axis=0)
            carry[:, sl] = a[tm - HALO:, :]
            pooled = (_causal_window_sum(ext, w) * _inv_count(i * tm, tm, w) - a).astype(BF16)
            mixed = _mm(pooled, pw_v[g])
            zs, dzs = _silu_and_grad(z)
            keep("pooled", g * PG, pooled)
            keep("silu", g * PG, zs)
            keep("dsilu", g * PG, dzs)
            y_ref[:, sl] = (mixed * ps_ref[:, sl] * zs).astype(BF16)

        for hd in range(bpd):
            sl = slice(hd * PG, (hd + 1) * PG)
            u, v, z = blocks
            blocks = ahead(G + hd + 1)
            ug, dug = _gelu_and_grad(u)
            vg, dvg = _gelu_and_grad(v)
            vhat, rstd = _norm_rows(vg)
            zs, dzs = _silu_and_grad(z)
            keep("gelu_u", hd * PG, ug)
            keep("dgelu_u", hd * PG, dug)
            keep("vhat", hd * PG, vhat)
            keep("rstd_dgelu_v", hd * PG, rstd * dvg)
            keep("silu", D + hd * PG, zs)
            keep("dsilu", D + hd * PG, dzs)
            vn = (vhat * lg_ref[:, sl] + lb_ref[:, sl]).astype(BF16)
            gated = ug * zs
            for n in range(tm // CHUNK):
                rs = slice(n * CHUNK, (n + 1) * CHUNK)
                sv = _mm(sw_ref[hd], vn[rs, :]) + sb_ref[:, hd:hd + 1]
                y_ref[rs, D + hd * PG:D + (hd + 1) * PG] = (gated[rs, :] * sv).astype(BF16)

        @pl.when(i == nt - 1)
        def _():
            sends, recvs = _direct_gather_copies(narrow, gathered, send_sems, recv_sems)
            for cp in sends:
                cp.wait_send()
            for cp in recvs:
                cp.wait_recv()
            for cp in own_copies():
                cp.wait()

    vec = pl.BlockSpec((1, D), lambda i: (0, 0))
    return pl.pallas_call(
        body, name="front_forward",
        out_shape=[jax.ShapeDtypeStruct((T, SAVED_WIDTH * D), BF16), jax.ShapeDtypeStruct((T, 2 * D), BF16),
                   jax.ShapeDtypeStruct((D, T), BF16)]
        + [jax.ShapeDtypeStruct((4,) + s.shape, BF16) for s in later_shards],
        grid=(nt,),
        in_specs=[pl.BlockSpec((tm, D), lambda i: (i, 0)), ANY, ANY, vec, vec, vec,
                  pl.BlockSpec(sgu_wm.shape, lambda i: (0, 0, 0)), pl.BlockSpec(sgu_bias_t.shape, lambda i: (0, 0))]
        + [ANY] * nl,
        out_specs=[pl.BlockSpec((tm, SAVED_WIDTH * D), lambda i: (i, 0)), pl.BlockSpec((tm, 2 * D), lambda i: (i, 0)),
                   pl.BlockSpec((D, tm), lambda i: (0, i))] + [ANY] * nl,
        scratch_shapes=[pltpu.VMEM(s.shape, F32) for s in later_shards]
        + [pltpu.VMEM(s.shape, BF16) for s in later_shards]
        + [pltpu.VMEM(w_in.shape, BF16), pltpu.VMEM(pool_w.shape, BF16), pltpu.VMEM((HALO, D), F32),
           pltpu.SemaphoreType.DMA((2,)), pltpu.SemaphoreType.DMA((nl,)), pltpu.SemaphoreType.DMA((nl,)),
           pltpu.SemaphoreType.DMA((6 * nl,)), pltpu.SemaphoreType.DMA((6 * nl,))],
        compiler_params=pltpu.CompilerParams(dimension_semantics=("arbitrary",), vmem_limit_bytes=VMEM_LIMIT),
    )(x, w_in, pool_w, pool_scale, sgu_g, sgu_b, sgu_wm, sgu_bias_t, *later_shards)


def _tail(y, x, p, target, w_out, w_gate, w_ple, ln_g, ln_b, gate_b, tm):
    T, D = x.shape
    K = p.shape[1]
    nq, _, cq = w_ple.shape
    nt = T // tm

    def body(y_ref, x_ref, p_ref, t_ref, wout_any, wg_any, wp_any, lng_ref, lnb_ref, bg_ref,
             dxp_ref, dy_ref, dwout_any, dwg_any, dwp_any, dlng_ref, dlnb_ref, dbg_ref, ssq_ref,
             wout_v, wg_v, wp_v, dwout_acc, dwg_acc, dwp_acc, sems):
        i = pl.program_id(0)

        @pl.when(i == 0)
        def _():
            loads = [pltpu.make_async_copy(s, d, sems.at[k])
                     for k, (s, d) in enumerate(((wout_any, wout_v), (wg_any, wg_v), (wp_any, wp_v)))]
            for cp in loads:
                cp.start()
            for ref in (dwout_acc, dwg_acc, dwp_acc, dlng_ref, dlnb_ref, dbg_ref, ssq_ref):
                ref[...] = jnp.zeros_like(ref)
            for cp in loads:
                cp.wait()

        halves = [slice(k * tm // 2, (k + 1) * tm // 2) for k in range(2)]

        def total(parts):
            return sum(jnp.sum(part, axis=0, keepdims=True) for part in parts)

        yb = [y_ref[r, :] for r in halves]
        pb = [p_ref[r, :].astype(BF16) for r in halves]
        mix = [_mm(v, wout_v[...]) for v in yb]
        normed = [_norm_rows(DEEPNORM_ALPHA * x_ref[r, :] + m) for r, m in zip(halves, mix)]
        xhat, rstd = [n[0] for n in normed], [n[1] for n in normed]
        x1 = [v * lng_ref[...] + lnb_ref[...] for v in xhat]
        x1b = [v.astype(BF16) for v in x1]
        gate = [jax.nn.sigmoid(_mm(v, wg_v[...]) + bg_ref[...]) for v in x1b]
        e = [jnp.concatenate([_mm(v, wp_v[qq]) for qq in range(nq)], axis=1) for v in pb]
        diff = [a + g * ee - t_ref[r, :] for a, g, ee, r in zip(x1, gate, e, halves)]
        ssq_ref[...] += total([d * d for d in diff])

        dout = [d * (1.0 / D) for d in diff]
        d_e = [(do * g).astype(BF16) for do, g in zip(dout, gate)]
        dgl = [do * ee * g * (1.0 - g) for do, ee, g in zip(dout, e, gate)]
        dglb = [v.astype(BF16) for v in dgl]
        dbg_ref[...] += total(dgl)
        pb_t, d_e_t, x1b_t, dglb_t = (jnp.concatenate(v, axis=0) for v in (pb, d_e, x1b, dglb))
        for qq in range(nq):
            dwp_acc[qq] += _mm_tn(pb_t, d_e_t[:, qq * cq:(qq + 1) * cq])
        for c0 in range(0, D, MXU_COLS):
            dwg_acc[:, c0:c0 + MXU_COLS] += _mm_tn(x1b_t, dglb_t[:, c0:c0 + MXU_COLS])
        d_x1 = [do + _mm_nt(dg, wg_v[...]) for do, dg in zip(dout, dglb)]
        dlng_ref[...] += total([d * xh for d, xh in zip(d_x1, xhat)])
        dlnb_ref[...] += total(d_x1)
        d_r = [_norm_rows_bwd(d * lng_ref[...], xh, rs) for d, xh, rs in zip(d_x1, xhat, rstd)]
        drb = [v.astype(BF16) for v in d_r]
        for r, v in zip(halves, d_r):
            dxp_ref[r, :] = DEEPNORM_ALPHA * v
        for c0 in range(0, 2 * D, 2 * MXU_COLS):
            for r, v in zip(halves, drb):
                dy_ref[r, c0:c0 + 2 * MXU_COLS] = _mm_nt(v, wout_v[c0:c0 + 2 * MXU_COLS, :]).astype(BF16)

        drb_t = jnp.concatenate(drb, axis=0)
        for c0 in range(0, D, MXU_COLS):
            dwout_acc[:, c0:c0 + MXU_COLS] += _mm_tn(y_ref[...], drb_t[:, c0:c0 + MXU_COLS])

        @pl.when(i == nt - 1)
        def _():
            stores = [pltpu.make_async_copy(s, d, sems.at[k])
                      for k, (s, d) in enumerate(((dwout_acc, dwout_any), (dwg_acc, dwg_any), (dwp_acc, dwp_any)))]
            for cp in stores:
                cp.start()
            for cp in stores:
                cp.wait()

    vec = pl.BlockSpec((1, D), lambda i: (0, 0))
    vec_shape = jax.ShapeDtypeStruct((1, D), F32)

    def tile(cols):
        return pl.BlockSpec((tm, cols), lambda i: (i, 0))

    return pl.pallas_call(
        body, name="tail",
        out_shape=[jax.ShapeDtypeStruct((T, D), F32), jax.ShapeDtypeStruct((T, 2 * D), BF16),
                   jax.ShapeDtypeStruct(w_out.shape, F32), jax.ShapeDtypeStruct(w_gate.shape, F32),
                   jax.ShapeDtypeStruct(w_ple.shape, F32), vec_shape, vec_shape, vec_shape, vec_shape],
        grid=(nt,),
        in_specs=[tile(2 * D), tile(D), tile(K), tile(D), ANY, ANY, ANY, vec, vec, vec],
        out_specs=[tile(D), tile(2 * D), ANY, ANY, ANY, vec, vec, vec, vec],
        scratch_shapes=[pltpu.VMEM(w_out.shape, BF16), pltpu.VMEM(w_gate.shape, BF16), pltpu.VMEM(w_ple.shape, BF16),
                        pltpu.VMEM(w_out.shape, F32), pltpu.VMEM(w_gate.shape, F32), pltpu.VMEM(w_ple.shape, F32),
                        pltpu.SemaphoreType.DMA((3,))],
        compiler_params=pltpu.CompilerParams(dimension_semantics=("arbitrary",), vmem_limit_bytes=VMEM_LIMIT),
    )(y, x, p, target, w_out, w_gate, w_ple, ln_g, ln_b, gate_b)


def _front_backward(kept, d_y, dx_part, w_in, pool_w, pool_scale, sgu_g, sgu_b, sgu_wm, sgu_bias_t, tm):
    T = kept.shape[0]
    D = kept.shape[1] // SAVED_WIDTH
    nq, _, cq = w_in.shape
    G, PG = pool_w.shape[0], pool_w.shape[1]
    nt = T // tm

    def tile_of(i):
        return nt - 1 - jnp.minimum(i, nt - 1)

    def body(kept_ref, dy_ref, dxp_ref, win_any, pw_ref, ps_ref, lg_ref, lb_ref, sw_ref, sb_ref,
             dh_ref, dx_ref, dpw_ref, dps_ref, dlg_ref, dlb_ref, dsw_ref, dsb_ref, win_v, dh_keep, carry, sems):
        i = pl.program_id(0)
        ti = tile_of(i)

        def saved(part, col, rows=slice(None)):
            return kept_ref[rows, SAVED[part] * D + col:SAVED[part] * D + col + PG]

        @pl.when(i == 0)
        def _():
            cp = pltpu.make_async_copy(win_any, win_v, sems.at[0])
            cp.start()
            carry[...] = jnp.zeros_like(carry)
            for ref in (dpw_ref, dps_ref, dlg_ref, dlb_ref, dsw_ref, dsb_ref):
                ref[...] = jnp.zeros_like(ref)
            cp.wait()

        def dx_columns(r0):
            dx = dxp_ref[:, r0:r0 + MXU_COLS]
            for qq in range(nq):
                dx = dx + _mm_nt(dh_keep[(i + 1) % 2, :, qq * cq:(qq + 1) * cq], win_v[qq, r0:r0 + MXU_COLS, :])
            dx_ref[:, r0:r0 + MXU_COLS] = dx

        dx_chunks = list(range(0, D, MXU_COLS))
        stages = G + D // PG

        def pool_stage(g, w):
            sl = slice(g * PG, (g + 1) * PG)
            pooled = saved("pooled", g * PG)
            mixed = _mm(pooled, pw_ref[g])
            dy = dy_ref[:, sl].astype(F32)
            d_ypool = dy * saved("silu", g * PG).astype(F32)
            dh_ref[:, 3 * D + g * PG:3 * D + (g + 1) * PG] = (
                dy * (mixed * ps_ref[:, sl]) * saved("dsilu", g * PG).astype(F32)).astype(BF16)
            dps_ref[:, sl] += jnp.sum(d_ypool * mixed, axis=0, keepdims=True)
            d_mixed = (d_ypool * ps_ref[:, sl]).astype(BF16)
            dpw_ref[g] += _mm_tn(pooled, d_mixed)
            d_pooled = _mm_nt(d_mixed, pw_ref[g])
            scaled = d_pooled * _inv_count(ti * tm, tm, w)
            after = jnp.concatenate([scaled, carry[:, sl]], axis=0)
            carry[:, sl] = scaled[:HALO, :]
            dh_ref[:, sl] = (_anticausal_window_sum(after, w) - d_pooled).astype(BF16)

        def gating_stage(hd):
            sl = slice(hd * PG, (hd + 1) * PG)
            vhat = saved("vhat", hd * PG).astype(F32)
            vn = (vhat * lg_ref[:, sl] + lb_ref[:, sl]).astype(BF16)
            d_vn_chunks = []
            for n in range(tm // CHUNK):
                rs = slice(n * CHUNK, (n + 1) * CHUNK)
                sv = _mm(sw_ref[hd], vn[rs, :]) + sb_ref[:, hd:hd + 1]
                ug = saved("gelu_u", hd * PG, rs).astype(F32)
                dy = dy_ref[rs, D + hd * PG:D + (hd + 1) * PG].astype(F32)
                d_ysgu = dy * saved("silu", D + hd * PG, rs).astype(F32)
                dh_ref[rs, 4 * D + hd * PG:4 * D + (hd + 1) * PG] = (
                    dy * (ug * sv) * saved("dsilu", D + hd * PG, rs).astype(F32)).astype(BF16)
                dh_ref[rs, D + hd * PG:D + (hd + 1) * PG] = (
                    d_ysgu * sv * saved("dgelu_u", hd * PG, rs).astype(F32)).astype(BF16)
                d_sv = d_ysgu * ug
                dsb_ref[:, hd:hd + 1] += jnp.sum(d_sv, axis=1, keepdims=True)
                d_svb = d_sv.astype(BF16)
                dsw_ref[hd] += _mm_nt(d_svb, vn[rs, :])
                d_vn_chunks.append(_mm_tn(sw_ref[hd], d_svb))
            d_vn = jnp.concatenate(d_vn_chunks, axis=0)
            dlg_ref[:, sl] += jnp.sum(d_vn * vhat, axis=0, keepdims=True)
            dlb_ref[:, sl] += jnp.sum(d_vn, axis=0, keepdims=True)
            d_vg = _norm_rows_bwd(d_vn * lg_ref[:, sl], vhat, saved("rstd_dgelu_v", hd * PG).astype(F32))
            dh_ref[:, 2 * D + hd * PG:2 * D + (hd + 1) * PG] = d_vg.astype(BF16)

        def work(make_dx, make_dh):
            for stage in range(stages):
                if make_dx:
                    for r0 in dx_chunks[stage * len(dx_chunks) // stages:(stage + 1) * len(dx_chunks) // stages]:
                        dx_columns(r0)
                if make_dh and stage < G:
                    pool_stage(stage, POOL_WINDOWS[stage])
                elif make_dh:
                    gating_stage(stage - G)
            if make_dh:
                dh_keep[i % 2] = dh_ref[...]

        pl.when(i == 0)(functools.partial(work, False, True))
        pl.when((i > 0) & (i < nt))(functools.partial(work, True, True))
        pl.when(i == nt)(functools.partial(work, True, False))

    vec = pl.BlockSpec((1, D), lambda i: (0, 0))
    vec_shape = jax.ShapeDtypeStruct((1, D), F32)

    def whole(shape):
        return pl.BlockSpec(shape, lambda i: (0,) * len(shape))

    return pl.pallas_call(
        body, name="front_backward",
        out_shape=[jax.ShapeDtypeStruct((T, 5 * D), BF16), jax.ShapeDtypeStruct((T, D), F32),
                   jax.ShapeDtypeStruct(pool_w.shape, F32), vec_shape, vec_shape,
                   vec_shape, jax.ShapeDtypeStruct(sgu_wm.shape, F32), jax.ShapeDtypeStruct(sgu_bias_t.shape, F32)],
        grid=(nt + 1,),
        in_specs=[pl.BlockSpec((tm, SAVED_WIDTH * D), lambda i: (tile_of(i), 0)),
                  pl.BlockSpec((tm, 2 * D), lambda i: (tile_of(i), 0)),
                  pl.BlockSpec((tm, D), lambda i: (jnp.minimum(nt - i, nt - 1), 0)), ANY,
                  whole(pool_w.shape), vec, vec, vec, whole(sgu_wm.shape), whole(sgu_bias_t.shape)],
        out_specs=[pl.BlockSpec((tm, 5 * D), lambda i: (tile_of(i), 0)),
                   pl.BlockSpec((tm, D), lambda i: (jnp.minimum(nt - i, nt - 1), 0)),
                   whole(pool_w.shape), vec, vec, vec, whole(sgu_wm.shape), whole(sgu_bias_t.shape)],
        scratch_shapes=[pltpu.VMEM(w_in.shape, BF16), pltpu.VMEM((2, tm, 5 * D), BF16), pltpu.VMEM((HALO, D), F32),
                        pltpu.SemaphoreType.DMA((1,))],
        compiler_params=pltpu.CompilerParams(dimension_semantics=("arbitrary",), vmem_limit_bytes=VMEM_LIMIT),
    )(kept, d_y, dx_part, w_in, pool_w, pool_scale, sgu_g, sgu_b, sgu_wm, sgu_bias_t)


def _weight_backward(d_h, xt, q, scatter_srcs, tm):
    D, T = xt.shape
    cq = d_h.shape[1] // 4
    hr = D // 2
    nt = T // tm
    ns = len(scatter_srcs)

    def body(q_ref, dh_ref, xt_ref, *refs):
        srcs, out_any, dsts = refs[:ns], refs[ns], refs[ns + 1:2 * ns + 1]
        (acc, land_a, send_b, land_b, mine_f, theirs_f,
         a_send, a_recv, b_send, b_recv, j_sems, o_sems, s_send, s_recv) = refs[2 * ns + 1:]
        s, t = pl.program_id(0), pl.program_id(1)
        x_, y_, c = _place()
        sibling = (x_, y_, 1 - c)
        own_rows = pl.ds(pl.multiple_of(c * hr, hr), hr)
        other_rows = pl.ds(pl.multiple_of((1 - c) * hr, hr), hr)

        @pl.when((s == 0) & (t == 0))
        def _():
            for cp in _scatter_copies(srcs, dsts, s_send, s_recv):
                cp.start()

        @pl.when(t == 0)
        def _():
            acc[s % 2] = jnp.zeros((D, cq), F32)

        for c0 in range(0, cq, MXU_COLS):
            acc[s % 2, :, c0:c0 + MXU_COLS] += _mm(xt_ref[...], dh_ref[:, c0:c0 + MXU_COLS])

        def swap(phase):
            return pltpu.make_async_remote_copy(
                src_ref=acc.at[phase % 2, other_rows], dst_ref=land_a.at[phase % 2], send_sem=a_send.at[phase],
                recv_sem=a_recv.at[phase], device_id=sibling, device_id_type=MESH)

        def pair_sum(phase):
            swap(phase).wait()
            return acc[phase % 2, own_rows, :] + land_a[phase % 2]

        def to_owner(slot):
            flip_x, flip_y = (slot + 1) >> 1, (slot + 1) & 1
            owner = (1 - x_ if flip_x else x_, 1 - y_ if flip_y else y_, c)
            return pltpu.make_async_remote_copy(
                src_ref=send_b.at[slot], dst_ref=land_b.at[slot], send_sem=b_send.at[slot],
                recv_sem=b_recv.at[slot], device_id=owner, device_id_type=MESH)

        for slot in range(3):
            @pl.when((s == slot) & (t == nt - 1))
            def _(slot=slot):
                swap(slot).start()

            @pl.when((s == slot + 1) & (t == 0))
            def _(slot=slot):
                send_b[slot] = pair_sum(slot).astype(BF16)
                to_owner(slot).start()

        @pl.when((s == 3) & (t == nt - 1))
        def _():
            swap(3).start()
            own = pair_sum(3)
            for slot in range(3):
                to_owner(slot).wait_recv()
            mine_f[...] = (own + land_b[0].astype(F32)) + (land_b[1].astype(F32) + land_b[2].astype(F32))
            join = pltpu.make_async_remote_copy(
                src_ref=mine_f, dst_ref=theirs_f, send_sem=j_sems.at[0], recv_sem=j_sems.at[1],
                device_id=sibling, device_id_type=MESH)
            join.start()
            out_mine = pltpu.make_async_copy(mine_f, out_any.at[own_rows], o_sems.at[0])
            out_mine.start()
            join.wait()
            out_theirs = pltpu.make_async_copy(theirs_f, out_any.at[other_rows], o_sems.at[1])
            out_theirs.start()
            for slot in range(3):
                to_owner(slot).wait_send()
            for cp in _scatter_copies(srcs, dsts, s_send, s_recv):
                cp.wait()
            out_mine.wait()
            out_theirs.wait()

    def quarter(s, t, q_ref):
        return (t, jnp.where(s == 3, q_ref[0], q_ref[0] ^ (s + 1)))

    dma = pltpu.SemaphoreType.DMA
    return pl.pallas_call(
        body, name="weight_backward",
        out_shape=[jax.ShapeDtypeStruct((D, cq), F32)] + _scatter_shapes(scatter_srcs),
        grid_spec=pltpu.PrefetchScalarGridSpec(
            num_scalar_prefetch=1, grid=(4, nt),
            in_specs=[pl.BlockSpec((tm, cq), quarter), pl.BlockSpec((D, tm), lambda s, t, q_ref: (0, t))] + [ANY] * ns,
            out_specs=[ANY] * (ns + 1),
            scratch_shapes=[pltpu.VMEM((2, D, cq), F32), pltpu.VMEM((2, hr, cq), F32), pltpu.VMEM((3, hr, cq), BF16),
                            pltpu.VMEM((3, hr, cq), BF16), pltpu.VMEM((hr, cq), F32), pltpu.VMEM((hr, cq), F32),
                            dma((4,)), dma((4,)), dma((3,)), dma((3,)), dma((2,)), dma((2,)), dma((3 * ns,)), dma((3 * ns,))]),
        compiler_params=pltpu.CompilerParams(dimension_semantics=("arbitrary", "arbitrary"),
                                             vmem_limit_bytes=VMEM_LIMIT),
    )(jnp.reshape(q, (1,)).astype(jnp.int32), d_h, xt, *scatter_srcs)


def _pair_reduce(grads, wholes):
    n, nw = len(grads), len(wholes)
    half_shapes = [(4,) + g.shape[2:] for g in grads]

    def body(*refs):
        g_any, w_any, o_any, wo_any = refs[:n], refs[n:n + nw], refs[n + nw:2 * n + nw], refs[2 * n + nw:2 * n + 2 * nw]
        rest = refs[2 * n + 2 * nw:]
        mine, theirs, send, land, out = (rest[k * n:(k + 1) * n] for k in range(5))
        w_v, w_land, w_out = (rest[5 * n + k * nw:5 * n + (k + 1) * nw] for k in range(3))
        load_sems, send_sems, recv_sems, store_sems = rest[5 * n + 3 * nw:]
        x, y, c = _place()
        sibling = (x, y, 1 - c)
        loads = [pltpu.make_async_copy(g_any[k].at[:, 1 - c], theirs[k], load_sems.at[k]) for k in range(n)]
        loads += [pltpu.make_async_copy(g_any[k].at[:, c], mine[k], load_sems.at[n + k]) for k in range(n)]
        loads += [pltpu.make_async_copy(w_any[k], w_v[k], load_sems.at[2 * n + k]) for k in range(nw)]
        for cp in loads:
            cp.start()
        sends = []
        for k in range(n):
            loads[k].wait()
            send[k][...] = theirs[k][...].astype(BF16)
            sends.append(pltpu.make_async_remote_copy(
                src_ref=send[k], dst_ref=land[k], send_sem=send_sems.at[k], recv_sem=recv_sems.at[k],
                device_id=sibling, device_id_type=MESH))
            sends[-1].start()
        for k in range(nw):
            loads[2 * n + k].wait()
            sends.append(pltpu.make_async_remote_copy(
                src_ref=w_v[k], dst_ref=w_land[k], send_sem=send_sems.at[n + k], recv_sem=recv_sems.at[n + k],
                device_id=sibling, device_id_type=MESH))
            sends[-1].start()
        stores = []
        for k in range(n):
            loads[n + k].wait()
            sends[k].wait_recv()
            out[k][...] = (mine[k][...] + land[k][...].astype(F32)).astype(BF16)
            stores.append(pltpu.make_async_copy(out[k], o_any[k], store_sems.at[k]))
            stores[-1].start()
        for k in range(nw):
            sends[n + k].wait_recv()
            w_out[k][...] = w_v[k][...] + w_land[k][...]
            stores.append(pltpu.make_async_copy(w_out[k], wo_any[k], store_sems.at[n + k]))
            stores[-1].start()
        for cp in sends:
            cp.wait_send()
        for cp in stores:
            cp.wait()

    dma = pltpu.SemaphoreType.DMA
    return pl.pallas_call(
        body, name="pair_reduce",
        out_shape=[jax.ShapeDtypeStruct(s, BF16) for s in half_shapes] + [jax.ShapeDtypeStruct(w.shape, F32) for w in wholes],
        in_specs=[ANY] * (n + nw), out_specs=[ANY] * (n + nw),
        scratch_shapes=[pltpu.VMEM(s, F32) for s in half_shapes] * 2 + [pltpu.VMEM(s, BF16) for s in half_shapes] * 3
        + [pltpu.VMEM(w.shape, F32) for w in wholes] * 3
        + [dma((2 * n + nw,)), dma((n + nw,)), dma((n + nw,)), dma((n + nw,))],
        compiler_params=pltpu.CompilerParams(vmem_limit_bytes=VMEM_LIMIT),
    )(*grads, *wholes)


def _token_tile(T, want):
    return math.gcd(T, want)


def kernel(x, p, w_in, pool_w, pool_scale, sgu_ln_g, sgu_ln_b, sgu_w, sgu_b, w_out, ln_g, ln_b, ple_w, ple_gate_w, ple_gate_b, loss_target, m_w_in, m_pool_w, m_pool_scale, m_sgu_ln_g, m_sgu_ln_b, m_sgu_w, m_sgu_b, m_w_out, m_ln_g, m_ln_b, m_ple_w, m_ple_gate_w, m_ple_gate_b, v_w_in, v_pool_w, v_pool_scale, v_sgu_ln_g, v_sgu_ln_b, v_sgu_w, v_sgu_b, v_w_out, v_ln_g, v_ln_b, v_ple_w, v_ple_gate_w, v_ple_gate_b):
    c = lax.axis_index("c")
    T, D = x.shape[1], x.shape[2]
    tm, tm_vpu, tm_acc = _token_tile(T, 512), _token_tile(T, 256), _token_tile(T, 2048)
    x2, p2, tgt = x[0], p[0, 0], loss_target[0]
    G, PGQ, PG = pool_w.shape[1], pool_w.shape[2], pool_w.shape[3]

    w_in_f, pool_f = _gather_weights([w_in[0], pool_w[0].reshape(G * PGQ, PG)])
    pool_f = pool_f.reshape(4, G, PGQ, PG).transpose(1, 0, 2, 3).reshape(G, 4 * PGQ, PG)
    tril = jnp.tril(jnp.ones((CHUNK, CHUNK), dtype=bool))
    sgu_wm = jnp.where(tril[None], sgu_w[0], 0.0).astype(BF16)
    sgu_bias_t = sgu_b[0].T

    kept, y, xt, w_out_f, w_gate_f, w_ple_f = _front_forward(
        x2, w_in_f, pool_f, pool_scale, sgu_ln_g, sgu_ln_b, sgu_wm, sgu_bias_t, [w_out[0], ple_gate_w[0], ple_w[0]],
        tm_vpu)
    w_out_f = w_out_f.reshape(-1, D)
    w_gate_f = w_gate_f.reshape(-1, D)
    (dx_part, d_y, d_w_out, d_w_gate, d_w_ple, d_ln_g, d_ln_b, d_gate_b, ssq) = _tail(
        y, x2, p2, tgt, w_out_f, w_gate_f, w_ple_f, ln_g, ln_b, ple_gate_b, tm)
    d_h, d_x, d_pool_w, d_pool_scale, d_sgu_g, d_sgu_b, d_sgu_w, d_sgu_bias_t = _front_backward(
        kept, d_y, dx_part, w_in_f, pool_f, pool_scale, sgu_ln_g, sgu_ln_b, sgu_wm, sgu_bias_t, tm_vpu)
    grads = [d_w_out.reshape(4, -1, D), d_w_gate.reshape(4, -1, D), d_w_ple,
             d_pool_w.reshape(G, 4, PGQ, PG).transpose(1, 0, 2, 3).reshape(4, G * PGQ, PG)]
    grads = [g.reshape(4, 2, g.shape[1] // 2, g.shape[2]) for g in grads]
    vectors = ["pool_scale", "sgu_ln_g", "sgu_ln_b", "ln_g", "ln_b", "ple_gate_b"]
    rows = [d_pool_scale, d_sgu_g, d_sgu_b, d_ln_g, d_ln_b, d_gate_b,
            jnp.pad(jnp.reshape((0.5 / D) * jnp.sum(ssq), (1, 1)), ((0, 1), (0, D - 1))),
            jnp.pad(d_sgu_bias_t.T, ((0, 4), (0, D - CHUNK)))]
    small = jnp.concatenate(rows, axis=0)
    d_sgu_w = jnp.where(tril[None], d_sgu_w, 0.0).reshape(-1, CHUNK)
    q = 2 * lax.axis_index("x") + lax.axis_index("y")
    *parts, small_chip, sw_chip = _pair_reduce(grads, [small, d_sgu_w])
    d_w_in, *slots, small_slots, sw_slots = _weight_backward(d_h, xt, q, parts + [small_chip, sw_chip], tm_acc)
    *halves, small_total, sw_total = _sum_fours(parts, slots, [small_chip, sw_chip], [small_slots, sw_slots], q)
    sibling_halves = _join_halves_with_sibling(halves)
    loss = small_total[6, 0]

    big_names = ["w_out", "ple_gate_w", "ple_w", "pool_w"]
    given = dict(w_in=(w_in, m_w_in, v_w_in), w_out=(w_out, m_w_out, v_w_out),
                 ple_gate_w=(ple_gate_w, m_ple_gate_w, v_ple_gate_w), ple_w=(ple_w, m_ple_w, v_ple_w),
                 pool_w=(pool_w, m_pool_w, v_pool_w), pool_scale=(pool_scale, m_pool_scale, v_pool_scale),
                 sgu_ln_g=(sgu_ln_g, m_sgu_ln_g, v_sgu_ln_g), sgu_ln_b=(sgu_ln_b, m_sgu_ln_b, v_sgu_ln_b),
                 sgu_w=(sgu_w, m_sgu_w, v_sgu_w), sgu_b=(sgu_b, m_sgu_b, v_sgu_b), ln_g=(ln_g, m_ln_g, v_ln_g),
                 ln_b=(ln_b, m_ln_b, v_ln_b), ple_gate_b=(ple_gate_b, m_ple_gate_b, v_ple_gate_b))
    grad, delta, new_m, new_v = {}, {}, {}, {}
    grad["w_in"], delta["w_in"], new_m["w_in"], new_v["w_in"] = (
        t[None] for t in _adamw(w_in[0], d_w_in, m_w_in[0], v_w_in[0], "adamw_w_in"))
    flat = [(2 * g.shape[0], g.shape[1]) for g in halves]
    small_items = [(*given[n], (0, slice(k, k + 1), slice(None))) for k, n in enumerate(vectors)]
    small_items.append((*(t[0] for t in given["sgu_b"]), (0, slice(8, 8 + sgu_b.shape[1]), slice(0, CHUNK))))
    small_items.append((*(t.reshape(-1, CHUNK) for t in given["sgu_w"]), (1, slice(None), slice(None))))
    outs = _adamw_joined(
        [given[n][0].reshape(f) for n, f in zip(big_names, flat)], halves, sibling_halves,
        [given[n][1].reshape(f) for n, f in zip(big_names, flat)],
        [given[n][2].reshape(f) for n, f in zip(big_names, flat)], small_items, [small_total, sw_total], c)
    for name, four in zip(big_names + vectors + ["sgu_b", "sgu_w"], outs):
        grad[name], delta[name], new_m[name], new_v[name] = (t.reshape(given[name][0].shape) for t in four)

    order = ["w_in", "pool_w", "pool_scale", "sgu_ln_g", "sgu_ln_b", "sgu_w", "sgu_b", "w_out", "ln_g", "ln_b",
             "ple_w", "ple_gate_w", "ple_gate_b"]
    return (loss, d_x[None], *[grad[n] for n in order], *[delta[n] for n in order],
            *[new_m[n] for n in order], *[new_v[n] for n in order])
```

```python
import functools
import math

import jax
import jax.numpy as jnp
from jax import lax
from jax.experimental import pallas as pl
from jax.experimental.pallas import tpu as pltpu

F32, BF16 = jnp.float32, jnp.bfloat16
MESH = pl.DeviceIdType.MESH
ANY = pl.BlockSpec(memory_space=pl.ANY)

POOL_WINDOWS = (2, 4, 8, 16)
HALO = 16
CHUNK = 128
MXU_COLS = 256
LN_EPS = 1e-5
DEEPNORM_ALPHA = 2.0 ** 0.25
ADAM_LR, ADAM_B1, ADAM_B2, ADAM_EPS, ADAM_WD, ADAM_STEP = 1e-3, 0.9, 0.999, 1e-8, 0.01, 10
VMEM_LIMIT = 56 * 1024 * 1024
GELU_K = math.sqrt(2.0 / math.pi)
GELU_C = 0.044715
SAVED = {"pooled": 0, "gelu_u": 1, "dgelu_u": 2, "vhat": 3, "rstd_dgelu_v": 4, "z": 5}
SAVED_WIDTH = 7


def _mm(a, b):
    return jnp.dot(a, b, preferred_element_type=F32)


def _mm_nt(a, b):
    return lax.dot_general(a, b, (((1,), (1,)), ((), ())), preferred_element_type=F32)


def _mm_tn(a, b):
    return lax.dot_general(a, b, (((0,), (0,)), ((), ())), preferred_element_type=F32)


def _gelu_and_grad(x):
    x2 = x * x
    t = jnp.tanh(x * (GELU_K + (GELU_K * GELU_C) * x2))
    hx = 0.5 * x
    g = hx + hx * t
    dg = (0.5 + 0.5 * t) + (hx - hx * t * t) * (GELU_K + (3.0 * GELU_K * GELU_C) * x2)
    return g, dg


def _silu_and_grad(z):
    sig = jax.nn.sigmoid(z)
    zs = z * sig
    return zs, sig + zs * (1.0 - sig)


def _norm_rows(x):
    mu = jnp.mean(x, axis=-1, keepdims=True)
    xc = x - mu
    var = jnp.mean(xc * xc, axis=-1, keepdims=True)
    rstd = lax.rsqrt(var + LN_EPS)
    return xc * rstd, rstd


def _norm_rows_bwd(dxhat, xhat, rstd):
    m1 = jnp.mean(dxhat, axis=-1, keepdims=True)
    m2 = jnp.mean(dxhat * xhat, axis=-1, keepdims=True)
    return rstd * (dxhat - m1 - xhat * m2)


def _inv_count(row0, rows, w):
    t = row0 + lax.broadcasted_iota(jnp.int32, (rows, 1), 0)
    return 1.0 / jnp.minimum(t + 1, w).astype(F32)


def _causal_window_sum(ext, w):
    s, sh = ext, 1
    while sh < w:
        s = s + pltpu.roll(s, sh, axis=0)
        sh *= 2
    return s[HALO:, :]


def _anticausal_window_sum(ext, w):
    n, s, sh = ext.shape[0], ext, 1
    while sh < w:
        s = s + pltpu.roll(s, n - sh, axis=0)
        sh *= 2
    return s[: n - HALO, :]


def _place():
    return lax.axis_index("x"), lax.axis_index("y"), lax.axis_index("c")


def _gather_weights(shards):
    n = len(shards)
    piece = [s.shape[0] // 4 for s in shards]

    def body(*refs):
        wide, dsts, srcs = refs[:n], refs[n:2 * n], refs[2 * n:3 * n]
        send_sems, recv_sems, local_sems = refs[3 * n:]
        for k in range(n):
            for r0 in range(0, 4 * piece[k], CHUNK):
                srcs[k][r0:r0 + CHUNK, :] = wide[k][r0:r0 + CHUNK, :].astype(BF16)
        x, y, c = _place()
        me, sibling = (x, y, c), (x, y, 1 - c)
        across_x, across_y = (1 - x, y, c), (x, 1 - y, c)
        q, qx, qy, qf = 2 * x + y, 2 * (1 - x) + y, 2 * x + (1 - y), 2 * (1 - x) + (1 - y)

        def rows(ref, cc, p, k):
            return ref.at[pl.ds((2 * cc + p) * piece[k], piece[k])]

        def copy(k, sem, qq, cc, p, to, own=False):
            landing = rows(dsts[k].at[qq], cc, p, k)
            return pltpu.make_async_remote_copy(
                src_ref=rows(srcs[k], cc, p, k) if own else landing, dst_ref=landing,
                send_sem=send_sems.at[12 * k + sem], recv_sem=recv_sems.at[12 * k + sem],
                device_id=to, device_id_type=MESH)

        started = []

        def go(cp):
            cp.start()
            started.append(cp)

        mine = [pltpu.make_async_copy(srcs[k], dsts[k].at[q], local_sems.at[k]) for k in range(n)]
        for cp in mine:
            cp.start()
        for k in range(n):
            for p in range(2):
                go(copy(k, p, q, c, p, across_x, own=True))
                go(copy(k, 2 + p, q, c, p, across_y, own=True))
        for k in range(n):
            copy(k, 0, qx, c, 0, me).wait_recv()
            go(copy(k, 4, qx, c, 0, across_y))
            go(copy(k, 6, qx, c, 0, sibling))
            copy(k, 3, qy, c, 1, me).wait_recv()
            go(copy(k, 5, qy, c, 1, across_x))
            go(copy(k, 9, qy, c, 1, sibling))
        for k in range(n):
            copy(k, 1, qx, c, 1, me).wait_recv()
            go(copy(k, 7, qx, c, 1, sibling))
            copy(k, 2, qy, c, 0, me).wait_recv()
            go(copy(k, 8, qy, c, 0, sibling))
        for k in range(n):
            copy(k, 4, qf, c, 0, me).wait_recv()
            go(copy(k, 10, qf, c, 0, sibling))
            copy(k, 5, qf, c, 1, me).wait_recv()
            go(copy(k, 11, qf, c, 1, sibling))
        for k in range(n):
            for sem, qq, p in ((6, qx, 0), (7, qx, 1), (8, qy, 0), (9, qy, 1), (10, qf, 0), (11, qf, 1)):
                copy(k, sem, qq, 1 - c, p, me).wait_recv()
        for cp in started:
            cp.wait_send()
        for cp in mine:
            cp.wait()

    return pl.pallas_call(
        body, name="gather_weights",
        out_shape=[jax.ShapeDtypeStruct((4,) + s.shape, BF16) for s in shards],
        in_specs=[pl.BlockSpec(memory_space=pltpu.VMEM)] * n, out_specs=[ANY] * n,
        scratch_shapes=[pltpu.VMEM(s.shape, BF16) for s in shards]
        + [pltpu.SemaphoreType.DMA((12 * n,)), pltpu.SemaphoreType.DMA((12 * n,)), pltpu.SemaphoreType.DMA((n,))],
        compiler_params=pltpu.CompilerParams(vmem_limit_bytes=VMEM_LIMIT),
    )(*shards)


def _direct_gather_copies(srcs, dsts, send_sems, recv_sems):
    x, y, c = _place()
    q = 2 * x + y
    sends, recvs = [], []
    for k, (src, dst) in enumerate(zip(srcs, dsts)):
        half = src.shape[0] // 2
        for j, chip in enumerate([(1 - x, y), (x, 1 - y), (1 - x, 1 - y)]):
            for core in range(2):
                sends.append(pltpu.make_async_remote_copy(
                    src_ref=src.at[pl.ds(c * half, half)], dst_ref=dst.at[q, pl.ds(c * half, half)],
                    send_sem=send_sems.at[6 * k + 2 * j + core], recv_sem=recv_sems.at[6 * k + 2 * j + c],
                    device_id=(*chip, core), device_id_type=MESH))
                landed = dst.at[2 * chip[0] + chip[1], pl.ds(core * half, half)]
                recvs.append(pltpu.make_async_remote_copy(
                    src_ref=landed, dst_ref=landed, send_sem=send_sems.at[6 * k + 2 * j + core],
                    recv_sem=recv_sems.at[6 * k + 2 * j + core], device_id=(x, y, c), device_id_type=MESH))
    return sends, recvs


def _scatter_copies(srcs, dsts, send_sems, recv_sems):
    x, y, c = _place()
    copies = []
    for j, chip in enumerate([(1 - x, y), (x, 1 - y), (1 - x, 1 - y)]):
        for k, (src, dst) in enumerate(zip(srcs, dsts)):
            copies.append(pltpu.make_async_remote_copy(
                src_ref=src.at[2 * chip[0] + chip[1]] if len(src.shape) == 3 else src, dst_ref=dst.at[j],
                send_sem=send_sems.at[3 * k + j], recv_sem=recv_sems.at[3 * k + j],
                device_id=(*chip, c), device_id_type=MESH))
    return copies


def _scatter_shapes(parts):
    return [jax.ShapeDtypeStruct((3,) + (p.shape[1:] if p.ndim == 3 else p.shape), p.dtype) for p in parts]


def _join_halves_with_sibling(halves):
    n = len(halves)

    def body(*refs):
        srcs, dsts = refs[:n], refs[n:2 * n]
        send_sems, recv_sems = refs[2 * n:]
        x, y, c = _place()
        copies = [pltpu.make_async_remote_copy(
            src_ref=srcs[k], dst_ref=dsts[k], send_sem=send_sems.at[k], recv_sem=recv_sems.at[k],
            device_id=(x, y, 1 - c), device_id_type=MESH) for k in range(n)]
        for cp in copies:
            cp.start()
        for cp in copies:
            cp.wait()

    return pl.pallas_call(
        body, name="join_halves",
        out_shape=[jax.ShapeDtypeStruct(h.shape, h.dtype) for h in halves],
        in_specs=[ANY] * n, out_specs=[ANY] * n,
        scratch_shapes=[pltpu.SemaphoreType.DMA((n,)), pltpu.SemaphoreType.DMA((n,))],
    )(*halves)


def _row_block(rows, cols, n_arrays):
    cap = max(8, (VMEM_LIMIT // 4) // (8 * n_arrays * cols))
    rb = rows
    while rb > cap and rb % 2 == 0:
        rb //= 2
    return rb


def _scalar(value):
    return jnp.reshape(value, (1,)).astype(jnp.int32)


def _whole(a):
    return pl.BlockSpec(a.shape, lambda i, s_ref: (0,) * a.ndim)


def _sum_fours(parts, slots, wholes, wholes_slots, q):
    n, nw = len(parts), len(wholes)

    def four(own, s):
        return (own[...].astype(F32) + s[0].astype(F32)) + (s[1].astype(F32) + s[2].astype(F32))

    def body(q_ref, *refs):
        p, s, w, ws = refs[:n], refs[n:2 * n], refs[2 * n:2 * n + nw], refs[2 * n + nw:2 * n + 2 * nw]
        o, wo = refs[2 * n + 2 * nw:3 * n + 2 * nw], refs[3 * n + 2 * nw:]
        for k in range(n):
            o[k][...] = four(p[k], s[k])
        for k in range(nw):
            wo[k][...] = four(w[k], ws[k])

    return pl.pallas_call(
        body, name="sum_fours",
        out_shape=[jax.ShapeDtypeStruct(a.shape[1:], F32) for a in parts] + [jax.ShapeDtypeStruct(a.shape, F32) for a in wholes],
        grid_spec=pltpu.PrefetchScalarGridSpec(
            num_scalar_prefetch=1, grid=(2,),
            in_specs=[pl.BlockSpec((None, a.shape[1] // 2, a.shape[2]), lambda i, q_ref: (q_ref[0], i, 0)) for a in parts]
            + [pl.BlockSpec((3, a.shape[1] // 2, a.shape[2]), lambda i, q_ref: (0, i, 0)) for a in slots]
            + [_whole(a) for a in wholes + wholes_slots],
            out_specs=[pl.BlockSpec((a.shape[1] // 2, a.shape[2]), lambda i, q_ref: (i, 0)) for a in parts]
            + [_whole(a) for a in wholes]),
        compiler_params=pltpu.CompilerParams(vmem_limit_bytes=VMEM_LIMIT),
    )(_scalar(q), *parts, *slots, *wholes, *wholes_slots)


def _adamw_math(w, g, m, v):
    nm = ADAM_B1 * m + (1.0 - ADAM_B1) * g
    nv = ADAM_B2 * v + (1.0 - ADAM_B2) * (g * g)
    m_hat = nm / (1.0 - ADAM_B1 ** ADAM_STEP)
    v_hat = nv / (1.0 - ADAM_B2 ** ADAM_STEP)
    return -ADAM_LR * (m_hat / (jnp.sqrt(v_hat) + ADAM_EPS) + ADAM_WD * w), nm, nv


def _adamw(w, g, m, v, name):
    rows, cols = w.shape
    rb = _row_block(rows, cols, 8)

    def body(w_ref, g_ref, m_ref, v_ref, go_ref, d_ref, nm_ref, nv_ref):
        go_ref[...] = g_ref[...]
        d_ref[...], nm_ref[...], nv_ref[...] = _adamw_math(w_ref[...], g_ref[...], m_ref[...], v_ref[...])

    spec = pl.BlockSpec((rb, cols), lambda r: (r, 0))
    out = jax.ShapeDtypeStruct(w.shape, F32)
    return pl.pallas_call(body, name=name, out_shape=[out] * 4, grid=(rows // rb,),
                          in_specs=[spec] * 4, out_specs=[spec] * 4)(w, g, m, v)


def _adamw_joined(ws, g_mine, g_sibling, ms, vs, small, small_grads, c):
    n, ns, ng = len(ws), len(small), len(small_grads)

    def body(c_ref, *refs):
        big, tot, sm = refs[:5 * n], refs[5 * n:5 * n + ng], refs[5 * n + ng:5 * n + ng + 3 * ns]
        outs = refs[5 * n + ng + 3 * ns:]
        mine = c_ref[0] == pl.program_id(0)
        for k in range(n):
            w, gm, gs, m, v = big[5 * k:5 * k + 5]
            g = jnp.where(mine, gm[...], gs[...])
            outs[4 * k][...] = g
            outs[4 * k + 1][...], outs[4 * k + 2][...], outs[4 * k + 3][...] = _adamw_math(w[...], g, m[...], v[...])
        for k in range(ns):
            w, m, v = sm[3 * k:3 * k + 3]
            which, rows, cols = small[k][3]
            g = tot[which][rows, cols]
            o = outs[4 * (n + k):4 * (n + k) + 4]
            o[0][...] = g
            o[1][...], o[2][...], o[3][...] = _adamw_math(w[...], g, m[...], v[...])

    def half(a):
        return pl.BlockSpec((a.shape[0] // 2, a.shape[1]), lambda hf, c_ref: (hf, 0))

    in_specs, operands = [], []
    for k in range(n):
        in_specs += [half(ws[k]), _whole(g_mine[k]), _whole(g_sibling[k]), half(ms[k]), half(vs[k])]
        operands += [ws[k], g_mine[k], g_sibling[k], ms[k], vs[k]]
    operands += list(small_grads) + [a for item in small for a in item[:3]]
    in_specs += [_whole(a) for a in operands[5 * n:]]
    outs = pl.pallas_call(
        body, name="adamw_joined",
        out_shape=[jax.ShapeDtypeStruct(w.shape, F32) for w in ws for _ in range(4)]
        + [jax.ShapeDtypeStruct(item[0].shape, F32) for item in small for _ in range(4)],
        grid_spec=pltpu.PrefetchScalarGridSpec(
            num_scalar_prefetch=1, grid=(2,),
            in_specs=in_specs,
            out_specs=[half(w) for w in ws for _ in range(4)] + [_whole(item[0]) for item in small for _ in range(4)]),
        compiler_params=pltpu.CompilerParams(vmem_limit_bytes=VMEM_LIMIT),
    )(_scalar(c), *operands)
    return [outs[4 * k:4 * k + 4] for k in range(n + ns)]


def _front_forward(x, w_in, pool_w, pool_scale, sgu_g, sgu_b, sgu_wm, sgu_bias_t, later_shards, tm):
    T, D = x.shape
    nq, _, cq = w_in.shape
    G, PG = pool_w.shape[0], pool_w.shape[1]
    nt = T // tm
    bpd = D // PG
    nl = len(later_shards)

    def body(x_ref, win_any, pw_any, ps_ref, lg_ref, lb_ref, sw_ref, sb_ref, *refs):
        shards_any, (keep_ref, y_ref, xt_ref), gathered = refs[:nl], refs[nl:nl + 3], refs[nl + 3:2 * nl + 3]
        wide, narrow = refs[2 * nl + 3:3 * nl + 3], refs[3 * nl + 3:4 * nl + 3]
        win_v, pw_v, carry, sems, load_sems, own_sems, send_sems, recv_sems = refs[4 * nl + 3:]
        i = pl.program_id(0)
        own_quarter = 2 * lax.axis_index("x") + lax.axis_index("y")

        def own_copies():
            return [pltpu.make_async_copy(narrow[k], gathered[k].at[own_quarter], own_sems.at[k]) for k in range(nl)]

        @pl.when(i == 0)
        def _():
            c1 = pltpu.make_async_copy(win_any, win_v, sems.at[0])
            c2 = pltpu.make_async_copy(pw_any, pw_v, sems.at[1])
            loads = [pltpu.make_async_copy(shards_any[k], wide[k], load_sems.at[k]) for k in range(nl)]
            for cp in [c1, c2] + loads:
                cp.start()
            carry[...] = jnp.zeros_like(carry)
            for k in range(nl):
                loads[k].wait()
                for r0 in range(0, wide[k].shape[0], CHUNK):
                    narrow[k][r0:r0 + CHUNK, :] = wide[k][r0:r0 + CHUNK, :].astype(BF16)
            for cp in own_copies() + _direct_gather_copies(narrow, gathered, send_sems, recv_sems)[0]:
                cp.start()
            c1.wait()
            c2.wait()

        xb = x_ref[...].astype(BF16)
        xt_ref[...] = x_ref[...].T.astype(BF16)

        def h_block(j):
            qq, off = divmod(j * PG, cq)
            return _mm(xb, win_v[qq, :, off:off + PG])

        def keep(part, col, value):
            keep_ref[:, SAVED[part] * D + col:SAVED[part] * D + col + PG] = value.astype(BF16)

        def ahead(stage):
            if stage < G:
                return h_block(stage), h_block(3 * bpd + stage)
            if stage < G + bpd:
                hd = stage - G
                return h_block(bpd + hd), h_block(2 * bpd + hd), h_block(4 * bpd + hd)
            return None

        blocks = ahead(0)

        for g, w in enumerate(POOL_WINDOWS):
            sl = slice(g * PG, (g + 1) * PG)
            a, z = blocks
            blocks = ahead(g + 1)
            ext = jnp.concatenate([carry[:, sl], a], axis=0)
            carry[:, sl] = a[tm - HALO:, :]
            pooled = (_causal_window_sum(ext, w) * _inv_count(i * tm, tm, w) - a).astype(BF16)
            mixed = _mm(pooled, pw_v[g])
            keep("pooled", g * PG, pooled)
            keep("z", g * PG, z)
            y_ref[:, sl] = (mixed * ps_ref[:, sl] * (z * jax.nn.sigmoid(z))).astype(BF16)

        for hd in range(bpd):
            sl = slice(hd * PG, (hd + 1) * PG)
            u, v, z = blocks
            blocks = ahead(G + hd + 1)
            ug, dug = _gelu_and_grad(u)
            vg, dvg = _gelu_and_grad(v)
            vhat, rstd = _norm_rows(vg)
            keep("gelu_u", hd * PG, ug)
            keep("dgelu_u", hd * PG, dug)
            keep("vhat", hd * PG, vhat)
            keep("rstd_dgelu_v", hd * PG, rstd * dvg)
            keep("z", D + hd * PG, z)
            vn = (vhat * lg_ref[:, sl] + lb_ref[:, sl]).astype(BF16)
            gated = ug * (z * jax.nn.sigmoid(z))
            for n in range(tm // CHUNK):
                rs = slice(n * CHUNK, (n + 1) * CHUNK)
                sv = _mm(sw_ref[hd], vn[rs, :]) + sb_ref[:, hd:hd + 1]
                y_ref[rs, D + hd * PG:D + (hd + 1) * PG] = (gated[rs, :] * sv).astype(BF16)

        @pl.when(i == nt - 1)
        def _():
            sends, recvs = _direct_gather_copies(narrow, gathered, send_sems, recv_sems)
            for cp in sends:
                cp.wait_send()
            for cp in recvs:
                cp.wait_recv()
            for cp in own_copies():
                cp.wait()

    vec = pl.BlockSpec((1, D), lambda i: (0, 0))
    return pl.pallas_call(
        body, name="front_forward",
        out_shape=[jax.ShapeDtypeStruct((T, SAVED_WIDTH * D), BF16), jax.ShapeDtypeStruct((T, 2 * D), BF16),
                   jax.ShapeDtypeStruct((D, T), BF16)]
        + [jax.ShapeDtypeStruct((4,) + s.shape, BF16) for s in later_shards],
        grid=(nt,),
        in_specs=[pl.BlockSpec((tm, D), lambda i: (i, 0)), ANY, ANY, vec, vec, vec,
                  pl.BlockSpec(sgu_wm.shape, lambda i: (0, 0, 0)), pl.BlockSpec(sgu_bias_t.shape, lambda i: (0, 0))]
        + [ANY] * nl,
        out_specs=[pl.BlockSpec((tm, SAVED_WIDTH * D), lambda i: (i, 0)), pl.BlockSpec((tm, 2 * D), lambda i: (i, 0)),
                   pl.BlockSpec((D, tm), lambda i: (0, i))] + [ANY] * nl,
        scratch_shapes=[pltpu.VMEM(s.shape, F32) for s in later_shards]
        + [pltpu.VMEM(s.shape, BF16) for s in later_shards]
        + [pltpu.VMEM(w_in.shape, BF16), pltpu.VMEM(pool_w.shape, BF16), pltpu.VMEM((HALO, D), F32),
           pltpu.SemaphoreType.DMA((2,)), pltpu.SemaphoreType.DMA((nl,)), pltpu.SemaphoreType.DMA((nl,)),
           pltpu.SemaphoreType.DMA((6 * nl,)), pltpu.SemaphoreType.DMA((6 * nl,))],
        compiler_params=pltpu.CompilerParams(dimension_semantics=("arbitrary",), vmem_limit_bytes=VMEM_LIMIT),
    )(x, w_in, pool_w, pool_scale, sgu_g, sgu_b, sgu_wm, sgu_bias_t, *later_shards)


def _tail(y, x, p, target, w_out, w_gate, w_ple, ln_g, ln_b, gate_b, tm):
    T, D = x.shape
    K = p.shape[1]
    nq, _, cq = w_ple.shape
    nt = T // tm

    def body(y_ref, x_ref, p_ref, t_ref, wout_any, wg_any, wp_any, lng_ref, lnb_ref, bg_ref,
             dxp_ref, dy_ref, dwout_any, dwg_any, dwp_any, dlng_ref, dlnb_ref, dbg_ref, ssq_ref,
             wout_v, wg_v, wp_v, dwout_acc, dwg_acc, dwp_acc, sems):
        i = pl.program_id(0)

        @pl.when(i == 0)
        def _():
            loads = [pltpu.make_async_copy(s, d, sems.at[k])
                     for k, (s, d) in enumerate(((wout_any, wout_v), (wg_any, wg_v), (wp_any, wp_v)))]
            for cp in loads:
                cp.start()
            for ref in (dwout_acc, dwg_acc, dwp_acc, dlng_ref, dlnb_ref, dbg_ref, ssq_ref):
                ref[...] = jnp.zeros_like(ref)
            for cp in loads:
                cp.wait()

        halves = [slice(k * tm // 2, (k + 1) * tm // 2) for k in range(2)]

        def total(parts):
            return sum(jnp.sum(part, axis=0, keepdims=True) for part in parts)

        yb = [y_ref[r, :] for r in halves]
        pb = [p_ref[r, :].astype(BF16) for r in halves]
        mix = [_mm(v, wout_v[...]) for v in yb]
        normed = [_norm_rows(DEEPNORM_ALPHA * x_ref[r, :] + m) for r, m in zip(halves, mix)]
        xhat, rstd = [n[0] for n in normed], [n[1] for n in normed]
        x1 = [v * lng_ref[...] + lnb_ref[...] for v in xhat]
        x1b = [v.astype(BF16) for v in x1]
        gate = [jax.nn.sigmoid(_mm(v, wg_v[...]) + bg_ref[...]) for v in x1b]
        e = [jnp.concatenate([_mm(v, wp_v[qq]) for qq in range(nq)], axis=1) for v in pb]
        diff = [a + g * ee - t_ref[r, :] for a, g, ee, r in zip(x1, gate, e, halves)]
        ssq_ref[...] += total([d * d for d in diff])

        dout = [d * (1.0 / D) for d in diff]
        d_e = [(do * g).astype(BF16) for do, g in zip(dout, gate)]
        dgl = [do * ee * g * (1.0 - g) for do, ee, g in zip(dout, e, gate)]
        dglb = [v.astype(BF16) for v in dgl]
        dbg_ref[...] += total(dgl)
        pb_t, d_e_t, x1b_t, dglb_t = (jnp.concatenate(v, axis=0) for v in (pb, d_e, x1b, dglb))
        for qq in range(nq):
            dwp_acc[qq] += _mm_tn(pb_t, d_e_t[:, qq * cq:(qq + 1) * cq])
        for c0 in range(0, D, MXU_COLS):
            dwg_acc[:, c0:c0 + MXU_COLS] += _mm_tn(x1b_t, dglb_t[:, c0:c0 + MXU_COLS])
        d_x1 = [do + _mm_nt(dg, wg_v[...]) for do, dg in zip(dout, dglb)]
        dlng_ref[...] += total([d * xh for d, xh in zip(d_x1, xhat)])
        dlnb_ref[...] += total(d_x1)
        d_r = [_norm_rows_bwd(d * lng_ref[...], xh, rs) for d, xh, rs in zip(d_x1, xhat, rstd)]
        drb = [v.astype(BF16) for v in d_r]
        for r, v in zip(halves, d_r):
            dxp_ref[r, :] = (DEEPNORM_ALPHA * v).astype(BF16)
        for c0 in range(0, 2 * D, 2 * MXU_COLS):
            for r, v in zip(halves, drb):
                dy_ref[r, c0:c0 + 2 * MXU_COLS] = _mm_nt(v, wout_v[c0:c0 + 2 * MXU_COLS, :]).astype(BF16)

        drb_t = jnp.concatenate(drb, axis=0)
        for c0 in range(0, D, MXU_COLS):
            dwout_acc[:, c0:c0 + MXU_COLS] += _mm_tn(y_ref[...], drb_t[:, c0:c0 + MXU_COLS])

        @pl.when(i == nt - 1)
        def _():
            stores = [pltpu.make_async_copy(s, d, sems.at[k])
                      for k, (s, d) in enumerate(((dwout_acc, dwout_any), (dwg_acc, dwg_any), (dwp_acc, dwp_any)))]
            for cp in stores:
                cp.start()
            for cp in stores:
                cp.wait()

    vec = pl.BlockSpec((1, D), lambda i: (0, 0))
    vec_shape = jax.ShapeDtypeStruct((1, D), F32)

    def tile(cols):
        return pl.BlockSpec((tm, cols), lambda i: (i, 0))

    return pl.pallas_call(
        body, name="tail",
        out_shape=[jax.ShapeDtypeStruct((T, D), BF16), jax.ShapeDtypeStruct((T, 2 * D), BF16),
                   jax.ShapeDtypeStruct(w_out.shape, F32), jax.ShapeDtypeStruct(w_gate.shape, F32),
                   jax.ShapeDtypeStruct(w_ple.shape, F32), vec_shape, vec_shape, vec_shape, vec_shape],
        grid=(nt,),
        in_specs=[tile(2 * D), tile(D), tile(K), tile(D), ANY, ANY, ANY, vec, vec, vec],
        out_specs=[tile(D), tile(2 * D), ANY, ANY, ANY, vec, vec, vec, vec],
        scratch_shapes=[pltpu.VMEM(w_out.shape, BF16), pltpu.VMEM(w_gate.shape, BF16), pltpu.VMEM(w_ple.shape, BF16),
                        pltpu.VMEM(w_out.shape, F32), pltpu.VMEM(w_gate.shape, F32), pltpu.VMEM(w_ple.shape, F32),
                        pltpu.SemaphoreType.DMA((3,))],
        compiler_params=pltpu.CompilerParams(dimension_semantics=("arbitrary",), vmem_limit_bytes=VMEM_LIMIT),
    )(y, x, p, target, w_out, w_gate, w_ple, ln_g, ln_b, gate_b)


def _front_backward(kept, d_y, dx_part, w_in, pool_w, pool_scale, sgu_g, sgu_b, sgu_wm, sgu_bias_t, tm):
    T = kept.shape[0]
    D = kept.shape[1] // SAVED_WIDTH
    nq, _, cq = w_in.shape
    G, PG = pool_w.shape[0], pool_w.shape[1]
    nt = T // tm

    def tile_of(i):
        return nt - 1 - jnp.minimum(i, nt - 1)

    def body(kept_ref, dy_ref, dxp_ref, win_any, pw_ref, ps_ref, lg_ref, lb_ref, sw_ref, sb_ref,
             dh_ref, dx_ref, dpw_ref, dps_ref, dlg_ref, dlb_ref, dsw_ref, dsb_ref, win_v, dh_keep, carry, sems):
        i = pl.program_id(0)
        ti = tile_of(i)

        def saved(part, col, rows=slice(None)):
            return kept_ref[rows, SAVED[part] * D + col:SAVED[part] * D + col + PG]

        @pl.when(i == 0)
        def _():
            cp = pltpu.make_async_copy(win_any, win_v, sems.at[0])
            cp.start()
            carry[...] = jnp.zeros_like(carry)
            for ref in (dpw_ref, dps_ref, dlg_ref, dlb_ref, dsw_ref, dsb_ref):
                ref[...] = jnp.zeros_like(ref)
            cp.wait()

        def dx_columns(r0):
            dx = dxp_ref[:, r0:r0 + MXU_COLS].astype(F32)
            for qq in range(nq):
                dx = dx + _mm_nt(dh_keep[(i + 1) % 2, :, qq * cq:(qq + 1) * cq], win_v[qq, r0:r0 + MXU_COLS, :])
            dx_ref[:, r0:r0 + MXU_COLS] = dx

        dx_chunks = list(range(0, D, MXU_COLS))
        stages = G + D // PG

        def pool_stage(g, w):
            sl = slice(g * PG, (g + 1) * PG)
            pooled = saved("pooled", g * PG)
            mixed = _mm(pooled, pw_ref[g])
            dy = dy_ref[:, sl].astype(F32)
            zs, dzs = _silu_and_grad(saved("z", g * PG).astype(F32))
            d_ypool = dy * zs
            dh_ref[:, 3 * D + g * PG:3 * D + (g + 1) * PG] = (dy * (mixed * ps_ref[:, sl]) * dzs).astype(BF16)
            dps_ref[:, sl] += jnp.sum(d_ypool * mixed, axis=0, keepdims=True)
            d_mixed = (d_ypool * ps_ref[:, sl]).astype(BF16)
            dpw_ref[g] += _mm_tn(pooled, d_mixed)
            d_pooled = _mm_nt(d_mixed, pw_ref[g])
            scaled = d_pooled * _inv_count(ti * tm, tm, w)
            after = jnp.concatenate([scaled, carry[:, sl]], axis=0)
            carry[:, sl] = scaled[:HALO, :]
            dh_ref[:, sl] = (_anticausal_window_sum(after, w) - d_pooled).astype(BF16)

        def gating_stage(hd):
            sl = slice(hd * PG, (hd + 1) * PG)
            vhat = saved("vhat", hd * PG).astype(F32)
            vn = (vhat * lg_ref[:, sl] + lb_ref[:, sl]).astype(BF16)
            d_vn_chunks = []
            for n in range(tm // CHUNK):
                rs = slice(n * CHUNK, (n + 1) * CHUNK)
                sv = _mm(sw_ref[hd], vn[rs, :]) + sb_ref[:, hd:hd + 1]
                ug = saved("gelu_u", hd * PG, rs).astype(F32)
                dy = dy_ref[rs, D + hd * PG:D + (hd + 1) * PG].astype(F32)
                zs, dzs = _silu_and_grad(saved("z", D + hd * PG, rs).astype(F32))
                d_ysgu = dy * zs
                dh_ref[rs, 4 * D + hd * PG:4 * D + (hd + 1) * PG] = (dy * (ug * sv) * dzs).astype(BF16)
                dh_ref[rs, D + hd * PG:D + (hd + 1) * PG] = (
                    d_ysgu * sv * saved("dgelu_u", hd * PG, rs).astype(F32)).astype(BF16)
                d_sv = d_ysgu * ug
                dsb_ref[:, hd:hd + 1] += jnp.sum(d_sv, axis=1, keepdims=True)
                d_svb = d_sv.astype(BF16)
                dsw_ref[hd] += _mm_nt(d_svb, vn[rs, :])
                d_vn_chunks.append(_mm_tn(sw_ref[hd], d_svb))
            d_vn = jnp.concatenate(d_vn_chunks, axis=0)
            dlg_ref[:, sl] += jnp.sum(d_vn * vhat, axis=0, keepdims=True)
            dlb_ref[:, sl] += jnp.sum(d_vn, axis=0, keepdims=True)
            d_vg = _norm_rows_bwd(d_vn * lg_ref[:, sl], vhat, saved("rstd_dgelu_v", hd * PG).astype(F32))
            dh_ref[:, 2 * D + hd * PG:2 * D + (hd + 1) * PG] = d_vg.astype(BF16)

        def work(make_dx, make_dh):
            for stage in range(stages):
                if make_dx:
                    for r0 in dx_chunks[stage * len(dx_chunks) // stages:(stage + 1) * len(dx_chunks) // stages]:
                        dx_columns(r0)
                if make_dh and stage < G:
                    pool_stage(stage, POOL_WINDOWS[stage])
                elif make_dh:
                    gating_stage(stage - G)
            if make_dh:
                dh_keep[i % 2] = dh_ref[...]

        pl.when(i == 0)(functools.partial(work, False, True))
        pl.when((i > 0) & (i < nt))(functools.partial(work, True, True))
        pl.when(i == nt)(functools.partial(work, True, False))

    vec = pl.BlockSpec((1, D), lambda i: (0, 0))
    vec_shape = jax.ShapeDtypeStruct((1, D), F32)

    def whole(shape):
        return pl.BlockSpec(shape, lambda i: (0,) * len(shape))

    return pl.pallas_call(
        body, name="front_backward",
        out_shape=[jax.ShapeDtypeStruct((T, 5 * D), BF16), jax.ShapeDtypeStruct((T, D), F32),
                   jax.ShapeDtypeStruct(pool_w.shape, F32), vec_shape, vec_shape,
                   vec_shape, jax.ShapeDtypeStruct(sgu_wm.shape, F32), jax.ShapeDtypeStruct(sgu_bias_t.shape, F32)],
        grid=(nt + 1,),
        in_specs=[pl.BlockSpec((tm, SAVED_WIDTH * D), lambda i: (tile_of(i), 0)),
                  pl.BlockSpec((tm, 2 * D), lambda i: (tile_of(i), 0)),
                  pl.BlockSpec((tm, D), lambda i: (jnp.minimum(nt - i, nt - 1), 0)), ANY,
                  whole(pool_w.shape), vec, vec, vec, whole(sgu_wm.shape), whole(sgu_bias_t.shape)],
        out_specs=[pl.BlockSpec((tm, 5 * D), lambda i: (tile_of(i), 0)),
                   pl.BlockSpec((tm, D), lambda i: (jnp.minimum(nt - i, nt - 1), 0)),
                   whole(pool_w.shape), vec, vec, vec, whole(sgu_wm.shape), whole(sgu_bias_t.shape)],
        scratch_shapes=[pltpu.VMEM(w_in.shape, BF16), pltpu.VMEM((2, tm, 5 * D), BF16), pltpu.VMEM((HALO, D), F32),
                        pltpu.SemaphoreType.DMA((1,))],
        compiler_params=pltpu.CompilerParams(dimension_semantics=("arbitrary",), vmem_limit_bytes=VMEM_LIMIT),
    )(kept, d_y, dx_part, w_in, pool_w, pool_scale, sgu_g, sgu_b, sgu_wm, sgu_bias_t)


def _weight_backward(d_h, xt, q, scatter_srcs, tm):
    D, T = xt.shape
    cq = d_h.shape[1] // 4
    hr = D // 2
    nt = T // tm
    ns = len(scatter_srcs)

    def body(q_ref, dh_ref, xt_ref, *refs):
        srcs, out_any, dsts = refs[:ns], refs[ns], refs[ns + 1:2 * ns + 1]
        (acc, land_a, send_b, land_b, mine_f, theirs_f,
         a_send, a_recv, b_send, b_recv, j_sems, o_sems, s_send, s_recv) = refs[2 * ns + 1:]
        s, t = pl.program_id(0), pl.program_id(1)
        x_, y_, c = _place()
        sibling = (x_, y_, 1 - c)
        own_rows = pl.ds(pl.multiple_of(c * hr, hr), hr)
        other_rows = pl.ds(pl.multiple_of((1 - c) * hr, hr), hr)

        @pl.when((s == 0) & (t == 0))
        def _():
            for cp in _scatter_copies(srcs, dsts, s_send, s_recv):
                cp.start()

        @pl.when(t == 0)
        def _():
            acc[s % 2] = jnp.zeros((D, cq), F32)

        for c0 in range(0, cq, MXU_COLS):
            acc[s % 2, :, c0:c0 + MXU_COLS] += _mm(xt_ref[...], dh_ref[:, c0:c0 + MXU_COLS])

        def swap(phase):
            return pltpu.make_async_remote_copy(
                src_ref=acc.at[phase % 2, other_rows], dst_ref=land_a.at[phase % 2], send_sem=a_send.at[phase],
                recv_sem=a_recv.at[phase], device_id=sibling, device_id_type=MESH)

        def pair_sum(phase):
            swap(phase).wait()
            return acc[phase % 2, own_rows, :] + land_a[phase % 2]

        def to_owner(slot):
            flip_x, flip_y = (slot + 1) >> 1, (slot + 1) & 1
            owner = (1 - x_ if flip_x else x_, 1 - y_ if flip_y else y_, c)
            return pltpu.make_async_remote_copy(
                src_ref=send_b.at[slot], dst_ref=land_b.at[slot], send_sem=b_send.at[slot],
                recv_sem=b_recv.at[slot], device_id=owner, device_id_type=MESH)

        for slot in range(3):
            @pl.when((s == slot) & (t == nt - 1))
            def _(slot=slot):
                swap(slot).start()

            @pl.when((s == slot + 1) & (t == 0))
            def _(slot=slot):
                send_b[slot] = pair_sum(slot).astype(BF16)
                to_owner(slot).start()

        @pl.when((s == 3) & (t == nt - 1))
        def _():
            swap(3).start()
            own = pair_sum(3)
            for slot in range(3):
                to_owner(slot).wait_recv()
            mine_f[...] = (own + land_b[0].astype(F32)) + (land_b[1].astype(F32) + land_b[2].astype(F32))
            join = pltpu.make_async_remote_copy(
                src_ref=mine_f, dst_ref=theirs_f, send_sem=j_sems.at[0], recv_sem=j_sems.at[1],
                device_id=sibling, device_id_type=MESH)
            join.start()
            out_mine = pltpu.make_async_copy(mine_f, out_any.at[own_rows], o_sems.at[0])
            out_mine.start()
            join.wait()
            out_theirs = pltpu.make_async_copy(theirs_f, out_any.at[other_rows], o_sems.at[1])
            out_theirs.start()
            for slot in range(3):
                to_owner(slot).wait_send()
            for cp in _scatter_copies(srcs, dsts, s_send, s_recv):
                cp.wait()
            out_mine.wait()
            out_theirs.wait()

    def quarter(s, t, q_ref):
        return (t, jnp.where(s == 3, q_ref[0], q_ref[0] ^ (s + 1)))

    dma = pltpu.SemaphoreType.DMA
    return pl.pallas_call(
        body, name="weight_backward",
        out_shape=[jax.ShapeDtypeStruct((D, cq), F32)] + _scatter_shapes(scatter_srcs),
        grid_spec=pltpu.PrefetchScalarGridSpec(
            num_scalar_prefetch=1, grid=(4, nt),
            in_specs=[pl.BlockSpec((tm, cq), quarter), pl.BlockSpec((D, tm), lambda s, t, q_ref: (0, t))] + [ANY] * ns,
            out_specs=[ANY] * (ns + 1),
            scratch_shapes=[pltpu.VMEM((2, D, cq), F32), pltpu.VMEM((2, hr, cq), F32), pltpu.VMEM((3, hr, cq), BF16),
                            pltpu.VMEM((3, hr, cq), BF16), pltpu.VMEM((hr, cq), F32), pltpu.VMEM((hr, cq), F32),
                            dma((4,)), dma((4,)), dma((3,)), dma((3,)), dma((2,)), dma((2,)), dma((3 * ns,)), dma((3 * ns,))]),
        compiler_params=pltpu.CompilerParams(dimension_semantics=("arbitrary", "arbitrary"),
                                             vmem_limit_bytes=VMEM_LIMIT),
    )(jnp.reshape(q, (1,)).astype(jnp.int32), d_h, xt, *scatter_srcs)


def _pair_reduce(grads, wholes):
    n, nw = len(grads), len(wholes)
    half_shapes = [(4,) + g.shape[2:] for g in grads]

    def body(*refs):
        g_any, w_any, o_any, wo_any = refs[:n], refs[n:n + nw], refs[n + nw:2 * n + nw], refs[2 * n + nw:2 * n + 2 * nw]
        rest = refs[2 * n + 2 * nw:]
        mine, theirs, send, land, out = (rest[k * n:(k + 1) * n] for k in range(5))
        w_v, w_land, w_out = (rest[5 * n + k * nw:5 * n + (k + 1) * nw] for k in range(3))
        load_sems, send_sems, recv_sems, store_sems = rest[5 * n + 3 * nw:]
        x, y, c = _place()
        sibling = (x, y, 1 - c)
        loads = [pltpu.make_async_copy(g_any[k].at[:, 1 - c], theirs[k], load_sems.at[k]) for k in range(n)]
        loads += [pltpu.make_async_copy(g_any[k].at[:, c], mine[k], load_sems.at[n + k]) for k in range(n)]
        loads += [pltpu.make_async_copy(w_any[k], w_v[k], load_sems.at[2 * n + k]) for k in range(nw)]
        for cp in loads:
            cp.start()
        sends = []
        for k in range(n):
            loads[k].wait()
            send[k][...] = theirs[k][...].astype(BF16)
            sends.append(pltpu.make_async_remote_copy(
                src_ref=send[k], dst_ref=land[k], send_sem=send_sems.at[k], recv_sem=recv_sems.at[k],
                device_id=sibling, device_id_type=MESH))
            sends[-1].start()
        for k in range(nw):
            loads[2 * n + k].wait()
            sends.append(pltpu.make_async_remote_copy(
                src_ref=w_v[k], dst_ref=w_land[k], send_sem=send_sems.at[n + k], recv_sem=recv_sems.at[n + k],
                device_id=sibling, device_id_type=MESH))
            sends[-1].start()
        stores = []
        for k in range(n):
            loads[n + k].wait()
            sends[k].wait_recv()
            out[k][...] = (mine[k][...] + land[k][...].astype(F32)).astype(BF16)
            stores.append(pltpu.make_async_copy(out[k], o_any[k], store_sems.at[k]))
            stores[-1].start()
        for k in range(nw):
            sends[n + k].wait_recv()
            w_out[k][...] = w_v[k][...] + w_land[k][...]
            stores.append(pltpu.make_async_copy(w_out[k], wo_any[k], store_sems.at[n + k]))
            stores[-1].start()
        for cp in sends:
            cp.wait_send()
        for cp in stores:
            cp.wait()

    dma = pltpu.SemaphoreType.DMA
    return pl.pallas_call(
        body, name="pair_reduce",
        out_shape=[jax.ShapeDtypeStruct(s, BF16) for s in half_shapes] + [jax.ShapeDtypeStruct(w.shape, F32) for w in wholes],
        in_specs=[ANY] * (n + nw), out_specs=[ANY] * (n + nw),
        scratch_shapes=[pltpu.VMEM(s, F32) for s in half_shapes] * 2 + [pltpu.VMEM(s, BF16) for s in half_shapes] * 3
        + [pltpu.VMEM(w.shape, F32) for w in wholes] * 3
        + [dma((2 * n + nw,)), dma((n + nw,)), dma((n + nw,)), dma((n + nw,))],
        compiler_params=pltpu.CompilerParams(vmem_limit_bytes=VMEM_LIMIT),
    )(*grads, *wholes)


def _token_tile(T, want):
    return math.gcd(T, want)


def kernel(x, p, w_in, pool_w, pool_scale, sgu_ln_g, sgu_ln_b, sgu_w, sgu_b, w_out, ln_g, ln_b, ple_w, ple_gate_w, ple_gate_b, loss_target, m_w_in, m_pool_w, m_pool_scale, m_sgu_ln_g, m_sgu_ln_b, m_sgu_w, m_sgu_b, m_w_out, m_ln_g, m_ln_b, m_ple_w, m_ple_gate_w, m_ple_gate_b, v_w_in, v_pool_w, v_pool_scale, v_sgu_ln_g, v_sgu_ln_b, v_sgu_w, v_sgu_b, v_w_out, v_ln_g, v_ln_b, v_ple_w, v_ple_gate_w, v_ple_gate_b):
    c = lax.axis_index("c")
    T, D = x.shape[1], x.shape[2]
    tm, tm_vpu, tm_acc = _token_tile(T, 512), _token_tile(T, 256), _token_tile(T, 2048)
    x2, p2, tgt = x[0], p[0, 0], loss_target[0]
    G, PGQ, PG = pool_w.shape[1], pool_w.shape[2], pool_w.shape[3]

    w_in_f, pool_f = _gather_weights([w_in[0], pool_w[0].reshape(G * PGQ, PG)])
    pool_f = pool_f.reshape(4, G, PGQ, PG).transpose(1, 0, 2, 3).reshape(G, 4 * PGQ, PG)
    tril = jnp.tril(jnp.ones((CHUNK, CHUNK), dtype=bool))
    sgu_wm = jnp.where(tril[None], sgu_w[0], 0.0).astype(BF16)
    sgu_bias_t = sgu_b[0].T

    kept, y, xt, w_out_f, w_gate_f, w_ple_f = _front_forward(
        x2, w_in_f, pool_f, pool_scale, sgu_ln_g, sgu_ln_b, sgu_wm, sgu_bias_t, [w_out[0], ple_gate_w[0], ple_w[0]],
        tm_vpu)
    w_out_f = w_out_f.reshape(-1, D)
    w_gate_f = w_gate_f.reshape(-1, D)
    (dx_part, d_y, d_w_out, d_w_gate, d_w_ple, d_ln_g, d_ln_b, d_gate_b, ssq) = _tail(
        y, x2, p2, tgt, w_out_f, w_gate_f, w_ple_f, ln_g, ln_b, ple_gate_b, tm)
    d_h, d_x, d_pool_w, d_pool_scale, d_sgu_g, d_sgu_b, d_sgu_w, d_sgu_bias_t = _front_backward(
        kept, d_y, dx_part, w_in_f, pool_f, pool_scale, sgu_ln_g, sgu_ln_b, sgu_wm, sgu_bias_t, tm_vpu)
    grads = [d_w_out.reshape(4, -1, D), d_w_gate.reshape(4, -1, D), d_w_ple,
             d_pool_w.reshape(G, 4, PGQ, PG).transpose(1, 0, 2, 3).reshape(4, G * PGQ, PG)]
    grads = [g.reshape(4, 2, g.shape[1] // 2, g.shape[2]) for g in grads]
    vectors = ["pool_scale", "sgu_ln_g", "sgu_ln_b", "ln_g", "ln_b", "ple_gate_b"]
    rows = [d_pool_scale, d_sgu_g, d_sgu_b, d_ln_g, d_ln_b, d_gate_b,
            jnp.pad(jnp.reshape((0.5 / D) * jnp.sum(ssq), (1, 1)), ((0, 1), (0, D - 1))),
            jnp.pad(d_sgu_bias_t.T, ((0, 4), (0, D - CHUNK)))]
    small = jnp.concatenate(rows, axis=0)
    d_sgu_w = jnp.where(tril[None], d_sgu_w, 0.0).reshape(-1, CHUNK)
    q = 2 * lax.axis_index("x") + lax.axis_index("y")
    *parts, small_chip, sw_chip = _pair_reduce(grads, [small, d_sgu_w])
    d_w_in, *slots, small_slots, sw_slots = _weight_backward(d_h, xt, q, parts + [small_chip, sw_chip], tm_acc)
    *halves, small_total, sw_total = _sum_fours(parts, slots, [small_chip, sw_chip], [small_slots, sw_slots], q)
    sibling_halves = _join_halves_with_sibling(halves)
    loss = small_total[6, 0]

    big_names = ["w_out", "ple_gate_w", "ple_w", "pool_w"]
    given = dict(w_in=(w_in, m_w_in, v_w_in), w_out=(w_out, m_w_out, v_w_out),
                 ple_gate_w=(ple_gate_w, m_ple_gate_w, v_ple_gate_w), ple_w=(ple_w, m_ple_w, v_ple_w),
                 pool_w=(pool_w, m_pool_w, v_pool_w), pool_scale=(pool_scale, m_pool_scale, v_pool_scale),
                 sgu_ln_g=(sgu_ln_g, m_sgu_ln_g, v_sgu_ln_g), sgu_ln_b=(sgu_ln_b, m_sgu_ln_b, v_sgu_ln_b),
                 sgu_w=(sgu_w, m_sgu_w, v_sgu_w), sgu_b=(sgu_b, m_sgu_b, v_sgu_b), ln_g=(ln_g, m_ln_g, v_ln_g),
                 ln_b=(ln_b, m_ln_b, v_ln_b), ple_gate_b=(ple_gate_b, m_ple_gate_b, v_ple_gate_b))
    grad, delta, new_m, new_v = {}, {}, {}, {}
    grad["w_in"], delta["w_in"], new_m["w_in"], new_v["w_in"] = (
        t[None] for t in _adamw(w_in[0], d_w_in, m_w_in[0], v_w_in[0], "adamw_w_in"))
    flat = [(2 * g.shape[0], g.shape[1]) for g in halves]
    small_items = [(*given[n], (0, slice(k, k + 1), slice(None))) for k, n in enumerate(vectors)]
    small_items.append((*(t[0] for t in given["sgu_b"]), (0, slice(8, 8 + sgu_b.shape[1]), slice(0, CHUNK))))
    small_items.append((*(t.reshape(-1, CHUNK) for t in given["sgu_w"]), (1, slice(None), slice(None))))
    outs = _adamw_joined(
        [given[n][0].reshape(f) for n, f in zip(big_names, flat)], halves, sibling_halves,
        [given[n][1].reshape(f) for n, f in zip(big_names, flat)],
        [given[n][2].reshape(f) for n, f in zip(big_names, flat)], small_items, [small_total, sw_total], c)
    for name, four in zip(big_names + vectors + ["sgu_b", "sgu_w"], outs):
        grad[name], delta[name], new_m[name], new_v[name] = (t.reshape(given[name][0].shape) for t in four)

    order = ["w_in", "pool_w", "pool_scale", "sgu_ln_g", "sgu_ln_b", "sgu_w", "sgu_b", "w_out", "ln_g", "ln_b",
             "ple_w", "ple_gate_w", "ple_gate_b"]
    return (loss, d_x[None], *[grad[n] for n in order], *[delta[n] for n in order],
            *[new_m[n] for n in order], *[new_v[n] for n in order])
```

```python
import functools
import math

import jax
import jax.numpy as jnp
from jax import lax
from jax.experimental import pallas as pl
from jax.experimental.pallas import tpu as pltpu

F32, BF16 = jnp.float32, jnp.bfloat16
MESH = pl.DeviceIdType.MESH
ANY = pl.BlockSpec(memory_space=pl.ANY)

POOL_WINDOWS = (2, 4, 8, 16)
HALO = 16
CHUNK = 128
MXU_COLS = 256
LN_EPS = 1e-5
DEEPNORM_ALPHA = 2.0 ** 0.25
ADAM_LR, ADAM_B1, ADAM_B2, ADAM_EPS, ADAM_WD, ADAM_STEP = 1e-3, 0.9, 0.999, 1e-8, 0.01, 10
VMEM_LIMIT = 56 * 1024 * 1024
GELU_K = math.sqrt(2.0 / math.pi)
GELU_C = 0.044715
SAVED = {"pooled": 0, "gelu_u": 1, "dgelu_u": 2, "vhat": 3, "rstd_dgelu_v": 4, "silu": 5, "dsilu": 7}
SAVED_WIDTH = 9


def _mm(a, b):
    return jnp.dot(a, b, preferred_element_type=F32)


def _mm_nt(a, b):
    return lax.dot_general(a, b, (((1,), (1,)), ((), ())), preferred_element_type=F32)


def _mm_tn(a, b):
    return lax.dot_general(a, b, (((0,), (0,)), ((), ())), preferred_element_type=F32)


def _gelu_and_grad(x):
    x2 = x * x
    t = jnp.tanh(x * (GELU_K + (GELU_K * GELU_C) * x2))
    hx = 0.5 * x
    g = hx + hx * t
    dg = (0.5 + 0.5 * t) + (hx - hx * t * t) * (GELU_K + (3.0 * GELU_K * GELU_C) * x2)
    return g, dg


def _silu_and_grad(z):
    sig = jax.nn.sigmoid(z)
    zs = z * sig
    return zs, sig + zs * (1.0 - sig)


def _norm_rows(x):
    mu = jnp.mean(x, axis=-1, keepdims=True)
    xc = x - mu
    var = jnp.mean(xc * xc, axis=-1, keepdims=True)
    rstd = lax.rsqrt(var + LN_EPS)
    return xc * rstd, rstd


def _norm_rows_bwd(dxhat, xhat, rstd):
    m1 = jnp.mean(dxhat, axis=-1, keepdims=True)
    m2 = jnp.mean(dxhat * xhat, axis=-1, keepdims=True)
    return rstd * (dxhat - m1 - xhat * m2)


def _inv_count(row0, rows, w):
    t = row0 + lax.broadcasted_iota(jnp.int32, (rows, 1), 0)
    return 1.0 / jnp.minimum(t + 1, w).astype(F32)


def _causal_window_sum(ext, w):
    s, sh = ext, 1
    while sh < w:
        s = s + pltpu.roll(s, sh, axis=0)
        sh *= 2
    return s[HALO:, :]


def _anticausal_window_sum(ext, w):
    n, s, sh = ext.shape[0], ext, 1
    while sh < w:
        s = s + pltpu.roll(s, n - sh, axis=0)
        sh *= 2
    return s[: n - HALO, :]


def _place():
    return lax.axis_index("x"), lax.axis_index("y"), lax.axis_index("c")


def _gather_weights(shards):
    n = len(shards)
    piece = [s.shape[0] // 4 for s in shards]

    def body(*refs):
        wide, dsts, srcs = refs[:n], refs[n:2 * n], refs[2 * n:3 * n]
        send_sems, recv_sems, local_sems = refs[3 * n:]
        for k in range(n):
            for r0 in range(0, 4 * piece[k], CHUNK):
                srcs[k][r0:r0 + CHUNK, :] = wide[k][r0:r0 + CHUNK, :].astype(BF16)
        x, y, c = _place()
        me, sibling = (x, y, c), (x, y, 1 - c)
        across_x, across_y = (1 - x, y, c), (x, 1 - y, c)
        q, qx, qy, qf = 2 * x + y, 2 * (1 - x) + y, 2 * x + (1 - y), 2 * (1 - x) + (1 - y)

        def rows(ref, cc, p, k):
            return ref.at[pl.ds((2 * cc + p) * piece[k], piece[k])]

        def copy(k, sem, qq, cc, p, to, own=False):
            landing = rows(dsts[k].at[qq], cc, p, k)
            return pltpu.make_async_remote_copy(
                src_ref=rows(srcs[k], cc, p, k) if own else landing, dst_ref=landing,
                send_sem=send_sems.at[12 * k + sem], recv_sem=recv_sems.at[12 * k + sem],
                device_id=to, device_id_type=MESH)

        started = []

        def go(cp):
            cp.start()
            started.append(cp)

        mine = [pltpu.make_async_copy(srcs[k], dsts[k].at[q], local_sems.at[k]) for k in range(n)]
        for cp in mine:
            cp.start()
        for k in range(n):
            for p in range(2):
                go(copy(k, p, q, c, p, across_x, own=True))
                go(copy(k, 2 + p, q, c, p, across_y, own=True))
        for k in range(n):
            copy(k, 0, qx, c, 0, me).wait_recv()
            go(copy(k, 4, qx, c, 0, across_y))
            go(copy(k, 6, qx, c, 0, sibling))
            copy(k, 3, qy, c, 1, me).wait_recv()
            go(copy(k, 5, qy, c, 1, across_x))
            go(copy(k, 9, qy, c, 1, sibling))
        for k in range(n):
            copy(k, 1, qx, c, 1, me).wait_recv()
            go(copy(k, 7, qx, c, 1, sibling))
            copy(k, 2, qy, c, 0, me).wait_recv()
            go(copy(k, 8, qy, c, 0, sibling))
        for k in range(n):
            copy(k, 4, qf, c, 0, me).wait_recv()
            go(copy(k, 10, qf, c, 0, sibling))
            copy(k, 5, qf, c, 1, me).wait_recv()
            go(copy(k, 11, qf, c, 1, sibling))
        for k in range(n):
            for sem, qq, p in ((6, qx, 0), (7, qx, 1), (8, qy, 0), (9, qy, 1), (10, qf, 0), (11, qf, 1)):
                copy(k, sem, qq, 1 - c, p, me).wait_recv()
        for cp in started:
            cp.wait_send()
        for cp in mine:
            cp.wait()

    return pl.pallas_call(
        body, name="gather_weights",
        out_shape=[jax.ShapeDtypeStruct((4,) + s.shape, BF16) for s in shards],
        in_specs=[pl.BlockSpec(memory_space=pltpu.VMEM)] * n, out_specs=[ANY] * n,
        scratch_shapes=[pltpu.VMEM(s.shape, BF16) for s in shards]
        + [pltpu.SemaphoreType.DMA((12 * n,)), pltpu.SemaphoreType.DMA((12 * n,)), pltpu.SemaphoreType.DMA((n,))],
        compiler_params=pltpu.CompilerParams(vmem_limit_bytes=VMEM_LIMIT),
    )(*shards)


def _direct_gather_copies(srcs, dsts, send_sems, recv_sems):
    x, y, c = _place()
    q = 2 * x + y
    sends, recvs = [], []
    for k, (src, dst) in enumerate(zip(srcs, dsts)):
        half = src.shape[0] // 2
        for j, chip in enumerate([(1 - x, y), (x, 1 - y), (1 - x, 1 - y)]):
            for core in range(2):
                sends.append(pltpu.make_async_remote_copy(
                    src_ref=src.at[pl.ds(c * half, half)], dst_ref=dst.at[q, pl.ds(c * half, half)],
                    send_sem=send_sems.at[6 * k + 2 * j + core], recv_sem=recv_sems.at[6 * k + 2 * j + c],
                    device_id=(*chip, core), device_id_type=MESH))
                landed = dst.at[2 * chip[0] + chip[1], pl.ds(core * half, half)]
                recvs.append(pltpu.make_async_remote_copy(
                    src_ref=landed, dst_ref=landed, send_sem=send_sems.at[6 * k + 2 * j + core],
                    recv_sem=recv_sems.at[6 * k + 2 * j + core], device_id=(x, y, c), device_id_type=MESH))
    return sends, recvs


def _scatter_copies(srcs, dsts, send_sems, recv_sems):
    x, y, c = _place()
    copies = []
    for j, chip in enumerate([(1 - x, y), (x, 1 - y), (1 - x, 1 - y)]):
        for k, (src, dst) in enumerate(zip(srcs, dsts)):
            copies.append(pltpu.make_async_remote_copy(
                src_ref=src.at[2 * chip[0] + chip[1]] if len(src.shape) == 3 else src, dst_ref=dst.at[j],
                send_sem=send_sems.at[3 * k + j], recv_sem=recv_sems.at[3 * k + j],
                device_id=(*chip, c), device_id_type=MESH))
    return copies


def _scatter_shapes(parts):
    return [jax.ShapeDtypeStruct((3,) + (p.shape[1:] if p.ndim == 3 else p.shape), p.dtype) for p in parts]


def _join_halves_with_sibling(halves):
    n = len(halves)

    def body(*refs):
        srcs, dsts = refs[:n], refs[n:2 * n]
        send_sems, recv_sems = refs[2 * n:]
        x, y, c = _place()
        copies = [pltpu.make_async_remote_copy(
            src_ref=srcs[k], dst_ref=dsts[k], send_sem=send_sems.at[k], recv_sem=recv_sems.at[k],
            device_id=(x, y, 1 - c), device_id_type=MESH) for k in range(n)]
        for cp in copies:
            cp.start()
        for cp in copies:
            cp.wait()

    return pl.pallas_call(
        body, name="join_halves",
        out_shape=[jax.ShapeDtypeStruct(h.shape, h.dtype) for h in halves],
        in_specs=[ANY] * n, out_specs=[ANY] * n,
        scratch_shapes=[pltpu.SemaphoreType.DMA((n,)), pltpu.SemaphoreType.DMA((n,))],
    )(*halves)


def _row_block(rows, cols, n_arrays):
    cap = max(8, (VMEM_LIMIT // 4) // (8 * n_arrays * cols))
    rb = rows
    while rb > cap and rb % 2 == 0:
        rb //= 2
    return rb


def _scalar(value):
    return jnp.reshape(value, (1,)).astype(jnp.int32)


def _whole(a):
    return pl.BlockSpec(a.shape, lambda i, s_ref: (0,) * a.ndim)


def _sum_fours(parts, slots, wholes, wholes_slots, q):
    n, nw = len(parts), len(wholes)

    def four(own, s):
        return (own[...].astype(F32) + s[0].astype(F32)) + (s[1].astype(F32) + s[2].astype(F32))

    def body(q_ref, *refs):
        p, s, w, ws = refs[:n], refs[n:2 * n], refs[2 * n:2 * n + nw], refs[2 * n + nw:2 * n + 2 * nw]
        o, wo = refs[2 * n + 2 * nw:3 * n + 2 * nw], refs[3 * n + 2 * nw:]
        for k in range(n):
            o[k][...] = four(p[k], s[k])
        for k in range(nw):
            wo[k][...] = four(w[k], ws[k])

    return pl.pallas_call(
        body, name="sum_fours",
        out_shape=[jax.ShapeDtypeStruct(a.shape[1:], F32) for a in parts] + [jax.ShapeDtypeStruct(a.shape, F32) for a in wholes],
        grid_spec=pltpu.PrefetchScalarGridSpec(
            num_scalar_prefetch=1, grid=(2,),
            in_specs=[pl.BlockSpec((None, a.shape[1] // 2, a.shape[2]), lambda i, q_ref: (q_ref[0], i, 0)) for a in parts]
            + [pl.BlockSpec((3, a.shape[1] // 2, a.shape[2]), lambda i, q_ref: (0, i, 0)) for a in slots]
            + [_whole(a) for a in wholes + wholes_slots],
            out_specs=[pl.BlockSpec((a.shape[1] // 2, a.shape[2]), lambda i, q_ref: (i, 0)) for a in parts]
            + [_whole(a) for a in wholes]),
        compiler_params=pltpu.CompilerParams(vmem_limit_bytes=VMEM_LIMIT),
    )(_scalar(q), *parts, *slots, *wholes, *wholes_slots)


def _adamw_math(w, g, m, v):
    nm = ADAM_B1 * m + (1.0 - ADAM_B1) * g
    nv = ADAM_B2 * v + (1.0 - ADAM_B2) * (g * g)
    m_hat = nm / (1.0 - ADAM_B1 ** ADAM_STEP)
    v_hat = nv / (1.0 - ADAM_B2 ** ADAM_STEP)
    return -ADAM_LR * (m_hat / (jnp.sqrt(v_hat) + ADAM_EPS) + ADAM_WD * w), nm, nv


def _adamw(w, g, m, v, name):
    rows, cols = w.shape
    rb = _row_block(rows, cols, 8)

    def body(w_ref, g_ref, m_ref, v_ref, go_ref, d_ref, nm_ref, nv_ref):
        go_ref[...] = g_ref[...]
        d_ref[...], nm_ref[...], nv_ref[...] = _adamw_math(w_ref[...], g_ref[...], m_ref[...], v_ref[...])

    spec = pl.BlockSpec((rb, cols), lambda r: (r, 0))
    out = jax.ShapeDtypeStruct(w.shape, F32)
    return pl.pallas_call(body, name=name, out_shape=[out] * 4, grid=(rows // rb,),
                          in_specs=[spec] * 4, out_specs=[spec] * 4)(w, g, m, v)


def _adamw_joined(ws, g_mine, g_sibling, ms, vs, small, small_grads, c):
    n, ns, ng = len(ws), len(small), len(small_grads)

    def body(c_ref, *refs):
        big, tot, sm = refs[:5 * n], refs[5 * n:5 * n + ng], refs[5 * n + ng:5 * n + ng + 3 * ns]
        outs = refs[5 * n + ng + 3 * ns:]
        mine = c_ref[0] == pl.program_id(0)
        for k in range(n):
            w, gm, gs, m, v = big[5 * k:5 * k + 5]
            g = jnp.where(mine, gm[...], gs[...])
            outs[4 * k][...] = g
            outs[4 * k + 1][...], outs[4 * k + 2][...], outs[4 * k + 3][...] = _adamw_math(w[...], g, m[...], v[...])
        for k in range(ns):
            w, m, v = sm[3 * k:3 * k + 3]
            which, rows, cols = small[k][3]
            g = tot[which][rows, cols]
            o = outs[4 * (n + k):4 * (n + k) + 4]
            o[0][...] = g
            o[1][...], o[2][...], o[3][...] = _adamw_math(w[...], g, m[...], v[...])

    def half(a):
        return pl.BlockSpec((a.shape[0] // 2, a.shape[1]), lambda hf, c_ref: (hf, 0))

    in_specs, operands = [], []
    for k in range(n):
        in_specs += [half(ws[k]), _whole(g_mine[k]), _whole(g_sibling[k]), half(ms[k]), half(vs[k])]
        operands += [ws[k], g_mine[k], g_sibling[k], ms[k], vs[k]]
    operands += list(small_grads) + [a for item in small for a in item[:3]]
    in_specs += [_whole(a) for a in operands[5 * n:]]
    outs = pl.pallas_call(
        body, name="adamw_joined",
        out_shape=[jax.ShapeDtypeStruct(w.shape, F32) for w in ws for _ in range(4)]
        + [jax.ShapeDtypeStruct(item[0].shape, F32) for item in small for _ in range(4)],
        grid_spec=pltpu.PrefetchScalarGridSpec(
            num_scalar_prefetch=1, grid=(2,),
            in_specs=in_specs,
            out_specs=[half(w) for w in ws for _ in range(4)] + [_whole(item[0]) for item in small for _ in range(4)]),
        compiler_params=pltpu.CompilerParams(vmem_limit_bytes=VMEM_LIMIT),
    )(_scalar(c), *operands)
    return [outs[4 * k:4 * k + 4] for k in range(n + ns)]


def _front_forward(x, w_in, pool_w, pool_scale, sgu_g, sgu_b, sgu_wm, sgu_bias_t, later_shards, tm):
    T, D = x.shape
    nq, _, cq = w_in.shape
    G, PG = pool_w.shape[0], pool_w.shape[1]
    nt = T // tm
    bpd = D // PG
    nl = len(later_shards)

    def body(x_ref, win_any, pw_any, ps_ref, lg_ref, lb_ref, sw_ref, sb_ref, *refs):
        shards_any, (keep_ref, y_ref, xt_ref), gathered = refs[:nl], refs[nl:nl + 3], refs[nl + 3:2 * nl + 3]
        wide, narrow = refs[2 * nl + 3:3 * nl + 3], refs[3 * nl + 3:4 * nl + 3]
        win_v, pw_v, carry, sems, load_sems, own_sems, send_sems, recv_sems = refs[4 * nl + 3:]
        i = pl.program_id(0)
        own_quarter = 2 * lax.axis_index("x") + lax.axis_index("y")

        def own_copies():
            return [pltpu.make_async_copy(narrow[k], gathered[k].at[own_quarter], own_sems.at[k]) for k in range(nl)]

        @pl.when(i == 0)
        def _():
            c1 = pltpu.make_async_copy(win_any, win_v, sems.at[0])
            c2 = pltpu.make_async_copy(pw_any, pw_v, sems.at[1])
            loads = [pltpu.make_async_copy(shards_any[k], wide[k], load_sems.at[k]) for k in range(nl)]
            for cp in [c1, c2] + loads:
                cp.start()
            carry[...] = jnp.zeros_like(carry)
            for k in range(nl):
                loads[k].wait()
                for r0 in range(0, wide[k].shape[0], CHUNK):
                    narrow[k][r0:r0 + CHUNK, :] = wide[k][r0:r0 + CHUNK, :].astype(BF16)
            for cp in own_copies() + _direct_gather_copies(narrow, gathered, send_sems, recv_sems)[0]:
                cp.start()
            c1.wait()
            c2.wait()

        xb = x_ref[...].astype(BF16)
        xt_ref[...] = x_ref[...].T.astype(BF16)

        def h_block(j):
            qq, off = divmod(j * PG, cq)
            return _mm(xb, win_v[qq, :, off:off + PG])

        def keep(part, col, value):
            keep_ref[:, SAVED[part] * D + col:SAVED[part] * D + col + PG] = value.astype(BF16)

        def ahead(stage):
            if stage < G:
                return h_block(stage), h_block(3 * bpd + stage)
            if stage < G + bpd:
                hd = stage - G
                return h_block(bpd + hd), h_block(2 * bpd + hd), h_block(4 * bpd + hd)
            return None

        blocks = ahead(0)

        for g, w in enumerate(POOL_WINDOWS):
            sl = slice(g * PG, (g + 1) * PG)
            a, z = blocks
            blocks = ahead(g + 1)
            ext = jnp.concatenate([carry[:, sl], a], axis=0)
            carry[:, sl] = a[tm - HALO:, :]
            pooled = (_causal_window_sum(ext, w) * _inv_count(i * tm, tm, w) - a).astype(BF16)
            mixed = _mm(pooled, pw_v[g])
            zs, dzs = _silu_and_grad(z)
            keep("pooled", g * PG, pooled)
            keep("silu", g * PG, zs)
            keep("dsilu", g * PG, dzs)
            y_ref[:, sl] = (mixed * ps_ref[:, sl] * zs).astype(BF16)

        for hd in range(bpd):
            sl = slice(hd * PG, (hd + 1) * PG)
            u, v, z = blocks
            blocks = ahead(G + hd + 1)
            ug, dug = _gelu_and_grad(u)
            vg, dvg = _gelu_and_grad(v)
            vhat, rstd = _norm_rows(vg)
            zs, dzs = _silu_and_grad(z)
            keep("gelu_u", hd * PG, ug)
            keep("dgelu_u", hd * PG, dug)
            keep("vhat", hd * PG, vhat)
            keep("rstd_dgelu_v", hd * PG, rstd * dvg)
            keep("silu", D + hd * PG, zs)
            keep("dsilu", D + hd * PG, dzs)
            vn = (vhat * lg_ref[:, sl] + lb_ref[:, sl]).astype(BF16)
            gated = ug * zs
            for n in range(tm // CHUNK):
                rs = slice(n * CHUNK, (n + 1) * CHUNK)
                sv = _mm(sw_ref[hd], vn[rs, :]) + sb_ref[:, hd:hd + 1]
                y_ref[rs, D + hd * PG:D + (hd + 1) * PG] = (gated[rs, :] * sv).astype(BF16)

        @pl.when(i == nt - 1)
        def _():
            sends, recvs = _direct_gather_copies(narrow, gathered, send_sems, recv_sems)
            for cp in sends:
                cp.wait_send()
            for cp in recvs:
                cp.wait_recv()
            for cp in own_copies():
                cp.wait()

    vec = pl.BlockSpec((1, D), lambda i: (0, 0))
    return pl.pallas_call(
        body, name="front_forward",
        out_shape=[jax.ShapeDtypeStruct((T, SAVED_WIDTH * D), BF16), jax.ShapeDtypeStruct((T, 2 * D), BF16),
                   jax.ShapeDtypeStruct((D, T), BF16)]
        + [jax.ShapeDtypeStruct((4,) + s.shape, BF16) for s in later_shards],
        grid=(nt,),
        in_specs=[pl.BlockSpec((tm, D), lambda i: (i, 0)), ANY, ANY, vec, vec, vec,
                  pl.BlockSpec(sgu_wm.shape, lambda i: (0, 0, 0)), pl.BlockSpec(sgu_bias_t.shape, lambda i: (0, 0))]
        + [ANY] * nl,
        out_specs=[pl.BlockSpec((tm, SAVED_WIDTH * D), lambda i: (i, 0)), pl.BlockSpec((tm, 2 * D), lambda i: (i, 0)),
                   pl.BlockSpec((D, tm), lambda i: (0, i))] + [ANY] * nl,
        scratch_shapes=[pltpu.VMEM(s.shape, F32) for s in later_shards]
        + [pltpu.VMEM(s.shape, BF16) for s in later_shards]
        + [pltpu.VMEM(w_in.shape, BF16), pltpu.VMEM(pool_w.shape, BF16), pltpu.VMEM((HALO, D), F32),
           pltpu.SemaphoreType.DMA((2,)), pltpu.SemaphoreType.DMA((nl,)), pltpu.SemaphoreType.DMA((nl,)),
           pltpu.SemaphoreType.DMA((6 * nl,)), pltpu.SemaphoreType.DMA((6 * nl,))],
        compiler_params=pltpu.CompilerParams(dimension_semantics=("arbitrary",), vmem_limit_bytes=VMEM_LIMIT),
    )(x, w_in, pool_w, pool_scale, sgu_g, sgu_b, sgu_wm, sgu_bias_t, *later_shards)


def _tail(y, x, p, target, w_out, w_gate, w_ple, ln_g, ln_b, gate_b, tm):
    T, D = x.shape
    K = p.shape[1]
    nq, _, cq = w_ple.shape
    nt = T // tm

    def body(y_ref, x_ref, p_ref, t_ref, wout_any, wg_any, wp_any, lng_ref, lnb_ref, bg_ref,
             dxp_ref, dy_ref, dwout_any, dwg_any, dwp_any, dlng_ref, dlnb_ref, dbg_ref, ssq_ref,
             wout_v, wg_v, wp_v, dwout_acc, dwg_acc, dwp_acc, sems):
        i = pl.program_id(0)

        @pl.when(i == 0)
        def _():
            loads = [pltpu.make_async_copy(s, d, sems.at[k])
                     for k, (s, d) in enumerate(((wout_any, wout_v), (wg_any, wg_v), (wp_any, wp_v)))]
            for cp in loads:
                cp.start()
            for ref in (dwout_acc, dwg_acc, dwp_acc, dlng_ref, dlnb_ref, dbg_ref, ssq_ref):
                ref[...] = jnp.zeros_like(ref)
            for cp in loads:
                cp.wait()

        halves = [slice(k * tm // 2, (k + 1) * tm // 2) for k in range(2)]

        def total(parts):
            return sum(jnp.sum(part, axis=0, keepdims=True) for part in parts)

        yb = [y_ref[r, :] for r in halves]
        pb = [p_ref[r, :].astype(BF16) for r in halves]
        mix = [_mm(v, wout_v[...]) for v in yb]
        normed = [_norm_rows(DEEPNORM_ALPHA * x_ref[r, :] + m) for r, m in zip(halves, mix)]
        xhat, rstd = [n[0] for n in normed], [n[1] for n in normed]
        x1 = [v * lng_ref[...] + lnb_ref[...] for v in xhat]
        x1b = [v.astype(BF16) for v in x1]
        gate = [jax.nn.sigmoid(_mm(v, wg_v[...]) + bg_ref[...]) for v in x1b]
        e = [jnp.concatenate([_mm(v, wp_v[qq]) for qq in range(nq)], axis=1) for v in pb]
        diff = [a + g * ee - t_ref[r, :] for a, g, ee, r in zip(x1, gate, e, halves)]
        ssq_ref[...] += total([d * d for d in diff])

        dout = [d * (1.0 / D) for d in diff]
        d_e = [(do * g).astype(BF16) for do, g in zip(dout, gate)]
        dgl = [do * ee * g * (1.0 - g) for do, ee, g in zip(dout, e, gate)]
        dglb = [v.astype(BF16) for v in dgl]
        dbg_ref[...] += total(dgl)
        pb_t, d_e_t, x1b_t, dglb_t = (jnp.concatenate(v, axis=0) for v in (pb, d_e, x1b, dglb))
        for qq in range(nq):
            dwp_acc[qq] += _mm_tn(pb_t, d_e_t[:, qq * cq:(qq + 1) * cq])
        for c0 in range(0, D, MXU_COLS):
            dwg_acc[:, c0:c0 + MXU_COLS] += _mm_tn(x1b_t, dglb_t[:, c0:c0 + MXU_COLS])
        d_x1 = [do + _mm_nt(dg, wg_v[...]) for do, dg in zip(dout, dglb)]
        dlng_ref[...] += total([d * xh for d, xh in zip(d_x1, xhat)])
        dlnb_ref[...] += total(d_x1)
        d_r = [_norm_rows_bwd(d * lng_ref[...], xh, rs) for d, xh, rs in zip(d_x1, xhat, rstd)]
        drb = [v.astype(BF16) for v in d_r]
        for r, v in zip(halves, d_r):
            dxp_ref[r, :] = DEEPNORM_ALPHA * v
        for c0 in range(0, 2 * D, 2 * MXU_COLS):
            for r, v in zip(halves, drb):
                dy_ref[r, c0:c0 + 2 * MXU_COLS] = _mm_nt(v, wout_v[c0:c0 + 2 * MXU_COLS, :]).astype(BF16)

        drb_t = jnp.concatenate(drb, axis=0)
        for c0 in range(0, D, MXU_COLS):
            dwout_acc[:, c0:c0 + MXU_COLS] += _mm_tn(y_ref[...], drb_t[:, c0:c0 + MXU_COLS])

        @pl.when(i == nt - 1)
        def _():
            stores = [pltpu.make_async_copy(s, d, sems.at[k])
                      for k, (s, d) in enumerate(((dwout_acc, dwout_any), (dwg_acc, dwg_any), (dwp_acc, dwp_any)))]
            for cp in stores:
                cp.start()
            for cp in stores:
                cp.wait()

    vec = pl.BlockSpec((1, D), lambda i: (0, 0))
    vec_shape = jax.ShapeDtypeStruct((1, D), F32)

    def tile(cols):
        return pl.BlockSpec((tm, cols), lambda i: (i, 0))

    return pl.pallas_call(
        body, name="tail",
        out_shape=[jax.ShapeDtypeStruct((T, D), F32), jax.ShapeDtypeStruct((T, 2 * D), BF16),
                   jax.ShapeDtypeStruct(w_out.shape, F32), jax.ShapeDtypeStruct(w_gate.shape, F32),
                   jax.ShapeDtypeStruct(w_ple.shape, F32), vec_shape, vec_shape, vec_shape, vec_shape],
        grid=(nt,),
        in_specs=[tile(2 * D), tile(D), tile(K), tile(D), ANY, ANY, ANY, vec, vec, vec],
        out_specs=[tile(D), tile(2 * D), ANY, ANY, ANY, vec, vec, vec, vec],
        scratch_shapes=[pltpu.VMEM(w_out.shape, BF16), pltpu.VMEM(w_gate.shape, BF16), pltpu.VMEM(w_ple.shape, BF16),
                        pltpu.VMEM(w_out.shape, F32), pltpu.VMEM(w_gate.shape, F32), pltpu.VMEM(w_ple.shape, F32),
                        pltpu.SemaphoreType.DMA((3,))],
        compiler_params=pltpu.CompilerParams(dimension_semantics=("arbitrary",), vmem_limit_bytes=VMEM_LIMIT),
    )(y, x, p, target, w_out, w_gate, w_ple, ln_g, ln_b, gate_b)


def _front_backward(kept, d_y, dx_part, w_in, pool_w, pool_scale, sgu_g, sgu_b, sgu_wm, sgu_bias_t, tm):
    T = kept.shape[0]
    D = kept.shape[1] // SAVED_WIDTH
    nq, _, cq = w_in.shape
    G, PG = pool_w.shape[0], pool_w.shape[1]
    nt = T // tm

    def tile_of(i):
        return nt - 1 - jnp.minimum(i, nt - 1)

    def body(kept_ref, dy_ref, dxp_ref, win_any, pw_ref, ps_ref, lg_ref, lb_ref, sw_ref, sb_ref,
             dh_ref, dx_ref, dpw_ref, dps_ref, dlg_ref, dlb_ref, dsw_ref, dsb_ref, win_v, dh_keep, carry, sems):
        i = pl.program_id(0)
        ti = tile_of(i)

        def saved(part, col, rows=slice(None)):
            return kept_ref[rows, SAVED[part] * D + col:SAVED[part] * D + col + PG]

        @pl.when(i == 0)
        def _():
            cp = pltpu.make_async_copy(win_any, win_v, sems.at[0])
            cp.start()
            carry[...] = jnp.zeros_like(carry)
            for ref in (dpw_ref, dps_ref, dlg_ref, dlb_ref, dsw_ref, dsb_ref):
                ref[...] = jnp.zeros_like(ref)
            cp.wait()

        def dx_columns(r0):
            dx = dxp_ref[:, r0:r0 + MXU_COLS]
            for qq in range(nq):
                dx = dx + _mm_nt(dh_keep[(i + 1) % 2, :, qq * cq:(qq + 1) * cq], win_v[qq, r0:r0 + MXU_COLS, :])
            dx_ref[:, r0:r0 + MXU_COLS] = dx

        dx_chunks = list(range(0, D, MXU_COLS))
        stages = G + D // PG

        def pool_stage(g, w):
            sl = slice(g * PG, (g + 1) * PG)
            pooled = saved("pooled", g * PG)
            mixed = _mm(pooled, pw_ref[g])
            dy = dy_ref[:, sl].astype(F32)
            d_ypool = dy * saved("silu", g * PG).astype(F32)
            dh_ref[:, 3 * D + g * PG:3 * D + (g + 1) * PG] = (
                dy * (mixed * ps_ref[:, sl]) * saved("dsilu", g * PG).astype(F32)).astype(BF16)
            dps_ref[:, sl] += jnp.sum(d_ypool * mixed, axis=0, keepdims=True)
            d_mixed = (d_ypool * ps_ref[:, sl]).astype(BF16)
            dpw_ref[g] += _mm_tn(pooled, d_mixed)
            d_pooled = _mm_nt(d_mixed, pw_ref[g])
            scaled = d_pooled * _inv_count(ti * tm, tm, w)
            after = jnp.concatenate([scaled, carry[:, sl]], axis=0)
            carry[:, sl] = scaled[:HALO, :]
            dh_ref[:, sl] = (_anticausal_window_sum(after, w) - d_pooled).astype(BF16)

        def gating_stage(hd):
            sl = slice(hd * PG, (hd + 1) * PG)
            vhat = saved("vhat", hd * PG).astype(F32)
            vn = (vhat * lg_ref[:, sl] + lb_ref[:, sl]).astype(BF16)
            chunk_rows = [slice(n * CHUNK, (n + 1) * CHUNK) for n in range(tm // CHUNK)]
            vn_wide = jnp.concatenate([vn[rs, :] for rs in chunk_rows], axis=1)
            sv_wide = _mm(sw_ref[hd], vn_wide) + sb_ref[:, hd:hd + 1]
            d_sv_parts = []
            for n, rs in enumerate(chunk_rows):
                sv = sv_wide[:, n * PG:(n + 1) * PG]
                ug = saved("gelu_u", hd * PG, rs).astype(F32)
                dy = dy_ref[rs, D + hd * PG:D + (hd + 1) * PG].astype(F32)
                d_ysgu = dy * saved("silu", D + hd * PG, rs).astype(F32)
                dh_ref[rs, 4 * D + hd * PG:4 * D + (hd + 1) * PG] = (
                    dy * (ug * sv) * saved("dsilu", D + hd * PG, rs).astype(F32)).astype(BF16)
                dh_ref[rs, D + hd * PG:D + (hd + 1) * PG] = (
                    d_ysgu * sv * saved("dgelu_u", hd * PG, rs).astype(F32)).astype(BF16)
                d_sv_parts.append(d_ysgu * ug)
            d_sv_wide = jnp.concatenate(d_sv_parts, axis=1)
            dsb_ref[:, hd:hd + 1] += jnp.sum(d_sv_wide, axis=1, keepdims=True)
            d_svb = d_sv_wide.astype(BF16)
            dsw_ref[hd] += _mm_nt(d_svb, vn_wide)
            d_vn_wide = _mm_tn(sw_ref[hd], d_svb)
            d_vn = jnp.concatenate([d_vn_wide[:, n * PG:(n + 1) * PG] for n in range(len(chunk_rows))], axis=0)
            dlg_ref[:, sl] += jnp.sum(d_vn * vhat, axis=0, keepdims=True)
            dlb_ref[:, sl] += jnp.sum(d_vn, axis=0, keepdims=True)
            d_vg = _norm_rows_bwd(d_vn * lg_ref[:, sl], vhat, saved("rstd_dgelu_v", hd * PG).astype(F32))
            dh_ref[:, 2 * D + hd * PG:2 * D + (hd + 1) * PG] = d_vg.astype(BF16)

        def work(make_dx, make_dh):
            for stage in range(stages):
                if make_dx:
                    for r0 in dx_chunks[stage * len(dx_chunks) // stages:(stage + 1) * len(dx_chunks) // stages]:
                        dx_columns(r0)
                if make_dh and stage < G:
                    pool_stage(stage, POOL_WINDOWS[stage])
                elif make_dh:
                    gating_stage(stage - G)
            if make_dh:
                dh_keep[i % 2] = dh_ref[...]

        pl.when(i == 0)(functools.partial(work, False, True))
        pl.when((i > 0) & (i < nt))(functools.partial(work, True, True))
        pl.when(i == nt)(functools.partial(work, True, False))

    vec = pl.BlockSpec((1, D), lambda i: (0, 0))
    vec_shape = jax.ShapeDtypeStruct((1, D), F32)

    def whole(shape):
        return pl.BlockSpec(shape, lambda i: (0,) * len(shape))

    return pl.pallas_call(
        body, name="front_backward",
        out_shape=[jax.ShapeDtypeStruct((T, 5 * D), BF16), jax.ShapeDtypeStruct((T, D), F32),
                   jax.ShapeDtypeStruct(pool_w.shape, F32), vec_shape, vec_shape,
                   vec_shape, jax.ShapeDtypeStruct(sgu_wm.shape, F32), jax.ShapeDtypeStruct(sgu_bias_t.shape, F32)],
        grid=(nt + 1,),
        in_specs=[pl.BlockSpec((tm, SAVED_WIDTH * D), lambda i: (tile_of(i), 0)),
                  pl.BlockSpec((tm, 2 * D), lambda i: (tile_of(i), 0)),
                  pl.BlockSpec((tm, D), lambda i: (jnp.minimum(nt - i, nt - 1), 0)), ANY,
                  whole(pool_w.shape), vec, vec, vec, whole(sgu_wm.shape), whole(sgu_bias_t.shape)],
        out_specs=[pl.BlockSpec((tm, 5 * D), lambda i: (tile_of(i), 0)),
                   pl.BlockSpec((tm, D), lambda i: (jnp.minimum(nt - i, nt - 1), 0)),
                   whole(pool_w.shape), vec, vec, vec, whole(sgu_wm.shape), whole(sgu_bias_t.shape)],
        scratch_shapes=[pltpu.VMEM(w_in.shape, BF16), pltpu.VMEM((2, tm, 5 * D), BF16), pltpu.VMEM((HALO, D), F32),
                        pltpu.SemaphoreType.DMA((1,))],
        compiler_params=pltpu.CompilerParams(dimension_semantics=("arbitrary",), vmem_limit_bytes=VMEM_LIMIT),
    )(kept, d_y, dx_part, w_in, pool_w, pool_scale, sgu_g, sgu_b, sgu_wm, sgu_bias_t)


def _weight_backward(d_h, xt, q, scatter_srcs, tm):
    D, T = xt.shape
    cq = d_h.shape[1] // 4
    hr = D // 2
    nt = T // tm
    ns = len(scatter_srcs)

    def body(q_ref, dh_ref, xt_ref, *refs):
        srcs, out_any, dsts = refs[:ns], refs[ns], refs[ns + 1:2 * ns + 1]
        (acc, land_a, send_b, land_b, mine_f, theirs_f,
         a_send, a_recv, b_send, b_recv, j_sems, o_sems, s_send, s_recv) = refs[2 * ns + 1:]
        s, t = pl.program_id(0), pl.program_id(1)
        x_, y_, c = _place()
        sibling = (x_, y_, 1 - c)
        own_rows = pl.ds(pl.multiple_of(c * hr, hr), hr)
        other_rows = pl.ds(pl.multiple_of((1 - c) * hr, hr), hr)

        @pl.when((s == 0) & (t == 0))
        def _():
            for cp in _scatter_copies(srcs, dsts, s_send, s_recv):
                cp.start()

        @pl.when(t == 0)
        def _():
            acc[s % 2] = jnp.zeros((D, cq), F32)

        for c0 in range(0, cq, MXU_COLS):
            acc[s % 2, :, c0:c0 + MXU_COLS] += _mm(xt_ref[...], dh_ref[:, c0:c0 + MXU_COLS])

        def swap(phase):
            return pltpu.make_async_remote_copy(
                src_ref=acc.at[phase % 2, other_rows], dst_ref=land_a.at[phase % 2], send_sem=a_send.at[phase],
                recv_sem=a_recv.at[phase], device_id=sibling, device_id_type=MESH)

        def pair_sum(phase):
            swap(phase).wait()
            return acc[phase % 2, own_rows, :] + land_a[phase % 2]

        def to_owner(slot):
            flip_x, flip_y = (slot + 1) >> 1, (slot + 1) & 1
            owner = (1 - x_ if flip_x else x_, 1 - y_ if flip_y else y_, c)
            return pltpu.make_async_remote_copy(
                src_ref=send_b.at[slot], dst_ref=land_b.at[slot], send_sem=b_send.at[slot],
                recv_sem=b_recv.at[slot], device_id=owner, device_id_type=MESH)

        for slot in range(3):
            @pl.when((s == slot) & (t == nt - 1))
            def _(slot=slot):
                swap(slot).start()

            @pl.when((s == slot + 1) & (t == 0))
            def _(slot=slot):
                send_b[slot] = pair_sum(slot).astype(BF16)
                to_owner(slot).start()

        @pl.when((s == 3) & (t == nt - 1))
        def _():
            swap(3).start()
            own = pair_sum(3)
            for slot in range(3):
                to_owner(slot).wait_recv()
            mine_f[...] = (own + land_b[0].astype(F32)) + (land_b[1].astype(F32) + land_b[2].astype(F32))
            join = pltpu.make_async_remote_copy(
                src_ref=mine_f, dst_ref=theirs_f, send_sem=j_sems.at[0], recv_sem=j_sems.at[1],
                device_id=sibling, device_id_type=MESH)
            join.start()
            out_mine = pltpu.make_async_copy(mine_f, out_any.at[own_rows], o_sems.at[0])
            out_mine.start()
            join.wait()
            out_theirs = pltpu.make_async_copy(theirs_f, out_any.at[other_rows], o_sems.at[1])
            out_theirs.start()
            for slot in range(3):
                to_owner(slot).wait_send()
            for cp in _scatter_copies(srcs, dsts, s_send, s_recv):
                cp.wait()
            out_mine.wait()
            out_theirs.wait()

    def quarter(s, t, q_ref):
        return (t, jnp.where(s == 3, q_ref[0], q_ref[0] ^ (s + 1)))

    dma = pltpu.SemaphoreType.DMA
    return pl.pallas_call(
        body, name="weight_backward",
        out_shape=[jax.ShapeDtypeStruct((D, cq), F32)] + _scatter_shapes(scatter_srcs),
        grid_spec=pltpu.PrefetchScalarGridSpec(
            num_scalar_prefetch=1, grid=(4, nt),
            in_specs=[pl.BlockSpec((tm, cq), quarter), pl.BlockSpec((D, tm), lambda s, t, q_ref: (0, t))] + [ANY] * ns,
            out_specs=[ANY] * (ns + 1),
            scratch_shapes=[pltpu.VMEM((2, D, cq), F32), pltpu.VMEM((2, hr, cq), F32), pltpu.VMEM((3, hr, cq), BF16),
                            pltpu.VMEM((3, hr, cq), BF16), pltpu.VMEM((hr, cq), F32), pltpu.VMEM((hr, cq), F32),
                            dma((4,)), dma((4,)), dma((3,)), dma((3,)), dma((2,)), dma((2,)), dma((3 * ns,)), dma((3 * ns,))]),
        compiler_params=pltpu.CompilerParams(dimension_semantics=("arbitrary", "arbitrary"),
                                             vmem_limit_bytes=VMEM_LIMIT),
    )(jnp.reshape(q, (1,)).astype(jnp.int32), d_h, xt, *scatter_srcs)


def _pair_reduce(grads, wholes):
    n, nw = len(grads), len(wholes)
    half_shapes = [(4,) + g.shape[2:] for g in grads]

    def body(*refs):
        g_any, w_any, o_any, wo_any = refs[:n], refs[n:n + nw], refs[n + nw:2 * n + nw], refs[2 * n + nw:2 * n + 2 * nw]
        rest = refs[2 * n + 2 * nw:]
        mine, theirs, send, land, out = (rest[k * n:(k + 1) * n] for k in range(5))
        w_v, w_land, w_out = (rest[5 * n + k * nw:5 * n + (k + 1) * nw] for k in range(3))
        load_sems, send_sems, recv_sems, store_sems = rest[5 * n + 3 * nw:]
        x, y, c = _place()
        sibling = (x, y, 1 - c)
        loads = [pltpu.make_async_copy(g_any[k].at[:, 1 - c], theirs[k], load_sems.at[k]) for k in range(n)]
        loads += [pltpu.make_async_copy(g_any[k].at[:, c], mine[k], load_sems.at[n + k]) for k in range(n)]
        loads += [pltpu.make_async_copy(w_any[k], w_v[k], load_sems.at[2 * n + k]) for k in range(nw)]
        for cp in loads:
            cp.start()
        sends = []
        for k in range(n):
            loads[k].wait()
            send[k][...] = theirs[k][...].astype(BF16)
            sends.append(pltpu.make_async_remote_copy(
                src_ref=send[k], dst_ref=land[k], send_sem=send_sems.at[k], recv_sem=recv_sems.at[k],
                device_id=sibling, device_id_type=MESH))
            sends[-1].start()
        for k in range(nw):
            loads[2 * n + k].wait()
            sends.append(pltpu.make_async_remote_copy(
                src_ref=w_v[k], dst_ref=w_land[k], send_sem=send_sems.at[n + k], recv_sem=recv_sems.at[n + k],
                device_id=sibling, device_id_type=MESH))
            sends[-1].start()
        stores = []
        for k in range(n):
            loads[n + k].wait()
            sends[k].wait_recv()
            out[k][...] = (mine[k][...] + land[k][...].astype(F32)).astype(BF16)
            stores.append(pltpu.make_async_copy(out[k], o_any[k], store_sems.at[k]))
            stores[-1].start()
        for k in range(nw):
            sends[n + k].wait_recv()
            w_out[k][...] = w_v[k][...] + w_land[k][...]
            stores.append(pltpu.make_async_copy(w_out[k], wo_any[k], store_sems.at[n + k]))
            stores[-1].start()
        for cp in sends:
            cp.wait_send()
        for cp in stores:
            cp.wait()

    dma = pltpu.SemaphoreType.DMA
    return pl.pallas_call(
        body, name="pair_reduce",
        out_shape=[jax.ShapeDtypeStruct(s, BF16) for s in half_shapes] + [jax.ShapeDtypeStruct(w.shape, F32) for w in wholes],
        in_specs=[ANY] * (n + nw), out_specs=[ANY] * (n + nw),
        scratch_shapes=[pltpu.VMEM(s, F32) for s in half_shapes] * 2 + [pltpu.VMEM(s, BF16) for s in half_shapes] * 3
        + [pltpu.VMEM(w.shape, F32) for w in wholes] * 3
        + [dma((2 * n + nw,)), dma((n + nw,)), dma((n + nw,)), dma((n + nw,))],
        compiler_params=pltpu.CompilerParams(vmem_limit_bytes=VMEM_LIMIT),
    )(*grads, *wholes)


def _token_tile(T, want):
    return math.gcd(T, want)


def kernel(x, p, w_in, pool_w, pool_scale, sgu_ln_g, sgu_ln_b, sgu_w, sgu_b, w_out, ln_g, ln_b, ple_w, ple_gate_w, ple_gate_b, loss_target, m_w_in, m_pool_w, m_pool_scale, m_sgu_ln_g, m_sgu_ln_b, m_sgu_w, m_sgu_b, m_w_out, m_ln_g, m_ln_b, m_ple_w, m_ple_gate_w, m_ple_gate_b, v_w_in, v_pool_w, v_pool_scale, v_sgu_ln_g, v_sgu_ln_b, v_sgu_w, v_sgu_b, v_w_out, v_ln_g, v_ln_b, v_ple_w, v_ple_gate_w, v_ple_gate_b):
    c = lax.axis_index("c")
    T, D = x.shape[1], x.shape[2]
    tm, tm_vpu, tm_acc = _token_tile(T, 512), _token_tile(T, 256), _token_tile(T, 2048)
    x2, p2, tgt = x[0], p[0, 0], loss_target[0]
    G, PGQ, PG = pool_w.shape[1], pool_w.shape[2], pool_w.shape[3]

    w_in_f, pool_f = _gather_weights([w_in[0], pool_w[0].reshape(G * PGQ, PG)])
    pool_f = pool_f.reshape(4, G, PGQ, PG).transpose(1, 0, 2, 3).reshape(G, 4 * PGQ, PG)
    tril = jnp.tril(jnp.ones((CHUNK, CHUNK), dtype=bool))
    sgu_wm = jnp.where(tril[None], sgu_w[0], 0.0).astype(BF16)
    sgu_bias_t = sgu_b[0].T

    kept, y, xt, w_out_f, w_gate_f, w_ple_f = _front_forward(
        x2, w_in_f, pool_f, pool_scale, sgu_ln_g, sgu_ln_b, sgu_wm, sgu_bias_t, [w_out[0], ple_gate_w[0], ple_w[0]],
        tm_vpu)
    w_out_f = w_out_f.reshape(-1, D)
    w_gate_f = w_gate_f.reshape(-1, D)
    (dx_part, d_y, d_w_out, d_w_gate, d_w_ple, d_ln_g, d_ln_b, d_gate_b, ssq) = _tail(
        y, x2, p2, tgt, w_out_f, w_gate_f, w_ple_f, ln_g, ln_b, ple_gate_b, tm)
    d_h, d_x, d_pool_w, d_pool_scale, d_sgu_g, d_sgu_b, d_sgu_w, d_sgu_bias_t = _front_backward(
        kept, d_y, dx_part, w_in_f, pool_f, pool_scale, sgu_ln_g, sgu_ln_b, sgu_wm, sgu_bias_t, tm_vpu)
    grads = [d_w_out.reshape(4, -1, D), d_w_gate.reshape(4, -1, D), d_w_ple,
             d_pool_w.reshape(G, 4, PGQ, PG).transpose(1, 0, 2, 3).reshape(4, G * PGQ, PG)]
    grads = [g.reshape(4, 2, g.shape[1] // 2, g.shape[2]) for g in grads]
    vectors = ["pool_scale", "sgu_ln_g", "sgu_ln_b", "ln_g", "ln_b", "ple_gate_b"]
    rows = [d_pool_scale, d_sgu_g, d_sgu_b, d_ln_g, d_ln_b, d_gate_b,
            jnp.pad(jnp.reshape((0.5 / D) * jnp.sum(ssq), (1, 1)), ((0, 1), (0, D - 1))),
            jnp.pad(d_sgu_bias_t.T, ((0, 4), (0, D - CHUNK)))]
    small = jnp.concatenate(rows, axis=0)
    d_sgu_w = jnp.where(tril[None], d_sgu_w, 0.0).reshape(-1, CHUNK)
    q = 2 * lax.axis_index("x") + lax.axis_index("y")
    *parts, small_chip, sw_chip = _pair_reduce(grads, [small, d_sgu_w])
    d_w_in, *slots, small_slots, sw_slots = _weight_backward(d_h, xt, q, parts + [small_chip, sw_chip], tm_acc)
    *halves, small_total, sw_total = _sum_fours(parts, slots, [small_chip, sw_chip], [small_slots, sw_slots], q)
    sibling_halves = _join_halves_with_sibling(halves)
    loss = small_total[6, 0]

    big_names = ["w_out", "ple_gate_w", "ple_w", "pool_w"]
    given = dict(w_in=(w_in, m_w_in, v_w_in), w_out=(w_out, m_w_out, v_w_out),
                 ple_gate_w=(ple_gate_w, m_ple_gate_w, v_ple_gate_w), ple_w=(ple_w, m_ple_w, v_ple_w),
                 pool_w=(pool_w, m_pool_w, v_pool_w), pool_scale=(pool_scale, m_pool_scale, v_pool_scale),
                 sgu_ln_g=(sgu_ln_g, m_sgu_ln_g, v_sgu_ln_g), sgu_ln_b=(sgu_ln_b, m_sgu_ln_b, v_sgu_ln_b),
                 sgu_w=(sgu_w, m_sgu_w, v_sgu_w), sgu_b=(sgu_b, m_sgu_b, v_sgu_b), ln_g=(ln_g, m_ln_g, v_ln_g),
                 ln_b=(ln_b, m_ln_b, v_ln_b), ple_gate_b=(ple_gate_b, m_ple_gate_b, v_ple_gate_b))
    grad, delta, new_m, new_v = {}, {}, {}, {}
    grad["w_in"], delta["w_in"], new_m["w_in"], new_v["w_in"] = (
        t[None] for t in _adamw(w_in[0], d_w_in, m_w_in[0], v_w_in[0], "adamw_w_in"))
    flat = [(2 * g.shape[0], g.shape[1]) for g in halves]
    small_items = [(*given[n], (0, slice(k, k + 1), slice(None))) for k, n in enumerate(vectors)]
    small_items.append((*(t[0] for t in given["sgu_b"]), (0, slice(8, 8 + sgu_b.shape[1]), slice(0, CHUNK))))
    small_items.append((*(t.reshape(-1, CHUNK) for t in given["sgu_w"]), (1, slice(None), slice(None))))
    outs = _adamw_joined(
        [given[n][0].reshape(f) for n, f in zip(big_names, flat)], halves, sibling_halves,
        [given[n][1].reshape(f) for n, f in zip(big_names, flat)],
        [given[n][2].reshape(f) for n, f in zip(big_names, flat)], small_items, [small_total, sw_total], c)
    for name, four in zip(big_names + vectors + ["sgu_b", "sgu_w"], outs):
        grad[name], delta[name], new_m[name], new_v[name] = (t.reshape(given[name][0].shape) for t in four)

    order = ["w_in", "pool_w", "pool_scale", "sgu_ln_g", "sgu_ln_b", "sgu_w", "sgu_b", "w_out", "ln_g", "ln_b",
             "ple_w", "ple_gate_w", "ple_gate_b"]
    return (loss, d_x[None], *[grad[n] for n in order], *[delta[n] for n in order],
            *[new_m[n] for n in order], *[new_v[n] for n in order])
```

```python
import functools
import math

import jax
import jax.numpy as jnp
from jax import lax
from jax.experimental import pallas as pl
from jax.experimental.pallas import tpu as pltpu

F32, BF16 = jnp.float32, jnp.bfloat16
MESH = pl.DeviceIdType.MESH
ANY = pl.BlockSpec(memory_space=pl.ANY)

POOL_WINDOWS = (2, 4, 8, 16)
HALO = 16
CHUNK = 128
MXU_COLS = 256
LN_EPS = 1e-5
DEEPNORM_ALPHA = 2.0 ** 0.25
ADAM_LR, ADAM_B1, ADAM_B2, ADAM_EPS, ADAM_WD, ADAM_STEP = 1e-3, 0.9, 0.999, 1e-8, 0.01, 10
VMEM_LIMIT = 56 * 1024 * 1024
GELU_K = math.sqrt(2.0 / math.pi)
GELU_C = 0.044715
SAVED = {"pooled": 0, "gelu_u": 1, "dgelu_u": 2, "vhat": 3, "rstd_dgelu_v": 4, "silu": 5, "dsilu": 7}
SAVED_WIDTH = 9


def _mm(a, b):
    return jnp.dot(a, b, preferred_element_type=F32)


def _mm_nt(a, b):
    return lax.dot_general(a, b, (((1,), (1,)), ((), ())), preferred_element_type=F32)


def _mm_tn(a, b):
    return lax.dot_general(a, b, (((0,), (0,)), ((), ())), preferred_element_type=F32)


def _gelu_and_grad(x):
    x2 = x * x
    t = jnp.tanh(x * (GELU_K + (GELU_K * GELU_C) * x2))
    hx = 0.5 * x
    g = hx + hx * t
    dg = (0.5 + 0.5 * t) + (hx - hx * t * t) * (GELU_K + (3.0 * GELU_K * GELU_C) * x2)
    return g, dg


def _silu_and_grad(z):
    sig = jax.nn.sigmoid(z)
    zs = z * sig
    return zs, sig + zs * (1.0 - sig)


def _norm_rows(x):
    mu = jnp.mean(x, axis=-1, keepdims=True)
    xc = x - mu
    var = jnp.mean(xc * xc, axis=-1, keepdims=True)
    rstd = lax.rsqrt(var + LN_EPS)
    return xc * rstd, rstd


def _norm_rows_bwd(dxhat, xhat, rstd):
    m1 = jnp.mean(dxhat, axis=-1, keepdims=True)
    m2 = jnp.mean(dxhat * xhat, axis=-1, keepdims=True)
    return rstd * (dxhat - m1 - xhat * m2)


def _inv_count(row0, rows, w):
    t = row0 + lax.broadcasted_iota(jnp.int32, (rows, 1), 0)
    return 1.0 / jnp.minimum(t + 1, w).astype(F32)


def _causal_window_sum(ext, w):
    s, sh = ext, 1
    while sh < w:
        s = s + pltpu.roll(s, sh, axis=0)
        sh *= 2
    return s[HALO:, :]


def _anticausal_window_sum(ext, w):
    n, s, sh = ext.shape[0], ext, 1
    while sh < w:
        s = s + pltpu.roll(s, n - sh, axis=0)
        sh *= 2
    return s[: n - HALO, :]


def _place():
    return lax.axis_index("x"), lax.axis_index("y"), lax.axis_index("c")


def _gather_weights(shards):
    n = len(shards)
    piece = [s.shape[0] // 4 for s in shards]

    def body(*refs):
        wide, dsts, srcs = refs[:n], refs[n:2 * n], refs[2 * n:3 * n]
        send_sems, recv_sems, local_sems = refs[3 * n:]
        for k in range(n):
            for r0 in range(0, 4 * piece[k], CHUNK):
                srcs[k][r0:r0 + CHUNK, :] = wide[k][r0:r0 + CHUNK, :].astype(BF16)
        x, y, c = _place()
        me, sibling = (x, y, c), (x, y, 1 - c)
        across_x, across_y = (1 - x, y, c), (x, 1 - y, c)
        q, qx, qy, qf = 2 * x + y, 2 * (1 - x) + y, 2 * x + (1 - y), 2 * (1 - x) + (1 - y)

        def rows(ref, cc, p, k):
            return ref.at[pl.ds((2 * cc + p) * piece[k], piece[k])]

        def copy(k, sem, qq, cc, p, to, own=False):
            landing = rows(dsts[k].at[qq], cc, p, k)
            return pltpu.make_async_remote_copy(
                src_ref=rows(srcs[k], cc, p, k) if own else landing, dst_ref=landing,
                send_sem=send_sems.at[12 * k + sem], recv_sem=recv_sems.at[12 * k + sem],
                device_id=to, device_id_type=MESH)

        started = []

        def go(cp):
            cp.start()
            started.append(cp)

        mine = [pltpu.make_async_copy(srcs[k], dsts[k].at[q], local_sems.at[k]) for k in range(n)]
        for cp in mine:
            cp.start()
        for k in range(n):
            for p in range(2):
                go(copy(k, p, q, c, p, across_x, own=True))
                go(copy(k, 2 + p, q, c, p, across_y, own=True))
        for k in range(n):
            copy(k, 0, qx, c, 0, me).wait_recv()
            go(copy(k, 4, qx, c, 0, across_y))
            go(copy(k, 6, qx, c, 0, sibling))
            copy(k, 3, qy, c, 1, me).wait_recv()
            go(copy(k, 5, qy, c, 1, across_x))
            go(copy(k, 9, qy, c, 1, sibling))
        for k in range(n):
            copy(k, 1, qx, c, 1, me).wait_recv()
            go(copy(k, 7, qx, c, 1, sibling))
            copy(k, 2, qy, c, 0, me).wait_recv()
            go(copy(k, 8, qy, c, 0, sibling))
        for k in range(n):
            copy(k, 4, qf, c, 0, me).wait_recv()
            go(copy(k, 10, qf, c, 0, sibling))
            copy(k, 5, qf, c, 1, me).wait_recv()
            go(copy(k, 11, qf, c, 1, sibling))
        for k in range(n):
            for sem, qq, p in ((6, qx, 0), (7, qx, 1), (8, qy, 0), (9, qy, 1), (10, qf, 0), (11, qf, 1)):
                copy(k, sem, qq, 1 - c, p, me).wait_recv()
        for cp in started:
            cp.wait_send()
        for cp in mine:
            cp.wait()

    return pl.pallas_call(
        body, name="gather_weights",
        out_shape=[jax.ShapeDtypeStruct((4,) + s.shape, BF16) for s in shards],
        in_specs=[pl.BlockSpec(memory_space=pltpu.VMEM)] * n, out_specs=[ANY] * n,
        scratch_shapes=[pltpu.VMEM(s.shape, BF16) for s in shards]
        + [pltpu.SemaphoreType.DMA((12 * n,)), pltpu.SemaphoreType.DMA((12 * n,)), pltpu.SemaphoreType.DMA((n,))],
        compiler_params=pltpu.CompilerParams(vmem_limit_bytes=VMEM_LIMIT),
    )(*shards)


def _direct_gather_copies(srcs, dsts, send_sems, recv_sems):
    x, y, c = _place()
    q = 2 * x + y
    sends, recvs = [], []
    for k, (src, dst) in enumerate(zip(srcs, dsts)):
        half = src.shape[0] // 2
        for j, chip in enumerate([(1 - x, y), (x, 1 - y), (1 - x, 1 - y)]):
            for core in range(2):
                sends.append(pltpu.make_async_remote_copy(
                    src_ref=src.at[pl.ds(c * half, half)], dst_ref=dst.at[q, pl.ds(c * half, half)],
                    send_sem=send_sems.at[6 * k + 2 * j + core], recv_sem=recv_sems.at[6 * k + 2 * j + c],
                    device_id=(*chip, core), device_id_type=MESH))
                landed = dst.at[2 * chip[0] + chip[1], pl.ds(core * half, half)]
                recvs.append(pltpu.make_async_remote_copy(
                    src_ref=landed, dst_ref=landed, send_sem=send_sems.at[6 * k + 2 * j + core],
                    recv_sem=recv_sems.at[6 * k + 2 * j + core], device_id=(x, y, c), device_id_type=MESH))
    return sends, recvs


def _scatter_copies(srcs, dsts, send_sems, recv_sems):
    x, y, c = _place()
    copies = []
    for j, chip in enumerate([(1 - x, y), (x, 1 - y), (1 - x, 1 - y)]):
        for k, (src, dst) in enumerate(zip(srcs, dsts)):
            copies.append(pltpu.make_async_remote_copy(
                src_ref=src.at[2 * chip[0] + chip[1]] if len(src.shape) == 3 else src, dst_ref=dst.at[j],
                send_sem=send_sems.at[3 * k + j], recv_sem=recv_sems.at[3 * k + j],
                device_id=(*chip, c), device_id_type=MESH))
    return copies


def _scatter_shapes(parts):
    return [jax.ShapeDtypeStruct((3,) + (p.shape[1:] if p.ndim == 3 else p.shape), p.dtype) for p in parts]


def _join_halves_with_sibling(halves):
    n = len(halves)

    def body(*refs):
        srcs, dsts = refs[:n], refs[n:2 * n]
        send_sems, recv_sems = refs[2 * n:]
        x, y, c = _place()
        copies = [pltpu.make_async_remote_copy(
            src_ref=srcs[k], dst_ref=dsts[k], send_sem=send_sems.at[k], recv_sem=recv_sems.at[k],
            device_id=(x, y, 1 - c), device_id_type=MESH) for k in range(n)]
        for cp in copies:
            cp.start()
        for cp in copies:
            cp.wait()

    return pl.pallas_call(
        body, name="join_halves",
        out_shape=[jax.ShapeDtypeStruct(h.shape, h.dtype) for h in halves],
        in_specs=[ANY] * n, out_specs=[ANY] * n,
        scratch_shapes=[pltpu.SemaphoreType.DMA((n,)), pltpu.SemaphoreType.DMA((n,))],
    )(*halves)


def _row_block(rows, cols, n_arrays):
    cap = max(8, (VMEM_LIMIT // 2) // (8 * n_arrays * cols))
    rb = rows
    while rb > cap and rb % 2 == 0:
        rb //= 2
    return rb


def _scalar(value):
    return jnp.reshape(value, (1,)).astype(jnp.int32)


def _whole(a):
    return pl.BlockSpec(a.shape, lambda i, s_ref: (0,) * a.ndim)


def _sum_fours(parts, slots, wholes, wholes_slots, q):
    n, nw = len(parts), len(wholes)

    def four(own, s):
        return (own[...].astype(F32) + s[0].astype(F32)) + (s[1].astype(F32) + s[2].astype(F32))

    def body(q_ref, *refs):
        p, s, w, ws = refs[:n], refs[n:2 * n], refs[2 * n:2 * n + nw], refs[2 * n + nw:2 * n + 2 * nw]
        o, wo = refs[2 * n + 2 * nw:3 * n + 2 * nw], refs[3 * n + 2 * nw:]
        for k in range(n):
            o[k][...] = four(p[k], s[k])
        for k in range(nw):
            wo[k][...] = four(w[k], ws[k])

    return pl.pallas_call(
        body, name="sum_fours",
        out_shape=[jax.ShapeDtypeStruct(a.shape[1:], F32) for a in parts] + [jax.ShapeDtypeStruct(a.shape, F32) for a in wholes],
        grid_spec=pltpu.PrefetchScalarGridSpec(
            num_scalar_prefetch=1, grid=(2,),
            in_specs=[pl.BlockSpec((None, a.shape[1] // 2, a.shape[2]), lambda i, q_ref: (q_ref[0], i, 0)) for a in parts]
            + [pl.BlockSpec((3, a.shape[1] // 2, a.shape[2]), lambda i, q_ref: (0, i, 0)) for a in slots]
            + [_whole(a) for a in wholes + wholes_slots],
            out_specs=[pl.BlockSpec((a.shape[1] // 2, a.shape[2]), lambda i, q_ref: (i, 0)) for a in parts]
            + [_whole(a) for a in wholes]),
        compiler_params=pltpu.CompilerParams(vmem_limit_bytes=VMEM_LIMIT),
    )(_scalar(q), *parts, *slots, *wholes, *wholes_slots)


def _adamw_math(w, g, m, v):
    nm = ADAM_B1 * m + (1.0 - ADAM_B1) * g
    nv = ADAM_B2 * v + (1.0 - ADAM_B2) * (g * g)
    m_hat = nm / (1.0 - ADAM_B1 ** ADAM_STEP)
    v_hat = nv / (1.0 - ADAM_B2 ** ADAM_STEP)
    return -ADAM_LR * (m_hat / (jnp.sqrt(v_hat) + ADAM_EPS) + ADAM_WD * w), nm, nv


def _adamw(w, g, m, v, name):
    rows, cols = w.shape
    rb = _row_block(rows, cols, 8)

    def body(w_ref, g_ref, m_ref, v_ref, go_ref, d_ref, nm_ref, nv_ref):
        go_ref[...] = g_ref[...]
        d_ref[...], nm_ref[...], nv_ref[...] = _adamw_math(w_ref[...], g_ref[...], m_ref[...], v_ref[...])

    spec = pl.BlockSpec((rb, cols), lambda r: (r, 0))
    out = jax.ShapeDtypeStruct(w.shape, F32)
    return pl.pallas_call(body, name=name, out_shape=[out] * 4, grid=(rows // rb,),
                          in_specs=[spec] * 4, out_specs=[spec] * 4,
                          compiler_params=pltpu.CompilerParams(vmem_limit_bytes=VMEM_LIMIT))(w, g, m, v)


def _adamw_joined(ws, g_mine, g_sibling, ms, vs, small, small_grads, c):
    n, ns, ng = len(ws), len(small), len(small_grads)

    def body(c_ref, *refs):
        big, tot, sm = refs[:5 * n], refs[5 * n:5 * n + ng], refs[5 * n + ng:5 * n + ng + 3 * ns]
        outs = refs[5 * n + ng + 3 * ns:]
        mine = c_ref[0] == pl.program_id(0)
        for k in range(n):
            w, gm, gs, m, v = big[5 * k:5 * k + 5]
            g = jnp.where(mine, gm[...], gs[...])
            outs[4 * k][...] = g
            outs[4 * k + 1][...], outs[4 * k + 2][...], outs[4 * k + 3][...] = _adamw_math(w[...], g, m[...], v[...])
        for k in range(ns):
            w, m, v = sm[3 * k:3 * k + 3]
            which, rows, cols = small[k][3]
            g = tot[which][rows, cols]
            if small[k][4]:
                i_pos = lax.broadcasted_iota(jnp.int32, g.shape, 0) % CHUNK
                g = jnp.where(lax.broadcasted_iota(jnp.int32, g.shape, 1) <= i_pos, g, 0.0)
            o = outs[4 * (n + k):4 * (n + k) + 4]
            o[0][...] = g
            o[1][...], o[2][...], o[3][...] = _adamw_math(w[...], g, m[...], v[...])

    def half(a):
        return pl.BlockSpec((a.shape[0] // 2, a.shape[1]), lambda hf, c_ref: (hf, 0))

    in_specs, operands = [], []
    for k in range(n):
        in_specs += [half(ws[k]), _whole(g_mine[k]), _whole(g_sibling[k]), half(ms[k]), half(vs[k])]
        operands += [ws[k], g_mine[k], g_sibling[k], ms[k], vs[k]]
    operands += list(small_grads) + [a for item in small for a in item[:3]]
    in_specs += [_whole(a) for a in operands[5 * n:]]
    outs = pl.pallas_call(
        body, name="adamw_joined",
        out_shape=[jax.ShapeDtypeStruct(w.shape, F32) for w in ws for _ in range(4)]
        + [jax.ShapeDtypeStruct(item[0].shape, F32) for item in small for _ in range(4)],
        grid_spec=pltpu.PrefetchScalarGridSpec(
            num_scalar_prefetch=1, grid=(2,),
            in_specs=in_specs,
            out_specs=[half(w) for w in ws for _ in range(4)] + [_whole(item[0]) for item in small for _ in range(4)]),
        compiler_params=pltpu.CompilerParams(vmem_limit_bytes=VMEM_LIMIT),
    )(_scalar(c), *operands)
    return [outs[4 * k:4 * k + 4] for k in range(n + ns)]


def _front_forward(x, w_in, pool_w, pool_scale, sgu_g, sgu_b, sgu_wm, sgu_bias_t, later_shards, tm):
    T, D = x.shape
    nq, _, cq = w_in.shape
    G, PG = pool_w.shape[0], pool_w.shape[1]
    nt = T // tm
    bpd = D // PG
    nl = len(later_shards)

    def body(x_ref, win_any, pw_any, ps_ref, lg_ref, lb_ref, sw_ref, sb_ref, *refs):
        shards_any, (keep_ref, y_ref, xt_ref), gathered = refs[:nl], refs[nl:nl + 3], refs[nl + 3:2 * nl + 3]
        wide, narrow = refs[2 * nl + 3:3 * nl + 3], refs[3 * nl + 3:4 * nl + 3]
        win_v, pw_v, carry, sems, load_sems, own_sems, send_sems, recv_sems = refs[4 * nl + 3:]
        i = pl.program_id(0)
        own_quarter = 2 * lax.axis_index("x") + lax.axis_index("y")

        def own_copies():
            return [pltpu.make_async_copy(narrow[k], gathered[k].at[own_quarter], own_sems.at[k]) for k in range(nl)]

        @pl.when(i == 0)
        def _():
            c1 = pltpu.make_async_copy(win_any, win_v, sems.at[0])
            c2 = pltpu.make_async_copy(pw_any, pw_v, sems.at[1])
            loads = [pltpu.make_async_copy(shards_any[k], wide[k], load_sems.at[k]) for k in range(nl)]
            for cp in [c1, c2] + loads:
                cp.start()
            carry[...] = jnp.zeros_like(carry)
            for k in range(nl):
                loads[k].wait()
                for r0 in range(0, wide[k].shape[0], CHUNK):
                    narrow[k][r0:r0 + CHUNK, :] = wide[k][r0:r0 + CHUNK, :].astype(BF16)
            for cp in own_copies() + _direct_gather_copies(narrow, gathered, send_sems, recv_sems)[0]:
                cp.start()
            c1.wait()
            c2.wait()

        xb = x_ref[...].astype(BF16)
        xt_ref[...] = x_ref[...].T.astype(BF16)

        def h_block(j):
            qq, off = divmod(j * PG, cq)
            return _mm(xb, win_v[qq, :, off:off + PG])

        def keep(part, col, value):
            keep_ref[:, SAVED[part] * D + col:SAVED[part] * D + col + PG] = value.astype(BF16)

        def ahead(stage):
            if stage < G:
                return h_block(stage), h_block(3 * bpd + stage)
            if stage < G + bpd:
                hd = stage - G
                return h_block(bpd + hd), h_block(2 * bpd + hd), h_block(4 * bpd + hd)
            return None

        blocks = ahead(0)

        for g, w in enumerate(POOL_WINDOWS):
            sl = slice(g * PG, (g + 1) * PG)
            a, z = blocks
            blocks = ahead(g + 1)
            ext = jnp.concatenate([carry[:, sl], a], axis=0)
            carry[:, sl] = a[tm - HALO:, :]
            pooled = (_causal_window_sum(ext, w) * _inv_count(i * tm, tm, w) - a).astype(BF16)
            mixed = _mm(pooled, pw_v[g])
            zs, dzs = _silu_and_grad(z)
            keep("pooled", g * PG, pooled)
            keep("silu", g * PG, zs)
            keep("dsilu", g * PG, dzs)
            y_ref[:, sl] = (mixed * ps_ref[:, sl] * zs).astype(BF16)

        for hd in range(bpd):
            sl = slice(hd * PG, (hd + 1) * PG)
            u, v, z = blocks
            blocks = ahead(G + hd + 1)
            ug, dug = _gelu_and_grad(u)
            vg, dvg = _gelu_and_grad(v)
            vhat, rstd = _norm_rows(vg)
            zs, dzs = _silu_and_grad(z)
            keep("gelu_u", hd * PG, ug)
            keep("dgelu_u", hd * PG, dug)
            keep("vhat", hd * PG, vhat)
            keep("rstd_dgelu_v", hd * PG, rstd * dvg)
            keep("silu", D + hd * PG, zs)
            keep("dsilu", D + hd * PG, dzs)
            vn = (vhat * lg_ref[:, sl] + lb_ref[:, sl]).astype(BF16)
            gated = ug * zs
            for n in range(tm // CHUNK):
                rs = slice(n * CHUNK, (n + 1) * CHUNK)
                sv = _mm(sw_ref[hd], vn[rs, :]) + sb_ref[:, hd:hd + 1]
                y_ref[rs, D + hd * PG:D + (hd + 1) * PG] = (gated[rs, :] * sv).astype(BF16)

        @pl.when(i == nt - 1)
        def _():
            sends, recvs = _direct_gather_copies(narrow, gathered, send_sems, recv_sems)
            for cp in sends:
                cp.wait_send()
            for cp in recvs:
                cp.wait_recv()
            for cp in own_copies():
                cp.wait()

    vec = pl.BlockSpec((1, D), lambda i: (0, 0))
    return pl.pallas_call(
        body, name="front_forward",
        out_shape=[jax.ShapeDtypeStruct((T, SAVED_WIDTH * D), BF16), jax.ShapeDtypeStruct((T, 2 * D), BF16),
                   jax.ShapeDtypeStruct((D, T), BF16)]
        + [jax.ShapeDtypeStruct((4,) + s.shape, BF16) for s in later_shards],
        grid=(nt,),
        in_specs=[pl.BlockSpec((tm, D), lambda i: (i, 0)), ANY, ANY, vec, vec, vec,
                  pl.BlockSpec(sgu_wm.shape, lambda i: (0, 0, 0)), pl.BlockSpec(sgu_bias_t.shape, lambda i: (0, 0))]
        + [ANY] * nl,
        out_specs=[pl.BlockSpec((tm, SAVED_WIDTH * D), lambda i: (i, 0)), pl.BlockSpec((tm, 2 * D), lambda i: (i, 0)),
                   pl.BlockSpec((D, tm), lambda i: (0, i))] + [ANY] * nl,
        scratch_shapes=[pltpu.VMEM(s.shape, F32) for s in later_shards]
        + [pltpu.VMEM(s.shape, BF16) for s in later_shards]
        + [pltpu.VMEM(w_in.shape, BF16), pltpu.VMEM(pool_w.shape, BF16), pltpu.VMEM((HALO, D), F32),
           pltpu.SemaphoreType.DMA((2,)), pltpu.SemaphoreType.DMA((nl,)), pltpu.SemaphoreType.DMA((nl,)),
           pltpu.SemaphoreType.DMA((6 * nl,)), pltpu.SemaphoreType.DMA((6 * nl,))],
        compiler_params=pltpu.CompilerParams(dimension_semantics=("arbitrary",), vmem_limit_bytes=VMEM_LIMIT),
    )(x, w_in, pool_w, pool_scale, sgu_g, sgu_b, sgu_wm, sgu_bias_t, *later_shards)


def _tail(y, x, p, target, w_out, w_gate, w_ple, ln_g, ln_b, gate_b, tm):
    T, D = x.shape
    K = p.shape[1]
    nq, _, cq = w_ple.shape
    nt = T // tm

    def body(y_ref, x_ref, p_ref, t_ref, wout_any, wg_any, wp_any, lng_ref, lnb_ref, bg_ref,
             dxp_ref, dy_ref, dwout_any, dwg_any, dwp_any, dlng_ref, dlnb_ref, dbg_ref, ssq_ref,
             wout_v, wg_v, wp_v, dwout_acc, dwg_acc, dwp_acc, sems):
        i = pl.program_id(0)

        @pl.when(i == 0)
        def _():
            loads = [pltpu.make_async_copy(s, d, sems.at[k])
                     for k, (s, d) in enumerate(((wout_any, wout_v), (wg_any, wg_v), (wp_any, wp_v)))]
            for cp in loads:
                cp.start()
            for ref in (dwout_acc, dwg_acc, dwp_acc, dlng_ref, dlnb_ref, dbg_ref, ssq_ref):
                ref[...] = jnp.zeros_like(ref)
            for cp in loads:
                cp.wait()

        halves = [slice(k * tm // 2, (k + 1) * tm // 2) for k in range(2)]

        def total(parts):
            return sum(jnp.sum(part, axis=0, keepdims=True) for part in parts)

        yb = [y_ref[r, :] for r in halves]
        pb = [p_ref[r, :].astype(BF16) for r in halves]
        mix = [_mm(v, wout_v[...]) for v in yb]
        normed = [_norm_rows(DEEPNORM_ALPHA * x_ref[r, :] + m) for r, m in zip(halves, mix)]
        xhat, rstd = [n[0] for n in normed], [n[1] for n in normed]
        x1 = [v * lng_ref[...] + lnb_ref[...] for v in xhat]
        x1b = [v.astype(BF16) for v in x1]
        gate = [jax.nn.sigmoid(_mm(v, wg_v[...]) + bg_ref[...]) for v in x1b]
        e = [jnp.concatenate([_mm(v, wp_v[qq]) for qq in range(nq)], axis=1) for v in pb]
        diff = [a + g * ee - t_ref[r, :] for a, g, ee, r in zip(x1, gate, e, halves)]
        ssq_ref[...] += total([d * d for d in diff])

        dout = [d * (1.0 / D) for d in diff]
        d_e = [(do * g).astype(BF16) for do, g in zip(dout, gate)]
        dgl = [do * ee * g * (1.0 - g) for do, ee, g in zip(dout, e, gate)]
        dglb = [v.astype(BF16) for v in dgl]
        dbg_ref[...] += total(dgl)
        pb_t, d_e_t, x1b_t, dglb_t = (jnp.concatenate(v, axis=0) for v in (pb, d_e, x1b, dglb))
        for qq in range(nq):
            dwp_acc[qq] += _mm_tn(pb_t, d_e_t[:, qq * cq:(qq + 1) * cq])
        for c0 in range(0, D, MXU_COLS):
            dwg_acc[:, c0:c0 + MXU_COLS] += _mm_tn(x1b_t, dglb_t[:, c0:c0 + MXU_COLS])
        d_x1 = [do + _mm_nt(dg, wg_v[...]) for do, dg in zip(dout, dglb)]
        dlng_ref[...] += total([d * xh for d, xh in zip(d_x1, xhat)])
        dlnb_ref[...] += total(d_x1)
        d_r = [_norm_rows_bwd(d * lng_ref[...], xh, rs) for d, xh, rs in zip(d_x1, xhat, rstd)]
        drb = [v.astype(BF16) for v in d_r]
        for r, v in zip(halves, d_r):
            dxp_ref[r, :] = DEEPNORM_ALPHA * v
        for c0 in range(0, 2 * D, 2 * MXU_COLS):
            for r, v in zip(halves, drb):
                dy_ref[r, c0:c0 + 2 * MXU_COLS] = _mm_nt(v, wout_v[c0:c0 + 2 * MXU_COLS, :]).astype(BF16)

        drb_t = jnp.concatenate(drb, axis=0)
        for c0 in range(0, D, MXU_COLS):
            dwout_acc[:, c0:c0 + MXU_COLS] += _mm_tn(y_ref[...], drb_t[:, c0:c0 + MXU_COLS])

        @pl.when(i == nt - 1)
        def _():
            stores = [pltpu.make_async_copy(s, d, sems.at[k])
                      for k, (s, d) in enumerate(((dwout_acc, dwout_any), (dwg_acc, dwg_any), (dwp_acc, dwp_any)))]
            for cp in stores:
                cp.start()
            for cp in stores:
                cp.wait()

    vec = pl.BlockSpec((1, D), lambda i: (0, 0))
    vec_shape = jax.ShapeDtypeStruct((1, D), F32)

    def tile(cols):
        return pl.BlockSpec((tm, cols), lambda i: (i, 0))

    return pl.pallas_call(
        body, name="tail",
        out_shape=[jax.ShapeDtypeStruct((T, D), F32), jax.ShapeDtypeStruct((T, 2 * D), BF16),
                   jax.ShapeDtypeStruct(w_out.shape, F32), jax.ShapeDtypeStruct(w_gate.shape, F32),
                   jax.ShapeDtypeStruct(w_ple.shape, F32), vec_shape, vec_shape, vec_shape, vec_shape],
        grid=(nt,),
        in_specs=[tile(2 * D), tile(D), tile(K), tile(D), ANY, ANY, ANY, vec, vec, vec],
        out_specs=[tile(D), tile(2 * D), ANY, ANY, ANY, vec, vec, vec, vec],
        scratch_shapes=[pltpu.VMEM(w_out.shape, BF16), pltpu.VMEM(w_gate.shape, BF16), pltpu.VMEM(w_ple.shape, BF16),
                        pltpu.VMEM(w_out.shape, F32), pltpu.VMEM(w_gate.shape, F32), pltpu.VMEM(w_ple.shape, F32),
                        pltpu.SemaphoreType.DMA((3,))],
        compiler_params=pltpu.CompilerParams(dimension_semantics=("arbitrary",), vmem_limit_bytes=VMEM_LIMIT),
    )(y, x, p, target, w_out, w_gate, w_ple, ln_g, ln_b, gate_b)


def _front_backward(kept, d_y, dx_part, w_in, pool_w, pool_scale, sgu_g, sgu_b, sgu_wm, sgu_bias_t, tm):
    T = kept.shape[0]
    D = kept.shape[1] // SAVED_WIDTH
    nq, _, cq = w_in.shape
    G, PG = pool_w.shape[0], pool_w.shape[1]
    nt = T // tm

    def tile_of(i):
        return nt - 1 - jnp.minimum(i, nt - 1)

    def body(kept_ref, dy_ref, dxp_ref, win_any, pw_ref, ps_ref, lg_ref, lb_ref, sw_ref, sb_ref,
             dh_ref, dx_ref, dpw_ref, dps_ref, dlg_ref, dlb_ref, dsw_ref, dsb_ref, win_v, dh_keep, carry, sems):
        i = pl.program_id(0)
        ti = tile_of(i)

        def saved(part, col, rows=slice(None)):
            return kept_ref[rows, SAVED[part] * D + col:SAVED[part] * D + col + PG]

        @pl.when(i == 0)
        def _():
            cp = pltpu.make_async_copy(win_any, win_v, sems.at[0])
            cp.start()
            carry[...] = jnp.zeros_like(carry)
            for ref in (dpw_ref, dps_ref, dlg_ref, dlb_ref, dsw_ref, dsb_ref):
                ref[...] = jnp.zeros_like(ref)
            cp.wait()

        def dx_columns(r0):
            dx = dxp_ref[:, r0:r0 + MXU_COLS]
            for qq in range(nq):
                dx = dx + _mm_nt(dh_keep[(i + 1) % 2, :, qq * cq:(qq + 1) * cq], win_v[qq, r0:r0 + MXU_COLS, :])
            dx_ref[:, r0:r0 + MXU_COLS] = dx

        dx_chunks = list(range(0, D, MXU_COLS))
        stages = G + D // PG

        def pool_stage(g, w):
            sl = slice(g * PG, (g + 1) * PG)
            pooled = saved("pooled", g * PG)
            mixed = _mm(pooled, pw_ref[g])
            dy = dy_ref[:, sl].astype(F32)
            d_ypool = dy * saved("silu", g * PG).astype(F32)
            dh_ref[:, 3 * D + g * PG:3 * D + (g + 1) * PG] = (
                dy * (mixed * ps_ref[:, sl]) * saved("dsilu", g * PG).astype(F32)).astype(BF16)
            dps_ref[:, sl] += jnp.sum(d_ypool * mixed, axis=0, keepdims=True)
            d_mixed = (d_ypool * ps_ref[:, sl]).astype(BF16)
            dpw_ref[g] += _mm_tn(pooled, d_mixed)
            d_pooled = _mm_nt(d_mixed, pw_ref[g])
            scaled = d_pooled * _inv_count(ti * tm, tm, w)
            after = jnp.concatenate([scaled, carry[:, sl]], axis=0)
            carry[:, sl] = scaled[:HALO, :]
            dh_ref[:, sl] = (_anticausal_window_sum(after, w) - d_pooled).astype(BF16)

        def gating_stage(hd):
            sl = slice(hd * PG, (hd + 1) * PG)
            vhat = saved("vhat", hd * PG).astype(F32)
            vn = (vhat * lg_ref[:, sl] + lb_ref[:, sl]).astype(BF16)
            chunk_rows = [slice(n * CHUNK, (n + 1) * CHUNK) for n in range(tm // CHUNK)]
            vn_wide = jnp.concatenate([vn[rs, :] for rs in chunk_rows], axis=1)
            sv_wide = _mm(sw_ref[hd], vn_wide) + sb_ref[:, hd:hd + 1]
            d_sv_parts = []
            for n, rs in enumerate(chunk_rows):
                sv = sv_wide[:, n * PG:(n + 1) * PG]
                ug = saved("gelu_u", hd * PG, rs).astype(F32)
                dy = dy_ref[rs, D + hd * PG:D + (hd + 1) * PG].astype(F32)
                d_ysgu = dy * saved("silu", D + hd * PG, rs).astype(F32)
                dh_ref[rs, 4 * D + hd * PG:4 * D + (hd + 1) * PG] = (
                    dy * (ug * sv) * saved("dsilu", D + hd * PG, rs).astype(F32)).astype(BF16)
                dh_ref[rs, D + hd * PG:D + (hd + 1) * PG] = (
                    d_ysgu * sv * saved("dgelu_u", hd * PG, rs).astype(F32)).astype(BF16)
                d_sv_parts.append(d_ysgu * ug)
            d_sv_wide = jnp.concatenate(d_sv_parts, axis=1)
            dsb_ref[:, hd:hd + 1] += jnp.sum(d_sv_wide, axis=1, keepdims=True)
            d_svb = d_sv_wide.astype(BF16)
            dsw_ref[hd] += _mm_nt(d_svb, vn_wide)
            d_vn_wide = _mm_tn(sw_ref[hd], d_svb)
            d_vn = jnp.concatenate([d_vn_wide[:, n * PG:(n + 1) * PG] for n in range(len(chunk_rows))], axis=0)
            dlg_ref[:, sl] += jnp.sum(d_vn * vhat, axis=0, keepdims=True)
            dlb_ref[:, sl] += jnp.sum(d_vn, axis=0, keepdims=True)
            d_vg = _norm_rows_bwd(d_vn * lg_ref[:, sl], vhat, saved("rstd_dgelu_v", hd * PG).astype(F32))
            dh_ref[:, 2 * D + hd * PG:2 * D + (hd + 1) * PG] = d_vg.astype(BF16)

        def work(make_dx, make_dh):
            for stage in range(stages):
                if make_dx:
                    for r0 in dx_chunks[stage * len(dx_chunks) // stages:(stage + 1) * len(dx_chunks) // stages]:
                        dx_columns(r0)
                if make_dh and stage < G:
                    pool_stage(stage, POOL_WINDOWS[stage])
                elif make_dh:
                    gating_stage(stage - G)
            if make_dh:
                dh_keep[i % 2] = dh_ref[...]

        pl.when(i == 0)(functools.partial(work, False, True))
        pl.when((i > 0) & (i < nt))(functools.partial(work, True, True))
        pl.when(i == nt)(functools.partial(work, True, False))

    vec = pl.BlockSpec((1, D), lambda i: (0, 0))
    vec_shape = jax.ShapeDtypeStruct((1, D), F32)

    def whole(shape):
        return pl.BlockSpec(shape, lambda i: (0,) * len(shape))

    return pl.pallas_call(
        body, name="front_backward",
        out_shape=[jax.ShapeDtypeStruct((T, 5 * D), BF16), jax.ShapeDtypeStruct((T, D), F32),
                   jax.ShapeDtypeStruct(pool_w.shape, F32), vec_shape, vec_shape,
                   vec_shape, jax.ShapeDtypeStruct(sgu_wm.shape, F32), jax.ShapeDtypeStruct(sgu_bias_t.shape, F32)],
        grid=(nt + 1,),
        in_specs=[pl.BlockSpec((tm, SAVED_WIDTH * D), lambda i: (tile_of(i), 0)),
                  pl.BlockSpec((tm, 2 * D), lambda i: (tile_of(i), 0)),
                  pl.BlockSpec((tm, D), lambda i: (jnp.minimum(nt - i, nt - 1), 0)), ANY,
                  whole(pool_w.shape), vec, vec, vec, whole(sgu_wm.shape), whole(sgu_bias_t.shape)],
        out_specs=[pl.BlockSpec((tm, 5 * D), lambda i: (tile_of(i), 0)),
                   pl.BlockSpec((tm, D), lambda i: (jnp.minimum(nt - i, nt - 1), 0)),
                   whole(pool_w.shape), vec, vec, vec, whole(sgu_wm.shape), whole(sgu_bias_t.shape)],
        scratch_shapes=[pltpu.VMEM(w_in.shape, BF16), pltpu.VMEM((2, tm, 5 * D), BF16), pltpu.VMEM((HALO, D), F32),
                        pltpu.SemaphoreType.DMA((1,))],
        compiler_params=pltpu.CompilerParams(dimension_semantics=("arbitrary",), vmem_limit_bytes=VMEM_LIMIT),
    )(kept, d_y, dx_part, w_in, pool_w, pool_scale, sgu_g, sgu_b, sgu_wm, sgu_bias_t)


def _weight_backward(d_h, xt, q, scatter_srcs, tm):
    D, T = xt.shape
    cq = d_h.shape[1] // 4
    hr = D // 2
    nt = T // tm
    ns = len(scatter_srcs)

    def body(q_ref, dh_ref, xt_ref, *refs):
        srcs, out_any, dsts = refs[:ns], refs[ns], refs[ns + 1:2 * ns + 1]
        (acc, land_a, send_b, land_b, mine_f, theirs_f,
         a_send, a_recv, b_send, b_recv, j_sems, o_sems, s_send, s_recv) = refs[2 * ns + 1:]
        s, t = pl.program_id(0), pl.program_id(1)
        x_, y_, c = _place()
        sibling = (x_, y_, 1 - c)
        own_rows = pl.ds(pl.multiple_of(c * hr, hr), hr)
        other_rows = pl.ds(pl.multiple_of((1 - c) * hr, hr), hr)

        @pl.when((s == 0) & (t == 0))
        def _():
            for cp in _scatter_copies(srcs, dsts, s_send, s_recv):
                cp.start()

        @pl.when(t == 0)
        def _():
            acc[s % 2] = jnp.zeros((D, cq), F32)

        for c0 in range(0, cq, MXU_COLS):
            acc[s % 2, :, c0:c0 + MXU_COLS] += _mm(xt_ref[...], dh_ref[:, c0:c0 + MXU_COLS])

        def swap(phase):
            return pltpu.make_async_remote_copy(
                src_ref=acc.at[phase % 2, other_rows], dst_ref=land_a.at[phase % 2], send_sem=a_send.at[phase],
                recv_sem=a_recv.at[phase], device_id=sibling, device_id_type=MESH)

        def pair_sum(phase):
            swap(phase).wait()
            return acc[phase % 2, own_rows, :] + land_a[phase % 2]

        def to_owner(slot):
            flip_x, flip_y = (slot + 1) >> 1, (slot + 1) & 1
            owner = (1 - x_ if flip_x else x_, 1 - y_ if flip_y else y_, c)
            return pltpu.make_async_remote_copy(
                src_ref=send_b.at[slot], dst_ref=land_b.at[slot], send_sem=b_send.at[slot],
                recv_sem=b_recv.at[slot], device_id=owner, device_id_type=MESH)

        for slot in range(3):
            @pl.when((s == slot) & (t == nt - 1))
            def _(slot=slot):
                swap(slot).start()

            @pl.when((s == slot + 1) & (t == 0))
            def _(slot=slot):
                send_b[slot] = pair_sum(slot).astype(BF16)
                to_owner(slot).start()

        @pl.when((s == 3) & (t == nt - 1))
        def _():
            swap(3).start()
            own = pair_sum(3)
            for slot in range(3):
                to_owner(slot).wait_recv()
            mine_f[...] = (own + land_b[0].astype(F32)) + (land_b[1].astype(F32) + land_b[2].astype(F32))
            join = pltpu.make_async_remote_copy(
                src_ref=mine_f, dst_ref=theirs_f, send_sem=j_sems.at[0], recv_sem=j_sems.at[1],
                device_id=sibling, device_id_type=MESH)
            join.start()
            out_mine = pltpu.make_async_copy(mine_f, out_any.at[own_rows], o_sems.at[0])
            out_mine.start()
            join.wait()
            out_theirs = pltpu.make_async_copy(theirs_f, out_any.at[other_rows], o_sems.at[1])
            out_theirs.start()
            for slot in range(3):
                to_owner(slot).wait_send()
            for cp in _scatter_copies(srcs, dsts, s_send, s_recv):
                cp.wait()
            out_mine.wait()
            out_theirs.wait()

    def quarter(s, t, q_ref):
        return (t, jnp.where(s == 3, q_ref[0], q_ref[0] ^ (s + 1)))

    dma = pltpu.SemaphoreType.DMA
    return pl.pallas_call(
        body, name="weight_backward",
        out_shape=[jax.ShapeDtypeStruct((D, cq), F32)] + _scatter_shapes(scatter_srcs),
        grid_spec=pltpu.PrefetchScalarGridSpec(
            num_scalar_prefetch=1, grid=(4, nt),
            in_specs=[pl.BlockSpec((tm, cq), quarter), pl.BlockSpec((D, tm), lambda s, t, q_ref: (0, t))] + [ANY] * ns,
            out_specs=[ANY] * (ns + 1),
            scratch_shapes=[pltpu.VMEM((2, D, cq), F32), pltpu.VMEM((2, hr, cq), F32), pltpu.VMEM((3, hr, cq), BF16),
                            pltpu.VMEM((3, hr, cq), BF16), pltpu.VMEM((hr, cq), F32), pltpu.VMEM((hr, cq), F32),
                            dma((4,)), dma((4,)), dma((3,)), dma((3,)), dma((2,)), dma((2,)), dma((3 * ns,)), dma((3 * ns,))]),
        compiler_params=pltpu.CompilerParams(dimension_semantics=("arbitrary", "arbitrary"),
                                             vmem_limit_bytes=VMEM_LIMIT),
    )(jnp.reshape(q, (1,)).astype(jnp.int32), d_h, xt, *scatter_srcs)


def _pair_reduce(grads, wholes):
    n, nw = len(grads), len(wholes)
    half_shapes = [(4,) + g.shape[2:] for g in grads]

    def body(*refs):
        g_any, w_any, o_any, wo_any = refs[:n], refs[n:n + nw], refs[n + nw:2 * n + nw], refs[2 * n + nw:2 * n + 2 * nw]
        rest = refs[2 * n + 2 * nw:]
        mine, theirs, send, land, out = (rest[k * n:(k + 1) * n] for k in range(5))
        w_v, w_land, w_out = (rest[5 * n + k * nw:5 * n + (k + 1) * nw] for k in range(3))
        load_sems, send_sems, recv_sems, store_sems = rest[5 * n + 3 * nw:]
        x, y, c = _place()
        sibling = (x, y, 1 - c)
        loads = [pltpu.make_async_copy(g_any[k].at[:, 1 - c], theirs[k], load_sems.at[k]) for k in range(n)]
        loads += [pltpu.make_async_copy(g_any[k].at[:, c], mine[k], load_sems.at[n + k]) for k in range(n)]
        loads += [pltpu.make_async_copy(w_any[k], w_v[k], load_sems.at[2 * n + k]) for k in range(nw)]
        for cp in loads:
            cp.start()
        sends = []
        for k in range(n):
            loads[k].wait()
            send[k][...] = theirs[k][...].astype(BF16)
            sends.append(pltpu.make_async_remote_copy(
                src_ref=send[k], dst_ref=land[k], send_sem=send_sems.at[k], recv_sem=recv_sems.at[k],
                device_id=sibling, device_id_type=MESH))
            sends[-1].start()
        for k in range(nw):
            loads[2 * n + k].wait()
            sends.append(pltpu.make_async_remote_copy(
                src_ref=w_v[k], dst_ref=w_land[k], send_sem=send_sems.at[n + k], recv_sem=recv_sems.at[n + k],
                device_id=sibling, device_id_type=MESH))
            sends[-1].start()
        stores = []
        for k in range(n):
            loads[n + k].wait()
            sends[k].wait_recv()
            out[k][...] = (mine[k][...] + land[k][...].astype(F32)).astype(BF16)
            stores.append(pltpu.make_async_copy(out[k], o_any[k], store_sems.at[k]))
            stores[-1].start()
        for k in range(nw):
            sends[n + k].wait_recv()
            w_out[k][...] = w_v[k][...] + w_land[k][...]
            stores.append(pltpu.make_async_copy(w_out[k], wo_any[k], store_sems.at[n + k]))
            stores[-1].start()
        for cp in sends:
            cp.wait_send()
        for cp in stores:
            cp.wait()

    dma = pltpu.SemaphoreType.DMA
    return pl.pallas_call(
        body, name="pair_reduce",
        out_shape=[jax.ShapeDtypeStruct(s, BF16) for s in half_shapes] + [jax.ShapeDtypeStruct(w.shape, F32) for w in wholes],
        in_specs=[ANY] * (n + nw), out_specs=[ANY] * (n + nw),
        scratch_shapes=[pltpu.VMEM(s, F32) for s in half_shapes] * 2 + [pltpu.VMEM(s, BF16) for s in half_shapes] * 3
        + [pltpu.VMEM(w.shape, F32) for w in wholes] * 3
        + [dma((2 * n + nw,)), dma((n + nw,)), dma((n + nw,)), dma((n + nw,))],
        compiler_params=pltpu.CompilerParams(vmem_limit_bytes=VMEM_LIMIT),
    )(*grads, *wholes)


def _token_tile(T, want):
    return math.gcd(T, want)


def kernel(x, p, w_in, pool_w, pool_scale, sgu_ln_g, sgu_ln_b, sgu_w, sgu_b, w_out, ln_g, ln_b, ple_w, ple_gate_w, ple_gate_b, loss_target, m_w_in, m_pool_w, m_pool_scale, m_sgu_ln_g, m_sgu_ln_b, m_sgu_w, m_sgu_b, m_w_out, m_ln_g, m_ln_b, m_ple_w, m_ple_gate_w, m_ple_gate_b, v_w_in, v_pool_w, v_pool_scale, v_sgu_ln_g, v_sgu_ln_b, v_sgu_w, v_sgu_b, v_w_out, v_ln_g, v_ln_b, v_ple_w, v_ple_gate_w, v_ple_gate_b):
    c = lax.axis_index("c")
    T, D = x.shape[1], x.shape[2]
    tm, tm_vpu, tm_acc = _token_tile(T, 512), _token_tile(T, 256), _token_tile(T, 2048)
    x2, p2, tgt = x[0], p[0, 0], loss_target[0]
    G, PGQ, PG = pool_w.shape[1], pool_w.shape[2], pool_w.shape[3]

    w_in_f, pool_f = _gather_weights([w_in[0], pool_w[0].reshape(G * PGQ, PG)])
    pool_f = pool_f.reshape(4, G, PGQ, PG).transpose(1, 0, 2, 3).reshape(G, 4 * PGQ, PG)
    tril = jnp.tril(jnp.ones((CHUNK, CHUNK), dtype=bool))
    sgu_wm = jnp.where(tril[None], sgu_w[0], 0.0).astype(BF16)
    sgu_bias_t = sgu_b[0].T

    kept, y, xt, w_out_f, w_gate_f, w_ple_f = _front_forward(
        x2, w_in_f, pool_f, pool_scale, sgu_ln_g, sgu_ln_b, sgu_wm, sgu_bias_t, [w_out[0], ple_gate_w[0], ple_w[0]],
        tm_vpu)
    w_out_f = w_out_f.reshape(-1, D)
    w_gate_f = w_gate_f.reshape(-1, D)
    (dx_part, d_y, d_w_out, d_w_gate, d_w_ple, d_ln_g, d_ln_b, d_gate_b, ssq) = _tail(
        y, x2, p2, tgt, w_out_f, w_gate_f, w_ple_f, ln_g, ln_b, ple_gate_b, tm)
    d_h, d_x, d_pool_w, d_pool_scale, d_sgu_g, d_sgu_b, d_sgu_w, d_sgu_bias_t = _front_backward(
        kept, d_y, dx_part, w_in_f, pool_f, pool_scale, sgu_ln_g, sgu_ln_b, sgu_wm, sgu_bias_t, tm_vpu)
    grads = [d_w_out.reshape(4, -1, D), d_w_gate.reshape(4, -1, D), d_w_ple,
             d_pool_w.reshape(G, 4, PGQ, PG).transpose(1, 0, 2, 3).reshape(4, G * PGQ, PG)]
    grads = [g.reshape(4, 2, g.shape[1] // 2, g.shape[2]) for g in grads]
    vectors = ["pool_scale", "sgu_ln_g", "sgu_ln_b", "ln_g", "ln_b", "ple_gate_b"]
    rows = [d_pool_scale, d_sgu_g, d_sgu_b, d_ln_g, d_ln_b, d_gate_b,
            jnp.pad(jnp.reshape((0.5 / D) * jnp.sum(ssq), (1, 1)), ((0, 1), (0, D - 1))),
            jnp.pad(d_sgu_bias_t.T, ((0, 4), (0, D - CHUNK)))]
    small = jnp.concatenate(rows, axis=0)
    d_sgu_w = d_sgu_w.reshape(-1, CHUNK)
    q = 2 * lax.axis_index("x") + lax.axis_index("y")
    *parts, small_chip, sw_chip = _pair_reduce(grads, [small, d_sgu_w])
    d_w_in, *slots, small_slots, sw_slots = _weight_backward(d_h, xt, q, parts + [small_chip, sw_chip], tm_acc)
    *halves, small_total, sw_total = _sum_fours(parts, slots, [small_chip, sw_chip], [small_slots, sw_slots], q)
    sibling_halves = _join_halves_with_sibling(halves)
    loss = small_total[6, 0]

    big_names = ["w_out", "ple_gate_w", "ple_w", "pool_w"]
    given = dict(w_in=(w_in, m_w_in, v_w_in), w_out=(w_out, m_w_out, v_w_out),
                 ple_gate_w=(ple_gate_w, m_ple_gate_w, v_ple_gate_w), ple_w=(ple_w, m_ple_w, v_ple_w),
                 pool_w=(pool_w, m_pool_w, v_pool_w), pool_scale=(pool_scale, m_pool_scale, v_pool_scale),
                 sgu_ln_g=(sgu_ln_g, m_sgu_ln_g, v_sgu_ln_g), sgu_ln_b=(sgu_ln_b, m_sgu_ln_b, v_sgu_ln_b),
                 sgu_w=(sgu_w, m_sgu_w, v_sgu_w), sgu_b=(sgu_b, m_sgu_b, v_sgu_b), ln_g=(ln_g, m_ln_g, v_ln_g),
                 ln_b=(ln_b, m_ln_b, v_ln_b), ple_gate_b=(ple_gate_b, m_ple_gate_b, v_ple_gate_b))
    grad, delta, new_m, new_v = {}, {}, {}, {}
    grad["w_in"], delta["w_in"], new_m["w_in"], new_v["w_in"] = (
        t[None] for t in _adamw(w_in[0], d_w_in, m_w_in[0], v_w_in[0], "adamw_w_in"))
    flat = [(2 * g.shape[0], g.shape[1]) for g in halves]
    small_items = [(*given[n], (0, slice(k, k + 1), slice(None)), False) for k, n in enumerate(vectors)]
    small_items.append((*(t[0] for t in given["sgu_b"]), (0, slice(8, 8 + sgu_b.shape[1]), slice(0, CHUNK)), False))
    small_items.append((*(t.reshape(-1, CHUNK) for t in given["sgu_w"]), (1, slice(None), slice(None)), True))
    outs = _adamw_joined(
        [given[n][0].reshape(f) for n, f in zip(big_names, flat)], halves, sibling_halves,
        [given[n][1].reshape(f) for n, f in zip(big_names, flat)],
        [given[n][2].reshape(f) for n, f in zip(big_names, flat)], small_items, [small_total, sw_total], c)
    for name, four in zip(big_names + vectors + ["sgu_b", "sgu_w"], outs):
        grad[name], delta[name], new_m[name], new_v[name] = (t.reshape(given[name][0].shape) for t in four)

    order = ["w_in", "pool_w", "pool_scale", "sgu_ln_g", "sgu_ln_b", "sgu_w", "sgu_b", "w_out", "ln_g", "ln_b",
             "ple_w", "ple_gate_w", "ple_gate_b"]
    return (loss, d_x[None], *[grad[n] for n in order], *[delta[n] for n in order],
            *[new_m[n] for n in order], *[new_v[n] for n in order])
```

```python
import functools
import math

import jax
import jax.numpy as jnp
from jax import lax
from jax.experimental import pallas as pl
from jax.experimental.pallas import tpu as pltpu

F32, BF16 = jnp.float32, jnp.bfloat16
MESH = pl.DeviceIdType.MESH
ANY = pl.BlockSpec(memory_space=pl.ANY)

POOL_WINDOWS = (2, 4, 8, 16)
HALO = 16
CHUNK = 128
MXU_COLS = 256
LN_EPS = 1e-5
DEEPNORM_ALPHA = 2.0 ** 0.25
ADAM_LR, ADAM_B1, ADAM_B2, ADAM_EPS, ADAM_WD, ADAM_STEP = 1e-3, 0.9, 0.999, 1e-8, 0.01, 10
VMEM_LIMIT = 56 * 1024 * 1024
GELU_K = math.sqrt(2.0 / math.pi)
GELU_C = 0.044715
SAVED = {"pooled": 0, "gelu_u": 1, "dgelu_u": 2, "vhat": 3, "rstd_dgelu_v": 4, "silu": 5, "dsilu": 7}
SAVED_WIDTH = 9


def _mm(a, b):
    return jnp.dot(a, b, preferred_element_type=F32)


def _mm_nt(a, b):
    return lax.dot_general(a, b, (((1,), (1,)), ((), ())), preferred_element_type=F32)


def _mm_tn(a, b):
    return lax.dot_general(a, b, (((0,), (0,)), ((), ())), preferred_element_type=F32)


def _gelu_and_grad(x):
    x2 = x * x
    t = jnp.tanh(x * (GELU_K + (GELU_K * GELU_C) * x2))
    hx = 0.5 * x
    g = hx + hx * t
    dg = (0.5 + 0.5 * t) + (hx - hx * t * t) * (GELU_K + (3.0 * GELU_K * GELU_C) * x2)
    return g, dg


def _silu_and_grad(z):
    sig = jax.nn.sigmoid(z)
    zs = z * sig
    return zs, sig + zs * (1.0 - sig)


def _norm_rows(x):
    mu = jnp.mean(x, axis=-1, keepdims=True)
    xc = x - mu
    var = jnp.mean(xc * xc, axis=-1, keepdims=True)
    rstd = lax.rsqrt(var + LN_EPS)
    return xc * rstd, rstd


def _norm_rows_bwd(dxhat, xhat, rstd):
    m1 = jnp.mean(dxhat, axis=-1, keepdims=True)
    m2 = jnp.mean(dxhat * xhat, axis=-1, keepdims=True)
    return rstd * (dxhat - m1 - xhat * m2)


def _inv_count(row0, rows, w):
    t = row0 + lax.broadcasted_iota(jnp.int32, (rows, 1), 0)
    return 1.0 / jnp.minimum(t + 1, w).astype(F32)


def _causal_window_sum(ext, w):
    s, sh = ext, 1
    while sh < w:
        s = s + pltpu.roll(s, sh, axis=0)
        sh *= 2
    return s[HALO:, :]


def _anticausal_window_sum(ext, w):
    n, s, sh = ext.shape[0], ext, 1
    while sh < w:
        s = s + pltpu.roll(s, n - sh, axis=0)
        sh *= 2
    return s[: n - HALO, :]


def _place():
    return lax.axis_index("x"), lax.axis_index("y"), lax.axis_index("c")


def _gather_weights(shards):
    n = len(shards)
    piece = [s.shape[0] // 4 for s in shards]

    def body(*refs):
        wide, dsts, srcs = refs[:n], refs[n:2 * n], refs[2 * n:3 * n]
        send_sems, recv_sems, local_sems = refs[3 * n:]
        for k in range(n):
            for r0 in range(0, 4 * piece[k], CHUNK):
                srcs[k][r0:r0 + CHUNK, :] = wide[k][r0:r0 + CHUNK, :].astype(BF16)
        x, y, c = _place()
        me, sibling = (x, y, c), (x, y, 1 - c)
        across_x, across_y = (1 - x, y, c), (x, 1 - y, c)
        q, qx, qy, qf = 2 * x + y, 2 * (1 - x) + y, 2 * x + (1 - y), 2 * (1 - x) + (1 - y)

        def rows(ref, cc, p, k):
            return ref.at[pl.ds((2 * cc + p) * piece[k], piece[k])]

        def copy(k, sem, qq, cc, p, to, own=False):
            landing = rows(dsts[k].at[qq], cc, p, k)
            return pltpu.make_async_remote_copy(
                src_ref=rows(srcs[k], cc, p, k) if own else landing, dst_ref=landing,
                send_sem=send_sems.at[12 * k + sem], recv_sem=recv_sems.at[12 * k + sem],
                device_id=to, device_id_type=MESH)

        started = []

        def go(cp):
            cp.start()
            started.append(cp)

        mine = [pltpu.make_async_copy(srcs[k], dsts[k].at[q], local_sems.at[k]) for k in range(n)]
        for cp in mine:
            cp.start()
        for k in range(n):
            for p in range(2):
                go(copy(k, p, q, c, p, across_x, own=True))
                go(copy(k, 2 + p, q, c, p, across_y, own=True))
        for k in range(n):
            copy(k, 0, qx, c, 0, me).wait_recv()
            go(copy(k, 4, qx, c, 0, across_y))
            go(copy(k, 6, qx, c, 0, sibling))
            copy(k, 3, qy, c, 1, me).wait_recv()
            go(copy(k, 5, qy, c, 1, across_x))
            go(copy(k, 9, qy, c, 1, sibling))
        for k in range(n):
            copy(k, 1, qx, c, 1, me).wait_recv()
            go(copy(k, 7, qx, c, 1, sibling))
            copy(k, 2, qy, c, 0, me).wait_recv()
            go(copy(k, 8, qy, c, 0, sibling))
        for k in range(n):
            copy(k, 4, qf, c, 0, me).wait_recv()
            go(copy(k, 10, qf, c, 0, sibling))
            copy(k, 5, qf, c, 1, me).wait_recv()
            go(copy(k, 11, qf, c, 1, sibling))
        for k in range(n):
            for sem, qq, p in ((6, qx, 0), (7, qx, 1), (8, qy, 0), (9, qy, 1), (10, qf, 0), (11, qf, 1)):
                copy(k, sem, qq, 1 - c, p, me).wait_recv()
        for cp in started:
            cp.wait_send()
        for cp in mine:
            cp.wait()

    return pl.pallas_call(
        body, name="gather_weights",
        out_shape=[jax.ShapeDtypeStruct((4,) + s.shape, BF16) for s in shards],
        in_specs=[pl.BlockSpec(memory_space=pltpu.VMEM)] * n, out_specs=[ANY] * n,
        scratch_shapes=[pltpu.VMEM(s.shape, BF16) for s in shards]
        + [pltpu.SemaphoreType.DMA((12 * n,)), pltpu.SemaphoreType.DMA((12 * n,)), pltpu.SemaphoreType.DMA((n,))],
        compiler_params=pltpu.CompilerParams(vmem_limit_bytes=VMEM_LIMIT),
    )(*shards)


def _direct_gather_copies(srcs, dsts, send_sems, recv_sems):
    x, y, c = _place()
    q = 2 * x + y
    sends, recvs = [], []
    for k, (src, dst) in enumerate(zip(srcs, dsts)):
        half = src.shape[0] // 2
        for j, chip in enumerate([(1 - x, y), (x, 1 - y), (1 - x, 1 - y)]):
            for core in range(2):
                sends.append(pltpu.make_async_remote_copy(
                    src_ref=src.at[pl.ds(c * half, half)], dst_ref=dst.at[q, pl.ds(c * half, half)],
                    send_sem=send_sems.at[6 * k + 2 * j + core], recv_sem=recv_sems.at[6 * k + 2 * j + c],
                    device_id=(*chip, core), device_id_type=MESH))
                landed = dst.at[2 * chip[0] + chip[1], pl.ds(core * half, half)]
                recvs.append(pltpu.make_async_remote_copy(
                    src_ref=landed, dst_ref=landed, send_sem=send_sems.at[6 * k + 2 * j + core],
                    recv_sem=recv_sems.at[6 * k + 2 * j + core], device_id=(x, y, c), device_id_type=MESH))
    return sends, recvs


def _scatter_copies(srcs, dsts, send_sems, recv_sems):
    x, y, c = _place()
    copies = []
    for j, chip in enumerate([(1 - x, y), (x, 1 - y), (1 - x, 1 - y)]):
        for k, (src, dst) in enumerate(zip(srcs, dsts)):
            copies.append(pltpu.make_async_remote_copy(
                src_ref=src.at[2 * chip[0] + chip[1]] if len(src.shape) == 3 else src, dst_ref=dst.at[j],
                send_sem=send_sems.at[3 * k + j], recv_sem=recv_sems.at[3 * k + j],
                device_id=(*chip, c), device_id_type=MESH))
    return copies


def _scatter_shapes(parts):
    return [jax.ShapeDtypeStruct((3,) + (p.shape[1:] if p.ndim == 3 else p.shape), p.dtype) for p in parts]


def _join_halves_with_sibling(halves):
    n = len(halves)

    def body(*refs):
        srcs, dsts = refs[:n], refs[n:2 * n]
        send_sems, recv_sems = refs[2 * n:]
        x, y, c = _place()
        copies = [pltpu.make_async_remote_copy(
            src_ref=srcs[k], dst_ref=dsts[k], send_sem=send_sems.at[k], recv_sem=recv_sems.at[k],
            device_id=(x, y, 1 - c), device_id_type=MESH) for k in range(n)]
        for cp in copies:
            cp.start()
        for cp in copies:
            cp.wait()

    return pl.pallas_call(
        body, name="join_halves",
        out_shape=[jax.ShapeDtypeStruct(h.shape, h.dtype) for h in halves],
        in_specs=[ANY] * n, out_specs=[ANY] * n,
        scratch_shapes=[pltpu.SemaphoreType.DMA((n,)), pltpu.SemaphoreType.DMA((n,))],
    )(*halves)


def _row_block(rows, cols, n_arrays):
    cap = max(8, (VMEM_LIMIT // 2) // (8 * n_arrays * cols))
    rb = rows
    while rb > cap and rb % 2 == 0:
        rb //= 2
    return rb


def _scalar(value):
    return jnp.reshape(value, (1,)).astype(jnp.int32)


def _whole(a):
    return pl.BlockSpec(a.shape, lambda i, s_ref: (0,) * a.ndim)


def _sum_fours(parts, slots, wholes, wholes_slots, q):
    n, nw = len(parts), len(wholes)

    def four(own, s):
        return (own[...].astype(F32) + s[0].astype(F32)) + (s[1].astype(F32) + s[2].astype(F32))

    def body(q_ref, *refs):
        p, s, w, ws = refs[:n], refs[n:2 * n], refs[2 * n:2 * n + nw], refs[2 * n + nw:2 * n + 2 * nw]
        o, wo = refs[2 * n + 2 * nw:3 * n + 2 * nw], refs[3 * n + 2 * nw:]
        for k in range(n):
            o[k][...] = four(p[k], s[k])
        for k in range(nw):
            wo[k][...] = four(w[k], ws[k])

    return pl.pallas_call(
        body, name="sum_fours",
        out_shape=[jax.ShapeDtypeStruct(a.shape[1:], F32) for a in parts] + [jax.ShapeDtypeStruct(a.shape, F32) for a in wholes],
        grid_spec=pltpu.PrefetchScalarGridSpec(
            num_scalar_prefetch=1, grid=(2,),
            in_specs=[pl.BlockSpec((None, a.shape[1] // 2, a.shape[2]), lambda i, q_ref: (q_ref[0], i, 0)) for a in parts]
            + [pl.BlockSpec((3, a.shape[1] // 2, a.shape[2]), lambda i, q_ref: (0, i, 0)) for a in slots]
            + [_whole(a) for a in wholes + wholes_slots],
            out_specs=[pl.BlockSpec((a.shape[1] // 2, a.shape[2]), lambda i, q_ref: (i, 0)) for a in parts]
            + [_whole(a) for a in wholes]),
        compiler_params=pltpu.CompilerParams(vmem_limit_bytes=VMEM_LIMIT),
    )(_scalar(q), *parts, *slots, *wholes, *wholes_slots)


def _adamw_math(w, g, m, v):
    nm = ADAM_B1 * m + (1.0 - ADAM_B1) * g
    nv = ADAM_B2 * v + (1.0 - ADAM_B2) * (g * g)
    m_hat = nm / (1.0 - ADAM_B1 ** ADAM_STEP)
    v_hat = nv / (1.0 - ADAM_B2 ** ADAM_STEP)
    return -ADAM_LR * (m_hat / (jnp.sqrt(v_hat) + ADAM_EPS) + ADAM_WD * w), nm, nv


def _adamw(w, g, m, v, name):
    rows, cols = w.shape
    rb = _row_block(rows, cols, 8)

    def body(w_ref, g_ref, m_ref, v_ref, go_ref, d_ref, nm_ref, nv_ref):
        go_ref[...] = g_ref[...]
        d_ref[...], nm_ref[...], nv_ref[...] = _adamw_math(w_ref[...], g_ref[...], m_ref[...], v_ref[...])

    spec = pl.BlockSpec((rb, cols), lambda r: (r, 0))
    out = jax.ShapeDtypeStruct(w.shape, F32)
    return pl.pallas_call(body, name=name, out_shape=[out] * 4, grid=(rows // rb,),
                          in_specs=[spec] * 4, out_specs=[spec] * 4,
                          compiler_params=pltpu.CompilerParams(vmem_limit_bytes=VMEM_LIMIT))(w, g, m, v)


def _adamw_joined(ws, g_mine, g_sibling, ms, vs, small, small_grads, c):
    n, ns, ng = len(ws), len(small), len(small_grads)

    def body(c_ref, *refs):
        big, tot, sm = refs[:5 * n], refs[5 * n:5 * n + ng], refs[5 * n + ng:5 * n + ng + 3 * ns]
        outs = refs[5 * n + ng + 3 * ns:]
        mine = c_ref[0] == pl.program_id(0)
        for k in range(n):
            w, gm, gs, m, v = big[5 * k:5 * k + 5]
            g = jnp.where(mine, gm[...], gs[...])
            outs[4 * k][...] = g
            outs[4 * k + 1][...], outs[4 * k + 2][...], outs[4 * k + 3][...] = _adamw_math(w[...], g, m[...], v[...])
        for k in range(ns):
            w, m, v = sm[3 * k:3 * k + 3]
            which, rows, cols = small[k][3]
            g = tot[which][rows, cols]
            if small[k][4]:
                i_pos = lax.broadcasted_iota(jnp.int32, g.shape, 0) % CHUNK
                g = jnp.where(lax.broadcasted_iota(jnp.int32, g.shape, 1) <= i_pos, g, 0.0)
            o = outs[4 * (n + k):4 * (n + k) + 4]
            o[0][...] = g
            o[1][...], o[2][...], o[3][...] = _adamw_math(w[...], g, m[...], v[...])

    def half(a):
        return pl.BlockSpec((a.shape[0] // 2, a.shape[1]), lambda hf, c_ref: (hf, 0))

    in_specs, operands = [], []
    for k in range(n):
        in_specs += [half(ws[k]), _whole(g_mine[k]), _whole(g_sibling[k]), half(ms[k]), half(vs[k])]
        operands += [ws[k], g_mine[k], g_sibling[k], ms[k], vs[k]]
    operands += list(small_grads) + [a for item in small for a in item[:3]]
    in_specs += [_whole(a) for a in operands[5 * n:]]
    outs = pl.pallas_call(
        body, name="adamw_joined",
        out_shape=[jax.ShapeDtypeStruct(w.shape, F32) for w in ws for _ in range(4)]
        + [jax.ShapeDtypeStruct(item[0].shape, F32) for item in small for _ in range(4)],
        grid_spec=pltpu.PrefetchScalarGridSpec(
            num_scalar_prefetch=1, grid=(2,),
            in_specs=in_specs,
            out_specs=[half(w) for w in ws for _ in range(4)] + [_whole(item[0]) for item in small for _ in range(4)]),
        compiler_params=pltpu.CompilerParams(vmem_limit_bytes=VMEM_LIMIT),
    )(_scalar(c), *operands)
    return [outs[4 * k:4 * k + 4] for k in range(n + ns)]


def _front_forward(x, w_in, pool_w, pool_scale, sgu_g, sgu_b, sgu_wm, sgu_bias_t, later_shards, tm):
    T, D = x.shape
    nq, _, cq = w_in.shape
    G, PG = pool_w.shape[0], pool_w.shape[1]
    nt = T // tm
    bpd = D // PG
    nl = len(later_shards)

    def body(x_ref, win_any, pw_any, ps_ref, lg_ref, lb_ref, sw_ref, sb_ref, *refs):
        shards_any, (keep_ref, y_ref, xt_ref), gathered = refs[:nl], refs[nl:nl + 3], refs[nl + 3:2 * nl + 3]
        wide, narrow = refs[2 * nl + 3:3 * nl + 3], refs[3 * nl + 3:4 * nl + 3]
        win_v, pw_v, carry, sems, load_sems, own_sems, send_sems, recv_sems = refs[4 * nl + 3:]
        i = pl.program_id(0)
        own_quarter = 2 * lax.axis_index("x") + lax.axis_index("y")

        def own_copies():
            return [pltpu.make_async_copy(narrow[k], gathered[k].at[own_quarter], own_sems.at[k]) for k in range(nl)]

        @pl.when(i == 0)
        def _():
            c1 = pltpu.make_async_copy(win_any, win_v, sems.at[0])
            c2 = pltpu.make_async_copy(pw_any, pw_v, sems.at[1])
            loads = [pltpu.make_async_copy(shards_any[k], wide[k], load_sems.at[k]) for k in range(nl)]
            for cp in [c1, c2] + loads:
                cp.start()
            carry[...] = jnp.zeros_like(carry)
            for k in range(nl):
                loads[k].wait()
                for r0 in range(0, wide[k].shape[0], CHUNK):
                    narrow[k][r0:r0 + CHUNK, :] = wide[k][r0:r0 + CHUNK, :].astype(BF16)
            for cp in own_copies() + _direct_gather_copies(narrow, gathered, send_sems, recv_sems)[0]:
                cp.start()
            c1.wait()
            c2.wait()

        xb = x_ref[...].astype(BF16)
        xt_ref[...] = x_ref[...].T.astype(BF16)

        def h_block(j):
            qq, off = divmod(j * PG, cq)
            return _mm(xb, win_v[qq, :, off:off + PG])

        def keep(part, col, value):
            keep_ref[:, SAVED[part] * D + col:SAVED[part] * D + col + PG] = value.astype(BF16)

        def ahead(stage):
            if stage < G:
                return h_block(stage), h_block(3 * bpd + stage)
            if stage < G + bpd:
                hd = stage - G
                return h_block(bpd + hd), h_block(2 * bpd + hd), h_block(4 * bpd + hd)
            return None

        blocks = ahead(0)

        for g, w in enumerate(POOL_WINDOWS):
            sl = slice(g * PG, (g + 1) * PG)
            a, z = blocks
            blocks = ahead(g + 1)
            ext = jnp.concatenate([carry[:, sl], a], axis=0)
            carry[:, sl] = a[tm - HALO:, :]
            pooled = (_causal_window_sum(ext, w) * _inv_count(i * tm, tm, w) - a).astype(BF16)
            mixed = _mm(pooled, pw_v[g])
            zs, dzs = _silu_and_grad(z)
            keep("pooled", g * PG, pooled)
            keep("silu", g * PG, zs)
            keep("dsilu", g * PG, dzs)
            y_ref[:, sl] = (mixed * ps_ref[:, sl] * zs).astype(BF16)

        for hd in range(bpd):
            sl = slice(hd * PG, (hd + 1) * PG)
            u, v, z = blocks
            blocks = ahead(G + hd + 1)
            ug, dug = _gelu_and_grad(u)
            vg, dvg = _gelu_and_grad(v)
            vhat, rstd = _norm_rows(vg)
            zs, dzs = _silu_and_grad(z)
            keep("gelu_u", hd * PG, ug)
            keep("dgelu_u", hd * PG, dug)
            keep("vhat", hd * PG, vhat)
            keep("rstd_dgelu_v", hd * PG, rstd * dvg)
            keep("silu", D + hd * PG, zs)
            keep("dsilu", D + hd * PG, dzs)
            vn = (vhat * lg_ref[:, sl] + lb_ref[:, sl]).astype(BF16)
            gated = ug * zs
            for n in range(tm // CHUNK):
                rs = slice(n * CHUNK, (n + 1) * CHUNK)
                sv = _mm(sw_ref[hd], vn[rs, :]) + sb_ref[:, hd:hd + 1]
                y_ref[rs, D + hd * PG:D + (hd + 1) * PG] = (gated[rs, :] * sv).astype(BF16)

        @pl.when(i == nt - 1)
        def _():
            sends, recvs = _direct_gather_copies(narrow, gathered, send_sems, recv_sems)
            for cp in sends:
                cp.wait_send()
            for cp in recvs:
                cp.wait_recv()
            for cp in own_copies():
                cp.wait()

    vec = pl.BlockSpec((1, D), lambda i: (0, 0))
    return pl.pallas_call(
        body, name="front_forward",
        out_shape=[jax.ShapeDtypeStruct((T, SAVED_WIDTH * D), BF16), jax.ShapeDtypeStruct((T, 2 * D), BF16),
                   jax.ShapeDtypeStruct((D, T), BF16)]
        + [jax.ShapeDtypeStruct((4,) + s.shape, BF16) for s in later_shards],
        grid=(nt,),
        in_specs=[pl.BlockSpec((tm, D), lambda i: (i, 0)), ANY, ANY, vec, vec, vec,
                  pl.BlockSpec(sgu_wm.shape, lambda i: (0, 0, 0)), pl.BlockSpec(sgu_bias_t.shape, lambda i: (0, 0))]
        + [ANY] * nl,
        out_specs=[pl.BlockSpec((tm, SAVED_WIDTH * D), lambda i: (i, 0)), pl.BlockSpec((tm, 2 * D), lambda i: (i, 0)),
                   pl.BlockSpec((D, tm), lambda i: (0, i))] + [ANY] * nl,
        scratch_shapes=[pltpu.VMEM(s.shape, F32) for s in later_shards]
        + [pltpu.VMEM(s.shape, BF16) for s in later_shards]
        + [pltpu.VMEM(w_in.shape, BF16), pltpu.VMEM(pool_w.shape, BF16), pltpu.VMEM((HALO, D), F32),
           pltpu.SemaphoreType.DMA((2,)), pltpu.SemaphoreType.DMA((nl,)), pltpu.SemaphoreType.DMA((nl,)),
           pltpu.SemaphoreType.DMA((6 * nl,)), pltpu.SemaphoreType.DMA((6 * nl,))],
        compiler_params=pltpu.CompilerParams(dimension_semantics=("arbitrary",), vmem_limit_bytes=VMEM_LIMIT),
    )(x, w_in, pool_w, pool_scale, sgu_g, sgu_b, sgu_wm, sgu_bias_t, *later_shards)


def _tail(y, x, p, target, w_out, w_gate, w_ple, ln_g, ln_b, gate_b, tm):
    T, D = x.shape
    K = p.shape[1]
    nq, _, cq = w_ple.shape
    nt = T // tm

    def body(y_ref, x_ref, p_ref, t_ref, wout_any, wg_any, wp_any, lng_ref, lnb_ref, bg_ref,
             dxp_ref, dy_ref, dwout_any, dwg_any, dwp_any, dlng_ref, dlnb_ref, dbg_ref, ssq_ref,
             wout_v, wg_v, wp_v, dwout_acc, dwg_acc, dwp_acc, sems):
        i = pl.program_id(0)

        @pl.when(i == 0)
        def _():
            loads = [pltpu.make_async_copy(s, d, sems.at[k])
                     for k, (s, d) in enumerate(((wout_any, wout_v), (wg_any, wg_v), (wp_any, wp_v)))]
            for cp in loads:
                cp.start()
            for ref in (dwout_acc, dwg_acc, dwp_acc, dlng_ref, dlnb_ref, dbg_ref, ssq_ref):
                ref[...] = jnp.zeros_like(ref)
            for cp in loads:
                cp.wait()

        halves = [slice(k * tm // 2, (k + 1) * tm // 2) for k in range(2)]

        def total(parts):
            return sum(jnp.sum(part, axis=0, keepdims=True) for part in parts)

        yb = [y_ref[r, :] for r in halves]
        pb = [p_ref[r, :].astype(BF16) for r in halves]
        mix = [_mm(v, wout_v[...]) for v in yb]
        normed = [_norm_rows(DEEPNORM_ALPHA * x_ref[r, :] + m) for r, m in zip(halves, mix)]
        xhat, rstd = [n[0] for n in normed], [n[1] for n in normed]
        x1 = [v * lng_ref[...] + lnb_ref[...] for v in xhat]
        x1b = [v.astype(BF16) for v in x1]
        gate = [jax.nn.sigmoid(_mm(v, wg_v[...]) + bg_ref[...]) for v in x1b]
        e = [jnp.concatenate([_mm(v, wp_v[qq]) for qq in range(nq)], axis=1) for v in pb]
        diff = [a + g * ee - t_ref[r, :] for a, g, ee, r in zip(x1, gate, e, halves)]
        ssq_ref[...] += total([d * d for d in diff])

        dout = [d * (1.0 / D) for d in diff]
        d_e = [(do * g).astype(BF16) for do, g in zip(dout, gate)]
        dgl = [do * ee * g * (1.0 - g) for do, ee, g in zip(dout, e, gate)]
        dglb = [v.astype(BF16) for v in dgl]
        dbg_ref[...] += total(dgl)
        pb_t, d_e_t, x1b_t, dglb_t = (jnp.concatenate(v, axis=0) for v in (pb, d_e, x1b, dglb))
        for qq in range(nq):
            dwp_acc[qq] += _mm_tn(pb_t, d_e_t[:, qq * cq:(qq + 1) * cq])
        for c0 in range(0, D, MXU_COLS):
            dwg_acc[:, c0:c0 + MXU_COLS] += _mm_tn(x1b_t, dglb_t[:, c0:c0 + MXU_COLS])
        d_x1 = [do + _mm_nt(dg, wg_v[...]) for do, dg in zip(dout, dglb)]
        dlng_ref[...] += total([d * xh for d, xh in zip(d_x1, xhat)])
        dlnb_ref[...] += total(d_x1)
        d_r = [_norm_rows_bwd(d * lng_ref[...], xh, rs) for d, xh, rs in zip(d_x1, xhat, rstd)]
        drb = [v.astype(BF16) for v in d_r]
        for r, v in zip(halves, d_r):
            dxp_ref[r, :] = DEEPNORM_ALPHA * v
        for c0 in range(0, 2 * D, 2 * MXU_COLS):
            for r, v in zip(halves, drb):
                dy_ref[r, c0:c0 + 2 * MXU_COLS] = _mm_nt(v, wout_v[c0:c0 + 2 * MXU_COLS, :]).astype(BF16)

        drb_t = jnp.concatenate(drb, axis=0)
        for c0 in range(0, D, MXU_COLS):
            dwout_acc[:, c0:c0 + MXU_COLS] += _mm_tn(y_ref[...], drb_t[:, c0:c0 + MXU_COLS])

        @pl.when(i == nt - 1)
        def _():
            stores = [pltpu.make_async_copy(s, d, sems.at[k])
                      for k, (s, d) in enumerate(((dwout_acc, dwout_any), (dwg_acc, dwg_any), (dwp_acc, dwp_any)))]
            for cp in stores:
                cp.start()
            for cp in stores:
                cp.wait()

    vec = pl.BlockSpec((1, D), lambda i: (0, 0))
    vec_shape = jax.ShapeDtypeStruct((1, D), F32)

    def tile(cols):
        return pl.BlockSpec((tm, cols), lambda i: (i, 0))

    return pl.pallas_call(
        body, name="tail",
        out_shape=[jax.ShapeDtypeStruct((T, D), F32), jax.ShapeDtypeStruct((T, 2 * D), BF16),
                   jax.ShapeDtypeStruct(w_out.shape, F32), jax.ShapeDtypeStruct(w_gate.shape, F32),
                   jax.ShapeDtypeStruct(w_ple.shape, F32), vec_shape, vec_shape, vec_shape, vec_shape],
        grid=(nt,),
        in_specs=[tile(2 * D), tile(D), tile(K), tile(D), ANY, ANY, ANY, vec, vec, vec],
        out_specs=[tile(D), tile(2 * D), ANY, ANY, ANY, vec, vec, vec, vec],
        scratch_shapes=[pltpu.VMEM(w_out.shape, BF16), pltpu.VMEM(w_gate.shape, BF16), pltpu.VMEM(w_ple.shape, BF16),
                        pltpu.VMEM(w_out.shape, F32), pltpu.VMEM(w_gate.shape, F32), pltpu.VMEM(w_ple.shape, F32),
                        pltpu.SemaphoreType.DMA((3,))],
        compiler_params=pltpu.CompilerParams(dimension_semantics=("arbitrary",), vmem_limit_bytes=VMEM_LIMIT),
    )(y, x, p, target, w_out, w_gate, w_ple, ln_g, ln_b, gate_b)


def _front_backward(kept, d_y, dx_part, w_in, pool_w, pool_scale, sgu_g, sgu_b, sgu_wm, sgu_bias_t, tm):
    T = kept.shape[0]
    D = kept.shape[1] // SAVED_WIDTH
    nq, _, cq = w_in.shape
    G, PG = pool_w.shape[0], pool_w.shape[1]
    nt = T // tm

    def tile_of(i):
        return nt - 1 - jnp.minimum(i, nt - 1)

    def body(kept_ref, dy_ref, dxp_ref, win_any, pw_ref, ps_ref, lg_ref, lb_ref, sw_ref, sb_ref,
             dh_ref, dx_ref, dpw_ref, dps_ref, dlg_ref, dlb_ref, dsw_ref, dsb_ref, win_v, dh_keep, carry, sems):
        i = pl.program_id(0)
        ti = tile_of(i)

        def saved(part, col, rows=slice(None)):
            return kept_ref[rows, SAVED[part] * D + col:SAVED[part] * D + col + PG]

        @pl.when(i == 0)
        def _():
            cp = pltpu.make_async_copy(win_any, win_v, sems.at[0])
            cp.start()
            carry[...] = jnp.zeros_like(carry)
            for ref in (dpw_ref, dps_ref, dlg_ref, dlb_ref, dsw_ref, dsb_ref):
                ref[...] = jnp.zeros_like(ref)
            cp.wait()

        def dx_columns(r0):
            dx = dxp_ref[:, r0:r0 + MXU_COLS]
            for qq in range(nq):
                dx = dx + _mm_nt(dh_keep[(i + 1) % 2, :, qq * cq:(qq + 1) * cq], win_v[qq, r0:r0 + MXU_COLS, :])
            dx_ref[:, r0:r0 + MXU_COLS] = dx

        dx_chunks = list(range(0, D, MXU_COLS))
        stages = G + D // PG

        def pool_stage(g, w):
            sl = slice(g * PG, (g + 1) * PG)
            pooled = saved("pooled", g * PG)
            mixed = _mm(pooled, pw_ref[g])
            dy = dy_ref[:, sl].astype(F32)
            d_ypool = dy * saved("silu", g * PG).astype(F32)
            dh_ref[:, 3 * D + g * PG:3 * D + (g + 1) * PG] = (
                dy * (mixed * ps_ref[:, sl]) * saved("dsilu", g * PG).astype(F32)).astype(BF16)
            dps_ref[:, sl] += jnp.sum(d_ypool * mixed, axis=0, keepdims=True)
            d_mixed = (d_ypool * ps_ref[:, sl]).astype(BF16)
            dpw_ref[g] += _mm_tn(pooled, d_mixed)
            d_pooled = _mm_nt(d_mixed, pw_ref[g])
            scaled = d_pooled * _inv_count(ti * tm, tm, w)
            after = jnp.concatenate([scaled, carry[:, sl]], axis=0)
            carry[:, sl] = scaled[:HALO, :]
            dh_ref[:, sl] = (_anticausal_window_sum(after, w) - d_pooled).astype(BF16)

        def gating_stage(hd):
            sl = slice(hd * PG, (hd + 1) * PG)
            vhat = saved("vhat", hd * PG).astype(F32)
            vn = (vhat * lg_ref[:, sl] + lb_ref[:, sl]).astype(BF16)
            chunk_rows = [slice(n * CHUNK, (n + 1) * CHUNK) for n in range(tm // CHUNK)]
            vn_wide = jnp.concatenate([vn[rs, :] for rs in chunk_rows], axis=1)
            sv_wide = _mm(sw_ref[hd], vn_wide) + sb_ref[:, hd:hd + 1]
            d_sv_parts = []
            for n, rs in enumerate(chunk_rows):
                sv = sv_wide[:, n * PG:(n + 1) * PG]
                ug = saved("gelu_u", hd * PG, rs).astype(F32)
                dy = dy_ref[rs, D + hd * PG:D + (hd + 1) * PG].astype(F32)
                d_ysgu = dy * saved("silu", D + hd * PG, rs).astype(F32)
                dh_ref[rs, 4 * D + hd * PG:4 * D + (hd + 1) * PG] = (
                    dy * (ug * sv) * saved("dsilu", D + hd * PG, rs).astype(F32)).astype(BF16)
                dh_ref[rs, D + hd * PG:D + (hd + 1) * PG] = (
                    d_ysgu * sv * saved("dgelu_u", hd * PG, rs).astype(F32)).astype(BF16)
                d_sv_parts.append(d_ysgu * ug)
            d_sv_wide = jnp.concatenate(d_sv_parts, axis=1)
            dsb_ref[:, hd:hd + 1] += jnp.sum(d_sv_wide, axis=1, keepdims=True)
            d_svb = d_sv_wide.astype(BF16)
            dsw_ref[hd] += _mm_nt(d_svb, vn_wide)
            d_vn_wide = _mm_tn(sw_ref[hd], d_svb)
            d_vn = jnp.concatenate([d_vn_wide[:, n * PG:(n + 1) * PG] for n in range(len(chunk_rows))], axis=0)
            dlg_ref[:, sl] += jnp.sum(d_vn * vhat, axis=0, keepdims=True)
            dlb_ref[:, sl] += jnp.sum(d_vn, axis=0, keepdims=True)
            d_vg = _norm_rows_bwd(d_vn * lg_ref[:, sl], vhat, saved("rstd_dgelu_v", hd * PG).astype(F32))
            dh_ref[:, 2 * D + hd * PG:2 * D + (hd + 1) * PG] = d_vg.astype(BF16)

        def work(make_dx, make_dh):
            for stage in range(stages):
                if make_dx:
                    for r0 in dx_chunks[stage * len(dx_chunks) // stages:(stage + 1) * len(dx_chunks) // stages]:
                        dx_columns(r0)
                if make_dh and stage < G:
                    pool_stage(stage, POOL_WINDOWS[stage])
                elif make_dh:
                    gating_stage(stage - G)
            if make_dh:
                dh_keep[i % 2] = dh_ref[...]

        pl.when(i == 0)(functools.partial(work, False, True))
        pl.when((i > 0) & (i < nt))(functools.partial(work, True, True))
        pl.when(i == nt)(functools.partial(work, True, False))

    vec = pl.BlockSpec((1, D), lambda i: (0, 0))
    vec_shape = jax.ShapeDtypeStruct((1, D), F32)

    def whole(shape):
        return pl.BlockSpec(shape, lambda i: (0,) * len(shape))

    return pl.pallas_call(
        body, name="front_backward",
        out_shape=[jax.ShapeDtypeStruct((T, 5 * D), BF16), jax.ShapeDtypeStruct((T, D), F32),
                   jax.ShapeDtypeStruct(pool_w.shape, F32), vec_shape, vec_shape,
                   vec_shape, jax.ShapeDtypeStruct(sgu_wm.shape, F32), jax.ShapeDtypeStruct(sgu_bias_t.shape, F32)],
        grid=(nt + 1,),
        in_specs=[pl.BlockSpec((tm, SAVED_WIDTH * D), lambda i: (tile_of(i), 0)),
                  pl.BlockSpec((tm, 2 * D), lambda i: (tile_of(i), 0)),
                  pl.BlockSpec((tm, D), lambda i: (jnp.minimum(nt - i, nt - 1), 0)), ANY,
                  whole(pool_w.shape), vec, vec, vec, whole(sgu_wm.shape), whole(sgu_bias_t.shape)],
        out_specs=[pl.BlockSpec((tm, 5 * D), lambda i: (tile_of(i), 0)),
                   pl.BlockSpec((tm, D), lambda i: (jnp.minimum(nt - i, nt - 1), 0)),
                   whole(pool_w.shape), vec, vec, vec, whole(sgu_wm.shape), whole(sgu_bias_t.shape)],
        scratch_shapes=[pltpu.VMEM(w_in.shape, BF16), pltpu.VMEM((2, tm, 5 * D), BF16), pltpu.VMEM((HALO, D), F32),
                        pltpu.SemaphoreType.DMA((1,))],
        compiler_params=pltpu.CompilerParams(dimension_semantics=("arbitrary",), vmem_limit_bytes=VMEM_LIMIT),
    )(kept, d_y, dx_part, w_in, pool_w, pool_scale, sgu_g, sgu_b, sgu_wm, sgu_bias_t)


def _weight_backward(d_h, xt, q, scatter_srcs, tm):
    D, T = xt.shape
    cq = d_h.shape[1] // 4
    hr = D // 2
    nt = T // tm
    ns = len(scatter_srcs)

    def body(q_ref, dh_ref, xt_ref, *refs):
        srcs, out_any, dsts = refs[:ns], refs[ns], refs[ns + 1:2 * ns + 1]
        (acc, land_a, send_b, land_b, mine_f, theirs_f,
         a_send, a_recv, b_send, b_recv, j_sems, o_sems, s_send, s_recv) = refs[2 * ns + 1:]
        s, t = pl.program_id(0), pl.program_id(1)
        x_, y_, c = _place()
        sibling = (x_, y_, 1 - c)
        own_rows = pl.ds(pl.multiple_of(c * hr, hr), hr)
        other_rows = pl.ds(pl.multiple_of((1 - c) * hr, hr), hr)

        @pl.when((s == 0) & (t == 0))
        def _():
            for cp in _scatter_copies(srcs, dsts, s_send, s_recv):
                cp.start()

        @pl.when(t == 0)
        def _():
            acc[s % 2] = jnp.zeros((D, cq), F32)

        for c0 in range(0, cq, MXU_COLS):
            acc[s % 2, :, c0:c0 + MXU_COLS] += _mm(xt_ref[...], dh_ref[:, c0:c0 + MXU_COLS])

        def swap(phase):
            return pltpu.make_async_remote_copy(
                src_ref=acc.at[phase % 2, other_rows], dst_ref=land_a.at[phase % 2], send_sem=a_send.at[phase],
                recv_sem=a_recv.at[phase], device_id=sibling, device_id_type=MESH)

        def pair_sum(phase):
            swap(phase).wait()
            return acc[phase % 2, own_rows, :] + land_a[phase % 2]

        def to_owner(slot):
            flip_x, flip_y = (slot + 1) >> 1, (slot + 1) & 1
            owner = (1 - x_ if flip_x else x_, 1 - y_ if flip_y else y_, c)
            return pltpu.make_async_remote_copy(
                src_ref=send_b.at[slot], dst_ref=land_b.at[slot], send_sem=b_send.at[slot],
                recv_sem=b_recv.at[slot], device_id=owner, device_id_type=MESH)

        for slot in range(3):
            @pl.when((s == slot) & (t == nt - 1))
            def _(slot=slot):
                swap(slot).start()

            @pl.when((s == slot + 1) & (t == 0))
            def _(slot=slot):
                send_b[slot] = pair_sum(slot).astype(BF16)
                to_owner(slot).start()

        @pl.when((s == 3) & (t == nt - 1))
        def _():
            swap(3).start()
            own = pair_sum(3)
            for slot in range(3):
                to_owner(slot).wait_recv()
            mine_f[...] = (own + land_b[0].astype(F32)) + (land_b[1].astype(F32) + land_b[2].astype(F32))
            join = pltpu.make_async_remote_copy(
                src_ref=mine_f, dst_ref=theirs_f, send_sem=j_sems.at[0], recv_sem=j_sems.at[1],
                device_id=sibling, device_id_type=MESH)
            join.start()
            out_mine = pltpu.make_async_copy(mine_f, out_any.at[own_rows], o_sems.at[0])
            out_mine.start()
            join.wait()
            out_theirs = pltpu.make_async_copy(theirs_f, out_any.at[other_rows], o_sems.at[1])
            out_theirs.start()
            for slot in range(3):
                to_owner(slot).wait_send()
            for cp in _scatter_copies(srcs, dsts, s_send, s_recv):
                cp.wait()
            out_mine.wait()
            out_theirs.wait()

    def quarter(s, t, q_ref):
        return (t, jnp.where(s == 3, q_ref[0], q_ref[0] ^ (s + 1)))

    dma = pltpu.SemaphoreType.DMA
    return pl.pallas_call(
        body, name="weight_backward",
        out_shape=[jax.ShapeDtypeStruct((D, cq), F32)] + _scatter_shapes(scatter_srcs),
        grid_spec=pltpu.PrefetchScalarGridSpec(
            num_scalar_prefetch=1, grid=(4, nt),
            in_specs=[pl.BlockSpec((tm, cq), quarter), pl.BlockSpec((D, tm), lambda s, t, q_ref: (0, t))] + [ANY] * ns,
            out_specs=[ANY] * (ns + 1),
            scratch_shapes=[pltpu.VMEM((2, D, cq), F32), pltpu.VMEM((2, hr, cq), F32), pltpu.VMEM((3, hr, cq), BF16),
                            pltpu.VMEM((3, hr, cq), BF16), pltpu.VMEM((hr, cq), F32), pltpu.VMEM((hr, cq), F32),
                            dma((4,)), dma((4,)), dma((3,)), dma((3,)), dma((2,)), dma((2,)), dma((3 * ns,)), dma((3 * ns,))]),
        compiler_params=pltpu.CompilerParams(dimension_semantics=("arbitrary", "arbitrary"),
                                             vmem_limit_bytes=VMEM_LIMIT),
    )(jnp.reshape(q, (1,)).astype(jnp.int32), d_h, xt, *scatter_srcs)


def _pair_reduce(grads, wholes):
    n, nw = len(grads), len(wholes)
    half_shapes = [(4,) + g.shape[2:] for g in grads]

    def body(*refs):
        g_any, w_any, o_any, wo_any = refs[:n], refs[n:n + nw], refs[n + nw:2 * n + nw], refs[2 * n + nw:2 * n + 2 * nw]
        rest = refs[2 * n + 2 * nw:]
        mine, theirs, send, land, out = (rest[k * n:(k + 1) * n] for k in range(5))
        w_v, w_land, w_out = (rest[5 * n + k * nw:5 * n + (k + 1) * nw] for k in range(3))
        load_sems, send_sems, recv_sems, store_sems = rest[5 * n + 3 * nw:]
        x, y, c = _place()
        sibling = (x, y, 1 - c)
        loads = [pltpu.make_async_copy(g_any[k].at[:, 1 - c], theirs[k], load_sems.at[k]) for k in range(n)]
        loads += [pltpu.make_async_copy(g_any[k].at[:, c], mine[k], load_sems.at[n + k]) for k in range(n)]
        loads += [pltpu.make_async_copy(w_any[k], w_v[k], load_sems.at[2 * n + k]) for k in range(nw)]
        for cp in loads:
            cp.start()
        sends = []
        for k in range(n):
            loads[k].wait()
            send[k][...] = theirs[k][...].astype(BF16)
            sends.append(pltpu.make_async_remote_copy(
                src_ref=send[k], dst_ref=land[k], send_sem=send_sems.at[k], recv_sem=recv_sems.at[k],
                device_id=sibling, device_id_type=MESH))
            sends[-1].start()
        for k in range(nw):
            loads[2 * n + k].wait()
            sends.append(pltpu.make_async_remote_copy(
                src_ref=w_v[k], dst_ref=w_land[k], send_sem=send_sems.at[n + k], recv_sem=recv_sems.at[n + k],
                device_id=sibling, device_id_type=MESH))
            sends[-1].start()
        stores = []
        for k in range(n):
            loads[n + k].wait()
            sends[k].wait_recv()
            out[k][...] = (mine[k][...] + land[k][...].astype(F32)).astype(BF16)
            stores.append(pltpu.make_async_copy(out[k], o_any[k], store_sems.at[k]))
            stores[-1].start()
        for k in range(nw):
            sends[n + k].wait_recv()
            w_out[k][...] = w_v[k][...] + w_land[k][...]
            stores.append(pltpu.make_async_copy(w_out[k], wo_any[k], store_sems.at[n + k]))
            stores[-1].start()
        for cp in sends:
            cp.wait_send()
        for cp in stores:
            cp.wait()

    dma = pltpu.SemaphoreType.DMA
    return pl.pallas_call(
        body, name="pair_reduce",
        out_shape=[jax.ShapeDtypeStruct(s, BF16) for s in half_shapes] + [jax.ShapeDtypeStruct(w.shape, F32) for w in wholes],
        in_specs=[ANY] * (n + nw), out_specs=[ANY] * (n + nw),
        scratch_shapes=[pltpu.VMEM(s, F32) for s in half_shapes] * 2 + [pltpu.VMEM(s, BF16) for s in half_shapes] * 3
        + [pltpu.VMEM(w.shape, F32) for w in wholes] * 3
        + [dma((2 * n + nw,)), dma((n + nw,)), dma((n + nw,)), dma((n + nw,))],
        compiler_params=pltpu.CompilerParams(vmem_limit_bytes=VMEM_LIMIT),
    )(*grads, *wholes)


def _token_tile(T, want):
    return math.gcd(T, want)


def kernel(x, p, w_in, pool_w, pool_scale, sgu_ln_g, sgu_ln_b, sgu_w, sgu_b, w_out, ln_g, ln_b, ple_w, ple_gate_w, ple_gate_b, loss_target, m_w_in, m_pool_w, m_pool_scale, m_sgu_ln_g, m_sgu_ln_b, m_sgu_w, m_sgu_b, m_w_out, m_ln_g, m_ln_b, m_ple_w, m_ple_gate_w, m_ple_gate_b, v_w_in, v_pool_w, v_pool_scale, v_sgu_ln_g, v_sgu_ln_b, v_sgu_w, v_sgu_b, v_w_out, v_ln_g, v_ln_b, v_ple_w, v_ple_gate_w, v_ple_gate_b):
    c = lax.axis_index("c")
    T, D = x.shape[1], x.shape[2]
    tm, tm_vpu, tm_acc = _token_tile(T, 512), _token_tile(T, 256), _token_tile(T, 2048)
    x2, p2, tgt = x[0], p[0, 0], loss_target[0]
    G, PGQ, PG = pool_w.shape[1], pool_w.shape[2], pool_w.shape[3]

    w_in_f, pool_f = _gather_weights([w_in[0], pool_w[0].reshape(G * PGQ, PG)])
    pool_f = pool_f.reshape(4, G, PGQ, PG).transpose(1, 0, 2, 3).reshape(G, 4 * PGQ, PG)
    tril = jnp.tril(jnp.ones((CHUNK, CHUNK), dtype=bool))
    sgu_wm = jnp.where(tril[None], sgu_w[0], 0.0).astype(BF16)
    sgu_bias_t = sgu_b[0].T

    kept, y, xt, w_out_f, w_gate_f, w_ple_f = _front_forward(
        x2, w_in_f, pool_f, pool_scale, sgu_ln_g, sgu_ln_b, sgu_wm, sgu_bias_t, [w_out[0], ple_gate_w[0], ple_w[0]],
        tm)
    w_out_f = w_out_f.reshape(-1, D)
    w_gate_f = w_gate_f.reshape(-1, D)
    (dx_part, d_y, d_w_out, d_w_gate, d_w_ple, d_ln_g, d_ln_b, d_gate_b, ssq) = _tail(
        y, x2, p2, tgt, w_out_f, w_gate_f, w_ple_f, ln_g, ln_b, ple_gate_b, tm)
    d_h, d_x, d_pool_w, d_pool_scale, d_sgu_g, d_sgu_b, d_sgu_w, d_sgu_bias_t = _front_backward(
        kept, d_y, dx_part, w_in_f, pool_f, pool_scale, sgu_ln_g, sgu_ln_b, sgu_wm, sgu_bias_t, tm_vpu)
    grads = [d_w_out.reshape(4, -1, D), d_w_gate.reshape(4, -1, D), d_w_ple,
             d_pool_w.reshape(G, 4, PGQ, PG).transpose(1, 0, 2, 3).reshape(4, G * PGQ, PG)]
    grads = [g.reshape(4, 2, g.shape[1] // 2, g.shape[2]) for g in grads]
    vectors = ["pool_scale", "sgu_ln_g", "sgu_ln_b", "ln_g", "ln_b", "ple_gate_b"]
    rows = [d_pool_scale, d_sgu_g, d_sgu_b, d_ln_g, d_ln_b, d_gate_b,
            jnp.pad(jnp.reshape((0.5 / D) * jnp.sum(ssq), (1, 1)), ((0, 1), (0, D - 1))),
            jnp.pad(d_sgu_bias_t.T, ((0, 4), (0, D - CHUNK)))]
    small = jnp.concatenate(rows, axis=0)
    d_sgu_w = d_sgu_w.reshape(-1, CHUNK)
    q = 2 * lax.axis_index("x") + lax.axis_index("y")
    *parts, small_chip, sw_chip = _pair_reduce(grads, [small, d_sgu_w])
    d_w_in, *slots, small_slots, sw_slots = _weight_backward(d_h, xt, q, parts + [small_chip, sw_chip], tm_acc)
    *halves, small_total, sw_total = _sum_fours(parts, slots, [small_chip, sw_chip], [small_slots, sw_slots], q)
    sibling_halves = _join_halves_with_sibling(halves)
    loss = small_total[6, 0]

    big_names = ["w_out", "ple_gate_w", "ple_w", "pool_w"]
    given = dict(w_in=(w_in, m_w_in, v_w_in), w_out=(w_out, m_w_out, v_w_out),
                 ple_gate_w=(ple_gate_w, m_ple_gate_w, v_ple_gate_w), ple_w=(ple_w, m_ple_w, v_ple_w),
                 pool_w=(pool_w, m_pool_w, v_pool_w), pool_scale=(pool_scale, m_pool_scale, v_pool_scale),
                 sgu_ln_g=(sgu_ln_g, m_sgu_ln_g, v_sgu_ln_g), sgu_ln_b=(sgu_ln_b, m_sgu_ln_b, v_sgu_ln_b),
                 sgu_w=(sgu_w, m_sgu_w, v_sgu_w), sgu_b=(sgu_b, m_sgu_b, v_sgu_b), ln_g=(ln_g, m_ln_g, v_ln_g),
                 ln_b=(ln_b, m_ln_b, v_ln_b), ple_gate_b=(ple_gate_b, m_ple_gate_b, v_ple_gate_b))
    grad, delta, new_m, new_v = {}, {}, {}, {}
    grad["w_in"], delta["w_in"], new_m["w_in"], new_v["w_in"] = (
        t[None] for t in _adamw(w_in[0], d_w_in, m_w_in[0], v_w_in[0], "adamw_w_in"))
    flat = [(2 * g.shape[0], g.shape[1]) for g in halves]
    small_items = [(*given[n], (0, slice(k, k + 1), slice(None)), False) for k, n in enumerate(vectors)]
    small_items.append((*(t[0] for t in given["sgu_b"]), (0, slice(8, 8 + sgu_b.shape[1]), slice(0, CHUNK)), False))
    small_items.append((*(t.reshape(-1, CHUNK) for t in given["sgu_w"]), (1, slice(None), slice(None)), True))
    outs = _adamw_joined(
        [given[n][0].reshape(f) for n, f in zip(big_names, flat)], halves, sibling_halves,
        [given[n][1].reshape(f) for n, f in zip(big_names, flat)],
        [given[n][2].reshape(f) for n, f in zip(big_names, flat)], small_items, [small_total, sw_total], c)
    for name, four in zip(big_names + vectors + ["sgu_b", "sgu_w"], outs):
        grad[name], delta[name], new_m[name], new_v[name] = (t.reshape(given[name][0].shape) for t in four)

    order = ["w_in", "pool_w", "pool_scale", "sgu_ln_g", "sgu_ln_b", "sgu_w", "sgu_b", "w_out", "ln_g", "ln_b",
             "ple_w", "ple_gate_w", "ple_gate_b"]
    return (loss, d_x[None], *[grad[n] for n in order], *[delta[n] for n in order],
            *[new_m[n] for n in order], *[new_v[n] for n in order])
```

```python
import functools
import math

import jax
import jax.numpy as jnp
from jax import lax
from jax.experimental import pallas as pl
from jax.experimental.pallas import tpu as pltpu

F32, BF16 = jnp.float32, jnp.bfloat16
MESH = pl.DeviceIdType.MESH
ANY = pl.BlockSpec(memory_space=pl.ANY)

POOL_WINDOWS = (2, 4, 8, 16)
HALO = 16
CHUNK = 128
MXU_COLS = 256
LN_EPS = 1e-5
DEEPNORM_ALPHA = 2.0 ** 0.25
ADAM_LR, ADAM_B1, ADAM_B2, ADAM_EPS, ADAM_WD, ADAM_STEP = 1e-3, 0.9, 0.999, 1e-8, 0.01, 10
VMEM_LIMIT = 56 * 1024 * 1024
GELU_K = math.sqrt(2.0 / math.pi)
GELU_C = 0.044715
SAVED = {"pooled": 0, "gelu_u": 1, "dgelu_u": 2, "vhat": 3, "rstd_dgelu_v": 4, "silu": 5, "dsilu": 7}
SAVED_WIDTH = 9


def _mm(a, b):
    return jnp.dot(a, b, preferred_element_type=F32)


def _mm_nt(a, b):
    return lax.dot_general(a, b, (((1,), (1,)), ((), ())), preferred_element_type=F32)


def _mm_tn(a, b):
    return lax.dot_general(a, b, (((0,), (0,)), ((), ())), preferred_element_type=F32)


def _gelu_and_grad(x):
    x2 = x * x
    t = jnp.tanh(x * (GELU_K + (GELU_K * GELU_C) * x2))
    hx = 0.5 * x
    g = hx + hx * t
    dg = (0.5 + 0.5 * t) + (hx - hx * t * t) * (GELU_K + (3.0 * GELU_K * GELU_C) * x2)
    return g, dg


def _silu_and_grad(z):
    sig = jax.nn.sigmoid(z)
    zs = z * sig
    return zs, sig + zs * (1.0 - sig)


def _norm_rows(x):
    mu = jnp.mean(x, axis=-1, keepdims=True)
    xc = x - mu
    var = jnp.mean(xc * xc, axis=-1, keepdims=True)
    rstd = lax.rsqrt(var + LN_EPS)
    return xc * rstd, rstd


def _norm_rows_bwd(dxhat, xhat, rstd):
    m1 = jnp.mean(dxhat, axis=-1, keepdims=True)
    m2 = jnp.mean(dxhat * xhat, axis=-1, keepdims=True)
    return rstd * (dxhat - m1 - xhat * m2)


def _inv_count(row0, rows, w):
    t = row0 + lax.broadcasted_iota(jnp.int32, (rows, 1), 0)
    return 1.0 / jnp.minimum(t + 1, w).astype(F32)


def _causal_window_sum(ext, w):
    s, sh = ext, 1
    while sh < w:
        s = s + pltpu.roll(s, sh, axis=0)
        sh *= 2
    return s[HALO:, :]


def _anticausal_window_sum(ext, w):
    n, s, sh = ext.shape[0], ext, 1
    while sh < w:
        s = s + pltpu.roll(s, n - sh, axis=0)
        sh *= 2
    return s[: n - HALO, :]


def _place():
    return lax.axis_index("x"), lax.axis_index("y"), lax.axis_index("c")


def _gather_weights(shards):
    n = len(shards)
    piece = [s.shape[0] // 4 for s in shards]

    def body(*refs):
        wide, dsts, srcs = refs[:n], refs[n:2 * n], refs[2 * n:3 * n]
        send_sems, recv_sems, local_sems = refs[3 * n:]
        for k in range(n):
            for r0 in range(0, 4 * piece[k], CHUNK):
                srcs[k][r0:r0 + CHUNK, :] = wide[k][r0:r0 + CHUNK, :].astype(BF16)
        x, y, c = _place()
        me, sibling = (x, y, c), (x, y, 1 - c)
        across_x, across_y = (1 - x, y, c), (x, 1 - y, c)
        q, qx, qy, qf = 2 * x + y, 2 * (1 - x) + y, 2 * x + (1 - y), 2 * (1 - x) + (1 - y)

        def rows(ref, cc, p, k):
            return ref.at[pl.ds((2 * cc + p) * piece[k], piece[k])]

        def copy(k, sem, qq, cc, p, to, own=False):
            landing = rows(dsts[k].at[qq], cc, p, k)
            return pltpu.make_async_remote_copy(
                src_ref=rows(srcs[k], cc, p, k) if own else landing, dst_ref=landing,
                send_sem=send_sems.at[12 * k + sem], recv_sem=recv_sems.at[12 * k + sem],
                device_id=to, device_id_type=MESH)

        started = []

        def go(cp):
            cp.start()
            started.append(cp)

        mine = [pltpu.make_async_copy(srcs[k], dsts[k].at[q], local_sems.at[k]) for k in range(n)]
        for cp in mine:
            cp.start()
        for k in range(n):
            for p in range(2):
                go(copy(k, p, q, c, p, across_x, own=True))
                go(copy(k, 2 + p, q, c, p, across_y, own=True))
        for k in range(n):
            copy(k, 0, qx, c, 0, me).wait_recv()
            go(copy(k, 4, qx, c, 0, across_y))
            go(copy(k, 6, qx, c, 0, sibling))
            copy(k, 3, qy, c, 1, me).wait_recv()
            go(copy(k, 5, qy, c, 1, across_x))
            go(copy(k, 9, qy, c, 1, sibling))
        for k in range(n):
            copy(k, 1, qx, c, 1, me).wait_recv()
            go(copy(k, 7, qx, c, 1, sibling))
            copy(k, 2, qy, c, 0, me).wait_recv()
            go(copy(k, 8, qy, c, 0, sibling))
        for k in range(n):
            copy(k, 4, qf, c, 0, me).wait_recv()
            go(copy(k, 10, qf, c, 0, sibling))
            copy(k, 5, qf, c, 1, me).wait_recv()
            go(copy(k, 11, qf, c, 1, sibling))
        for k in range(n):
            for sem, qq, p in ((6, qx, 0), (7, qx, 1), (8, qy, 0), (9, qy, 1), (10, qf, 0), (11, qf, 1)):
                copy(k, sem, qq, 1 - c, p, me).wait_recv()
        for cp in started:
            cp.wait_send()
        for cp in mine:
            cp.wait()

    return pl.pallas_call(
        body, name="gather_weights",
        out_shape=[jax.ShapeDtypeStruct((4,) + s.shape, BF16) for s in shards],
        in_specs=[pl.BlockSpec(memory_space=pltpu.VMEM)] * n, out_specs=[ANY] * n,
        scratch_shapes=[pltpu.VMEM(s.shape, BF16) for s in shards]
        + [pltpu.SemaphoreType.DMA((12 * n,)), pltpu.SemaphoreType.DMA((12 * n,)), pltpu.SemaphoreType.DMA((n,))],
        compiler_params=pltpu.CompilerParams(vmem_limit_bytes=VMEM_LIMIT),
    )(*shards)


def _direct_gather_copies(srcs, dsts, send_sems, recv_sems):
    x, y, c = _place()
    q = 2 * x + y
    sends, recvs = [], []
    for k, (src, dst) in enumerate(zip(srcs, dsts)):
        half = src.shape[0] // 2
        for j, chip in enumerate([(1 - x, y), (x, 1 - y), (1 - x, 1 - y)]):
            for core in range(2):
                sends.append(pltpu.make_async_remote_copy(
                    src_ref=src.at[pl.ds(c * half, half)], dst_ref=dst.at[q, pl.ds(c * half, half)],
                    send_sem=send_sems.at[6 * k + 2 * j + core], recv_sem=recv_sems.at[6 * k + 2 * j + c],
                    device_id=(*chip, core), device_id_type=MESH))
                landed = dst.at[2 * chip[0] + chip[1], pl.ds(core * half, half)]
                recvs.append(pltpu.make_async_remote_copy(
                    src_ref=landed, dst_ref=landed, send_sem=send_sems.at[6 * k + 2 * j + core],
                    recv_sem=recv_sems.at[6 * k + 2 * j + core], device_id=(x, y, c), device_id_type=MESH))
    return sends, recvs


def _scatter_copies(srcs, dsts, send_sems, recv_sems):
    x, y, c = _place()
    copies = []
    for j, chip in enumerate([(1 - x, y), (x, 1 - y), (1 - x, 1 - y)]):
        for k, (src, dst) in enumerate(zip(srcs, dsts)):
            copies.append(pltpu.make_async_remote_copy(
                src_ref=src.at[2 * chip[0] + chip[1]] if len(src.shape) == 3 else src, dst_ref=dst.at[j],
                send_sem=send_sems.at[3 * k + j], recv_sem=recv_sems.at[3 * k + j],
                device_id=(*chip, c), device_id_type=MESH))
    return copies


def _scatter_shapes(parts):
    return [jax.ShapeDtypeStruct((3,) + (p.shape[1:] if p.ndim == 3 else p.shape), p.dtype) for p in parts]


def _join_halves_with_sibling(halves):
    n = len(halves)

    def body(*refs):
        srcs, dsts = refs[:n], refs[n:2 * n]
        send_sems, recv_sems = refs[2 * n:]
        x, y, c = _place()
        copies = [pltpu.make_async_remote_copy(
            src_ref=srcs[k], dst_ref=dsts[k], send_sem=send_sems.at[k], recv_sem=recv_sems.at[k],
            device_id=(x, y, 1 - c), device_id_type=MESH) for k in range(n)]
        for cp in copies:
            cp.start()
        for cp in copies:
            cp.wait()

    return pl.pallas_call(
        body, name="join_halves",
        out_shape=[jax.ShapeDtypeStruct(h.shape, h.dtype) for h in halves],
        in_specs=[ANY] * n, out_specs=[ANY] * n,
        scratch_shapes=[pltpu.SemaphoreType.DMA((n,)), pltpu.SemaphoreType.DMA((n,))],
    )(*halves)


def _row_block(rows, cols, n_arrays):
    cap = max(8, (VMEM_LIMIT // 2) // (8 * n_arrays * cols))
    rb = rows
    while rb > cap and rb % 2 == 0:
        rb //= 2
    return rb


def _scalar(value):
    return jnp.reshape(value, (1,)).astype(jnp.int32)


def _whole(a):
    return pl.BlockSpec(a.shape, lambda i, s_ref: (0,) * a.ndim)


def _sum_fours(parts, slots, wholes, wholes_slots, q):
    n, nw = len(parts), len(wholes)

    def four(own, s):
        return (own[...].astype(F32) + s[0].astype(F32)) + (s[1].astype(F32) + s[2].astype(F32))

    def body(q_ref, *refs):
        p, s, w, ws = refs[:n], refs[n:2 * n], refs[2 * n:2 * n + nw], refs[2 * n + nw:2 * n + 2 * nw]
        o, wo = refs[2 * n + 2 * nw:3 * n + 2 * nw], refs[3 * n + 2 * nw:]
        for k in range(n):
            o[k][...] = four(p[k], s[k])
        for k in range(nw):
            wo[k][...] = four(w[k], ws[k])

    return pl.pallas_call(
        body, name="sum_fours",
        out_shape=[jax.ShapeDtypeStruct(a.shape[1:], F32) for a in parts] + [jax.ShapeDtypeStruct(a.shape, F32) for a in wholes],
        grid_spec=pltpu.PrefetchScalarGridSpec(
            num_scalar_prefetch=1, grid=(2,),
            in_specs=[pl.BlockSpec((None, a.shape[1] // 2, a.shape[2]), lambda i, q_ref: (q_ref[0], i, 0)) for a in parts]
            + [pl.BlockSpec((3, a.shape[1] // 2, a.shape[2]), lambda i, q_ref: (0, i, 0)) for a in slots]
            + [_whole(a) for a in wholes + wholes_slots],
            out_specs=[pl.BlockSpec((a.shape[1] // 2, a.shape[2]), lambda i, q_ref: (i, 0)) for a in parts]
            + [_whole(a) for a in wholes]),
        compiler_params=pltpu.CompilerParams(vmem_limit_bytes=VMEM_LIMIT),
    )(_scalar(q), *parts, *slots, *wholes, *wholes_slots)


def _adamw_math(w, g, m, v):
    nm = ADAM_B1 * m + (1.0 - ADAM_B1) * g
    nv = ADAM_B2 * v + (1.0 - ADAM_B2) * (g * g)
    m_hat = nm / (1.0 - ADAM_B1 ** ADAM_STEP)
    v_hat = nv / (1.0 - ADAM_B2 ** ADAM_STEP)
    return -ADAM_LR * (m_hat / (jnp.sqrt(v_hat) + ADAM_EPS) + ADAM_WD * w), nm, nv


def _adamw(w, g, m, v, name):
    rows, cols = w.shape
    rb = _row_block(rows, cols, 8)

    def body(w_ref, g_ref, m_ref, v_ref, go_ref, d_ref, nm_ref, nv_ref):
        go_ref[...] = g_ref[...]
        d_ref[...], nm_ref[...], nv_ref[...] = _adamw_math(w_ref[...], g_ref[...], m_ref[...], v_ref[...])

    spec = pl.BlockSpec((rb, cols), lambda r: (r, 0))
    out = jax.ShapeDtypeStruct(w.shape, F32)
    return pl.pallas_call(body, name=name, out_shape=[out] * 4, grid=(rows // rb,),
                          in_specs=[spec] * 4, out_specs=[spec] * 4,
                          compiler_params=pltpu.CompilerParams(vmem_limit_bytes=VMEM_LIMIT))(w, g, m, v)


def _adamw_joined(ws, g_mine, g_sibling, ms, vs, small, small_grads, c):
    n, ns, ng = len(ws), len(small), len(small_grads)

    def body(c_ref, *refs):
        big, tot, sm = refs[:5 * n], refs[5 * n:5 * n + ng], refs[5 * n + ng:5 * n + ng + 3 * ns]
        outs = refs[5 * n + ng + 3 * ns:]
        mine = c_ref[0] == pl.program_id(0)
        for k in range(n):
            w, gm, gs, m, v = big[5 * k:5 * k + 5]
            g = jnp.where(mine, gm[...], gs[...])
            outs[4 * k][...] = g
            outs[4 * k + 1][...], outs[4 * k + 2][...], outs[4 * k + 3][...] = _adamw_math(w[...], g, m[...], v[...])
        for k in range(ns):
            w, m, v = sm[3 * k:3 * k + 3]
            which, rows, cols = small[k][3]
            g = tot[which][rows, cols]
            if small[k][4]:
                i_pos = lax.broadcasted_iota(jnp.int32, g.shape, 0) % CHUNK
                g = jnp.where(lax.broadcasted_iota(jnp.int32, g.shape, 1) <= i_pos, g, 0.0)
            o = outs[4 * (n + k):4 * (n + k) + 4]
            o[0][...] = g
            o[1][...], o[2][...], o[3][...] = _adamw_math(w[...], g, m[...], v[...])

    def half(a):
        return pl.BlockSpec((a.shape[0] // 2, a.shape[1]), lambda hf, c_ref: (hf, 0))

    in_specs, operands = [], []
    for k in range(n):
        in_specs += [half(ws[k]), _whole(g_mine[k]), _whole(g_sibling[k]), half(ms[k]), half(vs[k])]
        operands += [ws[k], g_mine[k], g_sibling[k], ms[k], vs[k]]
    operands += list(small_grads) + [a for item in small for a in item[:3]]
    in_specs += [_whole(a) for a in operands[5 * n:]]
    outs = pl.pallas_call(
        body, name="adamw_joined",
        out_shape=[jax.ShapeDtypeStruct(w.shape, F32) for w in ws for _ in range(4)]
        + [jax.ShapeDtypeStruct(item[0].shape, F32) for item in small for _ in range(4)],
        grid_spec=pltpu.PrefetchScalarGridSpec(
            num_scalar_prefetch=1, grid=(2,),
            in_specs=in_specs,
            out_specs=[half(w) for w in ws for _ in range(4)] + [_whole(item[0]) for item in small for _ in range(4)]),
        compiler_params=pltpu.CompilerParams(vmem_limit_bytes=VMEM_LIMIT),
    )(_scalar(c), *operands)
    return [outs[4 * k:4 * k + 4] for k in range(n + ns)]


def _front_forward(x, w_in, pool_w, pool_scale, sgu_g, sgu_b, sgu_wm, sgu_bias_t, later_shards, tm):
    T, D = x.shape
    nq, _, cq = w_in.shape
    G, PG = pool_w.shape[0], pool_w.shape[1]
    nt = T // tm
    bpd = D // PG
    nl = len(later_shards)

    def body(x_ref, win_any, pw_any, ps_ref, lg_ref, lb_ref, sw_ref, sb_ref, *refs):
        shards_any, (keep_ref, y_ref, xt_ref), gathered = refs[:nl], refs[nl:nl + 3], refs[nl + 3:2 * nl + 3]
        wide, narrow = refs[2 * nl + 3:3 * nl + 3], refs[3 * nl + 3:4 * nl + 3]
        win_v, pw_v, carry, sems, load_sems, own_sems, send_sems, recv_sems = refs[4 * nl + 3:]
        i = pl.program_id(0)
        own_quarter = 2 * lax.axis_index("x") + lax.axis_index("y")

        def own_copies():
            return [pltpu.make_async_copy(narrow[k], gathered[k].at[own_quarter], own_sems.at[k]) for k in range(nl)]

        @pl.when(i == 0)
        def _():
            c1 = pltpu.make_async_copy(win_any, win_v, sems.at[0])
            c2 = pltpu.make_async_copy(pw_any, pw_v, sems.at[1])
            loads = [pltpu.make_async_copy(shards_any[k], wide[k], load_sems.at[k]) for k in range(nl)]
            for cp in [c1, c2] + loads:
                cp.start()
            carry[...] = jnp.zeros_like(carry)
            for k in range(nl):
                loads[k].wait()
                for r0 in range(0, wide[k].shape[0], CHUNK):
                    narrow[k][r0:r0 + CHUNK, :] = wide[k][r0:r0 + CHUNK, :].astype(BF16)
            for cp in own_copies() + _direct_gather_copies(narrow, gathered, send_sems, recv_sems)[0]:
                cp.start()
            c1.wait()
            c2.wait()

        xb = x_ref[...].astype(BF16)
        xt_ref[...] = x_ref[...].T.astype(BF16)

        def h_block(j):
            qq, off = divmod(j * PG, cq)
            return _mm(xb, win_v[qq, :, off:off + PG])

        def keep(part, col, value):
            keep_ref[:, SAVED[part] * D + col:SAVED[part] * D + col + PG] = value.astype(BF16)

        def ahead(stage):
            if stage < G:
                return h_block(stage), h_block(3 * bpd + stage)
            if stage < G + bpd:
                hd = stage - G
                return h_block(bpd + hd), h_block(2 * bpd + hd), h_block(4 * bpd + hd)
            return None

        blocks = ahead(0)

        for g, w in enumerate(POOL_WINDOWS):
            sl = slice(g * PG, (g + 1) * PG)
            a, z = blocks
            blocks = ahead(g + 1)
            ext = jnp.concatenate([carry[:, sl], a], axis=0)
            carry[:, sl] = a[tm - HALO:, :]
            pooled = (_causal_window_sum(ext, w) * _inv_count(i * tm, tm, w) - a).astype(BF16)
            mixed = _mm(pooled, pw_v[g])
            zs, dzs = _silu_and_grad(z)
            keep("pooled", g * PG, pooled)
            keep("silu", g * PG, zs)
            keep("dsilu", g * PG, dzs)
            y_ref[:, sl] = (mixed * ps_ref[:, sl] * zs).astype(BF16)

        for hd in range(bpd):
            sl = slice(hd * PG, (hd + 1) * PG)
            u, v, z = blocks
            blocks = ahead(G + hd + 1)
            ug, dug = _gelu_and_grad(u)
            vg, dvg = _gelu_and_grad(v)
            vhat, rstd = _norm_rows(vg)
            zs, dzs = _silu_and_grad(z)
            keep("gelu_u", hd * PG, ug)
            keep("dgelu_u", hd * PG, dug)
            keep("vhat", hd * PG, vhat)
            keep("rstd_dgelu_v", hd * PG, rstd * dvg)
            keep("silu", D + hd * PG, zs)
            keep("dsilu", D + hd * PG, dzs)
            vn = (vhat * lg_ref[:, sl] + lb_ref[:, sl]).astype(BF16)
            gated = ug * zs
            for n in range(tm // CHUNK):
                rs = slice(n * CHUNK, (n + 1) * CHUNK)
                sv = _mm(sw_ref[hd], vn[rs, :]) + sb_ref[:, hd:hd + 1]
                y_ref[rs, D + hd * PG:D + (hd + 1) * PG] = (gated[rs, :] * sv).astype(BF16)

        @pl.when(i == nt - 1)
        def _():
            sends, recvs = _direct_gather_copies(narrow, gathered, send_sems, recv_sems)
            for cp in sends:
                cp.wait_send()
            for cp in recvs:
                cp.wait_recv()
            for cp in own_copies():
                cp.wait()

    vec = pl.BlockSpec((1, D), lambda i: (0, 0))
    return pl.pallas_call(
        body, name="front_forward",
        out_shape=[jax.ShapeDtypeStruct((T, SAVED_WIDTH * D), BF16), jax.ShapeDtypeStruct((T, 2 * D), BF16),
                   jax.ShapeDtypeStruct((D, T), BF16)]
        + [jax.ShapeDtypeStruct((4,) + s.shape, BF16) for s in later_shards],
        grid=(nt,),
        in_specs=[pl.BlockSpec((tm, D), lambda i: (i, 0)), ANY, ANY, vec, vec, vec,
                  pl.BlockSpec(sgu_wm.shape, lambda i: (0, 0, 0)), pl.BlockSpec(sgu_bias_t.shape, lambda i: (0, 0))]
        + [ANY] * nl,
        out_specs=[pl.BlockSpec((tm, SAVED_WIDTH * D), lambda i: (i, 0)), pl.BlockSpec((tm, 2 * D), lambda i: (i, 0)),
                   pl.BlockSpec((D, tm), lambda i: (0, i))] + [ANY] * nl,
        scratch_shapes=[pltpu.VMEM(s.shape, F32) for s in later_shards]
        + [pltpu.VMEM(s.shape, BF16) for s in later_shards]
        + [pltpu.VMEM(w_in.shape, BF16), pltpu.VMEM(pool_w.shape, BF16), pltpu.VMEM((HALO, D), F32),
           pltpu.SemaphoreType.DMA((2,)), pltpu.SemaphoreType.DMA((nl,)), pltpu.SemaphoreType.DMA((nl,)),
           pltpu.SemaphoreType.DMA((6 * nl,)), pltpu.SemaphoreType.DMA((6 * nl,))],
        compiler_params=pltpu.CompilerParams(dimension_semantics=("arbitrary",), vmem_limit_bytes=VMEM_LIMIT),
    )(x, w_in, pool_w, pool_scale, sgu_g, sgu_b, sgu_wm, sgu_bias_t, *later_shards)


def _tail(y, x, p, target, w_out, w_gate, w_ple, ln_g, ln_b, gate_b, tm):
    T, D = x.shape
    K = p.shape[1]
    nq, _, cq = w_ple.shape
    nt = T // tm

    def body(y_ref, x_ref, p_ref, t_ref, wout_any, wg_any, wp_any, lng_ref, lnb_ref, bg_ref,
             dxp_ref, dy_ref, dwout_any, dwg_any, dwp_any, dlng_ref, dlnb_ref, dbg_ref, ssq_ref,
             wout_v, wg_v, wp_v, dwout_acc, dwg_acc, dwp_acc, sems):
        i = pl.program_id(0)

        @pl.when(i == 0)
        def _():
            loads = [pltpu.make_async_copy(s, d, sems.at[k])
                     for k, (s, d) in enumerate(((wout_any, wout_v), (wg_any, wg_v), (wp_any, wp_v)))]
            for cp in loads:
                cp.start()
            for ref in (dwout_acc, dwg_acc, dwp_acc, dlng_ref, dlnb_ref, dbg_ref, ssq_ref):
                ref[...] = jnp.zeros_like(ref)
            for cp in loads:
                cp.wait()

        halves = [slice(k * tm // 2, (k + 1) * tm // 2) for k in range(2)]

        def total(parts):
            return sum(jnp.sum(part, axis=0, keepdims=True) for part in parts)

        yb = [y_ref[r, :] for r in halves]
        pb = [p_ref[r, :].astype(BF16) for r in halves]
        mix = [_mm(v, wout_v[...]) for v in yb]
        normed = [_norm_rows(DEEPNORM_ALPHA * x_ref[r, :] + m) for r, m in zip(halves, mix)]
        xhat, rstd = [n[0] for n in normed], [n[1] for n in normed]
        x1 = [v * lng_ref[...] + lnb_ref[...] for v in xhat]
        x1b = [v.astype(BF16) for v in x1]
        gate = [jax.nn.sigmoid(_mm(v, wg_v[...]) + bg_ref[...]) for v in x1b]
        e = [jnp.concatenate([_mm(v, wp_v[qq]) for qq in range(nq)], axis=1) for v in pb]
        diff = [a + g * ee - t_ref[r, :] for a, g, ee, r in zip(x1, gate, e, halves)]
        ssq_ref[...] += total([d * d for d in diff])

        dout = [d * (1.0 / D) for d in diff]
        d_e = [(do * g).astype(BF16) for do, g in zip(dout, gate)]
        dgl = [do * ee * g * (1.0 - g) for do, ee, g in zip(dout, e, gate)]
        dglb = [v.astype(BF16) for v in dgl]
        dbg_ref[...] += total(dgl)
        pb_t, d_e_t, x1b_t, dglb_t = (jnp.concatenate(v, axis=0) for v in (pb, d_e, x1b, dglb))
        for qq in range(nq):
            dwp_acc[qq] += _mm_tn(pb_t, d_e_t[:, qq * cq:(qq + 1) * cq])
        for c0 in range(0, D, MXU_COLS):
            dwg_acc[:, c0:c0 + MXU_COLS] += _mm_tn(x1b_t, dglb_t[:, c0:c0 + MXU_COLS])
        d_x1 = [do + _mm_nt(dg, wg_v[...]) for do, dg in zip(dout, dglb)]
        dlng_ref[...] += total([d * xh for d, xh in zip(d_x1, xhat)])
        dlnb_ref[...] += total(d_x1)
        d_r = [_norm_rows_bwd(d * lng_ref[...], xh, rs) for d, xh, rs in zip(d_x1, xhat, rstd)]
        drb = [v.astype(BF16) for v in d_r]
        for r, v in zip(halves, d_r):
            dxp_ref[r, :] = DEEPNORM_ALPHA * v
        for c0 in range(0, 2 * D, 2 * MXU_COLS):
            for r, v in zip(halves, drb):
                dy_ref[r, c0:c0 + 2 * MXU_COLS] = _mm_nt(v, wout_v[c0:c0 + 2 * MXU_COLS, :]).astype(BF16)

        drb_t = jnp.concatenate(drb, axis=0)
        for c0 in range(0, D, MXU_COLS):
            dwout_acc[:, c0:c0 + MXU_COLS] += _mm_tn(y_ref[...], drb_t[:, c0:c0 + MXU_COLS])

        @pl.when(i == nt - 1)
        def _():
            stores = [pltpu.make_async_copy(s, d, sems.at[k])
                      for k, (s, d) in enumerate(((dwout_acc, dwout_any), (dwg_acc, dwg_any), (dwp_acc, dwp_any)))]
            for cp in stores:
                cp.start()
            for cp in stores:
                cp.wait()

    vec = pl.BlockSpec((1, D), lambda i: (0, 0))
    vec_shape = jax.ShapeDtypeStruct((1, D), F32)

    def tile(cols):
        return pl.BlockSpec((tm, cols), lambda i: (i, 0))

    return pl.pallas_call(
        body, name="tail",
        out_shape=[jax.ShapeDtypeStruct((T, D), F32), jax.ShapeDtypeStruct((T, 2 * D), BF16),
                   jax.ShapeDtypeStruct(w_out.shape, F32), jax.ShapeDtypeStruct(w_gate.shape, F32),
                   jax.ShapeDtypeStruct(w_ple.shape, F32), vec_shape, vec_shape, vec_shape, vec_shape],
        grid=(nt,),
        in_specs=[tile(2 * D), tile(D), tile(K), tile(D), ANY, ANY, ANY, vec, vec, vec],
        out_specs=[tile(D), tile(2 * D), ANY, ANY, ANY, vec, vec, vec, vec],
        scratch_shapes=[pltpu.VMEM(w_out.shape, BF16), pltpu.VMEM(w_gate.shape, BF16), pltpu.VMEM(w_ple.shape, BF16),
                        pltpu.VMEM(w_out.shape, F32), pltpu.VMEM(w_gate.shape, F32), pltpu.VMEM(w_ple.shape, F32),
                        pltpu.SemaphoreType.DMA((3,))],
        compiler_params=pltpu.CompilerParams(dimension_semantics=("arbitrary",), vmem_limit_bytes=VMEM_LIMIT),
    )(y, x, p, target, w_out, w_gate, w_ple, ln_g, ln_b, gate_b)


def _front_backward(kept, d_y, dx_part, w_in, pool_w, pool_scale, sgu_g, sgu_b, sgu_wm, sgu_bias_t, tm):
    T = kept.shape[0]
    D = kept.shape[1] // SAVED_WIDTH
    nq, _, cq = w_in.shape
    G, PG = pool_w.shape[0], pool_w.shape[1]
    nt = T // tm

    def tile_of(i):
        return nt - 1 - jnp.minimum(i, nt - 1)

    def body(kept_ref, dy_ref, dxp_ref, win_any, pw_ref, ps_ref, lg_ref, lb_ref, sw_ref, sb_ref,
             dh_ref, dx_ref, dpw_ref, dps_ref, dlg_ref, dlb_ref, dsw_ref, dsb_ref, win_v, dh_keep, carry, sems):
        i = pl.program_id(0)
        ti = tile_of(i)

        def saved(part, col, rows=slice(None)):
            return kept_ref[rows, SAVED[part] * D + col:SAVED[part] * D + col + PG]

        @pl.when(i == 0)
        def _():
            cp = pltpu.make_async_copy(win_any, win_v, sems.at[0])
            cp.start()
            carry[...] = jnp.zeros_like(carry)
            for ref in (dpw_ref, dps_ref, dlg_ref, dlb_ref, dsw_ref, dsb_ref):
                ref[...] = jnp.zeros_like(ref)
            cp.wait()

        def dx_columns(r0):
            dx = dxp_ref[:, r0:r0 + MXU_COLS]
            for qq in range(nq):
                dx = dx + _mm_nt(dh_keep[(i + 1) % 2, :, qq * cq:(qq + 1) * cq], win_v[qq, r0:r0 + MXU_COLS, :])
            dx_ref[:, r0:r0 + MXU_COLS] = dx

        dx_chunks = list(range(0, D, MXU_COLS))
        stages = G + D // PG

        def pool_stage(g, w):
            sl = slice(g * PG, (g + 1) * PG)
            pooled = saved("pooled", g * PG)
            mixed = _mm(pooled, pw_ref[g])
            dy = dy_ref[:, sl].astype(F32)
            d_ypool = dy * saved("silu", g * PG).astype(F32)
            dh_ref[:, 3 * D + g * PG:3 * D + (g + 1) * PG] = (
                dy * (mixed * ps_ref[:, sl]) * saved("dsilu", g * PG).astype(F32)).astype(BF16)
            dps_ref[:, sl] += jnp.sum(d_ypool * mixed, axis=0, keepdims=True)
            d_mixed = (d_ypool * ps_ref[:, sl]).astype(BF16)
            dpw_ref[g] += _mm_tn(pooled, d_mixed)
            d_pooled = _mm_nt(d_mixed, pw_ref[g])
            scaled = d_pooled * _inv_count(ti * tm, tm, w)
            after = jnp.concatenate([scaled, carry[:, sl]], axis=0)
            carry[:, sl] = scaled[:HALO, :]
            dh_ref[:, sl] = (_anticausal_window_sum(after, w) - d_pooled).astype(BF16)

        def gating_stage(hd):
            sl = slice(hd * PG, (hd + 1) * PG)
            vhat = saved("vhat", hd * PG).astype(F32)
            vn = (vhat * lg_ref[:, sl] + lb_ref[:, sl]).astype(BF16)
            chunk_rows = [slice(n * CHUNK, (n + 1) * CHUNK) for n in range(tm // CHUNK)]
            vn_wide = jnp.concatenate([vn[rs, :] for rs in chunk_rows], axis=1)
            sv_wide = _mm(sw_ref[hd], vn_wide) + sb_ref[:, hd:hd + 1]
            d_sv_parts = []
            for n, rs in enumerate(chunk_rows):
                sv = sv_wide[:, n * PG:(n + 1) * PG]
                ug = saved("gelu_u", hd * PG, rs).astype(F32)
                dy = dy_ref[rs, D + hd * PG:D + (hd + 1) * PG].astype(F32)
                d_ysgu = dy * saved("silu", D + hd * PG, rs).astype(F32)
                dh_ref[rs, 4 * D + hd * PG:4 * D + (hd + 1) * PG] = (
                    dy * (ug * sv) * saved("dsilu", D + hd * PG, rs).astype(F32)).astype(BF16)
                dh_ref[rs, D + hd * PG:D + (hd + 1) * PG] = (
                    d_ysgu * sv * saved("dgelu_u", hd * PG, rs).astype(F32)).astype(BF16)
                d_sv_parts.append(d_ysgu * ug)
            d_sv_wide = jnp.concatenate(d_sv_parts, axis=1)
            dsb_ref[:, hd:hd + 1] += jnp.sum(d_sv_wide, axis=1, keepdims=True)
            d_svb = d_sv_wide.astype(BF16)
            dsw_ref[hd] += _mm_nt(d_svb, vn_wide)
            d_vn_wide = _mm_tn(sw_ref[hd], d_svb)
            d_vn = jnp.concatenate([d_vn_wide[:, n * PG:(n + 1) * PG] for n in range(len(chunk_rows))], axis=0)
            dlg_ref[:, sl] += jnp.sum(d_vn * vhat, axis=0, keepdims=True)
            dlb_ref[:, sl] += jnp.sum(d_vn, axis=0, keepdims=True)
            d_vg = _norm_rows_bwd(d_vn * lg_ref[:, sl], vhat, saved("rstd_dgelu_v", hd * PG).astype(F32))
            dh_ref[:, 2 * D + hd * PG:2 * D + (hd + 1) * PG] = d_vg.astype(BF16)

        def work(make_dx, make_dh):
            for stage in range(stages):
                if make_dx:
                    for r0 in dx_chunks[stage * len(dx_chunks) // stages:(stage + 1) * len(dx_chunks) // stages]:
                        dx_columns(r0)
                if make_dh and stage < G:
                    pool_stage(stage, POOL_WINDOWS[stage])
                elif make_dh:
                    gating_stage(stage - G)
            if make_dh:
                dh_keep[i % 2] = dh_ref[...]

        pl.when(i == 0)(functools.partial(work, False, True))
        pl.when((i > 0) & (i < nt))(functools.partial(work, True, True))
        pl.when(i == nt)(functools.partial(work, True, False))

    vec = pl.BlockSpec((1, D), lambda i: (0, 0))
    vec_shape = jax.ShapeDtypeStruct((1, D), F32)

    def whole(shape):
        return pl.BlockSpec(shape, lambda i: (0,) * len(shape))

    return pl.pallas_call(
        body, name="front_backward",
        out_shape=[jax.ShapeDtypeStruct((T, 5 * D), BF16), jax.ShapeDtypeStruct((T, D), F32),
                   jax.ShapeDtypeStruct(pool_w.shape, F32), vec_shape, vec_shape,
                   vec_shape, jax.ShapeDtypeStruct(sgu_wm.shape, F32), jax.ShapeDtypeStruct(sgu_bias_t.shape, F32)],
        grid=(nt + 1,),
        in_specs=[pl.BlockSpec((tm, SAVED_WIDTH * D), lambda i: (tile_of(i), 0)),
                  pl.BlockSpec((tm, 2 * D), lambda i: (tile_of(i), 0)),
                  pl.BlockSpec((tm, D), lambda i: (jnp.minimum(nt - i, nt - 1), 0)), ANY,
                  whole(pool_w.shape), vec, vec, vec, whole(sgu_wm.shape), whole(sgu_bias_t.shape)],
        out_specs=[pl.BlockSpec((tm, 5 * D), lambda i: (tile_of(i), 0)),
                   pl.BlockSpec((tm, D), lambda i: (jnp.minimum(nt - i, nt - 1), 0)),
                   whole(pool_w.shape), vec, vec, vec, whole(sgu_wm.shape), whole(sgu_bias_t.shape)],
        scratch_shapes=[pltpu.VMEM(w_in.shape, BF16), pltpu.VMEM((2, tm, 5 * D), BF16), pltpu.VMEM((HALO, D), F32),
                        pltpu.SemaphoreType.DMA((1,))],
        compiler_params=pltpu.CompilerParams(dimension_semantics=("arbitrary",), vmem_limit_bytes=VMEM_LIMIT),
    )(kept, d_y, dx_part, w_in, pool_w, pool_scale, sgu_g, sgu_b, sgu_wm, sgu_bias_t)


def _weight_backward(d_h, xt, q, scatter_srcs, tm):
    D, T = xt.shape
    cq = d_h.shape[1] // 4
    hr = D // 2
    nt = T // tm
    ns = len(scatter_srcs)

    def body(q_ref, dh_ref, xt_ref, *refs):
        srcs, out_any, dsts = refs[:ns], refs[ns], refs[ns + 1:2 * ns + 1]
        (acc, land_a, send_b, land_b, mine_f, theirs_f,
         a_send, a_recv, b_send, b_recv, j_sems, o_sems, s_send, s_recv) = refs[2 * ns + 1:]
        s, t = pl.program_id(0), pl.program_id(1)
        x_, y_, c = _place()
        sibling = (x_, y_, 1 - c)
        own_rows = pl.ds(pl.multiple_of(c * hr, hr), hr)
        other_rows = pl.ds(pl.multiple_of((1 - c) * hr, hr), hr)

        @pl.when((s == 0) & (t == 0))
        def _():
            for cp in _scatter_copies(srcs, dsts, s_send, s_recv):
                cp.start()

        @pl.when(t == 0)
        def _():
            acc[s % 2] = jnp.zeros((D, cq), F32)

        for c0 in range(0, cq, MXU_COLS):
            acc[s % 2, :, c0:c0 + MXU_COLS] += _mm(xt_ref[...], dh_ref[:, c0:c0 + MXU_COLS])

        def swap(phase):
            return pltpu.make_async_remote_copy(
                src_ref=acc.at[phase % 2, other_rows], dst_ref=land_a.at[phase % 2], send_sem=a_send.at[phase],
                recv_sem=a_recv.at[phase], device_id=sibling, device_id_type=MESH)

        def pair_sum(phase):
            swap(phase).wait()
            return acc[phase % 2, own_rows, :] + land_a[phase % 2]

        def to_owner(slot):
            flip_x, flip_y = (slot + 1) >> 1, (slot + 1) & 1
            owner = (1 - x_ if flip_x else x_, 1 - y_ if flip_y else y_, c)
            return pltpu.make_async_remote_copy(
                src_ref=send_b.at[slot], dst_ref=land_b.at[slot], send_sem=b_send.at[slot],
                recv_sem=b_recv.at[slot], device_id=owner, device_id_type=MESH)

        for slot in range(3):
            @pl.when((s == slot) & (t == nt - 1))
            def _(slot=slot):
                swap(slot).start()

            @pl.when((s == slot + 1) & (t == 0))
            def _(slot=slot):
                send_b[slot] = pair_sum(slot).astype(BF16)
                to_owner(slot).start()

        @pl.when((s == 3) & (t == nt - 1))
        def _():
            swap(3).start()
            own = pair_sum(3)
            for slot in range(3):
                to_owner(slot).wait_recv()
            mine_f[...] = (own + land_b[0].astype(F32)) + (land_b[1].astype(F32) + land_b[2].astype(F32))
            join = pltpu.make_async_remote_copy(
                src_ref=mine_f, dst_ref=theirs_f, send_sem=j_sems.at[0], recv_sem=j_sems.at[1],
                device_id=sibling, device_id_type=MESH)
            join.start()
            out_mine = pltpu.make_async_copy(mine_f, out_any.at[own_rows], o_sems.at[0])
            out_mine.start()
            join.wait()
            out_theirs = pltpu.make_async_copy(theirs_f, out_any.at[other_rows], o_sems.at[1])
            out_theirs.start()
            for slot in range(3):
                to_owner(slot).wait_send()
            for cp in _scatter_copies(srcs, dsts, s_send, s_recv):
                cp.wait()
            out_mine.wait()
            out_theirs.wait()

    def quarter(s, t, q_ref):
        return (t, jnp.where(s == 3, q_ref[0], q_ref[0] ^ (s + 1)))

    dma = pltpu.SemaphoreType.DMA
    return pl.pallas_call(
        body, name="weight_backward",
        out_shape=[jax.ShapeDtypeStruct((D, cq), F32)] + _scatter_shapes(scatter_srcs),
        grid_spec=pltpu.PrefetchScalarGridSpec(
            num_scalar_prefetch=1, grid=(4, nt),
            in_specs=[pl.BlockSpec((tm, cq), quarter), pl.BlockSpec((D, tm), lambda s, t, q_ref: (0, t))] + [ANY] * ns,
            out_specs=[ANY] * (ns + 1),
            scratch_shapes=[pltpu.VMEM((2, D, cq), F32), pltpu.VMEM((2, hr, cq), F32), pltpu.VMEM((3, hr, cq), BF16),
                            pltpu.VMEM((3, hr, cq), BF16), pltpu.VMEM((hr, cq), F32), pltpu.VMEM((hr, cq), F32),
                            dma((4,)), dma((4,)), dma((3,)), dma((3,)), dma((2,)), dma((2,)), dma((3 * ns,)), dma((3 * ns,))]),
        compiler_params=pltpu.CompilerParams(dimension_semantics=("arbitrary", "arbitrary"),
                                             vmem_limit_bytes=VMEM_LIMIT),
    )(jnp.reshape(q, (1,)).astype(jnp.int32), d_h, xt, *scatter_srcs)


def _finish_reduce(parts, slots, wholes, wholes_slots, q):
    n, nw = len(parts), len(wholes)
    shapes = [p.shape[1:] for p in parts]

    def four(own, s):
        return (own[...].astype(F32) + s[0].astype(F32)) + (s[1].astype(F32) + s[2].astype(F32))

    def body(q_ref, *refs):
        p_any, s_any = refs[:n], refs[n:2 * n]
        w_any, ws_any = refs[2 * n:2 * n + nw], refs[2 * n + nw:2 * n + 2 * nw]
        outs = refs[2 * n + 2 * nw:4 * n + 3 * nw]
        mine_any, theirs_any, wo_any = outs[:n], outs[n:2 * n], outs[2 * n:2 * n + nw]
        rest = refs[4 * n + 3 * nw:]
        own_v, slots_v, mine_v, theirs_v = (rest[k * n:(k + 1) * n] for k in range(4))
        w_v, ws_v, wo_v = (rest[4 * n + k * nw:4 * n + (k + 1) * nw] for k in range(3))
        load_sems, send_sems, recv_sems, store_sems = rest[4 * n + 3 * nw:]
        x, y, c = _place()
        loads = [pltpu.make_async_copy(p_any[k].at[q_ref[0]], own_v[k], load_sems.at[k]) for k in range(n)]
        loads += [pltpu.make_async_copy(s_any[k], slots_v[k], load_sems.at[n + k]) for k in range(n)]
        loads += [pltpu.make_async_copy(w_any[k], w_v[k], load_sems.at[2 * n + k]) for k in range(nw)]
        loads += [pltpu.make_async_copy(ws_any[k], ws_v[k], load_sems.at[2 * n + nw + k]) for k in range(nw)]
        for cp in loads:
            cp.start()
        swaps, stores = [], []
        for k in range(n):
            loads[k].wait()
            loads[n + k].wait()
            mine_v[k][...] = four(own_v[k], slots_v[k])
            swaps.append(pltpu.make_async_remote_copy(
                src_ref=mine_v[k], dst_ref=theirs_v[k], send_sem=send_sems.at[k], recv_sem=recv_sems.at[k],
                device_id=(x, y, 1 - c), device_id_type=MESH))
            swaps[-1].start()
            stores.append(pltpu.make_async_copy(mine_v[k], mine_any[k], store_sems.at[k]))
            stores[-1].start()
        for k in range(nw):
            loads[2 * n + k].wait()
            loads[2 * n + nw + k].wait()
            wo_v[k][...] = four(w_v[k], ws_v[k])
            stores.append(pltpu.make_async_copy(wo_v[k], wo_any[k], store_sems.at[n + k]))
            stores[-1].start()
        for k in range(n):
            swaps[k].wait()
            stores.append(pltpu.make_async_copy(theirs_v[k], theirs_any[k], store_sems.at[n + nw + k]))
            stores[-1].start()
        for cp in stores:
            cp.wait()

    dma = pltpu.SemaphoreType.DMA
    f32_shapes = [jax.ShapeDtypeStruct(s, F32) for s in shapes]
    return pl.pallas_call(
        body, name="finish_reduce",
        out_shape=f32_shapes + f32_shapes + [jax.ShapeDtypeStruct(w.shape, F32) for w in wholes],
        grid_spec=pltpu.PrefetchScalarGridSpec(
            num_scalar_prefetch=1, grid=(1,),
            in_specs=[ANY] * (2 * n + 2 * nw), out_specs=[ANY] * (2 * n + nw),
            scratch_shapes=[pltpu.VMEM(s, BF16) for s in shapes] + [pltpu.VMEM((3,) + s, BF16) for s in shapes]
            + [pltpu.VMEM(s, F32) for s in shapes] * 2
            + [pltpu.VMEM(w.shape, F32) for w in wholes] + [pltpu.VMEM((3,) + w.shape, F32) for w in wholes]
            + [pltpu.VMEM(w.shape, F32) for w in wholes]
            + [dma((2 * n + 2 * nw,)), dma((n,)), dma((n,)), dma((2 * n + nw,))]),
        compiler_params=pltpu.CompilerParams(vmem_limit_bytes=VMEM_LIMIT),
    )(_scalar(q), *parts, *slots, *wholes, *wholes_slots)


def _pair_reduce(grads, wholes):
    n, nw = len(grads), len(wholes)
    half_shapes = [(4,) + g.shape[2:] for g in grads]

    def body(*refs):
        g_any, w_any, o_any, wo_any = refs[:n], refs[n:n + nw], refs[n + nw:2 * n + nw], refs[2 * n + nw:2 * n + 2 * nw]
        rest = refs[2 * n + 2 * nw:]
        mine, theirs, send, land, out = (rest[k * n:(k + 1) * n] for k in range(5))
        w_v, w_land, w_out = (rest[5 * n + k * nw:5 * n + (k + 1) * nw] for k in range(3))
        load_sems, send_sems, recv_sems, store_sems = rest[5 * n + 3 * nw:]
        x, y, c = _place()
        sibling = (x, y, 1 - c)
        loads = [pltpu.make_async_copy(g_any[k].at[:, 1 - c], theirs[k], load_sems.at[k]) for k in range(n)]
        loads += [pltpu.make_async_copy(g_any[k].at[:, c], mine[k], load_sems.at[n + k]) for k in range(n)]
        loads += [pltpu.make_async_copy(w_any[k], w_v[k], load_sems.at[2 * n + k]) for k in range(nw)]
        for cp in loads:
            cp.start()
        sends = []
        for k in range(n):
            loads[k].wait()
            send[k][...] = theirs[k][...].astype(BF16)
            sends.append(pltpu.make_async_remote_copy(
                src_ref=send[k], dst_ref=land[k], send_sem=send_sems.at[k], recv_sem=recv_sems.at[k],
                device_id=sibling, device_id_type=MESH))
            sends[-1].start()
        for k in range(nw):
            loads[2 * n + k].wait()
            sends.append(pltpu.make_async_remote_copy(
                src_ref=w_v[k], dst_ref=w_land[k], send_sem=send_sems.at[n + k], recv_sem=recv_sems.at[n + k],
                device_id=sibling, device_id_type=MESH))
            sends[-1].start()
        stores = []
        for k in range(n):
            loads[n + k].wait()
            sends[k].wait_recv()
            out[k][...] = (mine[k][...] + land[k][...].astype(F32)).astype(BF16)
            stores.append(pltpu.make_async_copy(out[k], o_any[k], store_sems.at[k]))
            stores[-1].start()
        for k in range(nw):
            sends[n + k].wait_recv()
            w_out[k][...] = w_v[k][...] + w_land[k][...]
            stores.append(pltpu.make_async_copy(w_out[k], wo_any[k], store_sems.at[n + k]))
            stores[-1].start()
        for cp in sends:
            cp.wait_send()
        for cp in stores:
            cp.wait()

    dma = pltpu.SemaphoreType.DMA
    return pl.pallas_call(
        body, name="pair_reduce",
        out_shape=[jax.ShapeDtypeStruct(s, BF16) for s in half_shapes] + [jax.ShapeDtypeStruct(w.shape, F32) for w in wholes],
        in_specs=[ANY] * (n + nw), out_specs=[ANY] * (n + nw),
        scratch_shapes=[pltpu.VMEM(s, F32) for s in half_shapes] * 2 + [pltpu.VMEM(s, BF16) for s in half_shapes] * 3
        + [pltpu.VMEM(w.shape, F32) for w in wholes] * 3
        + [dma((2 * n + nw,)), dma((n + nw,)), dma((n + nw,)), dma((n + nw,))],
        compiler_params=pltpu.CompilerParams(vmem_limit_bytes=VMEM_LIMIT),
    )(*grads, *wholes)


def _token_tile(T, want):
    return math.gcd(T, want)


def kernel(x, p, w_in, pool_w, pool_scale, sgu_ln_g, sgu_ln_b, sgu_w, sgu_b, w_out, ln_g, ln_b, ple_w, ple_gate_w, ple_gate_b, loss_target, m_w_in, m_pool_w, m_pool_scale, m_sgu_ln_g, m_sgu_ln_b, m_sgu_w, m_sgu_b, m_w_out, m_ln_g, m_ln_b, m_ple_w, m_ple_gate_w, m_ple_gate_b, v_w_in, v_pool_w, v_pool_scale, v_sgu_ln_g, v_sgu_ln_b, v_sgu_w, v_sgu_b, v_w_out, v_ln_g, v_ln_b, v_ple_w, v_ple_gate_w, v_ple_gate_b):
    c = lax.axis_index("c")
    T, D = x.shape[1], x.shape[2]
    tm, tm_vpu, tm_acc = _token_tile(T, 512), _token_tile(T, 256), _token_tile(T, 2048)
    x2, p2, tgt = x[0], p[0, 0], loss_target[0]
    G, PGQ, PG = pool_w.shape[1], pool_w.shape[2], pool_w.shape[3]

    w_in_f, pool_f = _gather_weights([w_in[0], pool_w[0].reshape(G * PGQ, PG)])
    pool_f = pool_f.reshape(4, G, PGQ, PG).transpose(1, 0, 2, 3).reshape(G, 4 * PGQ, PG)
    tril = jnp.tril(jnp.ones((CHUNK, CHUNK), dtype=bool))
    sgu_wm = jnp.where(tril[None], sgu_w[0], 0.0).astype(BF16)
    sgu_bias_t = sgu_b[0].T

    kept, y, xt, w_out_f, w_gate_f, w_ple_f = _front_forward(
        x2, w_in_f, pool_f, pool_scale, sgu_ln_g, sgu_ln_b, sgu_wm, sgu_bias_t, [w_out[0], ple_gate_w[0], ple_w[0]],
        tm_vpu)
    w_out_f = w_out_f.reshape(-1, D)
    w_gate_f = w_gate_f.reshape(-1, D)
    (dx_part, d_y, d_w_out, d_w_gate, d_w_ple, d_ln_g, d_ln_b, d_gate_b, ssq) = _tail(
        y, x2, p2, tgt, w_out_f, w_gate_f, w_ple_f, ln_g, ln_b, ple_gate_b, tm)
    d_h, d_x, d_pool_w, d_pool_scale, d_sgu_g, d_sgu_b, d_sgu_w, d_sgu_bias_t = _front_backward(
        kept, d_y, dx_part, w_in_f, pool_f, pool_scale, sgu_ln_g, sgu_ln_b, sgu_wm, sgu_bias_t, tm_vpu)
    grads = [d_w_out.reshape(4, -1, D), d_w_gate.reshape(4, -1, D), d_w_ple,
             d_pool_w.reshape(G, 4, PGQ, PG).transpose(1, 0, 2, 3).reshape(4, G * PGQ, PG)]
    grads = [g.reshape(4, 2, g.shape[1] // 2, g.shape[2]) for g in grads]
    vectors = ["pool_scale", "sgu_ln_g", "sgu_ln_b", "ln_g", "ln_b", "ple_gate_b"]
    rows = [d_pool_scale, d_sgu_g, d_sgu_b, d_ln_g, d_ln_b, d_gate_b,
            jnp.pad(jnp.reshape((0.5 / D) * jnp.sum(ssq), (1, 1)), ((0, 1), (0, D - 1))),
            jnp.pad(d_sgu_bias_t.T, ((0, 4), (0, D - CHUNK)))]
    small = jnp.concatenate(rows, axis=0)
    d_sgu_w = d_sgu_w.reshape(-1, CHUNK)
    q = 2 * lax.axis_index("x") + lax.axis_index("y")
    *parts, small_chip, sw_chip = _pair_reduce(grads, [small, d_sgu_w])
    d_w_in, *slots, small_slots, sw_slots = _weight_backward(d_h, xt, q, parts + [small_chip, sw_chip], tm_acc)
    reduced = _finish_reduce(parts, slots, [small_chip, sw_chip], [small_slots, sw_slots], q)
    n_big = len(parts)
    halves, sibling_halves, (small_total, sw_total) = reduced[:n_big], reduced[n_big:2 * n_big], reduced[2 * n_big:]
    loss = small_total[6, 0]

    big_names = ["w_out", "ple_gate_w", "ple_w", "pool_w"]
    given = dict(w_in=(w_in, m_w_in, v_w_in), w_out=(w_out, m_w_out, v_w_out),
                 ple_gate_w=(ple_gate_w, m_ple_gate_w, v_ple_gate_w), ple_w=(ple_w, m_ple_w, v_ple_w),
                 pool_w=(pool_w, m_pool_w, v_pool_w), pool_scale=(pool_scale, m_pool_scale, v_pool_scale),
                 sgu_ln_g=(sgu_ln_g, m_sgu_ln_g, v_sgu_ln_g), sgu_ln_b=(sgu_ln_b, m_sgu_ln_b, v_sgu_ln_b),
                 sgu_w=(sgu_w, m_sgu_w, v_sgu_w), sgu_b=(sgu_b, m_sgu_b, v_sgu_b), ln_g=(ln_g, m_ln_g, v_ln_g),
                 ln_b=(ln_b, m_ln_b, v_ln_b), ple_gate_b=(ple_gate_b, m_ple_gate_b, v_ple_gate_b))
    grad, delta, new_m, new_v = {}, {}, {}, {}
    grad["w_in"], delta["w_in"], new_m["w_in"], new_v["w_in"] = (
        t[None] for t in _adamw(w_in[0], d_w_in, m_w_in[0], v_w_in[0], "adamw_w_in"))
    flat = [(2 * g.shape[0], g.shape[1]) for g in halves]
    small_items = [(*given[n], (0, slice(k, k + 1), slice(None)), False) for k, n in enumerate(vectors)]
    small_items.append((*(t[0] for t in given["sgu_b"]), (0, slice(8, 8 + sgu_b.shape[1]), slice(0, CHUNK)), False))
    small_items.append((*(t.reshape(-1, CHUNK) for t in given["sgu_w"]), (1, slice(None), slice(None)), True))
    outs = _adamw_joined(
        [given[n][0].reshape(f) for n, f in zip(big_names, flat)], halves, sibling_halves,
        [given[n][1].reshape(f) for n, f in zip(big_names, flat)],
        [given[n][2].reshape(f) for n, f in zip(big_names, flat)], small_items, [small_total, sw_total], c)
    for name, four in zip(big_names + vectors + ["sgu_b", "sgu_w"], outs):
        grad[name], delta[name], new_m[name], new_v[name] = (t.reshape(given[name][0].shape) for t in four)

    order = ["w_in", "pool_w", "pool_scale", "sgu_ln_g", "sgu_ln_b", "sgu_w", "sgu_b", "w_out", "ln_g", "ln_b",
             "ple_w", "ple_gate_w", "ple_gate_b"]
    return (loss, d_x[None], *[grad[n] for n in order], *[delta[n] for n in order],
            *[new_m[n] for n in order], *[new_v[n] for n in order])
```

```python
import functools
import math

import jax
import jax.numpy as jnp
from jax import lax
from jax.experimental import pallas as pl
from jax.experimental.pallas import tpu as pltpu

F32, BF16 = jnp.float32, jnp.bfloat16
MESH = pl.DeviceIdType.MESH
ANY = pl.BlockSpec(memory_space=pl.ANY)

POOL_WINDOWS = (2, 4, 8, 16)
HALO = 16
CHUNK = 128
MXU_COLS = 256
LN_EPS = 1e-5
DEEPNORM_ALPHA = 2.0 ** 0.25
ADAM_LR, ADAM_B1, ADAM_B2, ADAM_EPS, ADAM_WD, ADAM_STEP = 1e-3, 0.9, 0.999, 1e-8, 0.01, 10
VMEM_LIMIT = 56 * 1024 * 1024
GELU_K = math.sqrt(2.0 / math.pi)
GELU_C = 0.044715
SAVED = {"pooled": 0, "gelu_u": 1, "dgelu_u": 2, "vhat": 3, "rstd_dgelu_v": 4, "silu": 5, "dsilu": 7}
SAVED_WIDTH = 9


def _mm(a, b):
    return jnp.dot(a, b, preferred_element_type=F32)


def _mm_nt(a, b):
    return lax.dot_general(a, b, (((1,), (1,)), ((), ())), preferred_element_type=F32)


def _mm_tn(a, b):
    return lax.dot_general(a, b, (((0,), (0,)), ((), ())), preferred_element_type=F32)


def _gelu_and_grad(x):
    x2 = x * x
    t = jnp.tanh(x * (GELU_K + (GELU_K * GELU_C) * x2))
    hx = 0.5 * x
    g = hx + hx * t
    dg = (0.5 + 0.5 * t) + (hx - hx * t * t) * (GELU_K + (3.0 * GELU_K * GELU_C) * x2)
    return g, dg


def _silu_and_grad(z):
    sig = jax.nn.sigmoid(z)
    zs = z * sig
    return zs, sig + zs * (1.0 - sig)


def _by_lane_halves(fn, x):
    half = x.shape[1] // 2
    parts = [fn(x[:, :half]), fn(x[:, half:])]
    return tuple(jnp.concatenate([a, b], axis=1) for a, b in zip(*parts))


def _norm_rows(x):
    mu = jnp.mean(x, axis=-1, keepdims=True)
    xc = x - mu
    var = jnp.mean(xc * xc, axis=-1, keepdims=True)
    rstd = lax.rsqrt(var + LN_EPS)
    return xc * rstd, rstd


def _norm_rows_bwd(dxhat, xhat, rstd):
    m1 = jnp.mean(dxhat, axis=-1, keepdims=True)
    m2 = jnp.mean(dxhat * xhat, axis=-1, keepdims=True)
    return rstd * (dxhat - m1 - xhat * m2)


def _inv_count(row0, rows, w):
    t = row0 + lax.broadcasted_iota(jnp.int32, (rows, 1), 0)
    return 1.0 / jnp.minimum(t + 1, w).astype(F32)


def _causal_window_sum(ext, w):
    s, sh = ext, 1
    while sh < w:
        s = s + pltpu.roll(s, sh, axis=0)
        sh *= 2
    return s[HALO:, :]


def _anticausal_window_sum(ext, w):
    n, s, sh = ext.shape[0], ext, 1
    while sh < w:
        s = s + pltpu.roll(s, n - sh, axis=0)
        sh *= 2
    return s[: n - HALO, :]


def _place():
    return lax.axis_index("x"), lax.axis_index("y"), lax.axis_index("c")


def _gather_weights(shards):
    n = len(shards)
    piece = [s.shape[0] // 4 for s in shards]

    def body(*refs):
        wide, dsts, srcs = refs[:n], refs[n:2 * n], refs[2 * n:3 * n]
        send_sems, recv_sems, local_sems = refs[3 * n:]
        for k in range(n):
            for r0 in range(0, 4 * piece[k], CHUNK):
                srcs[k][r0:r0 + CHUNK, :] = wide[k][r0:r0 + CHUNK, :].astype(BF16)
        x, y, c = _place()
        me, sibling = (x, y, c), (x, y, 1 - c)
        across_x, across_y = (1 - x, y, c), (x, 1 - y, c)
        q, qx, qy, qf = 2 * x + y, 2 * (1 - x) + y, 2 * x + (1 - y), 2 * (1 - x) + (1 - y)

        def rows(ref, cc, p, k):
            return ref.at[pl.ds((2 * cc + p) * piece[k], piece[k])]

        def copy(k, sem, qq, cc, p, to, own=False):
            landing = rows(dsts[k].at[qq], cc, p, k)
            return pltpu.make_async_remote_copy(
                src_ref=rows(srcs[k], cc, p, k) if own else landing, dst_ref=landing,
                send_sem=send_sems.at[12 * k + sem], recv_sem=recv_sems.at[12 * k + sem],
                device_id=to, device_id_type=MESH)

        started = []

        def go(cp):
            cp.start()
            started.append(cp)

        mine = [pltpu.make_async_copy(srcs[k], dsts[k].at[q], local_sems.at[k]) for k in range(n)]
        for cp in mine:
            cp.start()
        for k in range(n):
            for p in range(2):
                go(copy(k, p, q, c, p, across_x, own=True))
                go(copy(k, 2 + p, q, c, p, across_y, own=True))
        for k in range(n):
            copy(k, 0, qx, c, 0, me).wait_recv()
            go(copy(k, 4, qx, c, 0, across_y))
            go(copy(k, 6, qx, c, 0, sibling))
            copy(k, 3, qy, c, 1, me).wait_recv()
            go(copy(k, 5, qy, c, 1, across_x))
            go(copy(k, 9, qy, c, 1, sibling))
        for k in range(n):
            copy(k, 1, qx, c, 1, me).wait_recv()
            go(copy(k, 7, qx, c, 1, sibling))
            copy(k, 2, qy, c, 0, me).wait_recv()
            go(copy(k, 8, qy, c, 0, sibling))
        for k in range(n):
            copy(k, 4, qf, c, 0, me).wait_recv()
            go(copy(k, 10, qf, c, 0, sibling))
            copy(k, 5, qf, c, 1, me).wait_recv()
            go(copy(k, 11, qf, c, 1, sibling))
        for k in range(n):
            for sem, qq, p in ((6, qx, 0), (7, qx, 1), (8, qy, 0), (9, qy, 1), (10, qf, 0), (11, qf, 1)):
                copy(k, sem, qq, 1 - c, p, me).wait_recv()
        for cp in started:
            cp.wait_send()
        for cp in mine:
            cp.wait()

    return pl.pallas_call(
        body, name="gather_weights",
        out_shape=[jax.ShapeDtypeStruct((4,) + s.shape, BF16) for s in shards],
        in_specs=[pl.BlockSpec(memory_space=pltpu.VMEM)] * n, out_specs=[ANY] * n,
        scratch_shapes=[pltpu.VMEM(s.shape, BF16) for s in shards]
        + [pltpu.SemaphoreType.DMA((12 * n,)), pltpu.SemaphoreType.DMA((12 * n,)), pltpu.SemaphoreType.DMA((n,))],
        compiler_params=pltpu.CompilerParams(vmem_limit_bytes=VMEM_LIMIT),
    )(*shards)


def _direct_gather_copies(srcs, dsts, send_sems, recv_sems):
    x, y, c = _place()
    q = 2 * x + y
    sends, recvs = [], []
    for k, (src, dst) in enumerate(zip(srcs, dsts)):
        half = src.shape[0] // 2
        for j, chip in enumerate([(1 - x, y), (x, 1 - y), (1 - x, 1 - y)]):
            for core in range(2):
                sends.append(pltpu.make_async_remote_copy(
                    src_ref=src.at[pl.ds(c * half, half)], dst_ref=dst.at[q, pl.ds(c * half, half)],
                    send_sem=send_sems.at[6 * k + 2 * j + core], recv_sem=recv_sems.at[6 * k + 2 * j + c],
                    device_id=(*chip, core), device_id_type=MESH))
                landed = dst.at[2 * chip[0] + chip[1], pl.ds(core * half, half)]
                recvs.append(pltpu.make_async_remote_copy(
                    src_ref=landed, dst_ref=landed, send_sem=send_sems.at[6 * k + 2 * j + core],
                    recv_sem=recv_sems.at[6 * k + 2 * j + core], device_id=(x, y, c), device_id_type=MESH))
    return sends, recvs


def _scatter_copies(srcs, dsts, send_sems, recv_sems):
    x, y, c = _place()
    copies = []
    for j, chip in enumerate([(1 - x, y), (x, 1 - y), (1 - x, 1 - y)]):
        for k, (src, dst) in enumerate(zip(srcs, dsts)):
            copies.append(pltpu.make_async_remote_copy(
                src_ref=src.at[2 * chip[0] + chip[1]] if len(src.shape) == 3 else src, dst_ref=dst.at[j],
                send_sem=send_sems.at[3 * k + j], recv_sem=recv_sems.at[3 * k + j],
                device_id=(*chip, c), device_id_type=MESH))
    return copies


def _scatter_shapes(parts):
    return [jax.ShapeDtypeStruct((3,) + (p.shape[1:] if p.ndim == 3 else p.shape), p.dtype) for p in parts]


def _join_halves_with_sibling(halves):
    n = len(halves)

    def body(*refs):
        srcs, dsts = refs[:n], refs[n:2 * n]
        send_sems, recv_sems = refs[2 * n:]
        x, y, c = _place()
        copies = [pltpu.make_async_remote_copy(
            src_ref=srcs[k], dst_ref=dsts[k], send_sem=send_sems.at[k], recv_sem=recv_sems.at[k],
            device_id=(x, y, 1 - c), device_id_type=MESH) for k in range(n)]
        for cp in copies:
            cp.start()
        for cp in copies:
            cp.wait()

    return pl.pallas_call(
        body, name="join_halves",
        out_shape=[jax.ShapeDtypeStruct(h.shape, h.dtype) for h in halves],
        in_specs=[ANY] * n, out_specs=[ANY] * n,
        scratch_shapes=[pltpu.SemaphoreType.DMA((n,)), pltpu.SemaphoreType.DMA((n,))],
    )(*halves)


def _row_block(rows, cols, n_arrays):
    cap = max(8, (VMEM_LIMIT // 2) // (8 * n_arrays * cols))
    rb = rows
    while rb > cap and rb % 2 == 0:
        rb //= 2
    return rb


def _scalar(value):
    return jnp.reshape(value, (1,)).astype(jnp.int32)


def _whole(a):
    return pl.BlockSpec(a.shape, lambda i, s_ref: (0,) * a.ndim)


def _sum_fours(parts, slots, wholes, wholes_slots, q):
    n, nw = len(parts), len(wholes)

    def four(own, s):
        return (own[...].astype(F32) + s[0].astype(F32)) + (s[1].astype(F32) + s[2].astype(F32))

    def body(q_ref, *refs):
        p, s, w, ws = refs[:n], refs[n:2 * n], refs[2 * n:2 * n + nw], refs[2 * n + nw:2 * n + 2 * nw]
        o, wo = refs[2 * n + 2 * nw:3 * n + 2 * nw], refs[3 * n + 2 * nw:]
        for k in range(n):
            o[k][...] = four(p[k], s[k])
        for k in range(nw):
            wo[k][...] = four(w[k], ws[k])

    return pl.pallas_call(
        body, name="sum_fours",
        out_shape=[jax.ShapeDtypeStruct(a.shape[1:], F32) for a in parts] + [jax.ShapeDtypeStruct(a.shape, F32) for a in wholes],
        grid_spec=pltpu.PrefetchScalarGridSpec(
            num_scalar_prefetch=1, grid=(2,),
            in_specs=[pl.BlockSpec((None, a.shape[1] // 2, a.shape[2]), lambda i, q_ref: (q_ref[0], i, 0)) for a in parts]
            + [pl.BlockSpec((3, a.shape[1] // 2, a.shape[2]), lambda i, q_ref: (0, i, 0)) for a in slots]
            + [_whole(a) for a in wholes + wholes_slots],
            out_specs=[pl.BlockSpec((a.shape[1] // 2, a.shape[2]), lambda i, q_ref: (i, 0)) for a in parts]
            + [_whole(a) for a in wholes]),
        compiler_params=pltpu.CompilerParams(vmem_limit_bytes=VMEM_LIMIT),
    )(_scalar(q), *parts, *slots, *wholes, *wholes_slots)


def _adamw_math(w, g, m, v):
    nm = ADAM_B1 * m + (1.0 - ADAM_B1) * g
    nv = ADAM_B2 * v + (1.0 - ADAM_B2) * (g * g)
    m_hat = nm / (1.0 - ADAM_B1 ** ADAM_STEP)
    v_hat = nv / (1.0 - ADAM_B2 ** ADAM_STEP)
    return -ADAM_LR * (m_hat / (jnp.sqrt(v_hat) + ADAM_EPS) + ADAM_WD * w), nm, nv


def _adamw(w, g, m, v, name):
    rows, cols = w.shape
    rb = _row_block(rows, cols, 8)

    def body(w_ref, g_ref, m_ref, v_ref, go_ref, d_ref, nm_ref, nv_ref):
        go_ref[...] = g_ref[...]
        d_ref[...], nm_ref[...], nv_ref[...] = _adamw_math(w_ref[...], g_ref[...], m_ref[...], v_ref[...])

    spec = pl.BlockSpec((rb, cols), lambda r: (r, 0))
    out = jax.ShapeDtypeStruct(w.shape, F32)
    return pl.pallas_call(body, name=name, out_shape=[out] * 4, grid=(rows // rb,),
                          in_specs=[spec] * 4, out_specs=[spec] * 4,
                          compiler_params=pltpu.CompilerParams(vmem_limit_bytes=VMEM_LIMIT))(w, g, m, v)


def _adamw_joined(ws, g_mine, g_sibling, ms, vs, small, small_grads, c):
    n, ns, ng = len(ws), len(small), len(small_grads)

    def body(c_ref, *refs):
        big, tot, sm = refs[:5 * n], refs[5 * n:5 * n + ng], refs[5 * n + ng:5 * n + ng + 3 * ns]
        outs = refs[5 * n + ng + 3 * ns:]
        mine = c_ref[0] == pl.program_id(0)
        for k in range(n):
            w, gm, gs, m, v = big[5 * k:5 * k + 5]
            g = jnp.where(mine, gm[...], gs[...])
            outs[4 * k][...] = g
            outs[4 * k + 1][...], outs[4 * k + 2][...], outs[4 * k + 3][...] = _adamw_math(w[...], g, m[...], v[...])
        for k in range(ns):
            w, m, v = sm[3 * k:3 * k + 3]
            which, rows, cols = small[k][3]
            g = tot[which][rows, cols]
            if small[k][4]:
                i_pos = lax.broadcasted_iota(jnp.int32, g.shape, 0) % CHUNK
                g = jnp.where(lax.broadcasted_iota(jnp.int32, g.shape, 1) <= i_pos, g, 0.0)
            o = outs[4 * (n + k):4 * (n + k) + 4]
            o[0][...] = g
            o[1][...], o[2][...], o[3][...] = _adamw_math(w[...], g, m[...], v[...])

    def half(a):
        return pl.BlockSpec((a.shape[0] // 2, a.shape[1]), lambda hf, c_ref: (hf, 0))

    in_specs, operands = [], []
    for k in range(n):
        in_specs += [half(ws[k]), _whole(g_mine[k]), _whole(g_sibling[k]), half(ms[k]), half(vs[k])]
        operands += [ws[k], g_mine[k], g_sibling[k], ms[k], vs[k]]
    operands += list(small_grads) + [a for item in small for a in item[:3]]
    in_specs += [_whole(a) for a in operands[5 * n:]]
    outs = pl.pallas_call(
        body, name="adamw_joined",
        out_shape=[jax.ShapeDtypeStruct(w.shape, F32) for w in ws for _ in range(4)]
        + [jax.ShapeDtypeStruct(item[0].shape, F32) for item in small for _ in range(4)],
        grid_spec=pltpu.PrefetchScalarGridSpec(
            num_scalar_prefetch=1, grid=(2,),
            in_specs=in_specs,
            out_specs=[half(w) for w in ws for _ in range(4)] + [_whole(item[0]) for item in small for _ in range(4)]),
        compiler_params=pltpu.CompilerParams(vmem_limit_bytes=VMEM_LIMIT),
    )(_scalar(c), *operands)
    return [outs[4 * k:4 * k + 4] for k in range(n + ns)]


def _front_forward(x, w_in, pool_w, pool_scale, sgu_g, sgu_b, sgu_wm, sgu_bias_t, later_shards, tm):
    T, D = x.shape
    nq, _, cq = w_in.shape
    G, PG = pool_w.shape[0], pool_w.shape[1]
    nt = T // tm
    bpd = D // PG
    nl = len(later_shards)

    def body(x_ref, win_any, pw_any, ps_ref, lg_ref, lb_ref, sw_ref, sb_ref, *refs):
        shards_any, (keep_ref, y_ref, xt_ref), gathered = refs[:nl], refs[nl:nl + 3], refs[nl + 3:2 * nl + 3]
        wide, narrow = refs[2 * nl + 3:3 * nl + 3], refs[3 * nl + 3:4 * nl + 3]
        win_v, pw_v, carry, sems, load_sems, own_sems, send_sems, recv_sems = refs[4 * nl + 3:]
        i = pl.program_id(0)
        own_quarter = 2 * lax.axis_index("x") + lax.axis_index("y")

        def own_copies():
            return [pltpu.make_async_copy(narrow[k], gathered[k].at[own_quarter], own_sems.at[k]) for k in range(nl)]

        @pl.when(i == 0)
        def _():
            c1 = pltpu.make_async_copy(win_any, win_v, sems.at[0])
            c2 = pltpu.make_async_copy(pw_any, pw_v, sems.at[1])
            loads = [pltpu.make_async_copy(shards_any[k], wide[k], load_sems.at[k]) for k in range(nl)]
            for cp in [c1, c2] + loads:
                cp.start()
            carry[...] = jnp.zeros_like(carry)
            for k in range(nl):
                loads[k].wait()
                for r0 in range(0, wide[k].shape[0], CHUNK):
                    narrow[k][r0:r0 + CHUNK, :] = wide[k][r0:r0 + CHUNK, :].astype(BF16)
            for cp in own_copies() + _direct_gather_copies(narrow, gathered, send_sems, recv_sems)[0]:
                cp.start()
            c1.wait()
            c2.wait()

        xb = x_ref[...].astype(BF16)
        xt_ref[...] = x_ref[...].T.astype(BF16)

        def h_block(j):
            qq, off = divmod(j * PG, cq)
            return _mm(xb, win_v[qq, :, off:off + PG])

        def keep(part, col, value):
            keep_ref[:, SAVED[part] * D + col:SAVED[part] * D + col + PG] = value.astype(BF16)

        def ahead(stage):
            if stage < G:
                return h_block(stage), h_block(3 * bpd + stage)
            if stage < G + bpd:
                hd = stage - G
                return h_block(bpd + hd), h_block(2 * bpd + hd), h_block(4 * bpd + hd)
            return None

        blocks = ahead(0)

        for g, w in enumerate(POOL_WINDOWS):
            sl = slice(g * PG, (g + 1) * PG)
            a, z = blocks
            blocks = ahead(g + 1)
            ext = jnp.concatenate([carry[:, sl], a], axis=0)
            carry[:, sl] = a[tm - HALO:, :]
            pooled = (_causal_window_sum(ext, w) * _inv_count(i * tm, tm, w) - a).astype(BF16)
            mixed = _mm(pooled, pw_v[g])
            zs, dzs = _by_lane_halves(_silu_and_grad, z)
            keep("pooled", g * PG, pooled)
            keep("silu", g * PG, zs)
            keep("dsilu", g * PG, dzs)
            y_ref[:, sl] = (mixed * ps_ref[:, sl] * zs).astype(BF16)

        for hd in range(bpd):
            sl = slice(hd * PG, (hd + 1) * PG)
            u, v, z = blocks
            blocks = ahead(G + hd + 1)
            ug, dug = _by_lane_halves(_gelu_and_grad, u)
            vg, dvg = _by_lane_halves(_gelu_and_grad, v)
            vhat, rstd = _norm_rows(vg)
            zs, dzs = _by_lane_halves(_silu_and_grad, z)
            keep("gelu_u", hd * PG, ug)
            keep("dgelu_u", hd * PG, dug)
            keep("vhat", hd * PG, vhat)
            keep("rstd_dgelu_v", hd * PG, rstd * dvg)
            keep("silu", D + hd * PG, zs)
            keep("dsilu", D + hd * PG, dzs)
            vn = (vhat * lg_ref[:, sl] + lb_ref[:, sl]).astype(BF16)
            gated = ug * zs
            for n in range(tm // CHUNK):
                rs = slice(n * CHUNK, (n + 1) * CHUNK)
                sv = _mm(sw_ref[hd], vn[rs, :]) + sb_ref[:, hd:hd + 1]
                y_ref[rs, D + hd * PG:D + (hd + 1) * PG] = (gated[rs, :] * sv).astype(BF16)

        @pl.when(i == nt - 1)
        def _():
            sends, recvs = _direct_gather_copies(narrow, gathered, send_sems, recv_sems)
            for cp in sends:
                cp.wait_send()
            for cp in recvs:
                cp.wait_recv()
            for cp in own_copies():
                cp.wait()

    vec = pl.BlockSpec((1, D), lambda i: (0, 0))
    return pl.pallas_call(
        body, name="front_forward",
        out_shape=[jax.ShapeDtypeStruct((T, SAVED_WIDTH * D), BF16), jax.ShapeDtypeStruct((T, 2 * D), BF16),
                   jax.ShapeDtypeStruct((D, T), BF16)]
        + [jax.ShapeDtypeStruct((4,) + s.shape, BF16) for s in later_shards],
        grid=(nt,),
        in_specs=[pl.BlockSpec((tm, D), lambda i: (i, 0)), ANY, ANY, vec, vec, vec,
                  pl.BlockSpec(sgu_wm.shape, lambda i: (0, 0, 0)), pl.BlockSpec(sgu_bias_t.shape, lambda i: (0, 0))]
        + [ANY] * nl,
        out_specs=[pl.BlockSpec((tm, SAVED_WIDTH * D), lambda i: (i, 0)), pl.BlockSpec((tm, 2 * D), lambda i: (i, 0)),
                   pl.BlockSpec((D, tm), lambda i: (0, i))] + [ANY] * nl,
        scratch_shapes=[pltpu.VMEM(s.shape, F32) for s in later_shards]
        + [pltpu.VMEM(s.shape, BF16) for s in later_shards]
        + [pltpu.VMEM(w_in.shape, BF16), pltpu.VMEM(pool_w.shape, BF16), pltpu.VMEM((HALO, D), F32),
           pltpu.SemaphoreType.DMA((2,)), pltpu.SemaphoreType.DMA((nl,)), pltpu.SemaphoreType.DMA((nl,)),
           pltpu.SemaphoreType.DMA((6 * nl,)), pltpu.SemaphoreType.DMA((6 * nl,))],
        compiler_params=pltpu.CompilerParams(dimension_semantics=("arbitrary",), vmem_limit_bytes=VMEM_LIMIT),
    )(x, w_in, pool_w, pool_scale, sgu_g, sgu_b, sgu_wm, sgu_bias_t, *later_shards)


def _tail(y, x, p, target, w_out, w_gate, w_ple, ln_g, ln_b, gate_b, tm):
    T, D = x.shape
    K = p.shape[1]
    nq, _, cq = w_ple.shape
    nt = T // tm

    def body(y_ref, x_ref, p_ref, t_ref, wout_any, wg_any, wp_any, lng_ref, lnb_ref, bg_ref,
             dxp_ref, dy_ref, dwout_any, dwg_any, dwp_any, dlng_ref, dlnb_ref, dbg_ref, ssq_ref,
             wout_v, wg_v, wp_v, dwout_acc, dwg_acc, dwp_acc, sems):
        i = pl.program_id(0)

        @pl.when(i == 0)
        def _():
            loads = [pltpu.make_async_copy(s, d, sems.at[k])
                     for k, (s, d) in enumerate(((wout_any, wout_v), (wg_any, wg_v), (wp_any, wp_v)))]
            for cp in loads:
                cp.start()
            for ref in (dwout_acc, dwg_acc, dwp_acc, dlng_ref, dlnb_ref, dbg_ref, ssq_ref):
                ref[...] = jnp.zeros_like(ref)
            for cp in loads:
                cp.wait()

        halves = [slice(k * tm // 2, (k + 1) * tm // 2) for k in range(2)]

        def total(parts):
            return sum(jnp.sum(part, axis=0, keepdims=True) for part in parts)

        yb = [y_ref[r, :] for r in halves]
        pb = [p_ref[r, :].astype(BF16) for r in halves]
        mix = [_mm(v, wout_v[...]) for v in yb]
        normed = [_norm_rows(DEEPNORM_ALPHA * x_ref[r, :] + m) for r, m in zip(halves, mix)]
        xhat, rstd = [n[0] for n in normed], [n[1] for n in normed]
        x1 = [v * lng_ref[...] + lnb_ref[...] for v in xhat]
        x1b = [v.astype(BF16) for v in x1]
        gate = [jax.nn.sigmoid(_mm(v, wg_v[...]) + bg_ref[...]) for v in x1b]
        e = [jnp.concatenate([_mm(v, wp_v[qq]) for qq in range(nq)], axis=1) for v in pb]
        diff = [a + g * ee - t_ref[r, :] for a, g, ee, r in zip(x1, gate, e, halves)]
        ssq_ref[...] += total([d * d for d in diff])

        dout = [d * (1.0 / D) for d in diff]
        d_e = [(do * g).astype(BF16) for do, g in zip(dout, gate)]
        dgl = [do * ee * g * (1.0 - g) for do, ee, g in zip(dout, e, gate)]
        dglb = [v.astype(BF16) for v in dgl]
        dbg_ref[...] += total(dgl)
        pb_t, d_e_t, x1b_t, dglb_t = (jnp.concatenate(v, axis=0) for v in (pb, d_e, x1b, dglb))
        for qq in range(nq):
            dwp_acc[qq] += _mm_tn(pb_t, d_e_t[:, qq * cq:(qq + 1) * cq])
        for c0 in range(0, D, MXU_COLS):
            dwg_acc[:, c0:c0 + MXU_COLS] += _mm_tn(x1b_t, dglb_t[:, c0:c0 + MXU_COLS])
        d_x1 = [do + _mm_nt(dg, wg_v[...]) for do, dg in zip(dout, dglb)]
        dlng_ref[...] += total([d * xh for d, xh in zip(d_x1, xhat)])
        dlnb_ref[...] += total(d_x1)
        d_r = [_norm_rows_bwd(d * lng_ref[...], xh, rs) for d, xh, rs in zip(d_x1, xhat, rstd)]
        drb = [v.astype(BF16) for v in d_r]
        for r, v in zip(halves, d_r):
            dxp_ref[r, :] = DEEPNORM_ALPHA * v
        for c0 in range(0, 2 * D, 2 * MXU_COLS):
            for r, v in zip(halves, drb):
                dy_ref[r, c0:c0 + 2 * MXU_COLS] = _mm_nt(v, wout_v[c0:c0 + 2 * MXU_COLS, :]).astype(BF16)

        drb_t = jnp.concatenate(drb, axis=0)
        for c0 in range(0, D, MXU_COLS):
            dwout_acc[:, c0:c0 + MXU_COLS] += _mm_tn(y_ref[...], drb_t[:, c0:c0 + MXU_COLS])

        @pl.when(i == nt - 1)
        def _():
            stores = [pltpu.make_async_copy(s, d, sems.at[k])
                      for k, (s, d) in enumerate(((dwout_acc, dwout_any), (dwg_acc, dwg_any), (dwp_acc, dwp_any)))]
            for cp in stores:
                cp.start()
            for cp in stores:
                cp.wait()

    vec = pl.BlockSpec((1, D), lambda i: (0, 0))
    vec_shape = jax.ShapeDtypeStruct((1, D), F32)

    def tile(cols):
        return pl.BlockSpec((tm, cols), lambda i: (i, 0))

    return pl.pallas_call(
        body, name="tail",
        out_shape=[jax.ShapeDtypeStruct((T, D), F32), jax.ShapeDtypeStruct((T, 2 * D), BF16),
                   jax.ShapeDtypeStruct(w_out.shape, F32), jax.ShapeDtypeStruct(w_gate.shape, F32),
                   jax.ShapeDtypeStruct(w_ple.shape, F32), vec_shape, vec_shape, vec_shape, vec_shape],
        grid=(nt,),
        in_specs=[tile(2 * D), tile(D), tile(K), tile(D), ANY, ANY, ANY, vec, vec, vec],
        out_specs=[tile(D), tile(2 * D), ANY, ANY, ANY, vec, vec, vec, vec],
        scratch_shapes=[pltpu.VMEM(w_out.shape, BF16), pltpu.VMEM(w_gate.shape, BF16), pltpu.VMEM(w_ple.shape, BF16),
                        pltpu.VMEM(w_out.shape, F32), pltpu.VMEM(w_gate.shape, F32), pltpu.VMEM(w_ple.shape, F32),
                        pltpu.SemaphoreType.DMA((3,))],
        compiler_params=pltpu.CompilerParams(dimension_semantics=("arbitrary",), vmem_limit_bytes=VMEM_LIMIT),
    )(y, x, p, target, w_out, w_gate, w_ple, ln_g, ln_b, gate_b)


def _front_backward(kept, d_y, dx_part, w_in, pool_w, pool_scale, sgu_g, sgu_b, sgu_wm, sgu_bias_t, tm):
    T = kept.shape[0]
    D = kept.shape[1] // SAVED_WIDTH
    nq, _, cq = w_in.shape
    G, PG = pool_w.shape[0], pool_w.shape[1]
    nt = T // tm

    def tile_of(i):
        return nt - 1 - jnp.minimum(i, nt - 1)

    def body(kept_ref, dy_ref, dxp_ref, win_any, pw_ref, ps_ref, lg_ref, lb_ref, sw_ref, sb_ref,
             dh_ref, dx_ref, dpw_ref, dps_ref, dlg_ref, dlb_ref, dsw_ref, dsb_ref, win_v, dh_keep, carry, sems):
        i = pl.program_id(0)
        ti = tile_of(i)

        def saved(part, col, rows=slice(None)):
            return kept_ref[rows, SAVED[part] * D + col:SAVED[part] * D + col + PG]

        @pl.when(i == 0)
        def _():
            cp = pltpu.make_async_copy(win_any, win_v, sems.at[0])
            cp.start()
            carry[...] = jnp.zeros_like(carry)
            for ref in (dpw_ref, dps_ref, dlg_ref, dlb_ref, dsw_ref, dsb_ref):
                ref[...] = jnp.zeros_like(ref)
            cp.wait()

        def dx_columns(r0):
            dx = dxp_ref[:, r0:r0 + MXU_COLS]
            for qq in range(nq):
                dx = dx + _mm_nt(dh_keep[(i + 1) % 2, :, qq * cq:(qq + 1) * cq], win_v[qq, r0:r0 + MXU_COLS, :])
            dx_ref[:, r0:r0 + MXU_COLS] = dx

        dx_chunks = list(range(0, D, MXU_COLS))
        stages = G + D // PG

        def pool_stage(g, w):
            sl = slice(g * PG, (g + 1) * PG)
            pooled = saved("pooled", g * PG)
            mixed = _mm(pooled, pw_ref[g])
            dy = dy_ref[:, sl].astype(F32)
            d_ypool = dy * saved("silu", g * PG).astype(F32)
            dh_ref[:, 3 * D + g * PG:3 * D + (g + 1) * PG] = (
                dy * (mixed * ps_ref[:, sl]) * saved("dsilu", g * PG).astype(F32)).astype(BF16)
            dps_ref[:, sl] += jnp.sum(d_ypool * mixed, axis=0, keepdims=True)
            d_mixed = (d_ypool * ps_ref[:, sl]).astype(BF16)
            dpw_ref[g] += _mm_tn(pooled, d_mixed)
            d_pooled = _mm_nt(d_mixed, pw_ref[g])
            scaled = d_pooled * _inv_count(ti * tm, tm, w)
            after = jnp.concatenate([scaled, carry[:, sl]], axis=0)
            carry[:, sl] = scaled[:HALO, :]
            dh_ref[:, sl] = (_anticausal_window_sum(after, w) - d_pooled).astype(BF16)

        def gating_stage(hd):
            sl = slice(hd * PG, (hd + 1) * PG)
            vhat = saved("vhat", hd * PG).astype(F32)
            vn = (vhat * lg_ref[:, sl] + lb_ref[:, sl]).astype(BF16)
            chunk_rows = [slice(n * CHUNK, (n + 1) * CHUNK) for n in range(tm // CHUNK)]
            vn_wide = jnp.concatenate([vn[rs, :] for rs in chunk_rows], axis=1)
            sv_wide = _mm(sw_ref[hd], vn_wide) + sb_ref[:, hd:hd + 1]
            d_sv_parts = []
            for n, rs in enumerate(chunk_rows):
                sv = sv_wide[:, n * PG:(n + 1) * PG]
                ug = saved("gelu_u", hd * PG, rs).astype(F32)
                dy = dy_ref[rs, D + hd * PG:D + (hd + 1) * PG].astype(F32)
                d_ysgu = dy * saved("silu", D + hd * PG, rs).astype(F32)
                dh_ref[rs, 4 * D + hd * PG:4 * D + (hd + 1) * PG] = (
                    dy * (ug * sv) * saved("dsilu", D + hd * PG, rs).astype(F32)).astype(BF16)
                dh_ref[rs, D + hd * PG:D + (hd + 1) * PG] = (
                    d_ysgu * sv * saved("dgelu_u", hd * PG, rs).astype(F32)).astype(BF16)
                d_sv_parts.append(d_ysgu * ug)
            d_sv_wide = jnp.concatenate(d_sv_parts, axis=1)
            dsb_ref[:, hd:hd + 1] += jnp.sum(d_sv_wide, axis=1, keepdims=True)
            d_svb = d_sv_wide.astype(BF16)
            dsw_ref[hd] += _mm_nt(d_svb, vn_wide)
            d_vn_wide = _mm_tn(sw_ref[hd], d_svb)
            d_vn = jnp.concatenate([d_vn_wide[:, n * PG:(n + 1) * PG] for n in range(len(chunk_rows))], axis=0)
            dlg_ref[:, sl] += jnp.sum(d_vn * vhat, axis=0, keepdims=True)
            dlb_ref[:, sl] += jnp.sum(d_vn, axis=0, keepdims=True)
            d_vg = _norm_rows_bwd(d_vn * lg_ref[:, sl], vhat, saved("rstd_dgelu_v", hd * PG).astype(F32))
            dh_ref[:, 2 * D + hd * PG:2 * D + (hd + 1) * PG] = d_vg.astype(BF16)

        def work(make_dx, make_dh):
            for stage in range(stages):
                if make_dx:
                    for r0 in dx_chunks[stage * len(dx_chunks) // stages:(stage + 1) * len(dx_chunks) // stages]:
                        dx_columns(r0)
                if make_dh and stage < G:
                    pool_stage(stage, POOL_WINDOWS[stage])
                elif make_dh:
                    gating_stage(stage - G)
            if make_dh:
                dh_keep[i % 2] = dh_ref[...]

        pl.when(i == 0)(functools.partial(work, False, True))
        pl.when((i > 0) & (i < nt))(functools.partial(work, True, True))
        pl.when(i == nt)(functools.partial(work, True, False))

    vec = pl.BlockSpec((1, D), lambda i: (0, 0))
    vec_shape = jax.ShapeDtypeStruct((1, D), F32)

    def whole(shape):
        return pl.BlockSpec(shape, lambda i: (0,) * len(shape))

    return pl.pallas_call(
        body, name="front_backward",
        out_shape=[jax.ShapeDtypeStruct((T, 5 * D), BF16), jax.ShapeDtypeStruct((T, D), F32),
                   jax.ShapeDtypeStruct(pool_w.shape, F32), vec_shape, vec_shape,
                   vec_shape, jax.ShapeDtypeStruct(sgu_wm.shape, F32), jax.ShapeDtypeStruct(sgu_bias_t.shape, F32)],
        grid=(nt + 1,),
        in_specs=[pl.BlockSpec((tm, SAVED_WIDTH * D), lambda i: (tile_of(i), 0)),
                  pl.BlockSpec((tm, 2 * D), lambda i: (tile_of(i), 0)),
                  pl.BlockSpec((tm, D), lambda i: (jnp.minimum(nt - i, nt - 1), 0)), ANY,
                  whole(pool_w.shape), vec, vec, vec, whole(sgu_wm.shape), whole(sgu_bias_t.shape)],
        out_specs=[pl.BlockSpec((tm, 5 * D), lambda i: (tile_of(i), 0)),
                   pl.BlockSpec((tm, D), lambda i: (jnp.minimum(nt - i, nt - 1), 0)),
                   whole(pool_w.shape), vec, vec, vec, whole(sgu_wm.shape), whole(sgu_bias_t.shape)],
        scratch_shapes=[pltpu.VMEM(w_in.shape, BF16), pltpu.VMEM((2, tm, 5 * D), BF16), pltpu.VMEM((HALO, D), F32),
                        pltpu.SemaphoreType.DMA((1,))],
        compiler_params=pltpu.CompilerParams(dimension_semantics=("arbitrary",), vmem_limit_bytes=VMEM_LIMIT),
    )(kept, d_y, dx_part, w_in, pool_w, pool_scale, sgu_g, sgu_b, sgu_wm, sgu_bias_t)


def _weight_backward(d_h, xt, q, scatter_srcs, tm):
    D, T = xt.shape
    cq = d_h.shape[1] // 4
    hr = D // 2
    nt = T // tm
    ns = len(scatter_srcs)

    def body(q_ref, dh_ref, xt_ref, *refs):
        srcs, out_any, dsts = refs[:ns], refs[ns], refs[ns + 1:2 * ns + 1]
        (acc, land_a, send_b, land_b, mine_f, theirs_f,
         a_send, a_recv, b_send, b_recv, j_sems, o_sems, s_send, s_recv) = refs[2 * ns + 1:]
        s, t = pl.program_id(0), pl.program_id(1)
        x_, y_, c = _place()
        sibling = (x_, y_, 1 - c)
        own_rows = pl.ds(pl.multiple_of(c * hr, hr), hr)
        other_rows = pl.ds(pl.multiple_of((1 - c) * hr, hr), hr)

        @pl.when((s == 0) & (t == 0))
        def _():
            for cp in _scatter_copies(srcs, dsts, s_send, s_recv):
                cp.start()

        @pl.when(t == 0)
        def _():
            acc[s % 2] = jnp.zeros((D, cq), F32)

        for c0 in range(0, cq, MXU_COLS):
            acc[s % 2, :, c0:c0 + MXU_COLS] += _mm(xt_ref[...], dh_ref[:, c0:c0 + MXU_COLS])

        def swap(phase):
            return pltpu.make_async_remote_copy(
                src_ref=acc.at[phase % 2, other_rows], dst_ref=land_a.at[phase % 2], send_sem=a_send.at[phase],
                recv_sem=a_recv.at[phase], device_id=sibling, device_id_type=MESH)

        def pair_sum(phase):
            swap(phase).wait()
            return acc[phase % 2, own_rows, :] + land_a[phase % 2]

        def to_owner(slot):
            flip_x, flip_y = (slot + 1) >> 1, (slot + 1) & 1
            owner = (1 - x_ if flip_x else x_, 1 - y_ if flip_y else y_, c)
            return pltpu.make_async_remote_copy(
                src_ref=send_b.at[slot], dst_ref=land_b.at[slot], send_sem=b_send.at[slot],
                recv_sem=b_recv.at[slot], device_id=owner, device_id_type=MESH)

        for slot in range(3):
            @pl.when((s == slot) & (t == nt - 1))
            def _(slot=slot):
                swap(slot).start()

            @pl.when((s == slot + 1) & (t == 0))
            def _(slot=slot):
                send_b[slot] = pair_sum(slot).astype(BF16)
                to_owner(slot).start()

        @pl.when((s == 3) & (t == nt - 1))
        def _():
            swap(3).start()
            own = pair_sum(3)
            for slot in range(3):
                to_owner(slot).wait_recv()
            mine_f[...] = (own + land_b[0].astype(F32)) + (land_b[1].astype(F32) + land_b[2].astype(F32))
            join = pltpu.make_async_remote_copy(
                src_ref=mine_f, dst_ref=theirs_f, send_sem=j_sems.at[0], recv_sem=j_sems.at[1],
                device_id=sibling, device_id_type=MESH)
            join.start()
            out_mine = pltpu.make_async_copy(mine_f, out_any.at[own_rows], o_sems.at[0])
            out_mine.start()
            join.wait()
            out_theirs = pltpu.make_async_copy(theirs_f, out_any.at[other_rows], o_sems.at[1])
            out_theirs.start()
            for slot in range(3):
                to_owner(slot).wait_send()
            for cp in _scatter_copies(srcs, dsts, s_send, s_recv):
                cp.wait()
            out_mine.wait()
            out_theirs.wait()

    def quarter(s, t, q_ref):
        return (t, jnp.where(s == 3, q_ref[0], q_ref[0] ^ (s + 1)))

    dma = pltpu.SemaphoreType.DMA
    return pl.pallas_call(
        body, name="weight_backward",
        out_shape=[jax.ShapeDtypeStruct((D, cq), F32)] + _scatter_shapes(scatter_srcs),
        grid_spec=pltpu.PrefetchScalarGridSpec(
            num_scalar_prefetch=1, grid=(4, nt),
            in_specs=[pl.BlockSpec((tm, cq), quarter), pl.BlockSpec((D, tm), lambda s, t, q_ref: (0, t))] + [ANY] * ns,
            out_specs=[ANY] * (ns + 1),
            scratch_shapes=[pltpu.VMEM((2, D, cq), F32), pltpu.VMEM((2, hr, cq), F32), pltpu.VMEM((3, hr, cq), BF16),
                            pltpu.VMEM((3, hr, cq), BF16), pltpu.VMEM((hr, cq), F32), pltpu.VMEM((hr, cq), F32),
                            dma((4,)), dma((4,)), dma((3,)), dma((3,)), dma((2,)), dma((2,)), dma((3 * ns,)), dma((3 * ns,))]),
        compiler_params=pltpu.CompilerParams(dimension_semantics=("arbitrary", "arbitrary"),
                                             vmem_limit_bytes=VMEM_LIMIT),
    )(jnp.reshape(q, (1,)).astype(jnp.int32), d_h, xt, *scatter_srcs)


def _pair_reduce(grads, wholes):
    n, nw = len(grads), len(wholes)
    half_shapes = [(4,) + g.shape[2:] for g in grads]

    def body(*refs):
        g_any, w_any, o_any, wo_any = refs[:n], refs[n:n + nw], refs[n + nw:2 * n + nw], refs[2 * n + nw:2 * n + 2 * nw]
        rest = refs[2 * n + 2 * nw:]
        mine, theirs, send, land, out = (rest[k * n:(k + 1) * n] for k in range(5))
        w_v, w_land, w_out = (rest[5 * n + k * nw:5 * n + (k + 1) * nw] for k in range(3))
        load_sems, send_sems, recv_sems, store_sems = rest[5 * n + 3 * nw:]
        x, y, c = _place()
        sibling = (x, y, 1 - c)
        loads = [pltpu.make_async_copy(g_any[k].at[:, 1 - c], theirs[k], load_sems.at[k]) for k in range(n)]
        loads += [pltpu.make_async_copy(g_any[k].at[:, c], mine[k], load_sems.at[n + k]) for k in range(n)]
        loads += [pltpu.make_async_copy(w_any[k], w_v[k], load_sems.at[2 * n + k]) for k in range(nw)]
        for cp in loads:
            cp.start()
        sends = []
        for k in range(n):
            loads[k].wait()
            send[k][...] = theirs[k][...].astype(BF16)
            sends.append(pltpu.make_async_remote_copy(
                src_ref=send[k], dst_ref=land[k], send_sem=send_sems.at[k], recv_sem=recv_sems.at[k],
                device_id=sibling, device_id_type=MESH))
            sends[-1].start()
        for k in range(nw):
            loads[2 * n + k].wait()
            sends.append(pltpu.make_async_remote_copy(
                src_ref=w_v[k], dst_ref=w_land[k], send_sem=send_sems.at[n + k], recv_sem=recv_sems.at[n + k],
                device_id=sibling, device_id_type=MESH))
            sends[-1].start()
        stores = []
        for k in range(n):
            loads[n + k].wait()
            sends[k].wait_recv()
            out[k][...] = (mine[k][...] + land[k][...].astype(F32)).astype(BF16)
            stores.append(pltpu.make_async_copy(out[k], o_any[k], store_sems.at[k]))
            stores[-1].start()
        for k in range(nw):
            sends[n + k].wait_recv()
            w_out[k][...] = w_v[k][...] + w_land[k][...]
            stores.append(pltpu.make_async_copy(w_out[k], wo_any[k], store_sems.at[n + k]))
            stores[-1].start()
        for cp in sends:
            cp.wait_send()
        for cp in stores:
            cp.wait()

    dma = pltpu.SemaphoreType.DMA
    return pl.pallas_call(
        body, name="pair_reduce",
        out_shape=[jax.ShapeDtypeStruct(s, BF16) for s in half_shapes] + [jax.ShapeDtypeStruct(w.shape, F32) for w in wholes],
        in_specs=[ANY] * (n + nw), out_specs=[ANY] * (n + nw),
        scratch_shapes=[pltpu.VMEM(s, F32) for s in half_shapes] * 2 + [pltpu.VMEM(s, BF16) for s in half_shapes] * 3
        + [pltpu.VMEM(w.shape, F32) for w in wholes] * 3
        + [dma((2 * n + nw,)), dma((n + nw,)), dma((n + nw,)), dma((n + nw,))],
        compiler_params=pltpu.CompilerParams(vmem_limit_bytes=VMEM_LIMIT),
    )(*grads, *wholes)


def _token_tile(T, want):
    return math.gcd(T, want)


def kernel(x, p, w_in, pool_w, pool_scale, sgu_ln_g, sgu_ln_b, sgu_w, sgu_b, w_out, ln_g, ln_b, ple_w, ple_gate_w, ple_gate_b, loss_target, m_w_in, m_pool_w, m_pool_scale, m_sgu_ln_g, m_sgu_ln_b, m_sgu_w, m_sgu_b, m_w_out, m_ln_g, m_ln_b, m_ple_w, m_ple_gate_w, m_ple_gate_b, v_w_in, v_pool_w, v_pool_scale, v_sgu_ln_g, v_sgu_ln_b, v_sgu_w, v_sgu_b, v_w_out, v_ln_g, v_ln_b, v_ple_w, v_ple_gate_w, v_ple_gate_b):
    c = lax.axis_index("c")
    T, D = x.shape[1], x.shape[2]
    tm, tm_vpu, tm_acc = _token_tile(T, 512), _token_tile(T, 256), _token_tile(T, 2048)
    x2, p2, tgt = x[0], p[0, 0], loss_target[0]
    G, PGQ, PG = pool_w.shape[1], pool_w.shape[2], pool_w.shape[3]

    w_in_f, pool_f = _gather_weights([w_in[0], pool_w[0].reshape(G * PGQ, PG)])
    pool_f = pool_f.reshape(4, G, PGQ, PG).transpose(1, 0, 2, 3).reshape(G, 4 * PGQ, PG)
    tril = jnp.tril(jnp.ones((CHUNK, CHUNK), dtype=bool))
    sgu_wm = jnp.where(tril[None], sgu_w[0], 0.0).astype(BF16)
    sgu_bias_t = sgu_b[0].T

    kept, y, xt, w_out_f, w_gate_f, w_ple_f = _front_forward(
        x2, w_in_f, pool_f, pool_scale, sgu_ln_g, sgu_ln_b, sgu_wm, sgu_bias_t, [w_out[0], ple_gate_w[0], ple_w[0]],
        tm_vpu)
    w_out_f = w_out_f.reshape(-1, D)
    w_gate_f = w_gate_f.reshape(-1, D)
    (dx_part, d_y, d_w_out, d_w_gate, d_w_ple, d_ln_g, d_ln_b, d_gate_b, ssq) = _tail(
        y, x2, p2, tgt, w_out_f, w_gate_f, w_ple_f, ln_g, ln_b, ple_gate_b, tm)
    d_h, d_x, d_pool_w, d_pool_scale, d_sgu_g, d_sgu_b, d_sgu_w, d_sgu_bias_t = _front_backward(
        kept, d_y, dx_part, w_in_f, pool_f, pool_scale, sgu_ln_g, sgu_ln_b, sgu_wm, sgu_bias_t, tm_vpu)
    grads = [d_w_out.reshape(4, -1, D), d_w_gate.reshape(4, -1, D), d_w_ple,
             d_pool_w.reshape(G, 4, PGQ, PG).transpose(1, 0, 2, 3).reshape(4, G * PGQ, PG)]
    grads = [g.reshape(4, 2, g.shape[1] // 2, g.shape[2]) for g in grads]
    vectors = ["pool_scale", "sgu_ln_g", "sgu_ln_b", "ln_g", "ln_b", "ple_gate_b"]
    rows = [d_pool_scale, d_sgu_g, d_sgu_b, d_ln_g, d_ln_b, d_gate_b,
            jnp.pad(jnp.reshape((0.5 / D) * jnp.sum(ssq), (1, 1)), ((0, 1), (0, D - 1))),
            jnp.pad(d_sgu_bias_t.T, ((0, 4), (0, D - CHUNK)))]
    small = jnp.concatenate(rows, axis=0)
    d_sgu_w = d_sgu_w.reshape(-1, CHUNK)
    q = 2 * lax.axis_index("x") + lax.axis_index("y")
    *parts, small_chip, sw_chip = _pair_reduce(grads, [small, d_sgu_w])
    d_w_in, *slots, small_slots, sw_slots = _weight_backward(d_h, xt, q, parts + [small_chip, sw_chip], tm_acc)
    *halves, small_total, sw_total = _sum_fours(parts, slots, [small_chip, sw_chip], [small_slots, sw_slots], q)
    sibling_halves = _join_halves_with_sibling(halves)
    loss = small_total[6, 0]

    big_names = ["w_out", "ple_gate_w", "ple_w", "pool_w"]
    given = dict(w_in=(w_in, m_w_in, v_w_in), w_out=(w_out, m_w_out, v_w_out),
                 ple_gate_w=(ple_gate_w, m_ple_gate_w, v_ple_gate_w), ple_w=(ple_w, m_ple_w, v_ple_w),
                 pool_w=(pool_w, m_pool_w, v_pool_w), pool_scale=(pool_scale, m_pool_scale, v_pool_scale),
                 sgu_ln_g=(sgu_ln_g, m_sgu_ln_g, v_sgu_ln_g), sgu_ln_b=(sgu_ln_b, m_sgu_ln_b, v_sgu_ln_b),
                 sgu_w=(sgu_w, m_sgu_w, v_sgu_w), sgu_b=(sgu_b, m_sgu_b, v_sgu_b), ln_g=(ln_g, m_ln_g, v_ln_g),
                 ln_b=(ln_b, m_ln_b, v_ln_b), ple_gate_b=(ple_gate_b, m_ple_gate_b, v_ple_gate_b))
    grad, delta, new_m, new_v = {}, {}, {}, {}
    grad["w_in"], delta["w_in"], new_m["w_in"], new_v["w_in"] = (
        t[None] for t in _adamw(w_in[0], d_w_in, m_w_in[0], v_w_in[0], "adamw_w_in"))
    flat = [(2 * g.shape[0], g.shape[1]) for g in halves]
    small_items = [(*given[n], (0, slice(k, k + 1), slice(None)), False) for k, n in enumerate(vectors)]
    small_items.append((*(t[0] for t in given["sgu_b"]), (0, slice(8, 8 + sgu_b.shape[1]), slice(0, CHUNK)), False))
    small_items.append((*(t.reshape(-1, CHUNK) for t in given["sgu_w"]), (1, slice(None), slice(None)), True))
    outs = _adamw_joined(
        [given[n][0].reshape(f) for n, f in zip(big_names, flat)], halves, sibling_halves,
        [given[n][1].reshape(f) for n, f in zip(big_names, flat)],
        [given[n][2].reshape(f) for n, f in zip(big_names, flat)], small_items, [small_total, sw_total], c)
    for name, four in zip(big_names + vectors + ["sgu_b", "sgu_w"], outs):
        grad[name], delta[name], new_m[name], new_v[name] = (t.reshape(given[name][0].shape) for t in four)

    order = ["w_in", "pool_w", "pool_scale", "sgu_ln_g", "sgu_ln_b", "sgu_w", "sgu_b", "w_out", "ln_g", "ln_b",
             "ple_w", "ple_gate_w", "ple_gate_b"]
    return (loss, d_x[None], *[grad[n] for n in order], *[delta[n] for n in order],
            *[new_m[n] for n in order], *[new_v[n] for n in order])
```
